```python
import jax
import jax.numpy as jnp
from jax import lax
import numpy as np

D_MODEL = 1024
BATCH = 8
SEQ = 2048
DEPTH = 2

N_MIXERS = 2
GRID_W = 64
N_MEM = 256
MEM_HEADS = 4
MEM_DH = 64
MEM_W = MEM_HEADS * MEM_DH
NA_HEADS = 12
NA_DH = 64
NA_W = NA_HEADS * NA_DH
WIN_H_MAX = 8
WIN_W = 16
ML_HEADS = 4
ML_DH = 192
ML_W = ML_HEADS * ML_DH
CONV_K = 5
CHUNK = 128
MIX_W = NA_W + MEM_W
N_EXPERTS = 16
N_GROUPS = 4
EXPERTS_PER_GROUP = N_EXPERTS // N_GROUPS
TOP_K = 2
D_EXPERT = 512
MOE_BLOCK = 128
ALPHA = (2 * DEPTH) ** 0.25
BETA = (8 * DEPTH) ** -0.25
LN_EPS = 1e-5

kernel_name = 'hybrid_natten_mlstm_moe_encoder'


def layer_norm(x, g, b):
    xf = x.astype(jnp.float32)
    mu = jnp.mean(xf, axis=-1, keepdims=True)
    var = jnp.mean(jnp.square(xf - mu), axis=-1, keepdims=True)
    return ((xf - mu) * lax.rsqrt(var + LN_EPS) * g + b).astype(x.dtype)


def memory_attention(q, mem_k, mem_v):
    B, S, _ = q.shape
    q = q.reshape(B, S, MEM_HEADS, MEM_DH) * (MEM_DH ** -0.5)
    s = jnp.einsum('bshd,bmhd->bhsm', q, mem_k, preferred_element_type=jnp.float32)
    p = jax.nn.softmax(s, axis=-1).astype(mem_v.dtype)
    return jnp.einsum('bhsm,bmhd->bshd', p, mem_v).reshape(B, S, MEM_W)


def neighbourhood_attention(q, k, v, rpb):
    B, R, W, H, Dh = q.shape
    win_h = min(WIN_H_MAX, R)
    n_cb = W // WIN_W
    span = 2 * WIN_W
    q_cols = np.arange(W).reshape(n_cb, WIN_W)
    cb_start = np.clip(np.arange(n_cb) * WIN_W - WIN_W // 2, 0, W - span)
    key_cols = cb_start[:, None] + np.arange(span)[None, :]
    col_start = np.clip(q_cols - WIN_W // 2, 0, W - WIN_W)
    kc = key_cols[:, None, :]
    col_in = (kc >= col_start[..., None]) & (kc < col_start[..., None] + WIN_W)
    dc_idx = np.clip(kc - q_cols[..., None] + WIN_W - 1, 0, 2 * WIN_W - 2)
    rpb_c = jnp.transpose(rpb[:, :, dc_idx], (0, 2, 3, 1, 4))

    def row_block(r):
        rs = jnp.clip(r - win_h // 2, 0, R - win_h)
        k_rows = lax.dynamic_slice_in_dim(k, rs, win_h, axis=1)
        v_rows = lax.dynamic_slice_in_dim(v, rs, win_h, axis=1)
        k_blk = k_rows[:, :, key_cols]
        v_blk = v_rows[:, :, key_cols]
        q_blk = lax.dynamic_index_in_dim(q, r, axis=1, keepdims=False).reshape(B, n_cb, WIN_W, H, Dh)
        s = jnp.einsum('bjqhd,brjkhd->bhjqrk', q_blk, k_blk, preferred_element_type=jnp.float32)
        dr = rs + jnp.arange(win_h) - r + WIN_H_MAX - 1
        s = s + jnp.take(rpb_c, dr, axis=3)[None].astype(jnp.float32)
        s = jnp.where(col_in[:, :, None, :], s, -jnp.inf)
        p = jax.nn.softmax(s.reshape(B, H, n_cb, WIN_W, win_h * span), axis=-1)
        p = p.reshape(s.shape).astype(v.dtype)
        o = jnp.einsum('bhjqrk,brjkhd->bjqhd', p, v_blk)
        return o.reshape(B, W, H, Dh)

    out = lax.map(row_block, jnp.arange(R))
    return jnp.moveaxis(out, 0, 1)


def na_mixer(x, w_in, rpb, mem_k, mem_v):
    B, S, _ = x.shape
    rows = S // GRID_W
    h = x @ w_in
    q, k, v, q_mem = jnp.split(h, [NA_W, 2 * NA_W, 3 * NA_W], axis=-1)
    grid = (B, rows, GRID_W, NA_HEADS, NA_DH)
    y = neighbourhood_attention(q.reshape(grid) * (NA_DH ** -0.5), k.reshape(grid), v.reshape(grid), rpb)
    return jnp.concatenate([y.reshape(B, S, NA_W), memory_attention(q_mem, mem_k, mem_v)], axis=-1)


def mlstm_chunkwise(q, k, v, i_pre, f_pre):
    q, k, v = q.astype(jnp.float32), k.astype(jnp.float32), v.astype(jnp.float32)
    B, H, S, Dk = q.shape
    Dv = v.shape[-1]
    nc = S // CHUNK
    log_f = jax.nn.log_sigmoid(f_pre.astype(jnp.float32))
    i_pre = i_pre.astype(jnp.float32)

    def to_chunks(a):
        return jnp.moveaxis(a.reshape(a.shape[:2] + (nc, CHUNK) + a.shape[3:]), 2, 0)

    causal = np.tril(np.ones((CHUNK, CHUNK), dtype=bool))

    def step(carry, inp):
        C, n, m = carry
        qb, kb, vb, ib, fb = inp
        b = jnp.cumsum(fb, axis=-1)
        d = jnp.where(causal, b[..., :, None] - b[..., None, :] + ib[..., None, :], -jnp.inf)
        inter = b + m[..., None]
        m_t = jnp.maximum(jnp.max(d, axis=-1), inter)
        dexp = jnp.exp(d - m_t[..., None])
        iexp = jnp.exp(inter - m_t)
        s = jnp.einsum('bhtd,bhsd->bhts', qb, kb) * dexp
        num = jnp.einsum('bhts,bhse->bhte', s, vb) + iexp[..., None] * jnp.einsum('bhtd,bhde->bhte', qb, C)
        den = jnp.sum(s, axis=-1) + iexp * jnp.einsum('bhtd,bhd->bht', qb, n)
        h = num / jnp.maximum(jnp.abs(den), jnp.exp(-m_t))[..., None]
        b_last = b[..., -1]
        w_s = b_last[..., None] - b + ib
        m_new = jnp.maximum(b_last + m, jnp.max(w_s, axis=-1))
        wexp = jnp.exp(w_s - m_new[..., None])
        cexp = jnp.exp(b_last + m - m_new)
        C_new = cexp[..., None, None] * C + jnp.einsum('bhs,bhsd,bhse->bhde', wexp, kb, vb)
        n_new = cexp[..., None] * n + jnp.einsum('bhs,bhsd->bhd', wexp, kb)
        return (C_new, n_new, m_new), h

    init = (jnp.zeros((B, H, Dk, Dv), jnp.float32), jnp.zeros((B, H, Dk), jnp.float32),
            jnp.zeros((B, H), jnp.float32))
    _, hs = lax.scan(step, init, (to_chunks(q), to_chunks(k), to_chunks(v), to_chunks(i_pre), to_chunks(log_f)))
    return jnp.moveaxis(hs, 0, 2).reshape(B, H, S, Dv)


def mlstm_mixer(x, w_in, conv_w, conv_b, w_qkv, gate_b, norm_g, skip, mem_k, mem_v):
    B, S, _ = x.shape
    h = x @ w_in
    xm, z, gates, q_mem = jnp.split(h, [ML_W, 2 * ML_W, 2 * ML_W + 4 * ML_HEADS], axis=-1)
    xc = lax.conv_general_dilated(xm, conv_w[:, None, :], window_strides=(1,), padding='SAME',
                                  dimension_numbers=('NWC', 'WIO', 'NWC'), feature_group_count=ML_W)
    xc = jax.nn.silu(xc + conv_b)
    xc_h = xc.reshape(B, S, ML_HEADS, ML_DH)
    xm_h = xm.reshape(B, S, ML_HEADS, ML_DH)
    q = jnp.einsum('bshd,hde->bhse', xc_h, w_qkv[0])
    k = jnp.einsum('bshd,hde->bhse', xc_h, w_qkv[1]) * (ML_DH ** -0.5)
    v = jnp.einsum('bshd,hde->bhse', xm_h, w_qkv[2])
    g = (gates.reshape(B, S, 4, ML_HEADS) + gate_b).astype(jnp.float32)
    g = jnp.transpose(g, (2, 0, 3, 1))
    h_f = mlstm_chunkwise(q, k, v, g[0], g[1])
    h_b = jnp.flip(mlstm_chunkwise(jnp.flip(q, 2), jnp.flip(k, 2), jnp.flip(v, 2),
                                   jnp.flip(g[2], -1), jnp.flip(g[3], -1)), 2)
    hs = h_f + h_b
    mu = jnp.mean(hs, axis=-1, keepdims=True)
    var = jnp.mean(jnp.square(hs - mu), axis=-1, keepdims=True)
    hs = (hs - mu) * lax.rsqrt(var + LN_EPS)
    hs = jnp.transpose(hs, (0, 2, 1, 3)).reshape(B, S, ML_W) * norm_g
    y = ((hs + skip * xc) * jax.nn.silu(z)).astype(x.dtype)
    return jnp.concatenate([y, memory_attention(q_mem, mem_k, mem_v)], axis=-1)


def moe(x, router_w, router_b, w_gate, w_up, w_down):
    B, S, D = x.shape
    N = B * S
    xf = x.reshape(N, D)
    scores = jax.nn.sigmoid(jnp.matmul(xf, router_w, preferred_element_type=jnp.float32))
    biased = scores + router_b.astype(jnp.float32)
    grp_score = jnp.sum(lax.top_k(biased.reshape(N, N_GROUPS, EXPERTS_PER_GROUP), 2)[0], axis=-1)
    g_sel = jnp.argmax(grp_score, axis=-1)
    in_group = (jnp.arange(N_EXPERTS) // EXPERTS_PER_GROUP)[None, :] == g_sel[:, None]
    _, idx = lax.top_k(jnp.where(in_group, biased, -jnp.inf), TOP_K)
    w = jnp.take_along_axis(scores, idx, axis=1)
    w = w / jnp.sum(w, axis=-1, keepdims=True)
    NK = N * TOP_K
    flat_e = idx.reshape(NK)
    order = jnp.argsort(flat_e)
    e_sorted = flat_e[order]
    tok_sorted = order // TOP_K
    counts = jnp.bincount(flat_e, length=N_EXPERTS)
    padded = (counts + MOE_BLOCK - 1) // MOE_BLOCK * MOE_BLOCK
    pad_end = jnp.cumsum(padded)
    pad_start = pad_end - padded
    start = jnp.cumsum(counts) - counts
    dest = pad_start[e_sorted] + jnp.arange(NK) - start[e_sorted]
    n_blocks = -(-NK // MOE_BLOCK) + N_EXPERTS
    buf = jnp.zeros((n_blocks * MOE_BLOCK, D), x.dtype).at[dest].set(xf[tok_sorted])
    block_e = jnp.minimum(jnp.searchsorted(pad_end, jnp.arange(n_blocks) * MOE_BLOCK, side='right'),
                          N_EXPERTS - 1)

    def expert_block(args):
        xb, e = args
        hb = jax.nn.silu(xb @ w_gate[e]) * (xb @ w_up[e])
        return hb @ w_down[e]

    y = lax.map(expert_block, (buf.reshape(n_blocks, MOE_BLOCK, D), block_e)).reshape(n_blocks * MOE_BLOCK, D)
    w_sorted = w.reshape(NK)[order].astype(x.dtype)
    out = jnp.zeros((N, D), x.dtype).at[tok_sorted].add(y[dest] * w_sorted[:, None])
    return out.reshape(B, S, D)


def setup_inputs(seed: int = 0) -> dict:
    key = jax.random.key(seed)
    ks = jax.random.split(key, 26)
    n_a = (DEPTH + 1) // 2
    n_b = DEPTH // 2

    def nrm(k, shape, scale):
        return jax.random.normal(k, shape, jnp.float32) * scale

    x = nrm(ks[0], (BATCH, SEQ, D_MODEL), 1.0)
    mem = nrm(ks[1], (BATCH, N_MEM, D_MODEL), 1.0)
    mem_ln_g = 1.0 + nrm(ks[2], (D_MODEL,), 0.05)
    mem_ln_b = nrm(ks[3], (D_MODEL,), 0.02)
    w_mem_kv = nrm(ks[4], (D_MODEL, 2 * MEM_W), D_MODEL ** -0.5).at[:, MEM_W:].multiply(BETA)
    router_w = nrm(ks[5], (D_MODEL, N_EXPERTS), D_MODEL ** -0.5)
    router_b = nrm(ks[6], (N_EXPERTS,), 0.01)
    na_w_in = nrm(ks[7], (n_a, D_MODEL, 3 * NA_W + MEM_W), D_MODEL ** -0.5)
    na_w_in = na_w_in.at[:, :, 2 * NA_W:3 * NA_W].multiply(BETA)
    na_rpb = nrm(ks[8], (n_a, NA_HEADS, 2 * WIN_H_MAX - 1, 2 * WIN_W - 1), 0.1)
    ml_w_in = nrm(ks[9], (n_b, D_MODEL, 2 * ML_W + 4 * ML_HEADS + MEM_W), D_MODEL ** -0.5)
    ml_conv_w = nrm(ks[10], (n_b, CONV_K, ML_W), CONV_K ** -0.5)
    ml_conv_b = nrm(ks[11], (n_b, ML_W), 0.02)
    ml_w_qkv = nrm(ks[12], (n_b, 3, ML_HEADS, ML_DH, ML_DH), ML_DH ** -0.5).at[:, 2].multiply(BETA)
    i_bias = nrm(ks[13], (n_b, 2, ML_HEADS), 0.1)
    f_bias = jnp.linspace(3.0, 6.0, ML_HEADS)[None, None, :] + nrm(ks[14], (n_b, 2, ML_HEADS), 0.1)
    ml_gate_b = jnp.stack([i_bias[:, 0], f_bias[:, 0], i_bias[:, 1], f_bias[:, 1]], axis=1)
    ml_norm_g = 1.0 + nrm(ks[15], (n_b, ML_W), 0.05)
    ml_skip = 1.0 + nrm(ks[16], (n_b, ML_W), 0.05)
    w_out = nrm(ks[17], (DEPTH, MIX_W, D_MODEL), MIX_W ** -0.5 * BETA)
    ln_g = 1.0 + nrm(ks[18], (DEPTH, 2, D_MODEL), 0.05)
    ln_b = nrm(ks[19], (DEPTH, 2, D_MODEL), 0.02)
    exp_w_gate = nrm(ks[20], (DEPTH, N_EXPERTS, D_MODEL, D_EXPERT), D_MODEL ** -0.5)
    exp_w_up = nrm(ks[21], (DEPTH, N_EXPERTS, D_MODEL, D_EXPERT), D_MODEL ** -0.5)
    exp_w_down = nrm(ks[22], (DEPTH, N_EXPERTS, D_EXPERT, D_MODEL), D_EXPERT ** -0.5 * BETA)
    return {'x': x, 'mem': mem, 'mem_ln_g': mem_ln_g, 'mem_ln_b': mem_ln_b, 'w_mem_kv': w_mem_kv,
            'router_w': router_w, 'router_b': router_b, 'na_w_in': na_w_in, 'na_rpb': na_rpb,
            'ml_w_in': ml_w_in, 'ml_conv_w': ml_conv_w, 'ml_conv_b': ml_conv_b, 'ml_w_qkv': ml_w_qkv,
            'ml_gate_b': ml_gate_b, 'ml_norm_g': ml_norm_g, 'ml_skip': ml_skip, 'w_out': w_out,
            'ln_g': ln_g, 'ln_b': ln_b, 'exp_w_gate': exp_w_gate, 'exp_w_up': exp_w_up,
            'exp_w_down': exp_w_down}


def reference(x, mem, mem_ln_g, mem_ln_b, w_mem_kv, router_w, router_b, na_w_in, na_rpb,
              ml_w_in, ml_conv_w, ml_conv_b, ml_w_qkv, ml_gate_b, ml_norm_g, ml_skip, w_out,
              ln_g, ln_b, exp_w_gate, exp_w_up, exp_w_down):
    B, M, _ = mem.shape
    mem_k, mem_v = jnp.split(layer_norm(mem, mem_ln_g, mem_ln_b) @ w_mem_kv, 2, axis=-1)
    mem_k = mem_k.reshape(B, M, MEM_HEADS, MEM_DH)
    mem_v = mem_v.reshape(B, M, MEM_HEADS, MEM_DH)
    for i in range(DEPTH):
        j = i // N_MIXERS
        if i % N_MIXERS == 0:
            mixed = na_mixer(x, na_w_in[j], na_rpb[j], mem_k, mem_v)
        else:
            mixed = mlstm_mixer(x, ml_w_in[j], ml_conv_w[j], ml_conv_b[j], ml_w_qkv[j], ml_gate_b[j],
                                ml_norm_g[j], ml_skip[j], mem_k, mem_v)
        x = layer_norm(ALPHA * x + mixed @ w_out[i], ln_g[i, 0], ln_b[i, 0])
        x = layer_norm(ALPHA * x + moe(x, router_w, router_b, exp_w_gate[i], exp_w_up[i], exp_w_down[i]),
                       ln_g[i, 1], ln_b[i, 1])
    return x
```

```python
import functools

import numpy as np
import jax
import jax.numpy as jnp
from jax import lax
from jax.experimental import pallas as pl
from jax.experimental.pallas import tpu as pltpu

F32 = jnp.float32
BF16 = jnp.bfloat16
I32 = jnp.int32

D_MODEL = 1024
DEPTH = 2
GRID_W = 64
MEM_HEADS = 4
MEM_DH = 64
MEM_W = MEM_HEADS * MEM_DH
NA_HEADS = 12
NA_DH = 64
NA_W = NA_HEADS * NA_DH
WIN_H = 8
WIN_W = 16
ML_HEADS = 4
ML_DH = 192
ML_DHP = 256
ML_W = ML_HEADS * ML_DH
ML_WP = ML_HEADS * ML_DHP
CONV_K = 5
CHUNK = 128
N_EXPERTS = 16
N_GROUPS = 4
EXPERTS_PER_GROUP = N_EXPERTS // N_GROUPS
D_EXPERT = 512
ALPHA = (2 * DEPTH) ** 0.25
LN_EPS = 1e-5
NEG = -1e30

LANES = 128
SUBLANES = 8
ROW_CHUNKS = D_MODEL // LANES
MOE_BM = 256
VMEM_LIMIT = 48 * 1024 * 1024


def _cparams(*sem):
    return pltpu.CompilerParams(dimension_semantics=sem, vmem_limit_bytes=VMEM_LIMIT)


def _dot(a, b):
    return jnp.dot(a, b, preferred_element_type=F32)


def _dot_nt(a, b, precision=None):
    return lax.dot_general(a, b, (((1,), (1,)), ((), ())), precision=precision,
                           preferred_element_type=F32)


def _ln(z, g, b):
    mu = jnp.mean(z, axis=-1, keepdims=True)
    zc = z - mu
    var = jnp.mean(zc * zc, axis=-1, keepdims=True)
    return zc * lax.rsqrt(var + LN_EPS) * g + b


def _silu(x):
    return x * jax.nn.sigmoid(x)


def _read_rows(ref, n):
    return jnp.concatenate([ref[pl.ds(j, n, stride=ROW_CHUNKS), :] for j in range(ROW_CHUNKS)], axis=1)


def _write_rows(ref, val, n):
    for j in range(ROW_CHUNKS):
        ref[pl.ds(j, n, stride=ROW_CHUNKS), :] = val[:, j * LANES:(j + 1) * LANES]


def _memkv_kernel(m_ref, g_ref, b_ref, w_ref, k_ref, v_ref):
    z = _ln(m_ref[...], g_ref[...], b_ref[...])
    kv = _dot(z.astype(BF16), w_ref[...])
    k_ref[...] = kv[:, :MEM_W].astype(BF16)
    v_ref[...] = kv[:, MEM_W:].astype(BF16)


def _memkv(mem2, g, b, w):
    n = mem2.shape[0]
    tm = 256
    return pl.pallas_call(
        _memkv_kernel,
        grid=(n // tm,),
        in_specs=[pl.BlockSpec((tm, D_MODEL), lambda i: (i, 0)),
                  pl.BlockSpec((1, D_MODEL), lambda i: (0, 0)),
                  pl.BlockSpec((1, D_MODEL), lambda i: (0, 0)),
                  pl.BlockSpec((D_MODEL, 2 * MEM_W), lambda i: (0, 0))],
        out_specs=[pl.BlockSpec((tm, MEM_W), lambda i: (i, 0)),
                   pl.BlockSpec((tm, MEM_W), lambda i: (i, 0))],
        out_shape=[jax.ShapeDtypeStruct((n, MEM_W), BF16)] * 2,
        compiler_params=_cparams("parallel"),
        name="memkv",
    )(mem2, g, b, w)


def _proj_kernel(x_ref, w_ref, o_ref):
    o_ref[...] = _dot(x_ref[...].astype(BF16), w_ref[...]).astype(o_ref.dtype)


def _proj(x2, w, tm=512):
    n, k = x2.shape
    nout = w.shape[1]
    return pl.pallas_call(
        _proj_kernel,
        grid=(n // tm,),
        in_specs=[pl.BlockSpec((tm, k), lambda i: (i, 0)),
                  pl.BlockSpec((k, nout), lambda i: (0, 0))],
        out_specs=pl.BlockSpec((tm, nout), lambda i: (i, 0)),
        out_shape=jax.ShapeDtypeStruct((n, nout), BF16),
        compiler_params=_cparams("parallel"),
        name="in_proj",
    )(x2, w)


def _proj_gates_kernel(x_ref, w_ref, wg_ref, wgt_ref, o_ref, g_ref, gt_ref):
    xb = x_ref[...].astype(BF16)
    o_ref[...] = _dot(xb, w_ref[...]).astype(BF16)
    g_ref[...] = _dot(xb, wg_ref[...])
    gt_ref[...] = _dot_nt(wgt_ref[...], xb)


def _proj_gates(x2, w, wg, wgt, tm=512):
    n, k = x2.shape
    nout = w.shape[1]
    return pl.pallas_call(
        _proj_gates_kernel,
        grid=(n // tm,),
        in_specs=[pl.BlockSpec((tm, k), lambda i: (i, 0)),
                  pl.BlockSpec((k, nout), lambda i: (0, 0)),
                  pl.BlockSpec((k, LANES), lambda i: (0, 0)),
                  pl.BlockSpec((LANES, k), lambda i: (0, 0))],
        out_specs=[pl.BlockSpec((tm, nout), lambda i: (i, 0)),
                   pl.BlockSpec((tm, LANES), lambda i: (i, 0)),
                   pl.BlockSpec((LANES, tm), lambda i: (0, i))],
        out_shape=[jax.ShapeDtypeStruct((n, nout), BF16),
                   jax.ShapeDtypeStruct((n, LANES), F32),
                   jax.ShapeDtypeStruct((LANES, n), F32)],
        compiler_params=_cparams("parallel"),
        name="in_proj_gates",
    )(x2, w, wg, wgt)


def _softmax_pv(s, v):
    m = jnp.max(s, axis=-1, keepdims=True)
    p = jnp.exp(s - m)
    l = jnp.sum(p, axis=-1, keepdims=True)
    return _dot(p.astype(BF16), v) / l


def _na_kernel(q_ref, k_ref, v_ref, tbl_ref, o_ref, *, rows):
    lane = lax.broadcasted_iota(I32, (1, LANES), 1)
    first = lane < NA_DH
    nkeys = WIN_H * GRID_W

    def row(r, carry):
        rs = jnp.clip(r - WIN_H // 2, 0, rows - WIN_H)
        q = q_ref[0, pl.ds(pl.multiple_of(r * GRID_W, GRID_W), GRID_W), :]
        k = k_ref[0, pl.ds(pl.multiple_of(rs * GRID_W, GRID_W), nkeys), :]
        v = v_ref[0, pl.ds(pl.multiple_of(rs * GRID_W, GRID_W), nkeys), :]
        q = q * jnp.asarray(NA_DH ** -0.5, BF16)
        dr0 = rs - r + WIN_H - 1
        outs = []
        for half in range(2):
            qh = jnp.where(first if half == 0 else jnp.logical_not(first), q, jnp.zeros_like(q))
            s = _dot_nt(qh, k)
            bias = jnp.concatenate([tbl_ref[0, half, dr0 + 2 * m] for m in range(WIN_H // 2)], axis=1)
            outs.append(_softmax_pv(s + bias, v))
        o = jnp.where(first, outs[0], outs[1])
        o_ref[0, pl.ds(pl.multiple_of(r * GRID_W, GRID_W), GRID_W), :] = o.astype(o_ref.dtype)
        return carry

    lax.fori_loop(0, rows, row, 0)


def _na_bias_table(rpb):
    qc = np.arange(GRID_W)[:, None]
    kc = np.arange(GRID_W)[None, :]
    cs = np.clip(qc - WIN_W // 2, 0, GRID_W - WIN_W)
    col_in = (kc >= cs) & (kc < cs + WIN_W)
    dc = np.clip(kc - qc + WIN_W - 1, 0, 2 * WIN_W - 2)
    t = jnp.where(col_in, rpb[:, :, dc], NEG).astype(F32)
    t2 = jnp.concatenate([t[:, :-1], t[:, 1:]], axis=-1)
    return t2.reshape(NA_HEADS // 2, 2, 2 * WIN_H - 2, GRID_W, 2 * GRID_W)


def _na_attention(h3, tbl):
    b, s, _ = h3.shape
    rows = s // GRID_W
    npair = NA_HEADS // 2
    return pl.pallas_call(
        functools.partial(_na_kernel, rows=rows),
        grid=(b, npair),
        in_specs=[pl.BlockSpec((1, s, LANES), lambda i, p: (i, 0, p)),
                  pl.BlockSpec((1, s, LANES), lambda i, p: (i, 0, npair + p)),
                  pl.BlockSpec((1, s, LANES), lambda i, p: (i, 0, 2 * npair + p)),
                  pl.BlockSpec((1, 2, 2 * WIN_H - 2, GRID_W, 2 * GRID_W), lambda i, p: (p, 0, 0, 0, 0))],
        out_specs=pl.BlockSpec((1, s, LANES), lambda i, p: (i, 0, p)),
        out_shape=jax.ShapeDtypeStruct((b, s, NA_W), BF16),
        compiler_params=_cparams("parallel", "parallel"),
        name="na_attention",
    )(h3, h3, h3, tbl)


def _mem_attn_kernel(q_ref, k_ref, v_ref, o_ref):
    lane = lax.broadcasted_iota(I32, (1, LANES), 1)
    first = lane < MEM_DH
    q = q_ref[0] * jnp.asarray(MEM_DH ** -0.5, BF16)
    k = k_ref[0]
    v = v_ref[0]
    outs = []
    for half in range(2):
        qh = jnp.where(first if half == 0 else jnp.logical_not(first), q, jnp.zeros_like(q))
        outs.append(_softmax_pv(_dot_nt(qh, k), v))
    o_ref[0] = jnp.where(first, outs[0], outs[1]).astype(o_ref.dtype)


def _mem_attention(h3, col_block0, mem_k3, mem_v3, tq=512):
    b, s, _ = h3.shape
    nm = mem_k3.shape[1]
    npair = MEM_HEADS // 2
    return pl.pallas_call(
        _mem_attn_kernel,
        grid=(b, npair, s // tq),
        in_specs=[pl.BlockSpec((1, tq, LANES), lambda i, p, t: (i, t, col_block0 + p)),
                  pl.BlockSpec((1, nm, LANES), lambda i, p, t: (i, 0, p)),
                  pl.BlockSpec((1, nm, LANES), lambda i, p, t: (i, 0, p))],
        out_specs=pl.BlockSpec((1, tq, LANES), lambda i, p, t: (i, t, p)),
        out_shape=jax.ShapeDtypeStruct((b, s, MEM_W), BF16),
        compiler_params=_cparams("parallel", "parallel", "parallel"),
        name="mem_attention",
    )(h3, mem_k3, mem_v3)


def _outproj_ln_kernel(ya_ref, ym_ref, wa_ref, wm_ref, x_ref, g_ref, b_ref, o_ref, or_ref, *, tm):
    acc = _dot(ya_ref[...], wa_ref[...]) + _dot(ym_ref[...], wm_ref[...])
    out = _ln(ALPHA * x_ref[...] + acc, g_ref[...], b_ref[...])
    o_ref[...] = out
    _write_rows(or_ref, out, tm)


def _outproj_ln(ya, ym, wa, wm, x2, g, b, tm=512):
    n = x2.shape[0]
    ka, km = ya.shape[1], ym.shape[1]
    return pl.pallas_call(
        functools.partial(_outproj_ln_kernel, tm=tm),
        grid=(n // tm,),
        in_specs=[pl.BlockSpec((tm, ka), lambda i: (i, 0)),
                  pl.BlockSpec((tm, km), lambda i: (i, 0)),
                  pl.BlockSpec((ka, D_MODEL), lambda i: (0, 0)),
                  pl.BlockSpec((km, D_MODEL), lambda i: (0, 0)),
                  pl.BlockSpec((tm, D_MODEL), lambda i: (i, 0)),
                  pl.BlockSpec((1, D_MODEL), lambda i: (0, 0)),
                  pl.BlockSpec((1, D_MODEL), lambda i: (0, 0))],
        out_specs=[pl.BlockSpec((tm, D_MODEL), lambda i: (i, 0)),
                   pl.BlockSpec((tm * ROW_CHUNKS, LANES), lambda i: (i, 0))],
        out_shape=[jax.ShapeDtypeStruct((n, D_MODEL), F32),
                   jax.ShapeDtypeStruct((n * ROW_CHUNKS, LANES), F32)],
        compiler_params=_cparams("parallel"),
        name="outproj_ln",
    )(ya, ym, wa, wm, x2, g, b)


def _router_kernel(x_ref, rwt_ref, rb_ref, idx_ref, rank_ref, w_ref, cnt_ref, *, tm):
    @pl.when(pl.program_id(0) == 0)
    def _():
        cnt_ref[...] = jnp.zeros_like(cnt_ref)

    logits = _dot_nt(rwt_ref[...], x_ref[...], precision=lax.Precision.HIGHEST)
    scores = jax.nn.sigmoid(logits)
    biased = scores + rb_ref[...]
    bv = [biased[e:e + 1, :] for e in range(N_EXPERTS)]
    sv = [scores[e:e + 1, :] for e in range(N_EXPERTS)]

    grp = []
    for g in range(N_GROUPS):
        m = bv[g * EXPERTS_PER_GROUP:(g + 1) * EXPERTS_PER_GROUP]
        best = None
        for a in range(EXPERTS_PER_GROUP):
            for c in range(a + 1, EXPERTS_PER_GROUP):
                pair = m[a] + m[c]
                best = pair if best is None else jnp.maximum(best, pair)
        grp.append(best)
    gsel = jnp.zeros((1, tm), I32)
    gbest = grp[0]
    for g in range(1, N_GROUPS):
        better = grp[g] > gbest
        gsel = jnp.where(better, g, gsel)
        gbest = jnp.where(better, grp[g], gbest)

    def pick(vals, j):
        out = vals[j]
        for g in range(1, N_GROUPS):
            out = jnp.where(gsel == g, vals[g * EXPERTS_PER_GROUP + j], out)
        return out

    cb = [pick(bv, j) for j in range(EXPERTS_PER_GROUP)]
    cs = [pick(sv, j) for j in range(EXPERTS_PER_GROUP)]
    i1 = jnp.zeros((1, tm), I32)
    m1 = cb[0]
    s1 = cs[0]
    for j in range(1, EXPERTS_PER_GROUP):
        gt = cb[j] > m1
        i1 = jnp.where(gt, j, i1)
        m1 = jnp.where(gt, cb[j], m1)
        s1 = jnp.where(gt, cs[j], s1)
    i2 = jnp.zeros((1, tm), I32)
    m2 = jnp.full((1, tm), -jnp.inf, F32)
    s2 = jnp.zeros((1, tm), F32)
    for j in range(EXPERTS_PER_GROUP):
        ok = jnp.logical_and(i1 != j, cb[j] > m2)
        i2 = jnp.where(ok, j, i2)
        m2 = jnp.where(ok, cb[j], m2)
        s2 = jnp.where(ok, cs[j], s2)
    e1 = gsel * EXPERTS_PER_GROUP + i1
    e2 = gsel * EXPERTS_PER_GROUP + i2
    tot = s1 + s2
    idx_ref[...] = jnp.concatenate([e1, e2], axis=0)
    w_ref[...] = jnp.concatenate([s1 / tot, s2 / tot], axis=0)

    eio = lax.broadcasted_iota(I32, (N_EXPERTS, tm), 0)
    oh1 = eio == e1
    oh2 = eio == e2
    ohs = jnp.logical_or(oh1, oh2).astype(F32)
    before = (lax.broadcasted_iota(I32, (tm, tm), 0) < lax.broadcasted_iota(I32, (tm, tm), 1))
    pre = _dot(ohs.astype(BF16), before.astype(BF16))
    pos = cnt_ref[:, 0:1] + pre
    r1 = jnp.sum(jnp.where(oh1, pos, 0.0), axis=0, keepdims=True)
    r2 = jnp.sum(jnp.where(oh2, pos, 0.0), axis=0, keepdims=True)
    rank_ref[...] = jnp.concatenate([r1, r2], axis=0).astype(I32)
    cnt_ref[...] += jnp.sum(ohs, axis=1, keepdims=True)


def _router(x2, rwt, rb, tm=512):
    n = x2.shape[0]
    return pl.pallas_call(
        functools.partial(_router_kernel, tm=tm),
        grid=(n // tm,),
        in_specs=[pl.BlockSpec((tm, D_MODEL), lambda i: (i, 0)),
                  pl.BlockSpec((N_EXPERTS, D_MODEL), lambda i: (0, 0)),
                  pl.BlockSpec((N_EXPERTS, 1), lambda i: (0, 0))],
        out_specs=[pl.BlockSpec((2, tm), lambda i: (0, i)),
                   pl.BlockSpec((2, tm), lambda i: (0, i)),
                   pl.BlockSpec((2, tm), lambda i: (0, i)),
                   pl.BlockSpec((N_EXPERTS, LANES), lambda i: (0, 0))],
        out_shape=[jax.ShapeDtypeStruct((2, n), I32),
                   jax.ShapeDtypeStruct((2, n), I32),
                   jax.ShapeDtypeStruct((2, n), F32),
                   jax.ShapeDtypeStruct((N_EXPERTS, LANES), F32)],
        compiler_params=_cparams("arbitrary"),
        name="router",
    )(x2, rwt, rb)


def _plan_kernel(cnt_ref, idx_ref, rank_ref, dest_ref, meta_ref, *, tm, nbl):
    shift = MOE_BM.bit_length() - 1
    cnt = cnt_ref[...].astype(I32)
    padded = ((cnt + (MOE_BM - 1)) >> shift) << shift
    starts = []
    acc = jnp.zeros((1, LANES), I32)
    for e in range(N_EXPERTS):
        starts.append(acc)
        acc = acc + padded[e:e + 1, :]
    pad_start = jnp.concatenate(starts, axis=0)
    pad_end = pad_start + padded
    eio = lax.broadcasted_iota(I32, (N_EXPERTS, tm), 0)
    idx = idx_ref[...]
    rank = rank_ref[...]
    dests = []
    for k in range(2):
        oh = eio == idx[k:k + 1, :]
        dests.append(jnp.sum(jnp.where(oh, pad_start[:, 0:1], 0), axis=0, keepdims=True) + rank[k:k + 1, :])
    dest_ref[...] = jnp.concatenate(dests, axis=0)
    blk0 = lax.broadcasted_iota(I32, (N_EXPERTS, nbl), 1) * MOE_BM
    block_e = jnp.sum((pad_end[:, 0:1] <= blk0).astype(I32), axis=0, keepdims=True)
    block_e = jnp.minimum(block_e, N_EXPERTS - 1)
    n_used = jnp.broadcast_to(acc[:, 0:1] >> shift, (1, nbl))
    meta_ref[...] = jnp.concatenate([block_e, n_used, jnp.zeros((SUBLANES - 2, nbl), I32)], axis=0)


def _plan(cnt, idx, rank, n_blocks, tm=2048):
    n = idx.shape[1]
    tm = min(tm, n)
    nbl = -(-n_blocks // LANES) * LANES
    return pl.pallas_call(
        functools.partial(_plan_kernel, tm=tm, nbl=nbl),
        grid=(n // tm,),
        in_specs=[pl.BlockSpec((N_EXPERTS, LANES), lambda i: (0, 0)),
                  pl.BlockSpec((2, tm), lambda i: (0, i)),
                  pl.BlockSpec((2, tm), lambda i: (0, i))],
        out_specs=[pl.BlockSpec((2, tm), lambda i: (0, i)),
                   pl.BlockSpec((SUBLANES, nbl), lambda i: (0, 0))],
        out_shape=[jax.ShapeDtypeStruct((2, n), I32),
                   jax.ShapeDtypeStruct((SUBLANES, nbl), I32)],
        compiler_params=_cparams("arbitrary"),
        name="moe_plan",
    )(cnt, idx, rank)


def _row_copy(src_hbm, src_row, dst, dst_row, sem):
    return pltpu.make_async_copy(
        src_hbm.at[pl.ds(pl.multiple_of(src_row * ROW_CHUNKS, ROW_CHUNKS), ROW_CHUNKS), :],
        dst.at[pl.ds(pl.multiple_of(dst_row * ROW_CHUNKS, ROW_CHUNKS), ROW_CHUNKS), :],
        sem)


def _dispatch_kernel(dest_ref, x_hbm, xs_in_hbm, xs_hbm, sem, *, n, tc):
    del xs_in_hbm
    base = pl.program_id(0) * tc

    def issue(t, carry):
        tok = base + t
        _row_copy(x_hbm, tok, xs_hbm, dest_ref[tok], sem).start()
        _row_copy(x_hbm, tok, xs_hbm, dest_ref[n + tok], sem).start()
        return carry

    lax.fori_loop(0, tc, issue, 0)

    def drain(t, carry):
        _row_copy(x_hbm, 0, xs_hbm, 0, sem).wait()
        return carry

    lax.fori_loop(0, 2 * tc, drain, 0)


def _dispatch(dest_flat, xr, n_rows, tc=2048):
    n = xr.shape[0] // ROW_CHUNKS
    tc = min(tc, n)
    xs0 = jnp.zeros((n_rows * ROW_CHUNKS, LANES), F32)
    return pl.pallas_call(
        functools.partial(_dispatch_kernel, n=n, tc=tc),
        grid_spec=pltpu.PrefetchScalarGridSpec(
            num_scalar_prefetch=1,
            grid=(n // tc,),
            in_specs=[pl.BlockSpec(memory_space=pl.ANY), pl.BlockSpec(memory_space=pl.ANY)],
            out_specs=pl.BlockSpec(memory_space=pl.ANY),
            scratch_shapes=[pltpu.SemaphoreType.DMA(())]),
        out_shape=jax.ShapeDtypeStruct((n_rows * ROW_CHUNKS, LANES), F32),
        input_output_aliases={2: 0},
        compiler_params=_cparams("arbitrary"),
        name="moe_dispatch",
    )(dest_flat, xr, xs0)


def _experts_kernel(be_ref, nu_ref, xs_ref, wg_ref, wu_ref, wd_ref, y_ref):
    del be_ref
    used = pl.program_id(0) < nu_ref[0]

    @pl.when(used)
    def _():
        x = _read_rows(xs_ref, MOE_BM).astype(BF16)
        h = _silu(_dot(x, wg_ref[0])) * _dot(x, wu_ref[0])
        _write_rows(y_ref, _dot(h.astype(BF16), wd_ref[0]), MOE_BM)

    @pl.when(jnp.logical_not(used))
    def _():
        y_ref[...] = jnp.zeros_like(y_ref)


def _experts(block_e, n_used, xs, wg, wu, wd):
    n_blocks = block_e.shape[0]

    def blk(j, be, nu):
        return jnp.minimum(j, nu[0] - 1)

    return pl.pallas_call(
        _experts_kernel,
        grid_spec=pltpu.PrefetchScalarGridSpec(
            num_scalar_prefetch=2,
            grid=(n_blocks,),
            in_specs=[pl.BlockSpec((MOE_BM * ROW_CHUNKS, LANES), lambda j, be, nu: (j, 0)),
                      pl.BlockSpec((1, D_MODEL, D_EXPERT), lambda j, be, nu: (be[blk(j, be, nu)], 0, 0)),
                      pl.BlockSpec((1, D_MODEL, D_EXPERT), lambda j, be, nu: (be[blk(j, be, nu)], 0, 0)),
                      pl.BlockSpec((1, D_EXPERT, D_MODEL), lambda j, be, nu: (be[blk(j, be, nu)], 0, 0))],
            out_specs=pl.BlockSpec((MOE_BM * ROW_CHUNKS, LANES), lambda j, be, nu: (j, 0))),
        out_shape=jax.ShapeDtypeStruct(xs.shape, F32),
        compiler_params=_cparams("arbitrary"),
        name="moe_experts",
    )(block_e, n_used, xs, wg, wu, wd)


def _combine_ln_kernel(dest_ref, y_hbm, x_ref, w1_ref, w2_ref, g_ref, b_ref, o_ref, buf1, buf2, sem, *, n, tm):
    base = pl.program_id(0) * tm

    def issue(t, carry):
        tok = base + t
        _row_copy(y_hbm, dest_ref[tok], buf1, t, sem).start()
        _row_copy(y_hbm, dest_ref[n + tok], buf2, t, sem).start()
        return carry

    lax.fori_loop(0, tm, issue, 0)

    def drain(t, carry):
        _row_copy(y_hbm, 0, buf1, 0, sem).wait()
        return carry

    lax.fori_loop(0, 2 * tm, drain, 0)
    moe = w1_ref[...] * _read_rows(buf1, tm) + w2_ref[...] * _read_rows(buf2, tm)
    o_ref[...] = _ln(ALPHA * x_ref[...] + moe, g_ref[...], b_ref[...])


def _combine_ln(dest_flat, y, x2, w1, w2, g, b, tm=256):
    n = x2.shape[0]
    return pl.pallas_call(
        functools.partial(_combine_ln_kernel, n=n, tm=tm),
        grid_spec=pltpu.PrefetchScalarGridSpec(
            num_scalar_prefetch=1,
            grid=(n // tm,),
            in_specs=[pl.BlockSpec(memory_space=pl.ANY),
                      pl.BlockSpec((tm, D_MODEL), lambda i, d: (i, 0)),
                      pl.BlockSpec((tm, 1), lambda i, d: (i, 0)),
                      pl.BlockSpec((tm, 1), lambda i, d: (i, 0)),
                      pl.BlockSpec((1, D_MODEL), lambda i, d: (0, 0)),
                      pl.BlockSpec((1, D_MODEL), lambda i, d: (0, 0))],
            out_specs=pl.BlockSpec((tm, D_MODEL), lambda i, d: (i, 0)),
            scratch_shapes=[pltpu.VMEM((tm * ROW_CHUNKS, LANES), F32),
                            pltpu.VMEM((tm * ROW_CHUNKS, LANES), F32),
                            pltpu.SemaphoreType.DMA(())]),
        out_shape=jax.ShapeDtypeStruct((n, D_MODEL), F32),
        compiler_params=_cparams("arbitrary"),
        name="moe_combine_ln",
    )(dest_flat, y, x2, w1, w2, g, b)


def _moe_ln(x2, xr, rwt, rb, wg, wu, wd, g, b):
    n = x2.shape[0]
    n_blocks = (2 * n) // MOE_BM + N_EXPERTS
    idx, rank, w, cnt = _router(x2, rwt, rb)
    dest, meta = _plan(cnt, idx, rank, n_blocks)
    dest_flat = dest.reshape(2 * n)
    block_e = meta[0, :n_blocks]
    n_used = meta[1, :1]
    xs = _dispatch(dest_flat, xr, n_blocks * MOE_BM)
    y = _experts(block_e, n_used, xs, wg, wu, wd)
    return _combine_ln(dest_flat, y, x2, w[0].reshape(n, 1), w[1].reshape(n, 1), g, b)


def _conv_qkv_kernel(xm_ref, cw_ref, cb_ref, wq_ref, wk_ref, wv_ref, q_ref, k_ref, v_ref, xc_ref, *, s):
    xm_b = xm_ref[0]
    xm = xm_b.astype(F32)
    cw = cw_ref[...]
    row = lax.broadcasted_iota(I32, (s, 1), 0)
    half = CONV_K // 2
    acc = cb_ref[...] + xm * cw[half:half + 1, :]
    for sh in range(1, half + 1):
        past = jnp.where(row >= sh, pltpu.roll(xm, sh, axis=0), 0.0)
        acc = acc + past * cw[half - sh:half - sh + 1, :]
        nxt = jnp.where(row < s - sh, pltpu.roll(xm, s - sh, axis=0), 0.0)
        acc = acc + nxt * cw[half + sh:half + sh + 1, :]
    xc = _silu(acc).astype(BF16)
    xc_ref[0] = xc
    q_ref[0] = _dot(xc, wq_ref[0]).astype(BF16)
    k_ref[0] = (_dot(xc, wk_ref[0]) * (ML_DH ** -0.5)).astype(BF16)
    v_ref[0] = _dot(xm_b, wv_ref[0]).astype(BF16)


def _conv_qkv(main3, cw, cb, wq, wk, wv):
    b, s, _ = main3.shape
    tok = pl.BlockSpec((1, s, ML_DHP), lambda i, h: (i, 0, h))
    wspec = pl.BlockSpec((1, ML_DHP, ML_DHP), lambda i, h: (h, 0, 0))
    return pl.pallas_call(
        functools.partial(_conv_qkv_kernel, s=s),
        grid=(b, ML_HEADS),
        in_specs=[tok,
                  pl.BlockSpec((CONV_K, ML_DHP), lambda i, h: (0, h)),
                  pl.BlockSpec((1, ML_DHP), lambda i, h: (0, h)),
                  wspec, wspec, wspec],
        out_specs=[tok] * 4,
        out_shape=[jax.ShapeDtypeStruct((b, s, ML_WP), BF16)] * 4,
        compiler_params=_cparams("parallel", "parallel"),
        name="conv_qkv",
    )(main3, cw, cb, wq, wk, wv)


def _mlstm_kernel(q_ref, k_ref, v_ref, gc_ref, gr_ref, gbc_ref, gbr_ref, z_ref, xc_ref, ng_ref, sk_ref,
                  y_ref, hf_ref, hb_ref, cf_ref, cb_ref, nf_ref, nb_ref, mf_ref, mb_ref, *, s):
    head = pl.program_id(1)
    nc = s // CHUNK
    lane = lax.broadcasted_iota(I32, (1, LANES), 1)
    sub = lax.broadcasted_iota(I32, (LANES, 1), 0)
    ti = lax.broadcasted_iota(I32, (CHUNK, CHUNK), 0)
    tj = lax.broadcasted_iota(I32, (CHUNK, CHUNK), 1)
    hi = lax.Precision.HIGHEST

    for ref in (cf_ref, cb_ref, nf_ref, nb_ref, mf_ref, mb_ref):
        ref[...] = jnp.zeros_like(ref)

    def chunk(c, rev, h_ref, c_ref, n_ref, m_ref):
        t0 = pl.multiple_of(c * CHUNK, CHUNK)
        qb = q_ref[0, pl.ds(t0, CHUNK), :]
        kb = k_ref[0, pl.ds(t0, CHUNK), :]
        vb = v_ref[0, pl.ds(t0, CHUNK), :]
        gc = gc_ref[0, pl.ds(t0, CHUNK), :] + gbc_ref[...]
        gr = gr_ref[:, pl.ds(t0, CHUNK)] + gbr_ref[...]
        i_col = head + (2 * ML_HEADS if rev else 0)
        f_col = i_col + ML_HEADS
        allowed = (tj >= ti) if rev else (tj <= ti)
        tri = allowed.astype(F32)
        bc_all = jnp.dot(tri, jax.nn.log_sigmoid(gc), precision=hi, preferred_element_type=F32)
        br_all = _dot_nt(jax.nn.log_sigmoid(gr), tri, precision=hi)
        b_col = jnp.sum(jnp.where(lane == f_col, bc_all, 0.0), axis=1, keepdims=True)
        i_colv = jnp.sum(jnp.where(lane == i_col, gc, 0.0), axis=1, keepdims=True)
        b_row = jnp.sum(jnp.where(sub == f_col, br_all, 0.0), axis=0, keepdims=True)
        i_row = jnp.sum(jnp.where(sub == i_col, gr, 0.0), axis=0, keepdims=True)
        b_last = b_col[0:1, :] if rev else b_col[CHUNK - 1:CHUNK, :]
        m = m_ref[...]
        cmat = c_ref[...]
        nvec = n_ref[...]

        d = jnp.where(allowed, b_col - b_row + i_row, NEG)
        inter = b_col + m
        m_t = jnp.maximum(jnp.max(d, axis=1, keepdims=True), inter)
        dexp = jnp.exp(d - m_t)
        iexp = jnp.exp(inter - m_t)
        sc = _dot_nt(qb, kb) * dexp
        num = _dot(sc.astype(BF16), vb) + iexp * _dot(qb, cmat.astype(BF16))
        den = jnp.sum(sc, axis=1, keepdims=True) + iexp * jnp.sum(qb.astype(F32) * nvec, axis=1, keepdims=True)
        h_ref[pl.ds(t0, CHUNK), :] = num / jnp.maximum(jnp.abs(den), jnp.exp(-m_t))

        w_s = b_last - b_col + i_colv
        m_new = jnp.maximum(b_last + m, jnp.max(w_s, axis=0, keepdims=True))
        wexp = jnp.exp(w_s - m_new)
        cexp = jnp.exp(b_last + m - m_new)
        kw = kb.astype(F32) * wexp
        c_ref[...] = cexp * cmat + lax.dot_general(kw.astype(BF16), vb, (((0,), (0,)), ((), ())),
                                                   preferred_element_type=F32)
        n_ref[...] = cexp * nvec + jnp.sum(kw, axis=0, keepdims=True)
        m_ref[...] = m_new

    def step(i, carry):
        chunk(i, False, hf_ref, cf_ref, nf_ref, mf_ref)
        chunk(nc - 1 - i, True, hb_ref, cb_ref, nb_ref, mb_ref)
        return carry

    lax.fori_loop(0, nc, step, 0)

    real = lax.broadcasted_iota(I32, (1, ML_DHP), 1) < ML_DH
    tb = 256

    def fin(j, carry):
        t0 = pl.multiple_of(j * tb, tb)
        hs = hf_ref[pl.ds(t0, tb), :] + hb_ref[pl.ds(t0, tb), :]
        mu = jnp.sum(hs, axis=1, keepdims=True) * (1.0 / ML_DH)
        dev = jnp.where(real, hs - mu, 0.0)
        var = jnp.sum(dev * dev, axis=1, keepdims=True) * (1.0 / ML_DH)
        hn = dev * lax.rsqrt(var + LN_EPS) * ng_ref[...]
        xc = xc_ref[0, pl.ds(t0, tb), :].astype(F32)
        z = z_ref[0, pl.ds(t0, tb), :].astype(F32)
        y_ref[0, pl.ds(t0, tb), :] = ((hn + sk_ref[...] * xc) * _silu(z)).astype(BF16)
        return carry

    lax.fori_loop(0, s // tb, fin, 0)


def _mlstm(q, k, v, gcol3, grow, gbc, gbr, main3, xc, ng, sk):
    b, s, _ = q.shape
    tok = pl.BlockSpec((1, s, ML_DHP), lambda i, h: (i, 0, h))
    vec = pl.BlockSpec((1, ML_DHP), lambda i, h: (0, h))
    return pl.pallas_call(
        functools.partial(_mlstm_kernel, s=s),
        grid=(b, ML_HEADS),
        in_specs=[tok, tok, tok,
                  pl.BlockSpec((1, s, LANES), lambda i, h: (i, 0, 0)),
                  pl.BlockSpec((LANES, s), lambda i, h: (0, i)),
                  pl.BlockSpec((1, LANES), lambda i, h: (0, 0)),
                  pl.BlockSpec((LANES, 1), lambda i, h: (0, 0)),
                  pl.BlockSpec((1, s, ML_DHP), lambda i, h: (i, 0, ML_HEADS + h)),
                  tok, vec, vec],
        out_specs=tok,
        out_shape=jax.ShapeDtypeStruct((b, s, ML_WP), BF16),
        scratch_shapes=[pltpu.VMEM((s, ML_DHP), F32), pltpu.VMEM((s, ML_DHP), F32),
                        pltpu.VMEM((ML_DHP, ML_DHP), F32), pltpu.VMEM((ML_DHP, ML_DHP), F32),
                        pltpu.VMEM((1, ML_DHP), F32), pltpu.VMEM((1, ML_DHP), F32),
                        pltpu.VMEM((1, 1), F32), pltpu.VMEM((1, 1), F32)],
        compiler_params=_cparams("parallel", "parallel"),
        name="mlstm",
    )(q, k, v, gcol3, grow, gbc, gbr, main3, xc, ng, sk)


def _pad_heads(a, axis):
    a = jnp.moveaxis(a, axis, -1)
    lead = a.shape[:-1]
    a = a.reshape(lead + (ML_HEADS, ML_DH))
    a = jnp.pad(a, [(0, 0)] * len(lead) + [(0, 0), (0, ML_DHP - ML_DH)])
    return jnp.moveaxis(a.reshape(lead + (ML_WP,)), -1, axis)


def kernel(x, mem, mem_ln_g, mem_ln_b, w_mem_kv, router_w, router_b, na_w_in, na_rpb, ml_w_in, ml_conv_w,
           ml_conv_b, ml_w_qkv, ml_gate_b, ml_norm_g, ml_skip, w_out, ln_g, ln_b, exp_w_gate, exp_w_up,
           exp_w_down):
    b, s, d = x.shape
    n = b * s
    nm = mem.shape[1]
    row = lambda a: a.reshape(1, -1)

    mem_k, mem_v = _memkv(mem.reshape(b * nm, d), row(mem_ln_g), row(mem_ln_b), w_mem_kv.astype(BF16))
    mem_k3 = mem_k.reshape(b, nm, MEM_W)
    mem_v3 = mem_v.reshape(b, nm, MEM_W)
    rwt = router_w.T
    rb = router_b.reshape(N_EXPERTS, 1)

    x2 = x.reshape(n, d)

    h0 = _proj(x2, na_w_in[0].astype(BF16)).reshape(b, s, 3 * NA_W + MEM_W)
    y_na = _na_attention(h0, _na_bias_table(na_rpb[0]))
    y_mem = _mem_attention(h0, 3 * NA_W // LANES, mem_k3, mem_v3)
    wo = w_out[0].astype(BF16)
    x2, xr = _outproj_ln(y_na.reshape(n, NA_W), y_mem.reshape(n, MEM_W), wo[:NA_W], wo[NA_W:], x2,
                         row(ln_g[0, 0]), row(ln_b[0, 0]))
    x2 = _moe_ln(x2, xr, rwt, rb, exp_w_gate[0].astype(BF16), exp_w_up[0].astype(BF16),
                 exp_w_down[0].astype(BF16), row(ln_g[0, 1]), row(ln_b[0, 1]))

    w1 = ml_w_in[0]
    w_main = jnp.concatenate([_pad_heads(w1[:, :ML_W], 1), _pad_heads(w1[:, ML_W:2 * ML_W], 1),
                              w1[:, 2 * ML_W + 4 * ML_HEADS:]], axis=1).astype(BF16)
    w_g = jnp.pad(w1[:, 2 * ML_W:2 * ML_W + 4 * ML_HEADS], ((0, 0), (0, LANES - 4 * ML_HEADS))).astype(BF16)
    main, gcol, grow = _proj_gates(x2, w_main, w_g, w_g.T)
    main3 = main.reshape(b, s, 2 * ML_WP + MEM_W)
    wqkv = jnp.pad(ml_w_qkv[0], ((0, 0), (0, 0), (0, ML_DHP - ML_DH), (0, ML_DHP - ML_DH))).astype(BF16)
    q, k, v, xc = _conv_qkv(main3, _pad_heads(ml_conv_w[0], 1), _pad_heads(row(ml_conv_b[0]), 1),
                            wqkv[0], wqkv[1], wqkv[2])
    gb = jnp.pad(ml_gate_b[0].reshape(4 * ML_HEADS), (0, LANES - 4 * ML_HEADS))
    y_ml = _mlstm(q, k, v, gcol.reshape(b, s, LANES), grow, gb.reshape(1, LANES), gb.reshape(LANES, 1), main3, xc,
                  _pad_heads(row(ml_norm_g[0]), 1), _pad_heads(row(ml_skip[0]), 1))
    y_mem = _mem_attention(main3, 2 * ML_WP // LANES, mem_k3, mem_v3)
    wo = w_out[1]
    x2, xr = _outproj_ln(y_ml.reshape(n, ML_WP), y_mem.reshape(n, MEM_W), _pad_heads(wo[:ML_W], 0).astype(BF16),
                         wo[ML_W:].astype(BF16), x2, row(ln_g[1, 0]), row(ln_b[1, 0]))
    x2 = _moe_ln(x2, xr, rwt, rb, exp_w_gate[1].astype(BF16), exp_w_up[1].astype(BF16),
                 exp_w_down[1].astype(BF16), row(ln_g[1, 1]), row(ln_b[1, 1]))
    return x2.reshape(b, s, d)
```

```python
import functools

import numpy as np
import jax
import jax.numpy as jnp
from jax import lax
from jax.experimental import pallas as pl
from jax.experimental.pallas import tpu as pltpu

F32 = jnp.float32
BF16 = jnp.bfloat16
I32 = jnp.int32

D_MODEL = 1024
DEPTH = 2
GRID_W = 64
MEM_HEADS = 4
MEM_DH = 64
MEM_W = MEM_HEADS * MEM_DH
NA_HEADS = 12
NA_DH = 64
NA_W = NA_HEADS * NA_DH
WIN_H = 8
WIN_W = 16
ML_HEADS = 4
ML_DH = 192
ML_DHP = 256
ML_W = ML_HEADS * ML_DH
ML_WP = ML_HEADS * ML_DHP
CONV_K = 5
CHUNK = 128
N_EXPERTS = 16
N_GROUPS = 4
EXPERTS_PER_GROUP = N_EXPERTS // N_GROUPS
D_EXPERT = 512
ALPHA = (2 * DEPTH) ** 0.25
LN_EPS = 1e-5
NEG = -1e30

LANES = 128
SUBLANES = 8
ROW_CHUNKS = D_MODEL // LANES
MOE_BM = 256
VMEM_LIMIT = 48 * 1024 * 1024


def _cparams(*sem):
    return pltpu.CompilerParams(dimension_semantics=sem, vmem_limit_bytes=VMEM_LIMIT)


def _dot(a, b):
    return jnp.dot(a, b, preferred_element_type=F32)


def _dot_nt(a, b, precision=None):
    return lax.dot_general(a, b, (((1,), (1,)), ((), ())), precision=precision,
                           preferred_element_type=F32)


def _ln(z, g, b):
    mu = jnp.mean(z, axis=-1, keepdims=True)
    zc = z - mu
    var = jnp.mean(zc * zc, axis=-1, keepdims=True)
    return zc * lax.rsqrt(var + LN_EPS) * g + b


def _silu(x):
    return x * jax.nn.sigmoid(x)


def _read_rows(ref, n):
    return jnp.concatenate([ref[pl.ds(j, n, stride=ROW_CHUNKS), :] for j in range(ROW_CHUNKS)], axis=1)


def _write_rows(ref, val, n):
    for j in range(ROW_CHUNKS):
        ref[pl.ds(j, n, stride=ROW_CHUNKS), :] = val[:, j * LANES:(j + 1) * LANES]


def _memkv_kernel(m_ref, g_ref, b_ref, w_ref, k_ref, v_ref):
    z = _ln(m_ref[...], g_ref[...], b_ref[...])
    kv = _dot(z.astype(BF16), w_ref[...])
    k_ref[...] = kv[:, :MEM_W].astype(BF16)
    v_ref[...] = kv[:, MEM_W:].astype(BF16)


def _memkv(mem2, g, b, w):
    n = mem2.shape[0]
    tm = 256
    return pl.pallas_call(
        _memkv_kernel,
        grid=(n // tm,),
        in_specs=[pl.BlockSpec((tm, D_MODEL), lambda i: (i, 0)),
                  pl.BlockSpec((1, D_MODEL), lambda i: (0, 0)),
                  pl.BlockSpec((1, D_MODEL), lambda i: (0, 0)),
                  pl.BlockSpec((D_MODEL, 2 * MEM_W), lambda i: (0, 0))],
        out_specs=[pl.BlockSpec((tm, MEM_W), lambda i: (i, 0)),
                   pl.BlockSpec((tm, MEM_W), lambda i: (i, 0))],
        out_shape=[jax.ShapeDtypeStruct((n, MEM_W), BF16)] * 2,
        compiler_params=_cparams("parallel"),
        name="memkv",
    )(mem2, g, b, w)


def _proj_kernel(x_ref, w_ref, o_ref):
    o_ref[...] = _dot(x_ref[...].astype(BF16), w_ref[...]).astype(o_ref.dtype)


def _proj(x2, w, tm=512):
    n, k = x2.shape
    nout = w.shape[1]
    return pl.pallas_call(
        _proj_kernel,
        grid=(n // tm,),
        in_specs=[pl.BlockSpec((tm, k), lambda i: (i, 0)),
                  pl.BlockSpec((k, nout), lambda i: (0, 0))],
        out_specs=pl.BlockSpec((tm, nout), lambda i: (i, 0)),
        out_shape=jax.ShapeDtypeStruct((n, nout), BF16),
        compiler_params=_cparams("parallel"),
        name="in_proj",
    )(x2, w)


def _proj_gates_kernel(x_ref, w_ref, wg_ref, wgt_ref, o_ref, g_ref, gt_ref):
    xb = x_ref[...].astype(BF16)
    o_ref[...] = _dot(xb, w_ref[...]).astype(BF16)
    g_ref[...] = _dot(xb, wg_ref[...])
    gt_ref[...] = _dot_nt(wgt_ref[...], xb)


def _proj_gates(x2, w, wg, wgt, tm=512):
    n, k = x2.shape
    nout = w.shape[1]
    return pl.pallas_call(
        _proj_gates_kernel,
        grid=(n // tm,),
        in_specs=[pl.BlockSpec((tm, k), lambda i: (i, 0)),
                  pl.BlockSpec((k, nout), lambda i: (0, 0)),
                  pl.BlockSpec((k, LANES), lambda i: (0, 0)),
                  pl.BlockSpec((LANES, k), lambda i: (0, 0))],
        out_specs=[pl.BlockSpec((tm, nout), lambda i: (i, 0)),
                   pl.BlockSpec((tm, LANES), lambda i: (i, 0)),
                   pl.BlockSpec((LANES, tm), lambda i: (0, i))],
        out_shape=[jax.ShapeDtypeStruct((n, nout), BF16),
                   jax.ShapeDtypeStruct((n, LANES), F32),
                   jax.ShapeDtypeStruct((LANES, n), F32)],
        compiler_params=_cparams("parallel"),
        name="in_proj_gates",
    )(x2, w, wg, wgt)


def _softmax_pv(s, v):
    m = jnp.max(s, axis=-1, keepdims=True)
    p = jnp.exp(s - m)
    l = jnp.sum(p, axis=-1, keepdims=True)
    return _dot(p.astype(BF16), v) / l


def _na_kernel(q_ref, k_ref, v_ref, tbl_ref, o_ref, *, rows):
    lane = lax.broadcasted_iota(I32, (1, LANES), 1)
    first = lane < NA_DH
    nkeys = WIN_H * GRID_W

    def row(r, carry):
        rs = jnp.clip(r - WIN_H // 2, 0, rows - WIN_H)
        q = q_ref[0, pl.ds(pl.multiple_of(r * GRID_W, GRID_W), GRID_W), :]
        k = k_ref[0, pl.ds(pl.multiple_of(rs * GRID_W, GRID_W), nkeys), :]
        v = v_ref[0, pl.ds(pl.multiple_of(rs * GRID_W, GRID_W), nkeys), :]
        q = q * jnp.asarray(NA_DH ** -0.5, BF16)
        dr0 = rs - r + WIN_H - 1
        outs = []
        for half in range(2):
            qh = jnp.where(first if half == 0 else jnp.logical_not(first), q, jnp.zeros_like(q))
            s = _dot_nt(qh, k)
            bias = jnp.concatenate([tbl_ref[0, half, dr0 + 2 * m] for m in range(WIN_H // 2)], axis=1)
            outs.append(_softmax_pv(s + bias, v))
        o = jnp.where(first, outs[0], outs[1])
        o_ref[0, pl.ds(pl.multiple_of(r * GRID_W, GRID_W), GRID_W), :] = o.astype(o_ref.dtype)
        return carry

    lax.fori_loop(0, rows, row, 0)


def _na_bias_table(rpb):
    qc = np.arange(GRID_W)[:, None]
    kc = np.arange(GRID_W)[None, :]
    cs = np.clip(qc - WIN_W // 2, 0, GRID_W - WIN_W)
    col_in = (kc >= cs) & (kc < cs + WIN_W)
    dc = np.clip(kc - qc + WIN_W - 1, 0, 2 * WIN_W - 2)
    t = jnp.where(col_in, rpb[:, :, dc], NEG).astype(F32)
    t2 = jnp.concatenate([t[:, :-1], t[:, 1:]], axis=-1)
    return t2.reshape(NA_HEADS // 2, 2, 2 * WIN_H - 2, GRID_W, 2 * GRID_W)


def _na_attention(h3, tbl):
    b, s, _ = h3.shape
    rows = s // GRID_W
    npair = NA_HEADS // 2
    return pl.pallas_call(
        functools.partial(_na_kernel, rows=rows),
        grid=(b, npair),
        in_specs=[pl.BlockSpec((1, s, LANES), lambda i, p: (i, 0, p)),
                  pl.BlockSpec((1, s, LANES), lambda i, p: (i, 0, npair + p)),
                  pl.BlockSpec((1, s, LANES), lambda i, p: (i, 0, 2 * npair + p)),
                  pl.BlockSpec((1, 2, 2 * WIN_H - 2, GRID_W, 2 * GRID_W), lambda i, p: (p, 0, 0, 0, 0))],
        out_specs=pl.BlockSpec((1, s, LANES), lambda i, p: (i, 0, p)),
        out_shape=jax.ShapeDtypeStruct((b, s, NA_W), BF16),
        compiler_params=_cparams("parallel", "parallel"),
        name="na_attention",
    )(h3, h3, h3, tbl)


def _mem_attn_kernel(q_ref, k_ref, v_ref, o_ref):
    lane = lax.broadcasted_iota(I32, (1, LANES), 1)
    first = lane < MEM_DH
    q = q_ref[0] * jnp.asarray(MEM_DH ** -0.5, BF16)
    k = k_ref[0]
    v = v_ref[0]
    outs = []
    for half in range(2):
        qh = jnp.where(first if half == 0 else jnp.logical_not(first), q, jnp.zeros_like(q))
        outs.append(_softmax_pv(_dot_nt(qh, k), v))
    o_ref[0] = jnp.where(first, outs[0], outs[1]).astype(o_ref.dtype)


def _mem_attention(h3, col_block0, mem_k3, mem_v3, tq=512):
    b, s, _ = h3.shape
    nm = mem_k3.shape[1]
    npair = MEM_HEADS // 2
    return pl.pallas_call(
        _mem_attn_kernel,
        grid=(b, npair, s // tq),
        in_specs=[pl.BlockSpec((1, tq, LANES), lambda i, p, t: (i, t, col_block0 + p)),
                  pl.BlockSpec((1, nm, LANES), lambda i, p, t: (i, 0, p)),
                  pl.BlockSpec((1, nm, LANES), lambda i, p, t: (i, 0, p))],
        out_specs=pl.BlockSpec((1, tq, LANES), lambda i, p, t: (i, t, p)),
        out_shape=jax.ShapeDtypeStruct((b, s, MEM_W), BF16),
        compiler_params=_cparams("parallel", "parallel", "parallel"),
        name="mem_attention",
    )(h3, mem_k3, mem_v3)


def _outproj_ln_kernel(ya_ref, ym_ref, wa_ref, wm_ref, x_ref, g_ref, b_ref, o_ref, or_ref, *, tm):
    acc = _dot(ya_ref[...], wa_ref[...]) + _dot(ym_ref[...], wm_ref[...])
    out = _ln(ALPHA * x_ref[...] + acc, g_ref[...], b_ref[...])
    o_ref[...] = out
    _write_rows(or_ref, out, tm)


def _outproj_ln(ya, ym, wa, wm, x2, g, b, tm=512):
    n = x2.shape[0]
    ka, km = ya.shape[1], ym.shape[1]
    return pl.pallas_call(
        functools.partial(_outproj_ln_kernel, tm=tm),
        grid=(n // tm,),
        in_specs=[pl.BlockSpec((tm, ka), lambda i: (i, 0)),
                  pl.BlockSpec((tm, km), lambda i: (i, 0)),
                  pl.BlockSpec((ka, D_MODEL), lambda i: (0, 0)),
                  pl.BlockSpec((km, D_MODEL), lambda i: (0, 0)),
                  pl.BlockSpec((tm, D_MODEL), lambda i: (i, 0)),
                  pl.BlockSpec((1, D_MODEL), lambda i: (0, 0)),
                  pl.BlockSpec((1, D_MODEL), lambda i: (0, 0))],
        out_specs=[pl.BlockSpec((tm, D_MODEL), lambda i: (i, 0)),
                   pl.BlockSpec((tm * ROW_CHUNKS, LANES), lambda i: (i, 0))],
        out_shape=[jax.ShapeDtypeStruct((n, D_MODEL), F32),
                   jax.ShapeDtypeStruct((n * ROW_CHUNKS, LANES), F32)],
        compiler_params=_cparams("parallel"),
        name="outproj_ln",
    )(ya, ym, wa, wm, x2, g, b)


def _router_kernel(x_ref, rwt_ref, rb_ref, idx_ref, rank_ref, w_ref, cnt_ref, *, tm):
    @pl.when(pl.program_id(0) == 0)
    def _():
        cnt_ref[...] = jnp.zeros_like(cnt_ref)

    logits = _dot_nt(rwt_ref[...], x_ref[...], precision=lax.Precision.HIGHEST)
    scores = jax.nn.sigmoid(logits)
    biased = scores + rb_ref[...]
    bv = [biased[e:e + 1, :] for e in range(N_EXPERTS)]
    sv = [scores[e:e + 1, :] for e in range(N_EXPERTS)]

    grp = []
    for g in range(N_GROUPS):
        m = bv[g * EXPERTS_PER_GROUP:(g + 1) * EXPERTS_PER_GROUP]
        best = None
        for a in range(EXPERTS_PER_GROUP):
            for c in range(a + 1, EXPERTS_PER_GROUP):
                pair = m[a] + m[c]
                best = pair if best is None else jnp.maximum(best, pair)
        grp.append(best)
    gsel = jnp.zeros((1, tm), I32)
    gbest = grp[0]
    for g in range(1, N_GROUPS):
        better = grp[g] > gbest
        gsel = jnp.where(better, g, gsel)
        gbest = jnp.where(better, grp[g], gbest)

    def pick(vals, j):
        out = vals[j]
        for g in range(1, N_GROUPS):
            out = jnp.where(gsel == g, vals[g * EXPERTS_PER_GROUP + j], out)
        return out

    cb = [pick(bv, j) for j in range(EXPERTS_PER_GROUP)]
    cs = [pick(sv, j) for j in range(EXPERTS_PER_GROUP)]
    i1 = jnp.zeros((1, tm), I32)
    m1 = cb[0]
    s1 = cs[0]
    for j in range(1, EXPERTS_PER_GROUP):
        gt = cb[j] > m1
        i1 = jnp.where(gt, j, i1)
        m1 = jnp.where(gt, cb[j], m1)
        s1 = jnp.where(gt, cs[j], s1)
    i2 = jnp.zeros((1, tm), I32)
    m2 = jnp.full((1, tm), -jnp.inf, F32)
    s2 = jnp.zeros((1, tm), F32)
    for j in range(EXPERTS_PER_GROUP):
        ok = jnp.logical_and(i1 != j, cb[j] > m2)
        i2 = jnp.where(ok, j, i2)
        m2 = jnp.where(ok, cb[j], m2)
        s2 = jnp.where(ok, cs[j], s2)
    e1 = gsel * EXPERTS_PER_GROUP + i1
    e2 = gsel * EXPERTS_PER_GROUP + i2
    tot = s1 + s2
    idx_ref[...] = jnp.concatenate([e1, e2], axis=0)
    w_ref[...] = jnp.concatenate([s1 / tot, s2 / tot], axis=0)

    eio = lax.broadcasted_iota(I32, (N_EXPERTS, tm), 0)
    oh1 = eio == e1
    oh2 = eio == e2
    ohs = jnp.logical_or(oh1, oh2).astype(F32)
    before = (lax.broadcasted_iota(I32, (tm, tm), 0) < lax.broadcasted_iota(I32, (tm, tm), 1))
    pre = _dot(ohs.astype(BF16), before.astype(BF16))
    pos = cnt_ref[:, 0:1] + pre
    r1 = jnp.sum(jnp.where(oh1, pos, 0.0), axis=0, keepdims=True)
    r2 = jnp.sum(jnp.where(oh2, pos, 0.0), axis=0, keepdims=True)
    rank_ref[...] = jnp.concatenate([r1, r2], axis=0).astype(I32)
    cnt_ref[...] += jnp.sum(ohs, axis=1, keepdims=True)


def _router(x2, rwt, rb, tm=512):
    n = x2.shape[0]
    return pl.pallas_call(
        functools.partial(_router_kernel, tm=tm),
        grid=(n // tm,),
        in_specs=[pl.BlockSpec((tm, D_MODEL), lambda i: (i, 0)),
                  pl.BlockSpec((N_EXPERTS, D_MODEL), lambda i: (0, 0)),
                  pl.BlockSpec((N_EXPERTS, 1), lambda i: (0, 0))],
        out_specs=[pl.BlockSpec((2, tm), lambda i: (0, i)),
                   pl.BlockSpec((2, tm), lambda i: (0, i)),
                   pl.BlockSpec((2, tm), lambda i: (0, i)),
                   pl.BlockSpec((N_EXPERTS, LANES), lambda i: (0, 0))],
        out_shape=[jax.ShapeDtypeStruct((2, n), I32),
                   jax.ShapeDtypeStruct((2, n), I32),
                   jax.ShapeDtypeStruct((2, n), F32),
                   jax.ShapeDtypeStruct((N_EXPERTS, LANES), F32)],
        compiler_params=_cparams("arbitrary"),
        name="router",
    )(x2, rwt, rb)


def _plan_kernel(cnt_ref, idx_ref, rank_ref, dest_ref, meta_ref, *, tm, nbl):
    shift = MOE_BM.bit_length() - 1
    cnt = cnt_ref[...].astype(I32)
    padded = ((cnt + (MOE_BM - 1)) >> shift) << shift
    starts = []
    acc = jnp.zeros((1, LANES), I32)
    for e in range(N_EXPERTS):
        starts.append(acc)
        acc = acc + padded[e:e + 1, :]
    pad_start = jnp.concatenate(starts, axis=0)
    pad_end = pad_start + padded
    eio = lax.broadcasted_iota(I32, (N_EXPERTS, tm), 0)
    idx = idx_ref[...]
    rank = rank_ref[...]
    dests = []
    for k in range(2):
        oh = eio == idx[k:k + 1, :]
        dests.append(jnp.sum(jnp.where(oh, pad_start[:, 0:1], 0), axis=0, keepdims=True) + rank[k:k + 1, :])
    dest_ref[...] = jnp.concatenate(dests, axis=0)
    blk0 = lax.broadcasted_iota(I32, (N_EXPERTS, nbl), 1) * MOE_BM
    block_e = jnp.sum((pad_end[:, 0:1] <= blk0).astype(I32), axis=0, keepdims=True)
    block_e = jnp.minimum(block_e, N_EXPERTS - 1)
    n_used = jnp.broadcast_to(acc[:, 0:1] >> shift, (1, nbl))
    diag = lax.broadcasted_iota(I32, (N_EXPERTS, nbl), 0) == lax.broadcasted_iota(I32, (N_EXPERTS, nbl), 1)
    fill_lo = jnp.sum(jnp.where(diag, (pad_start + cnt)[:, 0:1], 0), axis=0, keepdims=True)
    fill_hi = jnp.sum(jnp.where(diag, pad_end[:, 0:1], 0), axis=0, keepdims=True)
    meta_ref[...] = jnp.concatenate([block_e, n_used, fill_lo, fill_hi, jnp.zeros((SUBLANES - 4, nbl), I32)],
                                    axis=0)


def _plan(cnt, idx, rank, n_blocks, tm=2048):
    n = idx.shape[1]
    tm = min(tm, n)
    nbl = -(-n_blocks // LANES) * LANES
    return pl.pallas_call(
        functools.partial(_plan_kernel, tm=tm, nbl=nbl),
        grid=(n // tm,),
        in_specs=[pl.BlockSpec((N_EXPERTS, LANES), lambda i: (0, 0)),
                  pl.BlockSpec((2, tm), lambda i: (0, i)),
                  pl.BlockSpec((2, tm), lambda i: (0, i))],
        out_specs=[pl.BlockSpec((2, tm), lambda i: (0, i)),
                   pl.BlockSpec((SUBLANES, nbl), lambda i: (0, 0))],
        out_shape=[jax.ShapeDtypeStruct((2, n), I32),
                   jax.ShapeDtypeStruct((SUBLANES, nbl), I32)],
        compiler_params=_cparams("arbitrary"),
        name="moe_plan",
    )(cnt, idx, rank)


def _row_copy(src_hbm, src_row, dst, dst_row, sem):
    return pltpu.make_async_copy(
        src_hbm.at[pl.ds(pl.multiple_of(src_row * ROW_CHUNKS, ROW_CHUNKS), ROW_CHUNKS), :],
        dst.at[pl.ds(pl.multiple_of(dst_row * ROW_CHUNKS, ROW_CHUNKS), ROW_CHUNKS), :],
        sem)


def _rows_wait(src_hbm, buf, sem):
    pltpu.make_async_copy(src_hbm.at[pl.ds(0, buf.shape[0]), :], buf, sem).wait()


def _invert_kernel(dest_ref, lo_ref, hi_ref, src_ref, *, n, n_rows):
    unroll = 8

    def scatter(i, carry):
        for u in range(unroll):
            t = i * unroll + u
            src_ref[dest_ref[t]] = t
            src_ref[dest_ref[n + t]] = t
        return carry

    lax.fori_loop(0, n // unroll, scatter, 0)

    def fill(slot, carry):
        src_ref[slot] = 0
        return carry

    for e in range(N_EXPERTS):
        lax.fori_loop(lo_ref[e], hi_ref[e], fill, 0)
    lax.fori_loop(hi_ref[N_EXPERTS - 1], n_rows, fill, 0)


def _invert(dest_flat, fill_lo, fill_hi, n_rows):
    n = dest_flat.shape[0] // 2
    return pl.pallas_call(
        functools.partial(_invert_kernel, n=n, n_rows=n_rows),
        grid_spec=pltpu.PrefetchScalarGridSpec(
            num_scalar_prefetch=3,
            grid=(1,),
            in_specs=[],
            out_specs=pl.BlockSpec(memory_space=pltpu.SMEM)),
        out_shape=jax.ShapeDtypeStruct((n_rows,), I32),
        compiler_params=_cparams("arbitrary"),
        name="moe_invert",
    )(dest_flat, fill_lo, fill_hi)


def _experts_kernel(be_ref, nu_ref, src_ref, x_hbm, wg_ref, wu_ref, wd_ref, y_ref, xb0, xb1, sem):
    del be_ref
    j = pl.program_id(0)
    n_used = nu_ref[0]
    used = j < n_used
    bufs = (xb0, xb1)

    def issue(blk, slot):
        base = blk * MOE_BM
        for r in range(MOE_BM):
            _row_copy(x_hbm, src_ref[base + r], bufs[slot], r, sem.at[slot]).start()

    @pl.when(j == 0)
    def _():
        issue(0, 0)

    for slot in range(2):
        @pl.when(jnp.logical_and(used, j % 2 == slot))
        def _():
            issue(jnp.minimum(j + 1, n_used - 1), 1 - slot)
            _rows_wait(x_hbm, bufs[slot], sem.at[slot])
            x = _read_rows(bufs[slot], MOE_BM).astype(BF16)
            h = _silu(_dot(x, wg_ref[0])) * _dot(x, wu_ref[0])
            _write_rows(y_ref, _dot(h.astype(BF16), wd_ref[0]), MOE_BM)

    @pl.when(j == n_used - 1)
    def _():
        _rows_wait(x_hbm, xb0, sem.at[(j + 1) % 2])

    @pl.when(jnp.logical_not(used))
    def _():
        y_ref[...] = jnp.zeros_like(y_ref)


def _experts(block_e, n_used, src, xr, wg, wu, wd):
    n_blocks = block_e.shape[0]

    def wblk(j, be, nu, sr):
        return (be[jnp.minimum(j, nu[0] - 1)], 0, 0)

    return pl.pallas_call(
        _experts_kernel,
        grid_spec=pltpu.PrefetchScalarGridSpec(
            num_scalar_prefetch=3,
            grid=(n_blocks,),
            in_specs=[pl.BlockSpec(memory_space=pl.ANY),
                      pl.BlockSpec((1, D_MODEL, D_EXPERT), wblk),
                      pl.BlockSpec((1, D_MODEL, D_EXPERT), wblk),
                      pl.BlockSpec((1, D_EXPERT, D_MODEL), wblk)],
            out_specs=pl.BlockSpec((MOE_BM * ROW_CHUNKS, LANES), lambda j, be, nu, sr: (j, 0)),
            scratch_shapes=[pltpu.VMEM((MOE_BM * ROW_CHUNKS, LANES), F32),
                            pltpu.VMEM((MOE_BM * ROW_CHUNKS, LANES), F32),
                            pltpu.SemaphoreType.DMA((2,))]),
        out_shape=jax.ShapeDtypeStruct((n_blocks * MOE_BM * ROW_CHUNKS, LANES), F32),
        compiler_params=_cparams("arbitrary"),
        name="moe_experts",
    )(block_e, n_used, src, xr, wg, wu, wd)


def _combine_ln_kernel(dest_ref, y_hbm, x_ref, w1_ref, w2_ref, g_ref, b_ref, o_ref, a0, a1, b0, b1, sem, *, n, tm):
    i = pl.program_id(0)
    nt = pl.num_programs(0)
    abufs = (a0, a1)
    bbufs = (b0, b1)
    unroll = 8

    def issue(tile, slot):
        base = tile * tm

        def body(c, carry):
            for u in range(unroll):
                t = c * unroll + u
                _row_copy(y_hbm, dest_ref[base + t], abufs[slot], t, sem.at[slot]).start()
                _row_copy(y_hbm, dest_ref[n + base + t], bbufs[slot], t, sem.at[slot]).start()
            return carry

        lax.fori_loop(0, tm // unroll, body, 0)

    @pl.when(i == 0)
    def _():
        issue(0, 0)

    for slot in range(2):
        @pl.when(i % 2 == slot)
        def _():
            @pl.when(i + 1 < nt)
            def _():
                issue(i + 1, 1 - slot)

            _rows_wait(y_hbm, abufs[slot], sem.at[slot])
            _rows_wait(y_hbm, bbufs[slot], sem.at[slot])
            moe = w1_ref[...] * _read_rows(abufs[slot], tm) + w2_ref[...] * _read_rows(bbufs[slot], tm)
            o_ref[...] = _ln(ALPHA * x_ref[...] + moe, g_ref[...], b_ref[...])


def _combine_ln(dest_flat, y, x2, w1, w2, g, b, tm=256):
    n = x2.shape[0]
    return pl.pallas_call(
        functools.partial(_combine_ln_kernel, n=n, tm=tm),
        grid_spec=pltpu.PrefetchScalarGridSpec(
            num_scalar_prefetch=1,
            grid=(n // tm,),
            in_specs=[pl.BlockSpec(memory_space=pl.ANY),
                      pl.BlockSpec((tm, D_MODEL), lambda i, d: (i, 0)),
                      pl.BlockSpec((tm, 1), lambda i, d: (i, 0)),
                      pl.BlockSpec((tm, 1), lambda i, d: (i, 0)),
                      pl.BlockSpec((1, D_MODEL), lambda i, d: (0, 0)),
                      pl.BlockSpec((1, D_MODEL), lambda i, d: (0, 0))],
            out_specs=pl.BlockSpec((tm, D_MODEL), lambda i, d: (i, 0)),
            scratch_shapes=[pltpu.VMEM((tm * ROW_CHUNKS, LANES), F32)] * 4
                           + [pltpu.SemaphoreType.DMA((2,))]),
        out_shape=jax.ShapeDtypeStruct((n, D_MODEL), F32),
        compiler_params=_cparams("arbitrary"),
        name="moe_combine_ln",
    )(dest_flat, y, x2, w1, w2, g, b)


def _moe_ln(x2, xr, rwt, rb, wg, wu, wd, g, b):
    n = x2.shape[0]
    n_blocks = (2 * n) // MOE_BM + N_EXPERTS
    idx, rank, w, cnt = _router(x2, rwt, rb)
    dest, meta = _plan(cnt, idx, rank, n_blocks)
    dest_flat = dest.reshape(2 * n)
    block_e = meta[0, :n_blocks]
    n_used = meta[1, :1]
    src = _invert(dest_flat, meta[2, :N_EXPERTS], meta[3, :N_EXPERTS], n_blocks * MOE_BM)
    y = _experts(block_e, n_used, src, xr, wg, wu, wd)
    return _combine_ln(dest_flat, y, x2, w[0].reshape(n, 1), w[1].reshape(n, 1), g, b)


def _conv_qkv_kernel(xm_ref, cw_ref, cb_ref, wq_ref, wk_ref, wv_ref, q_ref, k_ref, v_ref, xc_ref, *, s):
    xm_b = xm_ref[0]
    xm = xm_b.astype(F32)
    cw = cw_ref[...]
    row = lax.broadcasted_iota(I32, (s, 1), 0)
    half = CONV_K // 2
    acc = cb_ref[...] + xm * cw[half:half + 1, :]
    for sh in range(1, half + 1):
        past = jnp.where(row >= sh, pltpu.roll(xm, sh, axis=0), 0.0)
        acc = acc + past * cw[half - sh:half - sh + 1, :]
        nxt = jnp.where(row < s - sh, pltpu.roll(xm, s - sh, axis=0), 0.0)
        acc = acc + nxt * cw[half + sh:half + sh + 1, :]
    xc = _silu(acc).astype(BF16)
    xc_ref[0] = xc
    q_ref[0] = _dot(xc, wq_ref[0]).astype(BF16)
    k_ref[0] = (_dot(xc, wk_ref[0]) * (ML_DH ** -0.5)).astype(BF16)
    v_ref[0] = _dot(xm_b, wv_ref[0]).astype(BF16)


def _conv_qkv(main3, cw, cb, wq, wk, wv):
    b, s, _ = main3.shape
    tok = pl.BlockSpec((1, s, ML_DHP), lambda i, h: (i, 0, h))
    wspec = pl.BlockSpec((1, ML_DHP, ML_DHP), lambda i, h: (h, 0, 0))
    return pl.pallas_call(
        functools.partial(_conv_qkv_kernel, s=s),
        grid=(b, ML_HEADS),
        in_specs=[tok,
                  pl.BlockSpec((CONV_K, ML_DHP), lambda i, h: (0, h)),
                  pl.BlockSpec((1, ML_DHP), lambda i, h: (0, h)),
                  wspec, wspec, wspec],
        out_specs=[tok] * 4,
        out_shape=[jax.ShapeDtypeStruct((b, s, ML_WP), BF16)] * 4,
        compiler_params=_cparams("parallel", "parallel"),
        name="conv_qkv",
    )(main3, cw, cb, wq, wk, wv)


def _mlstm_kernel(q_ref, k_ref, v_ref, gc_ref, gr_ref, gbc_ref, gbr_ref, z_ref, xc_ref, ng_ref, sk_ref,
                  y_ref, hf_ref, hb_ref, cf_ref, cb_ref, nf_ref, nb_ref, mf_ref, mb_ref, *, s):
    head = pl.program_id(1)
    nc = s // CHUNK
    lane = lax.broadcasted_iota(I32, (1, LANES), 1)
    sub = lax.broadcasted_iota(I32, (LANES, 1), 0)
    ti = lax.broadcasted_iota(I32, (CHUNK, CHUNK), 0)
    tj = lax.broadcasted_iota(I32, (CHUNK, CHUNK), 1)
    hi = lax.Precision.HIGHEST

    for ref in (cf_ref, cb_ref, nf_ref, nb_ref, mf_ref, mb_ref):
        ref[...] = jnp.zeros_like(ref)

    def chunk(c, rev, h_ref, c_ref, n_ref, m_ref):
        t0 = pl.multiple_of(c * CHUNK, CHUNK)
        qb = q_ref[0, pl.ds(t0, CHUNK), :]
        kb = k_ref[0, pl.ds(t0, CHUNK), :]
        vb = v_ref[0, pl.ds(t0, CHUNK), :]
        gc = gc_ref[0, pl.ds(t0, CHUNK), :] + gbc_ref[...]
        gr = gr_ref[:, pl.ds(t0, CHUNK)] + gbr_ref[...]
        i_col = head + (2 * ML_HEADS if rev else 0)
        f_col = i_col + ML_HEADS
        allowed = (tj >= ti) if rev else (tj <= ti)
        tri = allowed.astype(F32)
        bc_all = jnp.dot(tri, jax.nn.log_sigmoid(gc), precision=hi, preferred_element_type=F32)
        br_all = _dot_nt(jax.nn.log_sigmoid(gr), tri, precision=hi)
        b_col = jnp.sum(jnp.where(lane == f_col, bc_all, 0.0), axis=1, keepdims=True)
        i_colv = jnp.sum(jnp.where(lane == i_col, gc, 0.0), axis=1, keepdims=True)
        b_row = jnp.sum(jnp.where(sub == f_col, br_all, 0.0), axis=0, keepdims=True)
        i_row = jnp.sum(jnp.where(sub == i_col, gr, 0.0), axis=0, keepdims=True)
        b_last = b_col[0:1, :] if rev else b_col[CHUNK - 1:CHUNK, :]
        m = m_ref[...]
        cmat = c_ref[...]
        nvec = n_ref[...]

        d = jnp.where(allowed, b_col - b_row + i_row, NEG)
        inter = b_col + m
        m_t = jnp.maximum(jnp.max(d, axis=1, keepdims=True), inter)
        dexp = jnp.exp(d - m_t)
        iexp = jnp.exp(inter - m_t)
        sc = _dot_nt(qb, kb) * dexp
        num = _dot(sc.astype(BF16), vb) + iexp * _dot(qb, cmat.astype(BF16))
        den = jnp.sum(sc, axis=1, keepdims=True) + iexp * jnp.sum(qb.astype(F32) * nvec, axis=1, keepdims=True)
        h_ref[pl.ds(t0, CHUNK), :] = num / jnp.maximum(jnp.abs(den), jnp.exp(-m_t))

        w_s = b_last - b_col + i_colv
        m_new = jnp.maximum(b_last + m, jnp.max(w_s, axis=0, keepdims=True))
        wexp = jnp.exp(w_s - m_new)
        cexp = jnp.exp(b_last + m - m_new)
        kw = kb.astype(F32) * wexp
        c_ref[...] = cexp * cmat + lax.dot_general(kw.astype(BF16), vb, (((0,), (0,)), ((), ())),
                                                   preferred_element_type=F32)
        n_ref[...] = cexp * nvec + jnp.sum(kw, axis=0, keepdims=True)
        m_ref[...] = m_new

    def step(i, carry):
        chunk(i, False, hf_ref, cf_ref, nf_ref, mf_ref)
        chunk(nc - 1 - i, True, hb_ref, cb_ref, nb_ref, mb_ref)
        return carry

    lax.fori_loop(0, nc, step, 0)

    real = lax.broadcasted_iota(I32, (1, ML_DHP), 1) < ML_DH
    tb = 256

    def fin(j, carry):
        t0 = pl.multiple_of(j * tb, tb)
        hs = hf_ref[pl.ds(t0, tb), :] + hb_ref[pl.ds(t0, tb), :]
        mu = jnp.sum(hs, axis=1, keepdims=True) * (1.0 / ML_DH)
        dev = jnp.where(real, hs - mu, 0.0)
        var = jnp.sum(dev * dev, axis=1, keepdims=True) * (1.0 / ML_DH)
        hn = dev * lax.rsqrt(var + LN_EPS) * ng_ref[...]
        xc = xc_ref[0, pl.ds(t0, tb), :].astype(F32)
        z = z_ref[0, pl.ds(t0, tb), :].astype(F32)
        y_ref[0, pl.ds(t0, tb), :] = ((hn + sk_ref[...] * xc) * _silu(z)).astype(BF16)
        return carry

    lax.fori_loop(0, s // tb, fin, 0)


def _mlstm(q, k, v, gcol3, grow, gbc, gbr, main3, xc, ng, sk):
    b, s, _ = q.shape
    tok = pl.BlockSpec((1, s, ML_DHP), lambda i, h: (i, 0, h))
    vec = pl.BlockSpec((1, ML_DHP), lambda i, h: (0, h))
    return pl.pallas_call(
        functools.partial(_mlstm_kernel, s=s),
        grid=(b, ML_HEADS),
        in_specs=[tok, tok, tok,
                  pl.BlockSpec((1, s, LANES), lambda i, h: (i, 0, 0)),
                  pl.BlockSpec((LANES, s), lambda i, h: (0, i)),
                  pl.BlockSpec((1, LANES), lambda i, h: (0, 0)),
                  pl.BlockSpec((LANES, 1), lambda i, h: (0, 0)),
                  pl.BlockSpec((1, s, ML_DHP), lambda i, h: (i, 0, ML_HEADS + h)),
                  tok, vec, vec],
        out_specs=tok,
        out_shape=jax.ShapeDtypeStruct((b, s, ML_WP), BF16),
        scratch_shapes=[pltpu.VMEM((s, ML_DHP), F32), pltpu.VMEM((s, ML_DHP), F32),
                        pltpu.VMEM((ML_DHP, ML_DHP), F32), pltpu.VMEM((ML_DHP, ML_DHP), F32),
                        pltpu.VMEM((1, ML_DHP), F32), pltpu.VMEM((1, ML_DHP), F32),
                        pltpu.VMEM((1, 1), F32), pltpu.VMEM((1, 1), F32)],
        compiler_params=_cparams("parallel", "parallel"),
        name="mlstm",
    )(q, k, v, gcol3, grow, gbc, gbr, main3, xc, ng, sk)


def _pad_heads(a, axis):
    a = jnp.moveaxis(a, axis, -1)
    lead = a.shape[:-1]
    a = a.reshape(lead + (ML_HEADS, ML_DH))
    a = jnp.pad(a, [(0, 0)] * len(lead) + [(0, 0), (0, ML_DHP - ML_DH)])
    return jnp.moveaxis(a.reshape(lead + (ML_WP,)), -1, axis)


def kernel(x, mem, mem_ln_g, mem_ln_b, w_mem_kv, router_w, router_b, na_w_in, na_rpb, ml_w_in, ml_conv_w,
           ml_conv_b, ml_w_qkv, ml_gate_b, ml_norm_g, ml_skip, w_out, ln_g, ln_b, exp_w_gate, exp_w_up,
           exp_w_down):
    b, s, d = x.shape
    n = b * s
    nm = mem.shape[1]
    row = lambda a: a.reshape(1, -1)

    mem_k, mem_v = _memkv(mem.reshape(b * nm, d), row(mem_ln_g), row(mem_ln_b), w_mem_kv.astype(BF16))
    mem_k3 = mem_k.reshape(b, nm, MEM_W)
    mem_v3 = mem_v.reshape(b, nm, MEM_W)
    rwt = router_w.T
    rb = router_b.reshape(N_EXPERTS, 1)

    x2 = x.reshape(n, d)

    h0 = _proj(x2, na_w_in[0].astype(BF16)).reshape(b, s, 3 * NA_W + MEM_W)
    y_na = _na_attention(h0, _na_bias_table(na_rpb[0]))
    y_mem = _mem_attention(h0, 3 * NA_W // LANES, mem_k3, mem_v3)
    wo = w_out[0].astype(BF16)
    x2, xr = _outproj_ln(y_na.reshape(n, NA_W), y_mem.reshape(n, MEM_W), wo[:NA_W], wo[NA_W:], x2,
                         row(ln_g[0, 0]), row(ln_b[0, 0]))
    x2 = _moe_ln(x2, xr, rwt, rb, exp_w_gate[0].astype(BF16), exp_w_up[0].astype(BF16),
                 exp_w_down[0].astype(BF16), row(ln_g[0, 1]), row(ln_b[0, 1]))

    w1 = ml_w_in[0]
    w_main = jnp.concatenate([_pad_heads(w1[:, :ML_W], 1), _pad_heads(w1[:, ML_W:2 * ML_W], 1),
                              w1[:, 2 * ML_W + 4 * ML_HEADS:]], axis=1).astype(BF16)
    w_g = jnp.pad(w1[:, 2 * ML_W:2 * ML_W + 4 * ML_HEADS], ((0, 0), (0, LANES - 4 * ML_HEADS))).astype(BF16)
    main, gcol, grow = _proj_gates(x2, w_main, w_g, w_g.T)
    main3 = main.reshape(b, s, 2 * ML_WP + MEM_W)
    wqkv = jnp.pad(ml_w_qkv[0], ((0, 0), (0, 0), (0, ML_DHP - ML_DH), (0, ML_DHP - ML_DH))).astype(BF16)
    q, k, v, xc = _conv_qkv(main3, _pad_heads(ml_conv_w[0], 1), _pad_heads(row(ml_conv_b[0]), 1),
                            wqkv[0], wqkv[1], wqkv[2])
    gb = jnp.pad(ml_gate_b[0].reshape(4 * ML_HEADS), (0, LANES - 4 * ML_HEADS))
    y_ml = _mlstm(q, k, v, gcol.reshape(b, s, LANES), grow, gb.reshape(1, LANES), gb.reshape(LANES, 1), main3, xc,
                  _pad_heads(row(ml_norm_g[0]), 1), _pad_heads(row(ml_skip[0]), 1))
    y_mem = _mem_attention(main3, 2 * ML_WP // LANES, mem_k3, mem_v3)
    wo = w_out[1]
    x2, xr = _outproj_ln(y_ml.reshape(n, ML_WP), y_mem.reshape(n, MEM_W), _pad_heads(wo[:ML_W], 0).astype(BF16),
                         wo[ML_W:].astype(BF16), x2, row(ln_g[1, 0]), row(ln_b[1, 0]))
    x2 = _moe_ln(x2, xr, rwt, rb, exp_w_gate[1].astype(BF16), exp_w_up[1].astype(BF16),
                 exp_w_down[1].astype(BF16), row(ln_g[1, 1]), row(ln_b[1, 1]))
    return x2.reshape(b, s, d)
```

```python
import functools

import numpy as np
import jax
import jax.numpy as jnp
from jax import lax
from jax.experimental import pallas as pl
from jax.experimental.pallas import tpu as pltpu

F32 = jnp.float32
BF16 = jnp.bfloat16
I32 = jnp.int32

D_MODEL = 1024
DEPTH = 2
GRID_W = 64
MEM_HEADS = 4
MEM_DH = 64
MEM_W = MEM_HEADS * MEM_DH
NA_HEADS = 12
NA_DH = 64
NA_W = NA_HEADS * NA_DH
WIN_H = 8
WIN_W = 16
ML_HEADS = 4
ML_DH = 192
ML_DHP = 256
ML_W = ML_HEADS * ML_DH
ML_WP = ML_HEADS * ML_DHP
CONV_K = 5
CHUNK = 128
N_EXPERTS = 16
N_GROUPS = 4
EXPERTS_PER_GROUP = N_EXPERTS // N_GROUPS
D_EXPERT = 512
ALPHA = (2 * DEPTH) ** 0.25
LN_EPS = 1e-5
NEG = -1e30

LANES = 128
SUBLANES = 8
ROW_CHUNKS = D_MODEL // LANES
MOE_BM = 256
NA_ROWS_PER_STEP = 4
VMEM_LIMIT = 48 * 1024 * 1024


def _cparams(*sem):
    return pltpu.CompilerParams(dimension_semantics=sem, vmem_limit_bytes=VMEM_LIMIT)


def _dot(a, b):
    return jnp.dot(a, b, preferred_element_type=F32)


def _dot_nt(a, b, precision=None):
    return lax.dot_general(a, b, (((1,), (1,)), ((), ())), precision=precision,
                           preferred_element_type=F32)


def _ln(z, g, b):
    mu = jnp.mean(z, axis=-1, keepdims=True)
    zc = z - mu
    var = jnp.mean(zc * zc, axis=-1, keepdims=True)
    return zc * lax.rsqrt(var + LN_EPS) * g + b


def _silu(x):
    return x * jax.nn.sigmoid(x)


def _read_rows(ref, n):
    return jnp.concatenate([ref[pl.ds(j, n, stride=ROW_CHUNKS), :] for j in range(ROW_CHUNKS)], axis=1)


def _write_rows(ref, val, n):
    for j in range(ROW_CHUNKS):
        ref[pl.ds(j, n, stride=ROW_CHUNKS), :] = val[:, j * LANES:(j + 1) * LANES]


def _memkv_kernel(m_ref, g_ref, b_ref, w_ref, k_ref, v_ref):
    z = _ln(m_ref[...], g_ref[...], b_ref[...])
    kv = _dot(z.astype(BF16), w_ref[...])
    k_ref[...] = kv[:, :MEM_W].astype(BF16)
    v_ref[...] = kv[:, MEM_W:].astype(BF16)


def _memkv(mem2, g, b, w):
    n = mem2.shape[0]
    tm = 256
    return pl.pallas_call(
        _memkv_kernel,
        grid=(n // tm,),
        in_specs=[pl.BlockSpec((tm, D_MODEL), lambda i: (i, 0)),
                  pl.BlockSpec((1, D_MODEL), lambda i: (0, 0)),
                  pl.BlockSpec((1, D_MODEL), lambda i: (0, 0)),
                  pl.BlockSpec((D_MODEL, 2 * MEM_W), lambda i: (0, 0))],
        out_specs=[pl.BlockSpec((tm, MEM_W), lambda i: (i, 0)),
                   pl.BlockSpec((tm, MEM_W), lambda i: (i, 0))],
        out_shape=[jax.ShapeDtypeStruct((n, MEM_W), BF16)] * 2,
        compiler_params=_cparams("parallel"),
        name="memkv",
    )(mem2, g, b, w)


def _proj_kernel(x_ref, w_ref, o_ref):
    o_ref[...] = _dot(x_ref[...].astype(BF16), w_ref[...]).astype(o_ref.dtype)


def _proj(x2, w, tm=512):
    n, k = x2.shape
    nout = w.shape[1]
    return pl.pallas_call(
        _proj_kernel,
        grid=(n // tm,),
        in_specs=[pl.BlockSpec((tm, k), lambda i: (i, 0)),
                  pl.BlockSpec((k, nout), lambda i: (0, 0))],
        out_specs=pl.BlockSpec((tm, nout), lambda i: (i, 0)),
        out_shape=jax.ShapeDtypeStruct((n, nout), BF16),
        compiler_params=_cparams("parallel"),
        name="in_proj",
    )(x2, w)


def _proj_gates_kernel(x_ref, w_ref, wg_ref, wgt_ref, o_ref, g_ref, gt_ref):
    xb = x_ref[...].astype(BF16)
    o_ref[...] = _dot(xb, w_ref[...]).astype(BF16)
    g_ref[...] = _dot(xb, wg_ref[...])
    gt_ref[...] = _dot_nt(wgt_ref[...], xb)


def _proj_gates(x2, w, wg, wgt, tm=512):
    n, k = x2.shape
    nout = w.shape[1]
    return pl.pallas_call(
        _proj_gates_kernel,
        grid=(n // tm,),
        in_specs=[pl.BlockSpec((tm, k), lambda i: (i, 0)),
                  pl.BlockSpec((k, nout), lambda i: (0, 0)),
                  pl.BlockSpec((k, LANES), lambda i: (0, 0)),
                  pl.BlockSpec((LANES, k), lambda i: (0, 0))],
        out_specs=[pl.BlockSpec((tm, nout), lambda i: (i, 0)),
                   pl.BlockSpec((tm, LANES), lambda i: (i, 0)),
                   pl.BlockSpec((LANES, tm), lambda i: (0, i))],
        out_shape=[jax.ShapeDtypeStruct((n, nout), BF16),
                   jax.ShapeDtypeStruct((n, LANES), F32),
                   jax.ShapeDtypeStruct((LANES, n), F32)],
        compiler_params=_cparams("parallel"),
        name="in_proj_gates",
    )(x2, w, wg, wgt)


def _softmax_pv(s, v):
    m = jnp.max(s, axis=-1, keepdims=True)
    p = jnp.exp(s - m)
    l = jnp.sum(p, axis=-1, keepdims=True)
    return _dot(p.astype(BF16), v) / l


def _na_kernel(q_ref, k_ref, v_ref, tbl_ref, o_ref, *, rows):
    lane = lax.broadcasted_iota(I32, (1, LANES), 1)
    first = lane < NA_DH
    nkeys = WIN_H * GRID_W

    def rows_step(i, carry):
        rr = [i * NA_ROWS_PER_STEP + u for u in range(NA_ROWS_PER_STEP)]
        rss = [jnp.clip(r - WIN_H // 2, 0, rows - WIN_H) for r in rr]
        scores = []
        for r, rs in zip(rr, rss):
            q = q_ref[0, pl.ds(pl.multiple_of(r * GRID_W, GRID_W), GRID_W), :]
            q = q * jnp.asarray(NA_DH ** -0.5, BF16)
            q2 = jnp.concatenate([jnp.where(first, q, jnp.zeros_like(q)),
                                  jnp.where(first, jnp.zeros_like(q), q)], axis=0)
            k = k_ref[0, pl.ds(pl.multiple_of(rs * GRID_W, GRID_W), nkeys), :]
            dr0 = rs - r + WIN_H - 1
            bias = jnp.concatenate(
                [jnp.concatenate([tbl_ref[0, half, dr0 + 2 * m] for m in range(WIN_H // 2)], axis=1)
                 for half in range(2)], axis=0)
            scores.append(_dot_nt(q2, k) + bias)
        probs = []
        for s in scores:
            p = jnp.exp(s - jnp.max(s, axis=-1, keepdims=True))
            probs.append((p.astype(BF16), jnp.sum(p, axis=-1, keepdims=True)))
        for r, rs, (p, l) in zip(rr, rss, probs):
            v = v_ref[0, pl.ds(pl.multiple_of(rs * GRID_W, GRID_W), nkeys), :]
            o = _dot(p, v) / l
            o = jnp.where(first, o[:GRID_W], o[GRID_W:])
            o_ref[0, pl.ds(pl.multiple_of(r * GRID_W, GRID_W), GRID_W), :] = o.astype(o_ref.dtype)
        return carry

    lax.fori_loop(0, rows // NA_ROWS_PER_STEP, rows_step, 0)


def _na_bias_table(rpb):
    qc = np.arange(GRID_W)[:, None]
    kc = np.arange(GRID_W)[None, :]
    cs = np.clip(qc - WIN_W // 2, 0, GRID_W - WIN_W)
    col_in = (kc >= cs) & (kc < cs + WIN_W)
    dc = np.clip(kc - qc + WIN_W - 1, 0, 2 * WIN_W - 2)
    t = jnp.where(col_in, rpb[:, :, dc], NEG).astype(F32)
    t2 = jnp.concatenate([t[:, :-1], t[:, 1:]], axis=-1)
    return t2.reshape(NA_HEADS // 2, 2, 2 * WIN_H - 2, GRID_W, 2 * GRID_W)


def _na_attention(h3, tbl):
    b, s, _ = h3.shape
    rows = s // GRID_W
    npair = NA_HEADS // 2
    return pl.pallas_call(
        functools.partial(_na_kernel, rows=rows),
        grid=(b, npair),
        in_specs=[pl.BlockSpec((1, s, LANES), lambda i, p: (i, 0, p)),
                  pl.BlockSpec((1, s, LANES), lambda i, p: (i, 0, npair + p)),
                  pl.BlockSpec((1, s, LANES), lambda i, p: (i, 0, 2 * npair + p)),
                  pl.BlockSpec((1, 2, 2 * WIN_H - 2, GRID_W, 2 * GRID_W), lambda i, p: (p, 0, 0, 0, 0))],
        out_specs=pl.BlockSpec((1, s, LANES), lambda i, p: (i, 0, p)),
        out_shape=jax.ShapeDtypeStruct((b, s, NA_W), BF16),
        compiler_params=_cparams("parallel", "parallel"),
        name="na_attention",
    )(h3, h3, h3, tbl)


def _mem_attn_kernel(q_ref, k_ref, v_ref, o_ref):
    lane = lax.broadcasted_iota(I32, (1, LANES), 1)
    first = lane < MEM_DH
    q = q_ref[0] * jnp.asarray(MEM_DH ** -0.5, BF16)
    k = k_ref[0]
    v = v_ref[0]
    outs = []
    for half in range(2):
        qh = jnp.where(first if half == 0 else jnp.logical_not(first), q, jnp.zeros_like(q))
        outs.append(_softmax_pv(_dot_nt(qh, k), v))
    o_ref[0] = jnp.where(first, outs[0], outs[1]).astype(o_ref.dtype)


def _mem_attention(h3, col_block0, mem_k3, mem_v3, tq=512):
    b, s, _ = h3.shape
    nm = mem_k3.shape[1]
    npair = MEM_HEADS // 2
    return pl.pallas_call(
        _mem_attn_kernel,
        grid=(b, npair, s // tq),
        in_specs=[pl.BlockSpec((1, tq, LANES), lambda i, p, t: (i, t, col_block0 + p)),
                  pl.BlockSpec((1, nm, LANES), lambda i, p, t: (i, 0, p)),
                  pl.BlockSpec((1, nm, LANES), lambda i, p, t: (i, 0, p))],
        out_specs=pl.BlockSpec((1, tq, LANES), lambda i, p, t: (i, t, p)),
        out_shape=jax.ShapeDtypeStruct((b, s, MEM_W), BF16),
        compiler_params=_cparams("parallel", "parallel", "parallel"),
        name="mem_attention",
    )(h3, mem_k3, mem_v3)


def _outproj_ln_kernel(ya_ref, ym_ref, wa_ref, wm_ref, x_ref, g_ref, b_ref, o_ref, or_ref, *, tm):
    acc = _dot(ya_ref[...], wa_ref[...]) + _dot(ym_ref[...], wm_ref[...])
    out = _ln(ALPHA * x_ref[...] + acc, g_ref[...], b_ref[...])
    o_ref[...] = out
    _write_rows(or_ref, out, tm)


def _outproj_ln(ya, ym, wa, wm, x2, g, b, tm=512):
    n = x2.shape[0]
    ka, km = ya.shape[1], ym.shape[1]
    return pl.pallas_call(
        functools.partial(_outproj_ln_kernel, tm=tm),
        grid=(n // tm,),
        in_specs=[pl.BlockSpec((tm, ka), lambda i: (i, 0)),
                  pl.BlockSpec((tm, km), lambda i: (i, 0)),
                  pl.BlockSpec((ka, D_MODEL), lambda i: (0, 0)),
                  pl.BlockSpec((km, D_MODEL), lambda i: (0, 0)),
                  pl.BlockSpec((tm, D_MODEL), lambda i: (i, 0)),
                  pl.BlockSpec((1, D_MODEL), lambda i: (0, 0)),
                  pl.BlockSpec((1, D_MODEL), lambda i: (0, 0))],
        out_specs=[pl.BlockSpec((tm, D_MODEL), lambda i: (i, 0)),
                   pl.BlockSpec((tm * ROW_CHUNKS, LANES), lambda i: (i, 0))],
        out_shape=[jax.ShapeDtypeStruct((n, D_MODEL), F32),
                   jax.ShapeDtypeStruct((n * ROW_CHUNKS, LANES), F32)],
        compiler_params=_cparams("parallel"),
        name="outproj_ln",
    )(ya, ym, wa, wm, x2, g, b)


def _router_kernel(x_ref, rwt_ref, rb_ref, idx_ref, rank_ref, w_ref, cnt_ref, *, tm):
    @pl.when(pl.program_id(0) == 0)
    def _():
        cnt_ref[...] = jnp.zeros_like(cnt_ref)

    logits = _dot_nt(rwt_ref[...], x_ref[...], precision=lax.Precision.HIGHEST)
    scores = jax.nn.sigmoid(logits)
    biased = scores + rb_ref[...]
    bv = [biased[e:e + 1, :] for e in range(N_EXPERTS)]
    sv = [scores[e:e + 1, :] for e in range(N_EXPERTS)]

    grp = []
    for g in range(N_GROUPS):
        m = bv[g * EXPERTS_PER_GROUP:(g + 1) * EXPERTS_PER_GROUP]
        best = None
        for a in range(EXPERTS_PER_GROUP):
            for c in range(a + 1, EXPERTS_PER_GROUP):
                pair = m[a] + m[c]
                best = pair if best is None else jnp.maximum(best, pair)
        grp.append(best)
    gsel = jnp.zeros((1, tm), I32)
    gbest = grp[0]
    for g in range(1, N_GROUPS):
        better = grp[g] > gbest
        gsel = jnp.where(better, g, gsel)
        gbest = jnp.where(better, grp[g], gbest)

    def pick(vals, j):
        out = vals[j]
        for g in range(1, N_GROUPS):
            out = jnp.where(gsel == g, vals[g * EXPERTS_PER_GROUP + j], out)
        return out

    cb = [pick(bv, j) for j in range(EXPERTS_PER_GROUP)]
    cs = [pick(sv, j) for j in range(EXPERTS_PER_GROUP)]
    i1 = jnp.zeros((1, tm), I32)
    m1 = cb[0]
    s1 = cs[0]
    for j in range(1, EXPERTS_PER_GROUP):
        gt = cb[j] > m1
        i1 = jnp.where(gt, j, i1)
        m1 = jnp.where(gt, cb[j], m1)
        s1 = jnp.where(gt, cs[j], s1)
    i2 = jnp.zeros((1, tm), I32)
    m2 = jnp.full((1, tm), -jnp.inf, F32)
    s2 = jnp.zeros((1, tm), F32)
    for j in range(EXPERTS_PER_GROUP):
        ok = jnp.logical_and(i1 != j, cb[j] > m2)
        i2 = jnp.where(ok, j, i2)
        m2 = jnp.where(ok, cb[j], m2)
        s2 = jnp.where(ok, cs[j], s2)
    e1 = gsel * EXPERTS_PER_GROUP + i1
    e2 = gsel * EXPERTS_PER_GROUP + i2
    tot = s1 + s2
    idx_ref[...] = jnp.concatenate([e1, e2], axis=0)
    w_ref[...] = jnp.concatenate([s1 / tot, s2 / tot], axis=0)

    eio = lax.broadcasted_iota(I32, (N_EXPERTS, tm), 0)
    oh1 = eio == e1
    oh2 = eio == e2
    ohs = jnp.logical_or(oh1, oh2).astype(F32)
    before = (lax.broadcasted_iota(I32, (tm, tm), 0) < lax.broadcasted_iota(I32, (tm, tm), 1))
    pre = _dot(ohs.astype(BF16), before.astype(BF16))
    pos = cnt_ref[:, 0:1] + pre
    r1 = jnp.sum(jnp.where(oh1, pos, 0.0), axis=0, keepdims=True)
    r2 = jnp.sum(jnp.where(oh2, pos, 0.0), axis=0, keepdims=True)
    rank_ref[...] = jnp.concatenate([r1, r2], axis=0).astype(I32)
    cnt_ref[...] += jnp.sum(ohs, axis=1, keepdims=True)


def _router(x2, rwt, rb, tm=512):
    n = x2.shape[0]
    return pl.pallas_call(
        functools.partial(_router_kernel, tm=tm),
        grid=(n // tm,),
        in_specs=[pl.BlockSpec((tm, D_MODEL), lambda i: (i, 0)),
                  pl.BlockSpec((N_EXPERTS, D_MODEL), lambda i: (0, 0)),
                  pl.BlockSpec((N_EXPERTS, 1), lambda i: (0, 0))],
        out_specs=[pl.BlockSpec((2, tm), lambda i: (0, i)),
                   pl.BlockSpec((2, tm), lambda i: (0, i)),
                   pl.BlockSpec((2, tm), lambda i: (0, i)),
                   pl.BlockSpec((N_EXPERTS, LANES), lambda i: (0, 0))],
        out_shape=[jax.ShapeDtypeStruct((2, n), I32),
                   jax.ShapeDtypeStruct((2, n), I32),
                   jax.ShapeDtypeStruct((2, n), F32),
                   jax.ShapeDtypeStruct((N_EXPERTS, LANES), F32)],
        compiler_params=_cparams("arbitrary"),
        name="router",
    )(x2, rwt, rb)


def _plan_kernel(cnt_ref, idx_ref, rank_ref, dest_ref, meta_ref, *, tm, nbl):
    shift = MOE_BM.bit_length() - 1
    cnt = cnt_ref[...].astype(I32)
    padded = ((cnt + (MOE_BM - 1)) >> shift) << shift
    starts = []
    acc = jnp.zeros((1, LANES), I32)
    for e in range(N_EXPERTS):
        starts.append(acc)
        acc = acc + padded[e:e + 1, :]
    pad_start = jnp.concatenate(starts, axis=0)
    pad_end = pad_start + padded
    eio = lax.broadcasted_iota(I32, (N_EXPERTS, tm), 0)
    idx = idx_ref[...]
    rank = rank_ref[...]
    dests = []
    for k in range(2):
        oh = eio == idx[k:k + 1, :]
        dests.append(jnp.sum(jnp.where(oh, pad_start[:, 0:1], 0), axis=0, keepdims=True) + rank[k:k + 1, :])
    dest_ref[...] = jnp.concatenate(dests, axis=0)
    blk0 = lax.broadcasted_iota(I32, (N_EXPERTS, nbl), 1) * MOE_BM
    block_e = jnp.sum((pad_end[:, 0:1] <= blk0).astype(I32), axis=0, keepdims=True)
    block_e = jnp.minimum(block_e, N_EXPERTS - 1)
    n_used = jnp.broadcast_to(acc[:, 0:1] >> shift, (1, nbl))
    diag = lax.broadcasted_iota(I32, (N_EXPERTS, nbl), 0) == lax.broadcasted_iota(I32, (N_EXPERTS, nbl), 1)
    fill_lo = jnp.sum(jnp.where(diag, (pad_start + cnt)[:, 0:1], 0), axis=0, keepdims=True)
    fill_hi = jnp.sum(jnp.where(diag, pad_end[:, 0:1], 0), axis=0, keepdims=True)
    meta_ref[...] = jnp.concatenate([block_e, n_used, fill_lo, fill_hi, jnp.zeros((SUBLANES - 4, nbl), I32)],
                                    axis=0)


def _plan(cnt, idx, rank, n_blocks, tm=2048):
    n = idx.shape[1]
    tm = min(tm, n)
    nbl = -(-n_blocks // LANES) * LANES
    return pl.pallas_call(
        functools.partial(_plan_kernel, tm=tm, nbl=nbl),
        grid=(n // tm,),
        in_specs=[pl.BlockSpec((N_EXPERTS, LANES), lambda i: (0, 0)),
                  pl.BlockSpec((2, tm), lambda i: (0, i)),
                  pl.BlockSpec((2, tm), lambda i: (0, i))],
        out_specs=[pl.BlockSpec((2, tm), lambda i: (0, i)),
                   pl.BlockSpec((SUBLANES, nbl), lambda i: (0, 0))],
        out_shape=[jax.ShapeDtypeStruct((2, n), I32),
                   jax.ShapeDtypeStruct((SUBLANES, nbl), I32)],
        compiler_params=_cparams("arbitrary"),
        name="moe_plan",
    )(cnt, idx, rank)


def _row_copy(src_hbm, src_row, dst, dst_row, sem):
    return pltpu.make_async_copy(
        src_hbm.at[pl.ds(pl.multiple_of(src_row * ROW_CHUNKS, ROW_CHUNKS), ROW_CHUNKS), :],
        dst.at[pl.ds(pl.multiple_of(dst_row * ROW_CHUNKS, ROW_CHUNKS), ROW_CHUNKS), :],
        sem)


def _rows_wait(src_hbm, buf, sem):
    pltpu.make_async_copy(src_hbm.at[pl.ds(0, buf.shape[0]), :], buf, sem).wait()


def _invert_kernel(dest_ref, lo_ref, hi_ref, src_ref, *, n, n_rows):
    unroll = 8

    def scatter(i, carry):
        for u in range(unroll):
            t = i * unroll + u
            src_ref[dest_ref[t]] = t
            src_ref[dest_ref[n + t]] = t
        return carry

    lax.fori_loop(0, n // unroll, scatter, 0)

    def fill(slot, carry):
        src_ref[slot] = 0
        return carry

    for e in range(N_EXPERTS):
        lax.fori_loop(lo_ref[e], hi_ref[e], fill, 0)
    lax.fori_loop(hi_ref[N_EXPERTS - 1], n_rows, fill, 0)


def _invert(dest_flat, fill_lo, fill_hi, n_rows):
    n = dest_flat.shape[0] // 2
    return pl.pallas_call(
        functools.partial(_invert_kernel, n=n, n_rows=n_rows),
        grid_spec=pltpu.PrefetchScalarGridSpec(
            num_scalar_prefetch=3,
            grid=(1,),
            in_specs=[],
            out_specs=pl.BlockSpec(memory_space=pltpu.SMEM)),
        out_shape=jax.ShapeDtypeStruct((n_rows,), I32),
        compiler_params=_cparams("arbitrary"),
        name="moe_invert",
    )(dest_flat, fill_lo, fill_hi)


def _experts_kernel(be_ref, nu_ref, src_ref, x_hbm, wg_ref, wu_ref, wd_ref, y_ref, xb0, xb1, sem):
    del be_ref
    j = pl.program_id(0)
    n_used = nu_ref[0]
    used = j < n_used
    bufs = (xb0, xb1)

    def issue(blk, slot):
        base = blk * MOE_BM
        for r in range(MOE_BM):
            _row_copy(x_hbm, src_ref[base + r], bufs[slot], r, sem.at[slot]).start()

    @pl.when(j == 0)
    def _():
        issue(0, 0)

    for slot in range(2):
        @pl.when(jnp.logical_and(used, j % 2 == slot))
        def _():
            _rows_wait(x_hbm, bufs[slot], sem.at[slot])
            x = _read_rows(bufs[slot], MOE_BM).astype(BF16)
            issue(jnp.minimum(j + 1, n_used - 1), 1 - slot)
            h = _silu(_dot(x, wg_ref[0])) * _dot(x, wu_ref[0])
            _write_rows(y_ref, _dot(h.astype(BF16), wd_ref[0]), MOE_BM)

    @pl.when(j == n_used - 1)
    def _():
        _rows_wait(x_hbm, xb0, sem.at[(j + 1) % 2])

    @pl.when(jnp.logical_not(used))
    def _():
        y_ref[...] = jnp.zeros_like(y_ref)


def _experts(block_e, n_used, src, xr, wg, wu, wd):
    n_blocks = block_e.shape[0]

    def wblk(j, be, nu, sr):
        return (be[jnp.minimum(j, nu[0] - 1)], 0, 0)

    return pl.pallas_call(
        _experts_kernel,
        grid_spec=pltpu.PrefetchScalarGridSpec(
            num_scalar_prefetch=3,
            grid=(n_blocks,),
            in_specs=[pl.BlockSpec(memory_space=pl.ANY),
                      pl.BlockSpec((1, D_MODEL, D_EXPERT), wblk),
                      pl.BlockSpec((1, D_MODEL, D_EXPERT), wblk),
                      pl.BlockSpec((1, D_EXPERT, D_MODEL), wblk)],
            out_specs=pl.BlockSpec((MOE_BM * ROW_CHUNKS, LANES), lambda j, be, nu, sr: (j, 0)),
            scratch_shapes=[pltpu.VMEM((MOE_BM * ROW_CHUNKS, LANES), F32),
                            pltpu.VMEM((MOE_BM * ROW_CHUNKS, LANES), F32),
                            pltpu.SemaphoreType.DMA((2,))]),
        out_shape=jax.ShapeDtypeStruct((n_blocks * MOE_BM * ROW_CHUNKS, LANES), F32),
        compiler_params=_cparams("arbitrary"),
        name="moe_experts",
    )(block_e, n_used, src, xr, wg, wu, wd)


def _combine_ln_kernel(dest_ref, y_hbm, x_ref, w1_ref, w2_ref, g_ref, b_ref, o_ref, a0, a1, b0, b1, sem, *, n, tm):
    i = pl.program_id(0)
    nt = pl.num_programs(0)
    abufs = (a0, a1)
    bbufs = (b0, b1)
    unroll = 8

    def issue(tile, slot):
        base = tile * tm

        def body(c, carry):
            for u in range(unroll):
                t = c * unroll + u
                _row_copy(y_hbm, dest_ref[base + t], abufs[slot], t, sem.at[slot]).start()
                _row_copy(y_hbm, dest_ref[n + base + t], bbufs[slot], t, sem.at[slot]).start()
            return carry

        lax.fori_loop(0, tm // unroll, body, 0)

    @pl.when(i == 0)
    def _():
        issue(0, 0)

    for slot in range(2):
        @pl.when(i % 2 == slot)
        def _():
            @pl.when(i + 1 < nt)
            def _():
                issue(i + 1, 1 - slot)

            _rows_wait(y_hbm, abufs[slot], sem.at[slot])
            _rows_wait(y_hbm, bbufs[slot], sem.at[slot])
            moe = w1_ref[...] * _read_rows(abufs[slot], tm) + w2_ref[...] * _read_rows(bbufs[slot], tm)
            o_ref[...] = _ln(ALPHA * x_ref[...] + moe, g_ref[...], b_ref[...])


def _combine_ln(dest_flat, y, x2, w1, w2, g, b, tm=256):
    n = x2.shape[0]
    return pl.pallas_call(
        functools.partial(_combine_ln_kernel, n=n, tm=tm),
        grid_spec=pltpu.PrefetchScalarGridSpec(
            num_scalar_prefetch=1,
            grid=(n // tm,),
            in_specs=[pl.BlockSpec(memory_space=pl.ANY),
                      pl.BlockSpec((tm, D_MODEL), lambda i, d: (i, 0)),
                      pl.BlockSpec((tm, 1), lambda i, d: (i, 0)),
                      pl.BlockSpec((tm, 1), lambda i, d: (i, 0)),
                      pl.BlockSpec((1, D_MODEL), lambda i, d: (0, 0)),
                      pl.BlockSpec((1, D_MODEL), lambda i, d: (0, 0))],
            out_specs=pl.BlockSpec((tm, D_MODEL), lambda i, d: (i, 0)),
            scratch_shapes=[pltpu.VMEM((tm * ROW_CHUNKS, LANES), F32)] * 4
                           + [pltpu.SemaphoreType.DMA((2,))]),
        out_shape=jax.ShapeDtypeStruct((n, D_MODEL), F32),
        compiler_params=_cparams("arbitrary"),
        name="moe_combine_ln",
    )(dest_flat, y, x2, w1, w2, g, b)


def _moe_ln(x2, xr, rwt, rb, wg, wu, wd, g, b):
    n = x2.shape[0]
    n_blocks = (2 * n) // MOE_BM + N_EXPERTS
    idx, rank, w, cnt = _router(x2, rwt, rb)
    dest, meta = _plan(cnt, idx, rank, n_blocks)
    dest_flat = dest.reshape(2 * n)
    block_e = meta[0, :n_blocks]
    n_used = meta[1, :1]
    src = _invert(dest_flat, meta[2, :N_EXPERTS], meta[3, :N_EXPERTS], n_blocks * MOE_BM)
    y = _experts(block_e, n_used, src, xr, wg, wu, wd)
    return _combine_ln(dest_flat, y, x2, w[0].reshape(n, 1), w[1].reshape(n, 1), g, b)


def _conv_qkv_kernel(xm_ref, cw_ref, cb_ref, wq_ref, wk_ref, wv_ref, q_ref, k_ref, v_ref, xc_ref, *, s):
    xm_b = xm_ref[0]
    xm = xm_b.astype(F32)
    cw = cw_ref[...]
    row = lax.broadcasted_iota(I32, (s, 1), 0)
    half = CONV_K // 2
    acc = cb_ref[...] + xm * cw[half:half + 1, :]
    for sh in range(1, half + 1):
        past = jnp.where(row >= sh, pltpu.roll(xm, sh, axis=0), 0.0)
        acc = acc + past * cw[half - sh:half - sh + 1, :]
        nxt = jnp.where(row < s - sh, pltpu.roll(xm, s - sh, axis=0), 0.0)
        acc = acc + nxt * cw[half + sh:half + sh + 1, :]
    xc = _silu(acc).astype(BF16)
    xc_ref[0] = xc
    q_ref[0] = _dot(xc, wq_ref[0]).astype(BF16)
    k_ref[0] = (_dot(xc, wk_ref[0]) * (ML_DH ** -0.5)).astype(BF16)
    v_ref[0] = _dot(xm_b, wv_ref[0]).astype(BF16)


def _conv_qkv(main3, cw, cb, wq, wk, wv):
    b, s, _ = main3.shape
    tok = pl.BlockSpec((1, s, ML_DHP), lambda i, h: (i, 0, h))
    wspec = pl.BlockSpec((1, ML_DHP, ML_DHP), lambda i, h: (h, 0, 0))
    return pl.pallas_call(
        functools.partial(_conv_qkv_kernel, s=s),
        grid=(b, ML_HEADS),
        in_specs=[tok,
                  pl.BlockSpec((CONV_K, ML_DHP), lambda i, h: (0, h)),
                  pl.BlockSpec((1, ML_DHP), lambda i, h: (0, h)),
                  wspec, wspec, wspec],
        out_specs=[tok] * 4,
        out_shape=[jax.ShapeDtypeStruct((b, s, ML_WP), BF16)] * 4,
        compiler_params=_cparams("parallel", "parallel"),
        name="conv_qkv",
    )(main3, cw, cb, wq, wk, wv)


def _split3(x):
    hi = x.astype(BF16)
    r1 = x - hi.astype(F32)
    mid = r1.astype(BF16)
    lo = (r1 - mid.astype(F32)).astype(BF16)
    return hi, mid, lo


def _gates_kernel(gc_ref, gr_ref, gbc_ref, gbr_ref, ac_ref, ar_ref, *, s):
    lane = lax.broadcasted_iota(I32, (1, LANES), 1)
    sub = lax.broadcasted_iota(I32, (LANES, 1), 0)
    ti = lax.broadcasted_iota(I32, (CHUNK, CHUNK), 0)
    tj = lax.broadcasted_iota(I32, (CHUNK, CHUNK), 1)
    lower = (tj <= ti).astype(BF16)
    upper = (ti <= tj).astype(BF16)

    def pick(idx, pre, suf, raw):
        fwd = jnp.logical_and(idx >= ML_HEADS, idx < 2 * ML_HEADS)
        bwd = jnp.logical_and(idx >= 3 * ML_HEADS, idx < 4 * ML_HEADS)
        return jnp.where(fwd, pre, jnp.where(bwd, suf, raw))

    def body(c, carry):
        t0 = pl.multiple_of(c * CHUNK, CHUNK)
        g = gc_ref[0, pl.ds(t0, CHUNK), :] + gbc_ref[...]
        ls = jax.nn.log_sigmoid(g)
        pre = sum(_dot(lower, part) for part in _split3(ls))
        suf = jnp.sum(ls, axis=0, keepdims=True) - pre + ls
        ac_ref[0, pl.ds(t0, CHUNK), :] = pick(lane, pre, suf, g)
        g = gr_ref[:, pl.ds(t0, CHUNK)] + gbr_ref[...]
        ls = jax.nn.log_sigmoid(g)
        pre = sum(_dot(part, upper) for part in _split3(ls))
        suf = jnp.sum(ls, axis=1, keepdims=True) - pre + ls
        ar_ref[:, pl.ds(t0, CHUNK)] = pick(sub, pre, suf, g)
        return carry

    lax.fori_loop(0, s // CHUNK, body, 0)


def _gates(gcol3, grow, gbc, gbr):
    b, s, _ = gcol3.shape
    return pl.pallas_call(
        functools.partial(_gates_kernel, s=s),
        grid=(b,),
        in_specs=[pl.BlockSpec((1, s, LANES), lambda i: (i, 0, 0)),
                  pl.BlockSpec((LANES, s), lambda i: (0, i)),
                  pl.BlockSpec((1, LANES), lambda i: (0, 0)),
                  pl.BlockSpec((LANES, 1), lambda i: (0, 0))],
        out_specs=[pl.BlockSpec((1, s, LANES), lambda i: (i, 0, 0)),
                   pl.BlockSpec((LANES, s), lambda i: (0, i))],
        out_shape=[jax.ShapeDtypeStruct((b, s, LANES), F32),
                   jax.ShapeDtypeStruct((LANES, b * s), F32)],
        compiler_params=_cparams("parallel"),
        name="mlstm_gates",
    )(gcol3, grow, gbc, gbr)


def _mlstm_kernel(q_ref, k_ref, v_ref, gc_ref, gr_ref, z_ref, xc_ref, ng_ref, sk_ref,
                  y_ref, hf_ref, hb_ref, cf_ref, cb_ref, nf_ref, nb_ref, mf_ref, mb_ref, *, s):
    head = pl.program_id(1)
    nc = s // CHUNK
    lane = lax.broadcasted_iota(I32, (1, LANES), 1)
    sub = lax.broadcasted_iota(I32, (LANES, 1), 0)
    ti = lax.broadcasted_iota(I32, (CHUNK, CHUNK), 0)
    tj = lax.broadcasted_iota(I32, (CHUNK, CHUNK), 1)

    for ref in (cf_ref, cb_ref, nf_ref, nb_ref, mf_ref, mb_ref):
        ref[...] = jnp.zeros_like(ref)

    def intra(c, rev):
        t0 = pl.multiple_of(c * CHUNK, CHUNK)
        qb = q_ref[0, pl.ds(t0, CHUNK), :]
        kb = k_ref[0, pl.ds(t0, CHUNK), :]
        vb = v_ref[0, pl.ds(t0, CHUNK), :]
        gc = gc_ref[0, pl.ds(t0, CHUNK), :]
        gr = gr_ref[:, pl.ds(t0, CHUNK)]
        i_col = head + (2 * ML_HEADS if rev else 0)
        f_col = i_col + ML_HEADS
        allowed = (tj >= ti) if rev else (tj <= ti)
        b_col = jnp.sum(jnp.where(lane == f_col, gc, 0.0), axis=1, keepdims=True)
        i_colv = jnp.sum(jnp.where(lane == i_col, gc, 0.0), axis=1, keepdims=True)
        b_row = jnp.sum(jnp.where(sub == f_col, gr, 0.0), axis=0, keepdims=True)
        i_row = jnp.sum(jnp.where(sub == i_col, gr, 0.0), axis=0, keepdims=True)
        b_last = b_col[0:1, :] if rev else b_col[CHUNK - 1:CHUNK, :]

        d = jnp.where(allowed, b_col - b_row + i_row, NEG)
        m_in = jnp.max(d, axis=1, keepdims=True)
        sc = _dot_nt(qb, kb) * jnp.exp(d - m_in)
        num_in = _dot(sc.astype(BF16), vb)
        den_in = jnp.sum(sc, axis=1, keepdims=True)
        return t0, qb, kb, vb, b_col, i_colv, b_last, m_in, num_in, den_in

    def update(parts, h_ref, c_ref, n_ref, m_ref):
        t0, qb, kb, vb, b_col, i_colv, b_last, m_in, num_in, den_in = parts
        m = m_ref[...]
        cmat = c_ref[...]
        nvec = n_ref[...]
        inter = b_col + m
        m_t = jnp.maximum(m_in, inter)
        a_in = jnp.exp(m_in - m_t)
        iexp = jnp.exp(inter - m_t)
        num = a_in * num_in + iexp * _dot(qb, cmat.astype(BF16))
        den = a_in * den_in + iexp * jnp.sum(qb.astype(F32) * nvec, axis=1, keepdims=True)
        h_ref[pl.ds(t0, CHUNK), :] = num / jnp.maximum(jnp.abs(den), jnp.exp(-m_t))

        w_s = b_last - b_col + i_colv
        m_new = jnp.maximum(b_last + m, jnp.max(w_s, axis=0, keepdims=True))
        wexp = jnp.exp(w_s - m_new)
        cexp = jnp.exp(b_last + m - m_new)
        kw = kb.astype(F32) * wexp
        c_ref[...] = cexp * cmat + lax.dot_general(kw.astype(BF16), vb, (((0,), (0,)), ((), ())),
                                                   preferred_element_type=F32)
        n_ref[...] = cexp * nvec + jnp.sum(kw, axis=0, keepdims=True)
        m_ref[...] = m_new

    def step(i, carry):
        fwd = intra(i, False)
        bwd = intra(nc - 1 - i, True)
        update(fwd, hf_ref, cf_ref, nf_ref, mf_ref)
        update(bwd, hb_ref, cb_ref, nb_ref, mb_ref)
        return carry

    lax.fori_loop(0, nc, step, 0)

    real = lax.broadcasted_iota(I32, (1, ML_DHP), 1) < ML_DH
    tb = 256

    def fin(j, carry):
        t0 = pl.multiple_of(j * tb, tb)
        hs = hf_ref[pl.ds(t0, tb), :] + hb_ref[pl.ds(t0, tb), :]
        mu = jnp.sum(hs, axis=1, keepdims=True) * (1.0 / ML_DH)
        dev = jnp.where(real, hs - mu, 0.0)
        var = jnp.sum(dev * dev, axis=1, keepdims=True) * (1.0 / ML_DH)
        hn = dev * lax.rsqrt(var + LN_EPS) * ng_ref[...]
        xc = xc_ref[0, pl.ds(t0, tb), :].astype(F32)
        z = z_ref[0, pl.ds(t0, tb), :].astype(F32)
        y_ref[0, pl.ds(t0, tb), :] = ((hn + sk_ref[...] * xc) * _silu(z)).astype(BF16)
        return carry

    lax.fori_loop(0, s // tb, fin, 0)


def _mlstm(q, k, v, gcol3, grow, main3, xc, ng, sk):
    b, s, _ = q.shape
    tok = pl.BlockSpec((1, s, ML_DHP), lambda i, h: (i, 0, h))
    vec = pl.BlockSpec((1, ML_DHP), lambda i, h: (0, h))
    return pl.pallas_call(
        functools.partial(_mlstm_kernel, s=s),
        grid=(b, ML_HEADS),
        in_specs=[tok, tok, tok,
                  pl.BlockSpec((1, s, LANES), lambda i, h: (i, 0, 0)),
                  pl.BlockSpec((LANES, s), lambda i, h: (0, i)),
                  pl.BlockSpec((1, s, ML_DHP), lambda i, h: (i, 0, ML_HEADS + h)),
                  tok, vec, vec],
        out_specs=tok,
        out_shape=jax.ShapeDtypeStruct((b, s, ML_WP), BF16),
        scratch_shapes=[pltpu.VMEM((s, ML_DHP), F32), pltpu.VMEM((s, ML_DHP), F32),
                        pltpu.VMEM((ML_DHP, ML_DHP), F32), pltpu.VMEM((ML_DHP, ML_DHP), F32),
                        pltpu.VMEM((1, ML_DHP), F32), pltpu.VMEM((1, ML_DHP), F32),
                        pltpu.VMEM((1, 1), F32), pltpu.VMEM((1, 1), F32)],
        compiler_params=_cparams("parallel", "parallel"),
        name="mlstm",
    )(q, k, v, gcol3, grow, main3, xc, ng, sk)


def _pad_heads(a, axis):
    a = jnp.moveaxis(a, axis, -1)
    lead = a.shape[:-1]
    a = a.reshape(lead + (ML_HEADS, ML_DH))
    a = jnp.pad(a, [(0, 0)] * len(lead) + [(0, 0), (0, ML_DHP - ML_DH)])
    return jnp.moveaxis(a.reshape(lead + (ML_WP,)), -1, axis)


def kernel(x, mem, mem_ln_g, mem_ln_b, w_mem_kv, router_w, router_b, na_w_in, na_rpb, ml_w_in, ml_conv_w,
           ml_conv_b, ml_w_qkv, ml_gate_b, ml_norm_g, ml_skip, w_out, ln_g, ln_b, exp_w_gate, exp_w_up,
           exp_w_down):
    b, s, d = x.shape
    n = b * s
    nm = mem.shape[1]
    row = lambda a: a.reshape(1, -1)

    mem_k, mem_v = _memkv(mem.reshape(b * nm, d), row(mem_ln_g), row(mem_ln_b), w_mem_kv.astype(BF16))
    mem_k3 = mem_k.reshape(b, nm, MEM_W)
    mem_v3 = mem_v.reshape(b, nm, MEM_W)
    rwt = router_w.T
    rb = router_b.reshape(N_EXPERTS, 1)

    x2 = x.reshape(n, d)

    h0 = _proj(x2, na_w_in[0].astype(BF16)).reshape(b, s, 3 * NA_W + MEM_W)
    y_na = _na_attention(h0, _na_bias_table(na_rpb[0]))
    y_mem = _mem_attention(h0, 3 * NA_W // LANES, mem_k3, mem_v3)
    wo = w_out[0].astype(BF16)
    x2, xr = _outproj_ln(y_na.reshape(n, NA_W), y_mem.reshape(n, MEM_W), wo[:NA_W], wo[NA_W:], x2,
                         row(ln_g[0, 0]), row(ln_b[0, 0]))
    x2 = _moe_ln(x2, xr, rwt, rb, exp_w_gate[0].astype(BF16), exp_w_up[0].astype(BF16),
                 exp_w_down[0].astype(BF16), row(ln_g[0, 1]), row(ln_b[0, 1]))

    w1 = ml_w_in[0]
    w_main = jnp.concatenate([_pad_heads(w1[:, :ML_W], 1), _pad_heads(w1[:, ML_W:2 * ML_W], 1),
                              w1[:, 2 * ML_W + 4 * ML_HEADS:]], axis=1).astype(BF16)
    w_g = jnp.pad(w1[:, 2 * ML_W:2 * ML_W + 4 * ML_HEADS], ((0, 0), (0, LANES - 4 * ML_HEADS))).astype(BF16)
    main, gcol, grow = _proj_gates(x2, w_main, w_g, w_g.T)
    main3 = main.reshape(b, s, 2 * ML_WP + MEM_W)
    wqkv = jnp.pad(ml_w_qkv[0], ((0, 0), (0, 0), (0, ML_DHP - ML_DH), (0, ML_DHP - ML_DH))).astype(BF16)
    q, k, v, xc = _conv_qkv(main3, _pad_heads(ml_conv_w[0], 1), _pad_heads(row(ml_conv_b[0]), 1),
                            wqkv[0], wqkv[1], wqkv[2])
    gb = jnp.pad(ml_gate_b[0].reshape(4 * ML_HEADS), (0, LANES - 4 * ML_HEADS))
    acol, arow = _gates(gcol.reshape(b, s, LANES), grow, gb.reshape(1, LANES), gb.reshape(LANES, 1))
    y_ml = _mlstm(q, k, v, acol, arow, main3, xc,
                  _pad_heads(row(ml_norm_g[0]), 1), _pad_heads(row(ml_skip[0]), 1))
    y_mem = _mem_attention(main3, 2 * ML_WP // LANES, mem_k3, mem_v3)
    wo = w_out[1]
    x2, xr = _outproj_ln(y_ml.reshape(n, ML_WP), y_mem.reshape(n, MEM_W), _pad_heads(wo[:ML_W], 0).astype(BF16),
                         wo[ML_W:].astype(BF16), x2, row(ln_g[1, 0]), row(ln_b[1, 0]))
    x2 = _moe_ln(x2, xr, rwt, rb, exp_w_gate[1].astype(BF16), exp_w_up[1].astype(BF16),
                 exp_w_down[1].astype(BF16), row(ln_g[1, 1]), row(ln_b[1, 1]))
    return x2.reshape(b, s, d)
```

```python
import functools

import numpy as np
import jax
import jax.numpy as jnp
from jax import lax
from jax.experimental import pallas as pl
from jax.experimental.pallas import tpu as pltpu

F32 = jnp.float32
BF16 = jnp.bfloat16
I32 = jnp.int32

D_MODEL = 1024
DEPTH = 2
GRID_W = 64
MEM_HEADS = 4
MEM_DH = 64
MEM_W = MEM_HEADS * MEM_DH
NA_HEADS = 12
NA_DH = 64
NA_W = NA_HEADS * NA_DH
WIN_H = 8
WIN_W = 16
ML_HEADS = 4
ML_DH = 192
ML_DHP = 256
ML_W = ML_HEADS * ML_DH
ML_WP = ML_HEADS * ML_DHP
CONV_K = 5
CHUNK = 128
N_EXPERTS = 16
N_GROUPS = 4
EXPERTS_PER_GROUP = N_EXPERTS // N_GROUPS
D_EXPERT = 512
ALPHA = (2 * DEPTH) ** 0.25
LN_EPS = 1e-5
NEG = -1e30

LANES = 128
SUBLANES = 8
ROW_CHUNKS = D_MODEL // LANES
MOE_BM = 256
MOE_TILE = 256
NA_ROWS_PER_STEP = 4
VMEM_LIMIT = 48 * 1024 * 1024


def _cparams(*sem):
    return pltpu.CompilerParams(dimension_semantics=sem, vmem_limit_bytes=VMEM_LIMIT)


def _dot(a, b):
    return jnp.dot(a, b, preferred_element_type=F32)


def _dot_nt(a, b, precision=None):
    return lax.dot_general(a, b, (((1,), (1,)), ((), ())), precision=precision,
                           preferred_element_type=F32)


def _ln(z, g, b):
    mu = jnp.mean(z, axis=-1, keepdims=True)
    zc = z - mu
    var = jnp.mean(zc * zc, axis=-1, keepdims=True)
    return zc * lax.rsqrt(var + LN_EPS) * g + b


def _silu(x):
    return x * jax.nn.sigmoid(x)


def _read_rows(ref, n):
    return jnp.concatenate([ref[pl.ds(j, n, stride=ROW_CHUNKS), :] for j in range(ROW_CHUNKS)], axis=1)


def _write_rows(ref, val, n):
    for j in range(ROW_CHUNKS):
        ref[pl.ds(j, n, stride=ROW_CHUNKS), :] = val[:, j * LANES:(j + 1) * LANES]


def _memkv_kernel(m_ref, g_ref, b_ref, w_ref, k_ref, v_ref):
    z = _ln(m_ref[...], g_ref[...], b_ref[...])
    kv = _dot(z.astype(BF16), w_ref[...])
    k_ref[...] = kv[:, :MEM_W].astype(BF16)
    v_ref[...] = kv[:, MEM_W:].astype(BF16)


def _memkv(mem2, g, b, w):
    n = mem2.shape[0]
    tm = 256
    return pl.pallas_call(
        _memkv_kernel,
        grid=(n // tm,),
        in_specs=[pl.BlockSpec((tm, D_MODEL), lambda i: (i, 0)),
                  pl.BlockSpec((1, D_MODEL), lambda i: (0, 0)),
                  pl.BlockSpec((1, D_MODEL), lambda i: (0, 0)),
                  pl.BlockSpec((D_MODEL, 2 * MEM_W), lambda i: (0, 0))],
        out_specs=[pl.BlockSpec((tm, MEM_W), lambda i: (i, 0)),
                   pl.BlockSpec((tm, MEM_W), lambda i: (i, 0))],
        out_shape=[jax.ShapeDtypeStruct((n, MEM_W), BF16)] * 2,
        compiler_params=_cparams("parallel"),
        name="memkv",
    )(mem2, g, b, w)


def _proj_kernel(x_ref, w_ref, o_ref):
    o_ref[...] = _dot(x_ref[...].astype(BF16), w_ref[...]).astype(o_ref.dtype)


def _proj(x2, w, tm=512):
    n, k = x2.shape
    nout = w.shape[1]
    return pl.pallas_call(
        _proj_kernel,
        grid=(n // tm,),
        in_specs=[pl.BlockSpec((tm, k), lambda i: (i, 0)),
                  pl.BlockSpec((k, nout), lambda i: (0, 0))],
        out_specs=pl.BlockSpec((tm, nout), lambda i: (i, 0)),
        out_shape=jax.ShapeDtypeStruct((n, nout), BF16),
        compiler_params=_cparams("parallel"),
        name="in_proj",
    )(x2, w)


def _proj_gates_kernel(x_ref, w_ref, wg_ref, wgt_ref, o_ref, g_ref, gt_ref):
    xb = x_ref[...].astype(BF16)
    o_ref[...] = _dot(xb, w_ref[...]).astype(BF16)
    g_ref[...] = _dot(xb, wg_ref[...])
    gt_ref[...] = _dot_nt(wgt_ref[...], xb)


def _proj_gates(x2, w, wg, wgt, tm=512):
    n, k = x2.shape
    nout = w.shape[1]
    return pl.pallas_call(
        _proj_gates_kernel,
        grid=(n // tm,),
        in_specs=[pl.BlockSpec((tm, k), lambda i: (i, 0)),
                  pl.BlockSpec((k, nout), lambda i: (0, 0)),
                  pl.BlockSpec((k, LANES), lambda i: (0, 0)),
                  pl.BlockSpec((LANES, k), lambda i: (0, 0))],
        out_specs=[pl.BlockSpec((tm, nout), lambda i: (i, 0)),
                   pl.BlockSpec((tm, LANES), lambda i: (i, 0)),
                   pl.BlockSpec((LANES, tm), lambda i: (0, i))],
        out_shape=[jax.ShapeDtypeStruct((n, nout), BF16),
                   jax.ShapeDtypeStruct((n, LANES), F32),
                   jax.ShapeDtypeStruct((LANES, n), F32)],
        compiler_params=_cparams("parallel"),
        name="in_proj_gates",
    )(x2, w, wg, wgt)


def _softmax_pv(s, v):
    m = jnp.max(s, axis=-1, keepdims=True)
    p = jnp.exp(s - m)
    l = jnp.sum(p, axis=-1, keepdims=True)
    return _dot(p.astype(BF16), v) / l


def _na_kernel(q_ref, k_ref, v_ref, tbl_ref, o_ref, *, rows):
    lane = lax.broadcasted_iota(I32, (1, LANES), 1)
    first = lane < NA_DH
    nkeys = WIN_H * GRID_W

    def rows_step(i, carry):
        rr = [i * NA_ROWS_PER_STEP + u for u in range(NA_ROWS_PER_STEP)]
        rss = [jnp.clip(r - WIN_H // 2, 0, rows - WIN_H) for r in rr]
        scores = []
        for r, rs in zip(rr, rss):
            q = q_ref[0, pl.ds(pl.multiple_of(r * GRID_W, GRID_W), GRID_W), :]
            q = q * jnp.asarray(NA_DH ** -0.5, BF16)
            q2 = jnp.concatenate([jnp.where(first, q, jnp.zeros_like(q)),
                                  jnp.where(first, jnp.zeros_like(q), q)], axis=0)
            k = k_ref[0, pl.ds(pl.multiple_of(rs * GRID_W, GRID_W), nkeys), :]
            dr0 = rs - r + WIN_H - 1
            bias = jnp.concatenate(
                [jnp.concatenate([tbl_ref[0, half, dr0 + 2 * m] for m in range(WIN_H // 2)], axis=1)
                 for half in range(2)], axis=0)
            scores.append(_dot_nt(q2, k) + bias)
        probs = []
        for s in scores:
            p = jnp.exp(s - jnp.max(s, axis=-1, keepdims=True))
            probs.append((p.astype(BF16), jnp.sum(p, axis=-1, keepdims=True)))
        for r, rs, (p, l) in zip(rr, rss, probs):
            v = v_ref[0, pl.ds(pl.multiple_of(rs * GRID_W, GRID_W), nkeys), :]
            o = _dot(p, v) / l
            o = jnp.where(first, o[:GRID_W], o[GRID_W:])
            o_ref[0, pl.ds(pl.multiple_of(r * GRID_W, GRID_W), GRID_W), :] = o.astype(o_ref.dtype)
        return carry

    lax.fori_loop(0, rows // NA_ROWS_PER_STEP, rows_step, 0)


def _na_bias_table(rpb):
    qc = np.arange(GRID_W)[:, None]
    kc = np.arange(GRID_W)[None, :]
    cs = np.clip(qc - WIN_W // 2, 0, GRID_W - WIN_W)
    col_in = (kc >= cs) & (kc < cs + WIN_W)
    dc = np.clip(kc - qc + WIN_W - 1, 0, 2 * WIN_W - 2)
    t = jnp.where(col_in, rpb[:, :, dc], NEG).astype(F32)
    t2 = jnp.concatenate([t[:, :-1], t[:, 1:]], axis=-1)
    return t2.reshape(NA_HEADS // 2, 2, 2 * WIN_H - 2, GRID_W, 2 * GRID_W)


def _na_attention(h3, tbl):
    b, s, _ = h3.shape
    rows = s // GRID_W
    npair = NA_HEADS // 2
    return pl.pallas_call(
        functools.partial(_na_kernel, rows=rows),
        grid=(b, npair),
        in_specs=[pl.BlockSpec((1, s, LANES), lambda i, p: (i, 0, p)),
                  pl.BlockSpec((1, s, LANES), lambda i, p: (i, 0, npair + p)),
                  pl.BlockSpec((1, s, LANES), lambda i, p: (i, 0, 2 * npair + p)),
                  pl.BlockSpec((1, 2, 2 * WIN_H - 2, GRID_W, 2 * GRID_W), lambda i, p: (p, 0, 0, 0, 0))],
        out_specs=pl.BlockSpec((1, s, LANES), lambda i, p: (i, 0, p)),
        out_shape=jax.ShapeDtypeStruct((b, s, NA_W), BF16),
        compiler_params=_cparams("parallel", "parallel"),
        name="na_attention",
    )(h3, h3, h3, tbl)


def _mem_attn_kernel(q_ref, k_ref, v_ref, o_ref):
    lane = lax.broadcasted_iota(I32, (1, LANES), 1)
    first = lane < MEM_DH
    q = q_ref[0] * jnp.asarray(MEM_DH ** -0.5, BF16)
    k = k_ref[0]
    v = v_ref[0]
    outs = []
    for half in range(2):
        qh = jnp.where(first if half == 0 else jnp.logical_not(first), q, jnp.zeros_like(q))
        outs.append(_softmax_pv(_dot_nt(qh, k), v))
    o_ref[0] = jnp.where(first, outs[0], outs[1]).astype(o_ref.dtype)


def _mem_attention(h3, col_block0, mem_k3, mem_v3, tq=512):
    b, s, _ = h3.shape
    nm = mem_k3.shape[1]
    npair = MEM_HEADS // 2
    return pl.pallas_call(
        _mem_attn_kernel,
        grid=(b, npair, s // tq),
        in_specs=[pl.BlockSpec((1, tq, LANES), lambda i, p, t: (i, t, col_block0 + p)),
                  pl.BlockSpec((1, nm, LANES), lambda i, p, t: (i, 0, p)),
                  pl.BlockSpec((1, nm, LANES), lambda i, p, t: (i, 0, p))],
        out_specs=pl.BlockSpec((1, tq, LANES), lambda i, p, t: (i, t, p)),
        out_shape=jax.ShapeDtypeStruct((b, s, MEM_W), BF16),
        compiler_params=_cparams("parallel", "parallel", "parallel"),
        name="mem_attention",
    )(h3, mem_k3, mem_v3)


def _outproj_ln_kernel(ya_ref, ym_ref, wa_ref, wm_ref, x_ref, g_ref, b_ref, o_ref, or_ref, *, tm):
    acc = _dot(ya_ref[...], wa_ref[...]) + _dot(ym_ref[...], wm_ref[...])
    out = _ln(ALPHA * x_ref[...] + acc, g_ref[...], b_ref[...])
    o_ref[...] = out
    _write_rows(or_ref, out, tm)


def _outproj_ln(ya, ym, wa, wm, x2, g, b, tm=512):
    n = x2.shape[0]
    ka, km = ya.shape[1], ym.shape[1]
    return pl.pallas_call(
        functools.partial(_outproj_ln_kernel, tm=tm),
        grid=(n // tm,),
        in_specs=[pl.BlockSpec((tm, ka), lambda i: (i, 0)),
                  pl.BlockSpec((tm, km), lambda i: (i, 0)),
                  pl.BlockSpec((ka, D_MODEL), lambda i: (0, 0)),
                  pl.BlockSpec((km, D_MODEL), lambda i: (0, 0)),
                  pl.BlockSpec((tm, D_MODEL), lambda i: (i, 0)),
                  pl.BlockSpec((1, D_MODEL), lambda i: (0, 0)),
                  pl.BlockSpec((1, D_MODEL), lambda i: (0, 0))],
        out_specs=[pl.BlockSpec((tm, D_MODEL), lambda i: (i, 0)),
                   pl.BlockSpec((tm * ROW_CHUNKS, LANES), lambda i: (i, 0))],
        out_shape=[jax.ShapeDtypeStruct((n, D_MODEL), F32),
                   jax.ShapeDtypeStruct((n * ROW_CHUNKS, LANES), F32)],
        compiler_params=_cparams("parallel"),
        name="outproj_ln",
    )(ya, ym, wa, wm, x2, g, b)


def _router_kernel(x_ref, rwt_ref, rb_ref, lpos_ref, w_ref, cnt_ref, tcnt_ref, toff_ref, tbef_ref, *, tm):
    @pl.when(pl.program_id(0) == 0)
    def _():
        cnt_ref[...] = jnp.zeros_like(cnt_ref)

    logits = _dot_nt(rwt_ref[...], x_ref[...], precision=lax.Precision.HIGHEST)
    scores = jax.nn.sigmoid(logits)
    biased = scores + rb_ref[...]
    bv = [biased[e:e + 1, :] for e in range(N_EXPERTS)]
    sv = [scores[e:e + 1, :] for e in range(N_EXPERTS)]

    grp = []
    for g in range(N_GROUPS):
        m = bv[g * EXPERTS_PER_GROUP:(g + 1) * EXPERTS_PER_GROUP]
        best = None
        for a in range(EXPERTS_PER_GROUP):
            for c in range(a + 1, EXPERTS_PER_GROUP):
                pair = m[a] + m[c]
                best = pair if best is None else jnp.maximum(best, pair)
        grp.append(best)
    gsel = jnp.zeros((1, tm), I32)
    gbest = grp[0]
    for g in range(1, N_GROUPS):
        better = grp[g] > gbest
        gsel = jnp.where(better, g, gsel)
        gbest = jnp.where(better, grp[g], gbest)

    def pick(vals, j):
        out = vals[j]
        for g in range(1, N_GROUPS):
            out = jnp.where(gsel == g, vals[g * EXPERTS_PER_GROUP + j], out)
        return out

    cb = [pick(bv, j) for j in range(EXPERTS_PER_GROUP)]
    cs = [pick(sv, j) for j in range(EXPERTS_PER_GROUP)]
    i1 = jnp.zeros((1, tm), I32)
    m1 = cb[0]
    s1 = cs[0]
    for j in range(1, EXPERTS_PER_GROUP):
        gt = cb[j] > m1
        i1 = jnp.where(gt, j, i1)
        m1 = jnp.where(gt, cb[j], m1)
        s1 = jnp.where(gt, cs[j], s1)
    i2 = jnp.zeros((1, tm), I32)
    m2 = jnp.full((1, tm), -jnp.inf, F32)
    s2 = jnp.zeros((1, tm), F32)
    for j in range(EXPERTS_PER_GROUP):
        ok = jnp.logical_and(i1 != j, cb[j] > m2)
        i2 = jnp.where(ok, j, i2)
        m2 = jnp.where(ok, cb[j], m2)
        s2 = jnp.where(ok, cs[j], s2)
    e1 = gsel * EXPERTS_PER_GROUP + i1
    e2 = gsel * EXPERTS_PER_GROUP + i2
    tot = s1 + s2
    w_ref[...] = jnp.concatenate([s1 / tot, s2 / tot], axis=0)

    i = pl.program_id(0)
    eio = lax.broadcasted_iota(I32, (N_EXPERTS, tm), 0)
    oh1 = eio == e1
    oh2 = eio == e2
    ohs = jnp.logical_or(oh1, oh2).astype(F32)
    before = (lax.broadcasted_iota(I32, (tm, tm), 0) < lax.broadcasted_iota(I32, (tm, tm), 1))
    pre = _dot(ohs.astype(BF16), before.astype(BF16))
    tile_cnt = jnp.sum(ohs, axis=1, keepdims=True)
    offs = []
    acc = jnp.zeros((1, 1), F32)
    for e in range(N_EXPERTS):
        offs.append(acc)
        acc = acc + tile_cnt[e:e + 1, :]
    tile_off = jnp.concatenate(offs, axis=0)
    pos = tile_off + pre
    p1 = jnp.sum(jnp.where(oh1, pos, 0.0), axis=0, keepdims=True)
    p2 = jnp.sum(jnp.where(oh2, pos, 0.0), axis=0, keepdims=True)
    lpos_ref[...] = jnp.concatenate([p1, p2], axis=0).astype(I32)

    @pl.when(i == 0)
    def _():
        for ref in (tcnt_ref, toff_ref, tbef_ref):
            ref[...] = jnp.zeros_like(ref)

    here = lax.broadcasted_iota(I32, (1, LANES), 1) == i
    tcnt_ref[...] = jnp.where(here, tile_cnt, tcnt_ref[...])
    toff_ref[...] = jnp.where(here, tile_off, toff_ref[...])
    tbef_ref[...] = jnp.where(here, cnt_ref[:, 0:1], tbef_ref[...])
    cnt_ref[...] += tile_cnt


def _router(x2, rwt, rb, tm):
    n = x2.shape[0]
    assert n // tm <= LANES
    table = pl.BlockSpec((N_EXPERTS, LANES), lambda i: (0, 0))
    return pl.pallas_call(
        functools.partial(_router_kernel, tm=tm),
        grid=(n // tm,),
        in_specs=[pl.BlockSpec((tm, D_MODEL), lambda i: (i, 0)),
                  pl.BlockSpec((N_EXPERTS, D_MODEL), lambda i: (0, 0)),
                  pl.BlockSpec((N_EXPERTS, 1), lambda i: (0, 0))],
        out_specs=[pl.BlockSpec((2, tm), lambda i: (0, i)),
                   pl.BlockSpec((2, tm), lambda i: (0, i)),
                   table, table, table, table],
        out_shape=[jax.ShapeDtypeStruct((2, n), I32),
                   jax.ShapeDtypeStruct((2, n), F32)]
                  + [jax.ShapeDtypeStruct((N_EXPERTS, LANES), F32)] * 4,
        compiler_params=_cparams("arbitrary"),
        name="router",
    )(x2, rwt, rb)


def _plan_kernel(cnt_ref, tbef_ref, meta_ref, rstart_ref, *, nbl):
    shift = MOE_BM.bit_length() - 1
    cnt = cnt_ref[...].astype(I32)
    padded = ((cnt + (MOE_BM - 1)) >> shift) << shift
    starts = []
    acc = jnp.zeros((1, LANES), I32)
    for e in range(N_EXPERTS):
        starts.append(acc)
        acc = acc + padded[e:e + 1, :]
    pad_start = jnp.concatenate(starts, axis=0)
    pad_end = pad_start + padded
    rstart_ref[...] = pad_start + tbef_ref[...].astype(I32)
    blk0 = lax.broadcasted_iota(I32, (N_EXPERTS, nbl), 1) * MOE_BM
    block_e = jnp.sum((pad_end[:, 0:1] <= blk0).astype(I32), axis=0, keepdims=True)
    block_e = jnp.minimum(block_e, N_EXPERTS - 1)
    n_used = jnp.broadcast_to(acc[:, 0:1] >> shift, (1, nbl))
    diag = lax.broadcasted_iota(I32, (N_EXPERTS, nbl), 0) == lax.broadcasted_iota(I32, (N_EXPERTS, nbl), 1)
    fill_lo = jnp.sum(jnp.where(diag, (pad_start + cnt)[:, 0:1], 0), axis=0, keepdims=True)
    fill_hi = jnp.sum(jnp.where(diag, pad_end[:, 0:1], 0), axis=0, keepdims=True)
    meta_ref[...] = jnp.concatenate([block_e, n_used, fill_lo, fill_hi, jnp.zeros((SUBLANES - 4, nbl), I32)],
                                    axis=0)


def _plan(cnt, tbef, n_blocks):
    nbl = -(-n_blocks // LANES) * LANES
    table = pl.BlockSpec((N_EXPERTS, LANES), lambda i: (0, 0))
    return pl.pallas_call(
        functools.partial(_plan_kernel, nbl=nbl),
        grid=(1,),
        in_specs=[table, table],
        out_specs=[pl.BlockSpec((SUBLANES, nbl), lambda i: (0, 0)), table],
        out_shape=[jax.ShapeDtypeStruct((SUBLANES, nbl), I32),
                   jax.ShapeDtypeStruct((N_EXPERTS, LANES), I32)],
        compiler_params=_cparams("arbitrary"),
        name="moe_plan",
    )(cnt, tbef)


def _rows(ref, row, nrows):
    return ref.at[pl.ds(pl.multiple_of(row * ROW_CHUNKS, ROW_CHUNKS), nrows * ROW_CHUNKS), :]


def _rows_wait(src_hbm, buf, sem):
    pltpu.make_async_copy(src_hbm.at[pl.ds(0, buf.shape[0]), :], buf, sem).wait()


def _copy_pieces(src, src_row, dst, dst_row, count, max_rows, sem, wait=False):
    bit = max_rows.bit_length() - 1
    while bit >= 0:
        size = 1 << bit
        done = (count >> (bit + 1)) << (bit + 1)

        @pl.when(((count >> bit) & 1) == 1)
        def _():
            cp = pltpu.make_async_copy(_rows(src, src_row + done, size), _rows(dst, dst_row + done, size), sem)
            cp.start()
            if wait:
                cp.wait()

        bit -= 1


def _tile_runs(tcnt_ref, toff_ref, rstart_ref, tile, buf, hbm, sem, *, to_hbm, tm):
    def per_expert(e, carry):
        k = tile * N_EXPERTS + e
        if to_hbm:
            _copy_pieces(buf, toff_ref[k], hbm, rstart_ref[k], tcnt_ref[k], tm, sem)
        else:
            _copy_pieces(hbm, rstart_ref[k], buf, toff_ref[k], tcnt_ref[k], tm, sem)
        return carry

    lax.fori_loop(0, N_EXPERTS, per_expert, 0)


def _dispatch_kernel(lpos_ref, tcnt_ref, toff_ref, rstart_ref, flo_ref, fhi_ref, nu_ref, x_ref, xs_hbm,
                     s0, s1, zbuf, sem, zsem, *, n, tm, n_blocks):
    i = pl.program_id(0)
    nt = pl.num_programs(0)
    bufs = (s0, s1)
    unroll = 8

    for slot in range(2):
        @pl.when(i % 2 == slot)
        def _():
            buf = bufs[slot]

            @pl.when(i >= 2)
            def _():
                _rows_wait(xs_hbm, buf, sem.at[slot])

            def place(c, carry):
                for u in range(unroll):
                    t = c * unroll + u
                    v = x_ref[pl.ds(pl.multiple_of(t * ROW_CHUNKS, ROW_CHUNKS), ROW_CHUNKS), :]
                    for k in range(2):
                        p = lpos_ref[k * n + i * tm + t]
                        buf[pl.ds(pl.multiple_of(p * ROW_CHUNKS, ROW_CHUNKS), ROW_CHUNKS), :] = v
                return carry

            lax.fori_loop(0, tm // unroll, place, 0)
            _tile_runs(tcnt_ref, toff_ref, rstart_ref, i, buf, xs_hbm, sem.at[slot], to_hbm=True, tm=tm)

    @pl.when(i == nt - 1)
    def _():
        for slot in range(2):
            @pl.when(nt > slot)
            def _():
                _rows_wait(xs_hbm, bufs[slot], sem.at[slot])

        zbuf[...] = jnp.zeros_like(zbuf)
        for e in range(N_EXPERTS):
            _copy_pieces(zbuf, 0, xs_hbm, flo_ref[e], fhi_ref[e] - flo_ref[e], MOE_BM // 2, zsem, wait=True)

        def zero_block(j, carry):
            cp = pltpu.make_async_copy(zbuf, _rows(xs_hbm, j * MOE_BM, MOE_BM), zsem)
            cp.start()
            cp.wait()
            return carry

        lax.fori_loop(nu_ref[0], n_blocks, zero_block, 0)


def _dispatch(lpos_flat, tcnt, toff, rstart, fill_lo, fill_hi, n_used, xr, n_blocks, tm):
    n = xr.shape[0] // ROW_CHUNKS
    return pl.pallas_call(
        functools.partial(_dispatch_kernel, n=n, tm=tm, n_blocks=n_blocks),
        grid_spec=pltpu.PrefetchScalarGridSpec(
            num_scalar_prefetch=7,
            grid=(n // tm,),
            in_specs=[pl.BlockSpec((tm * ROW_CHUNKS, LANES), lambda i, *_: (i, 0))],
            out_specs=pl.BlockSpec(memory_space=pl.ANY),
            scratch_shapes=[pltpu.VMEM((2 * tm * ROW_CHUNKS, LANES), F32),
                            pltpu.VMEM((2 * tm * ROW_CHUNKS, LANES), F32),
                            pltpu.VMEM((MOE_BM * ROW_CHUNKS, LANES), F32),
                            pltpu.SemaphoreType.DMA((2,)),
                            pltpu.SemaphoreType.DMA(())]),
        out_shape=jax.ShapeDtypeStruct((n_blocks * MOE_BM * ROW_CHUNKS, LANES), F32),
        compiler_params=_cparams("arbitrary"),
        name="moe_dispatch",
    )(lpos_flat, tcnt, toff, rstart, fill_lo, fill_hi, n_used, xr)


def _experts_kernel(be_ref, nu_ref, xs_ref, wg_ref, wu_ref, wd_ref, y_ref):
    del be_ref
    used = pl.program_id(0) < nu_ref[0]

    @pl.when(used)
    def _():
        x = _read_rows(xs_ref, MOE_BM).astype(BF16)
        h = _silu(_dot(x, wg_ref[0])) * _dot(x, wu_ref[0])
        _write_rows(y_ref, _dot(h.astype(BF16), wd_ref[0]), MOE_BM)

    @pl.when(jnp.logical_not(used))
    def _():
        y_ref[...] = jnp.zeros_like(y_ref)


def _experts(block_e, n_used, xs, wg, wu, wd):
    n_blocks = block_e.shape[0]

    def last_used(j, nu):
        return jnp.minimum(j, nu[0] - 1)

    def wblk(j, be, nu):
        return (be[last_used(j, nu)], 0, 0)

    return pl.pallas_call(
        _experts_kernel,
        grid_spec=pltpu.PrefetchScalarGridSpec(
            num_scalar_prefetch=2,
            grid=(n_blocks,),
            in_specs=[pl.BlockSpec((MOE_BM * ROW_CHUNKS, LANES), lambda j, be, nu: (last_used(j, nu), 0)),
                      pl.BlockSpec((1, D_MODEL, D_EXPERT), wblk),
                      pl.BlockSpec((1, D_MODEL, D_EXPERT), wblk),
                      pl.BlockSpec((1, D_EXPERT, D_MODEL), wblk)],
            out_specs=pl.BlockSpec((MOE_BM * ROW_CHUNKS, LANES), lambda j, be, nu: (j, 0))),
        out_shape=jax.ShapeDtypeStruct(xs.shape, F32),
        compiler_params=_cparams("arbitrary"),
        name="moe_experts",
    )(block_e, n_used, xs, wg, wu, wd)


def _combine_ln_kernel(lpos_ref, tcnt_ref, toff_ref, rstart_ref, y_hbm, x_ref, w1_ref, w2_ref, g_ref, b_ref, o_ref,
                       r0, r1, u1, u2, sem, *, n, tm):
    i = pl.program_id(0)
    nt = pl.num_programs(0)
    bufs = (r0, r1)
    unroll = 8

    def fetch(tile, slot):
        _tile_runs(tcnt_ref, toff_ref, rstart_ref, tile, bufs[slot], y_hbm, sem.at[slot], to_hbm=False, tm=tm)

    @pl.when(i == 0)
    def _():
        fetch(0, 0)

    for slot in range(2):
        @pl.when(i % 2 == slot)
        def _():
            @pl.when(i + 1 < nt)
            def _():
                fetch(i + 1, 1 - slot)

            buf = bufs[slot]
            _rows_wait(y_hbm, buf, sem.at[slot])

            def place(c, carry):
                for u in range(unroll):
                    t = c * unroll + u
                    dst = pl.ds(pl.multiple_of(t * ROW_CHUNKS, ROW_CHUNKS), ROW_CHUNKS)
                    for k, out in enumerate((u1, u2)):
                        p = lpos_ref[k * n + i * tm + t]
                        out[dst, :] = buf[pl.ds(pl.multiple_of(p * ROW_CHUNKS, ROW_CHUNKS), ROW_CHUNKS), :]
                return carry

            lax.fori_loop(0, tm // unroll, place, 0)
            moe = w1_ref[...] * _read_rows(u1, tm) + w2_ref[...] * _read_rows(u2, tm)
            o_ref[...] = _ln(ALPHA * x_ref[...] + moe, g_ref[...], b_ref[...])


def _combine_ln(lpos_flat, tcnt, toff, rstart, y, x2, w1, w2, g, b, tm):
    n = x2.shape[0]
    return pl.pallas_call(
        functools.partial(_combine_ln_kernel, n=n, tm=tm),
        grid_spec=pltpu.PrefetchScalarGridSpec(
            num_scalar_prefetch=4,
            grid=(n // tm,),
            in_specs=[pl.BlockSpec(memory_space=pl.ANY),
                      pl.BlockSpec((tm, D_MODEL), lambda i, *_: (i, 0)),
                      pl.BlockSpec((tm, 1), lambda i, *_: (i, 0)),
                      pl.BlockSpec((tm, 1), lambda i, *_: (i, 0)),
                      pl.BlockSpec((1, D_MODEL), lambda i, *_: (0, 0)),
                      pl.BlockSpec((1, D_MODEL), lambda i, *_: (0, 0))],
            out_specs=pl.BlockSpec((tm, D_MODEL), lambda i, *_: (i, 0)),
            scratch_shapes=[pltpu.VMEM((2 * tm * ROW_CHUNKS, LANES), F32)] * 2
                           + [pltpu.VMEM((tm * ROW_CHUNKS, LANES), F32)] * 2
                           + [pltpu.SemaphoreType.DMA((2,))]),
        out_shape=jax.ShapeDtypeStruct((n, D_MODEL), F32),
        compiler_params=_cparams("arbitrary"),
        name="moe_combine_ln",
    )(lpos_flat, tcnt, toff, rstart, y, x2, w1, w2, g, b)


def _moe_ln(x2, xr, rwt, rb, wg, wu, wd, g, b):
    n = x2.shape[0]
    n_blocks = (2 * n) // MOE_BM + N_EXPERTS
    tm = MOE_TILE
    nt = n // tm
    lpos, w, cnt, tcnt, toff, tbef = _router(x2, rwt, rb, tm)
    meta, rstart = _plan(cnt, tbef, n_blocks)
    block_e = meta[0, :n_blocks]
    n_used = meta[1, :1]

    def per_tile(table):
        return table[:, :nt].T.reshape(nt * N_EXPERTS).astype(I32)

    lpos_flat = lpos.reshape(2 * n)
    tcnt, toff, rstart = per_tile(tcnt), per_tile(toff), per_tile(rstart)
    xs = _dispatch(lpos_flat, tcnt, toff, rstart, meta[2, :N_EXPERTS], meta[3, :N_EXPERTS], n_used, xr,
                   n_blocks, tm)
    y = _experts(block_e, n_used, xs, wg, wu, wd)
    return _combine_ln(lpos_flat, tcnt, toff, rstart, y, x2, w[0].reshape(n, 1), w[1].reshape(n, 1), g, b, tm)


def _conv_qkv_kernel(xm_ref, cw_ref, cb_ref, wq_ref, wk_ref, wv_ref, q_ref, k_ref, v_ref, xc_ref, *, s):
    xm_b = xm_ref[0]
    xm = xm_b.astype(F32)
    cw = cw_ref[...]
    row = lax.broadcasted_iota(I32, (s, 1), 0)
    half = CONV_K // 2
    acc = cb_ref[...] + xm * cw[half:half + 1, :]
    for sh in range(1, half + 1):
        past = jnp.where(row >= sh, pltpu.roll(xm, sh, axis=0), 0.0)
        acc = acc + past * cw[half - sh:half - sh + 1, :]
        nxt = jnp.where(row < s - sh, pltpu.roll(xm, s - sh, axis=0), 0.0)
        acc = acc + nxt * cw[half + sh:half + sh + 1, :]
    xc = _silu(acc).astype(BF16)
    xc_ref[0] = xc
    q_ref[0] = _dot(xc, wq_ref[0]).astype(BF16)
    k_ref[0] = (_dot(xc, wk_ref[0]) * (ML_DH ** -0.5)).astype(BF16)
    v_ref[0] = _dot(xm_b, wv_ref[0]).astype(BF16)


def _conv_qkv(main3, cw, cb, wq, wk, wv):
    b, s, _ = main3.shape
    tok = pl.BlockSpec((1, s, ML_DHP), lambda i, h: (i, 0, h))
    wspec = pl.BlockSpec((1, ML_DHP, ML_DHP), lambda i, h: (h, 0, 0))
    return pl.pallas_call(
        functools.partial(_conv_qkv_kernel, s=s),
        grid=(b, ML_HEADS),
        in_specs=[tok,
                  pl.BlockSpec((CONV_K, ML_DHP), lambda i, h: (0, h)),
                  pl.BlockSpec((1, ML_DHP), lambda i, h: (0, h)),
                  wspec, wspec, wspec],
        out_specs=[tok] * 4,
        out_shape=[jax.ShapeDtypeStruct((b, s, ML_WP), BF16)] * 4,
        compiler_params=_cparams("parallel", "parallel"),
        name="conv_qkv",
    )(main3, cw, cb, wq, wk, wv)


def _split3(x):
    hi = x.astype(BF16)
    r1 = x - hi.astype(F32)
    mid = r1.astype(BF16)
    lo = (r1 - mid.astype(F32)).astype(BF16)
    return hi, mid, lo


def _gates_kernel(gc_ref, gr_ref, gbc_ref, gbr_ref, ac_ref, ar_ref, *, s):
    lane = lax.broadcasted_iota(I32, (1, LANES), 1)
    sub = lax.broadcasted_iota(I32, (LANES, 1), 0)
    ti = lax.broadcasted_iota(I32, (CHUNK, CHUNK), 0)
    tj = lax.broadcasted_iota(I32, (CHUNK, CHUNK), 1)
    lower = (tj <= ti).astype(BF16)
    upper = (ti <= tj).astype(BF16)

    def pick(idx, pre, suf, raw):
        fwd = jnp.logical_and(idx >= ML_HEADS, idx < 2 * ML_HEADS)
        bwd = jnp.logical_and(idx >= 3 * ML_HEADS, idx < 4 * ML_HEADS)
        return jnp.where(fwd, pre, jnp.where(bwd, suf, raw))

    def body(c, carry):
        t0 = pl.multiple_of(c * CHUNK, CHUNK)
        g = gc_ref[0, pl.ds(t0, CHUNK), :] + gbc_ref[...]
        ls = jax.nn.log_sigmoid(g)
        pre = sum(_dot(lower, part) for part in _split3(ls))
        suf = jnp.sum(ls, axis=0, keepdims=True) - pre + ls
        ac_ref[0, pl.ds(t0, CHUNK), :] = pick(lane, pre, suf, g)
        g = gr_ref[:, pl.ds(t0, CHUNK)] + gbr_ref[...]
        ls = jax.nn.log_sigmoid(g)
        pre = sum(_dot(part, upper) for part in _split3(ls))
        suf = jnp.sum(ls, axis=1, keepdims=True) - pre + ls
        ar_ref[:, pl.ds(t0, CHUNK)] = pick(sub, pre, suf, g)
        return carry

    lax.fori_loop(0, s // CHUNK, body, 0)


def _gates(gcol3, grow, gbc, gbr):
    b, s, _ = gcol3.shape
    return pl.pallas_call(
        functools.partial(_gates_kernel, s=s),
        grid=(b,),
        in_specs=[pl.BlockSpec((1, s, LANES), lambda i: (i, 0, 0)),
                  pl.BlockSpec((LANES, s), lambda i: (0, i)),
                  pl.BlockSpec((1, LANES), lambda i: (0, 0)),
                  pl.BlockSpec((LANES, 1), lambda i: (0, 0))],
        out_specs=[pl.BlockSpec((1, s, LANES), lambda i: (i, 0, 0)),
                   pl.BlockSpec((LANES, s), lambda i: (0, i))],
        out_shape=[jax.ShapeDtypeStruct((b, s, LANES), F32),
                   jax.ShapeDtypeStruct((LANES, b * s), F32)],
        compiler_params=_cparams("parallel"),
        name="mlstm_gates",
    )(gcol3, grow, gbc, gbr)


def _mlstm_kernel(q_ref, k_ref, v_ref, gc_ref, gr_ref, z_ref, xc_ref, ng_ref, sk_ref,
                  y_ref, hf_ref, hb_ref, cf_ref, cb_ref, nf_ref, nb_ref, mf_ref, mb_ref, *, s):
    head = pl.program_id(1)
    nc = s // CHUNK
    lane = lax.broadcasted_iota(I32, (1, LANES), 1)
    sub = lax.broadcasted_iota(I32, (LANES, 1), 0)
    ti = lax.broadcasted_iota(I32, (CHUNK, CHUNK), 0)
    tj = lax.broadcasted_iota(I32, (CHUNK, CHUNK), 1)

    for ref in (cf_ref, cb_ref, nf_ref, nb_ref, mf_ref, mb_ref):
        ref[...] = jnp.zeros_like(ref)

    def intra(c, rev):
        t0 = pl.multiple_of(c * CHUNK, CHUNK)
        qb = q_ref[0, pl.ds(t0, CHUNK), :]
        kb = k_ref[0, pl.ds(t0, CHUNK), :]
        vb = v_ref[0, pl.ds(t0, CHUNK), :]
        gc = gc_ref[0, pl.ds(t0, CHUNK), :]
        gr = gr_ref[:, pl.ds(t0, CHUNK)]
        i_col = head + (2 * ML_HEADS if rev else 0)
        f_col = i_col + ML_HEADS
        allowed = (tj >= ti) if rev else (tj <= ti)
        b_col = jnp.sum(jnp.where(lane == f_col, gc, 0.0), axis=1, keepdims=True)
        i_colv = jnp.sum(jnp.where(lane == i_col, gc, 0.0), axis=1, keepdims=True)
        b_row = jnp.sum(jnp.where(sub == f_col, gr, 0.0), axis=0, keepdims=True)
        i_row = jnp.sum(jnp.where(sub == i_col, gr, 0.0), axis=0, keepdims=True)
        b_last = b_col[0:1, :] if rev else b_col[CHUNK - 1:CHUNK, :]

        d = jnp.where(allowed, b_col - b_row + i_row, NEG)
        m_in = jnp.max(d, axis=1, keepdims=True)
        sc = _dot_nt(qb, kb) * jnp.exp(d - m_in)
        num_in = _dot(sc.astype(BF16), vb)
        den_in = jnp.sum(sc, axis=1, keepdims=True)
        return t0, qb, kb, vb, b_col, i_colv, b_last, m_in, num_in, den_in

    def update(parts, h_ref, c_ref, n_ref, m_ref):
        t0, qb, kb, vb, b_col, i_colv, b_last, m_in, num_in, den_in = parts
        m = m_ref[...]
        cmat = c_ref[...]
        nvec = n_ref[...]
        inter = b_col + m
        m_t = jnp.maximum(m_in, inter)
        a_in = jnp.exp(m_in - m_t)
        iexp = jnp.exp(inter - m_t)
        num = a_in * num_in + iexp * _dot(qb, cmat.astype(BF16))
        den = a_in * den_in + iexp * jnp.sum(qb.astype(F32) * nvec, axis=1, keepdims=True)
        h_ref[pl.ds(t0, CHUNK), :] = num / jnp.maximum(jnp.abs(den), jnp.exp(-m_t))

        w_s = b_last - b_col + i_colv
        m_new = jnp.maximum(b_last + m, jnp.max(w_s, axis=0, keepdims=True))
        wexp = jnp.exp(w_s - m_new)
        cexp = jnp.exp(b_last + m - m_new)
        kw = kb.astype(F32) * wexp
        c_ref[...] = cexp * cmat + lax.dot_general(kw.astype(BF16), vb, (((0,), (0,)), ((), ())),
                                                   preferred_element_type=F32)
        n_ref[...] = cexp * nvec + jnp.sum(kw, axis=0, keepdims=True)
        m_ref[...] = m_new

    def step(i, carry):
        fwd = intra(i, False)
        bwd = intra(nc - 1 - i, True)
        update(fwd, hf_ref, cf_ref, nf_ref, mf_ref)
        update(bwd, hb_ref, cb_ref, nb_ref, mb_ref)
        return carry

    lax.fori_loop(0, nc, step, 0)

    real = lax.broadcasted_iota(I32, (1, ML_DHP), 1) < ML_DH
    tb = 256

    def fin(j, carry):
        t0 = pl.multiple_of(j * tb, tb)
        hs = hf_ref[pl.ds(t0, tb), :] + hb_ref[pl.ds(t0, tb), :]
        mu = jnp.sum(hs, axis=1, keepdims=True) * (1.0 / ML_DH)
        dev = jnp.where(real, hs - mu, 0.0)
        var = jnp.sum(dev * dev, axis=1, keepdims=True) * (1.0 / ML_DH)
        hn = dev * lax.rsqrt(var + LN_EPS) * ng_ref[...]
        xc = xc_ref[0, pl.ds(t0, tb), :].astype(F32)
        z = z_ref[0, pl.ds(t0, tb), :].astype(F32)
        y_ref[0, pl.ds(t0, tb), :] = ((hn + sk_ref[...] * xc) * _silu(z)).astype(BF16)
        return carry

    lax.fori_loop(0, s // tb, fin, 0)


def _mlstm(q, k, v, gcol3, grow, main3, xc, ng, sk):
    b, s, _ = q.shape
    tok = pl.BlockSpec((1, s, ML_DHP), lambda i, h: (i, 0, h))
    vec = pl.BlockSpec((1, ML_DHP), lambda i, h: (0, h))
    return pl.pallas_call(
        functools.partial(_mlstm_kernel, s=s),
        grid=(b, ML_HEADS),
        in_specs=[tok, tok, tok,
                  pl.BlockSpec((1, s, LANES), lambda i, h: (i, 0, 0)),
                  pl.BlockSpec((LANES, s), lambda i, h: (0, i)),
                  pl.BlockSpec((1, s, ML_DHP), lambda i, h: (i, 0, ML_HEADS + h)),
                  tok, vec, vec],
        out_specs=tok,
        out_shape=jax.ShapeDtypeStruct((b, s, ML_WP), BF16),
        scratch_shapes=[pltpu.VMEM((s, ML_DHP), F32), pltpu.VMEM((s, ML_DHP), F32),
                        pltpu.VMEM((ML_DHP, ML_DHP), F32), pltpu.VMEM((ML_DHP, ML_DHP), F32),
                        pltpu.VMEM((1, ML_DHP), F32), pltpu.VMEM((1, ML_DHP), F32),
                        pltpu.VMEM((1, 1), F32), pltpu.VMEM((1, 1), F32)],
        compiler_params=_cparams("parallel", "parallel"),
        name="mlstm",
    )(q, k, v, gcol3, grow, main3, xc, ng, sk)


def _pad_heads(a, axis):
    a = jnp.moveaxis(a, axis, -1)
    lead = a.shape[:-1]
    a = a.reshape(lead + (ML_HEADS, ML_DH))
    a = jnp.pad(a, [(0, 0)] * len(lead) + [(0, 0), (0, ML_DHP - ML_DH)])
    return jnp.moveaxis(a.reshape(lead + (ML_WP,)), -1, axis)


def kernel(x, mem, mem_ln_g, mem_ln_b, w_mem_kv, router_w, router_b, na_w_in, na_rpb, ml_w_in, ml_conv_w,
           ml_conv_b, ml_w_qkv, ml_gate_b, ml_norm_g, ml_skip, w_out, ln_g, ln_b, exp_w_gate, exp_w_up,
           exp_w_down):
    b, s, d = x.shape
    n = b * s
    nm = mem.shape[1]
    row = lambda a: a.reshape(1, -1)

    mem_k, mem_v = _memkv(mem.reshape(b * nm, d), row(mem_ln_g), row(mem_ln_b), w_mem_kv.astype(BF16))
    mem_k3 = mem_k.reshape(b, nm, MEM_W)
    mem_v3 = mem_v.reshape(b, nm, MEM_W)
    rwt = router_w.T
    rb = router_b.reshape(N_EXPERTS, 1)

    x2 = x.reshape(n, d)

    h0 = _proj(x2, na_w_in[0].astype(BF16)).reshape(b, s, 3 * NA_W + MEM_W)
    y_na = _na_attention(h0, _na_bias_table(na_rpb[0]))
    y_mem = _mem_attention(h0, 3 * NA_W // LANES, mem_k3, mem_v3)
    wo = w_out[0].astype(BF16)
    x2, xr = _outproj_ln(y_na.reshape(n, NA_W), y_mem.reshape(n, MEM_W), wo[:NA_W], wo[NA_W:], x2,
                         row(ln_g[0, 0]), row(ln_b[0, 0]))
    x2 = _moe_ln(x2, xr, rwt, rb, exp_w_gate[0].astype(BF16), exp_w_up[0].astype(BF16),
                 exp_w_down[0].astype(BF16), row(ln_g[0, 1]), row(ln_b[0, 1]))

    w1 = ml_w_in[0]
    w_main = jnp.concatenate([_pad_heads(w1[:, :ML_W], 1), _pad_heads(w1[:, ML_W:2 * ML_W], 1),
                              w1[:, 2 * ML_W + 4 * ML_HEADS:]], axis=1).astype(BF16)
    w_g = jnp.pad(w1[:, 2 * ML_W:2 * ML_W + 4 * ML_HEADS], ((0, 0), (0, LANES - 4 * ML_HEADS))).astype(BF16)
    main, gcol, grow = _proj_gates(x2, w_main, w_g, w_g.T)
    main3 = main.reshape(b, s, 2 * ML_WP + MEM_W)
    wqkv = jnp.pad(ml_w_qkv[0], ((0, 0), (0, 0), (0, ML_DHP - ML_DH), (0, ML_DHP - ML_DH))).astype(BF16)
    q, k, v, xc = _conv_qkv(main3, _pad_heads(ml_conv_w[0], 1), _pad_heads(row(ml_conv_b[0]), 1),
                            wqkv[0], wqkv[1], wqkv[2])
    gb = jnp.pad(ml_gate_b[0].reshape(4 * ML_HEADS), (0, LANES - 4 * ML_HEADS))
    acol, arow = _gates(gcol.reshape(b, s, LANES), grow, gb.reshape(1, LANES), gb.reshape(LANES, 1))
    y_ml = _mlstm(q, k, v, acol, arow, main3, xc,
                  _pad_heads(row(ml_norm_g[0]), 1), _pad_heads(row(ml_skip[0]), 1))
    y_mem = _mem_attention(main3, 2 * ML_WP // LANES, mem_k3, mem_v3)
    wo = w_out[1]
    x2, xr = _outproj_ln(y_ml.reshape(n, ML_WP), y_mem.reshape(n, MEM_W), _pad_heads(wo[:ML_W], 0).astype(BF16),
                         wo[ML_W:].astype(BF16), x2, row(ln_g[1, 0]), row(ln_b[1, 0]))
    x2 = _moe_ln(x2, xr, rwt, rb, exp_w_gate[1].astype(BF16), exp_w_up[1].astype(BF16),
                 exp_w_down[1].astype(BF16), row(ln_g[1, 1]), row(ln_b[1, 1]))
    return x2.reshape(b, s, d)
```

```python
import functools

import numpy as np
import jax
import jax.numpy as jnp
from jax import lax
from jax.experimental import pallas as pl
from jax.experimental.pallas import tpu as pltpu

F32 = jnp.float32
BF16 = jnp.bfloat16
I32 = jnp.int32

D_MODEL = 1024
DEPTH = 2
GRID_W = 64
MEM_HEADS = 4
MEM_DH = 64
MEM_W = MEM_HEADS * MEM_DH
NA_HEADS = 12
NA_DH = 64
NA_W = NA_HEADS * NA_DH
WIN_H = 8
WIN_W = 16
ML_HEADS = 4
ML_DH = 192
ML_DHP = 256
ML_W = ML_HEADS * ML_DH
ML_WP = ML_HEADS * ML_DHP
CONV_K = 5
CHUNK = 128
N_EXPERTS = 16
N_GROUPS = 4
EXPERTS_PER_GROUP = N_EXPERTS // N_GROUPS
D_EXPERT = 512
ALPHA = (2 * DEPTH) ** 0.25
LN_EPS = 1e-5
NEG = -1e30

LANES = 128
SUBLANES = 8
ROW_CHUNKS = D_MODEL // LANES
MOE_BM = 256
MOE_TILE = 256
NA_ROWS_PER_STEP = 4
VMEM_LIMIT = 48 * 1024 * 1024


def _cparams(*sem):
    return pltpu.CompilerParams(dimension_semantics=sem, vmem_limit_bytes=VMEM_LIMIT)


def _dot(a, b):
    return jnp.dot(a, b, preferred_element_type=F32)


def _dot_nt(a, b, precision=None):
    return lax.dot_general(a, b, (((1,), (1,)), ((), ())), precision=precision,
                           preferred_element_type=F32)


def _ln(z, g, b):
    mu = jnp.mean(z, axis=-1, keepdims=True)
    zc = z - mu
    var = jnp.mean(zc * zc, axis=-1, keepdims=True)
    return zc * lax.rsqrt(var + LN_EPS) * g + b


def _silu(x):
    return x * jax.nn.sigmoid(x)


def _read_rows(ref, n):
    return jnp.concatenate([ref[pl.ds(j, n, stride=ROW_CHUNKS), :] for j in range(ROW_CHUNKS)], axis=1)


def _write_rows(ref, val, n):
    for j in range(ROW_CHUNKS):
        ref[pl.ds(j, n, stride=ROW_CHUNKS), :] = val[:, j * LANES:(j + 1) * LANES]


def _memkv_kernel(m_ref, g_ref, b_ref, w_ref, k_ref, v_ref):
    z = _ln(m_ref[...], g_ref[...], b_ref[...])
    kv = _dot(z.astype(BF16), w_ref[...])
    k_ref[...] = kv[:, :MEM_W].astype(BF16)
    v_ref[...] = kv[:, MEM_W:].astype(BF16)


def _memkv(mem2, g, b, w):
    n = mem2.shape[0]
    tm = 256
    return pl.pallas_call(
        _memkv_kernel,
        grid=(n // tm,),
        in_specs=[pl.BlockSpec((tm, D_MODEL), lambda i: (i, 0)),
                  pl.BlockSpec((1, D_MODEL), lambda i: (0, 0)),
                  pl.BlockSpec((1, D_MODEL), lambda i: (0, 0)),
                  pl.BlockSpec((D_MODEL, 2 * MEM_W), lambda i: (0, 0))],
        out_specs=[pl.BlockSpec((tm, MEM_W), lambda i: (i, 0)),
                   pl.BlockSpec((tm, MEM_W), lambda i: (i, 0))],
        out_shape=[jax.ShapeDtypeStruct((n, MEM_W), BF16)] * 2,
        compiler_params=_cparams("parallel"),
        name="memkv",
    )(mem2, g, b, w)


def _proj_kernel(x_ref, w_ref, o_ref):
    o_ref[...] = _dot(x_ref[...].astype(BF16), w_ref[...]).astype(o_ref.dtype)


def _proj(x2, w, tm=512):
    n, k = x2.shape
    nout = w.shape[1]
    return pl.pallas_call(
        _proj_kernel,
        grid=(n // tm,),
        in_specs=[pl.BlockSpec((tm, k), lambda i: (i, 0)),
                  pl.BlockSpec((k, nout), lambda i: (0, 0))],
        out_specs=pl.BlockSpec((tm, nout), lambda i: (i, 0)),
        out_shape=jax.ShapeDtypeStruct((n, nout), BF16),
        compiler_params=_cparams("parallel"),
        name="in_proj",
    )(x2, w)


def _proj_gates_kernel(x_ref, w_ref, wg_ref, wgt_ref, o_ref, g_ref, gt_ref):
    xb = x_ref[...].astype(BF16)
    o_ref[...] = _dot(xb, w_ref[...]).astype(BF16)
    g_ref[...] = _dot(xb, wg_ref[...])
    gt_ref[...] = _dot_nt(wgt_ref[...], xb)


def _proj_gates(x2, w, wg, wgt, tm=512):
    n, k = x2.shape
    nout = w.shape[1]
    return pl.pallas_call(
        _proj_gates_kernel,
        grid=(n // tm,),
        in_specs=[pl.BlockSpec((tm, k), lambda i: (i, 0)),
                  pl.BlockSpec((k, nout), lambda i: (0, 0)),
                  pl.BlockSpec((k, LANES), lambda i: (0, 0)),
                  pl.BlockSpec((LANES, k), lambda i: (0, 0))],
        out_specs=[pl.BlockSpec((tm, nout), lambda i: (i, 0)),
                   pl.BlockSpec((tm, LANES), lambda i: (i, 0)),
                   pl.BlockSpec((LANES, tm), lambda i: (0, i))],
        out_shape=[jax.ShapeDtypeStruct((n, nout), BF16),
                   jax.ShapeDtypeStruct((n, LANES), F32),
                   jax.ShapeDtypeStruct((LANES, n), F32)],
        compiler_params=_cparams("parallel"),
        name="in_proj_gates",
    )(x2, w, wg, wgt)


def _softmax_pv(s, v):
    m = jnp.max(s, axis=-1, keepdims=True)
    p = jnp.exp(s - m)
    l = jnp.sum(p, axis=-1, keepdims=True)
    return _dot(p.astype(BF16), v) / l


def _na_kernel(q_ref, k_ref, v_ref, tbl_ref, o_ref, *, rows):
    lane = lax.broadcasted_iota(I32, (1, LANES), 1)
    first = lane < NA_DH
    nkeys = WIN_H * GRID_W

    def rows_step(i, carry):
        rr = [i * NA_ROWS_PER_STEP + u for u in range(NA_ROWS_PER_STEP)]
        rss = [jnp.clip(r - WIN_H // 2, 0, rows - WIN_H) for r in rr]
        scores = []
        for r, rs in zip(rr, rss):
            q = q_ref[0, pl.ds(pl.multiple_of(r * GRID_W, GRID_W), GRID_W), :]
            q = q * jnp.asarray(NA_DH ** -0.5, BF16)
            q2 = jnp.concatenate([jnp.where(first, q, jnp.zeros_like(q)),
                                  jnp.where(first, jnp.zeros_like(q), q)], axis=0)
            k = k_ref[0, pl.ds(pl.multiple_of(rs * GRID_W, GRID_W), nkeys), :]
            dr0 = rs - r + WIN_H - 1
            bias = jnp.concatenate(
                [jnp.concatenate([tbl_ref[0, half, dr0 + 2 * m] for m in range(WIN_H // 2)], axis=1)
                 for half in range(2)], axis=0)
            scores.append(_dot_nt(q2, k) + bias)
        probs = []
        for s in scores:
            p = jnp.exp(s - jnp.max(s, axis=-1, keepdims=True))
            probs.append((p.astype(BF16), jnp.sum(p, axis=-1, keepdims=True)))
        for r, rs, (p, l) in zip(rr, rss, probs):
            v = v_ref[0, pl.ds(pl.multiple_of(rs * GRID_W, GRID_W), nkeys), :]
            o = _dot(p, v) / l
            o = jnp.where(first, o[:GRID_W], o[GRID_W:])
            o_ref[0, pl.ds(pl.multiple_of(r * GRID_W, GRID_W), GRID_W), :] = o.astype(o_ref.dtype)
        return carry

    lax.fori_loop(0, rows // NA_ROWS_PER_STEP, rows_step, 0)


def _na_bias_table(rpb):
    qc = np.arange(GRID_W)[:, None]
    kc = np.arange(GRID_W)[None, :]
    cs = np.clip(qc - WIN_W // 2, 0, GRID_W - WIN_W)
    col_in = (kc >= cs) & (kc < cs + WIN_W)
    side = GRID_W - WIN_W
    wide = jnp.pad(rpb, ((0, 0), (0, 0), (side, side)))
    t = jnp.stack([wide[:, :, GRID_W - 1 - q:2 * GRID_W - 1 - q] for q in range(GRID_W)], axis=2)
    t = jnp.where(col_in, t, NEG).astype(F32)
    t2 = jnp.concatenate([t[:, :-1], t[:, 1:]], axis=-1)
    return t2.reshape(NA_HEADS // 2, 2, 2 * WIN_H - 2, GRID_W, 2 * GRID_W)


def _na_attention(h3, tbl):
    b, s, _ = h3.shape
    rows = s // GRID_W
    npair = NA_HEADS // 2
    return pl.pallas_call(
        functools.partial(_na_kernel, rows=rows),
        grid=(b, npair),
        in_specs=[pl.BlockSpec((1, s, LANES), lambda i, p: (i, 0, p)),
                  pl.BlockSpec((1, s, LANES), lambda i, p: (i, 0, npair + p)),
                  pl.BlockSpec((1, s, LANES), lambda i, p: (i, 0, 2 * npair + p)),
                  pl.BlockSpec((1, 2, 2 * WIN_H - 2, GRID_W, 2 * GRID_W), lambda i, p: (p, 0, 0, 0, 0))],
        out_specs=pl.BlockSpec((1, s, LANES), lambda i, p: (i, 0, p)),
        out_shape=jax.ShapeDtypeStruct((b, s, NA_W), BF16),
        compiler_params=_cparams("parallel", "parallel"),
        name="na_attention",
    )(h3, h3, h3, tbl)


def _mem_attn_kernel(q_ref, k_ref, v_ref, o_ref):
    lane = lax.broadcasted_iota(I32, (1, LANES), 1)
    first = lane < MEM_DH
    q = q_ref[0] * jnp.asarray(MEM_DH ** -0.5, BF16)
    k = k_ref[0]
    v = v_ref[0]
    outs = []
    for half in range(2):
        qh = jnp.where(first if half == 0 else jnp.logical_not(first), q, jnp.zeros_like(q))
        outs.append(_softmax_pv(_dot_nt(qh, k), v))
    o_ref[0] = jnp.where(first, outs[0], outs[1]).astype(o_ref.dtype)


def _mem_attention(h3, col_block0, mem_k3, mem_v3, tq=512):
    b, s, _ = h3.shape
    nm = mem_k3.shape[1]
    npair = MEM_HEADS // 2
    return pl.pallas_call(
        _mem_attn_kernel,
        grid=(b, npair, s // tq),
        in_specs=[pl.BlockSpec((1, tq, LANES), lambda i, p, t: (i, t, col_block0 + p)),
                  pl.BlockSpec((1, nm, LANES), lambda i, p, t: (i, 0, p)),
                  pl.BlockSpec((1, nm, LANES), lambda i, p, t: (i, 0, p))],
        out_specs=pl.BlockSpec((1, tq, LANES), lambda i, p, t: (i, t, p)),
        out_shape=jax.ShapeDtypeStruct((b, s, MEM_W), BF16),
        compiler_params=_cparams("parallel", "parallel", "parallel"),
        name="mem_attention",
    )(h3, mem_k3, mem_v3)


def _outproj_ln_kernel(ya_ref, ym_ref, wa_ref, wm_ref, x_ref, g_ref, b_ref, o_ref, or_ref, *, tm):
    acc = _dot(ya_ref[...], wa_ref[...]) + _dot(ym_ref[...], wm_ref[...])
    out = _ln(ALPHA * x_ref[...] + acc, g_ref[...], b_ref[...])
    o_ref[...] = out
    _write_rows(or_ref, out, tm)


def _outproj_ln(ya, ym, wa, wm, x2, g, b, tm=512):
    n = x2.shape[0]
    ka, km = ya.shape[1], ym.shape[1]
    return pl.pallas_call(
        functools.partial(_outproj_ln_kernel, tm=tm),
        grid=(n // tm,),
        in_specs=[pl.BlockSpec((tm, ka), lambda i: (i, 0)),
                  pl.BlockSpec((tm, km), lambda i: (i, 0)),
                  pl.BlockSpec((ka, D_MODEL), lambda i: (0, 0)),
                  pl.BlockSpec((km, D_MODEL), lambda i: (0, 0)),
                  pl.BlockSpec((tm, D_MODEL), lambda i: (i, 0)),
                  pl.BlockSpec((1, D_MODEL), lambda i: (0, 0)),
                  pl.BlockSpec((1, D_MODEL), lambda i: (0, 0))],
        out_specs=[pl.BlockSpec((tm, D_MODEL), lambda i: (i, 0)),
                   pl.BlockSpec((tm * ROW_CHUNKS, LANES), lambda i: (i, 0))],
        out_shape=[jax.ShapeDtypeStruct((n, D_MODEL), F32),
                   jax.ShapeDtypeStruct((n * ROW_CHUNKS, LANES), F32)],
        compiler_params=_cparams("parallel"),
        name="outproj_ln",
    )(ya, ym, wa, wm, x2, g, b)


def _router_kernel(x_ref, rwh_ref, rwl_ref, rb_ref, lpos_ref, w_ref, cnt_ref, tcnt_ref, toff_ref, tbef_ref, *, tm):
    @pl.when(pl.program_id(0) == 0)
    def _():
        cnt_ref[...] = jnp.zeros_like(cnt_ref)

    x = x_ref[...]
    xh = x.astype(BF16)
    xl = (x - xh.astype(F32)).astype(BF16)
    logits_t = _dot(xh, rwh_ref[...]) + (_dot(xh, rwl_ref[...]) + _dot(xl, rwh_ref[...]))
    logits = logits_t.T[:N_EXPERTS]
    scores = jax.nn.sigmoid(logits)
    biased = scores + rb_ref[...]
    bv = [biased[e:e + 1, :] for e in range(N_EXPERTS)]
    sv = [scores[e:e + 1, :] for e in range(N_EXPERTS)]

    grp = []
    for g in range(N_GROUPS):
        m = bv[g * EXPERTS_PER_GROUP:(g + 1) * EXPERTS_PER_GROUP]
        best = None
        for a in range(EXPERTS_PER_GROUP):
            for c in range(a + 1, EXPERTS_PER_GROUP):
                pair = m[a] + m[c]
                best = pair if best is None else jnp.maximum(best, pair)
        grp.append(best)
    gsel = jnp.zeros((1, tm), I32)
    gbest = grp[0]
    for g in range(1, N_GROUPS):
        better = grp[g] > gbest
        gsel = jnp.where(better, g, gsel)
        gbest = jnp.where(better, grp[g], gbest)

    def pick(vals, j):
        out = vals[j]
        for g in range(1, N_GROUPS):
            out = jnp.where(gsel == g, vals[g * EXPERTS_PER_GROUP + j], out)
        return out

    cb = [pick(bv, j) for j in range(EXPERTS_PER_GROUP)]
    cs = [pick(sv, j) for j in range(EXPERTS_PER_GROUP)]
    i1 = jnp.zeros((1, tm), I32)
    m1 = cb[0]
    s1 = cs[0]
    for j in range(1, EXPERTS_PER_GROUP):
        gt = cb[j] > m1
        i1 = jnp.where(gt, j, i1)
        m1 = jnp.where(gt, cb[j], m1)
        s1 = jnp.where(gt, cs[j], s1)
    i2 = jnp.zeros((1, tm), I32)
    m2 = jnp.full((1, tm), -jnp.inf, F32)
    s2 = jnp.zeros((1, tm), F32)
    for j in range(EXPERTS_PER_GROUP):
        ok = jnp.logical_and(i1 != j, cb[j] > m2)
        i2 = jnp.where(ok, j, i2)
        m2 = jnp.where(ok, cb[j], m2)
        s2 = jnp.where(ok, cs[j], s2)
    e1 = gsel * EXPERTS_PER_GROUP + i1
    e2 = gsel * EXPERTS_PER_GROUP + i2
    tot = s1 + s2
    w_ref[...] = jnp.concatenate([s1 / tot, s2 / tot], axis=0)

    i = pl.program_id(0)
    eio = lax.broadcasted_iota(I32, (N_EXPERTS, tm), 0)
    oh1 = eio == e1
    oh2 = eio == e2
    ohs = jnp.logical_or(oh1, oh2).astype(F32)
    before = (lax.broadcasted_iota(I32, (tm, tm), 0) < lax.broadcasted_iota(I32, (tm, tm), 1))
    pre = _dot(ohs.astype(BF16), before.astype(BF16))
    tile_cnt = jnp.sum(ohs, axis=1, keepdims=True)
    offs = []
    acc = jnp.zeros((1, 1), F32)
    for e in range(N_EXPERTS):
        offs.append(acc)
        acc = acc + tile_cnt[e:e + 1, :]
    tile_off = jnp.concatenate(offs, axis=0)
    pos = tile_off + pre
    p1 = jnp.sum(jnp.where(oh1, pos, 0.0), axis=0, keepdims=True)
    p2 = jnp.sum(jnp.where(oh2, pos, 0.0), axis=0, keepdims=True)
    lpos_ref[...] = jnp.concatenate([p1, p2], axis=0).astype(I32)

    @pl.when(i == 0)
    def _():
        for ref in (tcnt_ref, toff_ref, tbef_ref):
            ref[...] = jnp.zeros_like(ref)

    here = lax.broadcasted_iota(I32, (1, LANES), 1) == i
    tcnt_ref[...] = jnp.where(here, tile_cnt, tcnt_ref[...])
    toff_ref[...] = jnp.where(here, tile_off, toff_ref[...])
    tbef_ref[...] = jnp.where(here, cnt_ref[:, 0:1], tbef_ref[...])
    cnt_ref[...] += tile_cnt


def _router(x2, rw, rb, tm):
    n = x2.shape[0]
    assert n // tm <= LANES
    table = pl.BlockSpec((N_EXPERTS, LANES), lambda i: (0, 0))
    return pl.pallas_call(
        functools.partial(_router_kernel, tm=tm),
        grid=(n // tm,),
        in_specs=[pl.BlockSpec((tm, D_MODEL), lambda i: (i, 0)),
                  pl.BlockSpec((D_MODEL, LANES), lambda i: (0, 0)),
                  pl.BlockSpec((D_MODEL, LANES), lambda i: (0, 0)),
                  pl.BlockSpec((N_EXPERTS, 1), lambda i: (0, 0))],
        out_specs=[pl.BlockSpec((2, tm), lambda i: (0, i)),
                   pl.BlockSpec((2, tm), lambda i: (0, i)),
                   table, table, table, table],
        out_shape=[jax.ShapeDtypeStruct((2, n), I32),
                   jax.ShapeDtypeStruct((2, n), F32)]
                  + [jax.ShapeDtypeStruct((N_EXPERTS, LANES), F32)] * 4,
        compiler_params=_cparams("arbitrary"),
        name="router",
    )(x2, rw[0], rw[1], rb)


def _plan_kernel(cnt_ref, tbef_ref, meta_ref, rstart_ref, *, nbl):
    shift = MOE_BM.bit_length() - 1
    cnt = cnt_ref[...].astype(I32)
    padded = ((cnt + (MOE_BM - 1)) >> shift) << shift
    starts = []
    acc = jnp.zeros((1, LANES), I32)
    for e in range(N_EXPERTS):
        starts.append(acc)
        acc = acc + padded[e:e + 1, :]
    pad_start = jnp.concatenate(starts, axis=0)
    pad_end = pad_start + padded
    rstart_ref[...] = pad_start + tbef_ref[...].astype(I32)
    blk0 = lax.broadcasted_iota(I32, (N_EXPERTS, nbl), 1) * MOE_BM
    block_e = jnp.sum((pad_end[:, 0:1] <= blk0).astype(I32), axis=0, keepdims=True)
    block_e = jnp.minimum(block_e, N_EXPERTS - 1)
    n_used = jnp.broadcast_to(acc[:, 0:1] >> shift, (1, nbl))
    diag = lax.broadcasted_iota(I32, (N_EXPERTS, nbl), 0) == lax.broadcasted_iota(I32, (N_EXPERTS, nbl), 1)
    fill_lo = jnp.sum(jnp.where(diag, (pad_start + cnt)[:, 0:1], 0), axis=0, keepdims=True)
    fill_hi = jnp.sum(jnp.where(diag, pad_end[:, 0:1], 0), axis=0, keepdims=True)
    meta_ref[...] = jnp.concatenate([block_e, n_used, fill_lo, fill_hi, jnp.zeros((SUBLANES - 4, nbl), I32)],
                                    axis=0)


def _plan(cnt, tbef, n_blocks):
    nbl = -(-n_blocks // LANES) * LANES
    table = pl.BlockSpec((N_EXPERTS, LANES), lambda i: (0, 0))
    return pl.pallas_call(
        functools.partial(_plan_kernel, nbl=nbl),
        grid=(1,),
        in_specs=[table, table],
        out_specs=[pl.BlockSpec((SUBLANES, nbl), lambda i: (0, 0)), table],
        out_shape=[jax.ShapeDtypeStruct((SUBLANES, nbl), I32),
                   jax.ShapeDtypeStruct((N_EXPERTS, LANES), I32)],
        compiler_params=_cparams("arbitrary"),
        name="moe_plan",
    )(cnt, tbef)


def _rows(ref, row, nrows):
    return ref.at[pl.ds(pl.multiple_of(row * ROW_CHUNKS, ROW_CHUNKS), nrows * ROW_CHUNKS), :]


def _rows_wait(src_hbm, buf, sem):
    pltpu.make_async_copy(src_hbm.at[pl.ds(0, buf.shape[0]), :], buf, sem).wait()


def _copy_pieces(src, src_row, dst, dst_row, count, max_rows, sem, wait=False):
    bit = max_rows.bit_length() - 1
    while bit >= 0:
        size = 1 << bit
        done = (count >> (bit + 1)) << (bit + 1)

        @pl.when(((count >> bit) & 1) == 1)
        def _():
            cp = pltpu.make_async_copy(_rows(src, src_row + done, size), _rows(dst, dst_row + done, size), sem)
            cp.start()
            if wait:
                cp.wait()

        bit -= 1


def _tile_runs(tcnt_ref, toff_ref, rstart_ref, tile, buf, hbm, sem, *, to_hbm, tm):
    def per_expert(e, carry):
        k = tile * N_EXPERTS + e
        if to_hbm:
            _copy_pieces(buf, toff_ref[k], hbm, rstart_ref[k], tcnt_ref[k], tm, sem)
        else:
            _copy_pieces(hbm, rstart_ref[k], buf, toff_ref[k], tcnt_ref[k], tm, sem)
        return carry

    lax.fori_loop(0, N_EXPERTS, per_expert, 0)


def _dispatch_kernel(lpos_ref, tcnt_ref, toff_ref, rstart_ref, flo_ref, fhi_ref, nu_ref, x_ref, xs_hbm,
                     s0, s1, zbuf, sem, zsem, *, n, tm, n_blocks):
    i = pl.program_id(0)
    nt = pl.num_programs(0)
    bufs = (s0, s1)
    unroll = 8

    for slot in range(2):
        @pl.when(i % 2 == slot)
        def _():
            buf = bufs[slot]

            @pl.when(i >= 2)
            def _():
                _rows_wait(xs_hbm, buf, sem.at[slot])

            def place(c, carry):
                for u in range(unroll):
                    t = c * unroll + u
                    v = x_ref[pl.ds(pl.multiple_of(t * ROW_CHUNKS, ROW_CHUNKS), ROW_CHUNKS), :]
                    for k in range(2):
                        p = lpos_ref[k * n + i * tm + t]
                        buf[pl.ds(pl.multiple_of(p * ROW_CHUNKS, ROW_CHUNKS), ROW_CHUNKS), :] = v
                return carry

            lax.fori_loop(0, tm // unroll, place, 0)
            _tile_runs(tcnt_ref, toff_ref, rstart_ref, i, buf, xs_hbm, sem.at[slot], to_hbm=True, tm=tm)

    @pl.when(i == nt - 1)
    def _():
        for slot in range(2):
            @pl.when(nt > slot)
            def _():
                _rows_wait(xs_hbm, bufs[slot], sem.at[slot])

        zbuf[...] = jnp.zeros_like(zbuf)
        for e in range(N_EXPERTS):
            _copy_pieces(zbuf, 0, xs_hbm, flo_ref[e], fhi_ref[e] - flo_ref[e], MOE_BM // 2, zsem, wait=True)

        def zero_block(j, carry):
            cp = pltpu.make_async_copy(zbuf, _rows(xs_hbm, j * MOE_BM, MOE_BM), zsem)
            cp.start()
            cp.wait()
            return carry

        lax.fori_loop(nu_ref[0], n_blocks, zero_block, 0)


def _dispatch(lpos_flat, tcnt, toff, rstart, fill_lo, fill_hi, n_used, xr, n_blocks, tm):
    n = xr.shape[0] // ROW_CHUNKS
    return pl.pallas_call(
        functools.partial(_dispatch_kernel, n=n, tm=tm, n_blocks=n_blocks),
        grid_spec=pltpu.PrefetchScalarGridSpec(
            num_scalar_prefetch=7,
            grid=(n // tm,),
            in_specs=[pl.BlockSpec((tm * ROW_CHUNKS, LANES), lambda i, *_: (i, 0))],
            out_specs=pl.BlockSpec(memory_space=pl.ANY),
            scratch_shapes=[pltpu.VMEM((2 * tm * ROW_CHUNKS, LANES), F32),
                            pltpu.VMEM((2 * tm * ROW_CHUNKS, LANES), F32),
                            pltpu.VMEM((MOE_BM * ROW_CHUNKS, LANES), F32),
                            pltpu.SemaphoreType.DMA((2,)),
                            pltpu.SemaphoreType.DMA(())]),
        out_shape=jax.ShapeDtypeStruct((n_blocks * MOE_BM * ROW_CHUNKS, LANES), F32),
        compiler_params=_cparams("arbitrary"),
        name="moe_dispatch",
    )(lpos_flat, tcnt, toff, rstart, fill_lo, fill_hi, n_used, xr)


def _experts_kernel(be_ref, nu_ref, xs_ref, wg_ref, wu_ref, wd_ref, y_ref, wgb, wub, wdb):
    j = pl.program_id(0)
    used = j < nu_ref[0]

    @pl.when(jnp.logical_and(used, jnp.logical_or(j == 0, be_ref[j] != be_ref[jnp.maximum(j - 1, 0)])))
    def _():
        wgb[...] = wg_ref[0, 0].astype(BF16)
        wub[...] = wu_ref[0, 0].astype(BF16)
        wdb[...] = wd_ref[0, 0].astype(BF16)

    @pl.when(used)
    def _():
        x = _read_rows(xs_ref, MOE_BM).astype(BF16)
        h = _silu(_dot(x, wgb[...])) * _dot(x, wub[...])
        _write_rows(y_ref, _dot(h.astype(BF16), wdb[...]), MOE_BM)

    @pl.when(jnp.logical_not(used))
    def _():
        y_ref[...] = jnp.zeros_like(y_ref)


def _experts(block_e, n_used, xs, wg, wu, wd, layer):
    n_blocks = block_e.shape[0]

    def last_used(j, nu):
        return jnp.minimum(j, nu[0] - 1)

    def wblk(j, be, nu):
        return (layer, be[last_used(j, nu)], 0, 0)

    return pl.pallas_call(
        _experts_kernel,
        grid_spec=pltpu.PrefetchScalarGridSpec(
            num_scalar_prefetch=2,
            grid=(n_blocks,),
            in_specs=[pl.BlockSpec((MOE_BM * ROW_CHUNKS, LANES), lambda j, be, nu: (last_used(j, nu), 0)),
                      pl.BlockSpec((1, 1, D_MODEL, D_EXPERT), wblk),
                      pl.BlockSpec((1, 1, D_MODEL, D_EXPERT), wblk),
                      pl.BlockSpec((1, 1, D_EXPERT, D_MODEL), wblk)],
            out_specs=pl.BlockSpec((MOE_BM * ROW_CHUNKS, LANES), lambda j, be, nu: (j, 0)),
            scratch_shapes=[pltpu.VMEM((D_MODEL, D_EXPERT), BF16), pltpu.VMEM((D_MODEL, D_EXPERT), BF16),
                            pltpu.VMEM((D_EXPERT, D_MODEL), BF16)]),
        out_shape=jax.ShapeDtypeStruct(xs.shape, F32),
        compiler_params=_cparams("arbitrary"),
        name="moe_experts",
    )(block_e, n_used, xs, wg, wu, wd)


def _combine_ln_kernel(lpos_ref, tcnt_ref, toff_ref, rstart_ref, y_hbm, x_ref, w1_ref, w2_ref, g_ref, b_ref, o_ref,
                       r0, r1, u1, u2, sem, *, n, tm):
    i = pl.program_id(0)
    nt = pl.num_programs(0)
    bufs = (r0, r1)
    unroll = 8

    def fetch(tile, slot):
        _tile_runs(tcnt_ref, toff_ref, rstart_ref, tile, bufs[slot], y_hbm, sem.at[slot], to_hbm=False, tm=tm)

    @pl.when(i == 0)
    def _():
        fetch(0, 0)

    for slot in range(2):
        @pl.when(i % 2 == slot)
        def _():
            @pl.when(i + 1 < nt)
            def _():
                fetch(i + 1, 1 - slot)

            buf = bufs[slot]
            _rows_wait(y_hbm, buf, sem.at[slot])

            def place(c, carry):
                for u in range(unroll):
                    t = c * unroll + u
                    dst = pl.ds(pl.multiple_of(t * ROW_CHUNKS, ROW_CHUNKS), ROW_CHUNKS)
                    for k, out in enumerate((u1, u2)):
                        p = lpos_ref[k * n + i * tm + t]
                        out[dst, :] = buf[pl.ds(pl.multiple_of(p * ROW_CHUNKS, ROW_CHUNKS), ROW_CHUNKS), :]
                return carry

            lax.fori_loop(0, tm // unroll, place, 0)
            moe = w1_ref[...] * _read_rows(u1, tm) + w2_ref[...] * _read_rows(u2, tm)
            o_ref[...] = _ln(ALPHA * x_ref[...] + moe, g_ref[...], b_ref[...])


def _combine_ln(lpos_flat, tcnt, toff, rstart, y, x2, w1, w2, g, b, tm):
    n = x2.shape[0]
    return pl.pallas_call(
        functools.partial(_combine_ln_kernel, n=n, tm=tm),
        grid_spec=pltpu.PrefetchScalarGridSpec(
            num_scalar_prefetch=4,
            grid=(n // tm,),
            in_specs=[pl.BlockSpec(memory_space=pl.ANY),
                      pl.BlockSpec((tm, D_MODEL), lambda i, *_: (i, 0)),
                      pl.BlockSpec((tm, 1), lambda i, *_: (i, 0)),
                      pl.BlockSpec((tm, 1), lambda i, *_: (i, 0)),
                      pl.BlockSpec((1, D_MODEL), lambda i, *_: (0, 0)),
                      pl.BlockSpec((1, D_MODEL), lambda i, *_: (0, 0))],
            out_specs=pl.BlockSpec((tm, D_MODEL), lambda i, *_: (i, 0)),
            scratch_shapes=[pltpu.VMEM((2 * tm * ROW_CHUNKS, LANES), F32)] * 2
                           + [pltpu.VMEM((tm * ROW_CHUNKS, LANES), F32)] * 2
                           + [pltpu.SemaphoreType.DMA((2,))]),
        out_shape=jax.ShapeDtypeStruct((n, D_MODEL), F32),
        compiler_params=_cparams("arbitrary"),
        name="moe_combine_ln",
    )(lpos_flat, tcnt, toff, rstart, y, x2, w1, w2, g, b)


def _moe_ln(x2, xr, rw, rb, wg, wu, wd, layer, g, b):
    n = x2.shape[0]
    n_blocks = (2 * n) // MOE_BM + N_EXPERTS
    tm = MOE_TILE
    nt = n // tm
    lpos, w, cnt, tcnt, toff, tbef = _router(x2, rw, rb, tm)
    meta, rstart = _plan(cnt, tbef, n_blocks)
    block_e = meta[0, :n_blocks]
    n_used = meta[1, :1]

    def per_tile(table):
        return table[:, :nt].T.reshape(nt * N_EXPERTS).astype(I32)

    lpos_flat = lpos.reshape(2 * n)
    tcnt, toff, rstart = per_tile(tcnt), per_tile(toff), per_tile(rstart)
    xs = _dispatch(lpos_flat, tcnt, toff, rstart, meta[2, :N_EXPERTS], meta[3, :N_EXPERTS], n_used, xr,
                   n_blocks, tm)
    y = _experts(block_e, n_used, xs, wg, wu, wd, layer)
    return _combine_ln(lpos_flat, tcnt, toff, rstart, y, x2, w[0].reshape(n, 1), w[1].reshape(n, 1), g, b, tm)


def _conv_qkv_kernel(xm_ref, cw_ref, cb_ref, wq_ref, wk_ref, wv_ref, q_ref, k_ref, v_ref, xc_ref, *, s):
    xm_b = xm_ref[0]
    xm = xm_b.astype(F32)
    cw = cw_ref[...]
    row = lax.broadcasted_iota(I32, (s, 1), 0)
    half = CONV_K // 2
    acc = cb_ref[...] + xm * cw[half:half + 1, :]
    for sh in range(1, half + 1):
        past = jnp.where(row >= sh, pltpu.roll(xm, sh, axis=0), 0.0)
        acc = acc + past * cw[half - sh:half - sh + 1, :]
        nxt = jnp.where(row < s - sh, pltpu.roll(xm, s - sh, axis=0), 0.0)
        acc = acc + nxt * cw[half + sh:half + sh + 1, :]
    xc = _silu(acc).astype(BF16)
    xc_ref[0] = xc
    q_ref[0] = _dot(xc, wq_ref[0]).astype(BF16)
    k_ref[0] = (_dot_nt(wk_ref[0], xc) * (ML_DH ** -0.5)).astype(BF16)
    v = _dot(xm_b, wv_ref[0])
    ones_lane = lax.broadcasted_iota(I32, (1, ML_DHP), 1) == ML_DH
    v_ref[0] = jnp.where(ones_lane, 1.0, v).astype(BF16)


def _conv_qkv(main3, cw, cb, wq, wk_t, wv):
    b, s, _ = main3.shape
    tok = pl.BlockSpec((1, s, ML_DHP), lambda i, h: (i, 0, h))
    wspec = pl.BlockSpec((1, ML_DHP, ML_DHP), lambda i, h: (h, 0, 0))
    tok_shape = jax.ShapeDtypeStruct((b, s, ML_WP), BF16)
    return pl.pallas_call(
        functools.partial(_conv_qkv_kernel, s=s),
        grid=(b, ML_HEADS),
        in_specs=[tok,
                  pl.BlockSpec((CONV_K, ML_DHP), lambda i, h: (0, h)),
                  pl.BlockSpec((1, ML_DHP), lambda i, h: (0, h)),
                  wspec, wspec, wspec],
        out_specs=[tok, pl.BlockSpec((1, ML_DHP, s), lambda i, h: (i, h, 0)), tok, tok],
        out_shape=[tok_shape, jax.ShapeDtypeStruct((b, ML_WP, s), BF16), tok_shape, tok_shape],
        compiler_params=_cparams("parallel", "parallel"),
        name="conv_qkv",
    )(main3, cw, cb, wq, wk_t, wv)


def _split3(x):
    hi = x.astype(BF16)
    r1 = x - hi.astype(F32)
    mid = r1.astype(BF16)
    lo = (r1 - mid.astype(F32)).astype(BF16)
    return hi, mid, lo


def _gates_kernel(gc_ref, gr_ref, gbc_ref, gbr_ref, ac_ref, ar_ref, *, s):
    lane = lax.broadcasted_iota(I32, (1, LANES), 1)
    sub = lax.broadcasted_iota(I32, (LANES, 1), 0)
    ti = lax.broadcasted_iota(I32, (CHUNK, CHUNK), 0)
    tj = lax.broadcasted_iota(I32, (CHUNK, CHUNK), 1)
    lower = (tj <= ti).astype(BF16)
    upper = (ti <= tj).astype(BF16)

    def pick(idx, pre, suf, raw):
        fwd = jnp.logical_and(idx >= ML_HEADS, idx < 2 * ML_HEADS)
        bwd = jnp.logical_and(idx >= 3 * ML_HEADS, idx < 4 * ML_HEADS)
        return jnp.where(fwd, pre, jnp.where(bwd, suf, raw))

    def body(c, carry):
        t0 = pl.multiple_of(c * CHUNK, CHUNK)
        g = gc_ref[0, pl.ds(t0, CHUNK), :] + gbc_ref[...]
        ls = jax.nn.log_sigmoid(g)
        pre = sum(_dot(lower, part) for part in _split3(ls))
        suf = jnp.sum(ls, axis=0, keepdims=True) - pre + ls
        ac_ref[0, pl.ds(t0, CHUNK), :] = pick(lane, pre, suf, g)
        g = gr_ref[:, pl.ds(t0, CHUNK)] + gbr_ref[...]
        ls = jax.nn.log_sigmoid(g)
        pre = sum(_dot(part, upper) for part in _split3(ls))
        suf = jnp.sum(ls, axis=1, keepdims=True) - pre + ls
        ar_ref[:, pl.ds(t0, CHUNK)] = pick(sub, pre, suf, g)
        return carry

    lax.fori_loop(0, s // CHUNK, body, 0)


def _gates(gcol3, grow, gbc, gbr):
    b, s, _ = gcol3.shape
    return pl.pallas_call(
        functools.partial(_gates_kernel, s=s),
        grid=(b,),
        in_specs=[pl.BlockSpec((1, s, LANES), lambda i: (i, 0, 0)),
                  pl.BlockSpec((LANES, s), lambda i: (0, i)),
                  pl.BlockSpec((1, LANES), lambda i: (0, 0)),
                  pl.BlockSpec((LANES, 1), lambda i: (0, 0))],
        out_specs=[pl.BlockSpec((1, s, LANES), lambda i: (i, 0, 0)),
                   pl.BlockSpec((LANES, s), lambda i: (0, i))],
        out_shape=[jax.ShapeDtypeStruct((b, s, LANES), F32),
                   jax.ShapeDtypeStruct((LANES, b * s), F32)],
        compiler_params=_cparams("parallel"),
        name="mlstm_gates",
    )(gcol3, grow, gbc, gbr)


def _mlstm_kernel(q_ref, kt_ref, v_ref, gc_ref, gr_ref, z_ref, xc_ref, ng_ref, sk_ref,
                  y_ref, hf_ref, hb_ref, cf_ref, cb_ref, mf_ref, mb_ref, *, s):
    head = pl.program_id(1)
    nc = s // CHUNK
    sub = lax.broadcasted_iota(I32, (LANES, 1), 0)
    gate = lax.broadcasted_iota(I32, (LANES, LANES), 0)
    ti = lax.broadcasted_iota(I32, (CHUNK, CHUNK), 0)
    tj = lax.broadcasted_iota(I32, (CHUNK, CHUNK), 1)

    for ref in (cf_ref, cb_ref, mf_ref, mb_ref):
        ref[...] = jnp.zeros_like(ref)

    def intra(c, rev):
        t0 = pl.multiple_of(c * CHUNK, CHUNK)
        qb = q_ref[0, pl.ds(t0, CHUNK), :]
        kt = kt_ref[0, :, pl.ds(t0, CHUNK)]
        vb = v_ref[0, pl.ds(t0, CHUNK), :]
        gc = gc_ref[0, pl.ds(t0, CHUNK), :]
        gr = gr_ref[:, pl.ds(t0, CHUNK)]
        i_idx = head + (2 * ML_HEADS if rev else 0)
        f_idx = i_idx + ML_HEADS
        allowed = (tj >= ti) if rev else (tj <= ti)
        sel = (gate == f_idx).astype(BF16)
        b_rep = sum(_dot(part, sel) for part in _split3(gc))
        b_row = jnp.sum(jnp.where(sub == f_idx, gr, 0.0), axis=0, keepdims=True)
        i_row = jnp.sum(jnp.where(sub == i_idx, gr, 0.0), axis=0, keepdims=True)
        b_last = b_rep[0:1, :] if rev else b_rep[CHUNK - 1:CHUNK, :]

        d = jnp.where(allowed, b_rep - b_row + i_row, NEG)
        m_in = jnp.max(d, axis=1, keepdims=True)
        sc = _dot(qb, kt) * jnp.exp(d - m_in)
        nd_in = _dot(sc.astype(BF16), vb)
        w_row = b_last - b_row + i_row
        return t0, qb, kt, vb, b_rep, b_last[:, 0:1], m_in, nd_in, w_row

    def twice(a):
        return jnp.concatenate([a, a], axis=1)

    def update(parts, h_ref, c_ref, m_ref):
        t0, qb, kt, vb, b_rep, b_last, m_in, nd_in, w_row = parts
        m = m_ref[...]
        cmat = c_ref[...]
        inter = b_rep + m
        m_t = jnp.maximum(m_in, inter)
        a_in = jnp.exp(m_in - m_t)
        iexp = jnp.exp(inter - m_t)
        nd = twice(a_in) * nd_in + twice(iexp) * _dot(qb, cmat.astype(BF16))
        den = nd[:, ML_DH:ML_DH + 1]
        h_ref[pl.ds(t0, CHUNK), :] = nd * (1.0 / jnp.maximum(jnp.abs(den), jnp.exp(-m_t[:, 0:1])))

        m_new = jnp.maximum(b_last + m, jnp.max(w_row, axis=1, keepdims=True))
        wexp = jnp.exp(w_row - m_new)
        cexp = jnp.exp(b_last + m - m_new)
        kw = (kt.astype(F32) * wexp).astype(BF16)
        c_ref[...] = cexp * cmat + _dot(kw, vb)
        m_ref[...] = m_new

    def step(i, carry):
        fwd = intra(i, False)
        bwd = intra(nc - 1 - i, True)
        update(fwd, hf_ref, cf_ref, mf_ref)
        update(bwd, hb_ref, cb_ref, mb_ref)
        return carry

    lax.fori_loop(0, nc, step, 0)

    real = lax.broadcasted_iota(I32, (1, ML_DHP), 1) < ML_DH
    tb = 256

    def fin(j, carry):
        t0 = pl.multiple_of(j * tb, tb)
        hs = jnp.where(real, hf_ref[pl.ds(t0, tb), :] + hb_ref[pl.ds(t0, tb), :], 0.0)
        mu = jnp.sum(hs, axis=1, keepdims=True) * (1.0 / ML_DH)
        dev = jnp.where(real, hs - mu, 0.0)
        var = jnp.sum(dev * dev, axis=1, keepdims=True) * (1.0 / ML_DH)
        hn = dev * lax.rsqrt(var + LN_EPS) * ng_ref[...]
        xc = xc_ref[0, pl.ds(t0, tb), :].astype(F32)
        z = z_ref[0, pl.ds(t0, tb), :].astype(F32)
        y_ref[0, pl.ds(t0, tb), :] = ((hn + sk_ref[...] * xc) * _silu(z)).astype(BF16)
        return carry

    lax.fori_loop(0, s // tb, fin, 0)


def _mlstm(q, kt, v, gcol3, grow, main3, xc, ng, sk):
    b, s, _ = q.shape
    tok = pl.BlockSpec((1, s, ML_DHP), lambda i, h: (i, 0, h))
    vec = pl.BlockSpec((1, ML_DHP), lambda i, h: (0, h))
    return pl.pallas_call(
        functools.partial(_mlstm_kernel, s=s),
        grid=(b, ML_HEADS),
        in_specs=[tok, pl.BlockSpec((1, ML_DHP, s), lambda i, h: (i, h, 0)), tok,
                  pl.BlockSpec((1, s, LANES), lambda i, h: (i, 0, 0)),
                  pl.BlockSpec((LANES, s), lambda i, h: (0, i)),
                  pl.BlockSpec((1, s, ML_DHP), lambda i, h: (i, 0, ML_HEADS + h)),
                  tok, vec, vec],
        out_specs=tok,
        out_shape=jax.ShapeDtypeStruct((b, s, ML_WP), BF16),
        scratch_shapes=[pltpu.VMEM((s, ML_DHP), F32), pltpu.VMEM((s, ML_DHP), F32),
                        pltpu.VMEM((ML_DHP, ML_DHP), F32), pltpu.VMEM((ML_DHP, ML_DHP), F32),
                        pltpu.VMEM((1, 1), F32), pltpu.VMEM((1, 1), F32)],
        compiler_params=_cparams("parallel", "parallel"),
        name="mlstm",
    )(q, kt, v, gcol3, grow, main3, xc, ng, sk)


def _pad_heads(a, axis):
    a = jnp.moveaxis(a, axis, -1)
    lead = a.shape[:-1]
    a = a.reshape(lead + (ML_HEADS, ML_DH))
    a = jnp.pad(a, [(0, 0)] * len(lead) + [(0, 0), (0, ML_DHP - ML_DH)])
    return jnp.moveaxis(a.reshape(lead + (ML_WP,)), -1, axis)


def kernel(x, mem, mem_ln_g, mem_ln_b, w_mem_kv, router_w, router_b, na_w_in, na_rpb, ml_w_in, ml_conv_w,
           ml_conv_b, ml_w_qkv, ml_gate_b, ml_norm_g, ml_skip, w_out, ln_g, ln_b, exp_w_gate, exp_w_up,
           exp_w_down):
    b, s, d = x.shape
    n = b * s
    nm = mem.shape[1]
    row = lambda a: a.reshape(1, -1)

    mem_k, mem_v = _memkv(mem.reshape(b * nm, d), row(mem_ln_g), row(mem_ln_b), w_mem_kv.astype(BF16))
    mem_k3 = mem_k.reshape(b, nm, MEM_W)
    mem_v3 = mem_v.reshape(b, nm, MEM_W)
    rw_pad = jnp.pad(router_w, ((0, 0), (0, LANES - N_EXPERTS)))
    rw_hi = rw_pad.astype(BF16)
    rw = (rw_hi, (rw_pad - rw_hi.astype(F32)).astype(BF16))
    rb = router_b.reshape(N_EXPERTS, 1)

    x2 = x.reshape(n, d)

    h0 = _proj(x2, na_w_in[0].astype(BF16)).reshape(b, s, 3 * NA_W + MEM_W)
    y_na = _na_attention(h0, _na_bias_table(na_rpb[0]))
    y_mem = _mem_attention(h0, 3 * NA_W // LANES, mem_k3, mem_v3)
    wo = w_out[0].astype(BF16)
    x2, xr = _outproj_ln(y_na.reshape(n, NA_W), y_mem.reshape(n, MEM_W), wo[:NA_W], wo[NA_W:], x2,
                         row(ln_g[0, 0]), row(ln_b[0, 0]))
    x2 = _moe_ln(x2, xr, rw, rb, exp_w_gate, exp_w_up, exp_w_down, 0, row(ln_g[0, 1]), row(ln_b[0, 1]))

    w1 = ml_w_in[0]
    w_main = jnp.concatenate([_pad_heads(w1[:, :ML_W], 1), _pad_heads(w1[:, ML_W:2 * ML_W], 1),
                              w1[:, 2 * ML_W + 4 * ML_HEADS:]], axis=1).astype(BF16)
    w_g = jnp.pad(w1[:, 2 * ML_W:2 * ML_W + 4 * ML_HEADS], ((0, 0), (0, LANES - 4 * ML_HEADS))).astype(BF16)
    main, gcol, grow = _proj_gates(x2, w_main, w_g, w_g.T)
    main3 = main.reshape(b, s, 2 * ML_WP + MEM_W)
    wqkv = jnp.pad(ml_w_qkv[0], ((0, 0), (0, 0), (0, ML_DHP - ML_DH), (0, ML_DHP - ML_DH))).astype(BF16)
    q, k, v, xc = _conv_qkv(main3, _pad_heads(ml_conv_w[0], 1), _pad_heads(row(ml_conv_b[0]), 1),
                            wqkv[0], jnp.swapaxes(wqkv[1], 1, 2), wqkv[2])
    gb = jnp.pad(ml_gate_b[0].reshape(4 * ML_HEADS), (0, LANES - 4 * ML_HEADS))
    acol, arow = _gates(gcol.reshape(b, s, LANES), grow, gb.reshape(1, LANES), gb.reshape(LANES, 1))
    y_ml = _mlstm(q, k, v, acol, arow, main3, xc,
                  _pad_heads(row(ml_norm_g[0]), 1), _pad_heads(row(ml_skip[0]), 1))
    y_mem = _mem_attention(main3, 2 * ML_WP // LANES, mem_k3, mem_v3)
    wo = w_out[1]
    x2, xr = _outproj_ln(y_ml.reshape(n, ML_WP), y_mem.reshape(n, MEM_W), _pad_heads(wo[:ML_W], 0).astype(BF16),
                         wo[ML_W:].astype(BF16), x2, row(ln_g[1, 0]), row(ln_b[1, 0]))
    x2 = _moe_ln(x2, xr, rw, rb, exp_w_gate, exp_w_up, exp_w_down, 1, row(ln_g[1, 1]), row(ln_b[1, 1]))
    return x2.reshape(b, s, d)
```

```python
import functools

import numpy as np
import jax
import jax.numpy as jnp
from jax import lax
from jax.experimental import pallas as pl
from jax.experimental.pallas import tpu as pltpu

F32 = jnp.float32
BF16 = jnp.bfloat16
I32 = jnp.int32

D_MODEL = 1024
DEPTH = 2
GRID_W = 64
MEM_HEADS = 4
MEM_DH = 64
MEM_W = MEM_HEADS * MEM_DH
NA_HEADS = 12
NA_DH = 64
NA_W = NA_HEADS * NA_DH
WIN_H = 8
WIN_W = 16
ML_HEADS = 4
ML_DH = 192
ML_DHP = 256
ML_W = ML_HEADS * ML_DH
ML_WP = ML_HEADS * ML_DHP
CONV_K = 5
CHUNK = 128
N_EXPERTS = 16
N_GROUPS = 4
EXPERTS_PER_GROUP = N_EXPERTS // N_GROUPS
D_EXPERT = 512
ALPHA = (2 * DEPTH) ** 0.25
LN_EPS = 1e-5
NEG = -1e30

LANES = 128
SUBLANES = 8
ROW_CHUNKS = D_MODEL // LANES
MOE_BM = 256
MOE_TILE = 512
ML_HPS = 2
NA_ROWS_PER_STEP = 8
VMEM_LIMIT = 48 * 1024 * 1024


def _cparams(*sem):
    return pltpu.CompilerParams(dimension_semantics=sem, vmem_limit_bytes=VMEM_LIMIT)


def _dot(a, b):
    return jnp.dot(a, b, preferred_element_type=F32)


def _dot_nt(a, b, precision=None):
    return lax.dot_general(a, b, (((1,), (1,)), ((), ())), precision=precision,
                           preferred_element_type=F32)


def _ln(z, g, b):
    mu = jnp.mean(z, axis=-1, keepdims=True)
    zc = z - mu
    var = jnp.mean(zc * zc, axis=-1, keepdims=True)
    return zc * lax.rsqrt(var + LN_EPS) * g + b


def _silu(x):
    return x * jax.nn.sigmoid(x)


def _read_rows(ref, n):
    return jnp.concatenate([ref[pl.ds(j, n, stride=ROW_CHUNKS), :] for j in range(ROW_CHUNKS)], axis=1)


def _write_rows(ref, val, n):
    for j in range(ROW_CHUNKS):
        ref[pl.ds(j, n, stride=ROW_CHUNKS), :] = val[:, j * LANES:(j + 1) * LANES]


def _memkv_kernel(m_ref, g_ref, b_ref, w_ref, k_ref, v_ref):
    z = _ln(m_ref[...], g_ref[...], b_ref[...])
    kv = _dot(z.astype(BF16), w_ref[...])
    k_ref[...] = kv[:, :MEM_W].astype(BF16)
    v_ref[...] = kv[:, MEM_W:].astype(BF16)


def _memkv(mem2, g, b, w):
    n = mem2.shape[0]
    tm = 256
    return pl.pallas_call(
        _memkv_kernel,
        grid=(n // tm,),
        in_specs=[pl.BlockSpec((tm, D_MODEL), lambda i: (i, 0)),
                  pl.BlockSpec((1, D_MODEL), lambda i: (0, 0)),
                  pl.BlockSpec((1, D_MODEL), lambda i: (0, 0)),
                  pl.BlockSpec((D_MODEL, 2 * MEM_W), lambda i: (0, 0))],
        out_specs=[pl.BlockSpec((tm, MEM_W), lambda i: (i, 0)),
                   pl.BlockSpec((tm, MEM_W), lambda i: (i, 0))],
        out_shape=[jax.ShapeDtypeStruct((n, MEM_W), BF16)] * 2,
        compiler_params=_cparams("parallel"),
        name="memkv",
    )(mem2, g, b, w)


def _proj_kernel(x_ref, w_ref, o_ref):
    o_ref[...] = _dot(x_ref[...].astype(BF16), w_ref[...]).astype(o_ref.dtype)


def _proj(x2, w, tm=512):
    n, k = x2.shape
    nout = w.shape[1]
    return pl.pallas_call(
        _proj_kernel,
        grid=(n // tm,),
        in_specs=[pl.BlockSpec((tm, k), lambda i: (i, 0)),
                  pl.BlockSpec((k, nout), lambda i: (0, 0))],
        out_specs=pl.BlockSpec((tm, nout), lambda i: (i, 0)),
        out_shape=jax.ShapeDtypeStruct((n, nout), BF16),
        compiler_params=_cparams("parallel"),
        name="in_proj",
    )(x2, w)


def _proj_gates_kernel(x_ref, w_ref, wg_ref, wgt_ref, o_ref, g_ref, gt_ref):
    xb = x_ref[...].astype(BF16)
    o_ref[...] = _dot(xb, w_ref[...]).astype(BF16)
    g_ref[...] = _dot(xb, wg_ref[...])
    gt_ref[...] = _dot_nt(wgt_ref[...], xb)


def _proj_gates(x2, w, wg, wgt, tm=512):
    n, k = x2.shape
    nout = w.shape[1]
    return pl.pallas_call(
        _proj_gates_kernel,
        grid=(n // tm,),
        in_specs=[pl.BlockSpec((tm, k), lambda i: (i, 0)),
                  pl.BlockSpec((k, nout), lambda i: (0, 0)),
                  pl.BlockSpec((k, LANES), lambda i: (0, 0)),
                  pl.BlockSpec((LANES, k), lambda i: (0, 0))],
        out_specs=[pl.BlockSpec((tm, nout), lambda i: (i, 0)),
                   pl.BlockSpec((tm, LANES), lambda i: (i, 0)),
                   pl.BlockSpec((LANES, tm), lambda i: (0, i))],
        out_shape=[jax.ShapeDtypeStruct((n, nout), BF16),
                   jax.ShapeDtypeStruct((n, LANES), F32),
                   jax.ShapeDtypeStruct((LANES, n), F32)],
        compiler_params=_cparams("parallel"),
        name="in_proj_gates",
    )(x2, w, wg, wgt)


def _softmax_pv(s, v):
    m = jnp.max(s, axis=-1, keepdims=True)
    p = jnp.exp(s - m)
    l = jnp.sum(p, axis=-1, keepdims=True)
    return _dot(p.astype(BF16), v) / l


def _na_kernel(q_ref, k_ref, v_ref, tbl_ref, o_ref, *, rows):
    lane = lax.broadcasted_iota(I32, (1, LANES), 1)
    first = lane < NA_DH
    nkeys = WIN_H * GRID_W

    def rows_step(i, carry):
        rr = [i * NA_ROWS_PER_STEP + u for u in range(NA_ROWS_PER_STEP)]
        rss = [jnp.clip(r - WIN_H // 2, 0, rows - WIN_H) for r in rr]
        scores = []
        for r, rs in zip(rr, rss):
            q = q_ref[0, pl.ds(pl.multiple_of(r * GRID_W, GRID_W), GRID_W), :]
            q = q * jnp.asarray(NA_DH ** -0.5, BF16)
            q2 = jnp.concatenate([jnp.where(first, q, jnp.zeros_like(q)),
                                  jnp.where(first, jnp.zeros_like(q), q)], axis=0)
            k = k_ref[0, pl.ds(pl.multiple_of(rs * GRID_W, GRID_W), nkeys), :]
            dr0 = rs - r + WIN_H - 1
            bias = jnp.concatenate(
                [jnp.concatenate([tbl_ref[0, half, dr0 + 2 * m] for m in range(WIN_H // 2)], axis=1)
                 for half in range(2)], axis=0)
            scores.append(_dot_nt(q2, k) + bias)
        probs = []
        for s in scores:
            p = jnp.exp(s - jnp.max(s, axis=-1, keepdims=True))
            probs.append((p.astype(BF16), jnp.sum(p, axis=-1, keepdims=True)))
        for r, rs, (p, l) in zip(rr, rss, probs):
            v = v_ref[0, pl.ds(pl.multiple_of(rs * GRID_W, GRID_W), nkeys), :]
            o = _dot(p, v) / l
            o = jnp.where(first, o[:GRID_W], o[GRID_W:])
            o_ref[0, pl.ds(pl.multiple_of(r * GRID_W, GRID_W), GRID_W), :] = o.astype(o_ref.dtype)
        return carry

    lax.fori_loop(0, rows // NA_ROWS_PER_STEP, rows_step, 0)


def _na_bias_table(rpb):
    qc = np.arange(GRID_W)[:, None]
    kc = np.arange(GRID_W)[None, :]
    cs = np.clip(qc - WIN_W // 2, 0, GRID_W - WIN_W)
    col_in = (kc >= cs) & (kc < cs + WIN_W)
    side = GRID_W - WIN_W
    wide = jnp.pad(rpb, ((0, 0), (0, 0), (side, side)))
    t = jnp.stack([wide[:, :, GRID_W - 1 - q:2 * GRID_W - 1 - q] for q in range(GRID_W)], axis=2)
    t = jnp.where(col_in, t, NEG).astype(F32)
    t2 = jnp.concatenate([t[:, :-1], t[:, 1:]], axis=-1)
    return t2.reshape(NA_HEADS // 2, 2, 2 * WIN_H - 2, GRID_W, 2 * GRID_W)


def _na_attention(h3, tbl):
    b, s, _ = h3.shape
    rows = s // GRID_W
    npair = NA_HEADS // 2
    return pl.pallas_call(
        functools.partial(_na_kernel, rows=rows),
        grid=(b, npair),
        in_specs=[pl.BlockSpec((1, s, LANES), lambda i, p: (i, 0, p)),
                  pl.BlockSpec((1, s, LANES), lambda i, p: (i, 0, npair + p)),
                  pl.BlockSpec((1, s, LANES), lambda i, p: (i, 0, 2 * npair + p)),
                  pl.BlockSpec((1, 2, 2 * WIN_H - 2, GRID_W, 2 * GRID_W), lambda i, p: (p, 0, 0, 0, 0))],
        out_specs=pl.BlockSpec((1, s, LANES), lambda i, p: (i, 0, p)),
        out_shape=jax.ShapeDtypeStruct((b, s, NA_W), BF16),
        compiler_params=_cparams("parallel", "parallel"),
        name="na_attention",
    )(h3, h3, h3, tbl)


def _mem_attn_kernel(q_ref, k_ref, v_ref, o_ref):
    lane = lax.broadcasted_iota(I32, (1, LANES), 1)
    first = lane < MEM_DH
    q = q_ref[0] * jnp.asarray(MEM_DH ** -0.5, BF16)
    k = k_ref[0]
    v = v_ref[0]
    outs = []
    for half in range(2):
        qh = jnp.where(first if half == 0 else jnp.logical_not(first), q, jnp.zeros_like(q))
        outs.append(_softmax_pv(_dot_nt(qh, k), v))
    o_ref[0] = jnp.where(first, outs[0], outs[1]).astype(o_ref.dtype)


def _mem_attention(h3, col_block0, mem_k3, mem_v3, tq=512):
    b, s, _ = h3.shape
    nm = mem_k3.shape[1]
    npair = MEM_HEADS // 2
    return pl.pallas_call(
        _mem_attn_kernel,
        grid=(b, npair, s // tq),
        in_specs=[pl.BlockSpec((1, tq, LANES), lambda i, p, t: (i, t, col_block0 + p)),
                  pl.BlockSpec((1, nm, LANES), lambda i, p, t: (i, 0, p)),
                  pl.BlockSpec((1, nm, LANES), lambda i, p, t: (i, 0, p))],
        out_specs=pl.BlockSpec((1, tq, LANES), lambda i, p, t: (i, t, p)),
        out_shape=jax.ShapeDtypeStruct((b, s, MEM_W), BF16),
        compiler_params=_cparams("parallel", "parallel", "parallel"),
        name="mem_attention",
    )(h3, mem_k3, mem_v3)


def _outproj_ln_kernel(ya_ref, ym_ref, wa_ref, wm_ref, x_ref, g_ref, b_ref, o_ref, or_ref, *, tm):
    acc = _dot(ya_ref[...], wa_ref[...]) + _dot(ym_ref[...], wm_ref[...])
    out = _ln(ALPHA * x_ref[...] + acc, g_ref[...], b_ref[...])
    o_ref[...] = out
    _write_rows(or_ref, out, tm)


def _outproj_ln(ya, ym, wa, wm, x2, g, b, tm=512):
    n = x2.shape[0]
    ka, km = ya.shape[1], ym.shape[1]
    return pl.pallas_call(
        functools.partial(_outproj_ln_kernel, tm=tm),
        grid=(n // tm,),
        in_specs=[pl.BlockSpec((tm, ka), lambda i: (i, 0)),
                  pl.BlockSpec((tm, km), lambda i: (i, 0)),
                  pl.BlockSpec((ka, D_MODEL), lambda i: (0, 0)),
                  pl.BlockSpec((km, D_MODEL), lambda i: (0, 0)),
                  pl.BlockSpec((tm, D_MODEL), lambda i: (i, 0)),
                  pl.BlockSpec((1, D_MODEL), lambda i: (0, 0)),
                  pl.BlockSpec((1, D_MODEL), lambda i: (0, 0))],
        out_specs=[pl.BlockSpec((tm, D_MODEL), lambda i: (i, 0)),
                   pl.BlockSpec((tm * ROW_CHUNKS, LANES), lambda i: (i, 0))],
        out_shape=[jax.ShapeDtypeStruct((n, D_MODEL), F32),
                   jax.ShapeDtypeStruct((n * ROW_CHUNKS, LANES), F32)],
        compiler_params=_cparams("parallel"),
        name="outproj_ln",
    )(ya, ym, wa, wm, x2, g, b)


def _router_kernel(x_ref, rwh_ref, rwl_ref, rb_ref, lpos_ref, w_ref, cnt_ref, tcnt_ref, toff_ref, tbef_ref, *, tm):
    @pl.when(pl.program_id(0) == 0)
    def _():
        cnt_ref[...] = jnp.zeros_like(cnt_ref)

    x = x_ref[...]
    xh = x.astype(BF16)
    xl = (x - xh.astype(F32)).astype(BF16)
    logits_t = _dot(xh, rwh_ref[...]) + (_dot(xh, rwl_ref[...]) + _dot(xl, rwh_ref[...]))
    logits = logits_t.T[:N_EXPERTS]
    scores = jax.nn.sigmoid(logits)
    biased = scores + rb_ref[...]
    bv = [biased[e:e + 1, :] for e in range(N_EXPERTS)]
    sv = [scores[e:e + 1, :] for e in range(N_EXPERTS)]

    grp = []
    for g in range(N_GROUPS):
        m = bv[g * EXPERTS_PER_GROUP:(g + 1) * EXPERTS_PER_GROUP]
        best = None
        for a in range(EXPERTS_PER_GROUP):
            for c in range(a + 1, EXPERTS_PER_GROUP):
                pair = m[a] + m[c]
                best = pair if best is None else jnp.maximum(best, pair)
        grp.append(best)
    gsel = jnp.zeros((1, tm), I32)
    gbest = grp[0]
    for g in range(1, N_GROUPS):
        better = grp[g] > gbest
        gsel = jnp.where(better, g, gsel)
        gbest = jnp.where(better, grp[g], gbest)

    def pick(vals, j):
        out = vals[j]
        for g in range(1, N_GROUPS):
            out = jnp.where(gsel == g, vals[g * EXPERTS_PER_GROUP + j], out)
        return out

    cb = [pick(bv, j) for j in range(EXPERTS_PER_GROUP)]
    cs = [pick(sv, j) for j in range(EXPERTS_PER_GROUP)]
    i1 = jnp.zeros((1, tm), I32)
    m1 = cb[0]
    s1 = cs[0]
    for j in range(1, EXPERTS_PER_GROUP):
        gt = cb[j] > m1
        i1 = jnp.where(gt, j, i1)
        m1 = jnp.where(gt, cb[j], m1)
        s1 = jnp.where(gt, cs[j], s1)
    i2 = jnp.zeros((1, tm), I32)
    m2 = jnp.full((1, tm), -jnp.inf, F32)
    s2 = jnp.zeros((1, tm), F32)
    for j in range(EXPERTS_PER_GROUP):
        ok = jnp.logical_and(i1 != j, cb[j] > m2)
        i2 = jnp.where(ok, j, i2)
        m2 = jnp.where(ok, cb[j], m2)
        s2 = jnp.where(ok, cs[j], s2)
    e1 = gsel * EXPERTS_PER_GROUP + i1
    e2 = gsel * EXPERTS_PER_GROUP + i2
    tot = s1 + s2
    w_ref[...] = jnp.concatenate([s1 / tot, s2 / tot], axis=0)

    i = pl.program_id(0)
    eio = lax.broadcasted_iota(I32, (N_EXPERTS, tm), 0)
    oh1 = eio == e1
    oh2 = eio == e2
    ohs = jnp.logical_or(oh1, oh2).astype(F32)
    before = (lax.broadcasted_iota(I32, (tm, tm), 0) < lax.broadcasted_iota(I32, (tm, tm), 1))
    pre = _dot(ohs.astype(BF16), before.astype(BF16))
    tile_cnt = jnp.sum(ohs, axis=1, keepdims=True)
    offs = []
    acc = jnp.zeros((1, 1), F32)
    for e in range(N_EXPERTS):
        offs.append(acc)
        acc = acc + tile_cnt[e:e + 1, :]
    tile_off = jnp.concatenate(offs, axis=0)
    pos = tile_off + pre
    p1 = jnp.sum(jnp.where(oh1, pos, 0.0), axis=0, keepdims=True)
    p2 = jnp.sum(jnp.where(oh2, pos, 0.0), axis=0, keepdims=True)
    lpos_ref[...] = jnp.concatenate([p1, p2], axis=0).astype(I32) * ROW_CHUNKS

    @pl.when(i == 0)
    def _():
        for ref in (tcnt_ref, toff_ref, tbef_ref):
            ref[...] = jnp.zeros_like(ref)

    here = lax.broadcasted_iota(I32, (1, LANES), 1) == i
    tcnt_ref[...] = jnp.where(here, tile_cnt, tcnt_ref[...])
    toff_ref[...] = jnp.where(here, tile_off, toff_ref[...])
    tbef_ref[...] = jnp.where(here, cnt_ref[:, 0:1], tbef_ref[...])
    cnt_ref[...] += tile_cnt


def _router(x2, rw, rb, tm):
    n = x2.shape[0]
    assert n // tm <= LANES
    table = pl.BlockSpec((N_EXPERTS, LANES), lambda i: (0, 0))
    return pl.pallas_call(
        functools.partial(_router_kernel, tm=tm),
        grid=(n // tm,),
        in_specs=[pl.BlockSpec((tm, D_MODEL), lambda i: (i, 0)),
                  pl.BlockSpec((D_MODEL, LANES), lambda i: (0, 0)),
                  pl.BlockSpec((D_MODEL, LANES), lambda i: (0, 0)),
                  pl.BlockSpec((N_EXPERTS, 1), lambda i: (0, 0))],
        out_specs=[pl.BlockSpec((2, tm), lambda i: (0, i)),
                   pl.BlockSpec((2, tm), lambda i: (0, i)),
                   table, table, table, table],
        out_shape=[jax.ShapeDtypeStruct((2, n), I32),
                   jax.ShapeDtypeStruct((2, n), F32)]
                  + [jax.ShapeDtypeStruct((N_EXPERTS, LANES), F32)] * 4,
        compiler_params=_cparams("arbitrary"),
        name="router",
    )(x2, rw[0], rw[1], rb)


def _plan_kernel(cnt_ref, tbef_ref, meta_ref, rstart_ref, *, nbl):
    shift = MOE_BM.bit_length() - 1
    cnt = cnt_ref[...].astype(I32)
    padded = ((cnt + (MOE_BM - 1)) >> shift) << shift
    starts = []
    acc = jnp.zeros((1, LANES), I32)
    for e in range(N_EXPERTS):
        starts.append(acc)
        acc = acc + padded[e:e + 1, :]
    pad_start = jnp.concatenate(starts, axis=0)
    pad_end = pad_start + padded
    rstart_ref[...] = pad_start + tbef_ref[...].astype(I32)
    blk0 = lax.broadcasted_iota(I32, (N_EXPERTS, nbl), 1) * MOE_BM
    block_e = jnp.sum((pad_end[:, 0:1] <= blk0).astype(I32), axis=0, keepdims=True)
    block_e = jnp.minimum(block_e, N_EXPERTS - 1)
    n_used = jnp.broadcast_to(acc[:, 0:1] >> shift, (1, nbl))
    diag = lax.broadcasted_iota(I32, (N_EXPERTS, nbl), 0) == lax.broadcasted_iota(I32, (N_EXPERTS, nbl), 1)
    fill_lo = jnp.sum(jnp.where(diag, (pad_start + cnt)[:, 0:1], 0), axis=0, keepdims=True)
    fill_hi = jnp.sum(jnp.where(diag, pad_end[:, 0:1], 0), axis=0, keepdims=True)
    meta_ref[...] = jnp.concatenate([block_e, n_used, fill_lo, fill_hi, jnp.zeros((SUBLANES - 4, nbl), I32)],
                                    axis=0)


def _plan(cnt, tbef, n_blocks):
    nbl = -(-n_blocks // LANES) * LANES
    table = pl.BlockSpec((N_EXPERTS, LANES), lambda i: (0, 0))
    return pl.pallas_call(
        functools.partial(_plan_kernel, nbl=nbl),
        grid=(1,),
        in_specs=[table, table],
        out_specs=[pl.BlockSpec((SUBLANES, nbl), lambda i: (0, 0)), table],
        out_shape=[jax.ShapeDtypeStruct((SUBLANES, nbl), I32),
                   jax.ShapeDtypeStruct((N_EXPERTS, LANES), I32)],
        compiler_params=_cparams("arbitrary"),
        name="moe_plan",
    )(cnt, tbef)


def _rows(ref, row, nrows):
    return ref.at[pl.ds(pl.multiple_of(row * ROW_CHUNKS, ROW_CHUNKS), nrows * ROW_CHUNKS), :]


def _rows_wait(src_hbm, buf, sem):
    pltpu.make_async_copy(src_hbm.at[pl.ds(0, buf.shape[0]), :], buf, sem).wait()


def _copy_pieces(src, src_row, dst, dst_row, count, max_rows, sem, wait=False):
    bit = max_rows.bit_length() - 1
    while bit >= 0:
        size = 1 << bit
        done = (count >> (bit + 1)) << (bit + 1)

        @pl.when(((count >> bit) & 1) == 1)
        def _():
            cp = pltpu.make_async_copy(_rows(src, src_row + done, size), _rows(dst, dst_row + done, size), sem)
            cp.start()
            if wait:
                cp.wait()

        bit -= 1


def _tile_runs(tcnt_ref, toff_ref, rstart_ref, tile, buf, hbm, sem, *, to_hbm, tm):
    def per_expert(e, carry):
        k = tile * N_EXPERTS + e
        if to_hbm:
            _copy_pieces(buf, toff_ref[k], hbm, rstart_ref[k], tcnt_ref[k], tm, sem)
        else:
            _copy_pieces(hbm, rstart_ref[k], buf, toff_ref[k], tcnt_ref[k], tm, sem)
        return carry

    lax.fori_loop(0, N_EXPERTS, per_expert, 0)


def _dispatch_kernel(lpos_ref, tcnt_ref, toff_ref, rstart_ref, flo_ref, fhi_ref, nu_ref, x_ref, xs_hbm,
                     s0, s1, zbuf, sem, zsem, *, n, tm, n_blocks):
    i = pl.program_id(0)
    nt = pl.num_programs(0)
    bufs = (s0, s1)
    unroll = 8

    for slot in range(2):
        @pl.when(i % 2 == slot)
        def _():
            buf = bufs[slot]

            @pl.when(i >= 2)
            def _():
                _rows_wait(xs_hbm, buf, sem.at[slot])

            def place(c, carry):
                tok = i * tm + c * unroll
                src = pl.multiple_of(c * (unroll * ROW_CHUNKS), unroll * ROW_CHUNKS)
                for u in range(unroll):
                    v = x_ref[pl.ds(src + u * ROW_CHUNKS, ROW_CHUNKS), :]
                    for k in range(2):
                        p = lpos_ref[k * n + tok + u]
                        buf[pl.ds(pl.multiple_of(p, ROW_CHUNKS), ROW_CHUNKS), :] = v
                return carry

            lax.fori_loop(0, tm // unroll, place, 0)
            _tile_runs(tcnt_ref, toff_ref, rstart_ref, i, buf, xs_hbm, sem.at[slot], to_hbm=True, tm=tm)

    @pl.when(i == nt - 1)
    def _():
        for slot in range(2):
            @pl.when(nt > slot)
            def _():
                _rows_wait(xs_hbm, bufs[slot], sem.at[slot])

        zbuf[...] = jnp.zeros_like(zbuf)
        for e in range(N_EXPERTS):
            _copy_pieces(zbuf, 0, xs_hbm, flo_ref[e], fhi_ref[e] - flo_ref[e], MOE_BM // 2, zsem, wait=True)

        def zero_block(j, carry):
            cp = pltpu.make_async_copy(zbuf, _rows(xs_hbm, j * MOE_BM, MOE_BM), zsem)
            cp.start()
            cp.wait()
            return carry

        lax.fori_loop(nu_ref[0], n_blocks, zero_block, 0)


def _dispatch(lpos_flat, tcnt, toff, rstart, fill_lo, fill_hi, n_used, xr, n_blocks, tm):
    n = xr.shape[0] // ROW_CHUNKS
    return pl.pallas_call(
        functools.partial(_dispatch_kernel, n=n, tm=tm, n_blocks=n_blocks),
        grid_spec=pltpu.PrefetchScalarGridSpec(
            num_scalar_prefetch=7,
            grid=(n // tm,),
            in_specs=[pl.BlockSpec((tm * ROW_CHUNKS, LANES), lambda i, *_: (i, 0))],
            out_specs=pl.BlockSpec(memory_space=pl.ANY),
            scratch_shapes=[pltpu.VMEM((2 * tm * ROW_CHUNKS, LANES), F32),
                            pltpu.VMEM((2 * tm * ROW_CHUNKS, LANES), F32),
                            pltpu.VMEM((MOE_BM * ROW_CHUNKS, LANES), F32),
                            pltpu.SemaphoreType.DMA((2,)),
                            pltpu.SemaphoreType.DMA(())]),
        out_shape=jax.ShapeDtypeStruct((n_blocks * MOE_BM * ROW_CHUNKS, LANES), F32),
        compiler_params=_cparams("arbitrary"),
        name="moe_dispatch",
    )(lpos_flat, tcnt, toff, rstart, fill_lo, fill_hi, n_used, xr)


def _experts_kernel(be_ref, nu_ref, xs_ref, wg_ref, wu_ref, wd_ref, y_ref, wgb, wub, wdb):
    j = pl.program_id(0)
    used = j < nu_ref[0]

    @pl.when(jnp.logical_and(used, jnp.logical_or(j == 0, be_ref[j] != be_ref[jnp.maximum(j - 1, 0)])))
    def _():
        wgb[...] = wg_ref[0, 0].astype(BF16)
        wub[...] = wu_ref[0, 0].astype(BF16)
        wdb[...] = wd_ref[0, 0].astype(BF16)

    @pl.when(used)
    def _():
        x = _read_rows(xs_ref, MOE_BM).astype(BF16)
        h = _silu(_dot(x, wgb[...])) * _dot(x, wub[...])
        _write_rows(y_ref, _dot(h.astype(BF16), wdb[...]), MOE_BM)

    @pl.when(jnp.logical_not(used))
    def _():
        y_ref[...] = jnp.zeros_like(y_ref)


def _experts(block_e, n_used, xs, wg, wu, wd, layer):
    n_blocks = block_e.shape[0]

    def last_used(j, nu):
        return jnp.minimum(j, nu[0] - 1)

    def wblk(j, be, nu):
        return (layer, be[last_used(j, nu)], 0, 0)

    return pl.pallas_call(
        _experts_kernel,
        grid_spec=pltpu.PrefetchScalarGridSpec(
            num_scalar_prefetch=2,
            grid=(n_blocks,),
            in_specs=[pl.BlockSpec((MOE_BM * ROW_CHUNKS, LANES), lambda j, be, nu: (last_used(j, nu), 0)),
                      pl.BlockSpec((1, 1, D_MODEL, D_EXPERT), wblk),
                      pl.BlockSpec((1, 1, D_MODEL, D_EXPERT), wblk),
                      pl.BlockSpec((1, 1, D_EXPERT, D_MODEL), wblk)],
            out_specs=pl.BlockSpec((MOE_BM * ROW_CHUNKS, LANES), lambda j, be, nu: (j, 0)),
            scratch_shapes=[pltpu.VMEM((D_MODEL, D_EXPERT), BF16), pltpu.VMEM((D_MODEL, D_EXPERT), BF16),
                            pltpu.VMEM((D_EXPERT, D_MODEL), BF16)]),
        out_shape=jax.ShapeDtypeStruct(xs.shape, F32),
        compiler_params=_cparams("arbitrary"),
        name="moe_experts",
    )(block_e, n_used, xs, wg, wu, wd)


def _combine_ln_kernel(lpos_ref, tcnt_ref, toff_ref, rstart_ref, y_hbm, x_ref, w1_ref, w2_ref, g_ref, b_ref, o_ref,
                       r0, r1, u1, u2, sem, *, n, tm):
    i = pl.program_id(0)
    nt = pl.num_programs(0)
    bufs = (r0, r1)
    unroll = 8

    def fetch(tile, slot):
        _tile_runs(tcnt_ref, toff_ref, rstart_ref, tile, bufs[slot], y_hbm, sem.at[slot], to_hbm=False, tm=tm)

    @pl.when(i == 0)
    def _():
        fetch(0, 0)

    for slot in range(2):
        @pl.when(i % 2 == slot)
        def _():
            @pl.when(i + 1 < nt)
            def _():
                fetch(i + 1, 1 - slot)

            buf = bufs[slot]
            _rows_wait(y_hbm, buf, sem.at[slot])

            def place(c, carry):
                tok = i * tm + c * unroll
                dst0 = pl.multiple_of(c * (unroll * ROW_CHUNKS), unroll * ROW_CHUNKS)
                for u in range(unroll):
                    dst = pl.ds(dst0 + u * ROW_CHUNKS, ROW_CHUNKS)
                    for k, out in enumerate((u1, u2)):
                        p = lpos_ref[k * n + tok + u]
                        out[dst, :] = buf[pl.ds(pl.multiple_of(p, ROW_CHUNKS), ROW_CHUNKS), :]
                return carry

            lax.fori_loop(0, tm // unroll, place, 0)
            moe = w1_ref[...] * _read_rows(u1, tm) + w2_ref[...] * _read_rows(u2, tm)
            o_ref[...] = _ln(ALPHA * x_ref[...] + moe, g_ref[...], b_ref[...])


def _combine_ln(lpos_flat, tcnt, toff, rstart, y, x2, w1, w2, g, b, tm):
    n = x2.shape[0]
    return pl.pallas_call(
        functools.partial(_combine_ln_kernel, n=n, tm=tm),
        grid_spec=pltpu.PrefetchScalarGridSpec(
            num_scalar_prefetch=4,
            grid=(n // tm,),
            in_specs=[pl.BlockSpec(memory_space=pl.ANY),
                      pl.BlockSpec((tm, D_MODEL), lambda i, *_: (i, 0)),
                      pl.BlockSpec((tm, 1), lambda i, *_: (i, 0)),
                      pl.BlockSpec((tm, 1), lambda i, *_: (i, 0)),
                      pl.BlockSpec((1, D_MODEL), lambda i, *_: (0, 0)),
                      pl.BlockSpec((1, D_MODEL), lambda i, *_: (0, 0))],
            out_specs=pl.BlockSpec((tm, D_MODEL), lambda i, *_: (i, 0)),
            scratch_shapes=[pltpu.VMEM((2 * tm * ROW_CHUNKS, LANES), F32)] * 2
                           + [pltpu.VMEM((tm * ROW_CHUNKS, LANES), F32)] * 2
                           + [pltpu.SemaphoreType.DMA((2,))]),
        out_shape=jax.ShapeDtypeStruct((n, D_MODEL), F32),
        compiler_params=_cparams("arbitrary"),
        name="moe_combine_ln",
    )(lpos_flat, tcnt, toff, rstart, y, x2, w1, w2, g, b)


def _moe_ln(x2, xr, rw, rb, wg, wu, wd, layer, g, b):
    n = x2.shape[0]
    n_blocks = (2 * n) // MOE_BM + N_EXPERTS
    tm = MOE_TILE
    nt = n // tm
    lpos, w, cnt, tcnt, toff, tbef = _router(x2, rw, rb, tm)
    meta, rstart = _plan(cnt, tbef, n_blocks)
    block_e = meta[0, :n_blocks]
    n_used = meta[1, :1]

    def per_tile(table):
        return table[:, :nt].T.reshape(nt * N_EXPERTS).astype(I32)

    lpos_flat = lpos.reshape(2 * n)
    tcnt, toff, rstart = per_tile(tcnt), per_tile(toff), per_tile(rstart)
    xs = _dispatch(lpos_flat, tcnt, toff, rstart, meta[2, :N_EXPERTS], meta[3, :N_EXPERTS], n_used, xr,
                   n_blocks, tm)
    y = _experts(block_e, n_used, xs, wg, wu, wd, layer)
    return _combine_ln(lpos_flat, tcnt, toff, rstart, y, x2, w[0].reshape(n, 1), w[1].reshape(n, 1), g, b, tm)


def _conv_qkv_kernel(xm_ref, cw_ref, cb_ref, wq_ref, wk_ref, wv_ref, q_ref, k_ref, v_ref, xc_ref, *, s):
    xm_b = xm_ref[0]
    xm = xm_b.astype(F32)
    cw = cw_ref[...]
    row = lax.broadcasted_iota(I32, (s, 1), 0)
    half = CONV_K // 2
    acc = cb_ref[...] + xm * cw[half:half + 1, :]
    for sh in range(1, half + 1):
        past = jnp.where(row >= sh, pltpu.roll(xm, sh, axis=0), 0.0)
        acc = acc + past * cw[half - sh:half - sh + 1, :]
        nxt = jnp.where(row < s - sh, pltpu.roll(xm, s - sh, axis=0), 0.0)
        acc = acc + nxt * cw[half + sh:half + sh + 1, :]
    xc = _silu(acc).astype(BF16)
    xc_ref[0] = xc
    q_ref[0] = _dot(xc, wq_ref[0]).astype(BF16)
    k_ref[0] = (_dot_nt(wk_ref[0], xc) * (ML_DH ** -0.5)).astype(BF16)
    v = _dot(xm_b, wv_ref[0])
    ones_lane = lax.broadcasted_iota(I32, (1, ML_DHP), 1) == ML_DH
    v_ref[0] = jnp.where(ones_lane, 1.0, v).astype(BF16)


def _conv_qkv(main3, cw, cb, wq, wk_t, wv):
    b, s, _ = main3.shape
    tok = pl.BlockSpec((1, s, ML_DHP), lambda i, h: (i, 0, h))
    wspec = pl.BlockSpec((1, ML_DHP, ML_DHP), lambda i, h: (h, 0, 0))
    tok_shape = jax.ShapeDtypeStruct((b, s, ML_WP), BF16)
    return pl.pallas_call(
        functools.partial(_conv_qkv_kernel, s=s),
        grid=(b, ML_HEADS),
        in_specs=[tok,
                  pl.BlockSpec((CONV_K, ML_DHP), lambda i, h: (0, h)),
                  pl.BlockSpec((1, ML_DHP), lambda i, h: (0, h)),
                  wspec, wspec, wspec],
        out_specs=[tok, pl.BlockSpec((1, ML_DHP, s), lambda i, h: (i, h, 0)), tok, tok],
        out_shape=[tok_shape, jax.ShapeDtypeStruct((b, ML_WP, s), BF16), tok_shape, tok_shape],
        compiler_params=_cparams("parallel", "parallel"),
        name="conv_qkv",
    )(main3, cw, cb, wq, wk_t, wv)


def _split3(x):
    hi = x.astype(BF16)
    r1 = x - hi.astype(F32)
    mid = r1.astype(BF16)
    lo = (r1 - mid.astype(F32)).astype(BF16)
    return hi, mid, lo


def _gates_kernel(gc_ref, gr_ref, gbc_ref, gbr_ref, ac_ref, ar_ref, *, s):
    lane = lax.broadcasted_iota(I32, (1, LANES), 1)
    sub = lax.broadcasted_iota(I32, (LANES, 1), 0)
    ti = lax.broadcasted_iota(I32, (CHUNK, CHUNK), 0)
    tj = lax.broadcasted_iota(I32, (CHUNK, CHUNK), 1)
    lower = (tj <= ti).astype(BF16)
    upper = (ti <= tj).astype(BF16)

    def pick(idx, pre, suf, raw):
        fwd = jnp.logical_and(idx >= ML_HEADS, idx < 2 * ML_HEADS)
        bwd = jnp.logical_and(idx >= 3 * ML_HEADS, idx < 4 * ML_HEADS)
        return jnp.where(fwd, pre, jnp.where(bwd, suf, raw))

    def body(c, carry):
        t0 = pl.multiple_of(c * CHUNK, CHUNK)
        g = gc_ref[0, pl.ds(t0, CHUNK), :] + gbc_ref[...]
        ls = jax.nn.log_sigmoid(g)
        pre = sum(_dot(lower, part) for part in _split3(ls))
        suf = jnp.sum(ls, axis=0, keepdims=True) - pre + ls
        ac_ref[0, pl.ds(t0, CHUNK), :] = pick(lane, pre, suf, g)
        g = gr_ref[:, pl.ds(t0, CHUNK)] + gbr_ref[...]
        ls = jax.nn.log_sigmoid(g)
        pre = sum(_dot(part, upper) for part in _split3(ls))
        suf = jnp.sum(ls, axis=1, keepdims=True) - pre + ls
        ar_ref[:, pl.ds(t0, CHUNK)] = pick(sub, pre, suf, g)
        return carry

    lax.fori_loop(0, s // CHUNK, body, 0)


def _gates(gcol3, grow, gbc, gbr):
    b, s, _ = gcol3.shape
    return pl.pallas_call(
        functools.partial(_gates_kernel, s=s),
        grid=(b,),
        in_specs=[pl.BlockSpec((1, s, LANES), lambda i: (i, 0, 0)),
                  pl.BlockSpec((LANES, s), lambda i: (0, i)),
                  pl.BlockSpec((1, LANES), lambda i: (0, 0)),
                  pl.BlockSpec((LANES, 1), lambda i: (0, 0))],
        out_specs=[pl.BlockSpec((1, s, LANES), lambda i: (i, 0, 0)),
                   pl.BlockSpec((LANES, s), lambda i: (0, i))],
        out_shape=[jax.ShapeDtypeStruct((b, s, LANES), F32),
                   jax.ShapeDtypeStruct((LANES, b * s), F32)],
        compiler_params=_cparams("parallel"),
        name="mlstm_gates",
    )(gcol3, grow, gbc, gbr)


def _mlstm_kernel(q_ref, kt_ref, v_ref, gc_ref, gr_ref, z_ref, xc_ref, ng_ref, sk_ref,
                  y_ref, hf_ref, hb_ref, cf_ref, cb_ref, m_ref, *, s):
    head0 = pl.program_id(1) * ML_HPS
    nc = s // CHUNK
    sub = lax.broadcasted_iota(I32, (LANES, 1), 0)
    gate = lax.broadcasted_iota(I32, (LANES, LANES), 0)
    ti = lax.broadcasted_iota(I32, (CHUNK, CHUNK), 0)
    tj = lax.broadcasted_iota(I32, (CHUNK, CHUNK), 1)

    for ref in (cf_ref, cb_ref, m_ref):
        ref[...] = jnp.zeros_like(ref)

    def intra(c, j, rev):
        t0 = pl.multiple_of(c * CHUNK, CHUNK)
        hl = slice(j * ML_DHP, (j + 1) * ML_DHP)
        qb = q_ref[0, pl.ds(t0, CHUNK), hl]
        kt = kt_ref[0, hl, pl.ds(t0, CHUNK)]
        vb = v_ref[0, pl.ds(t0, CHUNK), hl]
        gc = gc_ref[0, pl.ds(t0, CHUNK), :]
        gr = gr_ref[:, pl.ds(t0, CHUNK)]
        i_idx = head0 + j + (2 * ML_HEADS if rev else 0)
        f_idx = i_idx + ML_HEADS
        allowed = (tj >= ti) if rev else (tj <= ti)
        sel = (gate == f_idx).astype(BF16)
        b_rep = sum(_dot(part, sel) for part in _split3(gc))
        b_row = jnp.sum(jnp.where(sub == f_idx, gr, 0.0), axis=0, keepdims=True)
        i_row = jnp.sum(jnp.where(sub == i_idx, gr, 0.0), axis=0, keepdims=True)
        b_last = b_rep[0:1, :] if rev else b_rep[CHUNK - 1:CHUNK, :]

        d = jnp.where(allowed, b_rep - b_row + i_row, NEG)
        m_in = jnp.max(d, axis=1, keepdims=True)
        sc = _dot(qb, kt) * jnp.exp(d - m_in)
        nd_in = _dot(sc.astype(BF16), vb)
        w_row = b_last - b_row + i_row
        return t0, qb, kt, vb, b_rep, b_last[:, 0:1], m_in, nd_in, w_row

    def twice(a):
        return jnp.concatenate([a, a], axis=1)

    def update(parts, j, rev):
        t0, qb, kt, vb, b_rep, b_last, m_in, nd_in, w_row = parts
        h_ref, c_ref = (hb_ref, cb_ref) if rev else (hf_ref, cf_ref)
        hl = slice(j * ML_DHP, (j + 1) * ML_DHP)
        mrow = 2 * j + int(rev)
        m = m_ref[mrow:mrow + 1, 0:1]
        cmat = c_ref[j]
        inter = b_rep + m
        m_t = jnp.maximum(m_in, inter)
        a_in = jnp.exp(m_in - m_t)
        iexp = jnp.exp(inter - m_t)
        nd = twice(a_in) * nd_in + twice(iexp) * _dot(qb, cmat.astype(BF16))
        den = nd[:, ML_DH:ML_DH + 1]
        h_ref[pl.ds(t0, CHUNK), hl] = nd * (1.0 / jnp.maximum(jnp.abs(den), jnp.exp(-m_t[:, 0:1])))

        m_new = jnp.maximum(b_last + m, jnp.max(w_row, axis=1, keepdims=True))
        wexp = jnp.exp(w_row - m_new)
        cexp = jnp.exp(b_last + m - m_new)
        kw = (kt.astype(F32) * wexp).astype(BF16)
        c_ref[j] = cexp * cmat + _dot(kw, vb)
        m_ref[mrow:mrow + 1, :] = jnp.broadcast_to(m_new, (1, LANES))

    def step(i, carry):
        chains = [(j, rev) for j in range(ML_HPS) for rev in (False, True)]
        parts = [intra(nc - 1 - i if rev else i, j, rev) for j, rev in chains]
        for p, (j, rev) in zip(parts, chains):
            update(p, j, rev)
        return carry

    lax.fori_loop(0, nc, step, 0)

    real = lax.broadcasted_iota(I32, (1, ML_DHP), 1) < ML_DH
    tb = 256

    def fin(c, carry):
        t0 = pl.multiple_of(c * tb, tb)
        for j in range(ML_HPS):
            hl = slice(j * ML_DHP, (j + 1) * ML_DHP)
            hs = jnp.where(real, hf_ref[pl.ds(t0, tb), hl] + hb_ref[pl.ds(t0, tb), hl], 0.0)
            mu = jnp.sum(hs, axis=1, keepdims=True) * (1.0 / ML_DH)
            dev = jnp.where(real, hs - mu, 0.0)
            var = jnp.sum(dev * dev, axis=1, keepdims=True) * (1.0 / ML_DH)
            hn = dev * lax.rsqrt(var + LN_EPS) * ng_ref[:, hl]
            xc = xc_ref[0, pl.ds(t0, tb), hl].astype(F32)
            z = z_ref[0, pl.ds(t0, tb), hl].astype(F32)
            y_ref[0, pl.ds(t0, tb), hl] = ((hn + sk_ref[:, hl] * xc) * _silu(z)).astype(BF16)
        return carry

    lax.fori_loop(0, s // tb, fin, 0)


def _mlstm(q, kt, v, gcol3, grow, main3, xc, ng, sk):
    b, s, _ = q.shape
    width = ML_HPS * ML_DHP
    steps = ML_HEADS // ML_HPS
    tok = pl.BlockSpec((1, s, width), lambda i, h: (i, 0, h))
    vec = pl.BlockSpec((1, width), lambda i, h: (0, h))
    return pl.pallas_call(
        functools.partial(_mlstm_kernel, s=s),
        grid=(b, steps),
        in_specs=[tok, pl.BlockSpec((1, width, s), lambda i, h: (i, h, 0)), tok,
                  pl.BlockSpec((1, s, LANES), lambda i, h: (i, 0, 0)),
                  pl.BlockSpec((LANES, s), lambda i, h: (0, i)),
                  pl.BlockSpec((1, s, width), lambda i, h: (i, 0, steps + h)),
                  tok, vec, vec],
        out_specs=tok,
        out_shape=jax.ShapeDtypeStruct((b, s, ML_WP), BF16),
        scratch_shapes=[pltpu.VMEM((s, width), F32), pltpu.VMEM((s, width), F32),
                        pltpu.VMEM((ML_HPS, ML_DHP, ML_DHP), F32), pltpu.VMEM((ML_HPS, ML_DHP, ML_DHP), F32),
                        pltpu.VMEM((SUBLANES, LANES), F32)],
        compiler_params=_cparams("parallel", "parallel"),
        name="mlstm",
    )(q, kt, v, gcol3, grow, main3, xc, ng, sk)


def _pad_heads(a, axis):
    a = jnp.moveaxis(a, axis, -1)
    lead = a.shape[:-1]
    a = a.reshape(lead + (ML_HEADS, ML_DH))
    a = jnp.pad(a, [(0, 0)] * len(lead) + [(0, 0), (0, ML_DHP - ML_DH)])
    return jnp.moveaxis(a.reshape(lead + (ML_WP,)), -1, axis)


def kernel(x, mem, mem_ln_g, mem_ln_b, w_mem_kv, router_w, router_b, na_w_in, na_rpb, ml_w_in, ml_conv_w,
           ml_conv_b, ml_w_qkv, ml_gate_b, ml_norm_g, ml_skip, w_out, ln_g, ln_b, exp_w_gate, exp_w_up,
           exp_w_down):
    b, s, d = x.shape
    n = b * s
    nm = mem.shape[1]
    row = lambda a: a.reshape(1, -1)

    mem_k, mem_v = _memkv(mem.reshape(b * nm, d), row(mem_ln_g), row(mem_ln_b), w_mem_kv.astype(BF16))
    mem_k3 = mem_k.reshape(b, nm, MEM_W)
    mem_v3 = mem_v.reshape(b, nm, MEM_W)
    rw_pad = jnp.pad(router_w, ((0, 0), (0, LANES - N_EXPERTS)))
    rw_hi = rw_pad.astype(BF16)
    rw = (rw_hi, (rw_pad - rw_hi.astype(F32)).astype(BF16))
    rb = router_b.reshape(N_EXPERTS, 1)

    x2 = x.reshape(n, d)

    h0 = _proj(x2, na_w_in[0].astype(BF16)).reshape(b, s, 3 * NA_W + MEM_W)
    y_na = _na_attention(h0, _na_bias_table(na_rpb[0]))
    y_mem = _mem_attention(h0, 3 * NA_W // LANES, mem_k3, mem_v3)
    wo = w_out[0].astype(BF16)
    x2, xr = _outproj_ln(y_na.reshape(n, NA_W), y_mem.reshape(n, MEM_W), wo[:NA_W], wo[NA_W:], x2,
                         row(ln_g[0, 0]), row(ln_b[0, 0]))
    x2 = _moe_ln(x2, xr, rw, rb, exp_w_gate, exp_w_up, exp_w_down, 0, row(ln_g[0, 1]), row(ln_b[0, 1]))

    w1 = ml_w_in[0]
    w_main = jnp.concatenate([_pad_heads(w1[:, :ML_W], 1), _pad_heads(w1[:, ML_W:2 * ML_W], 1),
                              w1[:, 2 * ML_W + 4 * ML_HEADS:]], axis=1).astype(BF16)
    w_g = jnp.pad(w1[:, 2 * ML_W:2 * ML_W + 4 * ML_HEADS], ((0, 0), (0, LANES - 4 * ML_HEADS))).astype(BF16)
    main, gcol, grow = _proj_gates(x2, w_main, w_g, w_g.T)
    main3 = main.reshape(b, s, 2 * ML_WP + MEM_W)
    wqkv = jnp.pad(ml_w_qkv[0], ((0, 0), (0, 0), (0, ML_DHP - ML_DH), (0, ML_DHP - ML_DH))).astype(BF16)
    q, k, v, xc = _conv_qkv(main3, _pad_heads(ml_conv_w[0], 1), _pad_heads(row(ml_conv_b[0]), 1),
                            wqkv[0], jnp.swapaxes(wqkv[1], 1, 2), wqkv[2])
    gb = jnp.pad(ml_gate_b[0].reshape(4 * ML_HEADS), (0, LANES - 4 * ML_HEADS))
    acol, arow = _gates(gcol.reshape(b, s, LANES), grow, gb.reshape(1, LANES), gb.reshape(LANES, 1))
    y_ml = _mlstm(q, k, v, acol, arow, main3, xc,
                  _pad_heads(row(ml_norm_g[0]), 1), _pad_heads(row(ml_skip[0]), 1))
    y_mem = _mem_attention(main3, 2 * ML_WP // LANES, mem_k3, mem_v3)
    wo = w_out[1]
    x2, xr = _outproj_ln(y_ml.reshape(n, ML_WP), y_mem.reshape(n, MEM_W), _pad_heads(wo[:ML_W], 0).astype(BF16),
                         wo[ML_W:].astype(BF16), x2, row(ln_g[1, 0]), row(ln_b[1, 0]))
    x2 = _moe_ln(x2, xr, rw, rb, exp_w_gate, exp_w_up, exp_w_down, 1, row(ln_g[1, 1]), row(ln_b[1, 1]))
    return x2.reshape(b, s, d)
```

```python
import functools

import numpy as np
import jax
import jax.numpy as jnp
from jax import lax
from jax.experimental import pallas as pl
from jax.experimental.pallas import tpu as pltpu

F32 = jnp.float32
BF16 = jnp.bfloat16
I32 = jnp.int32

D_MODEL = 1024
DEPTH = 2
GRID_W = 64
MEM_HEADS = 4
MEM_DH = 64
MEM_W = MEM_HEADS * MEM_DH
NA_HEADS = 12
NA_DH = 64
NA_W = NA_HEADS * NA_DH
WIN_H = 8
WIN_W = 16
ML_HEADS = 4
ML_DH = 192
ML_DHP = 256
ML_W = ML_HEADS * ML_DH
ML_WP = ML_HEADS * ML_DHP
CONV_K = 5
CHUNK = 128
N_EXPERTS = 16
N_GROUPS = 4
EXPERTS_PER_GROUP = N_EXPERTS // N_GROUPS
D_EXPERT = 512
ALPHA = (2 * DEPTH) ** 0.25
LN_EPS = 1e-5
NEG = -1e30

LANES = 128
SUBLANES = 8
ROW_CHUNKS = D_MODEL // LANES
MOE_BM = 512
MOE_TILE = 512
ML_HPS = 2
NA_ROWS_PER_STEP = 8
VMEM_LIMIT = 48 * 1024 * 1024


def _cparams(*sem):
    return pltpu.CompilerParams(dimension_semantics=sem, vmem_limit_bytes=VMEM_LIMIT)


def _dot(a, b):
    return jnp.dot(a, b, preferred_element_type=F32)


def _dot_nt(a, b, precision=None):
    return lax.dot_general(a, b, (((1,), (1,)), ((), ())), precision=precision,
                           preferred_element_type=F32)


def _ln(z, g, b):
    mu = jnp.mean(z, axis=-1, keepdims=True)
    zc = z - mu
    var = jnp.mean(zc * zc, axis=-1, keepdims=True)
    return zc * lax.rsqrt(var + LN_EPS) * g + b


def _silu(x):
    return x * jax.nn.sigmoid(x)


def _read_rows(ref, n):
    return jnp.concatenate([ref[pl.ds(j, n, stride=ROW_CHUNKS), :] for j in range(ROW_CHUNKS)], axis=1)


def _write_rows(ref, val, n):
    for j in range(ROW_CHUNKS):
        ref[pl.ds(j, n, stride=ROW_CHUNKS), :] = val[:, j * LANES:(j + 1) * LANES]


def _memkv_kernel(m_ref, g_ref, b_ref, w_ref, k_ref, v_ref):
    z = _ln(m_ref[...], g_ref[...], b_ref[...])
    kv = _dot(z.astype(BF16), w_ref[...])
    k_ref[...] = kv[:, :MEM_W].astype(BF16)
    v_ref[...] = kv[:, MEM_W:].astype(BF16)


def _memkv(mem2, g, b, w):
    n = mem2.shape[0]
    tm = 256
    return pl.pallas_call(
        _memkv_kernel,
        grid=(n // tm,),
        in_specs=[pl.BlockSpec((tm, D_MODEL), lambda i: (i, 0)),
                  pl.BlockSpec((1, D_MODEL), lambda i: (0, 0)),
                  pl.BlockSpec((1, D_MODEL), lambda i: (0, 0)),
                  pl.BlockSpec((D_MODEL, 2 * MEM_W), lambda i: (0, 0))],
        out_specs=[pl.BlockSpec((tm, MEM_W), lambda i: (i, 0)),
                   pl.BlockSpec((tm, MEM_W), lambda i: (i, 0))],
        out_shape=[jax.ShapeDtypeStruct((n, MEM_W), BF16)] * 2,
        compiler_params=_cparams("parallel"),
        name="memkv",
    )(mem2, g, b, w)


def _proj_kernel(x_ref, w_ref, o_ref):
    o_ref[...] = _dot(x_ref[...].astype(BF16), w_ref[...]).astype(o_ref.dtype)


def _proj(x2, w, tm=512):
    n, k = x2.shape
    nout = w.shape[1]
    return pl.pallas_call(
        _proj_kernel,
        grid=(n // tm,),
        in_specs=[pl.BlockSpec((tm, k), lambda i: (i, 0)),
                  pl.BlockSpec((k, nout), lambda i: (0, 0))],
        out_specs=pl.BlockSpec((tm, nout), lambda i: (i, 0)),
        out_shape=jax.ShapeDtypeStruct((n, nout), BF16),
        compiler_params=_cparams("parallel"),
        name="in_proj",
    )(x2, w)


def _proj_gates_kernel(x_ref, w_ref, wg_ref, wgt_ref, o_ref, g_ref, gt_ref):
    xb = x_ref[...].astype(BF16)
    o_ref[...] = _dot(xb, w_ref[...]).astype(BF16)
    g_ref[...] = _dot(xb, wg_ref[...])
    gt_ref[...] = _dot_nt(wgt_ref[...], xb)


def _proj_gates(x2, w, wg, wgt, tm=512):
    n, k = x2.shape
    nout = w.shape[1]
    return pl.pallas_call(
        _proj_gates_kernel,
        grid=(n // tm,),
        in_specs=[pl.BlockSpec((tm, k), lambda i: (i, 0)),
                  pl.BlockSpec((k, nout), lambda i: (0, 0)),
                  pl.BlockSpec((k, LANES), lambda i: (0, 0)),
                  pl.BlockSpec((LANES, k), lambda i: (0, 0))],
        out_specs=[pl.BlockSpec((tm, nout), lambda i: (i, 0)),
                   pl.BlockSpec((tm, LANES), lambda i: (i, 0)),
                   pl.BlockSpec((LANES, tm), lambda i: (0, i))],
        out_shape=[jax.ShapeDtypeStruct((n, nout), BF16),
                   jax.ShapeDtypeStruct((n, LANES), F32),
                   jax.ShapeDtypeStruct((LANES, n), F32)],
        compiler_params=_cparams("parallel"),
        name="in_proj_gates",
    )(x2, w, wg, wgt)


def _softmax_pv(s, v):
    m = jnp.max(s, axis=-1, keepdims=True)
    p = jnp.exp(s - m)
    l = jnp.sum(p, axis=-1, keepdims=True)
    return _dot(p.astype(BF16), v) / l


def _na_kernel(q_ref, k_ref, v_ref, tbl_ref, o_ref, *, rows):
    lane = lax.broadcasted_iota(I32, (1, LANES), 1)
    first = lane < NA_DH
    nkeys = WIN_H * GRID_W

    def rows_step(i, carry):
        rr = [i * NA_ROWS_PER_STEP + u for u in range(NA_ROWS_PER_STEP)]
        rss = [jnp.clip(r - WIN_H // 2, 0, rows - WIN_H) for r in rr]
        scores = []
        for r, rs in zip(rr, rss):
            q = q_ref[0, pl.ds(pl.multiple_of(r * GRID_W, GRID_W), GRID_W), :]
            q = q * jnp.asarray(NA_DH ** -0.5, BF16)
            q2 = jnp.concatenate([jnp.where(first, q, jnp.zeros_like(q)),
                                  jnp.where(first, jnp.zeros_like(q), q)], axis=0)
            k = k_ref[0, pl.ds(pl.multiple_of(rs * GRID_W, GRID_W), nkeys), :]
            dr0 = rs - r + WIN_H - 1
            bias = jnp.concatenate(
                [jnp.concatenate([tbl_ref[0, half, dr0 + 2 * m] for m in range(WIN_H // 2)], axis=1)
                 for half in range(2)], axis=0)
            scores.append(_dot_nt(q2, k) + bias)
        probs = []
        for s in scores:
            p = jnp.exp(s - jnp.max(s, axis=-1, keepdims=True))
            probs.append((p.astype(BF16), jnp.sum(p, axis=-1, keepdims=True)))
        for r, rs, (p, l) in zip(rr, rss, probs):
            v = v_ref[0, pl.ds(pl.multiple_of(rs * GRID_W, GRID_W), nkeys), :]
            o = _dot(p, v) / l
            o = jnp.where(first, o[:GRID_W], o[GRID_W:])
            o_ref[0, pl.ds(pl.multiple_of(r * GRID_W, GRID_W), GRID_W), :] = o.astype(o_ref.dtype)
        return carry

    lax.fori_loop(0, rows // NA_ROWS_PER_STEP, rows_step, 0)


def _na_bias_table(rpb):
    qc = np.arange(GRID_W)[:, None]
    kc = np.arange(GRID_W)[None, :]
    cs = np.clip(qc - WIN_W // 2, 0, GRID_W - WIN_W)
    col_in = (kc >= cs) & (kc < cs + WIN_W)
    side = GRID_W - WIN_W
    wide = jnp.pad(rpb, ((0, 0), (0, 0), (side, side)))
    t = jnp.stack([wide[:, :, GRID_W - 1 - q:2 * GRID_W - 1 - q] for q in range(GRID_W)], axis=2)
    t = jnp.where(col_in, t, NEG).astype(F32)
    t2 = jnp.concatenate([t[:, :-1], t[:, 1:]], axis=-1)
    return t2.reshape(NA_HEADS // 2, 2, 2 * WIN_H - 2, GRID_W, 2 * GRID_W)


def _na_attention(h3, tbl):
    b, s, _ = h3.shape
    rows = s // GRID_W
    npair = NA_HEADS // 2
    return pl.pallas_call(
        functools.partial(_na_kernel, rows=rows),
        grid=(b, npair),
        in_specs=[pl.BlockSpec((1, s, LANES), lambda i, p: (i, 0, p)),
                  pl.BlockSpec((1, s, LANES), lambda i, p: (i, 0, npair + p)),
                  pl.BlockSpec((1, s, LANES), lambda i, p: (i, 0, 2 * npair + p)),
                  pl.BlockSpec((1, 2, 2 * WIN_H - 2, GRID_W, 2 * GRID_W), lambda i, p: (p, 0, 0, 0, 0))],
        out_specs=pl.BlockSpec((1, s, LANES), lambda i, p: (i, 0, p)),
        out_shape=jax.ShapeDtypeStruct((b, s, NA_W), BF16),
        compiler_params=_cparams("parallel", "parallel"),
        name="na_attention",
    )(h3, h3, h3, tbl)


def _mem_attn_kernel(q_ref, k_ref, v_ref, o_ref):
    lane = lax.broadcasted_iota(I32, (1, LANES), 1)
    first = lane < MEM_DH
    q = q_ref[0] * jnp.asarray(MEM_DH ** -0.5, BF16)
    k = k_ref[0]
    v = v_ref[0]
    outs = []
    for half in range(2):
        qh = jnp.where(first if half == 0 else jnp.logical_not(first), q, jnp.zeros_like(q))
        outs.append(_softmax_pv(_dot_nt(qh, k), v))
    o_ref[0] = jnp.where(first, outs[0], outs[1]).astype(o_ref.dtype)


def _mem_attention(h3, col_block0, mem_k3, mem_v3, tq=512):
    b, s, _ = h3.shape
    nm = mem_k3.shape[1]
    npair = MEM_HEADS // 2
    return pl.pallas_call(
        _mem_attn_kernel,
        grid=(b, npair, s // tq),
        in_specs=[pl.BlockSpec((1, tq, LANES), lambda i, p, t: (i, t, col_block0 + p)),
                  pl.BlockSpec((1, nm, LANES), lambda i, p, t: (i, 0, p)),
                  pl.BlockSpec((1, nm, LANES), lambda i, p, t: (i, 0, p))],
        out_specs=pl.BlockSpec((1, tq, LANES), lambda i, p, t: (i, t, p)),
        out_shape=jax.ShapeDtypeStruct((b, s, MEM_W), BF16),
        compiler_params=_cparams("parallel", "parallel", "parallel"),
        name="mem_attention",
    )(h3, mem_k3, mem_v3)


def _outproj_ln_kernel(ya_ref, ym_ref, wa_ref, wm_ref, x_ref, g_ref, b_ref, o_ref, or_ref, *, tm):
    acc = _dot(ya_ref[...], wa_ref[...]) + _dot(ym_ref[...], wm_ref[...])
    out = _ln(ALPHA * x_ref[...] + acc, g_ref[...], b_ref[...])
    o_ref[...] = out
    _write_rows(or_ref, out, tm)


def _outproj_ln(ya, ym, wa, wm, x2, g, b, tm=512):
    n = x2.shape[0]
    ka, km = ya.shape[1], ym.shape[1]
    return pl.pallas_call(
        functools.partial(_outproj_ln_kernel, tm=tm),
        grid=(n // tm,),
        in_specs=[pl.BlockSpec((tm, ka), lambda i: (i, 0)),
                  pl.BlockSpec((tm, km), lambda i: (i, 0)),
                  pl.BlockSpec((ka, D_MODEL), lambda i: (0, 0)),
                  pl.BlockSpec((km, D_MODEL), lambda i: (0, 0)),
                  pl.BlockSpec((tm, D_MODEL), lambda i: (i, 0)),
                  pl.BlockSpec((1, D_MODEL), lambda i: (0, 0)),
                  pl.BlockSpec((1, D_MODEL), lambda i: (0, 0))],
        out_specs=[pl.BlockSpec((tm, D_MODEL), lambda i: (i, 0)),
                   pl.BlockSpec((tm * ROW_CHUNKS, LANES), lambda i: (i, 0))],
        out_shape=[jax.ShapeDtypeStruct((n, D_MODEL), F32),
                   jax.ShapeDtypeStruct((n * ROW_CHUNKS, LANES), F32)],
        compiler_params=_cparams("parallel"),
        name="outproj_ln",
    )(ya, ym, wa, wm, x2, g, b)


def _router_kernel(x_ref, rwh_ref, rwl_ref, rb_ref, lpos_ref, w_ref, cnt_ref, tcnt_ref, toff_ref, tbef_ref, *, tm):
    @pl.when(pl.program_id(0) == 0)
    def _():
        cnt_ref[...] = jnp.zeros_like(cnt_ref)

    x = x_ref[...]
    xh = x.astype(BF16)
    xl = (x - xh.astype(F32)).astype(BF16)
    logits_t = _dot(xh, rwh_ref[...]) + (_dot(xh, rwl_ref[...]) + _dot(xl, rwh_ref[...]))
    logits = logits_t.T[:N_EXPERTS]
    scores = jax.nn.sigmoid(logits)
    biased = scores + rb_ref[...]
    bv = [biased[e:e + 1, :] for e in range(N_EXPERTS)]
    sv = [scores[e:e + 1, :] for e in range(N_EXPERTS)]

    grp = []
    for g in range(N_GROUPS):
        m = bv[g * EXPERTS_PER_GROUP:(g + 1) * EXPERTS_PER_GROUP]
        best = None
        for a in range(EXPERTS_PER_GROUP):
            for c in range(a + 1, EXPERTS_PER_GROUP):
                pair = m[a] + m[c]
                best = pair if best is None else jnp.maximum(best, pair)
        grp.append(best)
    gsel = jnp.zeros((1, tm), I32)
    gbest = grp[0]
    for g in range(1, N_GROUPS):
        better = grp[g] > gbest
        gsel = jnp.where(better, g, gsel)
        gbest = jnp.where(better, grp[g], gbest)

    def pick(vals, j):
        out = vals[j]
        for g in range(1, N_GROUPS):
            out = jnp.where(gsel == g, vals[g * EXPERTS_PER_GROUP + j], out)
        return out

    cb = [pick(bv, j) for j in range(EXPERTS_PER_GROUP)]
    cs = [pick(sv, j) for j in range(EXPERTS_PER_GROUP)]
    i1 = jnp.zeros((1, tm), I32)
    m1 = cb[0]
    s1 = cs[0]
    for j in range(1, EXPERTS_PER_GROUP):
        gt = cb[j] > m1
        i1 = jnp.where(gt, j, i1)
        m1 = jnp.where(gt, cb[j], m1)
        s1 = jnp.where(gt, cs[j], s1)
    i2 = jnp.zeros((1, tm), I32)
    m2 = jnp.full((1, tm), -jnp.inf, F32)
    s2 = jnp.zeros((1, tm), F32)
    for j in range(EXPERTS_PER_GROUP):
        ok = jnp.logical_and(i1 != j, cb[j] > m2)
        i2 = jnp.where(ok, j, i2)
        m2 = jnp.where(ok, cb[j], m2)
        s2 = jnp.where(ok, cs[j], s2)
    e1 = gsel * EXPERTS_PER_GROUP + i1
    e2 = gsel * EXPERTS_PER_GROUP + i2
    tot = s1 + s2
    w_ref[...] = jnp.concatenate([s1 / tot, s2 / tot], axis=0)

    i = pl.program_id(0)
    eio = lax.broadcasted_iota(I32, (N_EXPERTS, tm), 0)
    oh1 = eio == e1
    oh2 = eio == e2
    ohs = jnp.logical_or(oh1, oh2).astype(F32)
    before = (lax.broadcasted_iota(I32, (tm, tm), 0) < lax.broadcasted_iota(I32, (tm, tm), 1))
    pre = _dot(ohs.astype(BF16), before.astype(BF16))
    tile_cnt = jnp.sum(ohs, axis=1, keepdims=True)
    offs = []
    acc = jnp.zeros((1, 1), F32)
    for e in range(N_EXPERTS):
        offs.append(acc)
        acc = acc + tile_cnt[e:e + 1, :]
    tile_off = jnp.concatenate(offs, axis=0)
    pos = tile_off + pre
    p1 = jnp.sum(jnp.where(oh1, pos, 0.0), axis=0, keepdims=True)
    p2 = jnp.sum(jnp.where(oh2, pos, 0.0), axis=0, keepdims=True)
    lpos_ref[...] = jnp.concatenate([p1, p2], axis=0).astype(I32) * ROW_CHUNKS

    @pl.when(i == 0)
    def _():
        for ref in (tcnt_ref, toff_ref, tbef_ref):
            ref[...] = jnp.zeros_like(ref)

    here = lax.broadcasted_iota(I32, (1, LANES), 1) == i
    tcnt_ref[...] = jnp.where(here, tile_cnt, tcnt_ref[...])
    toff_ref[...] = jnp.where(here, tile_off, toff_ref[...])
    tbef_ref[...] = jnp.where(here, cnt_ref[:, 0:1], tbef_ref[...])
    cnt_ref[...] += tile_cnt


def _router(x2, rw, rb, tm):
    n = x2.shape[0]
    assert n // tm <= LANES
    table = pl.BlockSpec((N_EXPERTS, LANES), lambda i: (0, 0))
    return pl.pallas_call(
        functools.partial(_router_kernel, tm=tm),
        grid=(n // tm,),
        in_specs=[pl.BlockSpec((tm, D_MODEL), lambda i: (i, 0)),
                  pl.BlockSpec((D_MODEL, LANES), lambda i: (0, 0)),
                  pl.BlockSpec((D_MODEL, LANES), lambda i: (0, 0)),
                  pl.BlockSpec((N_EXPERTS, 1), lambda i: (0, 0))],
        out_specs=[pl.BlockSpec((2, tm), lambda i: (0, i)),
                   pl.BlockSpec((2, tm), lambda i: (0, i)),
                   table, table, table, table],
        out_shape=[jax.ShapeDtypeStruct((2, n), I32),
                   jax.ShapeDtypeStruct((2, n), F32)]
                  + [jax.ShapeDtypeStruct((N_EXPERTS, LANES), F32)] * 4,
        compiler_params=_cparams("arbitrary"),
        name="router",
    )(x2, rw[0], rw[1], rb)


def _plan_kernel(cnt_ref, tbef_ref, meta_ref, rstart_ref, *, nbl):
    shift = MOE_BM.bit_length() - 1
    cnt = cnt_ref[...].astype(I32)
    padded = ((cnt + (MOE_BM - 1)) >> shift) << shift
    starts = []
    acc = jnp.zeros((1, LANES), I32)
    for e in range(N_EXPERTS):
        starts.append(acc)
        acc = acc + padded[e:e + 1, :]
    pad_start = jnp.concatenate(starts, axis=0)
    pad_end = pad_start + padded
    rstart_ref[...] = pad_start + tbef_ref[...].astype(I32)
    blk0 = lax.broadcasted_iota(I32, (N_EXPERTS, nbl), 1) * MOE_BM
    block_e = jnp.sum((pad_end[:, 0:1] <= blk0).astype(I32), axis=0, keepdims=True)
    block_e = jnp.minimum(block_e, N_EXPERTS - 1)
    n_used = jnp.broadcast_to(acc[:, 0:1] >> shift, (1, nbl))
    diag = lax.broadcasted_iota(I32, (N_EXPERTS, nbl), 0) == lax.broadcasted_iota(I32, (N_EXPERTS, nbl), 1)
    fill_lo = jnp.sum(jnp.where(diag, (pad_start + cnt)[:, 0:1], 0), axis=0, keepdims=True)
    fill_hi = jnp.sum(jnp.where(diag, pad_end[:, 0:1], 0), axis=0, keepdims=True)
    meta_ref[...] = jnp.concatenate([block_e, n_used, fill_lo, fill_hi, jnp.zeros((SUBLANES - 4, nbl), I32)],
                                    axis=0)


def _plan(cnt, tbef, n_blocks):
    nbl = -(-n_blocks // LANES) * LANES
    table = pl.BlockSpec((N_EXPERTS, LANES), lambda i: (0, 0))
    return pl.pallas_call(
        functools.partial(_plan_kernel, nbl=nbl),
        grid=(1,),
        in_specs=[table, table],
        out_specs=[pl.BlockSpec((SUBLANES, nbl), lambda i: (0, 0)), table],
        out_shape=[jax.ShapeDtypeStruct((SUBLANES, nbl), I32),
                   jax.ShapeDtypeStruct((N_EXPERTS, LANES), I32)],
        compiler_params=_cparams("arbitrary"),
        name="moe_plan",
    )(cnt, tbef)


def _rows(ref, row, nrows):
    return ref.at[pl.ds(pl.multiple_of(row * ROW_CHUNKS, ROW_CHUNKS), nrows * ROW_CHUNKS), :]


def _rows_wait(src_hbm, buf, sem):
    pltpu.make_async_copy(src_hbm.at[pl.ds(0, buf.shape[0]), :], buf, sem).wait()


def _copy_pieces(src, src_row, dst, dst_row, count, max_rows, sem, wait=False):
    bit = max_rows.bit_length() - 1
    while bit >= 0:
        size = 1 << bit
        done = (count >> (bit + 1)) << (bit + 1)

        @pl.when(((count >> bit) & 1) == 1)
        def _():
            cp = pltpu.make_async_copy(_rows(src, src_row + done, size), _rows(dst, dst_row + done, size), sem)
            cp.start()
            if wait:
                cp.wait()

        bit -= 1


def _tile_runs(tcnt_ref, toff_ref, rstart_ref, tile, buf, hbm, sem, *, to_hbm, tm):
    def per_expert(e, carry):
        k = tile * N_EXPERTS + e
        if to_hbm:
            _copy_pieces(buf, toff_ref[k], hbm, rstart_ref[k], tcnt_ref[k], tm, sem)
        else:
            _copy_pieces(hbm, rstart_ref[k], buf, toff_ref[k], tcnt_ref[k], tm, sem)
        return carry

    lax.fori_loop(0, N_EXPERTS, per_expert, 0)


def _dispatch_kernel(lpos_ref, tcnt_ref, toff_ref, rstart_ref, flo_ref, fhi_ref, nu_ref, x_ref, xs_hbm,
                     s0, s1, zbuf, sem, zsem, *, n, tm, n_blocks):
    i = pl.program_id(0)
    nt = pl.num_programs(0)
    bufs = (s0, s1)
    unroll = 8

    for slot in range(2):
        @pl.when(i % 2 == slot)
        def _():
            buf = bufs[slot]

            @pl.when(i >= 2)
            def _():
                _rows_wait(xs_hbm, buf, sem.at[slot])

            def place(c, carry):
                tok = i * tm + c * unroll
                src = pl.multiple_of(c * (unroll * ROW_CHUNKS), unroll * ROW_CHUNKS)
                for u in range(unroll):
                    v = x_ref[pl.ds(src + u * ROW_CHUNKS, ROW_CHUNKS), :]
                    for k in range(2):
                        p = lpos_ref[k * n + tok + u]
                        buf[pl.ds(pl.multiple_of(p, ROW_CHUNKS), ROW_CHUNKS), :] = v
                return carry

            lax.fori_loop(0, tm // unroll, place, 0)
            _tile_runs(tcnt_ref, toff_ref, rstart_ref, i, buf, xs_hbm, sem.at[slot], to_hbm=True, tm=tm)

    @pl.when(i == nt - 1)
    def _():
        for slot in range(2):
            @pl.when(nt > slot)
            def _():
                _rows_wait(xs_hbm, bufs[slot], sem.at[slot])

        zbuf[...] = jnp.zeros_like(zbuf)
        for e in range(N_EXPERTS):
            _copy_pieces(zbuf, 0, xs_hbm, flo_ref[e], fhi_ref[e] - flo_ref[e], MOE_BM // 2, zsem, wait=True)

        def zero_block(j, carry):
            cp = pltpu.make_async_copy(zbuf, _rows(xs_hbm, j * MOE_BM, MOE_BM), zsem)
            cp.start()
            cp.wait()
            return carry

        lax.fori_loop(nu_ref[0], n_blocks, zero_block, 0)


def _dispatch(lpos_flat, tcnt, toff, rstart, fill_lo, fill_hi, n_used, xr, n_blocks, tm):
    n = xr.shape[0] // ROW_CHUNKS
    return pl.pallas_call(
        functools.partial(_dispatch_kernel, n=n, tm=tm, n_blocks=n_blocks),
        grid_spec=pltpu.PrefetchScalarGridSpec(
            num_scalar_prefetch=7,
            grid=(n // tm,),
            in_specs=[pl.BlockSpec((tm * ROW_CHUNKS, LANES), lambda i, *_: (i, 0))],
            out_specs=pl.BlockSpec(memory_space=pl.ANY),
            scratch_shapes=[pltpu.VMEM((2 * tm * ROW_CHUNKS, LANES), F32),
                            pltpu.VMEM((2 * tm * ROW_CHUNKS, LANES), F32),
                            pltpu.VMEM((MOE_BM * ROW_CHUNKS, LANES), F32),
                            pltpu.SemaphoreType.DMA((2,)),
                            pltpu.SemaphoreType.DMA(())]),
        out_shape=jax.ShapeDtypeStruct((n_blocks * MOE_BM * ROW_CHUNKS, LANES), F32),
        compiler_params=_cparams("arbitrary"),
        name="moe_dispatch",
    )(lpos_flat, tcnt, toff, rstart, fill_lo, fill_hi, n_used, xr)


def _experts_kernel(be_ref, nu_ref, xs_ref, wg_ref, wu_ref, wd_ref, y_ref, wgb, wub, wdb):
    j = pl.program_id(0)
    used = j < nu_ref[0]

    @pl.when(jnp.logical_and(used, jnp.logical_or(j == 0, be_ref[j] != be_ref[jnp.maximum(j - 1, 0)])))
    def _():
        wgb[...] = wg_ref[0, 0].astype(BF16)
        wub[...] = wu_ref[0, 0].astype(BF16)
        wdb[...] = wd_ref[0, 0].astype(BF16)

    @pl.when(used)
    def _():
        x = _read_rows(xs_ref, MOE_BM).astype(BF16)
        h = _silu(_dot(x, wgb[...])) * _dot(x, wub[...])
        _write_rows(y_ref, _dot(h.astype(BF16), wdb[...]), MOE_BM)

    @pl.when(jnp.logical_not(used))
    def _():
        y_ref[...] = jnp.zeros_like(y_ref)


def _experts(block_e, n_used, xs, wg, wu, wd, layer):
    n_blocks = block_e.shape[0]

    def last_used(j, nu):
        return jnp.minimum(j, nu[0] - 1)

    def wblk(j, be, nu):
        return (layer, be[last_used(j, nu)], 0, 0)

    return pl.pallas_call(
        _experts_kernel,
        grid_spec=pltpu.PrefetchScalarGridSpec(
            num_scalar_prefetch=2,
            grid=(n_blocks,),
            in_specs=[pl.BlockSpec((MOE_BM * ROW_CHUNKS, LANES), lambda j, be, nu: (last_used(j, nu), 0)),
                      pl.BlockSpec((1, 1, D_MODEL, D_EXPERT), wblk),
                      pl.BlockSpec((1, 1, D_MODEL, D_EXPERT), wblk),
                      pl.BlockSpec((1, 1, D_EXPERT, D_MODEL), wblk)],
            out_specs=pl.BlockSpec((MOE_BM * ROW_CHUNKS, LANES), lambda j, be, nu: (j, 0)),
            scratch_shapes=[pltpu.VMEM((D_MODEL, D_EXPERT), BF16), pltpu.VMEM((D_MODEL, D_EXPERT), BF16),
                            pltpu.VMEM((D_EXPERT, D_MODEL), BF16)]),
        out_shape=jax.ShapeDtypeStruct(xs.shape, F32),
        compiler_params=_cparams("arbitrary"),
        name="moe_experts",
    )(block_e, n_used, xs, wg, wu, wd)


def _combine_ln_kernel(lpos_ref, tcnt_ref, toff_ref, rstart_ref, y_hbm, x_ref, w1_ref, w2_ref, g_ref, b_ref, o_ref,
                       r0, r1, u1, u2, sem, *, n, tm):
    i = pl.program_id(0)
    nt = pl.num_programs(0)
    bufs = (r0, r1)
    unroll = 8

    def fetch(tile, slot):
        _tile_runs(tcnt_ref, toff_ref, rstart_ref, tile, bufs[slot], y_hbm, sem.at[slot], to_hbm=False, tm=tm)

    @pl.when(i == 0)
    def _():
        fetch(0, 0)

    for slot in range(2):
        @pl.when(i % 2 == slot)
        def _():
            @pl.when(i + 1 < nt)
            def _():
                fetch(i + 1, 1 - slot)

            buf = bufs[slot]
            _rows_wait(y_hbm, buf, sem.at[slot])

            def place(c, carry):
                tok = i * tm + c * unroll
                dst0 = pl.multiple_of(c * (unroll * ROW_CHUNKS), unroll * ROW_CHUNKS)
                for u in range(unroll):
                    dst = pl.ds(dst0 + u * ROW_CHUNKS, ROW_CHUNKS)
                    for k, out in enumerate((u1, u2)):
                        p = lpos_ref[k * n + tok + u]
                        out[dst, :] = buf[pl.ds(pl.multiple_of(p, ROW_CHUNKS), ROW_CHUNKS), :]
                return carry

            lax.fori_loop(0, tm // unroll, place, 0)
            moe = w1_ref[...] * _read_rows(u1, tm) + w2_ref[...] * _read_rows(u2, tm)
            o_ref[...] = _ln(ALPHA * x_ref[...] + moe, g_ref[...], b_ref[...])


def _combine_ln(lpos_flat, tcnt, toff, rstart, y, x2, w1, w2, g, b, tm):
    n = x2.shape[0]
    return pl.pallas_call(
        functools.partial(_combine_ln_kernel, n=n, tm=tm),
        grid_spec=pltpu.PrefetchScalarGridSpec(
            num_scalar_prefetch=4,
            grid=(n // tm,),
            in_specs=[pl.BlockSpec(memory_space=pl.ANY),
                      pl.BlockSpec((tm, D_MODEL), lambda i, *_: (i, 0)),
                      pl.BlockSpec((tm, 1), lambda i, *_: (i, 0)),
                      pl.BlockSpec((tm, 1), lambda i, *_: (i, 0)),
                      pl.BlockSpec((1, D_MODEL), lambda i, *_: (0, 0)),
                      pl.BlockSpec((1, D_MODEL), lambda i, *_: (0, 0))],
            out_specs=pl.BlockSpec((tm, D_MODEL), lambda i, *_: (i, 0)),
            scratch_shapes=[pltpu.VMEM((2 * tm * ROW_CHUNKS, LANES), F32)] * 2
                           + [pltpu.VMEM((tm * ROW_CHUNKS, LANES), F32)] * 2
                           + [pltpu.SemaphoreType.DMA((2,))]),
        out_shape=jax.ShapeDtypeStruct((n, D_MODEL), F32),
        compiler_params=_cparams("arbitrary"),
        name="moe_combine_ln",
    )(lpos_flat, tcnt, toff, rstart, y, x2, w1, w2, g, b)


def _moe_ln(x2, xr, rw, rb, wg, wu, wd, layer, g, b):
    n = x2.shape[0]
    n_blocks = (2 * n) // MOE_BM + N_EXPERTS
    tm = MOE_TILE
    nt = n // tm
    lpos, w, cnt, tcnt, toff, tbef = _router(x2, rw, rb, tm)
    meta, rstart = _plan(cnt, tbef, n_blocks)
    block_e = meta[0, :n_blocks]
    n_used = meta[1, :1]

    def per_tile(table):
        return table[:, :nt].T.reshape(nt * N_EXPERTS).astype(I32)

    lpos_flat = lpos.reshape(2 * n)
    tcnt, toff, rstart = per_tile(tcnt), per_tile(toff), per_tile(rstart)
    xs = _dispatch(lpos_flat, tcnt, toff, rstart, meta[2, :N_EXPERTS], meta[3, :N_EXPERTS], n_used, xr,
                   n_blocks, tm)
    y = _experts(block_e, n_used, xs, wg, wu, wd, layer)
    return _combine_ln(lpos_flat, tcnt, toff, rstart, y, x2, w[0].reshape(n, 1), w[1].reshape(n, 1), g, b, tm)


def _conv_qkv_kernel(xm_ref, cw_ref, cb_ref, wq_ref, wk_ref, wv_ref, q_ref, k_ref, v_ref, xc_ref, *, s):
    xm_b = xm_ref[0]
    xm = xm_b.astype(F32)
    cw = cw_ref[...]
    row = lax.broadcasted_iota(I32, (s, 1), 0)
    half = CONV_K // 2
    acc = cb_ref[...] + xm * cw[half:half + 1, :]
    for sh in range(1, half + 1):
        past = jnp.where(row >= sh, pltpu.roll(xm, sh, axis=0), 0.0)
        acc = acc + past * cw[half - sh:half - sh + 1, :]
        nxt = jnp.where(row < s - sh, pltpu.roll(xm, s - sh, axis=0), 0.0)
        acc = acc + nxt * cw[half + sh:half + sh + 1, :]
    xc = _silu(acc).astype(BF16)
    xc_ref[0] = xc
    q_ref[0] = _dot(xc, wq_ref[0]).astype(BF16)
    k_ref[0] = (_dot_nt(wk_ref[0], xc) * (ML_DH ** -0.5)).astype(BF16)
    v = _dot(xm_b, wv_ref[0])
    ones_lane = lax.broadcasted_iota(I32, (1, ML_DHP), 1) == ML_DH
    v_ref[0] = jnp.where(ones_lane, 1.0, v).astype(BF16)


def _conv_qkv(main3, cw, cb, wq, wk_t, wv):
    b, s, _ = main3.shape
    tok = pl.BlockSpec((1, s, ML_DHP), lambda i, h: (i, 0, h))
    wspec = pl.BlockSpec((1, ML_DHP, ML_DHP), lambda i, h: (h, 0, 0))
    tok_shape = jax.ShapeDtypeStruct((b, s, ML_WP), BF16)
    return pl.pallas_call(
        functools.partial(_conv_qkv_kernel, s=s),
        grid=(b, ML_HEADS),
        in_specs=[tok,
                  pl.BlockSpec((CONV_K, ML_DHP), lambda i, h: (0, h)),
                  pl.BlockSpec((1, ML_DHP), lambda i, h: (0, h)),
                  wspec, wspec, wspec],
        out_specs=[tok, pl.BlockSpec((1, ML_DHP, s), lambda i, h: (i, h, 0)), tok, tok],
        out_shape=[tok_shape, jax.ShapeDtypeStruct((b, ML_WP, s), BF16), tok_shape, tok_shape],
        compiler_params=_cparams("parallel", "parallel"),
        name="conv_qkv",
    )(main3, cw, cb, wq, wk_t, wv)


def _split3(x):
    hi = x.astype(BF16)
    r1 = x - hi.astype(F32)
    mid = r1.astype(BF16)
    lo = (r1 - mid.astype(F32)).astype(BF16)
    return hi, mid, lo


def _gates_kernel(gc_ref, gr_ref, gbc_ref, gbr_ref, ac_ref, ar_ref, *, s):
    lane = lax.broadcasted_iota(I32, (1, LANES), 1)
    sub = lax.broadcasted_iota(I32, (LANES, 1), 0)
    ti = lax.broadcasted_iota(I32, (CHUNK, CHUNK), 0)
    tj = lax.broadcasted_iota(I32, (CHUNK, CHUNK), 1)
    lower = (tj <= ti).astype(BF16)
    upper = (ti <= tj).astype(BF16)

    def pick(idx, pre, suf, raw):
        fwd = jnp.logical_and(idx >= ML_HEADS, idx < 2 * ML_HEADS)
        bwd = jnp.logical_and(idx >= 3 * ML_HEADS, idx < 4 * ML_HEADS)
        return jnp.where(fwd, pre, jnp.where(bwd, suf, raw))

    def body(c, carry):
        t0 = pl.multiple_of(c * CHUNK, CHUNK)
        g = gc_ref[0, pl.ds(t0, CHUNK), :] + gbc_ref[...]
        ls = jax.nn.log_sigmoid(g)
        pre = sum(_dot(lower, part) for part in _split3(ls))
        suf = jnp.sum(ls, axis=0, keepdims=True) - pre + ls
        ac_ref[0, pl.ds(t0, CHUNK), :] = pick(lane, pre, suf, g)
        g = gr_ref[:, pl.ds(t0, CHUNK)] + gbr_ref[...]
        ls = jax.nn.log_sigmoid(g)
        pre = sum(_dot(part, upper) for part in _split3(ls))
        suf = jnp.sum(ls, axis=1, keepdims=True) - pre + ls
        ar_ref[:, pl.ds(t0, CHUNK)] = pick(sub, pre, suf, g)
        return carry

    lax.fori_loop(0, s // CHUNK, body, 0)


def _gates(gcol3, grow, gbc, gbr):
    b, s, _ = gcol3.shape
    return pl.pallas_call(
        functools.partial(_gates_kernel, s=s),
        grid=(b,),
        in_specs=[pl.BlockSpec((1, s, LANES), lambda i: (i, 0, 0)),
                  pl.BlockSpec((LANES, s), lambda i: (0, i)),
                  pl.BlockSpec((1, LANES), lambda i: (0, 0)),
                  pl.BlockSpec((LANES, 1), lambda i: (0, 0))],
        out_specs=[pl.BlockSpec((1, s, LANES), lambda i: (i, 0, 0)),
                   pl.BlockSpec((LANES, s), lambda i: (0, i))],
        out_shape=[jax.ShapeDtypeStruct((b, s, LANES), F32),
                   jax.ShapeDtypeStruct((LANES, b * s), F32)],
        compiler_params=_cparams("parallel"),
        name="mlstm_gates",
    )(gcol3, grow, gbc, gbr)


def _mlstm_kernel(q_ref, kt_ref, v_ref, gc_ref, gr_ref, z_ref, xc_ref, ng_ref, sk_ref,
                  y_ref, hf_ref, hb_ref, cf_ref, cb_ref, m_ref, *, s):
    head0 = pl.program_id(1) * ML_HPS
    nc = s // CHUNK
    sub = lax.broadcasted_iota(I32, (LANES, 1), 0)
    gate = lax.broadcasted_iota(I32, (LANES, LANES), 0)
    ti = lax.broadcasted_iota(I32, (CHUNK, CHUNK), 0)
    tj = lax.broadcasted_iota(I32, (CHUNK, CHUNK), 1)

    for ref in (cf_ref, cb_ref, m_ref):
        ref[...] = jnp.zeros_like(ref)

    def intra(c, j, rev):
        t0 = pl.multiple_of(c * CHUNK, CHUNK)
        hl = slice(j * ML_DHP, (j + 1) * ML_DHP)
        qb = q_ref[0, pl.ds(t0, CHUNK), hl]
        kt = kt_ref[0, hl, pl.ds(t0, CHUNK)]
        vb = v_ref[0, pl.ds(t0, CHUNK), hl]
        gc = gc_ref[0, pl.ds(t0, CHUNK), :]
        gr = gr_ref[:, pl.ds(t0, CHUNK)]
        i_idx = head0 + j + (2 * ML_HEADS if rev else 0)
        f_idx = i_idx + ML_HEADS
        allowed = (tj >= ti) if rev else (tj <= ti)
        sel = (gate == f_idx).astype(BF16)
        b_rep = sum(_dot(part, sel) for part in _split3(gc))
        b_row = jnp.sum(jnp.where(sub == f_idx, gr, 0.0), axis=0, keepdims=True)
        i_row = jnp.sum(jnp.where(sub == i_idx, gr, 0.0), axis=0, keepdims=True)
        b_last = b_rep[0:1, :] if rev else b_rep[CHUNK - 1:CHUNK, :]

        d = jnp.where(allowed, b_rep - b_row + i_row, NEG)
        m_in = jnp.max(d, axis=1, keepdims=True)
        sc = _dot(qb, kt) * jnp.exp(d - m_in)
        nd_in = _dot(sc.astype(BF16), vb)
        w_row = b_last - b_row + i_row
        return t0, qb, kt, vb, b_rep, b_last[:, 0:1], m_in, nd_in, w_row

    def twice(a):
        return jnp.concatenate([a, a], axis=1)

    def update(parts, j, rev):
        t0, qb, kt, vb, b_rep, b_last, m_in, nd_in, w_row = parts
        h_ref, c_ref = (hb_ref, cb_ref) if rev else (hf_ref, cf_ref)
        hl = slice(j * ML_DHP, (j + 1) * ML_DHP)
        mrow = 2 * j + int(rev)
        m = m_ref[mrow:mrow + 1, 0:1]
        cmat = c_ref[j]
        inter = b_rep + m
        m_t = jnp.maximum(m_in, inter)
        a_in = jnp.exp(m_in - m_t)
        iexp = jnp.exp(inter - m_t)
        nd = twice(a_in) * nd_in + twice(iexp) * _dot(qb, cmat.astype(BF16))
        den = nd[:, ML_DH:ML_DH + 1]
        h_ref[pl.ds(t0, CHUNK), hl] = nd * (1.0 / jnp.maximum(jnp.abs(den), jnp.exp(-m_t[:, 0:1])))

        m_new = jnp.maximum(b_last + m, jnp.max(w_row, axis=1, keepdims=True))
        wexp = jnp.exp(w_row - m_new)
        cexp = jnp.exp(b_last + m - m_new)
        kw = (kt.astype(F32) * wexp).astype(BF16)
        c_ref[j] = cexp * cmat + _dot(kw, vb)
        m_ref[mrow:mrow + 1, :] = jnp.broadcast_to(m_new, (1, LANES))

    def step(i, carry):
        chains = [(j, rev) for j in range(ML_HPS) for rev in (False, True)]
        parts = [intra(nc - 1 - i if rev else i, j, rev) for j, rev in chains]
        for p, (j, rev) in zip(parts, chains):
            update(p, j, rev)
        return carry

    lax.fori_loop(0, nc, step, 0)

    real = lax.broadcasted_iota(I32, (1, ML_DHP), 1) < ML_DH
    tb = 256

    def fin(c, carry):
        t0 = pl.multiple_of(c * tb, tb)
        for j in range(ML_HPS):
            hl = slice(j * ML_DHP, (j + 1) * ML_DHP)
            hs = jnp.where(real, hf_ref[pl.ds(t0, tb), hl] + hb_ref[pl.ds(t0, tb), hl], 0.0)
            mu = jnp.sum(hs, axis=1, keepdims=True) * (1.0 / ML_DH)
            dev = jnp.where(real, hs - mu, 0.0)
            var = jnp.sum(dev * dev, axis=1, keepdims=True) * (1.0 / ML_DH)
            hn = dev * lax.rsqrt(var + LN_EPS) * ng_ref[:, hl]
            xc = xc_ref[0, pl.ds(t0, tb), hl].astype(F32)
            z = z_ref[0, pl.ds(t0, tb), hl].astype(F32)
            y_ref[0, pl.ds(t0, tb), hl] = ((hn + sk_ref[:, hl] * xc) * _silu(z)).astype(BF16)
        return carry

    lax.fori_loop(0, s // tb, fin, 0)


def _mlstm(q, kt, v, gcol3, grow, main3, xc, ng, sk):
    b, s, _ = q.shape
    width = ML_HPS * ML_DHP
    steps = ML_HEADS // ML_HPS
    tok = pl.BlockSpec((1, s, width), lambda i, h: (i, 0, h))
    vec = pl.BlockSpec((1, width), lambda i, h: (0, h))
    return pl.pallas_call(
        functools.partial(_mlstm_kernel, s=s),
        grid=(b, steps),
        in_specs=[tok, pl.BlockSpec((1, width, s), lambda i, h: (i, h, 0)), tok,
                  pl.BlockSpec((1, s, LANES), lambda i, h: (i, 0, 0)),
                  pl.BlockSpec((LANES, s), lambda i, h: (0, i)),
                  pl.BlockSpec((1, s, width), lambda i, h: (i, 0, steps + h)),
                  tok, vec, vec],
        out_specs=tok,
        out_shape=jax.ShapeDtypeStruct((b, s, ML_WP), BF16),
        scratch_shapes=[pltpu.VMEM((s, width), F32), pltpu.VMEM((s, width), F32),
                        pltpu.VMEM((ML_HPS, ML_DHP, ML_DHP), F32), pltpu.VMEM((ML_HPS, ML_DHP, ML_DHP), F32),
                        pltpu.VMEM((SUBLANES, LANES), F32)],
        compiler_params=_cparams("parallel", "parallel"),
        name="mlstm",
    )(q, kt, v, gcol3, grow, main3, xc, ng, sk)


def _pad_heads(a, axis):
    a = jnp.moveaxis(a, axis, -1)
    lead = a.shape[:-1]
    a = a.reshape(lead + (ML_HEADS, ML_DH))
    a = jnp.pad(a, [(0, 0)] * len(lead) + [(0, 0), (0, ML_DHP - ML_DH)])
    return jnp.moveaxis(a.reshape(lead + (ML_WP,)), -1, axis)


def kernel(x, mem, mem_ln_g, mem_ln_b, w_mem_kv, router_w, router_b, na_w_in, na_rpb, ml_w_in, ml_conv_w,
           ml_conv_b, ml_w_qkv, ml_gate_b, ml_norm_g, ml_skip, w_out, ln_g, ln_b, exp_w_gate, exp_w_up,
           exp_w_down):
    b, s, d = x.shape
    n = b * s
    nm = mem.shape[1]
    row = lambda a: a.reshape(1, -1)

    mem_k, mem_v = _memkv(mem.reshape(b * nm, d), row(mem_ln_g), row(mem_ln_b), w_mem_kv.astype(BF16))
    mem_k3 = mem_k.reshape(b, nm, MEM_W)
    mem_v3 = mem_v.reshape(b, nm, MEM_W)
    rw_pad = jnp.pad(router_w, ((0, 0), (0, LANES - N_EXPERTS)))
    rw_hi = rw_pad.astype(BF16)
    rw = (rw_hi, (rw_pad - rw_hi.astype(F32)).astype(BF16))
    rb = router_b.reshape(N_EXPERTS, 1)

    x2 = x.reshape(n, d)

    h0 = _proj(x2, na_w_in[0].astype(BF16)).reshape(b, s, 3 * NA_W + MEM_W)
    y_na = _na_attention(h0, _na_bias_table(na_rpb[0]))
    y_mem = _mem_attention(h0, 3 * NA_W // LANES, mem_k3, mem_v3)
    wo = w_out[0].astype(BF16)
    x2, xr = _outproj_ln(y_na.reshape(n, NA_W), y_mem.reshape(n, MEM_W), wo[:NA_W], wo[NA_W:], x2,
                         row(ln_g[0, 0]), row(ln_b[0, 0]))
    x2 = _moe_ln(x2, xr, rw, rb, exp_w_gate, exp_w_up, exp_w_down, 0, row(ln_g[0, 1]), row(ln_b[0, 1]))

    w1 = ml_w_in[0]
    w_main = jnp.concatenate([_pad_heads(w1[:, :ML_W], 1), _pad_heads(w1[:, ML_W:2 * ML_W], 1),
                              w1[:, 2 * ML_W + 4 * ML_HEADS:]], axis=1).astype(BF16)
    w_g = jnp.pad(w1[:, 2 * ML_W:2 * ML_W + 4 * ML_HEADS], ((0, 0), (0, LANES - 4 * ML_HEADS))).astype(BF16)
    main, gcol, grow = _proj_gates(x2, w_main, w_g, w_g.T)
    main3 = main.reshape(b, s, 2 * ML_WP + MEM_W)
    wqkv = jnp.pad(ml_w_qkv[0], ((0, 0), (0, 0), (0, ML_DHP - ML_DH), (0, ML_DHP - ML_DH))).astype(BF16)
    q, k, v, xc = _conv_qkv(main3, _pad_heads(ml_conv_w[0], 1), _pad_heads(row(ml_conv_b[0]), 1),
                            wqkv[0], jnp.swapaxes(wqkv[1], 1, 2), wqkv[2])
    gb = jnp.pad(ml_gate_b[0].reshape(4 * ML_HEADS), (0, LANES - 4 * ML_HEADS))
    acol, arow = _gates(gcol.reshape(b, s, LANES), grow, gb.reshape(1, LANES), gb.reshape(LANES, 1))
    y_ml = _mlstm(q, k, v, acol, arow, main3, xc,
                  _pad_heads(row(ml_norm_g[0]), 1), _pad_heads(row(ml_skip[0]), 1))
    y_mem = _mem_attention(main3, 2 * ML_WP // LANES, mem_k3, mem_v3)
    wo = w_out[1]
    x2, xr = _outproj_ln(y_ml.reshape(n, ML_WP), y_mem.reshape(n, MEM_W), _pad_heads(wo[:ML_W], 0).astype(BF16),
                         wo[ML_W:].astype(BF16), x2, row(ln_g[1, 0]), row(ln_b[1, 0]))
    x2 = _moe_ln(x2, xr, rw, rb, exp_w_gate, exp_w_up, exp_w_down, 1, row(ln_g[1, 1]), row(ln_b[1, 1]))
    return x2.reshape(b, s, d)
```

```python
import functools

import numpy as np
import jax
import jax.numpy as jnp
from jax import lax
from jax.experimental import pallas as pl
from jax.experimental.pallas import tpu as pltpu

F32 = jnp.float32
BF16 = jnp.bfloat16
I32 = jnp.int32

D_MODEL = 1024
DEPTH = 2
GRID_W = 64
MEM_HEADS = 4
MEM_DH = 64
MEM_W = MEM_HEADS * MEM_DH
NA_HEADS = 12
NA_DH = 64
NA_W = NA_HEADS * NA_DH
WIN_H = 8
WIN_W = 16
ML_HEADS = 4
ML_DH = 192
ML_DHP = 256
ML_W = ML_HEADS * ML_DH
ML_WP = ML_HEADS * ML_DHP
CONV_K = 5
CHUNK = 128
N_EXPERTS = 16
N_GROUPS = 4
EXPERTS_PER_GROUP = N_EXPERTS // N_GROUPS
D_EXPERT = 512
ALPHA = (2 * DEPTH) ** 0.25
LN_EPS = 1e-5
NEG = -1e30

LANES = 128
SUBLANES = 8
ROW_CHUNKS = D_MODEL // LANES
MOE_BM = 512
MOE_TILE = 512
ML_HPS = 2
NA_ROWS_PER_STEP = 8
VMEM_LIMIT = 48 * 1024 * 1024


def _cparams(*sem):
    return pltpu.CompilerParams(dimension_semantics=sem, vmem_limit_bytes=VMEM_LIMIT)


def _dot(a, b):
    return jnp.dot(a, b, preferred_element_type=F32)


def _dot_nt(a, b, precision=None):
    return lax.dot_general(a, b, (((1,), (1,)), ((), ())), precision=precision,
                           preferred_element_type=F32)


def _ln(z, g, b):
    mu = jnp.mean(z, axis=-1, keepdims=True)
    zc = z - mu
    var = jnp.mean(zc * zc, axis=-1, keepdims=True)
    return zc * lax.rsqrt(var + LN_EPS) * g + b


def _silu(x):
    return x * jax.nn.sigmoid(x)


def _read_rows(ref, n):
    return jnp.concatenate([ref[pl.ds(j, n, stride=ROW_CHUNKS), :] for j in range(ROW_CHUNKS)], axis=1)


def _write_rows(ref, val, n):
    for j in range(ROW_CHUNKS):
        ref[pl.ds(j, n, stride=ROW_CHUNKS), :] = val[:, j * LANES:(j + 1) * LANES]


def _memkv_kernel(m_ref, g_ref, b_ref, w_ref, k_ref, v_ref):
    z = _ln(m_ref[...], g_ref[...], b_ref[...])
    kv = _dot(z.astype(BF16), w_ref[...])
    k_ref[...] = kv[:, :MEM_W].astype(BF16)
    v_ref[...] = kv[:, MEM_W:].astype(BF16)


def _memkv(mem2, g, b, w):
    n = mem2.shape[0]
    tm = 256
    return pl.pallas_call(
        _memkv_kernel,
        grid=(n // tm,),
        in_specs=[pl.BlockSpec((tm, D_MODEL), lambda i: (i, 0)),
                  pl.BlockSpec((1, D_MODEL), lambda i: (0, 0)),
                  pl.BlockSpec((1, D_MODEL), lambda i: (0, 0)),
                  pl.BlockSpec((D_MODEL, 2 * MEM_W), lambda i: (0, 0))],
        out_specs=[pl.BlockSpec((tm, MEM_W), lambda i: (i, 0)),
                   pl.BlockSpec((tm, MEM_W), lambda i: (i, 0))],
        out_shape=[jax.ShapeDtypeStruct((n, MEM_W), BF16)] * 2,
        compiler_params=_cparams("parallel"),
        name="memkv",
    )(mem2, g, b, w)


def _proj_kernel(x_ref, w_ref, o_ref):
    o_ref[...] = _dot(x_ref[...].astype(BF16), w_ref[...]).astype(o_ref.dtype)


def _proj(x2, w, tm=512):
    n, k = x2.shape
    nout = w.shape[1]
    return pl.pallas_call(
        _proj_kernel,
        grid=(n // tm,),
        in_specs=[pl.BlockSpec((tm, k), lambda i: (i, 0)),
                  pl.BlockSpec((k, nout), lambda i: (0, 0))],
        out_specs=pl.BlockSpec((tm, nout), lambda i: (i, 0)),
        out_shape=jax.ShapeDtypeStruct((n, nout), BF16),
        compiler_params=_cparams("parallel"),
        name="in_proj",
    )(x2, w)


def _proj_gates_kernel(x_ref, w_ref, wg_ref, wgt_ref, o_ref, g_ref, gt_ref):
    xb = x_ref[...].astype(BF16)
    o_ref[...] = _dot(xb, w_ref[...]).astype(BF16)
    g_ref[...] = _dot(xb, wg_ref[...])
    gt_ref[...] = _dot_nt(wgt_ref[...], xb)


def _proj_gates(x2, w, wg, wgt, tm=512):
    n, k = x2.shape
    nout = w.shape[1]
    return pl.pallas_call(
        _proj_gates_kernel,
        grid=(n // tm,),
        in_specs=[pl.BlockSpec((tm, k), lambda i: (i, 0)),
                  pl.BlockSpec((k, nout), lambda i: (0, 0)),
                  pl.BlockSpec((k, LANES), lambda i: (0, 0)),
                  pl.BlockSpec((LANES, k), lambda i: (0, 0))],
        out_specs=[pl.BlockSpec((tm, nout), lambda i: (i, 0)),
                   pl.BlockSpec((tm, LANES), lambda i: (i, 0)),
                   pl.BlockSpec((LANES, tm), lambda i: (0, i))],
        out_shape=[jax.ShapeDtypeStruct((n, nout), BF16),
                   jax.ShapeDtypeStruct((n, LANES), F32),
                   jax.ShapeDtypeStruct((LANES, n), F32)],
        compiler_params=_cparams("parallel"),
        name="in_proj_gates",
    )(x2, w, wg, wgt)


def _softmax_pv(s, v):
    m = jnp.max(s, axis=-1, keepdims=True)
    p = jnp.exp(s - m)
    l = jnp.sum(p, axis=-1, keepdims=True)
    return _dot(p.astype(BF16), v) / l


def _na_kernel(q_ref, k_ref, v_ref, tbl_ref, o_ref, *, rows):
    lane = lax.broadcasted_iota(I32, (1, LANES), 1)
    first = lane < NA_DH
    nkeys = WIN_H * GRID_W

    def rows_step(i, carry):
        rr = [i * NA_ROWS_PER_STEP + u for u in range(NA_ROWS_PER_STEP)]
        rss = [jnp.clip(r - WIN_H // 2, 0, rows - WIN_H) for r in rr]
        scores = []
        for r, rs in zip(rr, rss):
            q = q_ref[0, pl.ds(pl.multiple_of(r * GRID_W, GRID_W), GRID_W), :]
            q = q * jnp.asarray(NA_DH ** -0.5, BF16)
            q2 = jnp.concatenate([jnp.where(first, q, jnp.zeros_like(q)),
                                  jnp.where(first, jnp.zeros_like(q), q)], axis=0)
            k = k_ref[0, pl.ds(pl.multiple_of(rs * GRID_W, GRID_W), nkeys), :]
            dr0 = rs - r + WIN_H - 1
            bias = jnp.concatenate(
                [jnp.concatenate([tbl_ref[0, half, dr0 + 2 * m] for m in range(WIN_H // 2)], axis=1)
                 for half in range(2)], axis=0)
            scores.append(_dot_nt(q2, k) + bias)
        probs = []
        for s in scores:
            p = jnp.exp(s - jnp.max(s, axis=-1, keepdims=True))
            probs.append((p.astype(BF16), jnp.sum(p, axis=-1, keepdims=True)))
        for r, rs, (p, l) in zip(rr, rss, probs):
            v = v_ref[0, pl.ds(pl.multiple_of(rs * GRID_W, GRID_W), nkeys), :]
            o = _dot(p, v) / l
            o = jnp.where(first, o[:GRID_W], o[GRID_W:])
            o_ref[0, pl.ds(pl.multiple_of(r * GRID_W, GRID_W), GRID_W), :] = o.astype(o_ref.dtype)
        return carry

    lax.fori_loop(0, rows // NA_ROWS_PER_STEP, rows_step, 0)


def _na_bias_table(rpb):
    qc = np.arange(GRID_W)[:, None]
    kc = np.arange(GRID_W)[None, :]
    cs = np.clip(qc - WIN_W // 2, 0, GRID_W - WIN_W)
    col_in = (kc >= cs) & (kc < cs + WIN_W)
    side = GRID_W - WIN_W
    wide = jnp.pad(rpb, ((0, 0), (0, 0), (side, side)))
    t = jnp.stack([wide[:, :, GRID_W - 1 - q:2 * GRID_W - 1 - q] for q in range(GRID_W)], axis=2)
    t = jnp.where(col_in, t, NEG).astype(F32)
    t2 = jnp.concatenate([t[:, :-1], t[:, 1:]], axis=-1)
    return t2.reshape(NA_HEADS // 2, 2, 2 * WIN_H - 2, GRID_W, 2 * GRID_W)


def _na_attention(h3, tbl):
    b, s, _ = h3.shape
    rows = s // GRID_W
    npair = NA_HEADS // 2
    return pl.pallas_call(
        functools.partial(_na_kernel, rows=rows),
        grid=(b, npair),
        in_specs=[pl.BlockSpec((1, s, LANES), lambda i, p: (i, 0, p)),
                  pl.BlockSpec((1, s, LANES), lambda i, p: (i, 0, npair + p)),
                  pl.BlockSpec((1, s, LANES), lambda i, p: (i, 0, 2 * npair + p)),
                  pl.BlockSpec((1, 2, 2 * WIN_H - 2, GRID_W, 2 * GRID_W), lambda i, p: (p, 0, 0, 0, 0))],
        out_specs=pl.BlockSpec((1, s, LANES), lambda i, p: (i, 0, p)),
        out_shape=jax.ShapeDtypeStruct((b, s, NA_W), BF16),
        compiler_params=_cparams("parallel", "parallel"),
        name="na_attention",
    )(h3, h3, h3, tbl)


def _mem_attn_kernel(q_ref, k_ref, v_ref, o_ref):
    lane = lax.broadcasted_iota(I32, (1, LANES), 1)
    first = lane < MEM_DH
    q = q_ref[0] * jnp.asarray(MEM_DH ** -0.5, BF16)
    k = k_ref[0]
    v = v_ref[0]
    outs = []
    for half in range(2):
        qh = jnp.where(first if half == 0 else jnp.logical_not(first), q, jnp.zeros_like(q))
        outs.append(_softmax_pv(_dot_nt(qh, k), v))
    o_ref[0] = jnp.where(first, outs[0], outs[1]).astype(o_ref.dtype)


def _mem_attention(h3, col_block0, mem_k3, mem_v3, tq=512):
    b, s, _ = h3.shape
    nm = mem_k3.shape[1]
    npair = MEM_HEADS // 2
    return pl.pallas_call(
        _mem_attn_kernel,
        grid=(b, npair, s // tq),
        in_specs=[pl.BlockSpec((1, tq, LANES), lambda i, p, t: (i, t, col_block0 + p)),
                  pl.BlockSpec((1, nm, LANES), lambda i, p, t: (i, 0, p)),
                  pl.BlockSpec((1, nm, LANES), lambda i, p, t: (i, 0, p))],
        out_specs=pl.BlockSpec((1, tq, LANES), lambda i, p, t: (i, t, p)),
        out_shape=jax.ShapeDtypeStruct((b, s, MEM_W), BF16),
        compiler_params=_cparams("parallel", "parallel", "parallel"),
        name="mem_attention",
    )(h3, mem_k3, mem_v3)


def _outproj_ln_kernel(ya_ref, ym_ref, wa_ref, wm_ref, x_ref, g_ref, b_ref, or_ref, *, tm):
    acc = _dot(ya_ref[...], wa_ref[...]) + _dot(ym_ref[...], wm_ref[...])
    _write_rows(or_ref, _ln(ALPHA * x_ref[...] + acc, g_ref[...], b_ref[...]), tm)


def _outproj_ln(ya, ym, wa, wm, x2, g, b, tm=512):
    n = x2.shape[0]
    ka, km = ya.shape[1], ym.shape[1]
    return pl.pallas_call(
        functools.partial(_outproj_ln_kernel, tm=tm),
        grid=(n // tm,),
        in_specs=[pl.BlockSpec((tm, ka), lambda i: (i, 0)),
                  pl.BlockSpec((tm, km), lambda i: (i, 0)),
                  pl.BlockSpec((ka, D_MODEL), lambda i: (0, 0)),
                  pl.BlockSpec((km, D_MODEL), lambda i: (0, 0)),
                  pl.BlockSpec((tm, D_MODEL), lambda i: (i, 0)),
                  pl.BlockSpec((1, D_MODEL), lambda i: (0, 0)),
                  pl.BlockSpec((1, D_MODEL), lambda i: (0, 0))],
        out_specs=pl.BlockSpec((tm * ROW_CHUNKS, LANES), lambda i: (i, 0)),
        out_shape=jax.ShapeDtypeStruct((n * ROW_CHUNKS, LANES), F32),
        compiler_params=_cparams("parallel"),
        name="outproj_ln",
    )(ya, ym, wa, wm, x2, g, b)


def _router_kernel(x_ref, rwh_ref, rwl_ref, rb_ref, lpos_ref, w_ref, cnt_ref, tcnt_ref, toff_ref, tbef_ref, *, tm):
    @pl.when(pl.program_id(0) == 0)
    def _():
        cnt_ref[...] = jnp.zeros_like(cnt_ref)

    x = _read_rows(x_ref, tm)
    xh = x.astype(BF16)
    xl = (x - xh.astype(F32)).astype(BF16)
    logits_t = _dot(xh, rwh_ref[...]) + (_dot(xh, rwl_ref[...]) + _dot(xl, rwh_ref[...]))
    logits = logits_t.T[:N_EXPERTS]
    scores = jax.nn.sigmoid(logits)
    biased = scores + rb_ref[...]
    bv = [biased[e:e + 1, :] for e in range(N_EXPERTS)]
    sv = [scores[e:e + 1, :] for e in range(N_EXPERTS)]

    grp = []
    for g in range(N_GROUPS):
        m = bv[g * EXPERTS_PER_GROUP:(g + 1) * EXPERTS_PER_GROUP]
        best = None
        for a in range(EXPERTS_PER_GROUP):
            for c in range(a + 1, EXPERTS_PER_GROUP):
                pair = m[a] + m[c]
                best = pair if best is None else jnp.maximum(best, pair)
        grp.append(best)
    gsel = jnp.zeros((1, tm), I32)
    gbest = grp[0]
    for g in range(1, N_GROUPS):
        better = grp[g] > gbest
        gsel = jnp.where(better, g, gsel)
        gbest = jnp.where(better, grp[g], gbest)

    def pick(vals, j):
        out = vals[j]
        for g in range(1, N_GROUPS):
            out = jnp.where(gsel == g, vals[g * EXPERTS_PER_GROUP + j], out)
        return out

    cb = [pick(bv, j) for j in range(EXPERTS_PER_GROUP)]
    cs = [pick(sv, j) for j in range(EXPERTS_PER_GROUP)]
    i1 = jnp.zeros((1, tm), I32)
    m1 = cb[0]
    s1 = cs[0]
    for j in range(1, EXPERTS_PER_GROUP):
        gt = cb[j] > m1
        i1 = jnp.where(gt, j, i1)
        m1 = jnp.where(gt, cb[j], m1)
        s1 = jnp.where(gt, cs[j], s1)
    i2 = jnp.zeros((1, tm), I32)
    m2 = jnp.full((1, tm), -jnp.inf, F32)
    s2 = jnp.zeros((1, tm), F32)
    for j in range(EXPERTS_PER_GROUP):
        ok = jnp.logical_and(i1 != j, cb[j] > m2)
        i2 = jnp.where(ok, j, i2)
        m2 = jnp.where(ok, cb[j], m2)
        s2 = jnp.where(ok, cs[j], s2)
    e1 = gsel * EXPERTS_PER_GROUP + i1
    e2 = gsel * EXPERTS_PER_GROUP + i2
    tot = s1 + s2
    w_ref[...] = jnp.concatenate([s1 / tot, s2 / tot], axis=0)

    i = pl.program_id(0)
    eio = lax.broadcasted_iota(I32, (N_EXPERTS, tm), 0)
    oh1 = eio == e1
    oh2 = eio == e2
    ohs = jnp.logical_or(oh1, oh2).astype(F32)
    before = (lax.broadcasted_iota(I32, (tm, tm), 0) < lax.broadcasted_iota(I32, (tm, tm), 1))
    pre = _dot(ohs.astype(BF16), before.astype(BF16))
    tile_cnt = jnp.sum(ohs, axis=1, keepdims=True)
    offs = []
    acc = jnp.zeros((1, 1), F32)
    for e in range(N_EXPERTS):
        offs.append(acc)
        acc = acc + tile_cnt[e:e + 1, :]
    tile_off = jnp.concatenate(offs, axis=0)
    pos = tile_off + pre
    p1 = jnp.sum(jnp.where(oh1, pos, 0.0), axis=0, keepdims=True)
    p2 = jnp.sum(jnp.where(oh2, pos, 0.0), axis=0, keepdims=True)
    lpos_ref[...] = jnp.concatenate([p1, p2], axis=0).astype(I32) * ROW_CHUNKS

    @pl.when(i == 0)
    def _():
        for ref in (tcnt_ref, toff_ref, tbef_ref):
            ref[...] = jnp.zeros_like(ref)

    here = lax.broadcasted_iota(I32, (1, LANES), 1) == i
    tcnt_ref[...] = jnp.where(here, tile_cnt, tcnt_ref[...])
    toff_ref[...] = jnp.where(here, tile_off, toff_ref[...])
    tbef_ref[...] = jnp.where(here, cnt_ref[:, 0:1], tbef_ref[...])
    cnt_ref[...] += tile_cnt


def _router(xr, rw, rb, tm):
    n = xr.shape[0] // ROW_CHUNKS
    assert n // tm <= LANES
    table = pl.BlockSpec((N_EXPERTS, LANES), lambda i: (0, 0))
    return pl.pallas_call(
        functools.partial(_router_kernel, tm=tm),
        grid=(n // tm,),
        in_specs=[pl.BlockSpec((tm * ROW_CHUNKS, LANES), lambda i: (i, 0)),
                  pl.BlockSpec((D_MODEL, LANES), lambda i: (0, 0)),
                  pl.BlockSpec((D_MODEL, LANES), lambda i: (0, 0)),
                  pl.BlockSpec((N_EXPERTS, 1), lambda i: (0, 0))],
        out_specs=[pl.BlockSpec((2, tm), lambda i: (0, i)),
                   pl.BlockSpec((2, tm), lambda i: (0, i)),
                   table, table, table, table],
        out_shape=[jax.ShapeDtypeStruct((2, n), I32),
                   jax.ShapeDtypeStruct((2, n), F32)]
                  + [jax.ShapeDtypeStruct((N_EXPERTS, LANES), F32)] * 4,
        compiler_params=_cparams("arbitrary"),
        name="router",
    )(xr, rw[0], rw[1], rb)


def _plan_kernel(cnt_ref, tbef_ref, meta_ref, rstart_ref, *, nbl):
    shift = MOE_BM.bit_length() - 1
    cnt = cnt_ref[...].astype(I32)
    padded = ((cnt + (MOE_BM - 1)) >> shift) << shift
    starts = []
    acc = jnp.zeros((1, LANES), I32)
    for e in range(N_EXPERTS):
        starts.append(acc)
        acc = acc + padded[e:e + 1, :]
    pad_start = jnp.concatenate(starts, axis=0)
    pad_end = pad_start + padded
    rstart_ref[...] = pad_start + tbef_ref[...].astype(I32)
    blk0 = lax.broadcasted_iota(I32, (N_EXPERTS, nbl), 1) * MOE_BM
    block_e = jnp.sum((pad_end[:, 0:1] <= blk0).astype(I32), axis=0, keepdims=True)
    block_e = jnp.minimum(block_e, N_EXPERTS - 1)
    n_used = jnp.broadcast_to(acc[:, 0:1] >> shift, (1, nbl))
    diag = lax.broadcasted_iota(I32, (N_EXPERTS, nbl), 0) == lax.broadcasted_iota(I32, (N_EXPERTS, nbl), 1)
    fill_lo = jnp.sum(jnp.where(diag, (pad_start + cnt)[:, 0:1], 0), axis=0, keepdims=True)
    fill_hi = jnp.sum(jnp.where(diag, pad_end[:, 0:1], 0), axis=0, keepdims=True)
    meta_ref[...] = jnp.concatenate([block_e, n_used, fill_lo, fill_hi, jnp.zeros((SUBLANES - 4, nbl), I32)],
                                    axis=0)


def _plan(cnt, tbef, n_blocks):
    nbl = -(-n_blocks // LANES) * LANES
    table = pl.BlockSpec((N_EXPERTS, LANES), lambda i: (0, 0))
    return pl.pallas_call(
        functools.partial(_plan_kernel, nbl=nbl),
        grid=(1,),
        in_specs=[table, table],
        out_specs=[pl.BlockSpec((SUBLANES, nbl), lambda i: (0, 0)), table],
        out_shape=[jax.ShapeDtypeStruct((SUBLANES, nbl), I32),
                   jax.ShapeDtypeStruct((N_EXPERTS, LANES), I32)],
        compiler_params=_cparams("arbitrary"),
        name="moe_plan",
    )(cnt, tbef)


def _rows(ref, row, nrows):
    return ref.at[pl.ds(pl.multiple_of(row * ROW_CHUNKS, ROW_CHUNKS), nrows * ROW_CHUNKS), :]


def _rows_wait(src_hbm, buf, sem):
    pltpu.make_async_copy(src_hbm.at[pl.ds(0, buf.shape[0]), :], buf, sem).wait()


def _copy_pieces(src, src_row, dst, dst_row, count, max_rows, sem, wait=False):
    bit = max_rows.bit_length() - 1
    while bit >= 0:
        size = 1 << bit
        done = (count >> (bit + 1)) << (bit + 1)

        @pl.when(((count >> bit) & 1) == 1)
        def _():
            cp = pltpu.make_async_copy(_rows(src, src_row + done, size), _rows(dst, dst_row + done, size), sem)
            cp.start()
            if wait:
                cp.wait()

        bit -= 1


def _tile_runs(tcnt_ref, toff_ref, rstart_ref, tile, buf, hbm, sem, *, to_hbm, tm):
    def per_expert(e, carry):
        k = tile * N_EXPERTS + e
        if to_hbm:
            _copy_pieces(buf, toff_ref[k], hbm, rstart_ref[k], tcnt_ref[k], tm, sem)
        else:
            _copy_pieces(hbm, rstart_ref[k], buf, toff_ref[k], tcnt_ref[k], tm, sem)
        return carry

    lax.fori_loop(0, N_EXPERTS, per_expert, 0)


def _dispatch_kernel(lpos_ref, tcnt_ref, toff_ref, rstart_ref, flo_ref, fhi_ref, nu_ref, x_ref, xs_hbm,
                     s0, s1, zbuf, sem, zsem, *, n, tm, n_blocks):
    i = pl.program_id(0)
    nt = pl.num_programs(0)
    bufs = (s0, s1)
    unroll = 8

    for slot in range(2):
        @pl.when(i % 2 == slot)
        def _():
            buf = bufs[slot]

            @pl.when(i >= 2)
            def _():
                _rows_wait(xs_hbm, buf, sem.at[slot])

            def place(c, carry):
                tok = i * tm + c * unroll
                src = pl.multiple_of(c * (unroll * ROW_CHUNKS), unroll * ROW_CHUNKS)
                for u in range(unroll):
                    v = x_ref[pl.ds(src + u * ROW_CHUNKS, ROW_CHUNKS), :]
                    for k in range(2):
                        p = lpos_ref[k * n + tok + u]
                        buf[pl.ds(pl.multiple_of(p, ROW_CHUNKS), ROW_CHUNKS), :] = v
                return carry

            lax.fori_loop(0, tm // unroll, place, 0)
            _tile_runs(tcnt_ref, toff_ref, rstart_ref, i, buf, xs_hbm, sem.at[slot], to_hbm=True, tm=tm)

    @pl.when(i == nt - 1)
    def _():
        for slot in range(2):
            @pl.when(nt > slot)
            def _():
                _rows_wait(xs_hbm, bufs[slot], sem.at[slot])

        zbuf[...] = jnp.zeros_like(zbuf)
        for e in range(N_EXPERTS):
            _copy_pieces(zbuf, 0, xs_hbm, flo_ref[e], fhi_ref[e] - flo_ref[e], MOE_BM // 2, zsem, wait=True)

        def zero_block(j, carry):
            cp = pltpu.make_async_copy(zbuf, _rows(xs_hbm, j * MOE_BM, MOE_BM), zsem)
            cp.start()
            cp.wait()
            return carry

        lax.fori_loop(nu_ref[0], n_blocks, zero_block, 0)


def _dispatch(lpos_flat, tcnt, toff, rstart, fill_lo, fill_hi, n_used, xr, n_blocks, tm):
    n = xr.shape[0] // ROW_CHUNKS
    return pl.pallas_call(
        functools.partial(_dispatch_kernel, n=n, tm=tm, n_blocks=n_blocks),
        grid_spec=pltpu.PrefetchScalarGridSpec(
            num_scalar_prefetch=7,
            grid=(n // tm,),
            in_specs=[pl.BlockSpec((tm * ROW_CHUNKS, LANES), lambda i, *_: (i, 0))],
            out_specs=pl.BlockSpec(memory_space=pl.ANY),
            scratch_shapes=[pltpu.VMEM((2 * tm * ROW_CHUNKS, LANES), F32),
                            pltpu.VMEM((2 * tm * ROW_CHUNKS, LANES), F32),
                            pltpu.VMEM((MOE_BM * ROW_CHUNKS, LANES), F32),
                            pltpu.SemaphoreType.DMA((2,)),
                            pltpu.SemaphoreType.DMA(())]),
        out_shape=jax.ShapeDtypeStruct((n_blocks * MOE_BM * ROW_CHUNKS, LANES), F32),
        compiler_params=_cparams("arbitrary"),
        name="moe_dispatch",
    )(lpos_flat, tcnt, toff, rstart, fill_lo, fill_hi, n_used, xr)


def _experts_kernel(be_ref, nu_ref, xs_ref, wg_ref, wu_ref, wd_ref, y_ref, wgb, wub, wdb):
    j = pl.program_id(0)
    used = j < nu_ref[0]

    @pl.when(jnp.logical_and(used, jnp.logical_or(j == 0, be_ref[j] != be_ref[jnp.maximum(j - 1, 0)])))
    def _():
        wgb[...] = wg_ref[0, 0].astype(BF16)
        wub[...] = wu_ref[0, 0].astype(BF16)
        wdb[...] = wd_ref[0, 0].astype(BF16)

    @pl.when(used)
    def _():
        x = _read_rows(xs_ref, MOE_BM).astype(BF16)
        h = _silu(_dot(x, wgb[...])) * _dot(x, wub[...])
        _write_rows(y_ref, _dot(h.astype(BF16), wdb[...]), MOE_BM)

    @pl.when(jnp.logical_not(used))
    def _():
        y_ref[...] = jnp.zeros_like(y_ref)


def _experts(block_e, n_used, xs, wg, wu, wd, layer):
    n_blocks = block_e.shape[0]

    def last_used(j, nu):
        return jnp.minimum(j, nu[0] - 1)

    def wblk(j, be, nu):
        return (layer, be[last_used(j, nu)], 0, 0)

    return pl.pallas_call(
        _experts_kernel,
        grid_spec=pltpu.PrefetchScalarGridSpec(
            num_scalar_prefetch=2,
            grid=(n_blocks,),
            in_specs=[pl.BlockSpec((MOE_BM * ROW_CHUNKS, LANES), lambda j, be, nu: (last_used(j, nu), 0)),
                      pl.BlockSpec((1, 1, D_MODEL, D_EXPERT), wblk),
                      pl.BlockSpec((1, 1, D_MODEL, D_EXPERT), wblk),
                      pl.BlockSpec((1, 1, D_EXPERT, D_MODEL), wblk)],
            out_specs=pl.BlockSpec((MOE_BM * ROW_CHUNKS, LANES), lambda j, be, nu: (j, 0)),
            scratch_shapes=[pltpu.VMEM((D_MODEL, D_EXPERT), BF16), pltpu.VMEM((D_MODEL, D_EXPERT), BF16),
                            pltpu.VMEM((D_EXPERT, D_MODEL), BF16)]),
        out_shape=jax.ShapeDtypeStruct(xs.shape, F32),
        compiler_params=_cparams("arbitrary"),
        name="moe_experts",
    )(block_e, n_used, xs, wg, wu, wd)


def _combine_ln_kernel(lpos_ref, tcnt_ref, toff_ref, rstart_ref, y_hbm, x_ref, w1_ref, w2_ref, g_ref, b_ref, o_ref,
                       r0, r1, u1, u2, sem, *, n, tm):
    i = pl.program_id(0)
    nt = pl.num_programs(0)
    bufs = (r0, r1)
    unroll = 8

    def fetch(tile, slot):
        _tile_runs(tcnt_ref, toff_ref, rstart_ref, tile, bufs[slot], y_hbm, sem.at[slot], to_hbm=False, tm=tm)

    @pl.when(i == 0)
    def _():
        fetch(0, 0)

    for slot in range(2):
        @pl.when(i % 2 == slot)
        def _():
            @pl.when(i + 1 < nt)
            def _():
                fetch(i + 1, 1 - slot)

            buf = bufs[slot]
            _rows_wait(y_hbm, buf, sem.at[slot])

            def place(c, carry):
                tok = i * tm + c * unroll
                dst0 = pl.multiple_of(c * (unroll * ROW_CHUNKS), unroll * ROW_CHUNKS)
                for u in range(unroll):
                    dst = pl.ds(dst0 + u * ROW_CHUNKS, ROW_CHUNKS)
                    for k, out in enumerate((u1, u2)):
                        p = lpos_ref[k * n + tok + u]
                        out[dst, :] = buf[pl.ds(pl.multiple_of(p, ROW_CHUNKS), ROW_CHUNKS), :]
                return carry

            lax.fori_loop(0, tm // unroll, place, 0)
            moe = w1_ref[...] * _read_rows(u1, tm) + w2_ref[...] * _read_rows(u2, tm)
            o_ref[...] = _ln(ALPHA * _read_rows(x_ref, tm) + moe, g_ref[...], b_ref[...])


def _combine_ln(lpos_flat, tcnt, toff, rstart, y, xr, w1, w2, g, b, tm):
    n = xr.shape[0] // ROW_CHUNKS
    return pl.pallas_call(
        functools.partial(_combine_ln_kernel, n=n, tm=tm),
        grid_spec=pltpu.PrefetchScalarGridSpec(
            num_scalar_prefetch=4,
            grid=(n // tm,),
            in_specs=[pl.BlockSpec(memory_space=pl.ANY),
                      pl.BlockSpec((tm * ROW_CHUNKS, LANES), lambda i, *_: (i, 0)),
                      pl.BlockSpec((tm, 1), lambda i, *_: (i, 0)),
                      pl.BlockSpec((tm, 1), lambda i, *_: (i, 0)),
                      pl.BlockSpec((1, D_MODEL), lambda i, *_: (0, 0)),
                      pl.BlockSpec((1, D_MODEL), lambda i, *_: (0, 0))],
            out_specs=pl.BlockSpec((tm, D_MODEL), lambda i, *_: (i, 0)),
            scratch_shapes=[pltpu.VMEM((2 * tm * ROW_CHUNKS, LANES), F32)] * 2
                           + [pltpu.VMEM((tm * ROW_CHUNKS, LANES), F32)] * 2
                           + [pltpu.SemaphoreType.DMA((2,))]),
        out_shape=jax.ShapeDtypeStruct((n, D_MODEL), F32),
        compiler_params=_cparams("arbitrary"),
        name="moe_combine_ln",
    )(lpos_flat, tcnt, toff, rstart, y, xr, w1, w2, g, b)


def _moe_ln(xr, rw, rb, wg, wu, wd, layer, g, b):
    n = xr.shape[0] // ROW_CHUNKS
    n_blocks = (2 * n) // MOE_BM + N_EXPERTS
    tm = MOE_TILE
    nt = n // tm
    lpos, w, cnt, tcnt, toff, tbef = _router(xr, rw, rb, tm)
    meta, rstart = _plan(cnt, tbef, n_blocks)
    block_e = meta[0, :n_blocks]
    n_used = meta[1, :1]

    def per_tile(table):
        return table[:, :nt].T.reshape(nt * N_EXPERTS).astype(I32)

    lpos_flat = lpos.reshape(2 * n)
    tcnt, toff, rstart = per_tile(tcnt), per_tile(toff), per_tile(rstart)
    xs = _dispatch(lpos_flat, tcnt, toff, rstart, meta[2, :N_EXPERTS], meta[3, :N_EXPERTS], n_used, xr,
                   n_blocks, tm)
    y = _experts(block_e, n_used, xs, wg, wu, wd, layer)
    return _combine_ln(lpos_flat, tcnt, toff, rstart, y, xr, w[0].reshape(n, 1), w[1].reshape(n, 1), g, b, tm)


def _conv_qkv_kernel(xm_ref, cw_ref, cb_ref, wq_ref, wk_ref, wv_ref, q_ref, k_ref, v_ref, xc_ref, *, s):
    xm_b = xm_ref[0]
    xm = xm_b.astype(F32)
    cw = cw_ref[...]
    row = lax.broadcasted_iota(I32, (s, 1), 0)
    half = CONV_K // 2
    acc = cb_ref[...] + xm * cw[half:half + 1, :]
    for sh in range(1, half + 1):
        past = jnp.where(row >= sh, pltpu.roll(xm, sh, axis=0), 0.0)
        acc = acc + past * cw[half - sh:half - sh + 1, :]
        nxt = jnp.where(row < s - sh, pltpu.roll(xm, s - sh, axis=0), 0.0)
        acc = acc + nxt * cw[half + sh:half + sh + 1, :]
    xc = _silu(acc).astype(BF16)
    xc_ref[0] = xc
    q_ref[0] = _dot(xc, wq_ref[0]).astype(BF16)
    k_ref[0] = (_dot_nt(wk_ref[0], xc) * (ML_DH ** -0.5)).astype(BF16)
    v = _dot(xm_b, wv_ref[0])
    ones_lane = lax.broadcasted_iota(I32, (1, ML_DHP), 1) == ML_DH
    v_ref[0] = jnp.where(ones_lane, 1.0, v).astype(BF16)


def _conv_qkv(main3, cw, cb, wq, wk_t, wv):
    b, s, _ = main3.shape
    tok = pl.BlockSpec((1, s, ML_DHP), lambda i, h: (i, 0, h))
    wspec = pl.BlockSpec((1, ML_DHP, ML_DHP), lambda i, h: (h, 0, 0))
    tok_shape = jax.ShapeDtypeStruct((b, s, ML_WP), BF16)
    return pl.pallas_call(
        functools.partial(_conv_qkv_kernel, s=s),
        grid=(b, ML_HEADS),
        in_specs=[tok,
                  pl.BlockSpec((CONV_K, ML_DHP), lambda i, h: (0, h)),
                  pl.BlockSpec((1, ML_DHP), lambda i, h: (0, h)),
                  wspec, wspec, wspec],
        out_specs=[tok, pl.BlockSpec((1, ML_DHP, s), lambda i, h: (i, h, 0)), tok, tok],
        out_shape=[tok_shape, jax.ShapeDtypeStruct((b, ML_WP, s), BF16), tok_shape, tok_shape],
        compiler_params=_cparams("parallel", "parallel"),
        name="conv_qkv",
    )(main3, cw, cb, wq, wk_t, wv)


def _split3(x):
    hi = x.astype(BF16)
    r1 = x - hi.astype(F32)
    mid = r1.astype(BF16)
    lo = (r1 - mid.astype(F32)).astype(BF16)
    return hi, mid, lo


def _gates_kernel(gc_ref, gr_ref, gbc_ref, gbr_ref, ac_ref, ar_ref, *, s):
    lane = lax.broadcasted_iota(I32, (1, LANES), 1)
    sub = lax.broadcasted_iota(I32, (LANES, 1), 0)
    ti = lax.broadcasted_iota(I32, (CHUNK, CHUNK), 0)
    tj = lax.broadcasted_iota(I32, (CHUNK, CHUNK), 1)
    lower = (tj <= ti).astype(BF16)
    upper = (ti <= tj).astype(BF16)

    def pick(idx, pre, suf, raw):
        fwd = jnp.logical_and(idx >= ML_HEADS, idx < 2 * ML_HEADS)
        bwd = jnp.logical_and(idx >= 3 * ML_HEADS, idx < 4 * ML_HEADS)
        return jnp.where(fwd, pre, jnp.where(bwd, suf, raw))

    def body(c, carry):
        t0 = pl.multiple_of(c * CHUNK, CHUNK)
        g = gc_ref[0, pl.ds(t0, CHUNK), :] + gbc_ref[...]
        ls = jax.nn.log_sigmoid(g)
        pre = sum(_dot(lower, part) for part in _split3(ls))
        suf = jnp.sum(ls, axis=0, keepdims=True) - pre + ls
        ac_ref[0, pl.ds(t0, CHUNK), :] = pick(lane, pre, suf, g)
        g = gr_ref[:, pl.ds(t0, CHUNK)] + gbr_ref[...]
        ls = jax.nn.log_sigmoid(g)
        pre = sum(_dot(part, upper) for part in _split3(ls))
        suf = jnp.sum(ls, axis=1, keepdims=True) - pre + ls
        ar_ref[:, pl.ds(t0, CHUNK)] = pick(sub, pre, suf, g)
        return carry

    lax.fori_loop(0, s // CHUNK, body, 0)


def _gates(gcol3, grow, gbc, gbr):
    b, s, _ = gcol3.shape
    return pl.pallas_call(
        functools.partial(_gates_kernel, s=s),
        grid=(b,),
        in_specs=[pl.BlockSpec((1, s, LANES), lambda i: (i, 0, 0)),
                  pl.BlockSpec((LANES, s), lambda i: (0, i)),
                  pl.BlockSpec((1, LANES), lambda i: (0, 0)),
                  pl.BlockSpec((LANES, 1), lambda i: (0, 0))],
        out_specs=[pl.BlockSpec((1, s, LANES), lambda i: (i, 0, 0)),
                   pl.BlockSpec((LANES, s), lambda i: (0, i))],
        out_shape=[jax.ShapeDtypeStruct((b, s, LANES), F32),
                   jax.ShapeDtypeStruct((LANES, b * s), F32)],
        compiler_params=_cparams("parallel"),
        name="mlstm_gates",
    )(gcol3, grow, gbc, gbr)


def _mlstm_kernel(q_ref, kt_ref, v_ref, gc_ref, gr_ref, z_ref, xc_ref, ng_ref, sk_ref,
                  y_ref, hf_ref, hb_ref, cf_ref, cb_ref, m_ref, *, s):
    head0 = pl.program_id(1) * ML_HPS
    nc = s // CHUNK
    sub = lax.broadcasted_iota(I32, (LANES, 1), 0)
    gate = lax.broadcasted_iota(I32, (LANES, LANES), 0)
    ti = lax.broadcasted_iota(I32, (CHUNK, CHUNK), 0)
    tj = lax.broadcasted_iota(I32, (CHUNK, CHUNK), 1)

    for ref in (cf_ref, cb_ref, m_ref):
        ref[...] = jnp.zeros_like(ref)

    def intra(c, j, rev):
        t0 = pl.multiple_of(c * CHUNK, CHUNK)
        hl = slice(j * ML_DHP, (j + 1) * ML_DHP)
        qb = q_ref[0, pl.ds(t0, CHUNK), hl]
        kt = kt_ref[0, hl, pl.ds(t0, CHUNK)]
        vb = v_ref[0, pl.ds(t0, CHUNK), hl]
        gc = gc_ref[0, pl.ds(t0, CHUNK), :]
        gr = gr_ref[:, pl.ds(t0, CHUNK)]
        i_idx = head0 + j + (2 * ML_HEADS if rev else 0)
        f_idx = i_idx + ML_HEADS
        allowed = (tj >= ti) if rev else (tj <= ti)
        sel = (gate == f_idx).astype(BF16)
        b_rep = sum(_dot(part, sel) for part in _split3(gc))
        b_row = jnp.sum(jnp.where(sub == f_idx, gr, 0.0), axis=0, keepdims=True)
        i_row = jnp.sum(jnp.where(sub == i_idx, gr, 0.0), axis=0, keepdims=True)
        b_last = b_rep[0:1, :] if rev else b_rep[CHUNK - 1:CHUNK, :]

        d = jnp.where(allowed, b_rep - b_row + i_row, NEG)
        m_in = jnp.max(d, axis=1, keepdims=True)
        sc = _dot(qb, kt) * jnp.exp(d - m_in)
        nd_in = _dot(sc.astype(BF16), vb)
        w_row = b_last - b_row + i_row
        return t0, qb, kt, vb, b_rep, b_last[:, 0:1], m_in, nd_in, w_row

    def twice(a):
        return jnp.concatenate([a, a], axis=1)

    def update(parts, j, rev):
        t0, qb, kt, vb, b_rep, b_last, m_in, nd_in, w_row = parts
        h_ref, c_ref = (hb_ref, cb_ref) if rev else (hf_ref, cf_ref)
        hl = slice(j * ML_DHP, (j + 1) * ML_DHP)
        mrow = 2 * j + int(rev)
        m = m_ref[mrow:mrow + 1, 0:1]
        cmat = c_ref[j]
        inter = b_rep + m
        m_t = jnp.maximum(m_in, inter)
        a_in = jnp.exp(m_in - m_t)
        iexp = jnp.exp(inter - m_t)
        nd = twice(a_in) * nd_in + twice(iexp) * _dot(qb, cmat.astype(BF16))
        den = nd[:, ML_DH:ML_DH + 1]
        h_ref[pl.ds(t0, CHUNK), hl] = nd * (1.0 / jnp.maximum(jnp.abs(den), jnp.exp(-m_t[:, 0:1])))

        m_new = jnp.maximum(b_last + m, jnp.max(w_row, axis=1, keepdims=True))
        wexp = jnp.exp(w_row - m_new)
        cexp = jnp.exp(b_last + m - m_new)
        kw = (kt.astype(F32) * wexp).astype(BF16)
        c_ref[j] = cexp * cmat + _dot(kw, vb)
        m_ref[mrow:mrow + 1, :] = jnp.broadcast_to(m_new, (1, LANES))

    def step(i, carry):
        chains = [(j, rev) for j in range(ML_HPS) for rev in (False, True)]
        parts = [intra(nc - 1 - i if rev else i, j, rev) for j, rev in chains]
        for p, (j, rev) in zip(parts, chains):
            update(p, j, rev)
        return carry

    lax.fori_loop(0, nc, step, 0)

    real = lax.broadcasted_iota(I32, (1, ML_DHP), 1) < ML_DH
    tb = 256

    def fin(c, carry):
        t0 = pl.multiple_of(c * tb, tb)
        for j in range(ML_HPS):
            hl = slice(j * ML_DHP, (j + 1) * ML_DHP)
            hs = jnp.where(real, hf_ref[pl.ds(t0, tb), hl] + hb_ref[pl.ds(t0, tb), hl], 0.0)
            mu = jnp.sum(hs, axis=1, keepdims=True) * (1.0 / ML_DH)
            dev = jnp.where(real, hs - mu, 0.0)
            var = jnp.sum(dev * dev, axis=1, keepdims=True) * (1.0 / ML_DH)
            hn = dev * lax.rsqrt(var + LN_EPS) * ng_ref[:, hl]
            xc = xc_ref[0, pl.ds(t0, tb), hl].astype(F32)
            z = z_ref[0, pl.ds(t0, tb), hl].astype(F32)
            y_ref[0, pl.ds(t0, tb), hl] = ((hn + sk_ref[:, hl] * xc) * _silu(z)).astype(BF16)
        return carry

    lax.fori_loop(0, s // tb, fin, 0)


def _mlstm(q, kt, v, gcol3, grow, main3, xc, ng, sk):
    b, s, _ = q.shape
    width = ML_HPS * ML_DHP
    steps = ML_HEADS // ML_HPS
    tok = pl.BlockSpec((1, s, width), lambda i, h: (i, 0, h))
    vec = pl.BlockSpec((1, width), lambda i, h: (0, h))
    return pl.pallas_call(
        functools.partial(_mlstm_kernel, s=s),
        grid=(b, steps),
        in_specs=[tok, pl.BlockSpec((1, width, s), lambda i, h: (i, h, 0)), tok,
                  pl.BlockSpec((1, s, LANES), lambda i, h: (i, 0, 0)),
                  pl.BlockSpec((LANES, s), lambda i, h: (0, i)),
                  pl.BlockSpec((1, s, width), lambda i, h: (i, 0, steps + h)),
                  tok, vec, vec],
        out_specs=tok,
        out_shape=jax.ShapeDtypeStruct((b, s, ML_WP), BF16),
        scratch_shapes=[pltpu.VMEM((s, width), F32), pltpu.VMEM((s, width), F32),
                        pltpu.VMEM((ML_HPS, ML_DHP, ML_DHP), F32), pltpu.VMEM((ML_HPS, ML_DHP, ML_DHP), F32),
                        pltpu.VMEM((SUBLANES, LANES), F32)],
        compiler_params=_cparams("parallel", "parallel"),
        name="mlstm",
    )(q, kt, v, gcol3, grow, main3, xc, ng, sk)


def _pad_heads(a, axis):
    a = jnp.moveaxis(a, axis, -1)
    lead = a.shape[:-1]
    a = a.reshape(lead + (ML_HEADS, ML_DH))
    a = jnp.pad(a, [(0, 0)] * len(lead) + [(0, 0), (0, ML_DHP - ML_DH)])
    return jnp.moveaxis(a.reshape(lead + (ML_WP,)), -1, axis)


def kernel(x, mem, mem_ln_g, mem_ln_b, w_mem_kv, router_w, router_b, na_w_in, na_rpb, ml_w_in, ml_conv_w,
           ml_conv_b, ml_w_qkv, ml_gate_b, ml_norm_g, ml_skip, w_out, ln_g, ln_b, exp_w_gate, exp_w_up,
           exp_w_down):
    b, s, d = x.shape
    n = b * s
    nm = mem.shape[1]
    row = lambda a: a.reshape(1, -1)

    mem_k, mem_v = _memkv(mem.reshape(b * nm, d), row(mem_ln_g), row(mem_ln_b), w_mem_kv.astype(BF16))
    mem_k3 = mem_k.reshape(b, nm, MEM_W)
    mem_v3 = mem_v.reshape(b, nm, MEM_W)
    rw_pad = jnp.pad(router_w, ((0, 0), (0, LANES - N_EXPERTS)))
    rw_hi = rw_pad.astype(BF16)
    rw = (rw_hi, (rw_pad - rw_hi.astype(F32)).astype(BF16))
    rb = router_b.reshape(N_EXPERTS, 1)

    x2 = x.reshape(n, d)

    h0 = _proj(x2, na_w_in[0].astype(BF16)).reshape(b, s, 3 * NA_W + MEM_W)
    y_na = _na_attention(h0, _na_bias_table(na_rpb[0]))
    y_mem = _mem_attention(h0, 3 * NA_W // LANES, mem_k3, mem_v3)
    wo = w_out[0].astype(BF16)
    xr = _outproj_ln(y_na.reshape(n, NA_W), y_mem.reshape(n, MEM_W), wo[:NA_W], wo[NA_W:], x2,
                     row(ln_g[0, 0]), row(ln_b[0, 0]))
    x2 = _moe_ln(xr, rw, rb, exp_w_gate, exp_w_up, exp_w_down, 0, row(ln_g[0, 1]), row(ln_b[0, 1]))

    w1 = ml_w_in[0]
    w_main = jnp.concatenate([_pad_heads(w1[:, :ML_W], 1), _pad_heads(w1[:, ML_W:2 * ML_W], 1),
                              w1[:, 2 * ML_W + 4 * ML_HEADS:]], axis=1).astype(BF16)
    w_g = jnp.pad(w1[:, 2 * ML_W:2 * ML_W + 4 * ML_HEADS], ((0, 0), (0, LANES - 4 * ML_HEADS))).astype(BF16)
    main, gcol, grow = _proj_gates(x2, w_main, w_g, w_g.T)
    main3 = main.reshape(b, s, 2 * ML_WP + MEM_W)
    wqkv = jnp.pad(ml_w_qkv[0], ((0, 0), (0, 0), (0, ML_DHP - ML_DH), (0, ML_DHP - ML_DH))).astype(BF16)
    q, k, v, xc = _conv_qkv(main3, _pad_heads(ml_conv_w[0], 1), _pad_heads(row(ml_conv_b[0]), 1),
                            wqkv[0], jnp.swapaxes(wqkv[1], 1, 2), wqkv[2])
    gb = jnp.pad(ml_gate_b[0].reshape(4 * ML_HEADS), (0, LANES - 4 * ML_HEADS))
    acol, arow = _gates(gcol.reshape(b, s, LANES), grow, gb.reshape(1, LANES), gb.reshape(LANES, 1))
    y_ml = _mlstm(q, k, v, acol, arow, main3, xc,
                  _pad_heads(row(ml_norm_g[0]), 1), _pad_heads(row(ml_skip[0]), 1))
    y_mem = _mem_attention(main3, 2 * ML_WP // LANES, mem_k3, mem_v3)
    wo = w_out[1]
    xr = _outproj_ln(y_ml.reshape(n, ML_WP), y_mem.reshape(n, MEM_W), _pad_heads(wo[:ML_W], 0).astype(BF16),
                     wo[ML_W:].astype(BF16), x2, row(ln_g[1, 0]), row(ln_b[1, 0]))
    x2 = _moe_ln(xr, rw, rb, exp_w_gate, exp_w_up, exp_w_down, 1, row(ln_g[1, 1]), row(ln_b[1, 1]))
    return x2.reshape(b, s, d)
```

```python
import functools

import numpy as np
import jax
import jax.numpy as jnp
from jax import lax
from jax.experimental import pallas as pl
from jax.experimental.pallas import tpu as pltpu

F32 = jnp.float32
BF16 = jnp.bfloat16
I32 = jnp.int32

D_MODEL = 1024
DEPTH = 2
GRID_W = 64
MEM_HEADS = 4
MEM_DH = 64
MEM_W = MEM_HEADS * MEM_DH
NA_HEADS = 12
NA_DH = 64
NA_W = NA_HEADS * NA_DH
WIN_H = 8
WIN_W = 16
ML_HEADS = 4
ML_DH = 192
ML_DHP = 256
ML_W = ML_HEADS * ML_DH
ML_WP = ML_HEADS * ML_DHP
CONV_K = 5
CHUNK = 128
N_EXPERTS = 16
N_GROUPS = 4
EXPERTS_PER_GROUP = N_EXPERTS // N_GROUPS
D_EXPERT = 512
ALPHA = (2 * DEPTH) ** 0.25
LN_EPS = 1e-5
NEG = -1e30

LANES = 128
SUBLANES = 8
ROW_CHUNKS = D_MODEL // LANES
MOE_BM = 512
MOE_TILE = 512
ML_HPS = 2
NA_ROWS_PER_STEP = 8
VMEM_LIMIT = 48 * 1024 * 1024


def _cparams(*sem):
    return pltpu.CompilerParams(dimension_semantics=sem, vmem_limit_bytes=VMEM_LIMIT)


def _dot(a, b):
    return jnp.dot(a, b, preferred_element_type=F32)


def _dot_nt(a, b, precision=None):
    return lax.dot_general(a, b, (((1,), (1,)), ((), ())), precision=precision,
                           preferred_element_type=F32)


def _ln(z, g, b):
    mu = jnp.mean(z, axis=-1, keepdims=True)
    zc = z - mu
    var = jnp.mean(zc * zc, axis=-1, keepdims=True)
    return zc * lax.rsqrt(var + LN_EPS) * g + b


def _silu(x):
    return x * jax.nn.sigmoid(x)


def _read_rows(ref, n):
    return jnp.concatenate([ref[pl.ds(j, n, stride=ROW_CHUNKS), :] for j in range(ROW_CHUNKS)], axis=1)


def _write_rows(ref, val, n):
    for j in range(ROW_CHUNKS):
        ref[pl.ds(j, n, stride=ROW_CHUNKS), :] = val[:, j * LANES:(j + 1) * LANES]


def _memkv_kernel(m_ref, g_ref, b_ref, w_ref, k_ref, v_ref):
    z = _ln(m_ref[...], g_ref[...], b_ref[...])
    kv = _dot(z.astype(BF16), w_ref[...])
    k_ref[...] = kv[:, :MEM_W].astype(BF16)
    v_ref[...] = kv[:, MEM_W:].astype(BF16)


def _memkv(mem2, g, b, w):
    n = mem2.shape[0]
    tm = 256
    return pl.pallas_call(
        _memkv_kernel,
        grid=(n // tm,),
        in_specs=[pl.BlockSpec((tm, D_MODEL), lambda i: (i, 0)),
                  pl.BlockSpec((1, D_MODEL), lambda i: (0, 0)),
                  pl.BlockSpec((1, D_MODEL), lambda i: (0, 0)),
                  pl.BlockSpec((D_MODEL, 2 * MEM_W), lambda i: (0, 0))],
        out_specs=[pl.BlockSpec((tm, MEM_W), lambda i: (i, 0)),
                   pl.BlockSpec((tm, MEM_W), lambda i: (i, 0))],
        out_shape=[jax.ShapeDtypeStruct((n, MEM_W), BF16)] * 2,
        compiler_params=_cparams("parallel"),
        name="memkv",
    )(mem2, g, b, w)


def _proj_kernel(x_ref, w_ref, o_ref):
    o_ref[...] = _dot(x_ref[...].astype(BF16), w_ref[...]).astype(o_ref.dtype)


def _proj(x2, w, tm=512):
    n, k = x2.shape
    nout = w.shape[1]
    return pl.pallas_call(
        _proj_kernel,
        grid=(n // tm,),
        in_specs=[pl.BlockSpec((tm, k), lambda i: (i, 0)),
                  pl.BlockSpec((k, nout), lambda i: (0, 0))],
        out_specs=pl.BlockSpec((tm, nout), lambda i: (i, 0)),
        out_shape=jax.ShapeDtypeStruct((n, nout), BF16),
        compiler_params=_cparams("parallel"),
        name="in_proj",
    )(x2, w)


def _split3(x):
    hi = x.astype(BF16)
    r1 = x - hi.astype(F32)
    mid = r1.astype(BF16)
    lo = (r1 - mid.astype(F32)).astype(BF16)
    return hi, mid, lo


def _proj_gates_kernel(x_ref, w_ref, wg_ref, wgt_ref, gbc_ref, gbr_ref, o_ref, g_ref, gt_ref, *, tm):
    xb = x_ref[...].astype(BF16)
    o_ref[...] = _dot(xb, w_ref[...]).astype(BF16)
    gcol = _dot(xb, wg_ref[...]) + gbc_ref[...]
    grow = _dot_nt(wgt_ref[...], xb) + gbr_ref[...]
    lane = lax.broadcasted_iota(I32, (1, LANES), 1)
    sub = lax.broadcasted_iota(I32, (LANES, 1), 0)
    ti = lax.broadcasted_iota(I32, (CHUNK, CHUNK), 0)
    tj = lax.broadcasted_iota(I32, (CHUNK, CHUNK), 1)
    lower = (tj <= ti).astype(BF16)
    upper = (ti <= tj).astype(BF16)

    def pick(idx, pre, suf, raw):
        fwd = jnp.logical_and(idx >= ML_HEADS, idx < 2 * ML_HEADS)
        bwd = jnp.logical_and(idx >= 3 * ML_HEADS, idx < 4 * ML_HEADS)
        return jnp.where(fwd, pre, jnp.where(bwd, suf, raw))

    for c in range(tm // CHUNK):
        tc = slice(c * CHUNK, (c + 1) * CHUNK)
        g = gcol[tc, :]
        ls = jax.nn.log_sigmoid(g)
        pre = sum(_dot(lower, part) for part in _split3(ls))
        suf = jnp.sum(ls, axis=0, keepdims=True) - pre + ls
        g_ref[tc, :] = pick(lane, pre, suf, g)
        g = grow[:, tc]
        ls = jax.nn.log_sigmoid(g)
        pre = sum(_dot(part, upper) for part in _split3(ls))
        suf = jnp.sum(ls, axis=1, keepdims=True) - pre + ls
        gt_ref[:, tc] = pick(sub, pre, suf, g)


def _proj_gates(x2, w, wg, wgt, gbc, gbr, tm=512):
    n, k = x2.shape
    nout = w.shape[1]
    return pl.pallas_call(
        functools.partial(_proj_gates_kernel, tm=tm),
        grid=(n // tm,),
        in_specs=[pl.BlockSpec((tm, k), lambda i: (i, 0)),
                  pl.BlockSpec((k, nout), lambda i: (0, 0)),
                  pl.BlockSpec((k, LANES), lambda i: (0, 0)),
                  pl.BlockSpec((LANES, k), lambda i: (0, 0)),
                  pl.BlockSpec((1, LANES), lambda i: (0, 0)),
                  pl.BlockSpec((LANES, 1), lambda i: (0, 0))],
        out_specs=[pl.BlockSpec((tm, nout), lambda i: (i, 0)),
                   pl.BlockSpec((tm, LANES), lambda i: (i, 0)),
                   pl.BlockSpec((LANES, tm), lambda i: (0, i))],
        out_shape=[jax.ShapeDtypeStruct((n, nout), BF16),
                   jax.ShapeDtypeStruct((n, LANES), F32),
                   jax.ShapeDtypeStruct((LANES, n), F32)],
        compiler_params=_cparams("parallel"),
        name="in_proj_gates",
    )(x2, w, wg, wgt, gbc, gbr)


def _na_kernel(q_ref, k_ref, v_ref, tbl_ref, o_ref, *, rows):
    lane = lax.broadcasted_iota(I32, (1, LANES), 1)
    first = lane < NA_DH
    nkeys = WIN_H * GRID_W

    def rows_step(i, carry):
        rr = [i * NA_ROWS_PER_STEP + u for u in range(NA_ROWS_PER_STEP)]
        rss = [jnp.clip(r - WIN_H // 2, 0, rows - WIN_H) for r in rr]
        scores = []
        for r, rs in zip(rr, rss):
            q = q_ref[0, pl.ds(pl.multiple_of(r * GRID_W, GRID_W), GRID_W), :]
            q = q * jnp.asarray(NA_DH ** -0.5, BF16)
            q2 = jnp.concatenate([jnp.where(first, q, jnp.zeros_like(q)),
                                  jnp.where(first, jnp.zeros_like(q), q)], axis=0)
            k = k_ref[0, pl.ds(pl.multiple_of(rs * GRID_W, GRID_W), nkeys), :]
            dr0 = rs - r + WIN_H - 1
            bias = jnp.concatenate(
                [jnp.concatenate([tbl_ref[0, half, dr0 + 2 * m] for m in range(WIN_H // 2)], axis=1)
                 for half in range(2)], axis=0)
            scores.append(_dot_nt(q2, k) + bias)
        probs = []
        for s in scores:
            p = jnp.exp(s - jnp.max(s, axis=-1, keepdims=True))
            probs.append((p.astype(BF16), jnp.sum(p, axis=-1, keepdims=True)))
        for r, rs, (p, l) in zip(rr, rss, probs):
            v = v_ref[0, pl.ds(pl.multiple_of(rs * GRID_W, GRID_W), nkeys), :]
            o = _dot(p, v) / l
            o = jnp.where(first, o[:GRID_W], o[GRID_W:])
            o_ref[0, pl.ds(pl.multiple_of(r * GRID_W, GRID_W), GRID_W), :] = o.astype(o_ref.dtype)
        return carry

    lax.fori_loop(0, rows // NA_ROWS_PER_STEP, rows_step, 0)


def _na_bias_table(rpb):
    qc = np.arange(GRID_W)[:, None]
    kc = np.arange(GRID_W)[None, :]
    cs = np.clip(qc - WIN_W // 2, 0, GRID_W - WIN_W)
    col_in = (kc >= cs) & (kc < cs + WIN_W)
    side = GRID_W - WIN_W
    wide = jnp.pad(rpb, ((0, 0), (0, 0), (side, side)))
    t = jnp.stack([wide[:, :, GRID_W - 1 - q:2 * GRID_W - 1 - q] for q in range(GRID_W)], axis=2)
    t = jnp.where(col_in, t, NEG).astype(F32)
    t2 = jnp.concatenate([t[:, :-1], t[:, 1:]], axis=-1)
    return t2.reshape(NA_HEADS // 2, 2, 2 * WIN_H - 2, GRID_W, 2 * GRID_W)


def _na_attention(h3, tbl):
    b, s, _ = h3.shape
    rows = s // GRID_W
    npair = NA_HEADS // 2
    return pl.pallas_call(
        functools.partial(_na_kernel, rows=rows),
        grid=(b, npair),
        in_specs=[pl.BlockSpec((1, s, LANES), lambda i, p: (i, 0, p)),
                  pl.BlockSpec((1, s, LANES), lambda i, p: (i, 0, npair + p)),
                  pl.BlockSpec((1, s, LANES), lambda i, p: (i, 0, 2 * npair + p)),
                  pl.BlockSpec((1, 2, 2 * WIN_H - 2, GRID_W, 2 * GRID_W), lambda i, p: (p, 0, 0, 0, 0))],
        out_specs=pl.BlockSpec((1, s, LANES), lambda i, p: (i, 0, p)),
        out_shape=jax.ShapeDtypeStruct((b, s, NA_W), BF16),
        compiler_params=_cparams("parallel", "parallel"),
        name="na_attention",
    )(h3, h3, h3, tbl)


def _outproj_ln_kernel(ya_ref, qm_ref, mk_ref, mv_ref, wa_ref, wm_ref, x_ref, g_ref, b_ref, or_ref, *, tm):
    lane = lax.broadcasted_iota(I32, (1, LANES), 1)
    first = lane < MEM_DH
    q = qm_ref[...] * jnp.asarray(MEM_DH ** -0.5, BF16)
    cols = [slice(p * LANES, (p + 1) * LANES) for p in range(MEM_HEADS // 2)]
    scores = []
    for c in cols:
        qp = q[:, c]
        q2 = jnp.concatenate([jnp.where(first, qp, jnp.zeros_like(qp)),
                              jnp.where(first, jnp.zeros_like(qp), qp)], axis=0)
        scores.append(_dot_nt(q2, mk_ref[0, :, c]))
    probs = []
    for s in scores:
        p = jnp.exp(s - jnp.max(s, axis=-1, keepdims=True))
        probs.append((p.astype(BF16), jnp.sum(p, axis=-1, keepdims=True)))
    outs = []
    for c, (p, l) in zip(cols, probs):
        o = _dot(p, mv_ref[0, :, c]) / l
        outs.append(jnp.where(first, o[:tm], o[tm:]))
    ym = jnp.concatenate(outs, axis=1).astype(BF16)
    acc = _dot(ya_ref[...], wa_ref[...]) + _dot(ym, wm_ref[...])
    _write_rows(or_ref, _ln(ALPHA * x_ref[...] + acc, g_ref[...], b_ref[...]), tm)


def _outproj_ln(ya, h2, qm_block, mem_k3, mem_v3, wa, wm, x2, g, b, tm=512):
    n = x2.shape[0]
    ka = ya.shape[1]
    nb, nm, _ = mem_k3.shape
    per_batch = n // nb // tm
    full = lambda shape: pl.BlockSpec(shape, lambda i: (0,) * len(shape))
    return pl.pallas_call(
        functools.partial(_outproj_ln_kernel, tm=tm),
        grid=(n // tm,),
        in_specs=[pl.BlockSpec((tm, ka), lambda i: (i, 0)),
                  pl.BlockSpec((tm, MEM_W), lambda i: (i, qm_block)),
                  pl.BlockSpec((1, nm, MEM_W), lambda i: (i // per_batch, 0, 0)),
                  pl.BlockSpec((1, nm, MEM_W), lambda i: (i // per_batch, 0, 0)),
                  full((ka, D_MODEL)), full((MEM_W, D_MODEL)),
                  pl.BlockSpec((tm, D_MODEL), lambda i: (i, 0)),
                  full((1, D_MODEL)), full((1, D_MODEL))],
        out_specs=pl.BlockSpec((tm * ROW_CHUNKS, LANES), lambda i: (i, 0)),
        out_shape=jax.ShapeDtypeStruct((n * ROW_CHUNKS, LANES), F32),
        compiler_params=_cparams("parallel"),
        name="outproj_ln",
    )(ya, h2, mem_k3, mem_v3, wa, wm, x2, g, b)


def _router_kernel(x_ref, rwh_ref, rwl_ref, rb_ref, lpos_ref, w_ref, cnt_ref, tcnt_ref, toff_ref, tbef_ref, *, tm):
    @pl.when(pl.program_id(0) == 0)
    def _():
        cnt_ref[...] = jnp.zeros_like(cnt_ref)

    x = _read_rows(x_ref, tm)
    xh = x.astype(BF16)
    xl = (x - xh.astype(F32)).astype(BF16)
    logits_t = _dot(xh, rwh_ref[...]) + (_dot(xh, rwl_ref[...]) + _dot(xl, rwh_ref[...]))
    logits = logits_t.T[:N_EXPERTS]
    scores = jax.nn.sigmoid(logits)
    biased = scores + rb_ref[...]
    bv = [biased[e:e + 1, :] for e in range(N_EXPERTS)]
    sv = [scores[e:e + 1, :] for e in range(N_EXPERTS)]

    grp = []
    for g in range(N_GROUPS):
        m = bv[g * EXPERTS_PER_GROUP:(g + 1) * EXPERTS_PER_GROUP]
        best = None
        for a in range(EXPERTS_PER_GROUP):
            for c in range(a + 1, EXPERTS_PER_GROUP):
                pair = m[a] + m[c]
                best = pair if best is None else jnp.maximum(best, pair)
        grp.append(best)
    gsel = jnp.zeros((1, tm), I32)
    gbest = grp[0]
    for g in range(1, N_GROUPS):
        better = grp[g] > gbest
        gsel = jnp.where(better, g, gsel)
        gbest = jnp.where(better, grp[g], gbest)

    def pick(vals, j):
        out = vals[j]
        for g in range(1, N_GROUPS):
            out = jnp.where(gsel == g, vals[g * EXPERTS_PER_GROUP + j], out)
        return out

    cb = [pick(bv, j) for j in range(EXPERTS_PER_GROUP)]
    cs = [pick(sv, j) for j in range(EXPERTS_PER_GROUP)]
    i1 = jnp.zeros((1, tm), I32)
    m1 = cb[0]
    s1 = cs[0]
    for j in range(1, EXPERTS_PER_GROUP):
        gt = cb[j] > m1
        i1 = jnp.where(gt, j, i1)
        m1 = jnp.where(gt, cb[j], m1)
        s1 = jnp.where(gt, cs[j], s1)
    i2 = jnp.zeros((1, tm), I32)
    m2 = jnp.full((1, tm), -jnp.inf, F32)
    s2 = jnp.zeros((1, tm), F32)
    for j in range(EXPERTS_PER_GROUP):
        ok = jnp.logical_and(i1 != j, cb[j] > m2)
        i2 = jnp.where(ok, j, i2)
        m2 = jnp.where(ok, cb[j], m2)
        s2 = jnp.where(ok, cs[j], s2)
    e1 = gsel * EXPERTS_PER_GROUP + i1
    e2 = gsel * EXPERTS_PER_GROUP + i2
    tot = s1 + s2
    w_ref[...] = jnp.concatenate([s1 / tot, s2 / tot], axis=0)

    i = pl.program_id(0)
    eio = lax.broadcasted_iota(I32, (N_EXPERTS, tm), 0)
    oh1 = eio == e1
    oh2 = eio == e2
    ohs = jnp.logical_or(oh1, oh2).astype(F32)
    before = (lax.broadcasted_iota(I32, (tm, tm), 0) < lax.broadcasted_iota(I32, (tm, tm), 1))
    pre = _dot(ohs.astype(BF16), before.astype(BF16))
    tile_cnt = jnp.sum(ohs, axis=1, keepdims=True)
    offs = []
    acc = jnp.zeros((1, 1), F32)
    for e in range(N_EXPERTS):
        offs.append(acc)
        acc = acc + tile_cnt[e:e + 1, :]
    tile_off = jnp.concatenate(offs, axis=0)
    pos = tile_off + pre
    p1 = jnp.sum(jnp.where(oh1, pos, 0.0), axis=0, keepdims=True)
    p2 = jnp.sum(jnp.where(oh2, pos, 0.0), axis=0, keepdims=True)
    lpos_ref[...] = jnp.concatenate([p1, p2], axis=0).astype(I32) * ROW_CHUNKS

    @pl.when(i == 0)
    def _():
        for ref in (tcnt_ref, toff_ref, tbef_ref):
            ref[...] = jnp.zeros_like(ref)

    here = lax.broadcasted_iota(I32, (1, LANES), 1) == i
    tcnt_ref[...] = jnp.where(here, tile_cnt, tcnt_ref[...])
    toff_ref[...] = jnp.where(here, tile_off, toff_ref[...])
    tbef_ref[...] = jnp.where(here, cnt_ref[:, 0:1], tbef_ref[...])
    cnt_ref[...] += tile_cnt


def _router(xr, rw, rb, tm):
    n = xr.shape[0] // ROW_CHUNKS
    assert n // tm <= LANES
    table = pl.BlockSpec((N_EXPERTS, LANES), lambda i: (0, 0))
    return pl.pallas_call(
        functools.partial(_router_kernel, tm=tm),
        grid=(n // tm,),
        in_specs=[pl.BlockSpec((tm * ROW_CHUNKS, LANES), lambda i: (i, 0)),
                  pl.BlockSpec((D_MODEL, LANES), lambda i: (0, 0)),
                  pl.BlockSpec((D_MODEL, LANES), lambda i: (0, 0)),
                  pl.BlockSpec((N_EXPERTS, 1), lambda i: (0, 0))],
        out_specs=[pl.BlockSpec((2, tm), lambda i: (0, i)),
                   pl.BlockSpec((2, tm), lambda i: (0, i)),
                   table, table, table, table],
        out_shape=[jax.ShapeDtypeStruct((2, n), I32),
                   jax.ShapeDtypeStruct((2, n), F32)]
                  + [jax.ShapeDtypeStruct((N_EXPERTS, LANES), F32)] * 4,
        compiler_params=_cparams("arbitrary"),
        name="router",
    )(xr, rw[0], rw[1], rb)


def _plan_kernel(cnt_ref, tbef_ref, meta_ref, rstart_ref, *, nbl):
    shift = MOE_BM.bit_length() - 1
    cnt = cnt_ref[...].astype(I32)
    padded = ((cnt + (MOE_BM - 1)) >> shift) << shift
    starts = []
    acc = jnp.zeros((1, LANES), I32)
    for e in range(N_EXPERTS):
        starts.append(acc)
        acc = acc + padded[e:e + 1, :]
    pad_start = jnp.concatenate(starts, axis=0)
    pad_end = pad_start + padded
    rstart_ref[...] = pad_start + tbef_ref[...].astype(I32)
    blk0 = lax.broadcasted_iota(I32, (N_EXPERTS, nbl), 1) * MOE_BM
    block_e = jnp.sum((pad_end[:, 0:1] <= blk0).astype(I32), axis=0, keepdims=True)
    block_e = jnp.minimum(block_e, N_EXPERTS - 1)
    n_used = jnp.broadcast_to(acc[:, 0:1] >> shift, (1, nbl))
    diag = lax.broadcasted_iota(I32, (N_EXPERTS, nbl), 0) == lax.broadcasted_iota(I32, (N_EXPERTS, nbl), 1)
    fill_lo = jnp.sum(jnp.where(diag, (pad_start + cnt)[:, 0:1], 0), axis=0, keepdims=True)
    fill_hi = jnp.sum(jnp.where(diag, pad_end[:, 0:1], 0), axis=0, keepdims=True)
    meta_ref[...] = jnp.concatenate([block_e, n_used, fill_lo, fill_hi, jnp.zeros((SUBLANES - 4, nbl), I32)],
                                    axis=0)


def _plan(cnt, tbef, n_blocks):
    nbl = -(-n_blocks // LANES) * LANES
    table = pl.BlockSpec((N_EXPERTS, LANES), lambda i: (0, 0))
    return pl.pallas_call(
        functools.partial(_plan_kernel, nbl=nbl),
        grid=(1,),
        in_specs=[table, table],
        out_specs=[pl.BlockSpec((SUBLANES, nbl), lambda i: (0, 0)), table],
        out_shape=[jax.ShapeDtypeStruct((SUBLANES, nbl), I32),
                   jax.ShapeDtypeStruct((N_EXPERTS, LANES), I32)],
        compiler_params=_cparams("arbitrary"),
        name="moe_plan",
    )(cnt, tbef)


def _rows(ref, row, nrows):
    return ref.at[pl.ds(pl.multiple_of(row * ROW_CHUNKS, ROW_CHUNKS), nrows * ROW_CHUNKS), :]


def _rows_wait(src_hbm, buf, sem):
    pltpu.make_async_copy(src_hbm.at[pl.ds(0, buf.shape[0]), :], buf, sem).wait()


def _copy_pieces(src, src_row, dst, dst_row, count, max_rows, sem, wait=False):
    bit = max_rows.bit_length() - 1
    while bit >= 0:
        size = 1 << bit
        done = (count >> (bit + 1)) << (bit + 1)

        @pl.when(((count >> bit) & 1) == 1)
        def _():
            cp = pltpu.make_async_copy(_rows(src, src_row + done, size), _rows(dst, dst_row + done, size), sem)
            cp.start()
            if wait:
                cp.wait()

        bit -= 1


def _tile_runs(tcnt_ref, toff_ref, rstart_ref, tile, buf, hbm, sem, *, to_hbm, tm):
    def per_expert(e, carry):
        k = tile * N_EXPERTS + e
        if to_hbm:
            _copy_pieces(buf, toff_ref[k], hbm, rstart_ref[k], tcnt_ref[k], tm, sem)
        else:
            _copy_pieces(hbm, rstart_ref[k], buf, toff_ref[k], tcnt_ref[k], tm, sem)
        return carry

    lax.fori_loop(0, N_EXPERTS, per_expert, 0)


def _dispatch_kernel(lpos_ref, tcnt_ref, toff_ref, rstart_ref, flo_ref, fhi_ref, nu_ref, x_ref, xs_hbm,
                     s0, s1, zbuf, sem, zsem, *, n, tm, n_blocks):
    i = pl.program_id(0)
    nt = pl.num_programs(0)
    bufs = (s0, s1)
    unroll = 8

    for slot in range(2):
        @pl.when(i % 2 == slot)
        def _():
            buf = bufs[slot]

            @pl.when(i >= 2)
            def _():
                _rows_wait(xs_hbm, buf, sem.at[slot])

            def place(c, carry):
                tok = i * tm + c * unroll
                src = pl.multiple_of(c * (unroll * ROW_CHUNKS), unroll * ROW_CHUNKS)
                for u in range(unroll):
                    v = x_ref[pl.ds(src + u * ROW_CHUNKS, ROW_CHUNKS), :]
                    for k in range(2):
                        p = lpos_ref[k * n + tok + u]
                        buf[pl.ds(pl.multiple_of(p, ROW_CHUNKS), ROW_CHUNKS), :] = v
                return carry

            lax.fori_loop(0, tm // unroll, place, 0)
            _tile_runs(tcnt_ref, toff_ref, rstart_ref, i, buf, xs_hbm, sem.at[slot], to_hbm=True, tm=tm)

    @pl.when(i == nt - 1)
    def _():
        for slot in range(2):
            @pl.when(nt > slot)
            def _():
                _rows_wait(xs_hbm, bufs[slot], sem.at[slot])

        zbuf[...] = jnp.zeros_like(zbuf)
        for e in range(N_EXPERTS):
            _copy_pieces(zbuf, 0, xs_hbm, flo_ref[e], fhi_ref[e] - flo_ref[e], MOE_BM // 2, zsem, wait=True)

        def zero_block(j, carry):
            cp = pltpu.make_async_copy(zbuf, _rows(xs_hbm, j * MOE_BM, MOE_BM), zsem)
            cp.start()
            cp.wait()
            return carry

        lax.fori_loop(nu_ref[0], n_blocks, zero_block, 0)


def _dispatch(lpos_flat, tcnt, toff, rstart, fill_lo, fill_hi, n_used, xr, n_blocks, tm):
    n = xr.shape[0] // ROW_CHUNKS
    return pl.pallas_call(
        functools.partial(_dispatch_kernel, n=n, tm=tm, n_blocks=n_blocks),
        grid_spec=pltpu.PrefetchScalarGridSpec(
            num_scalar_prefetch=7,
            grid=(n // tm,),
            in_specs=[pl.BlockSpec((tm * ROW_CHUNKS, LANES), lambda i, *_: (i, 0))],
            out_specs=pl.BlockSpec(memory_space=pl.ANY),
            scratch_shapes=[pltpu.VMEM((2 * tm * ROW_CHUNKS, LANES), F32),
                            pltpu.VMEM((2 * tm * ROW_CHUNKS, LANES), F32),
                            pltpu.VMEM((MOE_BM * ROW_CHUNKS, LANES), F32),
                            pltpu.SemaphoreType.DMA((2,)),
                            pltpu.SemaphoreType.DMA(())]),
        out_shape=jax.ShapeDtypeStruct((n_blocks * MOE_BM * ROW_CHUNKS, LANES), F32),
        compiler_params=_cparams("arbitrary"),
        name="moe_dispatch",
    )(lpos_flat, tcnt, toff, rstart, fill_lo, fill_hi, n_used, xr)


def _experts_kernel(be_ref, nu_ref, xs_ref, wg_ref, wu_ref, wd_ref, y_ref, wgb, wub, wdb):
    j = pl.program_id(0)
    used = j < nu_ref[0]

    @pl.when(jnp.logical_and(used, jnp.logical_or(j == 0, be_ref[j] != be_ref[jnp.maximum(j - 1, 0)])))
    def _():
        wgb[...] = wg_ref[0, 0].astype(BF16)
        wub[...] = wu_ref[0, 0].astype(BF16)
        wdb[...] = wd_ref[0, 0].astype(BF16)

    @pl.when(used)
    def _():
        x = _read_rows(xs_ref, MOE_BM).astype(BF16)
        h = _silu(_dot(x, wgb[...])) * _dot(x, wub[...])
        _write_rows(y_ref, _dot(h.astype(BF16), wdb[...]), MOE_BM)

    @pl.when(jnp.logical_not(used))
    def _():
        y_ref[...] = jnp.zeros_like(y_ref)


def _experts(block_e, n_used, xs, wg, wu, wd, layer):
    n_blocks = block_e.shape[0]

    def last_used(j, nu):
        return jnp.minimum(j, nu[0] - 1)

    def wblk(j, be, nu):
        return (layer, be[last_used(j, nu)], 0, 0)

    return pl.pallas_call(
        _experts_kernel,
        grid_spec=pltpu.PrefetchScalarGridSpec(
            num_scalar_prefetch=2,
            grid=(n_blocks,),
            in_specs=[pl.BlockSpec((MOE_BM * ROW_CHUNKS, LANES), lambda j, be, nu: (last_used(j, nu), 0)),
                      pl.BlockSpec((1, 1, D_MODEL, D_EXPERT), wblk),
                      pl.BlockSpec((1, 1, D_MODEL, D_EXPERT), wblk),
                      pl.BlockSpec((1, 1, D_EXPERT, D_MODEL), wblk)],
            out_specs=pl.BlockSpec((MOE_BM * ROW_CHUNKS, LANES), lambda j, be, nu: (j, 0)),
            scratch_shapes=[pltpu.VMEM((D_MODEL, D_EXPERT), BF16), pltpu.VMEM((D_MODEL, D_EXPERT), BF16),
                            pltpu.VMEM((D_EXPERT, D_MODEL), BF16)]),
        out_shape=jax.ShapeDtypeStruct(xs.shape, F32),
        compiler_params=_cparams("arbitrary"),
        name="moe_experts",
    )(block_e, n_used, xs, wg, wu, wd)


def _combine_ln_kernel(lpos_ref, tcnt_ref, toff_ref, rstart_ref, y_hbm, x_ref, w1_ref, w2_ref, g_ref, b_ref, o_ref,
                       r0, r1, u1, u2, sem, *, n, tm):
    i = pl.program_id(0)
    nt = pl.num_programs(0)
    bufs = (r0, r1)
    unroll = 8

    def fetch(tile, slot):
        _tile_runs(tcnt_ref, toff_ref, rstart_ref, tile, bufs[slot], y_hbm, sem.at[slot], to_hbm=False, tm=tm)

    @pl.when(i == 0)
    def _():
        fetch(0, 0)

    for slot in range(2):
        @pl.when(i % 2 == slot)
        def _():
            @pl.when(i + 1 < nt)
            def _():
                fetch(i + 1, 1 - slot)

            buf = bufs[slot]
            _rows_wait(y_hbm, buf, sem.at[slot])

            def place(c, carry):
                tok = i * tm + c * unroll
                dst0 = pl.multiple_of(c * (unroll * ROW_CHUNKS), unroll * ROW_CHUNKS)
                for u in range(unroll):
                    dst = pl.ds(dst0 + u * ROW_CHUNKS, ROW_CHUNKS)
                    for k, out in enumerate((u1, u2)):
                        p = lpos_ref[k * n + tok + u]
                        out[dst, :] = buf[pl.ds(pl.multiple_of(p, ROW_CHUNKS), ROW_CHUNKS), :]
                return carry

            lax.fori_loop(0, tm // unroll, place, 0)
            moe = w1_ref[...] * _read_rows(u1, tm) + w2_ref[...] * _read_rows(u2, tm)
            o_ref[...] = _ln(ALPHA * _read_rows(x_ref, tm) + moe, g_ref[...], b_ref[...])


def _combine_ln(lpos_flat, tcnt, toff, rstart, y, xr, w1, w2, g, b, tm):
    n = xr.shape[0] // ROW_CHUNKS
    return pl.pallas_call(
        functools.partial(_combine_ln_kernel, n=n, tm=tm),
        grid_spec=pltpu.PrefetchScalarGridSpec(
            num_scalar_prefetch=4,
            grid=(n // tm,),
            in_specs=[pl.BlockSpec(memory_space=pl.ANY),
                      pl.BlockSpec((tm * ROW_CHUNKS, LANES), lambda i, *_: (i, 0)),
                      pl.BlockSpec((tm, 1), lambda i, *_: (i, 0)),
                      pl.BlockSpec((tm, 1), lambda i, *_: (i, 0)),
                      pl.BlockSpec((1, D_MODEL), lambda i, *_: (0, 0)),
                      pl.BlockSpec((1, D_MODEL), lambda i, *_: (0, 0))],
            out_specs=pl.BlockSpec((tm, D_MODEL), lambda i, *_: (i, 0)),
            scratch_shapes=[pltpu.VMEM((2 * tm * ROW_CHUNKS, LANES), F32)] * 2
                           + [pltpu.VMEM((tm * ROW_CHUNKS, LANES), F32)] * 2
                           + [pltpu.SemaphoreType.DMA((2,))]),
        out_shape=jax.ShapeDtypeStruct((n, D_MODEL), F32),
        compiler_params=_cparams("arbitrary"),
        name="moe_combine_ln",
    )(lpos_flat, tcnt, toff, rstart, y, xr, w1, w2, g, b)


def _moe_ln(xr, rw, rb, wg, wu, wd, layer, g, b):
    n = xr.shape[0] // ROW_CHUNKS
    n_blocks = (2 * n) // MOE_BM + N_EXPERTS
    tm = MOE_TILE
    nt = n // tm
    lpos, w, cnt, tcnt, toff, tbef = _router(xr, rw, rb, tm)
    meta, rstart = _plan(cnt, tbef, n_blocks)
    block_e = meta[0, :n_blocks]
    n_used = meta[1, :1]

    def per_tile(table):
        return table[:, :nt].T.reshape(nt * N_EXPERTS).astype(I32)

    lpos_flat = lpos.reshape(2 * n)
    tcnt, toff, rstart = per_tile(tcnt), per_tile(toff), per_tile(rstart)
    xs = _dispatch(lpos_flat, tcnt, toff, rstart, meta[2, :N_EXPERTS], meta[3, :N_EXPERTS], n_used, xr,
                   n_blocks, tm)
    y = _experts(block_e, n_used, xs, wg, wu, wd, layer)
    return _combine_ln(lpos_flat, tcnt, toff, rstart, y, xr, w[0].reshape(n, 1), w[1].reshape(n, 1), g, b, tm)


def _conv_qkv_kernel(xm_ref, cw_ref, cb_ref, wq_ref, wk_ref, wv_ref, q_ref, k_ref, v_ref, xc_ref, *, s):
    xm_b = xm_ref[0]
    xm = xm_b.astype(F32)
    cw = cw_ref[...]
    row = lax.broadcasted_iota(I32, (s, 1), 0)
    half = CONV_K // 2
    acc = cb_ref[...] + xm * cw[half:half + 1, :]
    for sh in range(1, half + 1):
        past = jnp.where(row >= sh, pltpu.roll(xm, sh, axis=0), 0.0)
        acc = acc + past * cw[half - sh:half - sh + 1, :]
        nxt = jnp.where(row < s - sh, pltpu.roll(xm, s - sh, axis=0), 0.0)
        acc = acc + nxt * cw[half + sh:half + sh + 1, :]
    xc = _silu(acc).astype(BF16)
    xc_ref[0] = xc
    q_ref[0] = _dot(xc, wq_ref[0]).astype(BF16)
    k_ref[0] = (_dot_nt(wk_ref[0], xc) * (ML_DH ** -0.5)).astype(BF16)
    v = _dot(xm_b, wv_ref[0])
    ones_lane = lax.broadcasted_iota(I32, (1, ML_DHP), 1) == ML_DH
    v_ref[0] = jnp.where(ones_lane, 1.0, v).astype(BF16)


def _conv_qkv(main3, cw, cb, wq, wk_t, wv):
    b, s, _ = main3.shape
    tok = pl.BlockSpec((1, s, ML_DHP), lambda i, h: (i, 0, h))
    wspec = pl.BlockSpec((1, ML_DHP, ML_DHP), lambda i, h: (h, 0, 0))
    tok_shape = jax.ShapeDtypeStruct((b, s, ML_WP), BF16)
    return pl.pallas_call(
        functools.partial(_conv_qkv_kernel, s=s),
        grid=(b, ML_HEADS),
        in_specs=[tok,
                  pl.BlockSpec((CONV_K, ML_DHP), lambda i, h: (0, h)),
                  pl.BlockSpec((1, ML_DHP), lambda i, h: (0, h)),
                  wspec, wspec, wspec],
        out_specs=[tok, pl.BlockSpec((1, ML_DHP, s), lambda i, h: (i, h, 0)), tok, tok],
        out_shape=[tok_shape, jax.ShapeDtypeStruct((b, ML_WP, s), BF16), tok_shape, tok_shape],
        compiler_params=_cparams("parallel", "parallel"),
        name="conv_qkv",
    )(main3, cw, cb, wq, wk_t, wv)


def _mlstm_kernel(q_ref, kt_ref, v_ref, gc_ref, gr_ref, z_ref, xc_ref, ng_ref, sk_ref,
                  y_ref, hf_ref, hb_ref, cf_ref, cb_ref, m_ref, *, s):
    head0 = pl.program_id(1) * ML_HPS
    nc = s // CHUNK
    sub = lax.broadcasted_iota(I32, (LANES, 1), 0)
    gate = lax.broadcasted_iota(I32, (LANES, LANES), 0)
    ti = lax.broadcasted_iota(I32, (CHUNK, CHUNK), 0)
    tj = lax.broadcasted_iota(I32, (CHUNK, CHUNK), 1)

    for ref in (cf_ref, cb_ref, m_ref):
        ref[...] = jnp.zeros_like(ref)

    def intra(c, j, rev):
        t0 = pl.multiple_of(c * CHUNK, CHUNK)
        hl = slice(j * ML_DHP, (j + 1) * ML_DHP)
        qb = q_ref[0, pl.ds(t0, CHUNK), hl]
        kt = kt_ref[0, hl, pl.ds(t0, CHUNK)]
        vb = v_ref[0, pl.ds(t0, CHUNK), hl]
        gc = gc_ref[0, pl.ds(t0, CHUNK), :]
        gr = gr_ref[:, pl.ds(t0, CHUNK)]
        i_idx = head0 + j + (2 * ML_HEADS if rev else 0)
        f_idx = i_idx + ML_HEADS
        allowed = (tj >= ti) if rev else (tj <= ti)
        sel = (gate == f_idx).astype(BF16)
        b_rep = sum(_dot(part, sel) for part in _split3(gc))
        b_row = jnp.sum(jnp.where(sub == f_idx, gr, 0.0), axis=0, keepdims=True)
        i_row = jnp.sum(jnp.where(sub == i_idx, gr, 0.0), axis=0, keepdims=True)
        b_last = b_rep[0:1, :] if rev else b_rep[CHUNK - 1:CHUNK, :]

        d = jnp.where(allowed, b_rep - b_row + i_row, NEG)
        m_in = jnp.max(d, axis=1, keepdims=True)
        sc = _dot(qb, kt) * jnp.exp(d - m_in)
        nd_in = _dot(sc.astype(BF16), vb)
        w_row = b_last - b_row + i_row
        return t0, qb, kt, vb, b_rep, b_last[:, 0:1], m_in, nd_in, w_row

    def twice(a):
        return jnp.concatenate([a, a], axis=1)

    def update(parts, j, rev):
        t0, qb, kt, vb, b_rep, b_last, m_in, nd_in, w_row = parts
        h_ref, c_ref = (hb_ref, cb_ref) if rev else (hf_ref, cf_ref)
        hl = slice(j * ML_DHP, (j + 1) * ML_DHP)
        mrow = 2 * j + int(rev)
        m = m_ref[mrow:mrow + 1, 0:1]
        cmat = c_ref[j]
        inter = b_rep + m
        m_t = jnp.maximum(m_in, inter)
        a_in = jnp.exp(m_in - m_t)
        iexp = jnp.exp(inter - m_t)
        nd = twice(a_in) * nd_in + twice(iexp) * _dot(qb, cmat.astype(BF16))
        den = nd[:, ML_DH:ML_DH + 1]
        h_ref[pl.ds(t0, CHUNK), hl] = nd * (1.0 / jnp.maximum(jnp.abs(den), jnp.exp(-m_t[:, 0:1])))

        m_new = jnp.maximum(b_last + m, jnp.max(w_row, axis=1, keepdims=True))
        wexp = jnp.exp(w_row - m_new)
        cexp = jnp.exp(b_last + m - m_new)
        kw = (kt.astype(F32) * wexp).astype(BF16)
        c_ref[j] = cexp * cmat + _dot(kw, vb)
        m_ref[mrow:mrow + 1, :] = jnp.broadcast_to(m_new, (1, LANES))

    def step(i, carry):
        chains = [(j, rev) for j in range(ML_HPS) for rev in (False, True)]
        parts = [intra(nc - 1 - i if rev else i, j, rev) for j, rev in chains]
        for p, (j, rev) in zip(parts, chains):
            update(p, j, rev)
        return carry

    lax.fori_loop(0, nc, step, 0)

    real = lax.broadcasted_iota(I32, (1, ML_DHP), 1) < ML_DH
    tb = 256

    def fin(c, carry):
        t0 = pl.multiple_of(c * tb, tb)
        for j in range(ML_HPS):
            hl = slice(j * ML_DHP, (j + 1) * ML_DHP)
            hs = jnp.where(real, hf_ref[pl.ds(t0, tb), hl] + hb_ref[pl.ds(t0, tb), hl], 0.0)
            mu = jnp.sum(hs, axis=1, keepdims=True) * (1.0 / ML_DH)
            dev = jnp.where(real, hs - mu, 0.0)
            var = jnp.sum(dev * dev, axis=1, keepdims=True) * (1.0 / ML_DH)
            hn = dev * lax.rsqrt(var + LN_EPS) * ng_ref[:, hl]
            xc = xc_ref[0, pl.ds(t0, tb), hl].astype(F32)
            z = z_ref[0, pl.ds(t0, tb), hl].astype(F32)
            y_ref[0, pl.ds(t0, tb), hl] = ((hn + sk_ref[:, hl] * xc) * _silu(z)).astype(BF16)
        return carry

    lax.fori_loop(0, s // tb, fin, 0)


def _mlstm(q, kt, v, gcol3, grow, main3, xc, ng, sk):
    b, s, _ = q.shape
    width = ML_HPS * ML_DHP
    steps = ML_HEADS // ML_HPS
    tok = pl.BlockSpec((1, s, width), lambda i, h: (i, 0, h))
    vec = pl.BlockSpec((1, width), lambda i, h: (0, h))
    return pl.pallas_call(
        functools.partial(_mlstm_kernel, s=s),
        grid=(b, steps),
        in_specs=[tok, pl.BlockSpec((1, width, s), lambda i, h: (i, h, 0)), tok,
                  pl.BlockSpec((1, s, LANES), lambda i, h: (i, 0, 0)),
                  pl.BlockSpec((LANES, s), lambda i, h: (0, i)),
                  pl.BlockSpec((1, s, width), lambda i, h: (i, 0, steps + h)),
                  tok, vec, vec],
        out_specs=tok,
        out_shape=jax.ShapeDtypeStruct((b, s, ML_WP), BF16),
        scratch_shapes=[pltpu.VMEM((s, width), F32), pltpu.VMEM((s, width), F32),
                        pltpu.VMEM((ML_HPS, ML_DHP, ML_DHP), F32), pltpu.VMEM((ML_HPS, ML_DHP, ML_DHP), F32),
                        pltpu.VMEM((SUBLANES, LANES), F32)],
        compiler_params=_cparams("parallel", "parallel"),
        name="mlstm",
    )(q, kt, v, gcol3, grow, main3, xc, ng, sk)


def _pad_heads(a, axis):
    a = jnp.moveaxis(a, axis, -1)
    lead = a.shape[:-1]
    a = a.reshape(lead + (ML_HEADS, ML_DH))
    a = jnp.pad(a, [(0, 0)] * len(lead) + [(0, 0), (0, ML_DHP - ML_DH)])
    return jnp.moveaxis(a.reshape(lead + (ML_WP,)), -1, axis)


def kernel(x, mem, mem_ln_g, mem_ln_b, w_mem_kv, router_w, router_b, na_w_in, na_rpb, ml_w_in, ml_conv_w,
           ml_conv_b, ml_w_qkv, ml_gate_b, ml_norm_g, ml_skip, w_out, ln_g, ln_b, exp_w_gate, exp_w_up,
           exp_w_down):
    b, s, d = x.shape
    n = b * s
    nm = mem.shape[1]
    row = lambda a: a.reshape(1, -1)

    mem_k, mem_v = _memkv(mem.reshape(b * nm, d), row(mem_ln_g), row(mem_ln_b), w_mem_kv.astype(BF16))
    mem_k3 = mem_k.reshape(b, nm, MEM_W)
    mem_v3 = mem_v.reshape(b, nm, MEM_W)
    rw_pad = jnp.pad(router_w, ((0, 0), (0, LANES - N_EXPERTS)))
    rw_hi = rw_pad.astype(BF16)
    rw = (rw_hi, (rw_pad - rw_hi.astype(F32)).astype(BF16))
    rb = router_b.reshape(N_EXPERTS, 1)

    x2 = x.reshape(n, d)

    h0 = _proj(x2, na_w_in[0].astype(BF16)).reshape(b, s, 3 * NA_W + MEM_W)
    y_na = _na_attention(h0, _na_bias_table(na_rpb[0]))
    wo = w_out[0].astype(BF16)
    xr = _outproj_ln(y_na.reshape(n, NA_W), h0.reshape(n, 3 * NA_W + MEM_W), 3 * NA_W // MEM_W, mem_k3, mem_v3,
                     wo[:NA_W], wo[NA_W:], x2, row(ln_g[0, 0]), row(ln_b[0, 0]))
    x2 = _moe_ln(xr, rw, rb, exp_w_gate, exp_w_up, exp_w_down, 0, row(ln_g[0, 1]), row(ln_b[0, 1]))

    w1 = ml_w_in[0]
    w_main = jnp.concatenate([_pad_heads(w1[:, :ML_W], 1), _pad_heads(w1[:, ML_W:2 * ML_W], 1),
                              w1[:, 2 * ML_W + 4 * ML_HEADS:]], axis=1).astype(BF16)
    w_g = jnp.pad(w1[:, 2 * ML_W:2 * ML_W + 4 * ML_HEADS], ((0, 0), (0, LANES - 4 * ML_HEADS))).astype(BF16)
    gb = jnp.pad(ml_gate_b[0].reshape(4 * ML_HEADS), (0, LANES - 4 * ML_HEADS))
    main, acol, arow = _proj_gates(x2, w_main, w_g, w_g.T, gb.reshape(1, LANES), gb.reshape(LANES, 1))
    main3 = main.reshape(b, s, 2 * ML_WP + MEM_W)
    wqkv = jnp.pad(ml_w_qkv[0], ((0, 0), (0, 0), (0, ML_DHP - ML_DH), (0, ML_DHP - ML_DH))).astype(BF16)
    q, k, v, xc = _conv_qkv(main3, _pad_heads(ml_conv_w[0], 1), _pad_heads(row(ml_conv_b[0]), 1),
                            wqkv[0], jnp.swapaxes(wqkv[1], 1, 2), wqkv[2])
    y_ml = _mlstm(q, k, v, acol.reshape(b, s, LANES), arow, main3, xc,
                  _pad_heads(row(ml_norm_g[0]), 1), _pad_heads(row(ml_skip[0]), 1))
    wo = w_out[1]
    xr = _outproj_ln(y_ml.reshape(n, ML_WP), main, 2 * ML_WP // MEM_W, mem_k3, mem_v3,
                     _pad_heads(wo[:ML_W], 0).astype(BF16), wo[ML_W:].astype(BF16), x2,
                     row(ln_g[1, 0]), row(ln_b[1, 0]))
    x2 = _moe_ln(xr, rw, rb, exp_w_gate, exp_w_up, exp_w_down, 1, row(ln_g[1, 1]), row(ln_b[1, 1]))
    return x2.reshape(b, s, d)
```

```python
import functools

import numpy as np
import jax
import jax.numpy as jnp
from jax import lax
from jax.experimental import pallas as pl
from jax.experimental.pallas import tpu as pltpu

F32 = jnp.float32
BF16 = jnp.bfloat16
I32 = jnp.int32

D_MODEL = 1024
DEPTH = 2
GRID_W = 64
MEM_HEADS = 4
MEM_DH = 64
MEM_W = MEM_HEADS * MEM_DH
NA_HEADS = 12
NA_DH = 64
NA_W = NA_HEADS * NA_DH
WIN_H = 8
WIN_W = 16
ML_HEADS = 4
ML_DH = 192
ML_DHP = 256
ML_W = ML_HEADS * ML_DH
ML_WP = ML_HEADS * ML_DHP
CONV_K = 5
CHUNK = 128
N_EXPERTS = 16
N_GROUPS = 4
EXPERTS_PER_GROUP = N_EXPERTS // N_GROUPS
D_EXPERT = 512
ALPHA = (2 * DEPTH) ** 0.25
LN_EPS = 1e-5
NEG = -1e30

LANES = 128
SUBLANES = 8
ROW_CHUNKS = D_MODEL // LANES
MOE_BM = 512
MOE_TILE = 512
ML_HPS = 2
NA_ROWS_PER_STEP = 8
VMEM_LIMIT = 48 * 1024 * 1024


def _cparams(*sem):
    return pltpu.CompilerParams(dimension_semantics=sem, vmem_limit_bytes=VMEM_LIMIT)


def _dot(a, b):
    return jnp.dot(a, b, preferred_element_type=F32)


def _dot_nt(a, b, precision=None):
    return lax.dot_general(a, b, (((1,), (1,)), ((), ())), precision=precision,
                           preferred_element_type=F32)


def _ln(z, g, b):
    mu = jnp.mean(z, axis=-1, keepdims=True)
    zc = z - mu
    var = jnp.mean(zc * zc, axis=-1, keepdims=True)
    return zc * lax.rsqrt(var + LN_EPS) * g + b


def _silu(x):
    return x * jax.nn.sigmoid(x)


def _read_rows(ref, n):
    return jnp.concatenate([ref[pl.ds(j, n, stride=ROW_CHUNKS), :] for j in range(ROW_CHUNKS)], axis=1)


def _write_rows(ref, val, n):
    for j in range(ROW_CHUNKS):
        ref[pl.ds(j, n, stride=ROW_CHUNKS), :] = val[:, j * LANES:(j + 1) * LANES]


def _memkv_kernel(m_ref, g_ref, b_ref, w_ref, k_ref, v_ref):
    z = _ln(m_ref[...], g_ref[...], b_ref[...])
    kv = _dot(z.astype(BF16), w_ref[...])
    k_ref[...] = kv[:, :MEM_W].astype(BF16)
    v_ref[...] = kv[:, MEM_W:].astype(BF16)


def _memkv(mem2, g, b, w):
    n = mem2.shape[0]
    tm = 256
    return pl.pallas_call(
        _memkv_kernel,
        grid=(n // tm,),
        in_specs=[pl.BlockSpec((tm, D_MODEL), lambda i: (i, 0)),
                  pl.BlockSpec((1, D_MODEL), lambda i: (0, 0)),
                  pl.BlockSpec((1, D_MODEL), lambda i: (0, 0)),
                  pl.BlockSpec((D_MODEL, 2 * MEM_W), lambda i: (0, 0))],
        out_specs=[pl.BlockSpec((tm, MEM_W), lambda i: (i, 0)),
                   pl.BlockSpec((tm, MEM_W), lambda i: (i, 0))],
        out_shape=[jax.ShapeDtypeStruct((n, MEM_W), BF16)] * 2,
        compiler_params=_cparams("parallel"),
        name="memkv",
    )(mem2, g, b, w)


def _proj_kernel(x_ref, w_ref, o_ref):
    o_ref[...] = _dot(x_ref[...].astype(BF16), w_ref[...]).astype(o_ref.dtype)


def _proj(x2, w, tm=512):
    n, k = x2.shape
    nout = w.shape[1]
    return pl.pallas_call(
        _proj_kernel,
        grid=(n // tm,),
        in_specs=[pl.BlockSpec((tm, k), lambda i: (i, 0)),
                  pl.BlockSpec((k, nout), lambda i: (0, 0))],
        out_specs=pl.BlockSpec((tm, nout), lambda i: (i, 0)),
        out_shape=jax.ShapeDtypeStruct((n, nout), BF16),
        compiler_params=_cparams("parallel"),
        name="in_proj",
    )(x2, w)


def _split3(x):
    hi = x.astype(BF16)
    r1 = x - hi.astype(F32)
    mid = r1.astype(BF16)
    lo = (r1 - mid.astype(F32)).astype(BF16)
    return hi, mid, lo


def _proj_gates_kernel(x_ref, w_ref, wg_ref, wgt_ref, gbc_ref, gbr_ref, o_ref, g_ref, gt_ref, *, tm):
    xb = x_ref[...].astype(BF16)
    o_ref[...] = _dot(xb, w_ref[...]).astype(BF16)
    gcol = _dot(xb, wg_ref[...]) + gbc_ref[...]
    grow = _dot_nt(wgt_ref[...], xb) + gbr_ref[...]
    lane = lax.broadcasted_iota(I32, (1, LANES), 1)
    sub = lax.broadcasted_iota(I32, (LANES, 1), 0)
    ti = lax.broadcasted_iota(I32, (CHUNK, CHUNK), 0)
    tj = lax.broadcasted_iota(I32, (CHUNK, CHUNK), 1)
    lower = (tj <= ti).astype(BF16)
    upper = (ti <= tj).astype(BF16)

    def pick(idx, pre, suf, raw):
        fwd = jnp.logical_and(idx >= ML_HEADS, idx < 2 * ML_HEADS)
        bwd = jnp.logical_and(idx >= 3 * ML_HEADS, idx < 4 * ML_HEADS)
        return jnp.where(fwd, pre, jnp.where(bwd, suf, raw))

    for c in range(tm // CHUNK):
        tc = slice(c * CHUNK, (c + 1) * CHUNK)
        g = gcol[tc, :]
        ls = jax.nn.log_sigmoid(g)
        pre = sum(_dot(lower, part) for part in _split3(ls))
        suf = jnp.sum(ls, axis=0, keepdims=True) - pre + ls
        g_ref[tc, :] = pick(lane, pre, suf, g)
        g = grow[:, tc]
        ls = jax.nn.log_sigmoid(g)
        pre = sum(_dot(part, upper) for part in _split3(ls))
        suf = jnp.sum(ls, axis=1, keepdims=True) - pre + ls
        gt_ref[:, tc] = pick(sub, pre, suf, g)


def _proj_gates(x2, w, wg, wgt, gbc, gbr, tm=512):
    n, k = x2.shape
    nout = w.shape[1]
    return pl.pallas_call(
        functools.partial(_proj_gates_kernel, tm=tm),
        grid=(n // tm,),
        in_specs=[pl.BlockSpec((tm, k), lambda i: (i, 0)),
                  pl.BlockSpec((k, nout), lambda i: (0, 0)),
                  pl.BlockSpec((k, LANES), lambda i: (0, 0)),
                  pl.BlockSpec((LANES, k), lambda i: (0, 0)),
                  pl.BlockSpec((1, LANES), lambda i: (0, 0)),
                  pl.BlockSpec((LANES, 1), lambda i: (0, 0))],
        out_specs=[pl.BlockSpec((tm, nout), lambda i: (i, 0)),
                   pl.BlockSpec((tm, LANES), lambda i: (i, 0)),
                   pl.BlockSpec((LANES, tm), lambda i: (0, i))],
        out_shape=[jax.ShapeDtypeStruct((n, nout), BF16),
                   jax.ShapeDtypeStruct((n, LANES), F32),
                   jax.ShapeDtypeStruct((LANES, n), F32)],
        compiler_params=_cparams("parallel"),
        name="in_proj_gates",
    )(x2, w, wg, wgt, gbc, gbr)


def _na_kernel(q_ref, k_ref, v_ref, tbl_ref, o_ref, *, rows):
    lane = lax.broadcasted_iota(I32, (1, LANES), 1)
    first = lane < NA_DH
    nkeys = WIN_H * GRID_W

    def rows_step(i, carry):
        rr = [i * NA_ROWS_PER_STEP + u for u in range(NA_ROWS_PER_STEP)]
        rss = [jnp.clip(r - WIN_H // 2, 0, rows - WIN_H) for r in rr]
        scores = []
        for r, rs in zip(rr, rss):
            q = q_ref[0, pl.ds(pl.multiple_of(r * GRID_W, GRID_W), GRID_W), :]
            q = q * jnp.asarray(NA_DH ** -0.5, BF16)
            q2 = jnp.concatenate([jnp.where(first, q, jnp.zeros_like(q)),
                                  jnp.where(first, jnp.zeros_like(q), q)], axis=0)
            k = k_ref[0, pl.ds(pl.multiple_of(rs * GRID_W, GRID_W), nkeys), :]
            dr0 = rs - r + WIN_H - 1
            bias = jnp.concatenate(
                [jnp.concatenate([tbl_ref[0, half, dr0 + 2 * m] for m in range(WIN_H // 2)], axis=1)
                 for half in range(2)], axis=0)
            scores.append(_dot_nt(q2, k) + bias)
        probs = []
        for s in scores:
            p = jnp.exp(s - jnp.max(s, axis=-1, keepdims=True))
            probs.append((p.astype(BF16), jnp.sum(p, axis=-1, keepdims=True)))
        for r, rs, (p, l) in zip(rr, rss, probs):
            v = v_ref[0, pl.ds(pl.multiple_of(rs * GRID_W, GRID_W), nkeys), :]
            o = _dot(p, v) / l
            o = jnp.where(first, o[:GRID_W], o[GRID_W:])
            o_ref[0, pl.ds(pl.multiple_of(r * GRID_W, GRID_W), GRID_W), :] = o.astype(o_ref.dtype)
        return carry

    lax.fori_loop(0, rows // NA_ROWS_PER_STEP, rows_step, 0)


def _na_bias_table(rpb):
    qc = np.arange(GRID_W)[:, None]
    kc = np.arange(GRID_W)[None, :]
    cs = np.clip(qc - WIN_W // 2, 0, GRID_W - WIN_W)
    col_in = (kc >= cs) & (kc < cs + WIN_W)
    side = GRID_W - WIN_W
    wide = jnp.pad(rpb, ((0, 0), (0, 0), (side, side)))
    t = jnp.stack([wide[:, :, GRID_W - 1 - q:2 * GRID_W - 1 - q] for q in range(GRID_W)], axis=2)
    t = jnp.where(col_in, t, NEG).astype(F32)
    t2 = jnp.concatenate([t[:, :-1], t[:, 1:]], axis=-1)
    return t2.reshape(NA_HEADS // 2, 2, 2 * WIN_H - 2, GRID_W, 2 * GRID_W)


def _na_attention(h3, tbl):
    b, s, _ = h3.shape
    rows = s // GRID_W
    npair = NA_HEADS // 2
    return pl.pallas_call(
        functools.partial(_na_kernel, rows=rows),
        grid=(b, npair),
        in_specs=[pl.BlockSpec((1, s, LANES), lambda i, p: (i, 0, p)),
                  pl.BlockSpec((1, s, LANES), lambda i, p: (i, 0, npair + p)),
                  pl.BlockSpec((1, s, LANES), lambda i, p: (i, 0, 2 * npair + p)),
                  pl.BlockSpec((1, 2, 2 * WIN_H - 2, GRID_W, 2 * GRID_W), lambda i, p: (p, 0, 0, 0, 0))],
        out_specs=pl.BlockSpec((1, s, LANES), lambda i, p: (i, 0, p)),
        out_shape=jax.ShapeDtypeStruct((b, s, NA_W), BF16),
        compiler_params=_cparams("parallel", "parallel"),
        name="na_attention",
    )(h3, h3, h3, tbl)


def _outproj_ln_kernel(ya_ref, qm_ref, mk_ref, mv_ref, wa_ref, wm_ref, x_ref, g_ref, b_ref, or_ref, *, tm):
    lane = lax.broadcasted_iota(I32, (1, LANES), 1)
    first = lane < MEM_DH
    q = qm_ref[...] * jnp.asarray(MEM_DH ** -0.5, BF16)
    cols = [slice(p * LANES, (p + 1) * LANES) for p in range(MEM_HEADS // 2)]
    scores = []
    for c in cols:
        qp = q[:, c]
        q2 = jnp.concatenate([jnp.where(first, qp, jnp.zeros_like(qp)),
                              jnp.where(first, jnp.zeros_like(qp), qp)], axis=0)
        scores.append(_dot_nt(q2, mk_ref[0, :, c]))
    probs = []
    for s in scores:
        p = jnp.exp(s - jnp.max(s, axis=-1, keepdims=True))
        probs.append((p.astype(BF16), jnp.sum(p, axis=-1, keepdims=True)))
    outs = []
    for c, (p, l) in zip(cols, probs):
        o = _dot(p, mv_ref[0, :, c]) / l
        outs.append(jnp.where(first, o[:tm], o[tm:]))
    ym = jnp.concatenate(outs, axis=1).astype(BF16)
    acc = _dot(ya_ref[...], wa_ref[...]) + _dot(ym, wm_ref[...])
    _write_rows(or_ref, _ln(ALPHA * x_ref[...] + acc, g_ref[...], b_ref[...]), tm)


def _outproj_ln(ya, h2, qm_block, mem_k3, mem_v3, wa, wm, x2, g, b, tm=512):
    n = x2.shape[0]
    ka = ya.shape[1]
    nb, nm, _ = mem_k3.shape
    per_batch = n // nb // tm
    full = lambda shape: pl.BlockSpec(shape, lambda i: (0,) * len(shape))
    return pl.pallas_call(
        functools.partial(_outproj_ln_kernel, tm=tm),
        grid=(n // tm,),
        in_specs=[pl.BlockSpec((tm, ka), lambda i: (i, 0)),
                  pl.BlockSpec((tm, MEM_W), lambda i: (i, qm_block)),
                  pl.BlockSpec((1, nm, MEM_W), lambda i: (i // per_batch, 0, 0)),
                  pl.BlockSpec((1, nm, MEM_W), lambda i: (i // per_batch, 0, 0)),
                  full((ka, D_MODEL)), full((MEM_W, D_MODEL)),
                  pl.BlockSpec((tm, D_MODEL), lambda i: (i, 0)),
                  full((1, D_MODEL)), full((1, D_MODEL))],
        out_specs=pl.BlockSpec((tm * ROW_CHUNKS, LANES), lambda i: (i, 0)),
        out_shape=jax.ShapeDtypeStruct((n * ROW_CHUNKS, LANES), F32),
        compiler_params=_cparams("parallel"),
        name="outproj_ln",
    )(ya, h2, mem_k3, mem_v3, wa, wm, x2, g, b)


def _router_kernel(x_ref, rwh_ref, rwl_ref, rb_ref, lpos_ref, w_ref, cnt_ref, tcnt_ref, toff_ref, tbef_ref, *, tm):
    @pl.when(pl.program_id(0) == 0)
    def _():
        cnt_ref[...] = jnp.zeros_like(cnt_ref)

    x = _read_rows(x_ref, tm)
    xh = x.astype(BF16)
    xl = (x - xh.astype(F32)).astype(BF16)
    logits_t = _dot(xh, rwh_ref[...]) + (_dot(xh, rwl_ref[...]) + _dot(xl, rwh_ref[...]))
    logits = logits_t.T[:N_EXPERTS]
    scores = jax.nn.sigmoid(logits)
    biased = scores + rb_ref[...]
    bv = [biased[e:e + 1, :] for e in range(N_EXPERTS)]
    sv = [scores[e:e + 1, :] for e in range(N_EXPERTS)]

    grp = []
    for g in range(N_GROUPS):
        m = bv[g * EXPERTS_PER_GROUP:(g + 1) * EXPERTS_PER_GROUP]
        best = None
        for a in range(EXPERTS_PER_GROUP):
            for c in range(a + 1, EXPERTS_PER_GROUP):
                pair = m[a] + m[c]
                best = pair if best is None else jnp.maximum(best, pair)
        grp.append(best)
    gsel = jnp.zeros((1, tm), I32)
    gbest = grp[0]
    for g in range(1, N_GROUPS):
        better = grp[g] > gbest
        gsel = jnp.where(better, g, gsel)
        gbest = jnp.where(better, grp[g], gbest)

    def pick(vals, j):
        out = vals[j]
        for g in range(1, N_GROUPS):
            out = jnp.where(gsel == g, vals[g * EXPERTS_PER_GROUP + j], out)
        return out

    cb = [pick(bv, j) for j in range(EXPERTS_PER_GROUP)]
    cs = [pick(sv, j) for j in range(EXPERTS_PER_GROUP)]
    i1 = jnp.zeros((1, tm), I32)
    m1 = cb[0]
    s1 = cs[0]
    for j in range(1, EXPERTS_PER_GROUP):
        gt = cb[j] > m1
        i1 = jnp.where(gt, j, i1)
        m1 = jnp.where(gt, cb[j], m1)
        s1 = jnp.where(gt, cs[j], s1)
    i2 = jnp.zeros((1, tm), I32)
    m2 = jnp.full((1, tm), -jnp.inf, F32)
    s2 = jnp.zeros((1, tm), F32)
    for j in range(EXPERTS_PER_GROUP):
        ok = jnp.logical_and(i1 != j, cb[j] > m2)
        i2 = jnp.where(ok, j, i2)
        m2 = jnp.where(ok, cb[j], m2)
        s2 = jnp.where(ok, cs[j], s2)
    e1 = gsel * EXPERTS_PER_GROUP + i1
    e2 = gsel * EXPERTS_PER_GROUP + i2
    tot = s1 + s2
    w_ref[...] = jnp.concatenate([s1 / tot, s2 / tot], axis=0)

    i = pl.program_id(0)
    eio = lax.broadcasted_iota(I32, (N_EXPERTS, tm), 0)
    oh1 = eio == e1
    oh2 = eio == e2
    ohs = jnp.logical_or(oh1, oh2).astype(F32)
    before = (lax.broadcasted_iota(I32, (tm, tm), 0) < lax.broadcasted_iota(I32, (tm, tm), 1))
    pre = _dot(ohs.astype(BF16), before.astype(BF16))
    tile_cnt = jnp.sum(ohs, axis=1, keepdims=True)
    offs = []
    acc = jnp.zeros((1, 1), F32)
    for e in range(N_EXPERTS):
        offs.append(acc)
        acc = acc + tile_cnt[e:e + 1, :]
    tile_off = jnp.concatenate(offs, axis=0)
    pos = tile_off + pre
    p1 = jnp.sum(jnp.where(oh1, pos, 0.0), axis=0, keepdims=True)
    p2 = jnp.sum(jnp.where(oh2, pos, 0.0), axis=0, keepdims=True)
    lpos_ref[...] = jnp.concatenate([p1, p2], axis=0).astype(I32) * ROW_CHUNKS

    @pl.when(i == 0)
    def _():
        for ref in (tcnt_ref, toff_ref, tbef_ref):
            ref[...] = jnp.zeros_like(ref)

    here = lax.broadcasted_iota(I32, (1, LANES), 1) == i
    tcnt_ref[...] = jnp.where(here, tile_cnt, tcnt_ref[...])
    toff_ref[...] = jnp.where(here, tile_off, toff_ref[...])
    tbef_ref[...] = jnp.where(here, cnt_ref[:, 0:1], tbef_ref[...])
    cnt_ref[...] += tile_cnt


def _router(xr, rw, rb, tm):
    n = xr.shape[0] // ROW_CHUNKS
    assert n // tm <= LANES
    table = pl.BlockSpec((N_EXPERTS, LANES), lambda i: (0, 0))
    return pl.pallas_call(
        functools.partial(_router_kernel, tm=tm),
        grid=(n // tm,),
        in_specs=[pl.BlockSpec((tm * ROW_CHUNKS, LANES), lambda i: (i, 0)),
                  pl.BlockSpec((D_MODEL, LANES), lambda i: (0, 0)),
                  pl.BlockSpec((D_MODEL, LANES), lambda i: (0, 0)),
                  pl.BlockSpec((N_EXPERTS, 1), lambda i: (0, 0))],
        out_specs=[pl.BlockSpec((2, tm), lambda i: (0, i)),
                   pl.BlockSpec((2, tm), lambda i: (0, i)),
                   table, table, table, table],
        out_shape=[jax.ShapeDtypeStruct((2, n), I32),
                   jax.ShapeDtypeStruct((2, n), F32)]
                  + [jax.ShapeDtypeStruct((N_EXPERTS, LANES), F32)] * 4,
        compiler_params=_cparams("arbitrary"),
        name="router",
    )(xr, rw[0], rw[1], rb)


def _plan_kernel(cnt_ref, tbef_ref, meta_ref, rstart_ref, *, nbl):
    shift = MOE_BM.bit_length() - 1
    cnt = cnt_ref[...].astype(I32)
    padded = ((cnt + (MOE_BM - 1)) >> shift) << shift
    starts = []
    acc = jnp.zeros((1, LANES), I32)
    for e in range(N_EXPERTS):
        starts.append(acc)
        acc = acc + padded[e:e + 1, :]
    pad_start = jnp.concatenate(starts, axis=0)
    pad_end = pad_start + padded
    rstart_ref[...] = pad_start + tbef_ref[...].astype(I32)
    blk0 = lax.broadcasted_iota(I32, (N_EXPERTS, nbl), 1) * MOE_BM
    block_e = jnp.sum((pad_end[:, 0:1] <= blk0).astype(I32), axis=0, keepdims=True)
    block_e = jnp.minimum(block_e, N_EXPERTS - 1)
    n_used = jnp.broadcast_to(acc[:, 0:1] >> shift, (1, nbl))
    diag = lax.broadcasted_iota(I32, (N_EXPERTS, nbl), 0) == lax.broadcasted_iota(I32, (N_EXPERTS, nbl), 1)
    fill_lo = jnp.sum(jnp.where(diag, (pad_start + cnt)[:, 0:1], 0), axis=0, keepdims=True)
    fill_hi = jnp.sum(jnp.where(diag, pad_end[:, 0:1], 0), axis=0, keepdims=True)
    meta_ref[...] = jnp.concatenate([block_e, n_used, fill_lo, fill_hi, jnp.zeros((SUBLANES - 4, nbl), I32)],
                                    axis=0)


def _plan(cnt, tbef, n_blocks):
    nbl = -(-n_blocks // LANES) * LANES
    table = pl.BlockSpec((N_EXPERTS, LANES), lambda i: (0, 0))
    return pl.pallas_call(
        functools.partial(_plan_kernel, nbl=nbl),
        grid=(1,),
        in_specs=[table, table],
        out_specs=[pl.BlockSpec((SUBLANES, nbl), lambda i: (0, 0)), table],
        out_shape=[jax.ShapeDtypeStruct((SUBLANES, nbl), I32),
                   jax.ShapeDtypeStruct((N_EXPERTS, LANES), I32)],
        compiler_params=_cparams("arbitrary"),
        name="moe_plan",
    )(cnt, tbef)


def _rows(ref, row, nrows):
    return ref.at[pl.ds(pl.multiple_of(row * ROW_CHUNKS, ROW_CHUNKS), nrows * ROW_CHUNKS), :]


def _rows_wait(src_hbm, buf, sem):
    pltpu.make_async_copy(src_hbm.at[pl.ds(0, buf.shape[0]), :], buf, sem).wait()


def _copy_pieces(src, src_row, dst, dst_row, count, max_rows, sem, wait=False):
    bit = max_rows.bit_length() - 1
    while bit >= 0:
        size = 1 << bit
        done = (count >> (bit + 1)) << (bit + 1)

        @pl.when(((count >> bit) & 1) == 1)
        def _():
            cp = pltpu.make_async_copy(_rows(src, src_row + done, size), _rows(dst, dst_row + done, size), sem)
            cp.start()
            if wait:
                cp.wait()

        bit -= 1


def _tile_runs(tcnt_ref, toff_ref, rstart_ref, tile, buf, hbm, sem, *, to_hbm, tm):
    def per_expert(e, carry):
        k = tile * N_EXPERTS + e
        if to_hbm:
            _copy_pieces(buf, toff_ref[k], hbm, rstart_ref[k], tcnt_ref[k], tm, sem)
        else:
            _copy_pieces(hbm, rstart_ref[k], buf, toff_ref[k], tcnt_ref[k], tm, sem)
        return carry

    lax.fori_loop(0, N_EXPERTS, per_expert, 0)


def _dispatch_kernel(lpos_ref, tcnt_ref, toff_ref, rstart_ref, flo_ref, fhi_ref, nu_ref, x_ref, xs_hbm,
                     s0, s1, zbuf, sem, zsem, *, n, tm, n_blocks):
    i = pl.program_id(0)
    nt = pl.num_programs(0)
    bufs = (s0, s1)
    unroll = 8

    for slot in range(2):
        @pl.when(i % 2 == slot)
        def _():
            buf = bufs[slot]

            @pl.when(i >= 2)
            def _():
                _rows_wait(xs_hbm, buf, sem.at[slot])

            def place(c, carry):
                tok = i * tm + c * unroll
                src = pl.multiple_of(c * (unroll * ROW_CHUNKS), unroll * ROW_CHUNKS)
                for u in range(unroll):
                    v = x_ref[pl.ds(src + u * ROW_CHUNKS, ROW_CHUNKS), :]
                    for k in range(2):
                        p = lpos_ref[k * n + tok + u]
                        buf[pl.ds(pl.multiple_of(p, ROW_CHUNKS), ROW_CHUNKS), :] = v
                return carry

            lax.fori_loop(0, tm // unroll, place, 0)
            _tile_runs(tcnt_ref, toff_ref, rstart_ref, i, buf, xs_hbm, sem.at[slot], to_hbm=True, tm=tm)

    @pl.when(i == nt - 1)
    def _():
        for slot in range(2):
            @pl.when(nt > slot)
            def _():
                _rows_wait(xs_hbm, bufs[slot], sem.at[slot])

        zbuf[...] = jnp.zeros_like(zbuf)
        for e in range(N_EXPERTS):
            _copy_pieces(zbuf, 0, xs_hbm, flo_ref[e], fhi_ref[e] - flo_ref[e], MOE_BM // 2, zsem, wait=True)

        def zero_block(j, carry):
            cp = pltpu.make_async_copy(zbuf, _rows(xs_hbm, j * MOE_BM, MOE_BM), zsem)
            cp.start()
            cp.wait()
            return carry

        lax.fori_loop(nu_ref[0], n_blocks, zero_block, 0)


def _dispatch(lpos_flat, tcnt, toff, rstart, fill_lo, fill_hi, n_used, xr, n_blocks, tm):
    n = xr.shape[0] // ROW_CHUNKS
    return pl.pallas_call(
        functools.partial(_dispatch_kernel, n=n, tm=tm, n_blocks=n_blocks),
        grid_spec=pltpu.PrefetchScalarGridSpec(
            num_scalar_prefetch=7,
            grid=(n // tm,),
            in_specs=[pl.BlockSpec((tm * ROW_CHUNKS, LANES), lambda i, *_: (i, 0))],
            out_specs=pl.BlockSpec(memory_space=pl.ANY),
            scratch_shapes=[pltpu.VMEM((2 * tm * ROW_CHUNKS, LANES), F32),
                            pltpu.VMEM((2 * tm * ROW_CHUNKS, LANES), F32),
                            pltpu.VMEM((MOE_BM * ROW_CHUNKS, LANES), F32),
                            pltpu.SemaphoreType.DMA((2,)),
                            pltpu.SemaphoreType.DMA(())]),
        out_shape=jax.ShapeDtypeStruct((n_blocks * MOE_BM * ROW_CHUNKS, LANES), F32),
        compiler_params=_cparams("arbitrary"),
        name="moe_dispatch",
    )(lpos_flat, tcnt, toff, rstart, fill_lo, fill_hi, n_used, xr)


def _experts_kernel(be_ref, nu_ref, xs_ref, wg_ref, wu_ref, wd_ref, y_ref, wgb, wub, wdb):
    j = pl.program_id(0)
    used = j < nu_ref[0]

    @pl.when(jnp.logical_and(used, jnp.logical_or(j == 0, be_ref[j] != be_ref[jnp.maximum(j - 1, 0)])))
    def _():
        wgb[...] = wg_ref[0, 0].astype(BF16)
        wub[...] = wu_ref[0, 0].astype(BF16)
        wdb[...] = wd_ref[0, 0].astype(BF16)

    @pl.when(used)
    def _():
        half = MOE_BM // 2
        refs = [(xs_ref.at[pl.ds(k * half * ROW_CHUNKS, half * ROW_CHUNKS), :],
                 y_ref.at[pl.ds(k * half * ROW_CHUNKS, half * ROW_CHUNKS), :]) for k in range(2)]
        xs = [_read_rows(xr, half).astype(BF16) for xr, _ in refs]
        hs = [(_silu(_dot(x, wgb[...])) * _dot(x, wub[...])).astype(BF16) for x in xs]
        for h, (_, yr) in zip(hs, refs):
            _write_rows(yr, _dot(h, wdb[...]), half)

    @pl.when(jnp.logical_not(used))
    def _():
        y_ref[...] = jnp.zeros_like(y_ref)


def _experts(block_e, n_used, xs, wg, wu, wd, layer):
    n_blocks = block_e.shape[0]

    def last_used(j, nu):
        return jnp.minimum(j, nu[0] - 1)

    def wblk(j, be, nu):
        return (layer, be[last_used(j, nu)], 0, 0)

    return pl.pallas_call(
        _experts_kernel,
        grid_spec=pltpu.PrefetchScalarGridSpec(
            num_scalar_prefetch=2,
            grid=(n_blocks,),
            in_specs=[pl.BlockSpec((MOE_BM * ROW_CHUNKS, LANES), lambda j, be, nu: (last_used(j, nu), 0)),
                      pl.BlockSpec((1, 1, D_MODEL, D_EXPERT), wblk),
                      pl.BlockSpec((1, 1, D_MODEL, D_EXPERT), wblk),
                      pl.BlockSpec((1, 1, D_EXPERT, D_MODEL), wblk)],
            out_specs=pl.BlockSpec((MOE_BM * ROW_CHUNKS, LANES), lambda j, be, nu: (j, 0)),
            scratch_shapes=[pltpu.VMEM((D_MODEL, D_EXPERT), BF16), pltpu.VMEM((D_MODEL, D_EXPERT), BF16),
                            pltpu.VMEM((D_EXPERT, D_MODEL), BF16)]),
        out_shape=jax.ShapeDtypeStruct(xs.shape, F32),
        compiler_params=_cparams("arbitrary"),
        name="moe_experts",
    )(block_e, n_used, xs, wg, wu, wd)


def _combine_ln_kernel(lpos_ref, tcnt_ref, toff_ref, rstart_ref, y_hbm, x_ref, w1_ref, w2_ref, g_ref, b_ref, o_ref,
                       r0, r1, u1, u2, sem, *, n, tm):
    i = pl.program_id(0)
    nt = pl.num_programs(0)
    bufs = (r0, r1)
    unroll = 8

    def fetch(tile, slot):
        _tile_runs(tcnt_ref, toff_ref, rstart_ref, tile, bufs[slot], y_hbm, sem.at[slot], to_hbm=False, tm=tm)

    @pl.when(i == 0)
    def _():
        fetch(0, 0)

    for slot in range(2):
        @pl.when(i % 2 == slot)
        def _():
            @pl.when(i + 1 < nt)
            def _():
                fetch(i + 1, 1 - slot)

            buf = bufs[slot]
            _rows_wait(y_hbm, buf, sem.at[slot])

            def place(c, carry):
                tok = i * tm + c * unroll
                dst0 = pl.multiple_of(c * (unroll * ROW_CHUNKS), unroll * ROW_CHUNKS)
                for u in range(unroll):
                    dst = pl.ds(dst0 + u * ROW_CHUNKS, ROW_CHUNKS)
                    for k, out in enumerate((u1, u2)):
                        p = lpos_ref[k * n + tok + u]
                        out[dst, :] = buf[pl.ds(pl.multiple_of(p, ROW_CHUNKS), ROW_CHUNKS), :]
                return carry

            lax.fori_loop(0, tm // unroll, place, 0)
            moe = w1_ref[...] * _read_rows(u1, tm) + w2_ref[...] * _read_rows(u2, tm)
            o_ref[...] = _ln(ALPHA * _read_rows(x_ref, tm) + moe, g_ref[...], b_ref[...])


def _combine_ln(lpos_flat, tcnt, toff, rstart, y, xr, w1, w2, g, b, tm):
    n = xr.shape[0] // ROW_CHUNKS
    return pl.pallas_call(
        functools.partial(_combine_ln_kernel, n=n, tm=tm),
        grid_spec=pltpu.PrefetchScalarGridSpec(
            num_scalar_prefetch=4,
            grid=(n // tm,),
            in_specs=[pl.BlockSpec(memory_space=pl.ANY),
                      pl.BlockSpec((tm * ROW_CHUNKS, LANES), lambda i, *_: (i, 0)),
                      pl.BlockSpec((tm, 1), lambda i, *_: (i, 0)),
                      pl.BlockSpec((tm, 1), lambda i, *_: (i, 0)),
                      pl.BlockSpec((1, D_MODEL), lambda i, *_: (0, 0)),
                      pl.BlockSpec((1, D_MODEL), lambda i, *_: (0, 0))],
            out_specs=pl.BlockSpec((tm, D_MODEL), lambda i, *_: (i, 0)),
            scratch_shapes=[pltpu.VMEM((2 * tm * ROW_CHUNKS, LANES), F32)] * 2
                           + [pltpu.VMEM((tm * ROW_CHUNKS, LANES), F32)] * 2
                           + [pltpu.SemaphoreType.DMA((2,))]),
        out_shape=jax.ShapeDtypeStruct((n, D_MODEL), F32),
        compiler_params=_cparams("arbitrary"),
        name="moe_combine_ln",
    )(lpos_flat, tcnt, toff, rstart, y, xr, w1, w2, g, b)


def _moe_ln(xr, rw, rb, wg, wu, wd, layer, g, b):
    n = xr.shape[0] // ROW_CHUNKS
    n_blocks = (2 * n) // MOE_BM + N_EXPERTS
    tm = MOE_TILE
    nt = n // tm
    lpos, w, cnt, tcnt, toff, tbef = _router(xr, rw, rb, tm)
    meta, rstart = _plan(cnt, tbef, n_blocks)
    block_e = meta[0, :n_blocks]
    n_used = meta[1, :1]

    def per_tile(table):
        return table[:, :nt].T.reshape(nt * N_EXPERTS).astype(I32)

    lpos_flat = lpos.reshape(2 * n)
    tcnt, toff, rstart = per_tile(tcnt), per_tile(toff), per_tile(rstart)
    xs = _dispatch(lpos_flat, tcnt, toff, rstart, meta[2, :N_EXPERTS], meta[3, :N_EXPERTS], n_used, xr,
                   n_blocks, tm)
    y = _experts(block_e, n_used, xs, wg, wu, wd, layer)
    return _combine_ln(lpos_flat, tcnt, toff, rstart, y, xr, w[0].reshape(n, 1), w[1].reshape(n, 1), g, b, tm)


def _conv_qkv_kernel(xm_ref, cw_ref, cb_ref, wq_ref, wk_ref, wv_ref, q_ref, k_ref, v_ref, xc_ref, *, s):
    xm_b = xm_ref[0]
    xm = xm_b.astype(F32)
    cw = cw_ref[...]
    row = lax.broadcasted_iota(I32, (s, 1), 0)
    half = CONV_K // 2
    acc = cb_ref[...] + xm * cw[half:half + 1, :]
    for sh in range(1, half + 1):
        past = jnp.where(row >= sh, pltpu.roll(xm, sh, axis=0), 0.0)
        acc = acc + past * cw[half - sh:half - sh + 1, :]
        nxt = jnp.where(row < s - sh, pltpu.roll(xm, s - sh, axis=0), 0.0)
        acc = acc + nxt * cw[half + sh:half + sh + 1, :]
    xc = _silu(acc).astype(BF16)
    xc_ref[0] = xc
    q_ref[0] = _dot(xc, wq_ref[0]).astype(BF16)
    k_ref[0] = (_dot_nt(wk_ref[0], xc) * (ML_DH ** -0.5)).astype(BF16)
    v = _dot(xm_b, wv_ref[0])
    ones_lane = lax.broadcasted_iota(I32, (1, ML_DHP), 1) == ML_DH
    v_ref[0] = jnp.where(ones_lane, 1.0, v).astype(BF16)


def _conv_qkv(main3, cw, cb, wq, wk_t, wv):
    b, s, _ = main3.shape
    tok = pl.BlockSpec((1, s, ML_DHP), lambda i, h: (i, 0, h))
    wspec = pl.BlockSpec((1, ML_DHP, ML_DHP), lambda i, h: (h, 0, 0))
    tok_shape = jax.ShapeDtypeStruct((b, s, ML_WP), BF16)
    return pl.pallas_call(
        functools.partial(_conv_qkv_kernel, s=s),
        grid=(b, ML_HEADS),
        in_specs=[tok,
                  pl.BlockSpec((CONV_K, ML_DHP), lambda i, h: (0, h)),
                  pl.BlockSpec((1, ML_DHP), lambda i, h: (0, h)),
                  wspec, wspec, wspec],
        out_specs=[tok, pl.BlockSpec((1, ML_DHP, s), lambda i, h: (i, h, 0)), tok, tok],
        out_shape=[tok_shape, jax.ShapeDtypeStruct((b, ML_WP, s), BF16), tok_shape, tok_shape],
        compiler_params=_cparams("parallel", "parallel"),
        name="conv_qkv",
    )(main3, cw, cb, wq, wk_t, wv)


def _mlstm_kernel(q_ref, kt_ref, v_ref, gc_ref, gr_ref, z_ref, xc_ref, ng_ref, sk_ref,
                  y_ref, hf_ref, hb_ref, cf_ref, cb_ref, m_ref, *, s):
    head0 = pl.program_id(1) * ML_HPS
    nc = s // CHUNK
    sub = lax.broadcasted_iota(I32, (LANES, 1), 0)
    gate = lax.broadcasted_iota(I32, (LANES, LANES), 0)
    ti = lax.broadcasted_iota(I32, (CHUNK, CHUNK), 0)
    tj = lax.broadcasted_iota(I32, (CHUNK, CHUNK), 1)

    for ref in (cf_ref, cb_ref, m_ref):
        ref[...] = jnp.zeros_like(ref)

    def gates_of(c):
        t0 = pl.multiple_of(c * CHUNK, CHUNK)
        return _split3(gc_ref[0, pl.ds(t0, CHUNK), :]), gr_ref[:, pl.ds(t0, CHUNK)]

    def intra(c, j, rev, gates):
        t0 = pl.multiple_of(c * CHUNK, CHUNK)
        hl = slice(j * ML_DHP, (j + 1) * ML_DHP)
        qb = q_ref[0, pl.ds(t0, CHUNK), hl]
        kt = kt_ref[0, hl, pl.ds(t0, CHUNK)]
        vb = v_ref[0, pl.ds(t0, CHUNK), hl]
        gc_parts, gr = gates
        i_idx = head0 + j + (2 * ML_HEADS if rev else 0)
        f_idx = i_idx + ML_HEADS
        allowed = (tj >= ti) if rev else (tj <= ti)
        sel = (gate == f_idx).astype(BF16)
        b_rep = sum(_dot(part, sel) for part in gc_parts)
        b_row = jnp.sum(jnp.where(sub == f_idx, gr, 0.0), axis=0, keepdims=True)
        i_row = jnp.sum(jnp.where(sub == i_idx, gr, 0.0), axis=0, keepdims=True)
        b_last = b_rep[0:1, :] if rev else b_rep[CHUNK - 1:CHUNK, :]

        d = jnp.where(allowed, b_rep - b_row + i_row, NEG)
        m_in = jnp.max(d, axis=1, keepdims=True)
        sc = _dot(qb, kt) * jnp.exp(d - m_in)
        nd_in = _dot(sc.astype(BF16), vb)
        w_row = b_last - b_row + i_row
        return t0, qb, kt, vb, b_rep, b_last[:, 0:1], m_in, nd_in, w_row

    def twice(a):
        return jnp.concatenate([a, a], axis=1)

    def update(parts, j, rev):
        t0, qb, kt, vb, b_rep, b_last, m_in, nd_in, w_row = parts
        h_ref, c_ref = (hb_ref, cb_ref) if rev else (hf_ref, cf_ref)
        hl = slice(j * ML_DHP, (j + 1) * ML_DHP)
        mrow = 2 * j + int(rev)
        m = m_ref[mrow:mrow + 1, 0:1]
        cmat = c_ref[j]
        inter = b_rep + m
        m_t = jnp.maximum(m_in, inter)
        a_in = jnp.exp(m_in - m_t)
        iexp = jnp.exp(inter - m_t)
        nd = twice(a_in) * nd_in + twice(iexp) * _dot(qb, cmat.astype(BF16))
        den = nd[:, ML_DH:ML_DH + 1]
        h_ref[pl.ds(t0, CHUNK), hl] = nd * (1.0 / jnp.maximum(jnp.abs(den), jnp.exp(-m_t[:, 0:1])))

        m_new = jnp.maximum(b_last + m, jnp.max(w_row, axis=1, keepdims=True))
        wexp = jnp.exp(w_row - m_new)
        cexp = jnp.exp(b_last + m - m_new)
        kw = (kt.astype(F32) * wexp).astype(BF16)
        c_ref[j] = cexp * cmat + _dot(kw, vb)
        m_ref[mrow:mrow + 1, :] = jnp.broadcast_to(m_new, (1, LANES))

    def step(i, carry):
        chains = [(j, rev) for j in range(ML_HPS) for rev in (False, True)]
        chunk = {False: i, True: nc - 1 - i}
        gates = {rev: gates_of(c) for rev, c in chunk.items()}
        parts = [intra(chunk[rev], j, rev, gates[rev]) for j, rev in chains]
        for p, (j, rev) in zip(parts, chains):
            update(p, j, rev)
        return carry

    lax.fori_loop(0, nc, step, 0)

    real = lax.broadcasted_iota(I32, (1, ML_DHP), 1) < ML_DH
    tb = 256

    def fin(c, carry):
        t0 = pl.multiple_of(c * tb, tb)
        for j in range(ML_HPS):
            hl = slice(j * ML_DHP, (j + 1) * ML_DHP)
            hs = jnp.where(real, hf_ref[pl.ds(t0, tb), hl] + hb_ref[pl.ds(t0, tb), hl], 0.0)
            mu = jnp.sum(hs, axis=1, keepdims=True) * (1.0 / ML_DH)
            dev = jnp.where(real, hs - mu, 0.0)
            var = jnp.sum(dev * dev, axis=1, keepdims=True) * (1.0 / ML_DH)
            hn = dev * lax.rsqrt(var + LN_EPS) * ng_ref[:, hl]
            xc = xc_ref[0, pl.ds(t0, tb), hl].astype(F32)
            z = z_ref[0, pl.ds(t0, tb), hl].astype(F32)
            y_ref[0, pl.ds(t0, tb), hl] = ((hn + sk_ref[:, hl] * xc) * _silu(z)).astype(BF16)
        return carry

    lax.fori_loop(0, s // tb, fin, 0)


def _mlstm(q, kt, v, gcol3, grow, main3, xc, ng, sk):
    b, s, _ = q.shape
    width = ML_HPS * ML_DHP
    steps = ML_HEADS // ML_HPS
    tok = pl.BlockSpec((1, s, width), lambda i, h: (i, 0, h))
    vec = pl.BlockSpec((1, width), lambda i, h: (0, h))
    return pl.pallas_call(
        functools.partial(_mlstm_kernel, s=s),
        grid=(b, steps),
        in_specs=[tok, pl.BlockSpec((1, width, s), lambda i, h: (i, h, 0)), tok,
                  pl.BlockSpec((1, s, LANES), lambda i, h: (i, 0, 0)),
                  pl.BlockSpec((LANES, s), lambda i, h: (0, i)),
                  pl.BlockSpec((1, s, width), lambda i, h: (i, 0, steps + h)),
                  tok, vec, vec],
        out_specs=tok,
        out_shape=jax.ShapeDtypeStruct((b, s, ML_WP), BF16),
        scratch_shapes=[pltpu.VMEM((s, width), F32), pltpu.VMEM((s, width), F32),
                        pltpu.VMEM((ML_HPS, ML_DHP, ML_DHP), F32), pltpu.VMEM((ML_HPS, ML_DHP, ML_DHP), F32),
                        pltpu.VMEM((SUBLANES, LANES), F32)],
        compiler_params=_cparams("parallel", "parallel"),
        name="mlstm",
    )(q, kt, v, gcol3, grow, main3, xc, ng, sk)


def _pad_heads(a, axis):
    a = jnp.moveaxis(a, axis, -1)
    lead = a.shape[:-1]
    a = a.reshape(lead + (ML_HEADS, ML_DH))
    a = jnp.pad(a, [(0, 0)] * len(lead) + [(0, 0), (0, ML_DHP - ML_DH)])
    return jnp.moveaxis(a.reshape(lead + (ML_WP,)), -1, axis)


def kernel(x, mem, mem_ln_g, mem_ln_b, w_mem_kv, router_w, router_b, na_w_in, na_rpb, ml_w_in, ml_conv_w,
           ml_conv_b, ml_w_qkv, ml_gate_b, ml_norm_g, ml_skip, w_out, ln_g, ln_b, exp_w_gate, exp_w_up,
           exp_w_down):
    b, s, d = x.shape
    n = b * s
    nm = mem.shape[1]
    row = lambda a: a.reshape(1, -1)

    mem_k, mem_v = _memkv(mem.reshape(b * nm, d), row(mem_ln_g), row(mem_ln_b), w_mem_kv.astype(BF16))
    mem_k3 = mem_k.reshape(b, nm, MEM_W)
    mem_v3 = mem_v.reshape(b, nm, MEM_W)
    rw_pad = jnp.pad(router_w, ((0, 0), (0, LANES - N_EXPERTS)))
    rw_hi = rw_pad.astype(BF16)
    rw = (rw_hi, (rw_pad - rw_hi.astype(F32)).astype(BF16))
    rb = router_b.reshape(N_EXPERTS, 1)

    x2 = x.reshape(n, d)

    h0 = _proj(x2, na_w_in[0].astype(BF16)).reshape(b, s, 3 * NA_W + MEM_W)
    y_na = _na_attention(h0, _na_bias_table(na_rpb[0]))
    wo = w_out[0].astype(BF16)
    xr = _outproj_ln(y_na.reshape(n, NA_W), h0.reshape(n, 3 * NA_W + MEM_W), 3 * NA_W // MEM_W, mem_k3, mem_v3,
                     wo[:NA_W], wo[NA_W:], x2, row(ln_g[0, 0]), row(ln_b[0, 0]))
    x2 = _moe_ln(xr, rw, rb, exp_w_gate, exp_w_up, exp_w_down, 0, row(ln_g[0, 1]), row(ln_b[0, 1]))

    w1 = ml_w_in[0]
    w_main = jnp.concatenate([_pad_heads(w1[:, :ML_W], 1), _pad_heads(w1[:, ML_W:2 * ML_W], 1),
                              w1[:, 2 * ML_W + 4 * ML_HEADS:]], axis=1).astype(BF16)
    w_g = jnp.pad(w1[:, 2 * ML_W:2 * ML_W + 4 * ML_HEADS], ((0, 0), (0, LANES - 4 * ML_HEADS))).astype(BF16)
    gb = jnp.pad(ml_gate_b[0].reshape(4 * ML_HEADS), (0, LANES - 4 * ML_HEADS))
    main, acol, arow = _proj_gates(x2, w_main, w_g, w_g.T, gb.reshape(1, LANES), gb.reshape(LANES, 1))
    main3 = main.reshape(b, s, 2 * ML_WP + MEM_W)
    wqkv = jnp.pad(ml_w_qkv[0], ((0, 0), (0, 0), (0, ML_DHP - ML_DH), (0, ML_DHP - ML_DH))).astype(BF16)
    q, k, v, xc = _conv_qkv(main3, _pad_heads(ml_conv_w[0], 1), _pad_heads(row(ml_conv_b[0]), 1),
                            wqkv[0], jnp.swapaxes(wqkv[1], 1, 2), wqkv[2])
    y_ml = _mlstm(q, k, v, acol.reshape(b, s, LANES), arow, main3, xc,
                  _pad_heads(row(ml_norm_g[0]), 1), _pad_heads(row(ml_skip[0]), 1))
    wo = w_out[1]
    xr = _outproj_ln(y_ml.reshape(n, ML_WP), main, 2 * ML_WP // MEM_W, mem_k3, mem_v3,
                     _pad_heads(wo[:ML_W], 0).astype(BF16), wo[ML_W:].astype(BF16), x2,
                     row(ln_g[1, 0]), row(ln_b[1, 0]))
    x2 = _moe_ln(xr, rw, rb, exp_w_gate, exp_w_up, exp_w_down, 1, row(ln_g[1, 1]), row(ln_b[1, 1]))
    return x2.reshape(b, s, d)
```

```python
import functools

import numpy as np
import jax
import jax.numpy as jnp
from jax import lax
from jax.experimental import pallas as pl
from jax.experimental.pallas import tpu as pltpu

F32 = jnp.float32
BF16 = jnp.bfloat16
I32 = jnp.int32

D_MODEL = 1024
DEPTH = 2
GRID_W = 64
MEM_HEADS = 4
MEM_DH = 64
MEM_W = MEM_HEADS * MEM_DH
NA_HEADS = 12
NA_DH = 64
NA_W = NA_HEADS * NA_DH
WIN_H = 8
WIN_W = 16
ML_HEADS = 4
ML_DH = 192
ML_DHP = 256
ML_W = ML_HEADS * ML_DH
ML_WP = ML_HEADS * ML_DHP
CONV_K = 5
CHUNK = 128
N_EXPERTS = 16
N_GROUPS = 4
EXPERTS_PER_GROUP = N_EXPERTS // N_GROUPS
D_EXPERT = 512
ALPHA = (2 * DEPTH) ** 0.25
LN_EPS = 1e-5
NEG = -1e30

LANES = 128
SUBLANES = 8
ROW_CHUNKS = D_MODEL // LANES
MOE_BM = 512
MOE_TILE = 512
ML_HPS = 2
NA_ROWS_PER_STEP = 8
VMEM_LIMIT = 48 * 1024 * 1024


def _cparams(*sem):
    return pltpu.CompilerParams(dimension_semantics=sem, vmem_limit_bytes=VMEM_LIMIT)


def _dot(a, b):
    return jnp.dot(a, b, preferred_element_type=F32)


def _dot_nt(a, b, precision=None):
    return lax.dot_general(a, b, (((1,), (1,)), ((), ())), precision=precision,
                           preferred_element_type=F32)


def _ln(z, g, b):
    mu = jnp.mean(z, axis=-1, keepdims=True)
    zc = z - mu
    var = jnp.mean(zc * zc, axis=-1, keepdims=True)
    return zc * lax.rsqrt(var + LN_EPS) * g + b


def _silu(x):
    return x * jax.nn.sigmoid(x)


def _read_rows(ref, n):
    return jnp.concatenate([ref[pl.ds(j, n, stride=ROW_CHUNKS), :] for j in range(ROW_CHUNKS)], axis=1)


def _write_rows(ref, val, n):
    for j in range(ROW_CHUNKS):
        ref[pl.ds(j, n, stride=ROW_CHUNKS), :] = val[:, j * LANES:(j + 1) * LANES]


def _memkv_kernel(m_ref, g_ref, b_ref, w_ref, k_ref, v_ref):
    z = _ln(m_ref[...], g_ref[...], b_ref[...])
    kv = _dot(z.astype(BF16), w_ref[...])
    k_ref[...] = kv[:, :MEM_W].astype(BF16)
    v_ref[...] = kv[:, MEM_W:].astype(BF16)


def _memkv(mem2, g, b, w):
    n = mem2.shape[0]
    tm = 256
    return pl.pallas_call(
        _memkv_kernel,
        grid=(n // tm,),
        in_specs=[pl.BlockSpec((tm, D_MODEL), lambda i: (i, 0)),
                  pl.BlockSpec((1, D_MODEL), lambda i: (0, 0)),
                  pl.BlockSpec((1, D_MODEL), lambda i: (0, 0)),
                  pl.BlockSpec((D_MODEL, 2 * MEM_W), lambda i: (0, 0))],
        out_specs=[pl.BlockSpec((tm, MEM_W), lambda i: (i, 0)),
                   pl.BlockSpec((tm, MEM_W), lambda i: (i, 0))],
        out_shape=[jax.ShapeDtypeStruct((n, MEM_W), BF16)] * 2,
        compiler_params=_cparams("parallel"),
        name="memkv",
    )(mem2, g, b, w)


def _proj_kernel(x_ref, w_ref, o_ref):
    o_ref[...] = _dot(x_ref[...].astype(BF16), w_ref[...]).astype(o_ref.dtype)


def _proj(x2, w, tm=512):
    n, k = x2.shape
    nout = w.shape[1]
    return pl.pallas_call(
        _proj_kernel,
        grid=(n // tm,),
        in_specs=[pl.BlockSpec((tm, k), lambda i: (i, 0)),
                  pl.BlockSpec((k, nout), lambda i: (0, 0))],
        out_specs=pl.BlockSpec((tm, nout), lambda i: (i, 0)),
        out_shape=jax.ShapeDtypeStruct((n, nout), BF16),
        compiler_params=_cparams("parallel"),
        name="in_proj",
    )(x2, w)


def _split3(x):
    hi = x.astype(BF16)
    r1 = x - hi.astype(F32)
    mid = r1.astype(BF16)
    lo = (r1 - mid.astype(F32)).astype(BF16)
    return hi, mid, lo


def _proj_gates_kernel(x_ref, w_ref, wg_ref, wgt_ref, gbc_ref, gbr_ref, o_ref, g_ref, gt_ref, *, tm):
    xb = x_ref[...].astype(BF16)
    o_ref[...] = _dot(xb, w_ref[...]).astype(BF16)
    gcol = _dot(xb, wg_ref[...]) + gbc_ref[...]
    grow = _dot_nt(wgt_ref[...], xb) + gbr_ref[...]
    lane = lax.broadcasted_iota(I32, (1, LANES), 1)
    sub = lax.broadcasted_iota(I32, (LANES, 1), 0)
    ti = lax.broadcasted_iota(I32, (CHUNK, CHUNK), 0)
    tj = lax.broadcasted_iota(I32, (CHUNK, CHUNK), 1)
    lower = (tj <= ti).astype(BF16)
    upper = (ti <= tj).astype(BF16)

    def pick(idx, pre, suf, raw):
        fwd = jnp.logical_and(idx >= ML_HEADS, idx < 2 * ML_HEADS)
        bwd = jnp.logical_and(idx >= 3 * ML_HEADS, idx < 4 * ML_HEADS)
        return jnp.where(fwd, pre, jnp.where(bwd, suf, raw))

    for c in range(tm // CHUNK):
        tc = slice(c * CHUNK, (c + 1) * CHUNK)
        g = gcol[tc, :]
        ls = jax.nn.log_sigmoid(g)
        pre = sum(_dot(lower, part) for part in _split3(ls))
        suf = jnp.sum(ls, axis=0, keepdims=True) - pre + ls
        g_ref[tc, :] = pick(lane, pre, suf, g)
        g = grow[:, tc]
        ls = jax.nn.log_sigmoid(g)
        pre = sum(_dot(part, upper) for part in _split3(ls))
        suf = jnp.sum(ls, axis=1, keepdims=True) - pre + ls
        gt_ref[:, tc] = pick(sub, pre, suf, g)


def _proj_gates(x2, w, wg, wgt, gbc, gbr, tm=512):
    n, k = x2.shape
    nout = w.shape[1]
    return pl.pallas_call(
        functools.partial(_proj_gates_kernel, tm=tm),
        grid=(n // tm,),
        in_specs=[pl.BlockSpec((tm, k), lambda i: (i, 0)),
                  pl.BlockSpec((k, nout), lambda i: (0, 0)),
                  pl.BlockSpec((k, LANES), lambda i: (0, 0)),
                  pl.BlockSpec((LANES, k), lambda i: (0, 0)),
                  pl.BlockSpec((1, LANES), lambda i: (0, 0)),
                  pl.BlockSpec((LANES, 1), lambda i: (0, 0))],
        out_specs=[pl.BlockSpec((tm, nout), lambda i: (i, 0)),
                   pl.BlockSpec((tm, LANES), lambda i: (i, 0)),
                   pl.BlockSpec((LANES, tm), lambda i: (0, i))],
        out_shape=[jax.ShapeDtypeStruct((n, nout), BF16),
                   jax.ShapeDtypeStruct((n, LANES), F32),
                   jax.ShapeDtypeStruct((LANES, n), F32)],
        compiler_params=_cparams("parallel"),
        name="in_proj_gates",
    )(x2, w, wg, wgt, gbc, gbr)


def _na_kernel(q_ref, k_ref, v_ref, tbl_ref, o_ref, *, rows):
    lane = lax.broadcasted_iota(I32, (1, LANES), 1)
    first = lane < NA_DH
    nkeys = WIN_H * GRID_W

    def rows_step(i, carry):
        rr = [i * NA_ROWS_PER_STEP + u for u in range(NA_ROWS_PER_STEP)]
        rss = [jnp.clip(r - WIN_H // 2, 0, rows - WIN_H) for r in rr]
        scores = []
        for r, rs in zip(rr, rss):
            q = q_ref[0, pl.ds(pl.multiple_of(r * GRID_W, GRID_W), GRID_W), :]
            q = q * jnp.asarray(NA_DH ** -0.5, BF16)
            q2 = jnp.concatenate([jnp.where(first, q, jnp.zeros_like(q)),
                                  jnp.where(first, jnp.zeros_like(q), q)], axis=0)
            k = k_ref[0, pl.ds(pl.multiple_of(rs * GRID_W, GRID_W), nkeys), :]
            dr0 = rs - r + WIN_H - 1
            bias = jnp.concatenate(
                [jnp.concatenate([tbl_ref[0, half, dr0 + 2 * m] for m in range(WIN_H // 2)], axis=1)
                 for half in range(2)], axis=0)
            scores.append(_dot_nt(q2, k) + bias)
        probs = []
        for s in scores:
            p = jnp.exp(s - jnp.max(s, axis=-1, keepdims=True))
            probs.append((p.astype(BF16), jnp.sum(p, axis=-1, keepdims=True)))
        for r, rs, (p, l) in zip(rr, rss, probs):
            v = v_ref[0, pl.ds(pl.multiple_of(rs * GRID_W, GRID_W), nkeys), :]
            o = _dot(p, v) / l
            o = jnp.where(first, o[:GRID_W], o[GRID_W:])
            o_ref[0, pl.ds(pl.multiple_of(r * GRID_W, GRID_W), GRID_W), :] = o.astype(o_ref.dtype)
        return carry

    lax.fori_loop(0, rows // NA_ROWS_PER_STEP, rows_step, 0)


def _na_bias_table(rpb):
    qc = np.arange(GRID_W)[:, None]
    kc = np.arange(GRID_W)[None, :]
    cs = np.clip(qc - WIN_W // 2, 0, GRID_W - WIN_W)
    col_in = (kc >= cs) & (kc < cs + WIN_W)
    side = GRID_W - WIN_W
    wide = jnp.pad(rpb, ((0, 0), (0, 0), (side, side)))
    t = jnp.stack([wide[:, :, GRID_W - 1 - q:2 * GRID_W - 1 - q] for q in range(GRID_W)], axis=2)
    t = jnp.where(col_in, t, NEG).astype(F32)
    t2 = jnp.concatenate([t[:, :-1], t[:, 1:]], axis=-1)
    return t2.reshape(NA_HEADS // 2, 2, 2 * WIN_H - 2, GRID_W, 2 * GRID_W)


def _na_attention(h3, tbl):
    b, s, _ = h3.shape
    rows = s // GRID_W
    npair = NA_HEADS // 2
    return pl.pallas_call(
        functools.partial(_na_kernel, rows=rows),
        grid=(b, npair),
        in_specs=[pl.BlockSpec((1, s, LANES), lambda i, p: (i, 0, p)),
                  pl.BlockSpec((1, s, LANES), lambda i, p: (i, 0, npair + p)),
                  pl.BlockSpec((1, s, LANES), lambda i, p: (i, 0, 2 * npair + p)),
                  pl.BlockSpec((1, 2, 2 * WIN_H - 2, GRID_W, 2 * GRID_W), lambda i, p: (p, 0, 0, 0, 0))],
        out_specs=pl.BlockSpec((1, s, LANES), lambda i, p: (i, 0, p)),
        out_shape=jax.ShapeDtypeStruct((b, s, NA_W), BF16),
        compiler_params=_cparams("parallel", "parallel"),
        name="na_attention",
    )(h3, h3, h3, tbl)


def _outproj_ln_kernel(ya_ref, qm_ref, mk_ref, mv_ref, wa_ref, wm_ref, x_ref, g_ref, b_ref, or_ref, *, tm):
    lane = lax.broadcasted_iota(I32, (1, LANES), 1)
    first = lane < MEM_DH
    q = qm_ref[...] * jnp.asarray(MEM_DH ** -0.5, BF16)
    cols = [slice(p * LANES, (p + 1) * LANES) for p in range(MEM_HEADS // 2)]
    scores = []
    for c in cols:
        qp = q[:, c]
        q2 = jnp.concatenate([jnp.where(first, qp, jnp.zeros_like(qp)),
                              jnp.where(first, jnp.zeros_like(qp), qp)], axis=0)
        scores.append(_dot_nt(q2, mk_ref[0, :, c]))
    probs = []
    for s in scores:
        p = jnp.exp(s - jnp.max(s, axis=-1, keepdims=True))
        probs.append((p.astype(BF16), jnp.sum(p, axis=-1, keepdims=True)))
    outs = []
    for c, (p, l) in zip(cols, probs):
        o = _dot(p, mv_ref[0, :, c]) / l
        outs.append(jnp.where(first, o[:tm], o[tm:]))
    ym = jnp.concatenate(outs, axis=1).astype(BF16)
    acc = _dot(ya_ref[...], wa_ref[...]) + _dot(ym, wm_ref[...])
    _write_rows(or_ref, _ln(ALPHA * x_ref[...] + acc, g_ref[...], b_ref[...]), tm)


def _outproj_ln(ya, h2, qm_block, mem_k3, mem_v3, wa, wm, x2, g, b, tm=512):
    n = x2.shape[0]
    ka = ya.shape[1]
    nb, nm, _ = mem_k3.shape
    per_batch = n // nb // tm
    full = lambda shape: pl.BlockSpec(shape, lambda i: (0,) * len(shape))
    return pl.pallas_call(
        functools.partial(_outproj_ln_kernel, tm=tm),
        grid=(n // tm,),
        in_specs=[pl.BlockSpec((tm, ka), lambda i: (i, 0)),
                  pl.BlockSpec((tm, MEM_W), lambda i: (i, qm_block)),
                  pl.BlockSpec((1, nm, MEM_W), lambda i: (i // per_batch, 0, 0)),
                  pl.BlockSpec((1, nm, MEM_W), lambda i: (i // per_batch, 0, 0)),
                  full((ka, D_MODEL)), full((MEM_W, D_MODEL)),
                  pl.BlockSpec((tm, D_MODEL), lambda i: (i, 0)),
                  full((1, D_MODEL)), full((1, D_MODEL))],
        out_specs=pl.BlockSpec((tm * ROW_CHUNKS, LANES), lambda i: (i, 0)),
        out_shape=jax.ShapeDtypeStruct((n * ROW_CHUNKS, LANES), F32),
        compiler_params=_cparams("parallel"),
        name="outproj_ln",
    )(ya, h2, mem_k3, mem_v3, wa, wm, x2, g, b)


def _router_kernel(x_ref, rwh_ref, rwl_ref, rb_ref, lpos_ref, w_ref, cnt_ref, tcnt_ref, toff_ref, tbef_ref, *, tm):
    @pl.when(pl.program_id(0) == 0)
    def _():
        cnt_ref[...] = jnp.zeros_like(cnt_ref)

    x = _read_rows(x_ref, tm)
    xh = x.astype(BF16)
    xl = (x - xh.astype(F32)).astype(BF16)
    logits_t = _dot(xh, rwh_ref[...]) + (_dot(xh, rwl_ref[...]) + _dot(xl, rwh_ref[...]))
    logits = logits_t.T[:N_EXPERTS]
    scores = jax.nn.sigmoid(logits)
    biased = scores + rb_ref[...]
    bv = [biased[e:e + 1, :] for e in range(N_EXPERTS)]
    sv = [scores[e:e + 1, :] for e in range(N_EXPERTS)]

    grp = []
    for g in range(N_GROUPS):
        m = bv[g * EXPERTS_PER_GROUP:(g + 1) * EXPERTS_PER_GROUP]
        best = None
        for a in range(EXPERTS_PER_GROUP):
            for c in range(a + 1, EXPERTS_PER_GROUP):
                pair = m[a] + m[c]
                best = pair if best is None else jnp.maximum(best, pair)
        grp.append(best)
    gsel = jnp.zeros((1, tm), I32)
    gbest = grp[0]
    for g in range(1, N_GROUPS):
        better = grp[g] > gbest
        gsel = jnp.where(better, g, gsel)
        gbest = jnp.where(better, grp[g], gbest)

    def pick(vals, j):
        out = vals[j]
        for g in range(1, N_GROUPS):
            out = jnp.where(gsel == g, vals[g * EXPERTS_PER_GROUP + j], out)
        return out

    cb = [pick(bv, j) for j in range(EXPERTS_PER_GROUP)]
    cs = [pick(sv, j) for j in range(EXPERTS_PER_GROUP)]
    i1 = jnp.zeros((1, tm), I32)
    m1 = cb[0]
    s1 = cs[0]
    for j in range(1, EXPERTS_PER_GROUP):
        gt = cb[j] > m1
        i1 = jnp.where(gt, j, i1)
        m1 = jnp.where(gt, cb[j], m1)
        s1 = jnp.where(gt, cs[j], s1)
    i2 = jnp.zeros((1, tm), I32)
    m2 = jnp.full((1, tm), -jnp.inf, F32)
    s2 = jnp.zeros((1, tm), F32)
    for j in range(EXPERTS_PER_GROUP):
        ok = jnp.logical_and(i1 != j, cb[j] > m2)
        i2 = jnp.where(ok, j, i2)
        m2 = jnp.where(ok, cb[j], m2)
        s2 = jnp.where(ok, cs[j], s2)
    e1 = gsel * EXPERTS_PER_GROUP + i1
    e2 = gsel * EXPERTS_PER_GROUP + i2
    tot = s1 + s2
    w_ref[...] = jnp.concatenate([s1 / tot, s2 / tot], axis=0)

    i = pl.program_id(0)
    eio = lax.broadcasted_iota(I32, (N_EXPERTS, tm), 0)
    oh1 = eio == e1
    oh2 = eio == e2
    ohs = jnp.logical_or(oh1, oh2).astype(F32)
    before = (lax.broadcasted_iota(I32, (tm, tm), 0) < lax.broadcasted_iota(I32, (tm, tm), 1))
    pre = _dot(ohs.astype(BF16), before.astype(BF16))
    tile_cnt = jnp.sum(ohs, axis=1, keepdims=True)
    offs = []
    acc = jnp.zeros((1, 1), F32)
    for e in range(N_EXPERTS):
        offs.append(acc)
        acc = acc + tile_cnt[e:e + 1, :]
    tile_off = jnp.concatenate(offs, axis=0)
    pos = tile_off + pre
    p1 = jnp.sum(jnp.where(oh1, pos, 0.0), axis=0, keepdims=True)
    p2 = jnp.sum(jnp.where(oh2, pos, 0.0), axis=0, keepdims=True)
    lpos_ref[...] = jnp.concatenate([p1, p2], axis=0).astype(I32) * ROW_CHUNKS

    @pl.when(i == 0)
    def _():
        for ref in (tcnt_ref, toff_ref, tbef_ref):
            ref[...] = jnp.zeros_like(ref)

    here = lax.broadcasted_iota(I32, (1, LANES), 1) == i
    tcnt_ref[...] = jnp.where(here, tile_cnt, tcnt_ref[...])
    toff_ref[...] = jnp.where(here, tile_off, toff_ref[...])
    tbef_ref[...] = jnp.where(here, cnt_ref[:, 0:1], tbef_ref[...])
    cnt_ref[...] += tile_cnt


def _router(xr, rw, rb, tm):
    n = xr.shape[0] // ROW_CHUNKS
    assert n // tm <= LANES
    table = pl.BlockSpec((N_EXPERTS, LANES), lambda i: (0, 0))
    return pl.pallas_call(
        functools.partial(_router_kernel, tm=tm),
        grid=(n // tm,),
        in_specs=[pl.BlockSpec((tm * ROW_CHUNKS, LANES), lambda i: (i, 0)),
                  pl.BlockSpec((D_MODEL, LANES), lambda i: (0, 0)),
                  pl.BlockSpec((D_MODEL, LANES), lambda i: (0, 0)),
                  pl.BlockSpec((N_EXPERTS, 1), lambda i: (0, 0))],
        out_specs=[pl.BlockSpec((2, tm), lambda i: (0, i)),
                   pl.BlockSpec((2, tm), lambda i: (0, i)),
                   table, table, table, table],
        out_shape=[jax.ShapeDtypeStruct((2, n), I32),
                   jax.ShapeDtypeStruct((2, n), F32)]
                  + [jax.ShapeDtypeStruct((N_EXPERTS, LANES), F32)] * 4,
        compiler_params=_cparams("arbitrary"),
        name="router",
    )(xr, rw[0], rw[1], rb)


def _plan_kernel(cnt_ref, tbef_ref, meta_ref, rstart_ref, *, nbl):
    shift = MOE_BM.bit_length() - 1
    cnt = cnt_ref[...].astype(I32)
    padded = ((cnt + (MOE_BM - 1)) >> shift) << shift
    starts = []
    acc = jnp.zeros((1, LANES), I32)
    for e in range(N_EXPERTS):
        starts.append(acc)
        acc = acc + padded[e:e + 1, :]
    pad_start = jnp.concatenate(starts, axis=0)
    pad_end = pad_start + padded
    rstart_ref[...] = pad_start + tbef_ref[...].astype(I32)
    blk0 = lax.broadcasted_iota(I32, (N_EXPERTS, nbl), 1) * MOE_BM
    block_e = jnp.sum((pad_end[:, 0:1] <= blk0).astype(I32), axis=0, keepdims=True)
    block_e = jnp.minimum(block_e, N_EXPERTS - 1)
    n_used = jnp.broadcast_to(acc[:, 0:1] >> shift, (1, nbl))
    diag = lax.broadcasted_iota(I32, (N_EXPERTS, nbl), 0) == lax.broadcasted_iota(I32, (N_EXPERTS, nbl), 1)
    fill_lo = jnp.sum(jnp.where(diag, (pad_start + cnt)[:, 0:1], 0), axis=0, keepdims=True)
    fill_hi = jnp.sum(jnp.where(diag, pad_end[:, 0:1], 0), axis=0, keepdims=True)
    meta_ref[...] = jnp.concatenate([block_e, n_used, fill_lo, fill_hi, jnp.zeros((SUBLANES - 4, nbl), I32)],
                                    axis=0)


def _plan(cnt, tbef, n_blocks):
    nbl = -(-n_blocks // LANES) * LANES
    table = pl.BlockSpec((N_EXPERTS, LANES), lambda i: (0, 0))
    return pl.pallas_call(
        functools.partial(_plan_kernel, nbl=nbl),
        grid=(1,),
        in_specs=[table, table],
        out_specs=[pl.BlockSpec((SUBLANES, nbl), lambda i: (0, 0)), table],
        out_shape=[jax.ShapeDtypeStruct((SUBLANES, nbl), I32),
                   jax.ShapeDtypeStruct((N_EXPERTS, LANES), I32)],
        compiler_params=_cparams("arbitrary"),
        name="moe_plan",
    )(cnt, tbef)


def _rows(ref, row, nrows):
    return ref.at[pl.ds(pl.multiple_of(row * ROW_CHUNKS, ROW_CHUNKS), nrows * ROW_CHUNKS), :]


def _rows_wait(src_hbm, buf, sem):
    pltpu.make_async_copy(src_hbm.at[pl.ds(0, buf.shape[0]), :], buf, sem).wait()


def _copy_pieces(src, src_row, dst, dst_row, count, max_rows, sem, wait=False):
    bit = max_rows.bit_length() - 1
    while bit >= 0:
        size = 1 << bit
        done = (count >> (bit + 1)) << (bit + 1)

        @pl.when(((count >> bit) & 1) == 1)
        def _():
            cp = pltpu.make_async_copy(_rows(src, src_row + done, size), _rows(dst, dst_row + done, size), sem)
            cp.start(priority=bit % 2)
            if wait:
                cp.wait()

        bit -= 1


def _tile_runs(tcnt_ref, toff_ref, rstart_ref, tile, buf, hbm, sem, *, to_hbm, tm):
    def per_expert(e, carry):
        k = tile * N_EXPERTS + e
        if to_hbm:
            _copy_pieces(buf, toff_ref[k], hbm, rstart_ref[k], tcnt_ref[k], tm, sem)
        else:
            _copy_pieces(hbm, rstart_ref[k], buf, toff_ref[k], tcnt_ref[k], tm, sem)
        return carry

    lax.fori_loop(0, N_EXPERTS, per_expert, 0)


def _dispatch_kernel(lpos_ref, tcnt_ref, toff_ref, rstart_ref, flo_ref, fhi_ref, nu_ref, x_ref, xs_hbm,
                     s0, s1, zbuf, sem, zsem, *, n, tm, n_blocks):
    i = pl.program_id(0)
    nt = pl.num_programs(0)
    bufs = (s0, s1)
    unroll = 8

    for slot in range(2):
        @pl.when(i % 2 == slot)
        def _():
            buf = bufs[slot]

            @pl.when(i >= 2)
            def _():
                _rows_wait(xs_hbm, buf, sem.at[slot])

            def place(c, carry):
                tok = i * tm + c * unroll
                src = pl.multiple_of(c * (unroll * ROW_CHUNKS), unroll * ROW_CHUNKS)
                for u in range(unroll):
                    v = x_ref[pl.ds(src + u * ROW_CHUNKS, ROW_CHUNKS), :]
                    for k in range(2):
                        p = lpos_ref[k * n + tok + u]
                        buf[pl.ds(pl.multiple_of(p, ROW_CHUNKS), ROW_CHUNKS), :] = v
                return carry

            lax.fori_loop(0, tm // unroll, place, 0)
            _tile_runs(tcnt_ref, toff_ref, rstart_ref, i, buf, xs_hbm, sem.at[slot], to_hbm=True, tm=tm)

    @pl.when(i == nt - 1)
    def _():
        for slot in range(2):
            @pl.when(nt > slot)
            def _():
                _rows_wait(xs_hbm, bufs[slot], sem.at[slot])

        zbuf[...] = jnp.zeros_like(zbuf)
        for e in range(N_EXPERTS):
            _copy_pieces(zbuf, 0, xs_hbm, flo_ref[e], fhi_ref[e] - flo_ref[e], MOE_BM // 2, zsem, wait=True)

        def zero_block(j, carry):
            cp = pltpu.make_async_copy(zbuf, _rows(xs_hbm, j * MOE_BM, MOE_BM), zsem)
            cp.start()
            cp.wait()
            return carry

        lax.fori_loop(nu_ref[0], n_blocks, zero_block, 0)


def _dispatch(lpos_flat, tcnt, toff, rstart, fill_lo, fill_hi, n_used, xr, n_blocks, tm):
    n = xr.shape[0] // ROW_CHUNKS
    return pl.pallas_call(
        functools.partial(_dispatch_kernel, n=n, tm=tm, n_blocks=n_blocks),
        grid_spec=pltpu.PrefetchScalarGridSpec(
            num_scalar_prefetch=7,
            grid=(n // tm,),
            in_specs=[pl.BlockSpec((tm * ROW_CHUNKS, LANES), lambda i, *_: (i, 0))],
            out_specs=pl.BlockSpec(memory_space=pl.ANY),
            scratch_shapes=[pltpu.VMEM((2 * tm * ROW_CHUNKS, LANES), F32),
                            pltpu.VMEM((2 * tm * ROW_CHUNKS, LANES), F32),
                            pltpu.VMEM((MOE_BM * ROW_CHUNKS, LANES), F32),
                            pltpu.SemaphoreType.DMA((2,)),
                            pltpu.SemaphoreType.DMA(())]),
        out_shape=jax.ShapeDtypeStruct((n_blocks * MOE_BM * ROW_CHUNKS, LANES), F32),
        compiler_params=_cparams("arbitrary"),
        name="moe_dispatch",
    )(lpos_flat, tcnt, toff, rstart, fill_lo, fill_hi, n_used, xr)


def _experts_kernel(be_ref, nu_ref, xs_ref, wg_ref, wu_ref, wd_ref, y_ref, wgb, wub, wdb):
    j = pl.program_id(0)
    used = j < nu_ref[0]

    @pl.when(jnp.logical_and(used, jnp.logical_or(j == 0, be_ref[j] != be_ref[jnp.maximum(j - 1, 0)])))
    def _():
        wgb[...] = wg_ref[0, 0].astype(BF16)
        wub[...] = wu_ref[0, 0].astype(BF16)
        wdb[...] = wd_ref[0, 0].astype(BF16)

    @pl.when(used)
    def _():
        x = _read_rows(xs_ref, MOE_BM).astype(BF16)
        h = _silu(_dot(x, wgb[...])) * _dot(x, wub[...])
        _write_rows(y_ref, _dot(h.astype(BF16), wdb[...]), MOE_BM)

    @pl.when(jnp.logical_not(used))
    def _():
        y_ref[...] = jnp.zeros_like(y_ref)


def _experts(block_e, n_used, xs, wg, wu, wd, layer):
    n_blocks = block_e.shape[0]

    def last_used(j, nu):
        return jnp.minimum(j, nu[0] - 1)

    def wblk(j, be, nu):
        return (layer, be[last_used(j, nu)], 0, 0)

    return pl.pallas_call(
        _experts_kernel,
        grid_spec=pltpu.PrefetchScalarGridSpec(
            num_scalar_prefetch=2,
            grid=(n_blocks,),
            in_specs=[pl.BlockSpec((MOE_BM * ROW_CHUNKS, LANES), lambda j, be, nu: (last_used(j, nu), 0)),
                      pl.BlockSpec((1, 1, D_MODEL, D_EXPERT), wblk),
                      pl.BlockSpec((1, 1, D_MODEL, D_EXPERT), wblk),
                      pl.BlockSpec((1, 1, D_EXPERT, D_MODEL), wblk)],
            out_specs=pl.BlockSpec((MOE_BM * ROW_CHUNKS, LANES), lambda j, be, nu: (j, 0)),
            scratch_shapes=[pltpu.VMEM((D_MODEL, D_EXPERT), BF16), pltpu.VMEM((D_MODEL, D_EXPERT), BF16),
                            pltpu.VMEM((D_EXPERT, D_MODEL), BF16)]),
        out_shape=jax.ShapeDtypeStruct(xs.shape, F32),
        compiler_params=_cparams("arbitrary"),
        name="moe_experts",
    )(block_e, n_used, xs, wg, wu, wd)


def _combine_ln_kernel(lpos_ref, tcnt_ref, toff_ref, rstart_ref, y_hbm, x_ref, w1_ref, w2_ref, g_ref, b_ref, o_ref,
                       r0, r1, u1, u2, sem, *, n, tm):
    i = pl.program_id(0)
    nt = pl.num_programs(0)
    bufs = (r0, r1)
    unroll = 8

    def fetch(tile, slot):
        _tile_runs(tcnt_ref, toff_ref, rstart_ref, tile, bufs[slot], y_hbm, sem.at[slot], to_hbm=False, tm=tm)

    @pl.when(i == 0)
    def _():
        fetch(0, 0)

    for slot in range(2):
        @pl.when(i % 2 == slot)
        def _():
            @pl.when(i + 1 < nt)
            def _():
                fetch(i + 1, 1 - slot)

            buf = bufs[slot]
            _rows_wait(y_hbm, buf, sem.at[slot])

            def place(c, carry):
                tok = i * tm + c * unroll
                dst0 = pl.multiple_of(c * (unroll * ROW_CHUNKS), unroll * ROW_CHUNKS)
                for u in range(unroll):
                    dst = pl.ds(dst0 + u * ROW_CHUNKS, ROW_CHUNKS)
                    for k, out in enumerate((u1, u2)):
                        p = lpos_ref[k * n + tok + u]
                        out[dst, :] = buf[pl.ds(pl.multiple_of(p, ROW_CHUNKS), ROW_CHUNKS), :]
                return carry

            lax.fori_loop(0, tm // unroll, place, 0)
            moe = w1_ref[...] * _read_rows(u1, tm) + w2_ref[...] * _read_rows(u2, tm)
            o_ref[...] = _ln(ALPHA * _read_rows(x_ref, tm) + moe, g_ref[...], b_ref[...])


def _combine_ln(lpos_flat, tcnt, toff, rstart, y, xr, w1, w2, g, b, tm):
    n = xr.shape[0] // ROW_CHUNKS
    return pl.pallas_call(
        functools.partial(_combine_ln_kernel, n=n, tm=tm),
        grid_spec=pltpu.PrefetchScalarGridSpec(
            num_scalar_prefetch=4,
            grid=(n // tm,),
            in_specs=[pl.BlockSpec(memory_space=pl.ANY),
                      pl.BlockSpec((tm * ROW_CHUNKS, LANES), lambda i, *_: (i, 0)),
                      pl.BlockSpec((tm, 1), lambda i, *_: (i, 0)),
                      pl.BlockSpec((tm, 1), lambda i, *_: (i, 0)),
                      pl.BlockSpec((1, D_MODEL), lambda i, *_: (0, 0)),
                      pl.BlockSpec((1, D_MODEL), lambda i, *_: (0, 0))],
            out_specs=pl.BlockSpec((tm, D_MODEL), lambda i, *_: (i, 0)),
            scratch_shapes=[pltpu.VMEM((2 * tm * ROW_CHUNKS, LANES), F32)] * 2
                           + [pltpu.VMEM((tm * ROW_CHUNKS, LANES), F32)] * 2
                           + [pltpu.SemaphoreType.DMA((2,))]),
        out_shape=jax.ShapeDtypeStruct((n, D_MODEL), F32),
        compiler_params=_cparams("arbitrary"),
        name="moe_combine_ln",
    )(lpos_flat, tcnt, toff, rstart, y, xr, w1, w2, g, b)


def _moe_ln(xr, rw, rb, wg, wu, wd, layer, g, b):
    n = xr.shape[0] // ROW_CHUNKS
    n_blocks = (2 * n) // MOE_BM + N_EXPERTS
    tm = MOE_TILE
    nt = n // tm
    lpos, w, cnt, tcnt, toff, tbef = _router(xr, rw, rb, tm)
    meta, rstart = _plan(cnt, tbef, n_blocks)
    block_e = meta[0, :n_blocks]
    n_used = meta[1, :1]

    def per_tile(table):
        return table[:, :nt].T.reshape(nt * N_EXPERTS).astype(I32)

    lpos_flat = lpos.reshape(2 * n)
    tcnt, toff, rstart = per_tile(tcnt), per_tile(toff), per_tile(rstart)
    xs = _dispatch(lpos_flat, tcnt, toff, rstart, meta[2, :N_EXPERTS], meta[3, :N_EXPERTS], n_used, xr,
                   n_blocks, tm)
    y = _experts(block_e, n_used, xs, wg, wu, wd, layer)
    return _combine_ln(lpos_flat, tcnt, toff, rstart, y, xr, w[0].reshape(n, 1), w[1].reshape(n, 1), g, b, tm)


def _conv_qkv_kernel(xm_ref, cw_ref, cb_ref, wq_ref, wk_ref, wv_ref, q_ref, k_ref, v_ref, xc_ref, *, s):
    xm_b = xm_ref[0]
    xm = xm_b.astype(F32)
    cw = cw_ref[...]
    row = lax.broadcasted_iota(I32, (s, 1), 0)
    half = CONV_K // 2
    acc = cb_ref[...] + xm * cw[half:half + 1, :]
    for sh in range(1, half + 1):
        past = jnp.where(row >= sh, pltpu.roll(xm, sh, axis=0), 0.0)
        acc = acc + past * cw[half - sh:half - sh + 1, :]
        nxt = jnp.where(row < s - sh, pltpu.roll(xm, s - sh, axis=0), 0.0)
        acc = acc + nxt * cw[half + sh:half + sh + 1, :]
    xc = _silu(acc).astype(BF16)
    xc_ref[0] = xc
    q_ref[0] = _dot(xc, wq_ref[0]).astype(BF16)
    k_ref[0] = (_dot_nt(wk_ref[0], xc) * (ML_DH ** -0.5)).astype(BF16)
    v = _dot(xm_b, wv_ref[0])
    ones_lane = lax.broadcasted_iota(I32, (1, ML_DHP), 1) == ML_DH
    v_ref[0] = jnp.where(ones_lane, 1.0, v).astype(BF16)


def _conv_qkv(main3, cw, cb, wq, wk_t, wv):
    b, s, _ = main3.shape
    tok = pl.BlockSpec((1, s, ML_DHP), lambda i, h: (i, 0, h))
    wspec = pl.BlockSpec((1, ML_DHP, ML_DHP), lambda i, h: (h, 0, 0))
    tok_shape = jax.ShapeDtypeStruct((b, s, ML_WP), BF16)
    return pl.pallas_call(
        functools.partial(_conv_qkv_kernel, s=s),
        grid=(b, ML_HEADS),
        in_specs=[tok,
                  pl.BlockSpec((CONV_K, ML_DHP), lambda i, h: (0, h)),
                  pl.BlockSpec((1, ML_DHP), lambda i, h: (0, h)),
                  wspec, wspec, wspec],
        out_specs=[tok, pl.BlockSpec((1, ML_DHP, s), lambda i, h: (i, h, 0)), tok, tok],
        out_shape=[tok_shape, jax.ShapeDtypeStruct((b, ML_WP, s), BF16), tok_shape, tok_shape],
        compiler_params=_cparams("parallel", "parallel"),
        name="conv_qkv",
    )(main3, cw, cb, wq, wk_t, wv)


def _mlstm_kernel(q_ref, kt_ref, v_ref, gc_ref, gr_ref, z_ref, xc_ref, ng_ref, sk_ref,
                  y_ref, hf_ref, hb_ref, cf_ref, cb_ref, m_ref, *, s):
    head0 = pl.program_id(1) * ML_HPS
    nc = s // CHUNK
    sub = lax.broadcasted_iota(I32, (LANES, 1), 0)
    gate = lax.broadcasted_iota(I32, (LANES, LANES), 0)
    ti = lax.broadcasted_iota(I32, (CHUNK, CHUNK), 0)
    tj = lax.broadcasted_iota(I32, (CHUNK, CHUNK), 1)

    for ref in (cf_ref, cb_ref, m_ref):
        ref[...] = jnp.zeros_like(ref)

    def intra(c, j, rev):
        t0 = pl.multiple_of(c * CHUNK, CHUNK)
        hl = slice(j * ML_DHP, (j + 1) * ML_DHP)
        qb = q_ref[0, pl.ds(t0, CHUNK), hl]
        kt = kt_ref[0, hl, pl.ds(t0, CHUNK)]
        vb = v_ref[0, pl.ds(t0, CHUNK), hl]
        gc = gc_ref[0, pl.ds(t0, CHUNK), :]
        gr = gr_ref[:, pl.ds(t0, CHUNK)]
        i_idx = head0 + j + (2 * ML_HEADS if rev else 0)
        f_idx = i_idx + ML_HEADS
        allowed = (tj >= ti) if rev else (tj <= ti)
        sel = (gate == f_idx).astype(BF16)
        b_rep = sum(_dot(part, sel) for part in _split3(gc))
        b_row = jnp.sum(jnp.where(sub == f_idx, gr, 0.0), axis=0, keepdims=True)
        i_row = jnp.sum(jnp.where(sub == i_idx, gr, 0.0), axis=0, keepdims=True)
        b_last = b_rep[0:1, :] if rev else b_rep[CHUNK - 1:CHUNK, :]

        d = jnp.where(allowed, b_rep - b_row + i_row, NEG)
        m_in = jnp.max(d, axis=1, keepdims=True)
        sc = _dot(qb, kt) * jnp.exp(d - m_in)
        nd_in = _dot(sc.astype(BF16), vb)
        w_row = b_last - b_row + i_row
        return t0, qb, kt, vb, b_rep, b_last[:, 0:1], m_in, nd_in, w_row

    def twice(a):
        return jnp.concatenate([a, a], axis=1)

    def update(parts, j, rev):
        t0, qb, kt, vb, b_rep, b_last, m_in, nd_in, w_row = parts
        h_ref, c_ref = (hb_ref, cb_ref) if rev else (hf_ref, cf_ref)
        hl = slice(j * ML_DHP, (j + 1) * ML_DHP)
        mrow = 2 * j + int(rev)
        m = m_ref[mrow:mrow + 1, 0:1]
        cmat = c_ref[j]
        inter = b_rep + m
        m_t = jnp.maximum(m_in, inter)
        a_in = jnp.exp(m_in - m_t)
        iexp = jnp.exp(inter - m_t)
        nd = twice(a_in) * nd_in + twice(iexp) * _dot(qb, cmat.astype(BF16))
        den = nd[:, ML_DH:ML_DH + 1]
        h_ref[pl.ds(t0, CHUNK), hl] = nd * (1.0 / jnp.maximum(jnp.abs(den), jnp.exp(-m_t[:, 0:1])))

        m_new = jnp.maximum(b_last + m, jnp.max(w_row, axis=1, keepdims=True))
        wexp = jnp.exp(w_row - m_new)
        cexp = jnp.exp(b_last + m - m_new)
        kw = (kt.astype(F32) * wexp).astype(BF16)
        c_ref[j] = cexp * cmat + _dot(kw, vb)
        m_ref[mrow:mrow + 1, :] = jnp.broadcast_to(m_new, (1, LANES))

    def step(i, carry):
        chains = [(j, rev) for j in range(ML_HPS) for rev in (False, True)]
        parts = [intra(nc - 1 - i if rev else i, j, rev) for j, rev in chains]
        for p, (j, rev) in zip(parts, chains):
            update(p, j, rev)
        return carry

    lax.fori_loop(0, nc, step, 0)

    real = lax.broadcasted_iota(I32, (1, ML_DHP), 1) < ML_DH
    tb = 256

    def fin(c, carry):
        t0 = pl.multiple_of(c * tb, tb)
        for j in range(ML_HPS):
            hl = slice(j * ML_DHP, (j + 1) * ML_DHP)
            hs = jnp.where(real, hf_ref[pl.ds(t0, tb), hl] + hb_ref[pl.ds(t0, tb), hl], 0.0)
            mu = jnp.sum(hs, axis=1, keepdims=True) * (1.0 / ML_DH)
            dev = jnp.where(real, hs - mu, 0.0)
            var = jnp.sum(dev * dev, axis=1, keepdims=True) * (1.0 / ML_DH)
            hn = dev * lax.rsqrt(var + LN_EPS) * ng_ref[:, hl]
            xc = xc_ref[0, pl.ds(t0, tb), hl].astype(F32)
            z = z_ref[0, pl.ds(t0, tb), hl].astype(F32)
            y_ref[0, pl.ds(t0, tb), hl] = ((hn + sk_ref[:, hl] * xc) * _silu(z)).astype(BF16)
        return carry

    lax.fori_loop(0, s // tb, fin, 0)


def _mlstm(q, kt, v, gcol3, grow, main3, xc, ng, sk):
    b, s, _ = q.shape
    width = ML_HPS * ML_DHP
    steps = ML_HEADS // ML_HPS
    tok = pl.BlockSpec((1, s, width), lambda i, h: (i, 0, h))
    vec = pl.BlockSpec((1, width), lambda i, h: (0, h))
    return pl.pallas_call(
        functools.partial(_mlstm_kernel, s=s),
        grid=(b, steps),
        in_specs=[tok, pl.BlockSpec((1, width, s), lambda i, h: (i, h, 0)), tok,
                  pl.BlockSpec((1, s, LANES), lambda i, h: (i, 0, 0)),
                  pl.BlockSpec((LANES, s), lambda i, h: (0, i)),
                  pl.BlockSpec((1, s, width), lambda i, h: (i, 0, steps + h)),
                  tok, vec, vec],
        out_specs=tok,
        out_shape=jax.ShapeDtypeStruct((b, s, ML_WP), BF16),
        scratch_shapes=[pltpu.VMEM((s, width), F32), pltpu.VMEM((s, width), F32),
                        pltpu.VMEM((ML_HPS, ML_DHP, ML_DHP), F32), pltpu.VMEM((ML_HPS, ML_DHP, ML_DHP), F32),
                        pltpu.VMEM((SUBLANES, LANES), F32)],
        compiler_params=_cparams("parallel", "parallel"),
        name="mlstm",
    )(q, kt, v, gcol3, grow, main3, xc, ng, sk)


def _pad_heads(a, axis):
    a = jnp.moveaxis(a, axis, -1)
    lead = a.shape[:-1]
    a = a.reshape(lead + (ML_HEADS, ML_DH))
    a = jnp.pad(a, [(0, 0)] * len(lead) + [(0, 0), (0, ML_DHP - ML_DH)])
    return jnp.moveaxis(a.reshape(lead + (ML_WP,)), -1, axis)


def kernel(x, mem, mem_ln_g, mem_ln_b, w_mem_kv, router_w, router_b, na_w_in, na_rpb, ml_w_in, ml_conv_w,
           ml_conv_b, ml_w_qkv, ml_gate_b, ml_norm_g, ml_skip, w_out, ln_g, ln_b, exp_w_gate, exp_w_up,
           exp_w_down):
    b, s, d = x.shape
    n = b * s
    nm = mem.shape[1]
    row = lambda a: a.reshape(1, -1)

    mem_k, mem_v = _memkv(mem.reshape(b * nm, d), row(mem_ln_g), row(mem_ln_b), w_mem_kv.astype(BF16))
    mem_k3 = mem_k.reshape(b, nm, MEM_W)
    mem_v3 = mem_v.reshape(b, nm, MEM_W)
    rw_pad = jnp.pad(router_w, ((0, 0), (0, LANES - N_EXPERTS)))
    rw_hi = rw_pad.astype(BF16)
    rw = (rw_hi, (rw_pad - rw_hi.astype(F32)).astype(BF16))
    rb = router_b.reshape(N_EXPERTS, 1)

    x2 = x.reshape(n, d)

    h0 = _proj(x2, na_w_in[0].astype(BF16)).reshape(b, s, 3 * NA_W + MEM_W)
    y_na = _na_attention(h0, _na_bias_table(na_rpb[0]))
    wo = w_out[0].astype(BF16)
    xr = _outproj_ln(y_na.reshape(n, NA_W), h0.reshape(n, 3 * NA_W + MEM_W), 3 * NA_W // MEM_W, mem_k3, mem_v3,
                     wo[:NA_W], wo[NA_W:], x2, row(ln_g[0, 0]), row(ln_b[0, 0]))
    x2 = _moe_ln(xr, rw, rb, exp_w_gate, exp_w_up, exp_w_down, 0, row(ln_g[0, 1]), row(ln_b[0, 1]))

    w1 = ml_w_in[0]
    w_main = jnp.concatenate([_pad_heads(w1[:, :ML_W], 1), _pad_heads(w1[:, ML_W:2 * ML_W], 1),
                              w1[:, 2 * ML_W + 4 * ML_HEADS:]], axis=1).astype(BF16)
    w_g = jnp.pad(w1[:, 2 * ML_W:2 * ML_W + 4 * ML_HEADS], ((0, 0), (0, LANES - 4 * ML_HEADS))).astype(BF16)
    gb = jnp.pad(ml_gate_b[0].reshape(4 * ML_HEADS), (0, LANES - 4 * ML_HEADS))
    main, acol, arow = _proj_gates(x2, w_main, w_g, w_g.T, gb.reshape(1, LANES), gb.reshape(LANES, 1))
    main3 = main.reshape(b, s, 2 * ML_WP + MEM_W)
    wqkv = jnp.pad(ml_w_qkv[0], ((0, 0), (0, 0), (0, ML_DHP - ML_DH), (0, ML_DHP - ML_DH))).astype(BF16)
    q, k, v, xc = _conv_qkv(main3, _pad_heads(ml_conv_w[0], 1), _pad_heads(row(ml_conv_b[0]), 1),
                            wqkv[0], jnp.swapaxes(wqkv[1], 1, 2), wqkv[2])
    y_ml = _mlstm(q, k, v, acol.reshape(b, s, LANES), arow, main3, xc,
                  _pad_heads(row(ml_norm_g[0]), 1), _pad_heads(row(ml_skip[0]), 1))
    wo = w_out[1]
    xr = _outproj_ln(y_ml.reshape(n, ML_WP), main, 2 * ML_WP // MEM_W, mem_k3, mem_v3,
                     _pad_heads(wo[:ML_W], 0).astype(BF16), wo[ML_W:].astype(BF16), x2,
                     row(ln_g[1, 0]), row(ln_b[1, 0]))
    x2 = _moe_ln(xr, rw, rb, exp_w_gate, exp_w_up, exp_w_down, 1, row(ln_g[1, 1]), row(ln_b[1, 1]))
    return x2.reshape(b, s, d)
```

```python
import functools

import numpy as np
import jax
import jax.numpy as jnp
from jax import lax
from jax.experimental import pallas as pl
from jax.experimental.pallas import tpu as pltpu

F32 = jnp.float32
BF16 = jnp.bfloat16
I32 = jnp.int32

D_MODEL = 1024
DEPTH = 2
GRID_W = 64
MEM_HEADS = 4
MEM_DH = 64
MEM_W = MEM_HEADS * MEM_DH
NA_HEADS = 12
NA_DH = 64
NA_W = NA_HEADS * NA_DH
WIN_H = 8
WIN_W = 16
ML_HEADS = 4
ML_DH = 192
ML_DHP = 256
ML_W = ML_HEADS * ML_DH
ML_WP = ML_HEADS * ML_DHP
CONV_K = 5
CHUNK = 128
N_EXPERTS = 16
N_GROUPS = 4
EXPERTS_PER_GROUP = N_EXPERTS // N_GROUPS
D_EXPERT = 512
ALPHA = (2 * DEPTH) ** 0.25
LN_EPS = 1e-5
NEG = -1e30

LANES = 128
SUBLANES = 8
ROW_CHUNKS = D_MODEL // LANES
MOE_BM = 512
MOE_TILE = 512
ML_HPS = 2
NA_ROWS_PER_STEP = 8
ROW_TILE = 512
PLACE_UNROLL = 8
VMEM_LIMIT = 48 * 1024 * 1024


def _cparams(*sem):
    return pltpu.CompilerParams(dimension_semantics=sem, vmem_limit_bytes=VMEM_LIMIT)


def _dot(a, b):
    return jnp.dot(a, b, preferred_element_type=F32)


def _dot_nt(a, b, precision=None):
    return lax.dot_general(a, b, (((1,), (1,)), ((), ())), precision=precision,
                           preferred_element_type=F32)


def _ln(z, g, b):
    mu = jnp.mean(z, axis=-1, keepdims=True)
    zc = z - mu
    var = jnp.mean(zc * zc, axis=-1, keepdims=True)
    return zc * lax.rsqrt(var + LN_EPS) * g + b


def _silu(x):
    return x * jax.nn.sigmoid(x)


def _read_rows(ref, n):
    return jnp.concatenate([ref[pl.ds(j, n, stride=ROW_CHUNKS), :] for j in range(ROW_CHUNKS)], axis=1)


def _write_rows(ref, val, n):
    for j in range(ROW_CHUNKS):
        ref[pl.ds(j, n, stride=ROW_CHUNKS), :] = val[:, j * LANES:(j + 1) * LANES]


def _memkv_kernel(m_ref, g_ref, b_ref, w_ref, k_ref, v_ref):
    z = _ln(m_ref[...], g_ref[...], b_ref[...])
    kv = _dot(z.astype(BF16), w_ref[...])
    k_ref[...] = kv[:, :MEM_W].astype(BF16)
    v_ref[...] = kv[:, MEM_W:].astype(BF16)


def _memkv(mem2, g, b, w):
    n = mem2.shape[0]
    tm = min(ROW_TILE, n)
    return pl.pallas_call(
        _memkv_kernel,
        grid=(n // tm,),
        in_specs=[pl.BlockSpec((tm, D_MODEL), lambda i: (i, 0)),
                  pl.BlockSpec((1, D_MODEL), lambda i: (0, 0)),
                  pl.BlockSpec((1, D_MODEL), lambda i: (0, 0)),
                  pl.BlockSpec((D_MODEL, 2 * MEM_W), lambda i: (0, 0))],
        out_specs=[pl.BlockSpec((tm, MEM_W), lambda i: (i, 0)),
                   pl.BlockSpec((tm, MEM_W), lambda i: (i, 0))],
        out_shape=[jax.ShapeDtypeStruct((n, MEM_W), BF16)] * 2,
        compiler_params=_cparams("parallel"),
        name="memkv",
    )(mem2, g, b, w)


def _proj_kernel(x_ref, w_ref, o_ref):
    o_ref[...] = _dot(x_ref[...].astype(BF16), w_ref[...]).astype(o_ref.dtype)


def _proj(x2, w, tm=ROW_TILE):
    n, k = x2.shape
    nout = w.shape[1]
    return pl.pallas_call(
        _proj_kernel,
        grid=(n // tm,),
        in_specs=[pl.BlockSpec((tm, k), lambda i: (i, 0)),
                  pl.BlockSpec((k, nout), lambda i: (0, 0))],
        out_specs=pl.BlockSpec((tm, nout), lambda i: (i, 0)),
        out_shape=jax.ShapeDtypeStruct((n, nout), BF16),
        compiler_params=_cparams("parallel"),
        name="in_proj",
    )(x2, w)


def _split3(x):
    hi = x.astype(BF16)
    r1 = x - hi.astype(F32)
    mid = r1.astype(BF16)
    lo = (r1 - mid.astype(F32)).astype(BF16)
    return hi, mid, lo


def _proj_gates_kernel(x_ref, w_ref, wg_ref, wgt_ref, gbc_ref, gbr_ref, o_ref, g_ref, gt_ref, *, tm):
    xb = x_ref[...].astype(BF16)
    o_ref[...] = _dot(xb, w_ref[...]).astype(BF16)
    gcol = _dot(xb, wg_ref[...]) + gbc_ref[...]
    grow = _dot_nt(wgt_ref[...], xb) + gbr_ref[...]
    lane = lax.broadcasted_iota(I32, (1, LANES), 1)
    sub = lax.broadcasted_iota(I32, (LANES, 1), 0)
    ti = lax.broadcasted_iota(I32, (CHUNK, CHUNK), 0)
    tj = lax.broadcasted_iota(I32, (CHUNK, CHUNK), 1)
    lower = (tj <= ti).astype(BF16)
    upper = (ti <= tj).astype(BF16)

    def pick(idx, pre, suf, raw):
        fwd = jnp.logical_and(idx >= ML_HEADS, idx < 2 * ML_HEADS)
        bwd = jnp.logical_and(idx >= 3 * ML_HEADS, idx < 4 * ML_HEADS)
        return jnp.where(fwd, pre, jnp.where(bwd, suf, raw))

    for c in range(tm // CHUNK):
        tc = slice(c * CHUNK, (c + 1) * CHUNK)
        g = gcol[tc, :]
        ls = jax.nn.log_sigmoid(g)
        pre = sum(_dot(lower, part) for part in _split3(ls))
        suf = jnp.sum(ls, axis=0, keepdims=True) - pre + ls
        g_ref[tc, :] = pick(lane, pre, suf, g)
        g = grow[:, tc]
        ls = jax.nn.log_sigmoid(g)
        pre = sum(_dot(part, upper) for part in _split3(ls))
        suf = jnp.sum(ls, axis=1, keepdims=True) - pre + ls
        gt_ref[:, tc] = pick(sub, pre, suf, g)


def _proj_gates(x2, w, wg, wgt, gbc, gbr, tm=ROW_TILE):
    n, k = x2.shape
    nout = w.shape[1]
    return pl.pallas_call(
        functools.partial(_proj_gates_kernel, tm=tm),
        grid=(n // tm,),
        in_specs=[pl.BlockSpec((tm, k), lambda i: (i, 0)),
                  pl.BlockSpec((k, nout), lambda i: (0, 0)),
                  pl.BlockSpec((k, LANES), lambda i: (0, 0)),
                  pl.BlockSpec((LANES, k), lambda i: (0, 0)),
                  pl.BlockSpec((1, LANES), lambda i: (0, 0)),
                  pl.BlockSpec((LANES, 1), lambda i: (0, 0))],
        out_specs=[pl.BlockSpec((tm, nout), lambda i: (i, 0)),
                   pl.BlockSpec((tm, LANES), lambda i: (i, 0)),
                   pl.BlockSpec((LANES, tm), lambda i: (0, i))],
        out_shape=[jax.ShapeDtypeStruct((n, nout), BF16),
                   jax.ShapeDtypeStruct((n, LANES), F32),
                   jax.ShapeDtypeStruct((LANES, n), F32)],
        compiler_params=_cparams("parallel"),
        name="in_proj_gates",
    )(x2, w, wg, wgt, gbc, gbr)


def _na_kernel(q_ref, k_ref, v_ref, tbl_ref, o_ref, *, rows):
    lane = lax.broadcasted_iota(I32, (1, LANES), 1)
    first = lane < NA_DH
    nkeys = WIN_H * GRID_W

    def rows_step(i, carry):
        rr = [i * NA_ROWS_PER_STEP + u for u in range(NA_ROWS_PER_STEP)]
        rss = [jnp.clip(r - WIN_H // 2, 0, rows - WIN_H) for r in rr]
        scores = []
        for r, rs in zip(rr, rss):
            q = q_ref[0, pl.ds(pl.multiple_of(r * GRID_W, GRID_W), GRID_W), :]
            q = q * jnp.asarray(NA_DH ** -0.5, BF16)
            q2 = jnp.concatenate([jnp.where(first, q, jnp.zeros_like(q)),
                                  jnp.where(first, jnp.zeros_like(q), q)], axis=0)
            k = k_ref[0, pl.ds(pl.multiple_of(rs * GRID_W, GRID_W), nkeys), :]
            dr0 = rs - r + WIN_H - 1
            bias = jnp.concatenate(
                [jnp.concatenate([tbl_ref[0, half, dr0 + 2 * m] for m in range(WIN_H // 2)], axis=1)
                 for half in range(2)], axis=0)
            scores.append(_dot_nt(q2, k) + bias)
        probs = []
        for s in scores:
            p = jnp.exp(s - jnp.max(s, axis=-1, keepdims=True))
            probs.append((p.astype(BF16), jnp.sum(p, axis=-1, keepdims=True)))
        for r, rs, (p, l) in zip(rr, rss, probs):
            v = v_ref[0, pl.ds(pl.multiple_of(rs * GRID_W, GRID_W), nkeys), :]
            o = _dot(p, v) / l
            o = jnp.where(first, o[:GRID_W], o[GRID_W:])
            o_ref[0, pl.ds(pl.multiple_of(r * GRID_W, GRID_W), GRID_W), :] = o.astype(o_ref.dtype)
        return carry

    lax.fori_loop(0, rows // NA_ROWS_PER_STEP, rows_step, 0)


def _na_bias_table(rpb):
    qc = np.arange(GRID_W)[:, None]
    kc = np.arange(GRID_W)[None, :]
    cs = np.clip(qc - WIN_W // 2, 0, GRID_W - WIN_W)
    col_in = (kc >= cs) & (kc < cs + WIN_W)
    side = GRID_W - WIN_W
    wide = jnp.pad(rpb, ((0, 0), (0, 0), (side, side)))
    t = jnp.stack([wide[:, :, GRID_W - 1 - q:2 * GRID_W - 1 - q] for q in range(GRID_W)], axis=2)
    t = jnp.where(col_in, t, NEG).astype(F32)
    t2 = jnp.concatenate([t[:, :-1], t[:, 1:]], axis=-1)
    return t2.reshape(NA_HEADS // 2, 2, 2 * WIN_H - 2, GRID_W, 2 * GRID_W)


def _na_attention(h3, tbl):
    b, s, _ = h3.shape
    rows = s // GRID_W
    npair = NA_HEADS // 2
    return pl.pallas_call(
        functools.partial(_na_kernel, rows=rows),
        grid=(b, npair),
        in_specs=[pl.BlockSpec((1, s, LANES), lambda i, p: (i, 0, p)),
                  pl.BlockSpec((1, s, LANES), lambda i, p: (i, 0, npair + p)),
                  pl.BlockSpec((1, s, LANES), lambda i, p: (i, 0, 2 * npair + p)),
                  pl.BlockSpec((1, 2, 2 * WIN_H - 2, GRID_W, 2 * GRID_W), lambda i, p: (p, 0, 0, 0, 0))],
        out_specs=pl.BlockSpec((1, s, LANES), lambda i, p: (i, 0, p)),
        out_shape=jax.ShapeDtypeStruct((b, s, NA_W), BF16),
        compiler_params=_cparams("parallel", "parallel"),
        name="na_attention",
    )(h3, h3, h3, tbl)


def _outproj_ln_kernel(ya_ref, qm_ref, mk_ref, mv_ref, wa_ref, wm_ref, x_ref, g_ref, b_ref, or_ref, *, tm):
    lane = lax.broadcasted_iota(I32, (1, LANES), 1)
    first = lane < MEM_DH
    q = qm_ref[...] * jnp.asarray(MEM_DH ** -0.5, BF16)
    cols = [slice(p * LANES, (p + 1) * LANES) for p in range(MEM_HEADS // 2)]
    scores = []
    for c in cols:
        qp = q[:, c]
        q2 = jnp.concatenate([jnp.where(first, qp, jnp.zeros_like(qp)),
                              jnp.where(first, jnp.zeros_like(qp), qp)], axis=0)
        scores.append(_dot_nt(q2, mk_ref[0, :, c]))
    probs = []
    for s in scores:
        p = jnp.exp(s - jnp.max(s, axis=-1, keepdims=True))
        probs.append((p.astype(BF16), jnp.sum(p, axis=-1, keepdims=True)))
    outs = []
    for c, (p, l) in zip(cols, probs):
        o = _dot(p, mv_ref[0, :, c]) / l
        outs.append(jnp.where(first, o[:tm], o[tm:]))
    ym = jnp.concatenate(outs, axis=1).astype(BF16)
    acc = _dot(ya_ref[...], wa_ref[...]) + _dot(ym, wm_ref[...])
    _write_rows(or_ref, _ln(ALPHA * x_ref[...] + acc, g_ref[...], b_ref[...]), tm)


def _outproj_ln(ya, h2, qm_block, mem_k3, mem_v3, wa, wm, x2, g, b, tm=ROW_TILE):
    n = x2.shape[0]
    ka = ya.shape[1]
    nb, nm, _ = mem_k3.shape
    per_batch = n // nb // tm
    full = lambda shape: pl.BlockSpec(shape, lambda i: (0,) * len(shape))
    return pl.pallas_call(
        functools.partial(_outproj_ln_kernel, tm=tm),
        grid=(n // tm,),
        in_specs=[pl.BlockSpec((tm, ka), lambda i: (i, 0)),
                  pl.BlockSpec((tm, MEM_W), lambda i: (i, qm_block)),
                  pl.BlockSpec((1, nm, MEM_W), lambda i: (i // per_batch, 0, 0)),
                  pl.BlockSpec((1, nm, MEM_W), lambda i: (i // per_batch, 0, 0)),
                  full((ka, D_MODEL)), full((MEM_W, D_MODEL)),
                  pl.BlockSpec((tm, D_MODEL), lambda i: (i, 0)),
                  full((1, D_MODEL)), full((1, D_MODEL))],
        out_specs=pl.BlockSpec((tm * ROW_CHUNKS, LANES), lambda i: (i, 0)),
        out_shape=jax.ShapeDtypeStruct((n * ROW_CHUNKS, LANES), F32),
        compiler_params=_cparams("parallel"),
        name="outproj_ln",
    )(ya, h2, mem_k3, mem_v3, wa, wm, x2, g, b)


def _router_kernel(x_ref, rwh_ref, rwl_ref, rb_ref, lpos_ref, w_ref, cnt_ref, tcnt_ref, toff_ref, tbef_ref, *, tm):
    @pl.when(pl.program_id(0) == 0)
    def _():
        cnt_ref[...] = jnp.zeros_like(cnt_ref)

    x = _read_rows(x_ref, tm)
    xh = x.astype(BF16)
    xl = (x - xh.astype(F32)).astype(BF16)
    logits_t = _dot(xh, rwh_ref[...]) + (_dot(xh, rwl_ref[...]) + _dot(xl, rwh_ref[...]))
    logits = logits_t.T[:N_EXPERTS]
    scores = jax.nn.sigmoid(logits)
    biased = scores + rb_ref[...]
    bv = [biased[e:e + 1, :] for e in range(N_EXPERTS)]
    sv = [scores[e:e + 1, :] for e in range(N_EXPERTS)]

    grp = []
    for g in range(N_GROUPS):
        m = bv[g * EXPERTS_PER_GROUP:(g + 1) * EXPERTS_PER_GROUP]
        best = None
        for a in range(EXPERTS_PER_GROUP):
            for c in range(a + 1, EXPERTS_PER_GROUP):
                pair = m[a] + m[c]
                best = pair if best is None else jnp.maximum(best, pair)
        grp.append(best)
    gsel = jnp.zeros((1, tm), I32)
    gbest = grp[0]
    for g in range(1, N_GROUPS):
        better = grp[g] > gbest
        gsel = jnp.where(better, g, gsel)
        gbest = jnp.where(better, grp[g], gbest)

    def pick(vals, j):
        out = vals[j]
        for g in range(1, N_GROUPS):
            out = jnp.where(gsel == g, vals[g * EXPERTS_PER_GROUP + j], out)
        return out

    cb = [pick(bv, j) for j in range(EXPERTS_PER_GROUP)]
    cs = [pick(sv, j) for j in range(EXPERTS_PER_GROUP)]
    i1 = jnp.zeros((1, tm), I32)
    m1 = cb[0]
    s1 = cs[0]
    for j in range(1, EXPERTS_PER_GROUP):
        gt = cb[j] > m1
        i1 = jnp.where(gt, j, i1)
        m1 = jnp.where(gt, cb[j], m1)
        s1 = jnp.where(gt, cs[j], s1)
    i2 = jnp.zeros((1, tm), I32)
    m2 = jnp.full((1, tm), -jnp.inf, F32)
    s2 = jnp.zeros((1, tm), F32)
    for j in range(EXPERTS_PER_GROUP):
        ok = jnp.logical_and(i1 != j, cb[j] > m2)
        i2 = jnp.where(ok, j, i2)
        m2 = jnp.where(ok, cb[j], m2)
        s2 = jnp.where(ok, cs[j], s2)
    e1 = gsel * EXPERTS_PER_GROUP + i1
    e2 = gsel * EXPERTS_PER_GROUP + i2
    tot = s1 + s2
    w_ref[...] = jnp.concatenate([s1 / tot, s2 / tot], axis=0)

    i = pl.program_id(0)
    eio = lax.broadcasted_iota(I32, (N_EXPERTS, tm), 0)
    oh1 = eio == e1
    oh2 = eio == e2
    ohs = jnp.logical_or(oh1, oh2).astype(F32)
    before = (lax.broadcasted_iota(I32, (tm, tm), 0) < lax.broadcasted_iota(I32, (tm, tm), 1))
    pre = _dot(ohs.astype(BF16), before.astype(BF16))
    tile_cnt = jnp.sum(ohs, axis=1, keepdims=True)
    offs = []
    acc = jnp.zeros((1, 1), F32)
    for e in range(N_EXPERTS):
        offs.append(acc)
        acc = acc + tile_cnt[e:e + 1, :]
    tile_off = jnp.concatenate(offs, axis=0)
    pos = tile_off + pre
    p1 = jnp.sum(jnp.where(oh1, pos, 0.0), axis=0, keepdims=True)
    p2 = jnp.sum(jnp.where(oh2, pos, 0.0), axis=0, keepdims=True)
    lpos_ref[...] = jnp.concatenate([p1, p2], axis=0).astype(I32) * ROW_CHUNKS

    @pl.when(i == 0)
    def _():
        for ref in (tcnt_ref, toff_ref, tbef_ref):
            ref[...] = jnp.zeros_like(ref)

    here = lax.broadcasted_iota(I32, (1, LANES), 1) == i
    tcnt_ref[...] = jnp.where(here, tile_cnt, tcnt_ref[...])
    toff_ref[...] = jnp.where(here, tile_off, toff_ref[...])
    tbef_ref[...] = jnp.where(here, cnt_ref[:, 0:1], tbef_ref[...])
    cnt_ref[...] += tile_cnt


def _router(xr, rw, rb, tm):
    n = xr.shape[0] // ROW_CHUNKS
    assert n // tm <= LANES
    table = pl.BlockSpec((N_EXPERTS, LANES), lambda i: (0, 0))
    return pl.pallas_call(
        functools.partial(_router_kernel, tm=tm),
        grid=(n // tm,),
        in_specs=[pl.BlockSpec((tm * ROW_CHUNKS, LANES), lambda i: (i, 0)),
                  pl.BlockSpec((D_MODEL, LANES), lambda i: (0, 0)),
                  pl.BlockSpec((D_MODEL, LANES), lambda i: (0, 0)),
                  pl.BlockSpec((N_EXPERTS, 1), lambda i: (0, 0))],
        out_specs=[pl.BlockSpec((2, tm), lambda i: (0, i)),
                   pl.BlockSpec((2, tm), lambda i: (0, i)),
                   table, table, table, table],
        out_shape=[jax.ShapeDtypeStruct((2, n), I32),
                   jax.ShapeDtypeStruct((2, n), F32)]
                  + [jax.ShapeDtypeStruct((N_EXPERTS, LANES), F32)] * 4,
        compiler_params=_cparams("arbitrary"),
        name="router",
    )(xr, rw[0], rw[1], rb)


def _plan_kernel(cnt_ref, tbef_ref, meta_ref, rstart_ref, *, nbl):
    shift = MOE_BM.bit_length() - 1
    cnt = cnt_ref[...].astype(I32)
    padded = ((cnt + (MOE_BM - 1)) >> shift) << shift
    starts = []
    acc = jnp.zeros((1, LANES), I32)
    for e in range(N_EXPERTS):
        starts.append(acc)
        acc = acc + padded[e:e + 1, :]
    pad_start = jnp.concatenate(starts, axis=0)
    pad_end = pad_start + padded
    rstart_ref[...] = pad_start + tbef_ref[...].astype(I32)
    blk0 = lax.broadcasted_iota(I32, (N_EXPERTS, nbl), 1) * MOE_BM
    block_e = jnp.sum((pad_end[:, 0:1] <= blk0).astype(I32), axis=0, keepdims=True)
    block_e = jnp.minimum(block_e, N_EXPERTS - 1)
    n_used = jnp.broadcast_to(acc[:, 0:1] >> shift, (1, nbl))
    diag = lax.broadcasted_iota(I32, (N_EXPERTS, nbl), 0) == lax.broadcasted_iota(I32, (N_EXPERTS, nbl), 1)
    fill_lo = jnp.sum(jnp.where(diag, (pad_start + cnt)[:, 0:1], 0), axis=0, keepdims=True)
    fill_hi = jnp.sum(jnp.where(diag, pad_end[:, 0:1], 0), axis=0, keepdims=True)
    meta_ref[...] = jnp.concatenate([block_e, n_used, fill_lo, fill_hi, jnp.zeros((SUBLANES - 4, nbl), I32)],
                                    axis=0)


def _plan(cnt, tbef, n_blocks):
    nbl = -(-n_blocks // LANES) * LANES
    table = pl.BlockSpec((N_EXPERTS, LANES), lambda i: (0, 0))
    return pl.pallas_call(
        functools.partial(_plan_kernel, nbl=nbl),
        grid=(1,),
        in_specs=[table, table],
        out_specs=[pl.BlockSpec((SUBLANES, nbl), lambda i: (0, 0)), table],
        out_shape=[jax.ShapeDtypeStruct((SUBLANES, nbl), I32),
                   jax.ShapeDtypeStruct((N_EXPERTS, LANES), I32)],
        compiler_params=_cparams("arbitrary"),
        name="moe_plan",
    )(cnt, tbef)


def _rows(ref, row, nrows):
    return ref.at[pl.ds(pl.multiple_of(row * ROW_CHUNKS, ROW_CHUNKS), nrows * ROW_CHUNKS), :]


def _rows_wait(src_hbm, buf, sem):
    pltpu.make_async_copy(src_hbm.at[pl.ds(0, buf.shape[0]), :], buf, sem).wait()


def _copy_pieces(src, src_row, dst, dst_row, count, max_rows, sem, wait=False):
    bit = max_rows.bit_length() - 1
    while bit >= 0:
        size = 1 << bit
        done = (count >> (bit + 1)) << (bit + 1)

        @pl.when(((count >> bit) & 1) == 1)
        def _():
            cp = pltpu.make_async_copy(_rows(src, src_row + done, size), _rows(dst, dst_row + done, size), sem)
            cp.start()
            if wait:
                cp.wait()

        bit -= 1


def _tile_runs(tcnt_ref, toff_ref, rstart_ref, tile, buf, hbm, sem, *, to_hbm, tm):
    def per_expert(e, carry):
        k = tile * N_EXPERTS + e
        if to_hbm:
            _copy_pieces(buf, toff_ref[k], hbm, rstart_ref[k], tcnt_ref[k], tm, sem)
        else:
            _copy_pieces(hbm, rstart_ref[k], buf, toff_ref[k], tcnt_ref[k], tm, sem)
        return carry

    lax.fori_loop(0, N_EXPERTS, per_expert, 0)


def _dispatch_kernel(lpos_ref, tcnt_ref, toff_ref, rstart_ref, flo_ref, fhi_ref, nu_ref, x_ref, xs_hbm,
                     s0, s1, zbuf, sem, zsem, *, n, tm, n_blocks):
    i = pl.program_id(0)
    nt = pl.num_programs(0)
    bufs = (s0, s1)
    unroll = PLACE_UNROLL

    for slot in range(2):
        @pl.when(i % 2 == slot)
        def _():
            buf = bufs[slot]

            @pl.when(i >= 2)
            def _():
                _rows_wait(xs_hbm, buf, sem.at[slot])

            def place(c, carry):
                tok = i * tm + c * unroll
                src = pl.multiple_of(c * (unroll * ROW_CHUNKS), unroll * ROW_CHUNKS)
                for u in range(unroll):
                    v = x_ref[pl.ds(src + u * ROW_CHUNKS, ROW_CHUNKS), :]
                    for k in range(2):
                        p = lpos_ref[k * n + tok + u]
                        buf[pl.ds(pl.multiple_of(p, ROW_CHUNKS), ROW_CHUNKS), :] = v
                return carry

            lax.fori_loop(0, tm // unroll, place, 0)
            _tile_runs(tcnt_ref, toff_ref, rstart_ref, i, buf, xs_hbm, sem.at[slot], to_hbm=True, tm=tm)

    @pl.when(i == nt - 1)
    def _():
        for slot in range(2):
            @pl.when(nt > slot)
            def _():
                _rows_wait(xs_hbm, bufs[slot], sem.at[slot])

        zbuf[...] = jnp.zeros_like(zbuf)
        for e in range(N_EXPERTS):
            _copy_pieces(zbuf, 0, xs_hbm, flo_ref[e], fhi_ref[e] - flo_ref[e], MOE_BM // 2, zsem, wait=True)

        def zero_block(j, carry):
            cp = pltpu.make_async_copy(zbuf, _rows(xs_hbm, j * MOE_BM, MOE_BM), zsem)
            cp.start()
            cp.wait()
            return carry

        lax.fori_loop(nu_ref[0], n_blocks, zero_block, 0)


def _dispatch(lpos_flat, tcnt, toff, rstart, fill_lo, fill_hi, n_used, xr, n_blocks, tm):
    n = xr.shape[0] // ROW_CHUNKS
    return pl.pallas_call(
        functools.partial(_dispatch_kernel, n=n, tm=tm, n_blocks=n_blocks),
        grid_spec=pltpu.PrefetchScalarGridSpec(
            num_scalar_prefetch=7,
            grid=(n // tm,),
            in_specs=[pl.BlockSpec((tm * ROW_CHUNKS, LANES), lambda i, *_: (i, 0))],
            out_specs=pl.BlockSpec(memory_space=pl.ANY),
            scratch_shapes=[pltpu.VMEM((2 * tm * ROW_CHUNKS, LANES), F32),
                            pltpu.VMEM((2 * tm * ROW_CHUNKS, LANES), F32),
                            pltpu.VMEM((MOE_BM * ROW_CHUNKS, LANES), F32),
                            pltpu.SemaphoreType.DMA((2,)),
                            pltpu.SemaphoreType.DMA(())]),
        out_shape=jax.ShapeDtypeStruct((n_blocks * MOE_BM * ROW_CHUNKS, LANES), F32),
        compiler_params=_cparams("arbitrary"),
        name="moe_dispatch",
    )(lpos_flat, tcnt, toff, rstart, fill_lo, fill_hi, n_used, xr)


def _experts_kernel(be_ref, nu_ref, xs_ref, wg_ref, wu_ref, wd_ref, y_ref, wgb, wub, wdb):
    j = pl.program_id(0)
    used = j < nu_ref[0]

    @pl.when(jnp.logical_and(used, jnp.logical_or(j == 0, be_ref[j] != be_ref[jnp.maximum(j - 1, 0)])))
    def _():
        wgb[...] = wg_ref[0, 0].astype(BF16)
        wub[...] = wu_ref[0, 0].astype(BF16)
        wdb[...] = wd_ref[0, 0].astype(BF16)

    @pl.when(used)
    def _():
        x = _read_rows(xs_ref, MOE_BM).astype(BF16)
        h = _silu(_dot(x, wgb[...])) * _dot(x, wub[...])
        _write_rows(y_ref, _dot(h.astype(BF16), wdb[...]), MOE_BM)

    @pl.when(jnp.logical_not(used))
    def _():
        y_ref[...] = jnp.zeros_like(y_ref)


def _experts(block_e, n_used, xs, wg, wu, wd, layer):
    n_blocks = block_e.shape[0]

    def last_used(j, nu):
        return jnp.minimum(j, nu[0] - 1)

    def wblk(j, be, nu):
        return (layer, be[last_used(j, nu)], 0, 0)

    return pl.pallas_call(
        _experts_kernel,
        grid_spec=pltpu.PrefetchScalarGridSpec(
            num_scalar_prefetch=2,
            grid=(n_blocks,),
            in_specs=[pl.BlockSpec((MOE_BM * ROW_CHUNKS, LANES), lambda j, be, nu: (last_used(j, nu), 0)),
                      pl.BlockSpec((1, 1, D_MODEL, D_EXPERT), wblk),
                      pl.BlockSpec((1, 1, D_MODEL, D_EXPERT), wblk),
                      pl.BlockSpec((1, 1, D_EXPERT, D_MODEL), wblk)],
            out_specs=pl.BlockSpec((MOE_BM * ROW_CHUNKS, LANES), lambda j, be, nu: (j, 0)),
            scratch_shapes=[pltpu.VMEM((D_MODEL, D_EXPERT), BF16), pltpu.VMEM((D_MODEL, D_EXPERT), BF16),
                            pltpu.VMEM((D_EXPERT, D_MODEL), BF16)]),
        out_shape=jax.ShapeDtypeStruct(xs.shape, F32),
        compiler_params=_cparams("arbitrary"),
        name="moe_experts",
    )(block_e, n_used, xs, wg, wu, wd)


def _combine_ln_kernel(lpos_ref, tcnt_ref, toff_ref, rstart_ref, y_hbm, x_ref, w1_ref, w2_ref, g_ref, b_ref, o_ref,
                       r0, r1, u1, u2, sem, *, n, tm):
    i = pl.program_id(0)
    nt = pl.num_programs(0)
    bufs = (r0, r1)
    unroll = PLACE_UNROLL

    def fetch(tile, slot):
        _tile_runs(tcnt_ref, toff_ref, rstart_ref, tile, bufs[slot], y_hbm, sem.at[slot], to_hbm=False, tm=tm)

    @pl.when(i == 0)
    def _():
        fetch(0, 0)

    for slot in range(2):
        @pl.when(i % 2 == slot)
        def _():
            @pl.when(i + 1 < nt)
            def _():
                fetch(i + 1, 1 - slot)

            buf = bufs[slot]
            _rows_wait(y_hbm, buf, sem.at[slot])

            def place(c, carry):
                tok = i * tm + c * unroll
                dst0 = pl.multiple_of(c * (unroll * ROW_CHUNKS), unroll * ROW_CHUNKS)
                for u in range(unroll):
                    dst = pl.ds(dst0 + u * ROW_CHUNKS, ROW_CHUNKS)
                    for k, out in enumerate((u1, u2)):
                        p = lpos_ref[k * n + tok + u]
                        out[dst, :] = buf[pl.ds(pl.multiple_of(p, ROW_CHUNKS), ROW_CHUNKS), :]
                return carry

            lax.fori_loop(0, tm // unroll, place, 0)
            moe = w1_ref[...] * _read_rows(u1, tm) + w2_ref[...] * _read_rows(u2, tm)
            o_ref[...] = _ln(ALPHA * _read_rows(x_ref, tm) + moe, g_ref[...], b_ref[...])


def _combine_ln(lpos_flat, tcnt, toff, rstart, y, xr, w1, w2, g, b, tm):
    n = xr.shape[0] // ROW_CHUNKS
    return pl.pallas_call(
        functools.partial(_combine_ln_kernel, n=n, tm=tm),
        grid_spec=pltpu.PrefetchScalarGridSpec(
            num_scalar_prefetch=4,
            grid=(n // tm,),
            in_specs=[pl.BlockSpec(memory_space=pl.ANY),
                      pl.BlockSpec((tm * ROW_CHUNKS, LANES), lambda i, *_: (i, 0)),
                      pl.BlockSpec((tm, 1), lambda i, *_: (i, 0)),
                      pl.BlockSpec((tm, 1), lambda i, *_: (i, 0)),
                      pl.BlockSpec((1, D_MODEL), lambda i, *_: (0, 0)),
                      pl.BlockSpec((1, D_MODEL), lambda i, *_: (0, 0))],
            out_specs=pl.BlockSpec((tm, D_MODEL), lambda i, *_: (i, 0)),
            scratch_shapes=[pltpu.VMEM((2 * tm * ROW_CHUNKS, LANES), F32)] * 2
                           + [pltpu.VMEM((tm * ROW_CHUNKS, LANES), F32)] * 2
                           + [pltpu.SemaphoreType.DMA((2,))]),
        out_shape=jax.ShapeDtypeStruct((n, D_MODEL), F32),
        compiler_params=_cparams("arbitrary"),
        name="moe_combine_ln",
    )(lpos_flat, tcnt, toff, rstart, y, xr, w1, w2, g, b)


def _moe_ln(xr, rw, rb, wg, wu, wd, layer, g, b):
    n = xr.shape[0] // ROW_CHUNKS
    n_blocks = (2 * n) // MOE_BM + N_EXPERTS
    tm = MOE_TILE
    nt = n // tm
    lpos, w, cnt, tcnt, toff, tbef = _router(xr, rw, rb, tm)
    meta, rstart = _plan(cnt, tbef, n_blocks)
    block_e = meta[0, :n_blocks]
    n_used = meta[1, :1]

    def per_tile(table):
        return table[:, :nt].T.reshape(nt * N_EXPERTS).astype(I32)

    lpos_flat = lpos.reshape(2 * n)
    tcnt, toff, rstart = per_tile(tcnt), per_tile(toff), per_tile(rstart)
    xs = _dispatch(lpos_flat, tcnt, toff, rstart, meta[2, :N_EXPERTS], meta[3, :N_EXPERTS], n_used, xr,
                   n_blocks, tm)
    y = _experts(block_e, n_used, xs, wg, wu, wd, layer)
    return _combine_ln(lpos_flat, tcnt, toff, rstart, y, xr, w[0].reshape(n, 1), w[1].reshape(n, 1), g, b, tm)


def _conv_qkv_kernel(xm_ref, cw_ref, cb_ref, wq_ref, wk_ref, wv_ref, q_ref, k_ref, v_ref, xc_ref, *, s):
    xm_b = xm_ref[0]
    xm = xm_b.astype(F32)
    cw = cw_ref[...]
    row = lax.broadcasted_iota(I32, (s, 1), 0)
    half = CONV_K // 2
    acc = cb_ref[...] + xm * cw[half:half + 1, :]
    for sh in range(1, half + 1):
        past = jnp.where(row >= sh, pltpu.roll(xm, sh, axis=0), 0.0)
        acc = acc + past * cw[half - sh:half - sh + 1, :]
        nxt = jnp.where(row < s - sh, pltpu.roll(xm, s - sh, axis=0), 0.0)
        acc = acc + nxt * cw[half + sh:half + sh + 1, :]
    xc = _silu(acc).astype(BF16)
    xc_ref[0] = xc
    q_ref[0] = _dot(xc, wq_ref[0]).astype(BF16)
    k_ref[0] = (_dot_nt(wk_ref[0], xc) * (ML_DH ** -0.5)).astype(BF16)
    v = _dot(xm_b, wv_ref[0])
    ones_lane = lax.broadcasted_iota(I32, (1, ML_DHP), 1) == ML_DH
    v_ref[0] = jnp.where(ones_lane, 1.0, v).astype(BF16)


def _conv_qkv(main3, cw, cb, wq, wk_t, wv):
    b, s, _ = main3.shape
    tok = pl.BlockSpec((1, s, ML_DHP), lambda i, h: (i, 0, h))
    wspec = pl.BlockSpec((1, ML_DHP, ML_DHP), lambda i, h: (h, 0, 0))
    tok_shape = jax.ShapeDtypeStruct((b, s, ML_WP), BF16)
    return pl.pallas_call(
        functools.partial(_conv_qkv_kernel, s=s),
        grid=(b, ML_HEADS),
        in_specs=[tok,
                  pl.BlockSpec((CONV_K, ML_DHP), lambda i, h: (0, h)),
                  pl.BlockSpec((1, ML_DHP), lambda i, h: (0, h)),
                  wspec, wspec, wspec],
        out_specs=[tok, pl.BlockSpec((1, ML_DHP, s), lambda i, h: (i, h, 0)), tok, tok],
        out_shape=[tok_shape, jax.ShapeDtypeStruct((b, ML_WP, s), BF16), tok_shape, tok_shape],
        compiler_params=_cparams("parallel", "parallel"),
        name="conv_qkv",
    )(main3, cw, cb, wq, wk_t, wv)


def _mlstm_kernel(q_ref, kt_ref, v_ref, gc_ref, gr_ref, z_ref, xc_ref, ng_ref, sk_ref,
                  y_ref, hf_ref, hb_ref, cf_ref, cb_ref, m_ref, *, s):
    head0 = pl.program_id(1) * ML_HPS
    nc = s // CHUNK
    sub = lax.broadcasted_iota(I32, (LANES, 1), 0)
    gate = lax.broadcasted_iota(I32, (LANES, LANES), 0)
    ti = lax.broadcasted_iota(I32, (CHUNK, CHUNK), 0)
    tj = lax.broadcasted_iota(I32, (CHUNK, CHUNK), 1)

    for ref in (cf_ref, cb_ref, m_ref):
        ref[...] = jnp.zeros_like(ref)

    def intra(c, j, rev):
        t0 = pl.multiple_of(c * CHUNK, CHUNK)
        hl = slice(j * ML_DHP, (j + 1) * ML_DHP)
        qb = q_ref[0, pl.ds(t0, CHUNK), hl]
        kt = kt_ref[0, hl, pl.ds(t0, CHUNK)]
        vb = v_ref[0, pl.ds(t0, CHUNK), hl]
        gc = gc_ref[0, pl.ds(t0, CHUNK), :]
        gr = gr_ref[:, pl.ds(t0, CHUNK)]
        i_idx = head0 + j + (2 * ML_HEADS if rev else 0)
        f_idx = i_idx + ML_HEADS
        allowed = (tj >= ti) if rev else (tj <= ti)
        sel = (gate == f_idx).astype(BF16)
        b_rep = sum(_dot(part, sel) for part in _split3(gc))
        b_row = jnp.sum(jnp.where(sub == f_idx, gr, 0.0), axis=0, keepdims=True)
        i_row = jnp.sum(jnp.where(sub == i_idx, gr, 0.0), axis=0, keepdims=True)
        b_last = b_rep[0:1, :] if rev else b_rep[CHUNK - 1:CHUNK, :]

        d = jnp.where(allowed, b_rep - b_row + i_row, NEG)
        m_in = jnp.max(d, axis=1, keepdims=True)
        sc = _dot(qb, kt) * jnp.exp(d - m_in)
        nd_in = _dot(sc.astype(BF16), vb)
        w_row = b_last - b_row + i_row
        return t0, qb, kt, vb, b_rep, b_last[:, 0:1], m_in, nd_in, w_row

    def twice(a):
        return jnp.concatenate([a, a], axis=1)

    def update(parts, j, rev):
        t0, qb, kt, vb, b_rep, b_last, m_in, nd_in, w_row = parts
        h_ref, c_ref = (hb_ref, cb_ref) if rev else (hf_ref, cf_ref)
        hl = slice(j * ML_DHP, (j + 1) * ML_DHP)
        mrow = 2 * j + int(rev)
        m = m_ref[mrow:mrow + 1, 0:1]
        cmat = c_ref[j]
        inter = b_rep + m
        m_t = jnp.maximum(m_in, inter)
        a_in = jnp.exp(m_in - m_t)
        iexp = jnp.exp(inter - m_t)
        nd = twice(a_in) * nd_in + twice(iexp) * _dot(qb, cmat.astype(BF16))
        den = nd[:, ML_DH:ML_DH + 1]
        h_ref[pl.ds(t0, CHUNK), hl] = nd * (1.0 / jnp.maximum(jnp.abs(den), jnp.exp(-m_t[:, 0:1])))

        m_new = jnp.maximum(b_last + m, jnp.max(w_row, axis=1, keepdims=True))
        wexp = jnp.exp(w_row - m_new)
        cexp = jnp.exp(b_last + m - m_new)
        kw = (kt.astype(F32) * wexp).astype(BF16)
        c_ref[j] = cexp * cmat + _dot(kw, vb)
        m_ref[mrow:mrow + 1, :] = jnp.broadcast_to(m_new, (1, LANES))

    def step(i, carry):
        chains = [(j, rev) for j in range(ML_HPS) for rev in (False, True)]
        parts = [intra(nc - 1 - i if rev else i, j, rev) for j, rev in chains]
        for p, (j, rev) in zip(parts, chains):
            update(p, j, rev)
        return carry

    lax.fori_loop(0, nc, step, 0)

    real = lax.broadcasted_iota(I32, (1, ML_DHP), 1) < ML_DH
    tb = 2 * CHUNK

    def fin(c, carry):
        t0 = pl.multiple_of(c * tb, tb)
        for j in range(ML_HPS):
            hl = slice(j * ML_DHP, (j + 1) * ML_DHP)
            hs = jnp.where(real, hf_ref[pl.ds(t0, tb), hl] + hb_ref[pl.ds(t0, tb), hl], 0.0)
            mu = jnp.sum(hs, axis=1, keepdims=True) * (1.0 / ML_DH)
            dev = jnp.where(real, hs - mu, 0.0)
            var = jnp.sum(dev * dev, axis=1, keepdims=True) * (1.0 / ML_DH)
            hn = dev * lax.rsqrt(var + LN_EPS) * ng_ref[:, hl]
            xc = xc_ref[0, pl.ds(t0, tb), hl].astype(F32)
            z = z_ref[0, pl.ds(t0, tb), hl].astype(F32)
            y_ref[0, pl.ds(t0, tb), hl] = ((hn + sk_ref[:, hl] * xc) * _silu(z)).astype(BF16)
        return carry

    lax.fori_loop(0, s // tb, fin, 0)


def _mlstm(q, kt, v, gcol3, grow, main3, xc, ng, sk):
    b, s, _ = q.shape
    width = ML_HPS * ML_DHP
    steps = ML_HEADS // ML_HPS
    tok = pl.BlockSpec((1, s, width), lambda i, h: (i, 0, h))
    vec = pl.BlockSpec((1, width), lambda i, h: (0, h))
    return pl.pallas_call(
        functools.partial(_mlstm_kernel, s=s),
        grid=(b, steps),
        in_specs=[tok, pl.BlockSpec((1, width, s), lambda i, h: (i, h, 0)), tok,
                  pl.BlockSpec((1, s, LANES), lambda i, h: (i, 0, 0)),
                  pl.BlockSpec((LANES, s), lambda i, h: (0, i)),
                  pl.BlockSpec((1, s, width), lambda i, h: (i, 0, steps + h)),
                  tok, vec, vec],
        out_specs=tok,
        out_shape=jax.ShapeDtypeStruct((b, s, ML_WP), BF16),
        scratch_shapes=[pltpu.VMEM((s, width), F32), pltpu.VMEM((s, width), F32),
                        pltpu.VMEM((ML_HPS, ML_DHP, ML_DHP), F32), pltpu.VMEM((ML_HPS, ML_DHP, ML_DHP), F32),
                        pltpu.VMEM((SUBLANES, LANES), F32)],
        compiler_params=_cparams("parallel", "parallel"),
        name="mlstm",
    )(q, kt, v, gcol3, grow, main3, xc, ng, sk)


def _pad_heads(a, axis):
    a = jnp.moveaxis(a, axis, -1)
    lead = a.shape[:-1]
    a = a.reshape(lead + (ML_HEADS, ML_DH))
    a = jnp.pad(a, [(0, 0)] * len(lead) + [(0, 0), (0, ML_DHP - ML_DH)])
    return jnp.moveaxis(a.reshape(lead + (ML_WP,)), -1, axis)


def kernel(x, mem, mem_ln_g, mem_ln_b, w_mem_kv, router_w, router_b, na_w_in, na_rpb, ml_w_in, ml_conv_w,
           ml_conv_b, ml_w_qkv, ml_gate_b, ml_norm_g, ml_skip, w_out, ln_g, ln_b, exp_w_gate, exp_w_up,
           exp_w_down):
    b, s, d = x.shape
    n = b * s
    nm = mem.shape[1]
    row = lambda a: a.reshape(1, -1)

    mem_k, mem_v = _memkv(mem.reshape(b * nm, d), row(mem_ln_g), row(mem_ln_b), w_mem_kv.astype(BF16))
    mem_k3 = mem_k.reshape(b, nm, MEM_W)
    mem_v3 = mem_v.reshape(b, nm, MEM_W)
    rw_pad = jnp.pad(router_w, ((0, 0), (0, LANES - N_EXPERTS)))
    rw_hi = rw_pad.astype(BF16)
    rw = (rw_hi, (rw_pad - rw_hi.astype(F32)).astype(BF16))
    rb = router_b.reshape(N_EXPERTS, 1)

    x2 = x.reshape(n, d)

    h0 = _proj(x2, na_w_in[0].astype(BF16)).reshape(b, s, 3 * NA_W + MEM_W)
    y_na = _na_attention(h0, _na_bias_table(na_rpb[0]))
    wo = w_out[0].astype(BF16)
    xr = _outproj_ln(y_na.reshape(n, NA_W), h0.reshape(n, 3 * NA_W + MEM_W), 3 * NA_W // MEM_W, mem_k3, mem_v3,
                     wo[:NA_W], wo[NA_W:], x2, row(ln_g[0, 0]), row(ln_b[0, 0]))
    x2 = _moe_ln(xr, rw, rb, exp_w_gate, exp_w_up, exp_w_down, 0, row(ln_g[0, 1]), row(ln_b[0, 1]))

    w1 = ml_w_in[0]
    w_main = jnp.concatenate([_pad_heads(w1[:, :ML_W], 1), _pad_heads(w1[:, ML_W:2 * ML_W], 1),
                              w1[:, 2 * ML_W + 4 * ML_HEADS:]], axis=1).astype(BF16)
    w_g = jnp.pad(w1[:, 2 * ML_W:2 * ML_W + 4 * ML_HEADS], ((0, 0), (0, LANES - 4 * ML_HEADS))).astype(BF16)
    gb = jnp.pad(ml_gate_b[0].reshape(4 * ML_HEADS), (0, LANES - 4 * ML_HEADS))
    main, acol, arow = _proj_gates(x2, w_main, w_g, w_g.T, gb.reshape(1, LANES), gb.reshape(LANES, 1))
    main3 = main.reshape(b, s, 2 * ML_WP + MEM_W)
    wqkv = jnp.pad(ml_w_qkv[0], ((0, 0), (0, 0), (0, ML_DHP - ML_DH), (0, ML_DHP - ML_DH))).astype(BF16)
    q, k, v, xc = _conv_qkv(main3, _pad_heads(ml_conv_w[0], 1), _pad_heads(row(ml_conv_b[0]), 1),
                            wqkv[0], jnp.swapaxes(wqkv[1], 1, 2), wqkv[2])
    y_ml = _mlstm(q, k, v, acol.reshape(b, s, LANES), arow, main3, xc,
                  _pad_heads(row(ml_norm_g[0]), 1), _pad_heads(row(ml_skip[0]), 1))
    wo = w_out[1]
    xr = _outproj_ln(y_ml.reshape(n, ML_WP), main, 2 * ML_WP // MEM_W, mem_k3, mem_v3,
                     _pad_heads(wo[:ML_W], 0).astype(BF16), wo[ML_W:].astype(BF16), x2,
                     row(ln_g[1, 0]), row(ln_b[1, 0]))
    x2 = _moe_ln(xr, rw, rb, exp_w_gate, exp_w_up, exp_w_down, 1, row(ln_g[1, 1]), row(ln_b[1, 1]))
    return x2.reshape(b, s, d)
```

```python
import functools

import numpy as np
import jax
import jax.numpy as jnp
from jax import lax
from jax.experimental import pallas as pl
from jax.experimental.pallas import tpu as pltpu

F32 = jnp.float32
BF16 = jnp.bfloat16
I32 = jnp.int32

D_MODEL = 1024
DEPTH = 2
GRID_W = 64
MEM_HEADS = 4
MEM_DH = 64
MEM_W = MEM_HEADS * MEM_DH
NA_HEADS = 12
NA_DH = 64
NA_W = NA_HEADS * NA_DH
WIN_H = 8
WIN_W = 16
ML_HEADS = 4
ML_DH = 192
ML_DHP = 256
ML_W = ML_HEADS * ML_DH
ML_WP = ML_HEADS * ML_DHP
CONV_K = 5
CHUNK = 256
N_EXPERTS = 16
N_GROUPS = 4
EXPERTS_PER_GROUP = N_EXPERTS // N_GROUPS
D_EXPERT = 512
ALPHA = (2 * DEPTH) ** 0.25
LN_EPS = 1e-5
NEG = -1e30

LANES = 128
SUBLANES = 8
ROW_CHUNKS = D_MODEL // LANES
MOE_BM = 512
MOE_TILE = 512
ML_HPS = 2
NA_ROWS_PER_STEP = 8
ROW_TILE = 512
PLACE_UNROLL = 8
VMEM_LIMIT = 48 * 1024 * 1024


def _cparams(*sem):
    return pltpu.CompilerParams(dimension_semantics=sem, vmem_limit_bytes=VMEM_LIMIT)


def _dot(a, b):
    return jnp.dot(a, b, preferred_element_type=F32)


def _dot_nt(a, b, precision=None):
    return lax.dot_general(a, b, (((1,), (1,)), ((), ())), precision=precision,
                           preferred_element_type=F32)


def _ln(z, g, b):
    mu = jnp.mean(z, axis=-1, keepdims=True)
    zc = z - mu
    var = jnp.mean(zc * zc, axis=-1, keepdims=True)
    return zc * lax.rsqrt(var + LN_EPS) * g + b


def _silu(x):
    return x * jax.nn.sigmoid(x)


def _read_rows(ref, n):
    return jnp.concatenate([ref[pl.ds(j, n, stride=ROW_CHUNKS), :] for j in range(ROW_CHUNKS)], axis=1)


def _write_rows(ref, val, n):
    for j in range(ROW_CHUNKS):
        ref[pl.ds(j, n, stride=ROW_CHUNKS), :] = val[:, j * LANES:(j + 1) * LANES]


def _memkv_kernel(m_ref, g_ref, b_ref, w_ref, k_ref, v_ref):
    z = _ln(m_ref[...], g_ref[...], b_ref[...])
    kv = _dot(z.astype(BF16), w_ref[...])
    k_ref[...] = kv[:, :MEM_W].astype(BF16)
    v_ref[...] = kv[:, MEM_W:].astype(BF16)


def _memkv(mem2, g, b, w):
    n = mem2.shape[0]
    tm = min(ROW_TILE, n)
    return pl.pallas_call(
        _memkv_kernel,
        grid=(n // tm,),
        in_specs=[pl.BlockSpec((tm, D_MODEL), lambda i: (i, 0)),
                  pl.BlockSpec((1, D_MODEL), lambda i: (0, 0)),
                  pl.BlockSpec((1, D_MODEL), lambda i: (0, 0)),
                  pl.BlockSpec((D_MODEL, 2 * MEM_W), lambda i: (0, 0))],
        out_specs=[pl.BlockSpec((tm, MEM_W), lambda i: (i, 0)),
                   pl.BlockSpec((tm, MEM_W), lambda i: (i, 0))],
        out_shape=[jax.ShapeDtypeStruct((n, MEM_W), BF16)] * 2,
        compiler_params=_cparams("parallel"),
        name="memkv",
    )(mem2, g, b, w)


def _proj_kernel(x_ref, w_ref, o_ref):
    o_ref[...] = _dot(x_ref[...].astype(BF16), w_ref[...]).astype(o_ref.dtype)


def _proj(x2, w, tm=ROW_TILE):
    n, k = x2.shape
    nout = w.shape[1]
    return pl.pallas_call(
        _proj_kernel,
        grid=(n // tm,),
        in_specs=[pl.BlockSpec((tm, k), lambda i: (i, 0)),
                  pl.BlockSpec((k, nout), lambda i: (0, 0))],
        out_specs=pl.BlockSpec((tm, nout), lambda i: (i, 0)),
        out_shape=jax.ShapeDtypeStruct((n, nout), BF16),
        compiler_params=_cparams("parallel"),
        name="in_proj",
    )(x2, w)


def _split3(x):
    hi = x.astype(BF16)
    r1 = x - hi.astype(F32)
    mid = r1.astype(BF16)
    lo = (r1 - mid.astype(F32)).astype(BF16)
    return hi, mid, lo


def _proj_gates_kernel(x_ref, w_ref, wg_ref, wgt_ref, gbc_ref, gbr_ref, o_ref, g_ref, gt_ref, *, tm):
    xb = x_ref[...].astype(BF16)
    o_ref[...] = _dot(xb, w_ref[...]).astype(BF16)
    gcol = _dot(xb, wg_ref[...]) + gbc_ref[...]
    grow = _dot_nt(wgt_ref[...], xb) + gbr_ref[...]
    lane = lax.broadcasted_iota(I32, (1, LANES), 1)
    sub = lax.broadcasted_iota(I32, (LANES, 1), 0)
    ti = lax.broadcasted_iota(I32, (CHUNK, CHUNK), 0)
    tj = lax.broadcasted_iota(I32, (CHUNK, CHUNK), 1)
    lower = (tj <= ti).astype(BF16)
    upper = (ti <= tj).astype(BF16)

    def pick(idx, pre, suf, raw):
        fwd = jnp.logical_and(idx >= ML_HEADS, idx < 2 * ML_HEADS)
        bwd = jnp.logical_and(idx >= 3 * ML_HEADS, idx < 4 * ML_HEADS)
        return jnp.where(fwd, pre, jnp.where(bwd, suf, raw))

    for c in range(tm // CHUNK):
        tc = slice(c * CHUNK, (c + 1) * CHUNK)
        g = gcol[tc, :]
        ls = jax.nn.log_sigmoid(g)
        pre = sum(_dot(lower, part) for part in _split3(ls))
        suf = jnp.sum(ls, axis=0, keepdims=True) - pre + ls
        g_ref[tc, :] = pick(lane, pre, suf, g)
        g = grow[:, tc]
        ls = jax.nn.log_sigmoid(g)
        pre = sum(_dot(part, upper) for part in _split3(ls))
        suf = jnp.sum(ls, axis=1, keepdims=True) - pre + ls
        gt_ref[:, tc] = pick(sub, pre, suf, g)


def _proj_gates(x2, w, wg, wgt, gbc, gbr, tm=ROW_TILE):
    n, k = x2.shape
    nout = w.shape[1]
    return pl.pallas_call(
        functools.partial(_proj_gates_kernel, tm=tm),
        grid=(n // tm,),
        in_specs=[pl.BlockSpec((tm, k), lambda i: (i, 0)),
                  pl.BlockSpec((k, nout), lambda i: (0, 0)),
                  pl.BlockSpec((k, LANES), lambda i: (0, 0)),
                  pl.BlockSpec((LANES, k), lambda i: (0, 0)),
                  pl.BlockSpec((1, LANES), lambda i: (0, 0)),
                  pl.BlockSpec((LANES, 1), lambda i: (0, 0))],
        out_specs=[pl.BlockSpec((tm, nout), lambda i: (i, 0)),
                   pl.BlockSpec((tm, LANES), lambda i: (i, 0)),
                   pl.BlockSpec((LANES, tm), lambda i: (0, i))],
        out_shape=[jax.ShapeDtypeStruct((n, nout), BF16),
                   jax.ShapeDtypeStruct((n, LANES), F32),
                   jax.ShapeDtypeStruct((LANES, n), F32)],
        compiler_params=_cparams("parallel"),
        name="in_proj_gates",
    )(x2, w, wg, wgt, gbc, gbr)


def _na_kernel(q_ref, k_ref, v_ref, tbl_ref, o_ref, *, rows):
    lane = lax.broadcasted_iota(I32, (1, LANES), 1)
    first = lane < NA_DH
    nkeys = WIN_H * GRID_W

    def rows_step(i, carry):
        rr = [i * NA_ROWS_PER_STEP + u for u in range(NA_ROWS_PER_STEP)]
        rss = [jnp.clip(r - WIN_H // 2, 0, rows - WIN_H) for r in rr]
        scores = []
        for r, rs in zip(rr, rss):
            q = q_ref[0, pl.ds(pl.multiple_of(r * GRID_W, GRID_W), GRID_W), :]
            q = q * jnp.asarray(NA_DH ** -0.5, BF16)
            q2 = jnp.concatenate([jnp.where(first, q, jnp.zeros_like(q)),
                                  jnp.where(first, jnp.zeros_like(q), q)], axis=0)
            k = k_ref[0, pl.ds(pl.multiple_of(rs * GRID_W, GRID_W), nkeys), :]
            dr0 = rs - r + WIN_H - 1
            bias = jnp.concatenate(
                [jnp.concatenate([tbl_ref[0, half, dr0 + 2 * m] for m in range(WIN_H // 2)], axis=1)
                 for half in range(2)], axis=0)
            scores.append(_dot_nt(q2, k) + bias)
        probs = []
        for s in scores:
            p = jnp.exp(s - jnp.max(s, axis=-1, keepdims=True))
            probs.append((p.astype(BF16), jnp.sum(p, axis=-1, keepdims=True)))
        for r, rs, (p, l) in zip(rr, rss, probs):
            v = v_ref[0, pl.ds(pl.multiple_of(rs * GRID_W, GRID_W), nkeys), :]
            o = _dot(p, v) / l
            o = jnp.where(first, o[:GRID_W], o[GRID_W:])
            o_ref[0, pl.ds(pl.multiple_of(r * GRID_W, GRID_W), GRID_W), :] = o.astype(o_ref.dtype)
        return carry

    lax.fori_loop(0, rows // NA_ROWS_PER_STEP, rows_step, 0)


def _na_bias_table(rpb):
    qc = np.arange(GRID_W)[:, None]
    kc = np.arange(GRID_W)[None, :]
    cs = np.clip(qc - WIN_W // 2, 0, GRID_W - WIN_W)
    col_in = (kc >= cs) & (kc < cs + WIN_W)
    side = GRID_W - WIN_W
    wide = jnp.pad(rpb, ((0, 0), (0, 0), (side, side)))
    t = jnp.stack([wide[:, :, GRID_W - 1 - q:2 * GRID_W - 1 - q] for q in range(GRID_W)], axis=2)
    t = jnp.where(col_in, t, NEG).astype(F32)
    t2 = jnp.concatenate([t[:, :-1], t[:, 1:]], axis=-1)
    return t2.reshape(NA_HEADS // 2, 2, 2 * WIN_H - 2, GRID_W, 2 * GRID_W)


def _na_attention(h3, tbl):
    b, s, _ = h3.shape
    rows = s // GRID_W
    npair = NA_HEADS // 2
    return pl.pallas_call(
        functools.partial(_na_kernel, rows=rows),
        grid=(b, npair),
        in_specs=[pl.BlockSpec((1, s, LANES), lambda i, p: (i, 0, p)),
                  pl.BlockSpec((1, s, LANES), lambda i, p: (i, 0, npair + p)),
                  pl.BlockSpec((1, s, LANES), lambda i, p: (i, 0, 2 * npair + p)),
                  pl.BlockSpec((1, 2, 2 * WIN_H - 2, GRID_W, 2 * GRID_W), lambda i, p: (p, 0, 0, 0, 0))],
        out_specs=pl.BlockSpec((1, s, LANES), lambda i, p: (i, 0, p)),
        out_shape=jax.ShapeDtypeStruct((b, s, NA_W), BF16),
        compiler_params=_cparams("parallel", "parallel"),
        name="na_attention",
    )(h3, h3, h3, tbl)


def _outproj_ln_kernel(ya_ref, qm_ref, mk_ref, mv_ref, wa_ref, wm_ref, x_ref, g_ref, b_ref, or_ref, *, tm):
    lane = lax.broadcasted_iota(I32, (1, LANES), 1)
    first = lane < MEM_DH
    q = qm_ref[...] * jnp.asarray(MEM_DH ** -0.5, BF16)
    cols = [slice(p * LANES, (p + 1) * LANES) for p in range(MEM_HEADS // 2)]
    scores = []
    for c in cols:
        qp = q[:, c]
        q2 = jnp.concatenate([jnp.where(first, qp, jnp.zeros_like(qp)),
                              jnp.where(first, jnp.zeros_like(qp), qp)], axis=0)
        scores.append(_dot_nt(q2, mk_ref[0, :, c]))
    probs = []
    for s in scores:
        p = jnp.exp(s - jnp.max(s, axis=-1, keepdims=True))
        probs.append((p.astype(BF16), jnp.sum(p, axis=-1, keepdims=True)))
    outs = []
    for c, (p, l) in zip(cols, probs):
        o = _dot(p, mv_ref[0, :, c]) / l
        outs.append(jnp.where(first, o[:tm], o[tm:]))
    ym = jnp.concatenate(outs, axis=1).astype(BF16)
    acc = _dot(ya_ref[...], wa_ref[...]) + _dot(ym, wm_ref[...])
    _write_rows(or_ref, _ln(ALPHA * x_ref[...] + acc, g_ref[...], b_ref[...]), tm)


def _outproj_ln(ya, h2, qm_block, mem_k3, mem_v3, wa, wm, x2, g, b, tm=ROW_TILE):
    n = x2.shape[0]
    ka = ya.shape[1]
    nb, nm, _ = mem_k3.shape
    per_batch = n // nb // tm
    full = lambda shape: pl.BlockSpec(shape, lambda i: (0,) * len(shape))
    return pl.pallas_call(
        functools.partial(_outproj_ln_kernel, tm=tm),
        grid=(n // tm,),
        in_specs=[pl.BlockSpec((tm, ka), lambda i: (i, 0)),
                  pl.BlockSpec((tm, MEM_W), lambda i: (i, qm_block)),
                  pl.BlockSpec((1, nm, MEM_W), lambda i: (i // per_batch, 0, 0)),
                  pl.BlockSpec((1, nm, MEM_W), lambda i: (i // per_batch, 0, 0)),
                  full((ka, D_MODEL)), full((MEM_W, D_MODEL)),
                  pl.BlockSpec((tm, D_MODEL), lambda i: (i, 0)),
                  full((1, D_MODEL)), full((1, D_MODEL))],
        out_specs=pl.BlockSpec((tm * ROW_CHUNKS, LANES), lambda i: (i, 0)),
        out_shape=jax.ShapeDtypeStruct((n * ROW_CHUNKS, LANES), F32),
        compiler_params=_cparams("parallel"),
        name="outproj_ln",
    )(ya, h2, mem_k3, mem_v3, wa, wm, x2, g, b)


def _router_kernel(x_ref, rwh_ref, rwl_ref, rb_ref, lpos_ref, w_ref, cnt_ref, tcnt_ref, toff_ref, tbef_ref, *, tm):
    @pl.when(pl.program_id(0) == 0)
    def _():
        cnt_ref[...] = jnp.zeros_like(cnt_ref)

    x = _read_rows(x_ref, tm)
    xh = x.astype(BF16)
    xl = (x - xh.astype(F32)).astype(BF16)
    logits_t = _dot(xh, rwh_ref[...]) + (_dot(xh, rwl_ref[...]) + _dot(xl, rwh_ref[...]))
    logits = logits_t.T[:N_EXPERTS]
    scores = jax.nn.sigmoid(logits)
    biased = scores + rb_ref[...]
    bv = [biased[e:e + 1, :] for e in range(N_EXPERTS)]
    sv = [scores[e:e + 1, :] for e in range(N_EXPERTS)]

    grp = []
    for g in range(N_GROUPS):
        m = bv[g * EXPERTS_PER_GROUP:(g + 1) * EXPERTS_PER_GROUP]
        best = None
        for a in range(EXPERTS_PER_GROUP):
            for c in range(a + 1, EXPERTS_PER_GROUP):
                pair = m[a] + m[c]
                best = pair if best is None else jnp.maximum(best, pair)
        grp.append(best)
    gsel = jnp.zeros((1, tm), I32)
    gbest = grp[0]
    for g in range(1, N_GROUPS):
        better = grp[g] > gbest
        gsel = jnp.where(better, g, gsel)
        gbest = jnp.where(better, grp[g], gbest)

    def pick(vals, j):
        out = vals[j]
        for g in range(1, N_GROUPS):
            out = jnp.where(gsel == g, vals[g * EXPERTS_PER_GROUP + j], out)
        return out

    cb = [pick(bv, j) for j in range(EXPERTS_PER_GROUP)]
    cs = [pick(sv, j) for j in range(EXPERTS_PER_GROUP)]
    i1 = jnp.zeros((1, tm), I32)
    m1 = cb[0]
    s1 = cs[0]
    for j in range(1, EXPERTS_PER_GROUP):
        gt = cb[j] > m1
        i1 = jnp.where(gt, j, i1)
        m1 = jnp.where(gt, cb[j], m1)
        s1 = jnp.where(gt, cs[j], s1)
    i2 = jnp.zeros((1, tm), I32)
    m2 = jnp.full((1, tm), -jnp.inf, F32)
    s2 = jnp.zeros((1, tm), F32)
    for j in range(EXPERTS_PER_GROUP):
        ok = jnp.logical_and(i1 != j, cb[j] > m2)
        i2 = jnp.where(ok, j, i2)
        m2 = jnp.where(ok, cb[j], m2)
        s2 = jnp.where(ok, cs[j], s2)
    e1 = gsel * EXPERTS_PER_GROUP + i1
    e2 = gsel * EXPERTS_PER_GROUP + i2
    tot = s1 + s2
    w_ref[...] = jnp.concatenate([s1 / tot, s2 / tot], axis=0)

    i = pl.program_id(0)
    eio = lax.broadcasted_iota(I32, (N_EXPERTS, tm), 0)
    oh1 = eio == e1
    oh2 = eio == e2
    ohs = jnp.logical_or(oh1, oh2).astype(F32)
    before = (lax.broadcasted_iota(I32, (tm, tm), 0) < lax.broadcasted_iota(I32, (tm, tm), 1))
    pre = _dot(ohs.astype(BF16), before.astype(BF16))
    tile_cnt = jnp.sum(ohs, axis=1, keepdims=True)
    offs = []
    acc = jnp.zeros((1, 1), F32)
    for e in range(N_EXPERTS):
        offs.append(acc)
        acc = acc + tile_cnt[e:e + 1, :]
    tile_off = jnp.concatenate(offs, axis=0)
    pos = tile_off + pre
    p1 = jnp.sum(jnp.where(oh1, pos, 0.0), axis=0, keepdims=True)
    p2 = jnp.sum(jnp.where(oh2, pos, 0.0), axis=0, keepdims=True)
    lpos_ref[...] = jnp.concatenate([p1, p2], axis=0).astype(I32) * ROW_CHUNKS

    @pl.when(i == 0)
    def _():
        for ref in (tcnt_ref, toff_ref, tbef_ref):
            ref[...] = jnp.zeros_like(ref)

    here = lax.broadcasted_iota(I32, (1, LANES), 1) == i
    tcnt_ref[...] = jnp.where(here, tile_cnt, tcnt_ref[...])
    toff_ref[...] = jnp.where(here, tile_off, toff_ref[...])
    tbef_ref[...] = jnp.where(here, cnt_ref[:, 0:1], tbef_ref[...])
    cnt_ref[...] += tile_cnt


def _router(xr, rw, rb, tm):
    n = xr.shape[0] // ROW_CHUNKS
    assert n // tm <= LANES
    table = pl.BlockSpec((N_EXPERTS, LANES), lambda i: (0, 0))
    return pl.pallas_call(
        functools.partial(_router_kernel, tm=tm),
        grid=(n // tm,),
        in_specs=[pl.BlockSpec((tm * ROW_CHUNKS, LANES), lambda i: (i, 0)),
                  pl.BlockSpec((D_MODEL, LANES), lambda i: (0, 0)),
                  pl.BlockSpec((D_MODEL, LANES), lambda i: (0, 0)),
                  pl.BlockSpec((N_EXPERTS, 1), lambda i: (0, 0))],
        out_specs=[pl.BlockSpec((2, tm), lambda i: (0, i)),
                   pl.BlockSpec((2, tm), lambda i: (0, i)),
                   table, table, table, table],
        out_shape=[jax.ShapeDtypeStruct((2, n), I32),
                   jax.ShapeDtypeStruct((2, n), F32)]
                  + [jax.ShapeDtypeStruct((N_EXPERTS, LANES), F32)] * 4,
        compiler_params=_cparams("arbitrary"),
        name="router",
    )(xr, rw[0], rw[1], rb)


def _plan_kernel(cnt_ref, tbef_ref, meta_ref, rstart_ref, *, nbl):
    shift = MOE_BM.bit_length() - 1
    cnt = cnt_ref[...].astype(I32)
    padded = ((cnt + (MOE_BM - 1)) >> shift) << shift
    starts = []
    acc = jnp.zeros((1, LANES), I32)
    for e in range(N_EXPERTS):
        starts.append(acc)
        acc = acc + padded[e:e + 1, :]
    pad_start = jnp.concatenate(starts, axis=0)
    pad_end = pad_start + padded
    rstart_ref[...] = pad_start + tbef_ref[...].astype(I32)
    blk0 = lax.broadcasted_iota(I32, (N_EXPERTS, nbl), 1) * MOE_BM
    block_e = jnp.sum((pad_end[:, 0:1] <= blk0).astype(I32), axis=0, keepdims=True)
    block_e = jnp.minimum(block_e, N_EXPERTS - 1)
    n_used = jnp.broadcast_to(acc[:, 0:1] >> shift, (1, nbl))
    diag = lax.broadcasted_iota(I32, (N_EXPERTS, nbl), 0) == lax.broadcasted_iota(I32, (N_EXPERTS, nbl), 1)
    fill_lo = jnp.sum(jnp.where(diag, (pad_start + cnt)[:, 0:1], 0), axis=0, keepdims=True)
    fill_hi = jnp.sum(jnp.where(diag, pad_end[:, 0:1], 0), axis=0, keepdims=True)
    meta_ref[...] = jnp.concatenate([block_e, n_used, fill_lo, fill_hi, jnp.zeros((SUBLANES - 4, nbl), I32)],
                                    axis=0)


def _plan(cnt, tbef, n_blocks):
    nbl = -(-n_blocks // LANES) * LANES
    table = pl.BlockSpec((N_EXPERTS, LANES), lambda i: (0, 0))
    return pl.pallas_call(
        functools.partial(_plan_kernel, nbl=nbl),
        grid=(1,),
        in_specs=[table, table],
        out_specs=[pl.BlockSpec((SUBLANES, nbl), lambda i: (0, 0)), table],
        out_shape=[jax.ShapeDtypeStruct((SUBLANES, nbl), I32),
                   jax.ShapeDtypeStruct((N_EXPERTS, LANES), I32)],
        compiler_params=_cparams("arbitrary"),
        name="moe_plan",
    )(cnt, tbef)


def _rows(ref, row, nrows):
    return ref.at[pl.ds(pl.multiple_of(row * ROW_CHUNKS, ROW_CHUNKS), nrows * ROW_CHUNKS), :]


def _rows_wait(src_hbm, buf, sem):
    pltpu.make_async_copy(src_hbm.at[pl.ds(0, buf.shape[0]), :], buf, sem).wait()


def _copy_pieces(src, src_row, dst, dst_row, count, max_rows, sem, wait=False):
    bit = max_rows.bit_length() - 1
    while bit >= 0:
        size = 1 << bit
        done = (count >> (bit + 1)) << (bit + 1)

        @pl.when(((count >> bit) & 1) == 1)
        def _():
            cp = pltpu.make_async_copy(_rows(src, src_row + done, size), _rows(dst, dst_row + done, size), sem)
            cp.start()
            if wait:
                cp.wait()

        bit -= 1


def _tile_runs(tcnt_ref, toff_ref, rstart_ref, tile, buf, hbm, sem, *, to_hbm, tm):
    def per_expert(e, carry):
        k = tile * N_EXPERTS + e
        if to_hbm:
            _copy_pieces(buf, toff_ref[k], hbm, rstart_ref[k], tcnt_ref[k], tm, sem)
        else:
            _copy_pieces(hbm, rstart_ref[k], buf, toff_ref[k], tcnt_ref[k], tm, sem)
        return carry

    lax.fori_loop(0, N_EXPERTS, per_expert, 0)


def _dispatch_kernel(lpos_ref, tcnt_ref, toff_ref, rstart_ref, flo_ref, fhi_ref, nu_ref, x_ref, xs_hbm,
                     s0, s1, zbuf, sem, zsem, *, n, tm, n_blocks):
    i = pl.program_id(0)
    nt = pl.num_programs(0)
    bufs = (s0, s1)
    unroll = PLACE_UNROLL

    for slot in range(2):
        @pl.when(i % 2 == slot)
        def _():
            buf = bufs[slot]

            @pl.when(i >= 2)
            def _():
                _rows_wait(xs_hbm, buf, sem.at[slot])

            def place(c, carry):
                tok = i * tm + c * unroll
                src = pl.multiple_of(c * (unroll * ROW_CHUNKS), unroll * ROW_CHUNKS)
                for u in range(unroll):
                    v = x_ref[pl.ds(src + u * ROW_CHUNKS, ROW_CHUNKS), :]
                    for k in range(2):
                        p = lpos_ref[k * n + tok + u]
                        buf[pl.ds(pl.multiple_of(p, ROW_CHUNKS), ROW_CHUNKS), :] = v
                return carry

            lax.fori_loop(0, tm // unroll, place, 0)
            _tile_runs(tcnt_ref, toff_ref, rstart_ref, i, buf, xs_hbm, sem.at[slot], to_hbm=True, tm=tm)

    @pl.when(i == nt - 1)
    def _():
        for slot in range(2):
            @pl.when(nt > slot)
            def _():
                _rows_wait(xs_hbm, bufs[slot], sem.at[slot])

        zbuf[...] = jnp.zeros_like(zbuf)
        for e in range(N_EXPERTS):
            _copy_pieces(zbuf, 0, xs_hbm, flo_ref[e], fhi_ref[e] - flo_ref[e], MOE_BM // 2, zsem, wait=True)

        def zero_block(j, carry):
            cp = pltpu.make_async_copy(zbuf, _rows(xs_hbm, j * MOE_BM, MOE_BM), zsem)
            cp.start()
            cp.wait()
            return carry

        lax.fori_loop(nu_ref[0], n_blocks, zero_block, 0)


def _dispatch(lpos_flat, tcnt, toff, rstart, fill_lo, fill_hi, n_used, xr, n_blocks, tm):
    n = xr.shape[0] // ROW_CHUNKS
    return pl.pallas_call(
        functools.partial(_dispatch_kernel, n=n, tm=tm, n_blocks=n_blocks),
        grid_spec=pltpu.PrefetchScalarGridSpec(
            num_scalar_prefetch=7,
            grid=(n // tm,),
            in_specs=[pl.BlockSpec((tm * ROW_CHUNKS, LANES), lambda i, *_: (i, 0))],
            out_specs=pl.BlockSpec(memory_space=pl.ANY),
            scratch_shapes=[pltpu.VMEM((2 * tm * ROW_CHUNKS, LANES), F32),
                            pltpu.VMEM((2 * tm * ROW_CHUNKS, LANES), F32),
                            pltpu.VMEM((MOE_BM * ROW_CHUNKS, LANES), F32),
                            pltpu.SemaphoreType.DMA((2,)),
                            pltpu.SemaphoreType.DMA(())]),
        out_shape=jax.ShapeDtypeStruct((n_blocks * MOE_BM * ROW_CHUNKS, LANES), F32),
        compiler_params=_cparams("arbitrary"),
        name="moe_dispatch",
    )(lpos_flat, tcnt, toff, rstart, fill_lo, fill_hi, n_used, xr)


def _experts_kernel(be_ref, nu_ref, xs_ref, wg_ref, wu_ref, wd_ref, y_ref, wgb, wub, wdb):
    j = pl.program_id(0)
    used = j < nu_ref[0]

    @pl.when(jnp.logical_and(used, jnp.logical_or(j == 0, be_ref[j] != be_ref[jnp.maximum(j - 1, 0)])))
    def _():
        wgb[...] = wg_ref[0, 0].astype(BF16)
        wub[...] = wu_ref[0, 0].astype(BF16)
        wdb[...] = wd_ref[0, 0].astype(BF16)

    @pl.when(used)
    def _():
        x = _read_rows(xs_ref, MOE_BM).astype(BF16)
        h = _silu(_dot(x, wgb[...])) * _dot(x, wub[...])
        _write_rows(y_ref, _dot(h.astype(BF16), wdb[...]), MOE_BM)

    @pl.when(jnp.logical_not(used))
    def _():
        y_ref[...] = jnp.zeros_like(y_ref)


def _experts(block_e, n_used, xs, wg, wu, wd, layer):
    n_blocks = block_e.shape[0]

    def last_used(j, nu):
        return jnp.minimum(j, nu[0] - 1)

    def wblk(j, be, nu):
        return (layer, be[last_used(j, nu)], 0, 0)

    return pl.pallas_call(
        _experts_kernel,
        grid_spec=pltpu.PrefetchScalarGridSpec(
            num_scalar_prefetch=2,
            grid=(n_blocks,),
            in_specs=[pl.BlockSpec((MOE_BM * ROW_CHUNKS, LANES), lambda j, be, nu: (last_used(j, nu), 0)),
                      pl.BlockSpec((1, 1, D_MODEL, D_EXPERT), wblk),
                      pl.BlockSpec((1, 1, D_MODEL, D_EXPERT), wblk),
                      pl.BlockSpec((1, 1, D_EXPERT, D_MODEL), wblk)],
            out_specs=pl.BlockSpec((MOE_BM * ROW_CHUNKS, LANES), lambda j, be, nu: (j, 0)),
            scratch_shapes=[pltpu.VMEM((D_MODEL, D_EXPERT), BF16), pltpu.VMEM((D_MODEL, D_EXPERT), BF16),
                            pltpu.VMEM((D_EXPERT, D_MODEL), BF16)]),
        out_shape=jax.ShapeDtypeStruct(xs.shape, F32),
        compiler_params=_cparams("arbitrary"),
        name="moe_experts",
    )(block_e, n_used, xs, wg, wu, wd)


def _combine_ln_kernel(lpos_ref, tcnt_ref, toff_ref, rstart_ref, y_hbm, x_ref, w1_ref, w2_ref, g_ref, b_ref, o_ref,
                       r0, r1, u1, u2, sem, *, n, tm):
    i = pl.program_id(0)
    nt = pl.num_programs(0)
    bufs = (r0, r1)
    unroll = PLACE_UNROLL

    def fetch(tile, slot):
        _tile_runs(tcnt_ref, toff_ref, rstart_ref, tile, bufs[slot], y_hbm, sem.at[slot], to_hbm=False, tm=tm)

    @pl.when(i == 0)
    def _():
        fetch(0, 0)

    for slot in range(2):
        @pl.when(i % 2 == slot)
        def _():
            @pl.when(i + 1 < nt)
            def _():
                fetch(i + 1, 1 - slot)

            buf = bufs[slot]
            _rows_wait(y_hbm, buf, sem.at[slot])

            def place(c, carry):
                tok = i * tm + c * unroll
                dst0 = pl.multiple_of(c * (unroll * ROW_CHUNKS), unroll * ROW_CHUNKS)
                for u in range(unroll):
                    dst = pl.ds(dst0 + u * ROW_CHUNKS, ROW_CHUNKS)
                    for k, out in enumerate((u1, u2)):
                        p = lpos_ref[k * n + tok + u]
                        out[dst, :] = buf[pl.ds(pl.multiple_of(p, ROW_CHUNKS), ROW_CHUNKS), :]
                return carry

            lax.fori_loop(0, tm // unroll, place, 0)
            moe = w1_ref[...] * _read_rows(u1, tm) + w2_ref[...] * _read_rows(u2, tm)
            o_ref[...] = _ln(ALPHA * _read_rows(x_ref, tm) + moe, g_ref[...], b_ref[...])


def _combine_ln(lpos_flat, tcnt, toff, rstart, y, xr, w1, w2, g, b, tm):
    n = xr.shape[0] // ROW_CHUNKS
    return pl.pallas_call(
        functools.partial(_combine_ln_kernel, n=n, tm=tm),
        grid_spec=pltpu.PrefetchScalarGridSpec(
            num_scalar_prefetch=4,
            grid=(n // tm,),
            in_specs=[pl.BlockSpec(memory_space=pl.ANY),
                      pl.BlockSpec((tm * ROW_CHUNKS, LANES), lambda i, *_: (i, 0)),
                      pl.BlockSpec((tm, 1), lambda i, *_: (i, 0)),
                      pl.BlockSpec((tm, 1), lambda i, *_: (i, 0)),
                      pl.BlockSpec((1, D_MODEL), lambda i, *_: (0, 0)),
                      pl.BlockSpec((1, D_MODEL), lambda i, *_: (0, 0))],
            out_specs=pl.BlockSpec((tm, D_MODEL), lambda i, *_: (i, 0)),
            scratch_shapes=[pltpu.VMEM((2 * tm * ROW_CHUNKS, LANES), F32)] * 2
                           + [pltpu.VMEM((tm * ROW_CHUNKS, LANES), F32)] * 2
                           + [pltpu.SemaphoreType.DMA((2,))]),
        out_shape=jax.ShapeDtypeStruct((n, D_MODEL), F32),
        compiler_params=_cparams("arbitrary"),
        name="moe_combine_ln",
    )(lpos_flat, tcnt, toff, rstart, y, xr, w1, w2, g, b)


def _moe_ln(xr, rw, rb, wg, wu, wd, layer, g, b):
    n = xr.shape[0] // ROW_CHUNKS
    n_blocks = (2 * n) // MOE_BM + N_EXPERTS
    tm = MOE_TILE
    nt = n // tm
    lpos, w, cnt, tcnt, toff, tbef = _router(xr, rw, rb, tm)
    meta, rstart = _plan(cnt, tbef, n_blocks)
    block_e = meta[0, :n_blocks]
    n_used = meta[1, :1]

    def per_tile(table):
        return table[:, :nt].T.reshape(nt * N_EXPERTS).astype(I32)

    lpos_flat = lpos.reshape(2 * n)
    tcnt, toff, rstart = per_tile(tcnt), per_tile(toff), per_tile(rstart)
    xs = _dispatch(lpos_flat, tcnt, toff, rstart, meta[2, :N_EXPERTS], meta[3, :N_EXPERTS], n_used, xr,
                   n_blocks, tm)
    y = _experts(block_e, n_used, xs, wg, wu, wd, layer)
    return _combine_ln(lpos_flat, tcnt, toff, rstart, y, xr, w[0].reshape(n, 1), w[1].reshape(n, 1), g, b, tm)


def _conv_qkv_kernel(xm_ref, cw_ref, cb_ref, wq_ref, wk_ref, wv_ref, q_ref, k_ref, v_ref, xc_ref, *, s):
    xm_b = xm_ref[0]
    xm = xm_b.astype(F32)
    cw = cw_ref[...]
    row = lax.broadcasted_iota(I32, (s, 1), 0)
    half = CONV_K // 2
    acc = cb_ref[...] + xm * cw[half:half + 1, :]
    for sh in range(1, half + 1):
        past = jnp.where(row >= sh, pltpu.roll(xm, sh, axis=0), 0.0)
        acc = acc + past * cw[half - sh:half - sh + 1, :]
        nxt = jnp.where(row < s - sh, pltpu.roll(xm, s - sh, axis=0), 0.0)
        acc = acc + nxt * cw[half + sh:half + sh + 1, :]
    xc = _silu(acc).astype(BF16)
    xc_ref[0] = xc
    q_ref[0] = _dot(xc, wq_ref[0]).astype(BF16)
    k_ref[0] = (_dot_nt(wk_ref[0], xc) * (ML_DH ** -0.5)).astype(BF16)
    v = _dot(xm_b, wv_ref[0])
    ones_lane = lax.broadcasted_iota(I32, (1, ML_DHP), 1) == ML_DH
    v_ref[0] = jnp.where(ones_lane, 1.0, v).astype(BF16)


def _conv_qkv(main3, cw, cb, wq, wk_t, wv):
    b, s, _ = main3.shape
    tok = pl.BlockSpec((1, s, ML_DHP), lambda i, h: (i, 0, h))
    wspec = pl.BlockSpec((1, ML_DHP, ML_DHP), lambda i, h: (h, 0, 0))
    tok_shape = jax.ShapeDtypeStruct((b, s, ML_WP), BF16)
    return pl.pallas_call(
        functools.partial(_conv_qkv_kernel, s=s),
        grid=(b, ML_HEADS),
        in_specs=[tok,
                  pl.BlockSpec((CONV_K, ML_DHP), lambda i, h: (0, h)),
                  pl.BlockSpec((1, ML_DHP), lambda i, h: (0, h)),
                  wspec, wspec, wspec],
        out_specs=[tok, pl.BlockSpec((1, ML_DHP, s), lambda i, h: (i, h, 0)), tok, tok],
        out_shape=[tok_shape, jax.ShapeDtypeStruct((b, ML_WP, s), BF16), tok_shape, tok_shape],
        compiler_params=_cparams("parallel", "parallel"),
        name="conv_qkv",
    )(main3, cw, cb, wq, wk_t, wv)


def _mlstm_kernel(q_ref, kt_ref, v_ref, gc_ref, gr_ref, z_ref, xc_ref, ng_ref, sk_ref,
                  y_ref, hf_ref, hb_ref, cf_ref, cb_ref, m_ref, *, s):
    head0 = pl.program_id(1) * ML_HPS
    nc = s // CHUNK
    sub = lax.broadcasted_iota(I32, (LANES, 1), 0)
    gate = lax.broadcasted_iota(I32, (LANES, LANES), 0)
    ti = lax.broadcasted_iota(I32, (CHUNK, CHUNK), 0)
    tj = lax.broadcasted_iota(I32, (CHUNK, CHUNK), 1)

    for ref in (cf_ref, cb_ref, m_ref):
        ref[...] = jnp.zeros_like(ref)

    def intra(c, j, rev):
        t0 = pl.multiple_of(c * CHUNK, CHUNK)
        hl = slice(j * ML_DHP, (j + 1) * ML_DHP)
        qb = q_ref[0, pl.ds(t0, CHUNK), hl]
        kt = kt_ref[0, hl, pl.ds(t0, CHUNK)]
        vb = v_ref[0, pl.ds(t0, CHUNK), hl]
        gc = gc_ref[0, pl.ds(t0, CHUNK), :]
        gr = gr_ref[:, pl.ds(t0, CHUNK)]
        i_idx = head0 + j + (2 * ML_HEADS if rev else 0)
        f_idx = i_idx + ML_HEADS
        allowed = (tj >= ti) if rev else (tj <= ti)
        sel = (gate == f_idx).astype(BF16)
        b_rep = sum(_dot(part, sel) for part in _split3(gc))
        b_row = jnp.sum(jnp.where(sub == f_idx, gr, 0.0), axis=0, keepdims=True)
        i_row = jnp.sum(jnp.where(sub == i_idx, gr, 0.0), axis=0, keepdims=True)
        b_last = (b_rep[0:1, :] if rev else b_rep[CHUNK - 1:CHUNK, :])[:, 0:1]

        b_wide = jnp.concatenate([b_rep] * (CHUNK // LANES), axis=1)
        d = jnp.where(allowed, b_wide - b_row + i_row, NEG)
        m_in = jnp.max(d, axis=1, keepdims=True)
        sc = _dot(qb, kt) * jnp.exp(d - m_in)
        nd_in = _dot(sc.astype(BF16), vb)
        w_row = b_last - b_row + i_row
        return t0, qb, kt, vb, b_rep, b_last, m_in, nd_in, w_row

    def twice(a):
        return jnp.concatenate([a, a], axis=1)

    def update(parts, j, rev):
        t0, qb, kt, vb, b_rep, b_last, m_in, nd_in, w_row = parts
        h_ref, c_ref = (hb_ref, cb_ref) if rev else (hf_ref, cf_ref)
        hl = slice(j * ML_DHP, (j + 1) * ML_DHP)
        mrow = 2 * j + int(rev)
        m = m_ref[mrow:mrow + 1, 0:1]
        cmat = c_ref[j]
        inter = b_rep + m
        m_t = jnp.maximum(m_in, inter)
        a_in = jnp.exp(m_in - m_t)
        iexp = jnp.exp(inter - m_t)
        nd = twice(a_in) * nd_in + twice(iexp) * _dot(qb, cmat.astype(BF16))
        den = nd[:, ML_DH:ML_DH + 1]
        h_ref[pl.ds(t0, CHUNK), hl] = nd * (1.0 / jnp.maximum(jnp.abs(den), jnp.exp(-m_t[:, 0:1])))

        m_new = jnp.maximum(b_last + m, jnp.max(w_row, axis=1, keepdims=True))
        wexp = jnp.exp(w_row - m_new)
        cexp = jnp.exp(b_last + m - m_new)
        kw = (kt.astype(F32) * wexp).astype(BF16)
        c_ref[j] = cexp * cmat + _dot(kw, vb)
        m_ref[mrow:mrow + 1, :] = jnp.broadcast_to(m_new, (1, LANES))

    def step(i, carry):
        chains = [(j, rev) for j in range(ML_HPS) for rev in (False, True)]
        parts = [intra(nc - 1 - i if rev else i, j, rev) for j, rev in chains]
        for p, (j, rev) in zip(parts, chains):
            update(p, j, rev)
        return carry

    lax.fori_loop(0, nc, step, 0)

    real = lax.broadcasted_iota(I32, (1, ML_DHP), 1) < ML_DH
    tb = CHUNK

    def fin(c, carry):
        t0 = pl.multiple_of(c * tb, tb)
        for j in range(ML_HPS):
            hl = slice(j * ML_DHP, (j + 1) * ML_DHP)
            hs = jnp.where(real, hf_ref[pl.ds(t0, tb), hl] + hb_ref[pl.ds(t0, tb), hl], 0.0)
            mu = jnp.sum(hs, axis=1, keepdims=True) * (1.0 / ML_DH)
            dev = jnp.where(real, hs - mu, 0.0)
            var = jnp.sum(dev * dev, axis=1, keepdims=True) * (1.0 / ML_DH)
            hn = dev * lax.rsqrt(var + LN_EPS) * ng_ref[:, hl]
            xc = xc_ref[0, pl.ds(t0, tb), hl].astype(F32)
            z = z_ref[0, pl.ds(t0, tb), hl].astype(F32)
            y_ref[0, pl.ds(t0, tb), hl] = ((hn + sk_ref[:, hl] * xc) * _silu(z)).astype(BF16)
        return carry

    lax.fori_loop(0, s // tb, fin, 0)


def _mlstm(q, kt, v, gcol3, grow, main3, xc, ng, sk):
    b, s, _ = q.shape
    width = ML_HPS * ML_DHP
    steps = ML_HEADS // ML_HPS
    tok = pl.BlockSpec((1, s, width), lambda i, h: (i, 0, h))
    vec = pl.BlockSpec((1, width), lambda i, h: (0, h))
    return pl.pallas_call(
        functools.partial(_mlstm_kernel, s=s),
        grid=(b, steps),
        in_specs=[tok, pl.BlockSpec((1, width, s), lambda i, h: (i, h, 0)), tok,
                  pl.BlockSpec((1, s, LANES), lambda i, h: (i, 0, 0)),
                  pl.BlockSpec((LANES, s), lambda i, h: (0, i)),
                  pl.BlockSpec((1, s, width), lambda i, h: (i, 0, steps + h)),
                  tok, vec, vec],
        out_specs=tok,
        out_shape=jax.ShapeDtypeStruct((b, s, ML_WP), BF16),
        scratch_shapes=[pltpu.VMEM((s, width), F32), pltpu.VMEM((s, width), F32),
                        pltpu.VMEM((ML_HPS, ML_DHP, ML_DHP), F32), pltpu.VMEM((ML_HPS, ML_DHP, ML_DHP), F32),
                        pltpu.VMEM((SUBLANES, LANES), F32)],
        compiler_params=_cparams("parallel", "parallel"),
        name="mlstm",
    )(q, kt, v, gcol3, grow, main3, xc, ng, sk)


def _pad_heads(a, axis):
    a = jnp.moveaxis(a, axis, -1)
    lead = a.shape[:-1]
    a = a.reshape(lead + (ML_HEADS, ML_DH))
    a = jnp.pad(a, [(0, 0)] * len(lead) + [(0, 0), (0, ML_DHP - ML_DH)])
    return jnp.moveaxis(a.reshape(lead + (ML_WP,)), -1, axis)


def kernel(x, mem, mem_ln_g, mem_ln_b, w_mem_kv, router_w, router_b, na_w_in, na_rpb, ml_w_in, ml_conv_w,
           ml_conv_b, ml_w_qkv, ml_gate_b, ml_norm_g, ml_skip, w_out, ln_g, ln_b, exp_w_gate, exp_w_up,
           exp_w_down):
    b, s, d = x.shape
    n = b * s
    nm = mem.shape[1]
    row = lambda a: a.reshape(1, -1)

    mem_k, mem_v = _memkv(mem.reshape(b * nm, d), row(mem_ln_g), row(mem_ln_b), w_mem_kv.astype(BF16))
    mem_k3 = mem_k.reshape(b, nm, MEM_W)
    mem_v3 = mem_v.reshape(b, nm, MEM_W)
    rw_pad = jnp.pad(router_w, ((0, 0), (0, LANES - N_EXPERTS)))
    rw_hi = rw_pad.astype(BF16)
    rw = (rw_hi, (rw_pad - rw_hi.astype(F32)).astype(BF16))
    rb = router_b.reshape(N_EXPERTS, 1)

    x2 = x.reshape(n, d)

    h0 = _proj(x2, na_w_in[0].astype(BF16)).reshape(b, s, 3 * NA_W + MEM_W)
    y_na = _na_attention(h0, _na_bias_table(na_rpb[0]))
    wo = w_out[0].astype(BF16)
    xr = _outproj_ln(y_na.reshape(n, NA_W), h0.reshape(n, 3 * NA_W + MEM_W), 3 * NA_W // MEM_W, mem_k3, mem_v3,
                     wo[:NA_W], wo[NA_W:], x2, row(ln_g[0, 0]), row(ln_b[0, 0]))
    x2 = _moe_ln(xr, rw, rb, exp_w_gate, exp_w_up, exp_w_down, 0, row(ln_g[0, 1]), row(ln_b[0, 1]))

    w1 = ml_w_in[0]
    w_main = jnp.concatenate([_pad_heads(w1[:, :ML_W], 1), _pad_heads(w1[:, ML_W:2 * ML_W], 1),
                              w1[:, 2 * ML_W + 4 * ML_HEADS:]], axis=1).astype(BF16)
    w_g = jnp.pad(w1[:, 2 * ML_W:2 * ML_W + 4 * ML_HEADS], ((0, 0), (0, LANES - 4 * ML_HEADS))).astype(BF16)
    gb = jnp.pad(ml_gate_b[0].reshape(4 * ML_HEADS), (0, LANES - 4 * ML_HEADS))
    main, acol, arow = _proj_gates(x2, w_main, w_g, w_g.T, gb.reshape(1, LANES), gb.reshape(LANES, 1))
    main3 = main.reshape(b, s, 2 * ML_WP + MEM_W)
    wqkv = jnp.pad(ml_w_qkv[0], ((0, 0), (0, 0), (0, ML_DHP - ML_DH), (0, ML_DHP - ML_DH))).astype(BF16)
    q, k, v, xc = _conv_qkv(main3, _pad_heads(ml_conv_w[0], 1), _pad_heads(row(ml_conv_b[0]), 1),
                            wqkv[0], jnp.swapaxes(wqkv[1], 1, 2), wqkv[2])
    y_ml = _mlstm(q, k, v, acol.reshape(b, s, LANES), arow, main3, xc,
                  _pad_heads(row(ml_norm_g[0]), 1), _pad_heads(row(ml_skip[0]), 1))
    wo = w_out[1]
    xr = _outproj_ln(y_ml.reshape(n, ML_WP), main, 2 * ML_WP // MEM_W, mem_k3, mem_v3,
                     _pad_heads(wo[:ML_W], 0).astype(BF16), wo[ML_W:].astype(BF16), x2,
                     row(ln_g[1, 0]), row(ln_b[1, 0]))
    x2 = _moe_ln(xr, rw, rb, exp_w_gate, exp_w_up, exp_w_down, 1, row(ln_g[1, 1]), row(ln_b[1, 1]))
    return x2.reshape(b, s, d)
```

```python
import functools

import numpy as np
import jax
import jax.numpy as jnp
from jax import lax
from jax.experimental import pallas as pl
from jax.experimental.pallas import tpu as pltpu

F32 = jnp.float32
BF16 = jnp.bfloat16
I32 = jnp.int32

D_MODEL = 1024
DEPTH = 2
GRID_W = 64
MEM_HEADS = 4
MEM_DH = 64
MEM_W = MEM_HEADS * MEM_DH
NA_HEADS = 12
NA_DH = 64
NA_W = NA_HEADS * NA_DH
WIN_H = 8
WIN_W = 16
ML_HEADS = 4
ML_DH = 192
ML_DHP = 256
ML_W = ML_HEADS * ML_DH
ML_WP = ML_HEADS * ML_DHP
CONV_K = 5
CHUNK = 256
N_EXPERTS = 16
N_GROUPS = 4
EXPERTS_PER_GROUP = N_EXPERTS // N_GROUPS
D_EXPERT = 512
ALPHA = (2 * DEPTH) ** 0.25
LN_EPS = 1e-5
NEG = -1e30

LANES = 128
SUBLANES = 8
ROW_CHUNKS = D_MODEL // LANES
MOE_BM = 512
MOE_TILE = 512
ML_HPS = 2
NA_ROWS_PER_STEP = 8
ROW_TILE = 512
PLACE_UNROLL = 8
VMEM_LIMIT = 48 * 1024 * 1024


def _cparams(*sem):
    return pltpu.CompilerParams(dimension_semantics=sem, vmem_limit_bytes=VMEM_LIMIT)


def _dot(a, b):
    return jnp.dot(a, b, preferred_element_type=F32)


def _dot_nt(a, b, precision=None):
    return lax.dot_general(a, b, (((1,), (1,)), ((), ())), precision=precision,
                           preferred_element_type=F32)


def _ln(z, g, b):
    mu = jnp.mean(z, axis=-1, keepdims=True)
    zc = z - mu
    var = jnp.mean(zc * zc, axis=-1, keepdims=True)
    return zc * lax.rsqrt(var + LN_EPS) * g + b


def _silu(x):
    return x * jax.nn.sigmoid(x)


def _read_rows(ref, n):
    return jnp.concatenate([ref[pl.ds(j, n, stride=ROW_CHUNKS), :] for j in range(ROW_CHUNKS)], axis=1)


def _write_rows(ref, val, n):
    for j in range(ROW_CHUNKS):
        ref[pl.ds(j, n, stride=ROW_CHUNKS), :] = val[:, j * LANES:(j + 1) * LANES]


def _memkv_kernel(m_ref, g_ref, b_ref, w_ref, k_ref, v_ref):
    z = _ln(m_ref[...], g_ref[...], b_ref[...])
    kv = _dot(z.astype(BF16), w_ref[...])
    k_ref[...] = kv[:, :MEM_W].astype(BF16)
    v_ref[...] = kv[:, MEM_W:].astype(BF16)


def _memkv(mem2, g, b, w):
    n = mem2.shape[0]
    tm = min(ROW_TILE, n)
    return pl.pallas_call(
        _memkv_kernel,
        grid=(n // tm,),
        in_specs=[pl.BlockSpec((tm, D_MODEL), lambda i: (i, 0)),
                  pl.BlockSpec((1, D_MODEL), lambda i: (0, 0)),
                  pl.BlockSpec((1, D_MODEL), lambda i: (0, 0)),
                  pl.BlockSpec((D_MODEL, 2 * MEM_W), lambda i: (0, 0))],
        out_specs=[pl.BlockSpec((tm, MEM_W), lambda i: (i, 0)),
                   pl.BlockSpec((tm, MEM_W), lambda i: (i, 0))],
        out_shape=[jax.ShapeDtypeStruct((n, MEM_W), BF16)] * 2,
        compiler_params=_cparams("parallel"),
        name="memkv",
    )(mem2, g, b, w)


def _proj_kernel(x_ref, w_ref, o_ref):
    o_ref[...] = _dot(x_ref[...].astype(BF16), w_ref[...]).astype(o_ref.dtype)


def _proj(x2, w, tm=ROW_TILE):
    n, k = x2.shape
    nout = w.shape[1]
    return pl.pallas_call(
        _proj_kernel,
        grid=(n // tm,),
        in_specs=[pl.BlockSpec((tm, k), lambda i: (i, 0)),
                  pl.BlockSpec((k, nout), lambda i: (0, 0))],
        out_specs=pl.BlockSpec((tm, nout), lambda i: (i, 0)),
        out_shape=jax.ShapeDtypeStruct((n, nout), BF16),
        compiler_params=_cparams("parallel"),
        name="in_proj",
    )(x2, w)


def _split3(x):
    hi = x.astype(BF16)
    r1 = x - hi.astype(F32)
    mid = r1.astype(BF16)
    lo = (r1 - mid.astype(F32)).astype(BF16)
    return hi, mid, lo


def _proj_gates_kernel(x_ref, w_ref, wg_ref, wgt_ref, gbc_ref, gbr_ref, o_ref, g_ref, gt_ref, *, tm):
    xb = x_ref[...].astype(BF16)
    o_ref[...] = _dot(xb, w_ref[...]).astype(BF16)
    gcol = _dot(xb, wg_ref[...]) + gbc_ref[...]
    grow = _dot_nt(wgt_ref[...], xb) + gbr_ref[...]
    lane = lax.broadcasted_iota(I32, (1, LANES), 1)
    sub = lax.broadcasted_iota(I32, (LANES, 1), 0)
    ti = lax.broadcasted_iota(I32, (CHUNK, CHUNK), 0)
    tj = lax.broadcasted_iota(I32, (CHUNK, CHUNK), 1)
    lower = (tj <= ti).astype(BF16)
    upper = (ti <= tj).astype(BF16)

    def pick(idx, pre, suf, raw):
        fwd = jnp.logical_and(idx >= ML_HEADS, idx < 2 * ML_HEADS)
        bwd = jnp.logical_and(idx >= 3 * ML_HEADS, idx < 4 * ML_HEADS)
        return jnp.where(fwd, pre, jnp.where(bwd, suf, raw))

    for c in range(tm // CHUNK):
        tc = slice(c * CHUNK, (c + 1) * CHUNK)
        g = gcol[tc, :]
        ls = jax.nn.log_sigmoid(g)
        pre = sum(_dot(lower, part) for part in _split3(ls))
        suf = jnp.sum(ls, axis=0, keepdims=True) - pre + ls
        g_ref[tc, :] = pick(lane, pre, suf, g)
        g = grow[:, tc]
        ls = jax.nn.log_sigmoid(g)
        pre = sum(_dot(part, upper) for part in _split3(ls))
        suf = jnp.sum(ls, axis=1, keepdims=True) - pre + ls
        gt_ref[:, tc] = pick(sub, pre, suf, g)


def _proj_gates(x2, w, wg, wgt, gbc, gbr, tm=ROW_TILE):
    n, k = x2.shape
    nout = w.shape[1]
    return pl.pallas_call(
        functools.partial(_proj_gates_kernel, tm=tm),
        grid=(n // tm,),
        in_specs=[pl.BlockSpec((tm, k), lambda i: (i, 0)),
                  pl.BlockSpec((k, nout), lambda i: (0, 0)),
                  pl.BlockSpec((k, LANES), lambda i: (0, 0)),
                  pl.BlockSpec((LANES, k), lambda i: (0, 0)),
                  pl.BlockSpec((1, LANES), lambda i: (0, 0)),
                  pl.BlockSpec((LANES, 1), lambda i: (0, 0))],
        out_specs=[pl.BlockSpec((tm, nout), lambda i: (i, 0)),
                   pl.BlockSpec((tm, LANES), lambda i: (i, 0)),
                   pl.BlockSpec((LANES, tm), lambda i: (0, i))],
        out_shape=[jax.ShapeDtypeStruct((n, nout), BF16),
                   jax.ShapeDtypeStruct((n, LANES), F32),
                   jax.ShapeDtypeStruct((LANES, n), F32)],
        compiler_params=_cparams("parallel"),
        name="in_proj_gates",
    )(x2, w, wg, wgt, gbc, gbr)


def _na_kernel(q_ref, k_ref, v_ref, tbl_ref, o_ref, *, rows):
    lane = lax.broadcasted_iota(I32, (1, LANES), 1)
    first = lane < NA_DH
    nkeys = WIN_H * GRID_W

    def rows_step(i, carry):
        rr = [i * NA_ROWS_PER_STEP + u for u in range(NA_ROWS_PER_STEP)]
        rss = [jnp.clip(r - WIN_H // 2, 0, rows - WIN_H) for r in rr]
        scores = []
        for r, rs in zip(rr, rss):
            q = q_ref[0, pl.ds(pl.multiple_of(r * GRID_W, GRID_W), GRID_W), :]
            q = q * jnp.asarray(NA_DH ** -0.5, BF16)
            q2 = jnp.concatenate([jnp.where(first, q, jnp.zeros_like(q)),
                                  jnp.where(first, jnp.zeros_like(q), q)], axis=0)
            k = k_ref[0, pl.ds(pl.multiple_of(rs * GRID_W, GRID_W), nkeys), :]
            dr0 = rs - r + WIN_H - 1
            bias = jnp.concatenate(
                [jnp.concatenate([tbl_ref[0, half, dr0 + 2 * m] for m in range(WIN_H // 2)], axis=1)
                 for half in range(2)], axis=0)
            scores.append(_dot_nt(q2, k) + bias)
        probs = []
        for s in scores:
            p = jnp.exp(s - jnp.max(s, axis=-1, keepdims=True))
            probs.append((p.astype(BF16), jnp.sum(p, axis=-1, keepdims=True)))
        for r, rs, (p, l) in zip(rr, rss, probs):
            v = v_ref[0, pl.ds(pl.multiple_of(rs * GRID_W, GRID_W), nkeys), :]
            o = _dot(p, v) / l
            o = jnp.where(first, o[:GRID_W], o[GRID_W:])
            o_ref[0, pl.ds(pl.multiple_of(r * GRID_W, GRID_W), GRID_W), :] = o.astype(o_ref.dtype)
        return carry

    lax.fori_loop(0, rows // NA_ROWS_PER_STEP, rows_step, 0)


def _na_bias_table(rpb):
    qc = np.arange(GRID_W)[:, None]
    kc = np.arange(GRID_W)[None, :]
    cs = np.clip(qc - WIN_W // 2, 0, GRID_W - WIN_W)
    col_in = (kc >= cs) & (kc < cs + WIN_W)
    side = GRID_W - WIN_W
    wide = jnp.pad(rpb, ((0, 0), (0, 0), (side, side)))
    t = jnp.stack([wide[:, :, GRID_W - 1 - q:2 * GRID_W - 1 - q] for q in range(GRID_W)], axis=2)
    t = jnp.where(col_in, t, NEG).astype(F32)
    t2 = jnp.concatenate([t[:, :-1], t[:, 1:]], axis=-1)
    return t2.reshape(NA_HEADS // 2, 2, 2 * WIN_H - 2, GRID_W, 2 * GRID_W)


def _na_attention(h3, tbl):
    b, s, _ = h3.shape
    rows = s // GRID_W
    npair = NA_HEADS // 2
    return pl.pallas_call(
        functools.partial(_na_kernel, rows=rows),
        grid=(b, npair),
        in_specs=[pl.BlockSpec((1, s, LANES), lambda i, p: (i, 0, p)),
                  pl.BlockSpec((1, s, LANES), lambda i, p: (i, 0, npair + p)),
                  pl.BlockSpec((1, s, LANES), lambda i, p: (i, 0, 2 * npair + p)),
                  pl.BlockSpec((1, 2, 2 * WIN_H - 2, GRID_W, 2 * GRID_W), lambda i, p: (p, 0, 0, 0, 0))],
        out_specs=pl.BlockSpec((1, s, LANES), lambda i, p: (i, 0, p)),
        out_shape=jax.ShapeDtypeStruct((b, s, NA_W), BF16),
        compiler_params=_cparams("parallel", "parallel"),
        name="na_attention",
    )(h3, h3, h3, tbl)


def _outproj_ln_kernel(ya_ref, qm_ref, mk_ref, mv_ref, wa_ref, wm_ref, x_ref, g_ref, b_ref, or_ref, *, tm):
    lane = lax.broadcasted_iota(I32, (1, LANES), 1)
    first = lane < MEM_DH
    q = qm_ref[...] * jnp.asarray(MEM_DH ** -0.5, BF16)
    cols = [slice(p * LANES, (p + 1) * LANES) for p in range(MEM_HEADS // 2)]
    scores = []
    for c in cols:
        qp = q[:, c]
        q2 = jnp.concatenate([jnp.where(first, qp, jnp.zeros_like(qp)),
                              jnp.where(first, jnp.zeros_like(qp), qp)], axis=0)
        scores.append(_dot_nt(q2, mk_ref[0, :, c]))
    probs = []
    for s in scores:
        p = jnp.exp(s - jnp.max(s, axis=-1, keepdims=True))
        probs.append((p.astype(BF16), jnp.sum(p, axis=-1, keepdims=True)))
    outs = []
    for c, (p, l) in zip(cols, probs):
        o = _dot(p, mv_ref[0, :, c]) / l
        outs.append(jnp.where(first, o[:tm], o[tm:]))
    ym = jnp.concatenate(outs, axis=1).astype(BF16)
    acc = _dot(ya_ref[...], wa_ref[...]) + _dot(ym, wm_ref[...])
    _write_rows(or_ref, _ln(ALPHA * x_ref[...] + acc, g_ref[...], b_ref[...]), tm)


def _outproj_ln(ya, h2, qm_block, mem_k3, mem_v3, wa, wm, x2, g, b, tm=ROW_TILE):
    n = x2.shape[0]
    ka = ya.shape[1]
    nb, nm, _ = mem_k3.shape
    per_batch = n // nb // tm
    full = lambda shape: pl.BlockSpec(shape, lambda i: (0,) * len(shape))
    return pl.pallas_call(
        functools.partial(_outproj_ln_kernel, tm=tm),
        grid=(n // tm,),
        in_specs=[pl.BlockSpec((tm, ka), lambda i: (i, 0)),
                  pl.BlockSpec((tm, MEM_W), lambda i: (i, qm_block)),
                  pl.BlockSpec((1, nm, MEM_W), lambda i: (i // per_batch, 0, 0)),
                  pl.BlockSpec((1, nm, MEM_W), lambda i: (i // per_batch, 0, 0)),
                  full((ka, D_MODEL)), full((MEM_W, D_MODEL)),
                  pl.BlockSpec((tm, D_MODEL), lambda i: (i, 0)),
                  full((1, D_MODEL)), full((1, D_MODEL))],
        out_specs=pl.BlockSpec((tm * ROW_CHUNKS, LANES), lambda i: (i, 0)),
        out_shape=jax.ShapeDtypeStruct((n * ROW_CHUNKS, LANES), F32),
        compiler_params=_cparams("parallel"),
        name="outproj_ln",
    )(ya, h2, mem_k3, mem_v3, wa, wm, x2, g, b)


def _router_kernel(x_ref, rwh_ref, rwl_ref, rb_ref, lpos_ref, w_ref, cnt_ref, tcnt_ref, toff_ref, tbef_ref, *, tm):
    @pl.when(pl.program_id(0) == 0)
    def _():
        cnt_ref[...] = jnp.zeros_like(cnt_ref)

    x = _read_rows(x_ref, tm)
    xh = x.astype(BF16)
    xl = (x - xh.astype(F32)).astype(BF16)
    logits_t = _dot(xh, rwh_ref[...]) + (_dot(xh, rwl_ref[...]) + _dot(xl, rwh_ref[...]))
    logits = logits_t.T[:N_EXPERTS]
    scores = jax.nn.sigmoid(logits)
    biased = scores + rb_ref[...]
    bv = [biased[e:e + 1, :] for e in range(N_EXPERTS)]
    sv = [scores[e:e + 1, :] for e in range(N_EXPERTS)]

    grp = []
    for g in range(N_GROUPS):
        m = bv[g * EXPERTS_PER_GROUP:(g + 1) * EXPERTS_PER_GROUP]
        best = None
        for a in range(EXPERTS_PER_GROUP):
            for c in range(a + 1, EXPERTS_PER_GROUP):
                pair = m[a] + m[c]
                best = pair if best is None else jnp.maximum(best, pair)
        grp.append(best)
    gsel = jnp.zeros((1, tm), I32)
    gbest = grp[0]
    for g in range(1, N_GROUPS):
        better = grp[g] > gbest
        gsel = jnp.where(better, g, gsel)
        gbest = jnp.where(better, grp[g], gbest)

    def pick(vals, j):
        out = vals[j]
        for g in range(1, N_GROUPS):
            out = jnp.where(gsel == g, vals[g * EXPERTS_PER_GROUP + j], out)
        return out

    cb = [pick(bv, j) for j in range(EXPERTS_PER_GROUP)]
    cs = [pick(sv, j) for j in range(EXPERTS_PER_GROUP)]
    i1 = jnp.zeros((1, tm), I32)
    m1 = cb[0]
    s1 = cs[0]
    for j in range(1, EXPERTS_PER_GROUP):
        gt = cb[j] > m1
        i1 = jnp.where(gt, j, i1)
        m1 = jnp.where(gt, cb[j], m1)
        s1 = jnp.where(gt, cs[j], s1)
    i2 = jnp.zeros((1, tm), I32)
    m2 = jnp.full((1, tm), -jnp.inf, F32)
    s2 = jnp.zeros((1, tm), F32)
    for j in range(EXPERTS_PER_GROUP):
        ok = jnp.logical_and(i1 != j, cb[j] > m2)
        i2 = jnp.where(ok, j, i2)
        m2 = jnp.where(ok, cb[j], m2)
        s2 = jnp.where(ok, cs[j], s2)
    e1 = gsel * EXPERTS_PER_GROUP + i1
    e2 = gsel * EXPERTS_PER_GROUP + i2
    tot = s1 + s2
    w_ref[...] = jnp.concatenate([s1 / tot, s2 / tot], axis=0)

    i = pl.program_id(0)
    eio = lax.broadcasted_iota(I32, (N_EXPERTS, tm), 0)
    oh1 = eio == e1
    oh2 = eio == e2
    ohs = jnp.logical_or(oh1, oh2).astype(F32)
    before = (lax.broadcasted_iota(I32, (tm, tm), 0) < lax.broadcasted_iota(I32, (tm, tm), 1))
    pre = _dot(ohs.astype(BF16), before.astype(BF16))
    tile_cnt = jnp.sum(ohs, axis=1, keepdims=True)
    offs = []
    acc = jnp.zeros((1, 1), F32)
    for e in range(N_EXPERTS):
        offs.append(acc)
        acc = acc + tile_cnt[e:e + 1, :]
    tile_off = jnp.concatenate(offs, axis=0)
    pos = tile_off + pre
    p1 = jnp.sum(jnp.where(oh1, pos, 0.0), axis=0, keepdims=True)
    p2 = jnp.sum(jnp.where(oh2, pos, 0.0), axis=0, keepdims=True)
    lpos_ref[...] = jnp.concatenate([p1, p2], axis=0).astype(I32) * ROW_CHUNKS

    @pl.when(i == 0)
    def _():
        for ref in (tcnt_ref, toff_ref, tbef_ref):
            ref[...] = jnp.zeros_like(ref)

    here = lax.broadcasted_iota(I32, (1, LANES), 1) == i
    tcnt_ref[...] = jnp.where(here, tile_cnt, tcnt_ref[...])
    toff_ref[...] = jnp.where(here, tile_off, toff_ref[...])
    tbef_ref[...] = jnp.where(here, cnt_ref[:, 0:1], tbef_ref[...])
    cnt_ref[...] += tile_cnt


def _router(xr, rw, rb, tm):
    n = xr.shape[0] // ROW_CHUNKS
    assert n // tm <= LANES
    table = pl.BlockSpec((N_EXPERTS, LANES), lambda i: (0, 0))
    return pl.pallas_call(
        functools.partial(_router_kernel, tm=tm),
        grid=(n // tm,),
        in_specs=[pl.BlockSpec((tm * ROW_CHUNKS, LANES), lambda i: (i, 0)),
                  pl.BlockSpec((D_MODEL, LANES), lambda i: (0, 0)),
                  pl.BlockSpec((D_MODEL, LANES), lambda i: (0, 0)),
                  pl.BlockSpec((N_EXPERTS, 1), lambda i: (0, 0))],
        out_specs=[pl.BlockSpec((2, tm), lambda i: (0, i)),
                   pl.BlockSpec((2, tm), lambda i: (0, i)),
                   table, table, table, table],
        out_shape=[jax.ShapeDtypeStruct((2, n), I32),
                   jax.ShapeDtypeStruct((2, n), F32)]
                  + [jax.ShapeDtypeStruct((N_EXPERTS, LANES), F32)] * 4,
        compiler_params=_cparams("arbitrary"),
        name="router",
    )(xr, rw[0], rw[1], rb)


def _plan_kernel(cnt_ref, tbef_ref, meta_ref, rstart_ref, *, nbl):
    shift = MOE_BM.bit_length() - 1
    cnt = cnt_ref[...].astype(I32)
    padded = ((cnt + (MOE_BM - 1)) >> shift) << shift
    starts = []
    acc = jnp.zeros((1, LANES), I32)
    for e in range(N_EXPERTS):
        starts.append(acc)
        acc = acc + padded[e:e + 1, :]
    pad_start = jnp.concatenate(starts, axis=0)
    pad_end = pad_start + padded
    rstart_ref[...] = pad_start + tbef_ref[...].astype(I32)
    blk0 = lax.broadcasted_iota(I32, (N_EXPERTS, nbl), 1) * MOE_BM
    block_e = jnp.sum((pad_end[:, 0:1] <= blk0).astype(I32), axis=0, keepdims=True)
    block_e = jnp.minimum(block_e, N_EXPERTS - 1)
    n_used = jnp.broadcast_to(acc[:, 0:1] >> shift, (1, nbl))
    diag = lax.broadcasted_iota(I32, (N_EXPERTS, nbl), 0) == lax.broadcasted_iota(I32, (N_EXPERTS, nbl), 1)
    fill_lo = jnp.sum(jnp.where(diag, (pad_start + cnt)[:, 0:1], 0), axis=0, keepdims=True)
    fill_hi = jnp.sum(jnp.where(diag, pad_end[:, 0:1], 0), axis=0, keepdims=True)
    meta_ref[...] = jnp.concatenate([block_e, n_used, fill_lo, fill_hi, jnp.zeros((SUBLANES - 4, nbl), I32)],
                                    axis=0)


def _plan(cnt, tbef, n_blocks):
    nbl = -(-n_blocks // LANES) * LANES
    table = pl.BlockSpec((N_EXPERTS, LANES), lambda i: (0, 0))
    return pl.pallas_call(
        functools.partial(_plan_kernel, nbl=nbl),
        grid=(1,),
        in_specs=[table, table],
        out_specs=[pl.BlockSpec((SUBLANES, nbl), lambda i: (0, 0)), table],
        out_shape=[jax.ShapeDtypeStruct((SUBLANES, nbl), I32),
                   jax.ShapeDtypeStruct((N_EXPERTS, LANES), I32)],
        compiler_params=_cparams("arbitrary"),
        name="moe_plan",
    )(cnt, tbef)


def _rows(ref, row, nrows):
    return ref.at[pl.ds(pl.multiple_of(row * ROW_CHUNKS, ROW_CHUNKS), nrows * ROW_CHUNKS), :]


def _rows_wait(src_hbm, buf, sem):
    pltpu.make_async_copy(src_hbm.at[pl.ds(0, buf.shape[0]), :], buf, sem).wait()


def _copy_pieces(src, src_row, dst, dst_row, count, max_rows, sem, wait=False):
    bit = max_rows.bit_length() - 1
    while bit >= 0:
        size = 1 << bit
        done = (count >> (bit + 1)) << (bit + 1)

        @pl.when(((count >> bit) & 1) == 1)
        def _():
            cp = pltpu.make_async_copy(_rows(src, src_row + done, size), _rows(dst, dst_row + done, size), sem)
            cp.start()
            if wait:
                cp.wait()

        bit -= 1


def _tile_runs(tcnt_ref, toff_ref, rstart_ref, tile, buf, hbm, sem, *, to_hbm, tm):
    def per_expert(e, carry):
        k = tile * N_EXPERTS + e
        if to_hbm:
            _copy_pieces(buf, toff_ref[k], hbm, rstart_ref[k], tcnt_ref[k], tm, sem)
        else:
            _copy_pieces(hbm, rstart_ref[k], buf, toff_ref[k], tcnt_ref[k], tm, sem)
        return carry

    lax.fori_loop(0, N_EXPERTS, per_expert, 0)


def _dispatch_kernel(lpos_ref, tcnt_ref, toff_ref, rstart_ref, flo_ref, fhi_ref, nu_ref, x_ref, xs_hbm,
                     s0, s1, zbuf, sem, zsem, *, n, tm, n_blocks):
    i = pl.program_id(0)
    nt = pl.num_programs(0)
    bufs = (s0, s1)
    unroll = PLACE_UNROLL

    for slot in range(2):
        @pl.when(i % 2 == slot)
        def _():
            buf = bufs[slot]

            @pl.when(i >= 2)
            def _():
                _rows_wait(xs_hbm, buf, sem.at[slot])

            def place(c, carry):
                tok = i * tm + c * unroll
                src = pl.multiple_of(c * (unroll * ROW_CHUNKS), unroll * ROW_CHUNKS)
                for u in range(unroll):
                    v = x_ref[pl.ds(src + u * ROW_CHUNKS, ROW_CHUNKS), :]
                    for k in range(2):
                        p = lpos_ref[k * n + tok + u]
                        buf[pl.ds(pl.multiple_of(p, ROW_CHUNKS), ROW_CHUNKS), :] = v
                return carry

            lax.fori_loop(0, tm // unroll, place, 0)
            _tile_runs(tcnt_ref, toff_ref, rstart_ref, i, buf, xs_hbm, sem.at[slot], to_hbm=True, tm=tm)

    @pl.when(i == nt - 1)
    def _():
        for slot in range(2):
            @pl.when(nt > slot)
            def _():
                _rows_wait(xs_hbm, bufs[slot], sem.at[slot])

        zbuf[...] = jnp.zeros_like(zbuf)
        for e in range(N_EXPERTS):
            _copy_pieces(zbuf, 0, xs_hbm, flo_ref[e], fhi_ref[e] - flo_ref[e], MOE_BM // 2, zsem, wait=True)

        def zero_block(j, carry):
            cp = pltpu.make_async_copy(zbuf, _rows(xs_hbm, j * MOE_BM, MOE_BM), zsem)
            cp.start()
            cp.wait()
            return carry

        lax.fori_loop(nu_ref[0], n_blocks, zero_block, 0)


def _dispatch(lpos_flat, tcnt, toff, rstart, fill_lo, fill_hi, n_used, xr, n_blocks, tm):
    n = xr.shape[0] // ROW_CHUNKS
    return pl.pallas_call(
        functools.partial(_dispatch_kernel, n=n, tm=tm, n_blocks=n_blocks),
        grid_spec=pltpu.PrefetchScalarGridSpec(
            num_scalar_prefetch=7,
            grid=(n // tm,),
            in_specs=[pl.BlockSpec((tm * ROW_CHUNKS, LANES), lambda i, *_: (i, 0))],
            out_specs=pl.BlockSpec(memory_space=pl.ANY),
            scratch_shapes=[pltpu.VMEM((2 * tm * ROW_CHUNKS, LANES), F32),
                            pltpu.VMEM((2 * tm * ROW_CHUNKS, LANES), F32),
                            pltpu.VMEM((MOE_BM * ROW_CHUNKS, LANES), F32),
                            pltpu.SemaphoreType.DMA((2,)),
                            pltpu.SemaphoreType.DMA(())]),
        out_shape=jax.ShapeDtypeStruct((n_blocks * MOE_BM * ROW_CHUNKS, LANES), F32),
        compiler_params=_cparams("arbitrary"),
        name="moe_dispatch",
    )(lpos_flat, tcnt, toff, rstart, fill_lo, fill_hi, n_used, xr)


def _experts_kernel(be_ref, nu_ref, xs_ref, wg_ref, wu_ref, wd_ref, y_ref, wgb, wub, wdb):
    j = pl.program_id(0)
    used = j < nu_ref[0]

    @pl.when(jnp.logical_and(used, jnp.logical_or(j == 0, be_ref[j] != be_ref[jnp.maximum(j - 1, 0)])))
    def _():
        wgb[...] = wg_ref[0, 0].astype(BF16)
        wub[...] = wu_ref[0, 0].astype(BF16)
        wdb[...] = wd_ref[0, 0].astype(BF16)

    @pl.when(used)
    def _():
        x = _read_rows(xs_ref, MOE_BM).astype(BF16)
        h = _silu(_dot(x, wgb[...])) * _dot(x, wub[...])
        _write_rows(y_ref, _dot(h.astype(BF16), wdb[...]), MOE_BM)

    @pl.when(jnp.logical_not(used))
    def _():
        y_ref[...] = jnp.zeros_like(y_ref)


def _experts(block_e, n_used, xs, wg, wu, wd, layer):
    n_blocks = block_e.shape[0]

    def last_used(j, nu):
        return jnp.minimum(j, nu[0] - 1)

    def wblk(j, be, nu):
        return (layer, be[last_used(j, nu)], 0, 0)

    return pl.pallas_call(
        _experts_kernel,
        grid_spec=pltpu.PrefetchScalarGridSpec(
            num_scalar_prefetch=2,
            grid=(n_blocks,),
            in_specs=[pl.BlockSpec((MOE_BM * ROW_CHUNKS, LANES), lambda j, be, nu: (last_used(j, nu), 0)),
                      pl.BlockSpec((1, 1, D_MODEL, D_EXPERT), wblk),
                      pl.BlockSpec((1, 1, D_MODEL, D_EXPERT), wblk),
                      pl.BlockSpec((1, 1, D_EXPERT, D_MODEL), wblk)],
            out_specs=pl.BlockSpec((MOE_BM * ROW_CHUNKS, LANES), lambda j, be, nu: (j, 0)),
            scratch_shapes=[pltpu.VMEM((D_MODEL, D_EXPERT), BF16), pltpu.VMEM((D_MODEL, D_EXPERT), BF16),
                            pltpu.VMEM((D_EXPERT, D_MODEL), BF16)]),
        out_shape=jax.ShapeDtypeStruct(xs.shape, F32),
        compiler_params=_cparams("arbitrary"),
        name="moe_experts",
    )(block_e, n_used, xs, wg, wu, wd)


def _combine_ln_kernel(lpos_ref, tcnt_ref, toff_ref, rstart_ref, y_hbm, x_ref, w1_ref, w2_ref, g_ref, b_ref, o_ref,
                       r0, r1, u1, u2, sem, *, n, tm):
    i = pl.program_id(0)
    nt = pl.num_programs(0)
    bufs = (r0, r1)
    unroll = PLACE_UNROLL

    def fetch(tile, slot):
        _tile_runs(tcnt_ref, toff_ref, rstart_ref, tile, bufs[slot], y_hbm, sem.at[slot], to_hbm=False, tm=tm)

    @pl.when(i == 0)
    def _():
        fetch(0, 0)

    for slot in range(2):
        @pl.when(i % 2 == slot)
        def _():
            @pl.when(i + 1 < nt)
            def _():
                fetch(i + 1, 1 - slot)

            buf = bufs[slot]
            _rows_wait(y_hbm, buf, sem.at[slot])

            def place(c, carry):
                tok = i * tm + c * unroll
                dst0 = pl.multiple_of(c * (unroll * ROW_CHUNKS), unroll * ROW_CHUNKS)
                for u in range(unroll):
                    dst = pl.ds(dst0 + u * ROW_CHUNKS, ROW_CHUNKS)
                    for k, out in enumerate((u1, u2)):
                        p = lpos_ref[k * n + tok + u]
                        out[dst, :] = buf[pl.ds(pl.multiple_of(p, ROW_CHUNKS), ROW_CHUNKS), :]
                return carry

            lax.fori_loop(0, tm // unroll, place, 0)
            moe = w1_ref[...] * _read_rows(u1, tm) + w2_ref[...] * _read_rows(u2, tm)
            o_ref[...] = _ln(ALPHA * _read_rows(x_ref, tm) + moe, g_ref[...], b_ref[...])


def _combine_ln(lpos_flat, tcnt, toff, rstart, y, xr, w1, w2, g, b, tm):
    n = xr.shape[0] // ROW_CHUNKS
    return pl.pallas_call(
        functools.partial(_combine_ln_kernel, n=n, tm=tm),
        grid_spec=pltpu.PrefetchScalarGridSpec(
            num_scalar_prefetch=4,
            grid=(n // tm,),
            in_specs=[pl.BlockSpec(memory_space=pl.ANY),
                      pl.BlockSpec((tm * ROW_CHUNKS, LANES), lambda i, *_: (i, 0)),
                      pl.BlockSpec((tm, 1), lambda i, *_: (i, 0)),
                      pl.BlockSpec((tm, 1), lambda i, *_: (i, 0)),
                      pl.BlockSpec((1, D_MODEL), lambda i, *_: (0, 0)),
                      pl.BlockSpec((1, D_MODEL), lambda i, *_: (0, 0))],
            out_specs=pl.BlockSpec((tm, D_MODEL), lambda i, *_: (i, 0)),
            scratch_shapes=[pltpu.VMEM((2 * tm * ROW_CHUNKS, LANES), F32)] * 2
                           + [pltpu.VMEM((tm * ROW_CHUNKS, LANES), F32)] * 2
                           + [pltpu.SemaphoreType.DMA((2,))]),
        out_shape=jax.ShapeDtypeStruct((n, D_MODEL), F32),
        compiler_params=_cparams("arbitrary"),
        name="moe_combine_ln",
    )(lpos_flat, tcnt, toff, rstart, y, xr, w1, w2, g, b)


def _moe_ln(xr, rw, rb, wg, wu, wd, layer, g, b):
    n = xr.shape[0] // ROW_CHUNKS
    n_blocks = (2 * n) // MOE_BM + N_EXPERTS
    tm = MOE_TILE
    nt = n // tm
    lpos, w, cnt, tcnt, toff, tbef = _router(xr, rw, rb, tm)
    meta, rstart = _plan(cnt, tbef, n_blocks)
    block_e = meta[0, :n_blocks]
    n_used = meta[1, :1]

    def per_tile(table):
        return table[:, :nt].T.reshape(nt * N_EXPERTS).astype(I32)

    lpos_flat = lpos.reshape(2 * n)
    tcnt, toff, rstart = per_tile(tcnt), per_tile(toff), per_tile(rstart)
    xs = _dispatch(lpos_flat, tcnt, toff, rstart, meta[2, :N_EXPERTS], meta[3, :N_EXPERTS], n_used, xr,
                   n_blocks, tm)
    y = _experts(block_e, n_used, xs, wg, wu, wd, layer)
    return _combine_ln(lpos_flat, tcnt, toff, rstart, y, xr, w[0].reshape(n, 1), w[1].reshape(n, 1), g, b, tm)


def _conv_qkv_kernel(xm_ref, cw_ref, cb_ref, wq_ref, wk_ref, wv_ref, q_ref, k_ref, v_ref, xc_ref, *, s):
    xm_b = xm_ref[0]
    xm = xm_b.astype(F32)
    cw = cw_ref[...]
    row = lax.broadcasted_iota(I32, (s, 1), 0)
    half = CONV_K // 2
    acc = cb_ref[...] + xm * cw[half:half + 1, :]
    for sh in range(1, half + 1):
        past = jnp.where(row >= sh, pltpu.roll(xm, sh, axis=0), 0.0)
        acc = acc + past * cw[half - sh:half - sh + 1, :]
        nxt = jnp.where(row < s - sh, pltpu.roll(xm, s - sh, axis=0), 0.0)
        acc = acc + nxt * cw[half + sh:half + sh + 1, :]
    xc = _silu(acc).astype(BF16)
    xc_ref[0] = xc
    q_ref[0] = _dot(xc, wq_ref[0]).astype(BF16)
    k_ref[0] = (_dot_nt(wk_ref[0], xc) * (ML_DH ** -0.5)).astype(BF16)
    v = _dot(xm_b, wv_ref[0])
    ones_lane = lax.broadcasted_iota(I32, (1, ML_DHP), 1) == ML_DH
    v_ref[0] = jnp.where(ones_lane, 1.0, v).astype(BF16)


def _conv_qkv(main3, cw, cb, wq, wk_t, wv):
    b, s, _ = main3.shape
    tok = pl.BlockSpec((1, s, ML_DHP), lambda i, h: (i, 0, h))
    wspec = pl.BlockSpec((1, ML_DHP, ML_DHP), lambda i, h: (h, 0, 0))
    tok_shape = jax.ShapeDtypeStruct((b, s, ML_WP), BF16)
    return pl.pallas_call(
        functools.partial(_conv_qkv_kernel, s=s),
        grid=(b, ML_HEADS),
        in_specs=[tok,
                  pl.BlockSpec((CONV_K, ML_DHP), lambda i, h: (0, h)),
                  pl.BlockSpec((1, ML_DHP), lambda i, h: (0, h)),
                  wspec, wspec, wspec],
        out_specs=[tok, pl.BlockSpec((1, ML_DHP, s), lambda i, h: (i, h, 0)), tok, tok],
        out_shape=[tok_shape, jax.ShapeDtypeStruct((b, ML_WP, s), BF16), tok_shape, tok_shape],
        compiler_params=_cparams("parallel", "parallel"),
        name="conv_qkv",
    )(main3, cw, cb, wq, wk_t, wv)


def _mlstm_kernel(q_ref, kt_ref, v_ref, gc_ref, gr_ref, z_ref, xc_ref, ng_ref, sk_ref,
                  y_ref, hf_ref, hb_ref, cf_ref, cb_ref, m_ref, *, s):
    head0 = pl.program_id(1) * ML_HPS
    nc = s // CHUNK
    sub = lax.broadcasted_iota(I32, (LANES, 1), 0)
    gate = lax.broadcasted_iota(I32, (LANES, LANES), 0)
    ti = lax.broadcasted_iota(I32, (CHUNK, CHUNK), 0)
    tj = lax.broadcasted_iota(I32, (CHUNK, CHUNK), 1)

    for ref in (cf_ref, cb_ref, m_ref):
        ref[...] = jnp.zeros_like(ref)

    def intra(c, j, rev):
        t0 = pl.multiple_of(c * CHUNK, CHUNK)
        hl = slice(j * ML_DHP, (j + 1) * ML_DHP)
        qb = q_ref[0, pl.ds(t0, CHUNK), hl]
        kt = kt_ref[0, hl, pl.ds(t0, CHUNK)]
        vb = v_ref[0, pl.ds(t0, CHUNK), hl]
        gc = gc_ref[0, pl.ds(t0, CHUNK), :]
        gr = gr_ref[:, pl.ds(t0, CHUNK)]
        i_idx = head0 + j + (2 * ML_HEADS if rev else 0)
        f_idx = i_idx + ML_HEADS
        allowed = (tj >= ti) if rev else (tj <= ti)
        sel = (gate == f_idx).astype(BF16)
        b_rep = sum(_dot(part, sel) for part in _split3(gc))
        b_row = jnp.sum(jnp.where(sub == f_idx, gr, 0.0), axis=0, keepdims=True)
        i_row = jnp.sum(jnp.where(sub == i_idx, gr, 0.0), axis=0, keepdims=True)
        b_last = (b_rep[0:1, :] if rev else b_rep[CHUNK - 1:CHUNK, :])[:, 0:1]

        b_wide = jnp.concatenate([b_rep] * (CHUNK // LANES), axis=1)
        d = jnp.where(allowed, b_wide - b_row + i_row, NEG)
        m_in = jnp.max(d, axis=1, keepdims=True)
        sc = _dot(qb, kt) * jnp.exp(d - m_in)
        nd_in = _dot(sc.astype(BF16), vb)
        w_row = b_last - b_row + i_row
        return t0, qb, kt, vb, b_rep, b_last, m_in, nd_in, w_row

    def twice(a):
        return jnp.concatenate([a, a], axis=1)

    def update(parts, j, rev):
        t0, qb, kt, vb, b_rep, b_last, m_in, nd_in, w_row = parts
        h_ref, c_ref = (hb_ref, cb_ref) if rev else (hf_ref, cf_ref)
        hl = slice(j * ML_DHP, (j + 1) * ML_DHP)
        mrow = 2 * j + int(rev)
        m = m_ref[mrow:mrow + 1, 0:1]
        cmat = c_ref[j]
        inter = b_rep + m
        m_t = jnp.maximum(m_in, inter)
        a_in = jnp.exp(m_in - m_t)
        iexp = jnp.exp(inter - m_t)
        nd = twice(a_in) * nd_in + twice(iexp) * _dot(qb, cmat.astype(BF16))
        den = nd[:, ML_DH:ML_DH + 1]
        h_ref[pl.ds(t0, CHUNK), hl] = nd * (1.0 / jnp.maximum(jnp.abs(den), jnp.exp(-m_t[:, 0:1])))

        m_new = jnp.maximum(b_last + m, jnp.max(w_row, axis=1, keepdims=True))
        wexp = jnp.exp(w_row - m_new)
        cexp = jnp.exp(b_last + m - m_new)
        kw = (kt.astype(F32) * wexp).astype(BF16)
        c_ref[j] = cexp * cmat + _dot(kw, vb)
        m_ref[mrow:mrow + 1, :] = jnp.broadcast_to(m_new, (1, LANES))

    def step(i, carry):
        for j in range(ML_HPS):
            parts = [intra(nc - 1 - i if rev else i, j, rev) for rev in (False, True)]
            for p, rev in zip(parts, (False, True)):
                update(p, j, rev)
        return carry

    lax.fori_loop(0, nc, step, 0)

    real = lax.broadcasted_iota(I32, (1, ML_DHP), 1) < ML_DH
    tb = CHUNK

    def fin(c, carry):
        t0 = pl.multiple_of(c * tb, tb)
        for j in range(ML_HPS):
            hl = slice(j * ML_DHP, (j + 1) * ML_DHP)
            hs = jnp.where(real, hf_ref[pl.ds(t0, tb), hl] + hb_ref[pl.ds(t0, tb), hl], 0.0)
            mu = jnp.sum(hs, axis=1, keepdims=True) * (1.0 / ML_DH)
            dev = jnp.where(real, hs - mu, 0.0)
            var = jnp.sum(dev * dev, axis=1, keepdims=True) * (1.0 / ML_DH)
            hn = dev * lax.rsqrt(var + LN_EPS) * ng_ref[:, hl]
            xc = xc_ref[0, pl.ds(t0, tb), hl].astype(F32)
            z = z_ref[0, pl.ds(t0, tb), hl].astype(F32)
            y_ref[0, pl.ds(t0, tb), hl] = ((hn + sk_ref[:, hl] * xc) * _silu(z)).astype(BF16)
        return carry

    lax.fori_loop(0, s // tb, fin, 0)


def _mlstm(q, kt, v, gcol3, grow, main3, xc, ng, sk):
    b, s, _ = q.shape
    width = ML_HPS * ML_DHP
    steps = ML_HEADS // ML_HPS
    tok = pl.BlockSpec((1, s, width), lambda i, h: (i, 0, h))
    vec = pl.BlockSpec((1, width), lambda i, h: (0, h))
    return pl.pallas_call(
        functools.partial(_mlstm_kernel, s=s),
        grid=(b, steps),
        in_specs=[tok, pl.BlockSpec((1, width, s), lambda i, h: (i, h, 0)), tok,
                  pl.BlockSpec((1, s, LANES), lambda i, h: (i, 0, 0)),
                  pl.BlockSpec((LANES, s), lambda i, h: (0, i)),
                  pl.BlockSpec((1, s, width), lambda i, h: (i, 0, steps + h)),
                  tok, vec, vec],
        out_specs=tok,
        out_shape=jax.ShapeDtypeStruct((b, s, ML_WP), BF16),
        scratch_shapes=[pltpu.VMEM((s, width), F32), pltpu.VMEM((s, width), F32),
                        pltpu.VMEM((ML_HPS, ML_DHP, ML_DHP), F32), pltpu.VMEM((ML_HPS, ML_DHP, ML_DHP), F32),
                        pltpu.VMEM((SUBLANES, LANES), F32)],
        compiler_params=_cparams("parallel", "parallel"),
        name="mlstm",
    )(q, kt, v, gcol3, grow, main3, xc, ng, sk)


def _pad_heads(a, axis):
    a = jnp.moveaxis(a, axis, -1)
    lead = a.shape[:-1]
    a = a.reshape(lead + (ML_HEADS, ML_DH))
    a = jnp.pad(a, [(0, 0)] * len(lead) + [(0, 0), (0, ML_DHP - ML_DH)])
    return jnp.moveaxis(a.reshape(lead + (ML_WP,)), -1, axis)


def kernel(x, mem, mem_ln_g, mem_ln_b, w_mem_kv, router_w, router_b, na_w_in, na_rpb, ml_w_in, ml_conv_w,
           ml_conv_b, ml_w_qkv, ml_gate_b, ml_norm_g, ml_skip, w_out, ln_g, ln_b, exp_w_gate, exp_w_up,
           exp_w_down):
    b, s, d = x.shape
    n = b * s
    nm = mem.shape[1]
    row = lambda a: a.reshape(1, -1)

    mem_k, mem_v = _memkv(mem.reshape(b * nm, d), row(mem_ln_g), row(mem_ln_b), w_mem_kv.astype(BF16))
    mem_k3 = mem_k.reshape(b, nm, MEM_W)
    mem_v3 = mem_v.reshape(b, nm, MEM_W)
    rw_pad = jnp.pad(router_w, ((0, 0), (0, LANES - N_EXPERTS)))
    rw_hi = rw_pad.astype(BF16)
    rw = (rw_hi, (rw_pad - rw_hi.astype(F32)).astype(BF16))
    rb = router_b.reshape(N_EXPERTS, 1)

    x2 = x.reshape(n, d)

    h0 = _proj(x2, na_w_in[0].astype(BF16)).reshape(b, s, 3 * NA_W + MEM_W)
    y_na = _na_attention(h0, _na_bias_table(na_rpb[0]))
    wo = w_out[0].astype(BF16)
    xr = _outproj_ln(y_na.reshape(n, NA_W), h0.reshape(n, 3 * NA_W + MEM_W), 3 * NA_W // MEM_W, mem_k3, mem_v3,
                     wo[:NA_W], wo[NA_W:], x2, row(ln_g[0, 0]), row(ln_b[0, 0]))
    x2 = _moe_ln(xr, rw, rb, exp_w_gate, exp_w_up, exp_w_down, 0, row(ln_g[0, 1]), row(ln_b[0, 1]))

    w1 = ml_w_in[0]
    w_main = jnp.concatenate([_pad_heads(w1[:, :ML_W], 1), _pad_heads(w1[:, ML_W:2 * ML_W], 1),
                              w1[:, 2 * ML_W + 4 * ML_HEADS:]], axis=1).astype(BF16)
    w_g = jnp.pad(w1[:, 2 * ML_W:2 * ML_W + 4 * ML_HEADS], ((0, 0), (0, LANES - 4 * ML_HEADS))).astype(BF16)
    gb = jnp.pad(ml_gate_b[0].reshape(4 * ML_HEADS), (0, LANES - 4 * ML_HEADS))
    main, acol, arow = _proj_gates(x2, w_main, w_g, w_g.T, gb.reshape(1, LANES), gb.reshape(LANES, 1))
    main3 = main.reshape(b, s, 2 * ML_WP + MEM_W)
    wqkv = jnp.pad(ml_w_qkv[0], ((0, 0), (0, 0), (0, ML_DHP - ML_DH), (0, ML_DHP - ML_DH))).astype(BF16)
    q, k, v, xc = _conv_qkv(main3, _pad_heads(ml_conv_w[0], 1), _pad_heads(row(ml_conv_b[0]), 1),
                            wqkv[0], jnp.swapaxes(wqkv[1], 1, 2), wqkv[2])
    y_ml = _mlstm(q, k, v, acol.reshape(b, s, LANES), arow, main3, xc,
                  _pad_heads(row(ml_norm_g[0]), 1), _pad_heads(row(ml_skip[0]), 1))
    wo = w_out[1]
    xr = _outproj_ln(y_ml.reshape(n, ML_WP), main, 2 * ML_WP // MEM_W, mem_k3, mem_v3,
                     _pad_heads(wo[:ML_W], 0).astype(BF16), wo[ML_W:].astype(BF16), x2,
                     row(ln_g[1, 0]), row(ln_b[1, 0]))
    x2 = _moe_ln(xr, rw, rb, exp_w_gate, exp_w_up, exp_w_down, 1, row(ln_g[1, 1]), row(ln_b[1, 1]))
    return x2.reshape(b, s, d)
```

```python
import functools

import numpy as np
import jax
import jax.numpy as jnp
from jax import lax
from jax.experimental import pallas as pl
from jax.experimental.pallas import tpu as pltpu

F32 = jnp.float32
BF16 = jnp.bfloat16
I32 = jnp.int32

D_MODEL = 1024
DEPTH = 2
GRID_W = 64
MEM_HEADS = 4
MEM_DH = 64
MEM_W = MEM_HEADS * MEM_DH
NA_HEADS = 12
NA_DH = 64
NA_W = NA_HEADS * NA_DH
WIN_H = 8
WIN_W = 16
ML_HEADS = 4
ML_DH = 192
ML_DHP = 256
ML_W = ML_HEADS * ML_DH
ML_WP = ML_HEADS * ML_DHP
CONV_K = 5
CHUNK = 256
N_EXPERTS = 16
N_GROUPS = 4
EXPERTS_PER_GROUP = N_EXPERTS // N_GROUPS
D_EXPERT = 512
ALPHA = (2 * DEPTH) ** 0.25
LN_EPS = 1e-5
NEG = -1e30

LANES = 128
SUBLANES = 8
ROW_CHUNKS = D_MODEL // LANES
MOE_BM = 512
MOE_TILE = 512
ML_HPS = 2
NA_ROWS_PER_STEP = 16
ROW_TILE = 1024
PLACE_UNROLL = 8
VMEM_LIMIT = 48 * 1024 * 1024


def _cparams(*sem):
    return pltpu.CompilerParams(dimension_semantics=sem, vmem_limit_bytes=VMEM_LIMIT)


def _dot(a, b):
    return jnp.dot(a, b, preferred_element_type=F32)


def _dot_nt(a, b, precision=None):
    return lax.dot_general(a, b, (((1,), (1,)), ((), ())), precision=precision,
                           preferred_element_type=F32)


def _ln(z, g, b):
    mu = jnp.mean(z, axis=-1, keepdims=True)
    zc = z - mu
    var = jnp.mean(zc * zc, axis=-1, keepdims=True)
    return zc * lax.rsqrt(var + LN_EPS) * g + b


def _silu(x):
    return x * jax.nn.sigmoid(x)


def _read_rows(ref, n):
    return jnp.concatenate([ref[pl.ds(j, n, stride=ROW_CHUNKS), :] for j in range(ROW_CHUNKS)], axis=1)


def _write_rows(ref, val, n):
    for j in range(ROW_CHUNKS):
        ref[pl.ds(j, n, stride=ROW_CHUNKS), :] = val[:, j * LANES:(j + 1) * LANES]


def _memkv_kernel(m_ref, g_ref, b_ref, w_ref, k_ref, v_ref):
    z = _ln(m_ref[...], g_ref[...], b_ref[...])
    kv = _dot(z.astype(BF16), w_ref[...])
    k_ref[...] = kv[:, :MEM_W].astype(BF16)
    v_ref[...] = kv[:, MEM_W:].astype(BF16)


def _memkv(mem2, g, b, w):
    n = mem2.shape[0]
    tm = min(ROW_TILE, n)
    return pl.pallas_call(
        _memkv_kernel,
        grid=(n // tm,),
        in_specs=[pl.BlockSpec((tm, D_MODEL), lambda i: (i, 0)),
                  pl.BlockSpec((1, D_MODEL), lambda i: (0, 0)),
                  pl.BlockSpec((1, D_MODEL), lambda i: (0, 0)),
                  pl.BlockSpec((D_MODEL, 2 * MEM_W), lambda i: (0, 0))],
        out_specs=[pl.BlockSpec((tm, MEM_W), lambda i: (i, 0)),
                   pl.BlockSpec((tm, MEM_W), lambda i: (i, 0))],
        out_shape=[jax.ShapeDtypeStruct((n, MEM_W), BF16)] * 2,
        compiler_params=_cparams("parallel"),
        name="memkv",
    )(mem2, g, b, w)


def _proj_kernel(x_ref, w_ref, o_ref):
    o_ref[...] = _dot(x_ref[...].astype(BF16), w_ref[...]).astype(o_ref.dtype)


def _proj(x2, w, tm=ROW_TILE):
    n, k = x2.shape
    nout = w.shape[1]
    return pl.pallas_call(
        _proj_kernel,
        grid=(n // tm,),
        in_specs=[pl.BlockSpec((tm, k), lambda i: (i, 0)),
                  pl.BlockSpec((k, nout), lambda i: (0, 0))],
        out_specs=pl.BlockSpec((tm, nout), lambda i: (i, 0)),
        out_shape=jax.ShapeDtypeStruct((n, nout), BF16),
        compiler_params=_cparams("parallel"),
        name="in_proj",
    )(x2, w)


def _split3(x):
    hi = x.astype(BF16)
    r1 = x - hi.astype(F32)
    mid = r1.astype(BF16)
    lo = (r1 - mid.astype(F32)).astype(BF16)
    return hi, mid, lo


def _proj_gates_kernel(x_ref, w_ref, wg_ref, wgt_ref, gbc_ref, gbr_ref, o_ref, g_ref, gt_ref, *, tm):
    xb = x_ref[...].astype(BF16)
    o_ref[...] = _dot(xb, w_ref[...]).astype(BF16)
    gcol = _dot(xb, wg_ref[...]) + gbc_ref[...]
    grow = _dot_nt(wgt_ref[...], xb) + gbr_ref[...]
    lane = lax.broadcasted_iota(I32, (1, LANES), 1)
    sub = lax.broadcasted_iota(I32, (LANES, 1), 0)
    ti = lax.broadcasted_iota(I32, (CHUNK, CHUNK), 0)
    tj = lax.broadcasted_iota(I32, (CHUNK, CHUNK), 1)
    lower = (tj <= ti).astype(BF16)
    upper = (ti <= tj).astype(BF16)

    def pick(idx, pre, suf, raw):
        fwd = jnp.logical_and(idx >= ML_HEADS, idx < 2 * ML_HEADS)
        bwd = jnp.logical_and(idx >= 3 * ML_HEADS, idx < 4 * ML_HEADS)
        return jnp.where(fwd, pre, jnp.where(bwd, suf, raw))

    for c in range(tm // CHUNK):
        tc = slice(c * CHUNK, (c + 1) * CHUNK)
        g = gcol[tc, :]
        ls = jax.nn.log_sigmoid(g)
        pre = sum(_dot(lower, part) for part in _split3(ls))
        suf = jnp.sum(ls, axis=0, keepdims=True) - pre + ls
        g_ref[tc, :] = pick(lane, pre, suf, g)
        g = grow[:, tc]
        ls = jax.nn.log_sigmoid(g)
        pre = sum(_dot(part, upper) for part in _split3(ls))
        suf = jnp.sum(ls, axis=1, keepdims=True) - pre + ls
        gt_ref[:, tc] = pick(sub, pre, suf, g)


def _proj_gates(x2, w, wg, wgt, gbc, gbr, tm=ROW_TILE):
    n, k = x2.shape
    nout = w.shape[1]
    return pl.pallas_call(
        functools.partial(_proj_gates_kernel, tm=tm),
        grid=(n // tm,),
        in_specs=[pl.BlockSpec((tm, k), lambda i: (i, 0)),
                  pl.BlockSpec((k, nout), lambda i: (0, 0)),
                  pl.BlockSpec((k, LANES), lambda i: (0, 0)),
                  pl.BlockSpec((LANES, k), lambda i: (0, 0)),
                  pl.BlockSpec((1, LANES), lambda i: (0, 0)),
                  pl.BlockSpec((LANES, 1), lambda i: (0, 0))],
        out_specs=[pl.BlockSpec((tm, nout), lambda i: (i, 0)),
                   pl.BlockSpec((tm, LANES), lambda i: (i, 0)),
                   pl.BlockSpec((LANES, tm), lambda i: (0, i))],
        out_shape=[jax.ShapeDtypeStruct((n, nout), BF16),
                   jax.ShapeDtypeStruct((n, LANES), F32),
                   jax.ShapeDtypeStruct((LANES, n), F32)],
        compiler_params=_cparams("parallel"),
        name="in_proj_gates",
    )(x2, w, wg, wgt, gbc, gbr)


def _na_kernel(q_ref, k_ref, v_ref, tbl_ref, o_ref, *, rows):
    lane = lax.broadcasted_iota(I32, (1, LANES), 1)
    first = lane < NA_DH
    nkeys = WIN_H * GRID_W

    def rows_step(i, carry):
        rr = [i * NA_ROWS_PER_STEP + u for u in range(NA_ROWS_PER_STEP)]
        rss = [jnp.clip(r - WIN_H // 2, 0, rows - WIN_H) for r in rr]
        scores = []
        for r, rs in zip(rr, rss):
            q = q_ref[0, pl.ds(pl.multiple_of(r * GRID_W, GRID_W), GRID_W), :]
            q = q * jnp.asarray(NA_DH ** -0.5, BF16)
            q2 = jnp.concatenate([jnp.where(first, q, jnp.zeros_like(q)),
                                  jnp.where(first, jnp.zeros_like(q), q)], axis=0)
            k = k_ref[0, pl.ds(pl.multiple_of(rs * GRID_W, GRID_W), nkeys), :]
            dr0 = rs - r + WIN_H - 1
            bias = jnp.concatenate(
                [jnp.concatenate([tbl_ref[0, half, dr0 + 2 * m] for m in range(WIN_H // 2)], axis=1)
                 for half in range(2)], axis=0)
            scores.append(_dot_nt(q2, k) + bias)
        probs = []
        for s in scores:
            p = jnp.exp(s - jnp.max(s, axis=-1, keepdims=True))
            probs.append((p.astype(BF16), jnp.sum(p, axis=-1, keepdims=True)))
        for r, rs, (p, l) in zip(rr, rss, probs):
            v = v_ref[0, pl.ds(pl.multiple_of(rs * GRID_W, GRID_W), nkeys), :]
            o = _dot(p, v) / l
            o = jnp.where(first, o[:GRID_W], o[GRID_W:])
            o_ref[0, pl.ds(pl.multiple_of(r * GRID_W, GRID_W), GRID_W), :] = o.astype(o_ref.dtype)
        return carry

    lax.fori_loop(0, rows // NA_ROWS_PER_STEP, rows_step, 0)


def _na_bias_table(rpb):
    qc = np.arange(GRID_W)[:, None]
    kc = np.arange(GRID_W)[None, :]
    cs = np.clip(qc - WIN_W // 2, 0, GRID_W - WIN_W)
    col_in = (kc >= cs) & (kc < cs + WIN_W)
    side = GRID_W - WIN_W
    wide = jnp.pad(rpb, ((0, 0), (0, 0), (side, side)))
    t = jnp.stack([wide[:, :, GRID_W - 1 - q:2 * GRID_W - 1 - q] for q in range(GRID_W)], axis=2)
    t = jnp.where(col_in, t, NEG).astype(F32)
    t2 = jnp.concatenate([t[:, :-1], t[:, 1:]], axis=-1)
    return t2.reshape(NA_HEADS // 2, 2, 2 * WIN_H - 2, GRID_W, 2 * GRID_W)


def _na_attention(h3, tbl):
    b, s, _ = h3.shape
    rows = s // GRID_W
    npair = NA_HEADS // 2
    return pl.pallas_call(
        functools.partial(_na_kernel, rows=rows),
        grid=(b, npair),
        in_specs=[pl.BlockSpec((1, s, LANES), lambda i, p: (i, 0, p)),
                  pl.BlockSpec((1, s, LANES), lambda i, p: (i, 0, npair + p)),
                  pl.BlockSpec((1, s, LANES), lambda i, p: (i, 0, 2 * npair + p)),
                  pl.BlockSpec((1, 2, 2 * WIN_H - 2, GRID_W, 2 * GRID_W), lambda i, p: (p, 0, 0, 0, 0))],
        out_specs=pl.BlockSpec((1, s, LANES), lambda i, p: (i, 0, p)),
        out_shape=jax.ShapeDtypeStruct((b, s, NA_W), BF16),
        compiler_params=_cparams("parallel", "parallel"),
        name="na_attention",
    )(h3, h3, h3, tbl)


def _outproj_ln_kernel(ya_ref, qm_ref, mk_ref, mv_ref, wa_ref, wm_ref, x_ref, g_ref, b_ref, or_ref, *, tm):
    lane = lax.broadcasted_iota(I32, (1, LANES), 1)
    first = lane < MEM_DH
    q = qm_ref[...] * jnp.asarray(MEM_DH ** -0.5, BF16)
    cols = [slice(p * LANES, (p + 1) * LANES) for p in range(MEM_HEADS // 2)]
    scores = []
    for c in cols:
        qp = q[:, c]
        q2 = jnp.concatenate([jnp.where(first, qp, jnp.zeros_like(qp)),
                              jnp.where(first, jnp.zeros_like(qp), qp)], axis=0)
        scores.append(_dot_nt(q2, mk_ref[0, :, c]))
    probs = []
    for s in scores:
        p = jnp.exp(s - jnp.max(s, axis=-1, keepdims=True))
        probs.append((p.astype(BF16), jnp.sum(p, axis=-1, keepdims=True)))
    outs = []
    for c, (p, l) in zip(cols, probs):
        o = _dot(p, mv_ref[0, :, c]) / l
        outs.append(jnp.where(first, o[:tm], o[tm:]))
    ym = jnp.concatenate(outs, axis=1).astype(BF16)
    acc = _dot(ya_ref[...], wa_ref[...]) + _dot(ym, wm_ref[...])
    _write_rows(or_ref, _ln(ALPHA * x_ref[...] + acc, g_ref[...], b_ref[...]), tm)


def _outproj_ln(ya, h2, qm_block, mem_k3, mem_v3, wa, wm, x2, g, b, tm=ROW_TILE):
    n = x2.shape[0]
    ka = ya.shape[1]
    nb, nm, _ = mem_k3.shape
    per_batch = n // nb // tm
    full = lambda shape: pl.BlockSpec(shape, lambda i: (0,) * len(shape))
    return pl.pallas_call(
        functools.partial(_outproj_ln_kernel, tm=tm),
        grid=(n // tm,),
        in_specs=[pl.BlockSpec((tm, ka), lambda i: (i, 0)),
                  pl.BlockSpec((tm, MEM_W), lambda i: (i, qm_block)),
                  pl.BlockSpec((1, nm, MEM_W), lambda i: (i // per_batch, 0, 0)),
                  pl.BlockSpec((1, nm, MEM_W), lambda i: (i // per_batch, 0, 0)),
                  full((ka, D_MODEL)), full((MEM_W, D_MODEL)),
                  pl.BlockSpec((tm, D_MODEL), lambda i: (i, 0)),
                  full((1, D_MODEL)), full((1, D_MODEL))],
        out_specs=pl.BlockSpec((tm * ROW_CHUNKS, LANES), lambda i: (i, 0)),
        out_shape=jax.ShapeDtypeStruct((n * ROW_CHUNKS, LANES), F32),
        compiler_params=_cparams("parallel"),
        name="outproj_ln",
    )(ya, h2, mem_k3, mem_v3, wa, wm, x2, g, b)


def _router_kernel(x_ref, rwh_ref, rwl_ref, rb_ref, lpos_ref, w_ref, cnt_ref, tcnt_ref, toff_ref, tbef_ref, *, tm):
    @pl.when(pl.program_id(0) == 0)
    def _():
        cnt_ref[...] = jnp.zeros_like(cnt_ref)

    x = _read_rows(x_ref, tm)
    xh = x.astype(BF16)
    xl = (x - xh.astype(F32)).astype(BF16)
    logits_t = _dot(xh, rwh_ref[...]) + (_dot(xh, rwl_ref[...]) + _dot(xl, rwh_ref[...]))
    logits = logits_t.T[:N_EXPERTS]
    scores = jax.nn.sigmoid(logits)
    biased = scores + rb_ref[...]
    bv = [biased[e:e + 1, :] for e in range(N_EXPERTS)]
    sv = [scores[e:e + 1, :] for e in range(N_EXPERTS)]

    grp = []
    for g in range(N_GROUPS):
        m = bv[g * EXPERTS_PER_GROUP:(g + 1) * EXPERTS_PER_GROUP]
        best = None
        for a in range(EXPERTS_PER_GROUP):
            for c in range(a + 1, EXPERTS_PER_GROUP):
                pair = m[a] + m[c]
                best = pair if best is None else jnp.maximum(best, pair)
        grp.append(best)
    gsel = jnp.zeros((1, tm), I32)
    gbest = grp[0]
    for g in range(1, N_GROUPS):
        better = grp[g] > gbest
        gsel = jnp.where(better, g, gsel)
        gbest = jnp.where(better, grp[g], gbest)

    def pick(vals, j):
        out = vals[j]
        for g in range(1, N_GROUPS):
            out = jnp.where(gsel == g, vals[g * EXPERTS_PER_GROUP + j], out)
        return out

    cb = [pick(bv, j) for j in range(EXPERTS_PER_GROUP)]
    cs = [pick(sv, j) for j in range(EXPERTS_PER_GROUP)]
    i1 = jnp.zeros((1, tm), I32)
    m1 = cb[0]
    s1 = cs[0]
    for j in range(1, EXPERTS_PER_GROUP):
        gt = cb[j] > m1
        i1 = jnp.where(gt, j, i1)
        m1 = jnp.where(gt, cb[j], m1)
        s1 = jnp.where(gt, cs[j], s1)
    i2 = jnp.zeros((1, tm), I32)
    m2 = jnp.full((1, tm), -jnp.inf, F32)
    s2 = jnp.zeros((1, tm), F32)
    for j in range(EXPERTS_PER_GROUP):
        ok = jnp.logical_and(i1 != j, cb[j] > m2)
        i2 = jnp.where(ok, j, i2)
        m2 = jnp.where(ok, cb[j], m2)
        s2 = jnp.where(ok, cs[j], s2)
    e1 = gsel * EXPERTS_PER_GROUP + i1
    e2 = gsel * EXPERTS_PER_GROUP + i2
    tot = s1 + s2
    w_ref[...] = jnp.concatenate([s1 / tot, s2 / tot], axis=0)

    i = pl.program_id(0)
    eio = lax.broadcasted_iota(I32, (N_EXPERTS, tm), 0)
    oh1 = eio == e1
    oh2 = eio == e2
    ohs = jnp.logical_or(oh1, oh2).astype(F32)
    before = (lax.broadcasted_iota(I32, (tm, tm), 0) < lax.broadcasted_iota(I32, (tm, tm), 1))
    pre = _dot(ohs.astype(BF16), before.astype(BF16))
    tile_cnt = jnp.sum(ohs, axis=1, keepdims=True)
    offs = []
    acc = jnp.zeros((1, 1), F32)
    for e in range(N_EXPERTS):
        offs.append(acc)
        acc = acc + tile_cnt[e:e + 1, :]
    tile_off = jnp.concatenate(offs, axis=0)
    pos = tile_off + pre
    p1 = jnp.sum(jnp.where(oh1, pos, 0.0), axis=0, keepdims=True)
    p2 = jnp.sum(jnp.where(oh2, pos, 0.0), axis=0, keepdims=True)
    lpos_ref[...] = jnp.concatenate([p1, p2], axis=0).astype(I32) * ROW_CHUNKS

    @pl.when(i == 0)
    def _():
        for ref in (tcnt_ref, toff_ref, tbef_ref):
            ref[...] = jnp.zeros_like(ref)

    here = lax.broadcasted_iota(I32, (1, LANES), 1) == i
    tcnt_ref[...] = jnp.where(here, tile_cnt, tcnt_ref[...])
    toff_ref[...] = jnp.where(here, tile_off, toff_ref[...])
    tbef_ref[...] = jnp.where(here, cnt_ref[:, 0:1], tbef_ref[...])
    cnt_ref[...] += tile_cnt


def _router(xr, rw, rb, tm):
    n = xr.shape[0] // ROW_CHUNKS
    assert n // tm <= LANES
    table = pl.BlockSpec((N_EXPERTS, LANES), lambda i: (0, 0))
    return pl.pallas_call(
        functools.partial(_router_kernel, tm=tm),
        grid=(n // tm,),
        in_specs=[pl.BlockSpec((tm * ROW_CHUNKS, LANES), lambda i: (i, 0)),
                  pl.BlockSpec((D_MODEL, LANES), lambda i: (0, 0)),
                  pl.BlockSpec((D_MODEL, LANES), lambda i: (0, 0)),
                  pl.BlockSpec((N_EXPERTS, 1), lambda i: (0, 0))],
        out_specs=[pl.BlockSpec((2, tm), lambda i: (0, i)),
                   pl.BlockSpec((2, tm), lambda i: (0, i)),
                   table, table, table, table],
        out_shape=[jax.ShapeDtypeStruct((2, n), I32),
                   jax.ShapeDtypeStruct((2, n), F32)]
                  + [jax.ShapeDtypeStruct((N_EXPERTS, LANES), F32)] * 4,
        compiler_params=_cparams("arbitrary"),
        name="router",
    )(xr, rw[0], rw[1], rb)


def _plan_kernel(cnt_ref, tbef_ref, meta_ref, rstart_ref, *, nbl):
    shift = MOE_BM.bit_length() - 1
    cnt = cnt_ref[...].astype(I32)
    padded = ((cnt + (MOE_BM - 1)) >> shift) << shift
    starts = []
    acc = jnp.zeros((1, LANES), I32)
    for e in range(N_EXPERTS):
        starts.append(acc)
        acc = acc + padded[e:e + 1, :]
    pad_start = jnp.concatenate(starts, axis=0)
    pad_end = pad_start + padded
    rstart_ref[...] = pad_start + tbef_ref[...].astype(I32)
    blk0 = lax.broadcasted_iota(I32, (N_EXPERTS, nbl), 1) * MOE_BM
    block_e = jnp.sum((pad_end[:, 0:1] <= blk0).astype(I32), axis=0, keepdims=True)
    block_e = jnp.minimum(block_e, N_EXPERTS - 1)
    n_used = jnp.broadcast_to(acc[:, 0:1] >> shift, (1, nbl))
    diag = lax.broadcasted_iota(I32, (N_EXPERTS, nbl), 0) == lax.broadcasted_iota(I32, (N_EXPERTS, nbl), 1)
    fill_lo = jnp.sum(jnp.where(diag, (pad_start + cnt)[:, 0:1], 0), axis=0, keepdims=True)
    fill_hi = jnp.sum(jnp.where(diag, pad_end[:, 0:1], 0), axis=0, keepdims=True)
    meta_ref[...] = jnp.concatenate([block_e, n_used, fill_lo, fill_hi, jnp.zeros((SUBLANES - 4, nbl), I32)],
                                    axis=0)


def _plan(cnt, tbef, n_blocks):
    nbl = -(-n_blocks // LANES) * LANES
    table = pl.BlockSpec((N_EXPERTS, LANES), lambda i: (0, 0))
    return pl.pallas_call(
        functools.partial(_plan_kernel, nbl=nbl),
        grid=(1,),
        in_specs=[table, table],
        out_specs=[pl.BlockSpec((SUBLANES, nbl), lambda i: (0, 0)), table],
        out_shape=[jax.ShapeDtypeStruct((SUBLANES, nbl), I32),
                   jax.ShapeDtypeStruct((N_EXPERTS, LANES), I32)],
        compiler_params=_cparams("arbitrary"),
        name="moe_plan",
    )(cnt, tbef)


def _rows(ref, row, nrows):
    return ref.at[pl.ds(pl.multiple_of(row * ROW_CHUNKS, ROW_CHUNKS), nrows * ROW_CHUNKS), :]


def _rows_wait(src_hbm, buf, sem):
    pltpu.make_async_copy(src_hbm.at[pl.ds(0, buf.shape[0]), :], buf, sem).wait()


def _copy_pieces(src, src_row, dst, dst_row, count, max_rows, sem, wait=False):
    bit = max_rows.bit_length() - 1
    while bit >= 0:
        size = 1 << bit
        done = (count >> (bit + 1)) << (bit + 1)

        @pl.when(((count >> bit) & 1) == 1)
        def _():
            cp = pltpu.make_async_copy(_rows(src, src_row + done, size), _rows(dst, dst_row + done, size), sem)
            cp.start()
            if wait:
                cp.wait()

        bit -= 1


def _tile_runs(tcnt_ref, toff_ref, rstart_ref, tile, buf, hbm, sem, *, to_hbm, tm):
    def per_expert(e, carry):
        k = tile * N_EXPERTS + e
        if to_hbm:
            _copy_pieces(buf, toff_ref[k], hbm, rstart_ref[k], tcnt_ref[k], tm, sem)
        else:
            _copy_pieces(hbm, rstart_ref[k], buf, toff_ref[k], tcnt_ref[k], tm, sem)
        return carry

    lax.fori_loop(0, N_EXPERTS, per_expert, 0)


def _dispatch_kernel(lpos_ref, tcnt_ref, toff_ref, rstart_ref, flo_ref, fhi_ref, nu_ref, x_ref, xs_hbm,
                     s0, s1, zbuf, sem, zsem, *, n, tm, n_blocks):
    i = pl.program_id(0)
    nt = pl.num_programs(0)
    bufs = (s0, s1)
    unroll = PLACE_UNROLL

    for slot in range(2):
        @pl.when(i % 2 == slot)
        def _():
            buf = bufs[slot]

            @pl.when(i >= 2)
            def _():
                _rows_wait(xs_hbm, buf, sem.at[slot])

            def place(c, carry):
                tok = i * tm + c * unroll
                src = pl.multiple_of(c * (unroll * ROW_CHUNKS), unroll * ROW_CHUNKS)
                for u in range(unroll):
                    v = x_ref[pl.ds(src + u * ROW_CHUNKS, ROW_CHUNKS), :]
                    for k in range(2):
                        p = lpos_ref[k * n + tok + u]
                        buf[pl.ds(pl.multiple_of(p, ROW_CHUNKS), ROW_CHUNKS), :] = v
                return carry

            lax.fori_loop(0, tm // unroll, place, 0)
            _tile_runs(tcnt_ref, toff_ref, rstart_ref, i, buf, xs_hbm, sem.at[slot], to_hbm=True, tm=tm)

    @pl.when(i == nt - 1)
    def _():
        for slot in range(2):
            @pl.when(nt > slot)
            def _():
                _rows_wait(xs_hbm, bufs[slot], sem.at[slot])

        zbuf[...] = jnp.zeros_like(zbuf)
        for e in range(N_EXPERTS):
            _copy_pieces(zbuf, 0, xs_hbm, flo_ref[e], fhi_ref[e] - flo_ref[e], MOE_BM // 2, zsem, wait=True)

        def zero_block(j, carry):
            cp = pltpu.make_async_copy(zbuf, _rows(xs_hbm, j * MOE_BM, MOE_BM), zsem)
            cp.start()
            cp.wait()
            return carry

        lax.fori_loop(nu_ref[0], n_blocks, zero_block, 0)


def _dispatch(lpos_flat, tcnt, toff, rstart, fill_lo, fill_hi, n_used, xr, n_blocks, tm):
    n = xr.shape[0] // ROW_CHUNKS
    return pl.pallas_call(
        functools.partial(_dispatch_kernel, n=n, tm=tm, n_blocks=n_blocks),
        grid_spec=pltpu.PrefetchScalarGridSpec(
            num_scalar_prefetch=7,
            grid=(n // tm,),
            in_specs=[pl.BlockSpec((tm * ROW_CHUNKS, LANES), lambda i, *_: (i, 0))],
            out_specs=pl.BlockSpec(memory_space=pl.ANY),
            scratch_shapes=[pltpu.VMEM((2 * tm * ROW_CHUNKS, LANES), F32),
                            pltpu.VMEM((2 * tm * ROW_CHUNKS, LANES), F32),
                            pltpu.VMEM((MOE_BM * ROW_CHUNKS, LANES), F32),
                            pltpu.SemaphoreType.DMA((2,)),
                            pltpu.SemaphoreType.DMA(())]),
        out_shape=jax.ShapeDtypeStruct((n_blocks * MOE_BM * ROW_CHUNKS, LANES), F32),
        compiler_params=_cparams("arbitrary"),
        name="moe_dispatch",
    )(lpos_flat, tcnt, toff, rstart, fill_lo, fill_hi, n_used, xr)


def _experts_kernel(be_ref, nu_ref, xs_ref, wg_ref, wu_ref, wd_ref, y_ref, wgb, wub, wdb):
    j = pl.program_id(0)
    used = j < nu_ref[0]

    @pl.when(jnp.logical_and(used, jnp.logical_or(j == 0, be_ref[j] != be_ref[jnp.maximum(j - 1, 0)])))
    def _():
        wgb[...] = wg_ref[0, 0].astype(BF16)
        wub[...] = wu_ref[0, 0].astype(BF16)
        wdb[...] = wd_ref[0, 0].astype(BF16)

    @pl.when(used)
    def _():
        x = _read_rows(xs_ref, MOE_BM).astype(BF16)
        h = _silu(_dot(x, wgb[...])) * _dot(x, wub[...])
        _write_rows(y_ref, _dot(h.astype(BF16), wdb[...]), MOE_BM)

    @pl.when(jnp.logical_not(used))
    def _():
        y_ref[...] = jnp.zeros_like(y_ref)


def _experts(block_e, n_used, xs, wg, wu, wd, layer):
    n_blocks = block_e.shape[0]

    def last_used(j, nu):
        return jnp.minimum(j, nu[0] - 1)

    def wblk(j, be, nu):
        return (layer, be[last_used(j, nu)], 0, 0)

    return pl.pallas_call(
        _experts_kernel,
        grid_spec=pltpu.PrefetchScalarGridSpec(
            num_scalar_prefetch=2,
            grid=(n_blocks,),
            in_specs=[pl.BlockSpec((MOE_BM * ROW_CHUNKS, LANES), lambda j, be, nu: (last_used(j, nu), 0)),
                      pl.BlockSpec((1, 1, D_MODEL, D_EXPERT), wblk),
                      pl.BlockSpec((1, 1, D_MODEL, D_EXPERT), wblk),
                      pl.BlockSpec((1, 1, D_EXPERT, D_MODEL), wblk)],
            out_specs=pl.BlockSpec((MOE_BM * ROW_CHUNKS, LANES), lambda j, be, nu: (j, 0)),
            scratch_shapes=[pltpu.VMEM((D_MODEL, D_EXPERT), BF16), pltpu.VMEM((D_MODEL, D_EXPERT), BF16),
                            pltpu.VMEM((D_EXPERT, D_MODEL), BF16)]),
        out_shape=jax.ShapeDtypeStruct(xs.shape, F32),
        compiler_params=_cparams("arbitrary"),
        name="moe_experts",
    )(block_e, n_used, xs, wg, wu, wd)


def _combine_ln_kernel(lpos_ref, tcnt_ref, toff_ref, rstart_ref, y_hbm, x_ref, w1_ref, w2_ref, g_ref, b_ref, o_ref,
                       r0, r1, u1, u2, sem, *, n, tm):
    i = pl.program_id(0)
    nt = pl.num_programs(0)
    bufs = (r0, r1)
    unroll = PLACE_UNROLL

    def fetch(tile, slot):
        _tile_runs(tcnt_ref, toff_ref, rstart_ref, tile, bufs[slot], y_hbm, sem.at[slot], to_hbm=False, tm=tm)

    @pl.when(i == 0)
    def _():
        fetch(0, 0)

    for slot in range(2):
        @pl.when(i % 2 == slot)
        def _():
            @pl.when(i + 1 < nt)
            def _():
                fetch(i + 1, 1 - slot)

            buf = bufs[slot]
            _rows_wait(y_hbm, buf, sem.at[slot])

            def place(c, carry):
                tok = i * tm + c * unroll
                dst0 = pl.multiple_of(c * (unroll * ROW_CHUNKS), unroll * ROW_CHUNKS)
                for u in range(unroll):
                    dst = pl.ds(dst0 + u * ROW_CHUNKS, ROW_CHUNKS)
                    for k, out in enumerate((u1, u2)):
                        p = lpos_ref[k * n + tok + u]
                        out[dst, :] = buf[pl.ds(pl.multiple_of(p, ROW_CHUNKS), ROW_CHUNKS), :]
                return carry

            lax.fori_loop(0, tm // unroll, place, 0)
            moe = w1_ref[...] * _read_rows(u1, tm) + w2_ref[...] * _read_rows(u2, tm)
            o_ref[...] = _ln(ALPHA * _read_rows(x_ref, tm) + moe, g_ref[...], b_ref[...])


def _combine_ln(lpos_flat, tcnt, toff, rstart, y, xr, w1, w2, g, b, tm):
    n = xr.shape[0] // ROW_CHUNKS
    return pl.pallas_call(
        functools.partial(_combine_ln_kernel, n=n, tm=tm),
        grid_spec=pltpu.PrefetchScalarGridSpec(
            num_scalar_prefetch=4,
            grid=(n // tm,),
            in_specs=[pl.BlockSpec(memory_space=pl.ANY),
                      pl.BlockSpec((tm * ROW_CHUNKS, LANES), lambda i, *_: (i, 0)),
                      pl.BlockSpec((tm, 1), lambda i, *_: (i, 0)),
                      pl.BlockSpec((tm, 1), lambda i, *_: (i, 0)),
                      pl.BlockSpec((1, D_MODEL), lambda i, *_: (0, 0)),
                      pl.BlockSpec((1, D_MODEL), lambda i, *_: (0, 0))],
            out_specs=pl.BlockSpec((tm, D_MODEL), lambda i, *_: (i, 0)),
            scratch_shapes=[pltpu.VMEM((2 * tm * ROW_CHUNKS, LANES), F32)] * 2
                           + [pltpu.VMEM((tm * ROW_CHUNKS, LANES), F32)] * 2
                           + [pltpu.SemaphoreType.DMA((2,))]),
        out_shape=jax.ShapeDtypeStruct((n, D_MODEL), F32),
        compiler_params=_cparams("arbitrary"),
        name="moe_combine_ln",
    )(lpos_flat, tcnt, toff, rstart, y, xr, w1, w2, g, b)


def _moe_ln(xr, rw, rb, wg, wu, wd, layer, g, b):
    n = xr.shape[0] // ROW_CHUNKS
    n_blocks = (2 * n) // MOE_BM + N_EXPERTS
    tm = MOE_TILE
    nt = n // tm
    lpos, w, cnt, tcnt, toff, tbef = _router(xr, rw, rb, tm)
    meta, rstart = _plan(cnt, tbef, n_blocks)
    block_e = meta[0, :n_blocks]
    n_used = meta[1, :1]

    def per_tile(table):
        return table[:, :nt].T.reshape(nt * N_EXPERTS).astype(I32)

    lpos_flat = lpos.reshape(2 * n)
    tcnt, toff, rstart = per_tile(tcnt), per_tile(toff), per_tile(rstart)
    xs = _dispatch(lpos_flat, tcnt, toff, rstart, meta[2, :N_EXPERTS], meta[3, :N_EXPERTS], n_used, xr,
                   n_blocks, tm)
    y = _experts(block_e, n_used, xs, wg, wu, wd, layer)
    return _combine_ln(lpos_flat, tcnt, toff, rstart, y, xr, w[0].reshape(n, 1), w[1].reshape(n, 1), g, b, tm)


def _conv_qkv_kernel(xm_ref, cw_ref, cb_ref, wq_ref, wk_ref, wv_ref, q_ref, k_ref, v_ref, xc_ref, *, s):
    xm_b = xm_ref[0]
    xm = xm_b.astype(F32)
    cw = cw_ref[...]
    row = lax.broadcasted_iota(I32, (s, 1), 0)
    half = CONV_K // 2
    acc = cb_ref[...] + xm * cw[half:half + 1, :]
    for sh in range(1, half + 1):
        past = jnp.where(row >= sh, pltpu.roll(xm, sh, axis=0), 0.0)
        acc = acc + past * cw[half - sh:half - sh + 1, :]
        nxt = jnp.where(row < s - sh, pltpu.roll(xm, s - sh, axis=0), 0.0)
        acc = acc + nxt * cw[half + sh:half + sh + 1, :]
    xc = _silu(acc).astype(BF16)
    xc_ref[0] = xc
    q_ref[0] = _dot(xc, wq_ref[0]).astype(BF16)
    k_ref[0] = (_dot_nt(wk_ref[0], xc) * (ML_DH ** -0.5)).astype(BF16)
    v = _dot(xm_b, wv_ref[0])
    ones_lane = lax.broadcasted_iota(I32, (1, ML_DHP), 1) == ML_DH
    v_ref[0] = jnp.where(ones_lane, 1.0, v).astype(BF16)


def _conv_qkv(main3, cw, cb, wq, wk_t, wv):
    b, s, _ = main3.shape
    tok = pl.BlockSpec((1, s, ML_DHP), lambda i, h: (i, 0, h))
    wspec = pl.BlockSpec((1, ML_DHP, ML_DHP), lambda i, h: (h, 0, 0))
    tok_shape = jax.ShapeDtypeStruct((b, s, ML_WP), BF16)
    return pl.pallas_call(
        functools.partial(_conv_qkv_kernel, s=s),
        grid=(b, ML_HEADS),
        in_specs=[tok,
                  pl.BlockSpec((CONV_K, ML_DHP), lambda i, h: (0, h)),
                  pl.BlockSpec((1, ML_DHP), lambda i, h: (0, h)),
                  wspec, wspec, wspec],
        out_specs=[tok, pl.BlockSpec((1, ML_DHP, s), lambda i, h: (i, h, 0)), tok, tok],
        out_shape=[tok_shape, jax.ShapeDtypeStruct((b, ML_WP, s), BF16), tok_shape, tok_shape],
        compiler_params=_cparams("parallel", "parallel"),
        name="conv_qkv",
    )(main3, cw, cb, wq, wk_t, wv)


def _mlstm_kernel(q_ref, kt_ref, v_ref, gc_ref, gr_ref, z_ref, xc_ref, ng_ref, sk_ref,
                  y_ref, hf_ref, hb_ref, cf_ref, cb_ref, m_ref, *, s):
    head0 = pl.program_id(1) * ML_HPS
    nc = s // CHUNK
    sub = lax.broadcasted_iota(I32, (LANES, 1), 0)
    gate = lax.broadcasted_iota(I32, (LANES, LANES), 0)
    ti = lax.broadcasted_iota(I32, (CHUNK, CHUNK), 0)
    tj = lax.broadcasted_iota(I32, (CHUNK, CHUNK), 1)

    for ref in (cf_ref, cb_ref, m_ref):
        ref[...] = jnp.zeros_like(ref)

    def intra(c, j, rev):
        t0 = pl.multiple_of(c * CHUNK, CHUNK)
        hl = slice(j * ML_DHP, (j + 1) * ML_DHP)
        qb = q_ref[0, pl.ds(t0, CHUNK), hl]
        kt = kt_ref[0, hl, pl.ds(t0, CHUNK)]
        vb = v_ref[0, pl.ds(t0, CHUNK), hl]
        gc = gc_ref[0, pl.ds(t0, CHUNK), :]
        gr = gr_ref[:, pl.ds(t0, CHUNK)]
        i_idx = head0 + j + (2 * ML_HEADS if rev else 0)
        f_idx = i_idx + ML_HEADS
        allowed = (tj >= ti) if rev else (tj <= ti)
        sel = (gate == f_idx).astype(BF16)
        b_rep = sum(_dot(part, sel) for part in _split3(gc))
        b_row = jnp.sum(jnp.where(sub == f_idx, gr, 0.0), axis=0, keepdims=True)
        i_row = jnp.sum(jnp.where(sub == i_idx, gr, 0.0), axis=0, keepdims=True)
        b_last = (b_rep[0:1, :] if rev else b_rep[CHUNK - 1:CHUNK, :])[:, 0:1]

        b_wide = jnp.concatenate([b_rep] * (CHUNK // LANES), axis=1)
        d = jnp.where(allowed, b_wide - b_row + i_row, NEG)
        m_in = jnp.max(d, axis=1, keepdims=True)
        sc = _dot(qb, kt) * jnp.exp(d - m_in)
        nd_in = _dot(sc.astype(BF16), vb)
        w_row = b_last - b_row + i_row
        return t0, qb, kt, vb, b_rep, b_last, m_in, nd_in, w_row

    def twice(a):
        return jnp.concatenate([a, a], axis=1)

    def update(parts, j, rev):
        t0, qb, kt, vb, b_rep, b_last, m_in, nd_in, w_row = parts
        h_ref, c_ref = (hb_ref, cb_ref) if rev else (hf_ref, cf_ref)
        hl = slice(j * ML_DHP, (j + 1) * ML_DHP)
        mrow = 2 * j + int(rev)
        m = m_ref[mrow:mrow + 1, 0:1]
        cmat = c_ref[j]
        inter = b_rep + m
        m_t = jnp.maximum(m_in, inter)
        a_in = jnp.exp(m_in - m_t)
        iexp = jnp.exp(inter - m_t)
        nd = twice(a_in) * nd_in + twice(iexp) * _dot(qb, cmat.astype(BF16))
        den = nd[:, ML_DH:ML_DH + 1]
        h_ref[pl.ds(t0, CHUNK), hl] = nd * (1.0 / jnp.maximum(jnp.abs(den), jnp.exp(-m_t[:, 0:1])))

        m_new = jnp.maximum(b_last + m, jnp.max(w_row, axis=1, keepdims=True))
        wexp = jnp.exp(w_row - m_new)
        cexp = jnp.exp(b_last + m - m_new)
        kw = (kt.astype(F32) * wexp).astype(BF16)
        c_ref[j] = cexp * cmat + _dot(kw, vb)
        m_ref[mrow:mrow + 1, :] = jnp.broadcast_to(m_new, (1, LANES))

    def step(i, carry):
        for j in range(ML_HPS):
            parts = [intra(nc - 1 - i if rev else i, j, rev) for rev in (False, True)]
            for p, rev in zip(parts, (False, True)):
                update(p, j, rev)
        return carry

    lax.fori_loop(0, nc, step, 0)

    real = lax.broadcasted_iota(I32, (1, ML_DHP), 1) < ML_DH
    tb = CHUNK

    def fin(c, carry):
        t0 = pl.multiple_of(c * tb, tb)
        for j in range(ML_HPS):
            hl = slice(j * ML_DHP, (j + 1) * ML_DHP)
            hs = jnp.where(real, hf_ref[pl.ds(t0, tb), hl] + hb_ref[pl.ds(t0, tb), hl], 0.0)
            mu = jnp.sum(hs, axis=1, keepdims=True) * (1.0 / ML_DH)
            dev = jnp.where(real, hs - mu, 0.0)
            var = jnp.sum(dev * dev, axis=1, keepdims=True) * (1.0 / ML_DH)
            hn = dev * lax.rsqrt(var + LN_EPS) * ng_ref[:, hl]
            xc = xc_ref[0, pl.ds(t0, tb), hl].astype(F32)
            z = z_ref[0, pl.ds(t0, tb), hl].astype(F32)
            y_ref[0, pl.ds(t0, tb), hl] = ((hn + sk_ref[:, hl] * xc) * _silu(z)).astype(BF16)
        return carry

    lax.fori_loop(0, s // tb, fin, 0)


def _mlstm(q, kt, v, gcol3, grow, main3, xc, ng, sk):
    b, s, _ = q.shape
    width = ML_HPS * ML_DHP
    steps = ML_HEADS // ML_HPS
    tok = pl.BlockSpec((1, s, width), lambda i, h: (i, 0, h))
    vec = pl.BlockSpec((1, width), lambda i, h: (0, h))
    return pl.pallas_call(
        functools.partial(_mlstm_kernel, s=s),
        grid=(b, steps),
        in_specs=[tok, pl.BlockSpec((1, width, s), lambda i, h: (i, h, 0)), tok,
                  pl.BlockSpec((1, s, LANES), lambda i, h: (i, 0, 0)),
                  pl.BlockSpec((LANES, s), lambda i, h: (0, i)),
                  pl.BlockSpec((1, s, width), lambda i, h: (i, 0, steps + h)),
                  tok, vec, vec],
        out_specs=tok,
        out_shape=jax.ShapeDtypeStruct((b, s, ML_WP), BF16),
        scratch_shapes=[pltpu.VMEM((s, width), F32), pltpu.VMEM((s, width), F32),
                        pltpu.VMEM((ML_HPS, ML_DHP, ML_DHP), F32), pltpu.VMEM((ML_HPS, ML_DHP, ML_DHP), F32),
                        pltpu.VMEM((SUBLANES, LANES), F32)],
        compiler_params=_cparams("parallel", "parallel"),
        name="mlstm",
    )(q, kt, v, gcol3, grow, main3, xc, ng, sk)


def _pad_heads(a, axis):
    a = jnp.moveaxis(a, axis, -1)
    lead = a.shape[:-1]
    a = a.reshape(lead + (ML_HEADS, ML_DH))
    a = jnp.pad(a, [(0, 0)] * len(lead) + [(0, 0), (0, ML_DHP - ML_DH)])
    return jnp.moveaxis(a.reshape(lead + (ML_WP,)), -1, axis)


def kernel(x, mem, mem_ln_g, mem_ln_b, w_mem_kv, router_w, router_b, na_w_in, na_rpb, ml_w_in, ml_conv_w,
           ml_conv_b, ml_w_qkv, ml_gate_b, ml_norm_g, ml_skip, w_out, ln_g, ln_b, exp_w_gate, exp_w_up,
           exp_w_down):
    b, s, d = x.shape
    n = b * s
    nm = mem.shape[1]
    row = lambda a: a.reshape(1, -1)

    mem_k, mem_v = _memkv(mem.reshape(b * nm, d), row(mem_ln_g), row(mem_ln_b), w_mem_kv.astype(BF16))
    mem_k3 = mem_k.reshape(b, nm, MEM_W)
    mem_v3 = mem_v.reshape(b, nm, MEM_W)
    rw_pad = jnp.pad(router_w, ((0, 0), (0, LANES - N_EXPERTS)))
    rw_hi = rw_pad.astype(BF16)
    rw = (rw_hi, (rw_pad - rw_hi.astype(F32)).astype(BF16))
    rb = router_b.reshape(N_EXPERTS, 1)

    x2 = x.reshape(n, d)

    h0 = _proj(x2, na_w_in[0].astype(BF16)).reshape(b, s, 3 * NA_W + MEM_W)
    y_na = _na_attention(h0, _na_bias_table(na_rpb[0]))
    wo = w_out[0].astype(BF16)
    xr = _outproj_ln(y_na.reshape(n, NA_W), h0.reshape(n, 3 * NA_W + MEM_W), 3 * NA_W // MEM_W, mem_k3, mem_v3,
                     wo[:NA_W], wo[NA_W:], x2, row(ln_g[0, 0]), row(ln_b[0, 0]))
    x2 = _moe_ln(xr, rw, rb, exp_w_gate, exp_w_up, exp_w_down, 0, row(ln_g[0, 1]), row(ln_b[0, 1]))

    w1 = ml_w_in[0]
    w_main = jnp.concatenate([_pad_heads(w1[:, :ML_W], 1), _pad_heads(w1[:, ML_W:2 * ML_W], 1),
                              w1[:, 2 * ML_W + 4 * ML_HEADS:]], axis=1).astype(BF16)
    w_g = jnp.pad(w1[:, 2 * ML_W:2 * ML_W + 4 * ML_HEADS], ((0, 0), (0, LANES - 4 * ML_HEADS))).astype(BF16)
    gb = jnp.pad(ml_gate_b[0].reshape(4 * ML_HEADS), (0, LANES - 4 * ML_HEADS))
    main, acol, arow = _proj_gates(x2, w_main, w_g, w_g.T, gb.reshape(1, LANES), gb.reshape(LANES, 1))
    main3 = main.reshape(b, s, 2 * ML_WP + MEM_W)
    wqkv = jnp.pad(ml_w_qkv[0], ((0, 0), (0, 0), (0, ML_DHP - ML_DH), (0, ML_DHP - ML_DH))).astype(BF16)
    q, k, v, xc = _conv_qkv(main3, _pad_heads(ml_conv_w[0], 1), _pad_heads(row(ml_conv_b[0]), 1),
                            wqkv[0], jnp.swapaxes(wqkv[1], 1, 2), wqkv[2])
    y_ml = _mlstm(q, k, v, acol.reshape(b, s, LANES), arow, main3, xc,
                  _pad_heads(row(ml_norm_g[0]), 1), _pad_heads(row(ml_skip[0]), 1))
    wo = w_out[1]
    xr = _outproj_ln(y_ml.reshape(n, ML_WP), main, 2 * ML_WP // MEM_W, mem_k3, mem_v3,
                     _pad_heads(wo[:ML_W], 0).astype(BF16), wo[ML_W:].astype(BF16), x2,
                     row(ln_g[1, 0]), row(ln_b[1, 0]))
    x2 = _moe_ln(xr, rw, rb, exp_w_gate, exp_w_up, exp_w_down, 1, row(ln_g[1, 1]), row(ln_b[1, 1]))
    return x2.reshape(b, s, d)
```

```python
import functools

import numpy as np
import jax
import jax.numpy as jnp
from jax import lax
from jax.experimental import pallas as pl
from jax.experimental.pallas import tpu as pltpu

F32 = jnp.float32
BF16 = jnp.bfloat16
I32 = jnp.int32

D_MODEL = 1024
DEPTH = 2
GRID_W = 64
MEM_HEADS = 4
MEM_DH = 64
MEM_W = MEM_HEADS * MEM_DH
NA_HEADS = 12
NA_DH = 64
NA_W = NA_HEADS * NA_DH
WIN_H = 8
WIN_W = 16
ML_HEADS = 4
ML_DH = 192
ML_DHP = 256
ML_W = ML_HEADS * ML_DH
ML_WP = ML_HEADS * ML_DHP
CONV_K = 5
CHUNK = 256
N_EXPERTS = 16
N_GROUPS = 4
EXPERTS_PER_GROUP = N_EXPERTS // N_GROUPS
D_EXPERT = 512
ALPHA = (2 * DEPTH) ** 0.25
LN_EPS = 1e-5
NEG = -1e30

LANES = 128
SUBLANES = 8
ROW_CHUNKS = D_MODEL // LANES
MOE_BM = 512
MOE_TILE = 512
ML_HPS = 2
NA_ROWS_PER_STEP = 16
ROW_TILE = 1024
PLACE_UNROLL = 8
VMEM_LIMIT = 48 * 1024 * 1024


def _cparams(*sem):
    return pltpu.CompilerParams(dimension_semantics=sem, vmem_limit_bytes=VMEM_LIMIT)


def _dot(a, b):
    return jnp.dot(a, b, preferred_element_type=F32)


def _dot_nt(a, b, precision=None):
    return lax.dot_general(a, b, (((1,), (1,)), ((), ())), precision=precision,
                           preferred_element_type=F32)


def _ln(z, g, b):
    mu = jnp.mean(z, axis=-1, keepdims=True)
    zc = z - mu
    var = jnp.mean(zc * zc, axis=-1, keepdims=True)
    return zc * lax.rsqrt(var + LN_EPS) * g + b


def _silu(x):
    return x * jax.nn.sigmoid(x)


def _read_rows(ref, n):
    return jnp.concatenate([ref[pl.ds(j, n, stride=ROW_CHUNKS), :] for j in range(ROW_CHUNKS)], axis=1)


def _write_rows(ref, val, n):
    for j in range(ROW_CHUNKS):
        ref[pl.ds(j, n, stride=ROW_CHUNKS), :] = val[:, j * LANES:(j + 1) * LANES]


def _memkv_kernel(m_ref, g_ref, b_ref, w_ref, k_ref, v_ref):
    z = _ln(m_ref[...], g_ref[...], b_ref[...])
    kv = _dot(z.astype(BF16), w_ref[...])
    k_ref[...] = kv[:, :MEM_W].astype(BF16)
    v_ref[...] = kv[:, MEM_W:].astype(BF16)


def _memkv(mem2, g, b, w):
    n = mem2.shape[0]
    tm = min(ROW_TILE, n)
    return pl.pallas_call(
        _memkv_kernel,
        grid=(n // tm,),
        in_specs=[pl.BlockSpec((tm, D_MODEL), lambda i: (i, 0)),
                  pl.BlockSpec((1, D_MODEL), lambda i: (0, 0)),
                  pl.BlockSpec((1, D_MODEL), lambda i: (0, 0)),
                  pl.BlockSpec((D_MODEL, 2 * MEM_W), lambda i: (0, 0))],
        out_specs=[pl.BlockSpec((tm, MEM_W), lambda i: (i, 0)),
                   pl.BlockSpec((tm, MEM_W), lambda i: (i, 0))],
        out_shape=[jax.ShapeDtypeStruct((n, MEM_W), BF16)] * 2,
        compiler_params=_cparams("parallel"),
        name="memkv",
    )(mem2, g, b, w)


def _proj_kernel(x_ref, w_ref, o_ref):
    o_ref[...] = _dot(x_ref[...].astype(BF16), w_ref[...]).astype(o_ref.dtype)


def _proj(x2, w, tm=ROW_TILE):
    n, k = x2.shape
    nout = w.shape[1]
    return pl.pallas_call(
        _proj_kernel,
        grid=(n // tm,),
        in_specs=[pl.BlockSpec((tm, k), lambda i: (i, 0)),
                  pl.BlockSpec((k, nout), lambda i: (0, 0))],
        out_specs=pl.BlockSpec((tm, nout), lambda i: (i, 0)),
        out_shape=jax.ShapeDtypeStruct((n, nout), BF16),
        compiler_params=_cparams("parallel"),
        name="in_proj",
    )(x2, w)


def _split3(x):
    hi = x.astype(BF16)
    r1 = x - hi.astype(F32)
    mid = r1.astype(BF16)
    lo = (r1 - mid.astype(F32)).astype(BF16)
    return hi, mid, lo


def _proj_gates_kernel(x_ref, w_ref, wg_ref, wgt_ref, gbc_ref, gbr_ref, o_ref, g_ref, gt_ref, *, tm):
    xb = x_ref[...].astype(BF16)
    o_ref[...] = _dot(xb, w_ref[...]).astype(BF16)
    gcol = _dot(xb, wg_ref[...]) + gbc_ref[...]
    grow = _dot_nt(wgt_ref[...], xb) + gbr_ref[...]
    lane = lax.broadcasted_iota(I32, (1, LANES), 1)
    sub = lax.broadcasted_iota(I32, (LANES, 1), 0)
    ti = lax.broadcasted_iota(I32, (CHUNK, CHUNK), 0)
    tj = lax.broadcasted_iota(I32, (CHUNK, CHUNK), 1)
    lower = (tj <= ti).astype(BF16)
    upper = (ti <= tj).astype(BF16)

    def pick(idx, pre, suf, raw):
        fwd = jnp.logical_and(idx >= ML_HEADS, idx < 2 * ML_HEADS)
        bwd = jnp.logical_and(idx >= 3 * ML_HEADS, idx < 4 * ML_HEADS)
        return jnp.where(fwd, pre, jnp.where(bwd, suf, raw))

    for c in range(tm // CHUNK):
        tc = slice(c * CHUNK, (c + 1) * CHUNK)
        g = gcol[tc, :]
        ls = jax.nn.log_sigmoid(g)
        pre = sum(_dot(lower, part) for part in _split3(ls))
        suf = jnp.sum(ls, axis=0, keepdims=True) - pre + ls
        g_ref[tc, :] = pick(lane, pre, suf, g)
        g = grow[:, tc]
        ls = jax.nn.log_sigmoid(g)
        pre = sum(_dot(part, upper) for part in _split3(ls))
        suf = jnp.sum(ls, axis=1, keepdims=True) - pre + ls
        gt_ref[:, tc] = pick(sub, pre, suf, g)


def _proj_gates(x2, w, wg, wgt, gbc, gbr, tm=ROW_TILE):
    n, k = x2.shape
    nout = w.shape[1]
    return pl.pallas_call(
        functools.partial(_proj_gates_kernel, tm=tm),
        grid=(n // tm,),
        in_specs=[pl.BlockSpec((tm, k), lambda i: (i, 0)),
                  pl.BlockSpec((k, nout), lambda i: (0, 0)),
                  pl.BlockSpec((k, LANES), lambda i: (0, 0)),
                  pl.BlockSpec((LANES, k), lambda i: (0, 0)),
                  pl.BlockSpec((1, LANES), lambda i: (0, 0)),
                  pl.BlockSpec((LANES, 1), lambda i: (0, 0))],
        out_specs=[pl.BlockSpec((tm, nout), lambda i: (i, 0)),
                   pl.BlockSpec((tm, LANES), lambda i: (i, 0)),
                   pl.BlockSpec((LANES, tm), lambda i: (0, i))],
        out_shape=[jax.ShapeDtypeStruct((n, nout), BF16),
                   jax.ShapeDtypeStruct((n, LANES), F32),
                   jax.ShapeDtypeStruct((LANES, n), F32)],
        compiler_params=_cparams("parallel"),
        name="in_proj_gates",
    )(x2, w, wg, wgt, gbc, gbr)


def _na_kernel(q_ref, k_ref, v_ref, tbl_ref, o_ref, *, rows):
    lane = lax.broadcasted_iota(I32, (1, LANES), 1)
    first = lane < NA_DH
    nkeys = WIN_H * GRID_W

    def rows_step(i, carry):
        rr = [i * NA_ROWS_PER_STEP + u for u in range(NA_ROWS_PER_STEP)]
        rss = [jnp.clip(r - WIN_H // 2, 0, rows - WIN_H) for r in rr]
        scores = []
        for r, rs in zip(rr, rss):
            q = q_ref[0, pl.ds(pl.multiple_of(r * GRID_W, GRID_W), GRID_W), :]
            q = q * jnp.asarray(NA_DH ** -0.5, BF16)
            q2 = jnp.concatenate([jnp.where(first, q, jnp.zeros_like(q)),
                                  jnp.where(first, jnp.zeros_like(q), q)], axis=0)
            k = k_ref[0, pl.ds(pl.multiple_of(rs * GRID_W, GRID_W), nkeys), :]
            dr0 = rs - r + WIN_H - 1
            bias = jnp.concatenate(
                [jnp.concatenate([tbl_ref[0, half, dr0 + 2 * m] for m in range(WIN_H // 2)], axis=1)
                 for half in range(2)], axis=0)
            scores.append(_dot_nt(q2, k) + bias)
        probs = []
        for s in scores:
            p = jnp.exp(s - jnp.max(s, axis=-1, keepdims=True))
            probs.append((p.astype(BF16), jnp.sum(p, axis=-1, keepdims=True)))
        for r, rs, (p, l) in zip(rr, rss, probs):
            v = v_ref[0, pl.ds(pl.multiple_of(rs * GRID_W, GRID_W), nkeys), :]
            o = _dot(p, v) / l
            o = jnp.where(first, o[:GRID_W], o[GRID_W:])
            o_ref[0, pl.ds(pl.multiple_of(r * GRID_W, GRID_W), GRID_W), :] = o.astype(o_ref.dtype)
        return carry

    lax.fori_loop(0, rows // NA_ROWS_PER_STEP, rows_step, 0)


def _na_bias_table(rpb):
    qc = np.arange(GRID_W)[:, None]
    kc = np.arange(GRID_W)[None, :]
    cs = np.clip(qc - WIN_W // 2, 0, GRID_W - WIN_W)
    col_in = (kc >= cs) & (kc < cs + WIN_W)
    side = GRID_W - WIN_W
    wide = jnp.pad(rpb, ((0, 0), (0, 0), (side, side)))
    t = jnp.stack([wide[:, :, GRID_W - 1 - q:2 * GRID_W - 1 - q] for q in range(GRID_W)], axis=2)
    t = jnp.where(col_in, t, NEG).astype(F32)
    t2 = jnp.concatenate([t[:, :-1], t[:, 1:]], axis=-1)
    return t2.reshape(NA_HEADS // 2, 2, 2 * WIN_H - 2, GRID_W, 2 * GRID_W)


def _na_attention(h3, tbl):
    b, s, _ = h3.shape
    rows = s // GRID_W
    npair = NA_HEADS // 2
    return pl.pallas_call(
        functools.partial(_na_kernel, rows=rows),
        grid=(b, npair),
        in_specs=[pl.BlockSpec((1, s, LANES), lambda i, p: (i, 0, p)),
                  pl.BlockSpec((1, s, LANES), lambda i, p: (i, 0, npair + p)),
                  pl.BlockSpec((1, s, LANES), lambda i, p: (i, 0, 2 * npair + p)),
                  pl.BlockSpec((1, 2, 2 * WIN_H - 2, GRID_W, 2 * GRID_W), lambda i, p: (p, 0, 0, 0, 0))],
        out_specs=pl.BlockSpec((1, s, LANES), lambda i, p: (i, 0, p)),
        out_shape=jax.ShapeDtypeStruct((b, s, NA_W), BF16),
        compiler_params=_cparams("parallel", "parallel"),
        name="na_attention",
    )(h3, h3, h3, tbl)


def _outproj_ln_kernel(ya_ref, qm_ref, mk_ref, mv_ref, wa_ref, wm_ref, x_ref, g_ref, b_ref, or_ref, *, tm):
    lane = lax.broadcasted_iota(I32, (1, LANES), 1)
    first = lane < MEM_DH
    q = qm_ref[...] * jnp.asarray(MEM_DH ** -0.5, BF16)
    cols = [slice(p * LANES, (p + 1) * LANES) for p in range(MEM_HEADS // 2)]
    scores = []
    for c in cols:
        qp = q[:, c]
        q2 = jnp.concatenate([jnp.where(first, qp, jnp.zeros_like(qp)),
                              jnp.where(first, jnp.zeros_like(qp), qp)], axis=0)
        scores.append(_dot_nt(q2, mk_ref[0, :, c]))
    probs = []
    for s in scores:
        p = jnp.exp(s - jnp.max(s, axis=-1, keepdims=True))
        probs.append((p.astype(BF16), jnp.sum(p, axis=-1, keepdims=True)))
    outs = []
    for c, (p, l) in zip(cols, probs):
        o = _dot(p, mv_ref[0, :, c]) / l
        outs.append(jnp.where(first, o[:tm], o[tm:]))
    ym = jnp.concatenate(outs, axis=1).astype(BF16)
    acc = _dot(ya_ref[...], wa_ref[...]) + _dot(ym, wm_ref[...])
    _write_rows(or_ref, _ln(ALPHA * x_ref[...] + acc, g_ref[...], b_ref[...]), tm)


def _outproj_ln(ya, h2, qm_block, mem_k3, mem_v3, wa, wm, x2, g, b, tm=ROW_TILE):
    n = x2.shape[0]
    ka = ya.shape[1]
    nb, nm, _ = mem_k3.shape
    per_batch = n // nb // tm
    full = lambda shape: pl.BlockSpec(shape, lambda i: (0,) * len(shape))
    return pl.pallas_call(
        functools.partial(_outproj_ln_kernel, tm=tm),
        grid=(n // tm,),
        in_specs=[pl.BlockSpec((tm, ka), lambda i: (i, 0)),
                  pl.BlockSpec((tm, MEM_W), lambda i: (i, qm_block)),
                  pl.BlockSpec((1, nm, MEM_W), lambda i: (i // per_batch, 0, 0)),
                  pl.BlockSpec((1, nm, MEM_W), lambda i: (i // per_batch, 0, 0)),
                  full((ka, D_MODEL)), full((MEM_W, D_MODEL)),
                  pl.BlockSpec((tm, D_MODEL), lambda i: (i, 0)),
                  full((1, D_MODEL)), full((1, D_MODEL))],
        out_specs=pl.BlockSpec((tm * ROW_CHUNKS, LANES), lambda i: (i, 0)),
        out_shape=jax.ShapeDtypeStruct((n * ROW_CHUNKS, LANES), F32),
        compiler_params=_cparams("parallel"),
        name="outproj_ln",
    )(ya, h2, mem_k3, mem_v3, wa, wm, x2, g, b)


def _router_kernel(x_ref, rwh_ref, rwl_ref, rb_ref, lpos_ref, w_ref, cnt_ref, tcnt_ref, toff_ref, tbef_ref, *, tm):
    @pl.when(pl.program_id(0) == 0)
    def _():
        cnt_ref[...] = jnp.zeros_like(cnt_ref)

    x = _read_rows(x_ref, tm)
    xh = x.astype(BF16)
    xl = (x - xh.astype(F32)).astype(BF16)
    logits_t = _dot(xh, rwh_ref[...]) + (_dot(xh, rwl_ref[...]) + _dot(xl, rwh_ref[...]))
    logits = logits_t.T[:N_EXPERTS]
    scores = jax.nn.sigmoid(logits)
    biased = scores + rb_ref[...]
    bv = [biased[e:e + 1, :] for e in range(N_EXPERTS)]
    sv = [scores[e:e + 1, :] for e in range(N_EXPERTS)]

    grp = []
    for g in range(N_GROUPS):
        m = bv[g * EXPERTS_PER_GROUP:(g + 1) * EXPERTS_PER_GROUP]
        best = None
        for a in range(EXPERTS_PER_GROUP):
            for c in range(a + 1, EXPERTS_PER_GROUP):
                pair = m[a] + m[c]
                best = pair if best is None else jnp.maximum(best, pair)
        grp.append(best)
    gsel = jnp.zeros((1, tm), I32)
    gbest = grp[0]
    for g in range(1, N_GROUPS):
        better = grp[g] > gbest
        gsel = jnp.where(better, g, gsel)
        gbest = jnp.where(better, grp[g], gbest)

    def pick(vals, j):
        out = vals[j]
        for g in range(1, N_GROUPS):
            out = jnp.where(gsel == g, vals[g * EXPERTS_PER_GROUP + j], out)
        return out

    cb = [pick(bv, j) for j in range(EXPERTS_PER_GROUP)]
    cs = [pick(sv, j) for j in range(EXPERTS_PER_GROUP)]
    i1 = jnp.zeros((1, tm), I32)
    m1 = cb[0]
    s1 = cs[0]
    for j in range(1, EXPERTS_PER_GROUP):
        gt = cb[j] > m1
        i1 = jnp.where(gt, j, i1)
        m1 = jnp.where(gt, cb[j], m1)
        s1 = jnp.where(gt, cs[j], s1)
    i2 = jnp.zeros((1, tm), I32)
    m2 = jnp.full((1, tm), -jnp.inf, F32)
    s2 = jnp.zeros((1, tm), F32)
    for j in range(EXPERTS_PER_GROUP):
        ok = jnp.logical_and(i1 != j, cb[j] > m2)
        i2 = jnp.where(ok, j, i2)
        m2 = jnp.where(ok, cb[j], m2)
        s2 = jnp.where(ok, cs[j], s2)
    e1 = gsel * EXPERTS_PER_GROUP + i1
    e2 = gsel * EXPERTS_PER_GROUP + i2
    tot = s1 + s2
    w_ref[...] = jnp.concatenate([s1 / tot, s2 / tot], axis=0)

    i = pl.program_id(0)
    eio = lax.broadcasted_iota(I32, (N_EXPERTS, tm), 0)
    oh1 = eio == e1
    oh2 = eio == e2
    ohs = jnp.logical_or(oh1, oh2).astype(F32)
    before = (lax.broadcasted_iota(I32, (tm, tm), 0) < lax.broadcasted_iota(I32, (tm, tm), 1))
    pre = _dot(ohs.astype(BF16), before.astype(BF16))
    tile_cnt = jnp.sum(ohs, axis=1, keepdims=True)
    offs = []
    acc = jnp.zeros((1, 1), F32)
    for e in range(N_EXPERTS):
        offs.append(acc)
        acc = acc + tile_cnt[e:e + 1, :]
    tile_off = jnp.concatenate(offs, axis=0)
    pos = tile_off + pre
    p1 = jnp.sum(jnp.where(oh1, pos, 0.0), axis=0, keepdims=True)
    p2 = jnp.sum(jnp.where(oh2, pos, 0.0), axis=0, keepdims=True)
    lpos_ref[...] = jnp.concatenate([p1, p2], axis=0).astype(I32) * ROW_CHUNKS

    @pl.when(i == 0)
    def _():
        for ref in (tcnt_ref, toff_ref, tbef_ref):
            ref[...] = jnp.zeros_like(ref)

    here = lax.broadcasted_iota(I32, (1, LANES), 1) == i
    tcnt_ref[...] = jnp.where(here, tile_cnt, tcnt_ref[...])
    toff_ref[...] = jnp.where(here, tile_off, toff_ref[...])
    tbef_ref[...] = jnp.where(here, cnt_ref[:, 0:1], tbef_ref[...])
    cnt_ref[...] += tile_cnt


def _router(xr, rw, rb, tm):
    n = xr.shape[0] // ROW_CHUNKS
    assert n // tm <= LANES
    table = pl.BlockSpec((N_EXPERTS, LANES), lambda i: (0, 0))
    return pl.pallas_call(
        functools.partial(_router_kernel, tm=tm),
        grid=(n // tm,),
        in_specs=[pl.BlockSpec((tm * ROW_CHUNKS, LANES), lambda i: (i, 0)),
                  pl.BlockSpec((D_MODEL, LANES), lambda i: (0, 0)),
                  pl.BlockSpec((D_MODEL, LANES), lambda i: (0, 0)),
                  pl.BlockSpec((N_EXPERTS, 1), lambda i: (0, 0))],
        out_specs=[pl.BlockSpec((2, tm), lambda i: (0, i)),
                   pl.BlockSpec((2, tm), lambda i: (0, i)),
                   table, table, table, table],
        out_shape=[jax.ShapeDtypeStruct((2, n), I32),
                   jax.ShapeDtypeStruct((2, n), F32)]
                  + [jax.ShapeDtypeStruct((N_EXPERTS, LANES), F32)] * 4,
        compiler_params=_cparams("arbitrary"),
        name="router",
    )(xr, rw[0], rw[1], rb)


def _plan_kernel(cnt_ref, tbef_ref, meta_ref, rstart_ref, *, nbl):
    shift = MOE_BM.bit_length() - 1
    cnt = cnt_ref[...].astype(I32)
    padded = ((cnt + (MOE_BM - 1)) >> shift) << shift
    starts = []
    acc = jnp.zeros((1, LANES), I32)
    for e in range(N_EXPERTS):
        starts.append(acc)
        acc = acc + padded[e:e + 1, :]
    pad_start = jnp.concatenate(starts, axis=0)
    pad_end = pad_start + padded
    rstart_ref[...] = pad_start + tbef_ref[...].astype(I32)
    blk0 = lax.broadcasted_iota(I32, (N_EXPERTS, nbl), 1) * MOE_BM
    block_e = jnp.sum((pad_end[:, 0:1] <= blk0).astype(I32), axis=0, keepdims=True)
    block_e = jnp.minimum(block_e, N_EXPERTS - 1)
    n_used = jnp.broadcast_to(acc[:, 0:1] >> shift, (1, nbl))
    diag = lax.broadcasted_iota(I32, (N_EXPERTS, nbl), 0) == lax.broadcasted_iota(I32, (N_EXPERTS, nbl), 1)
    fill_lo = jnp.sum(jnp.where(diag, (pad_start + cnt)[:, 0:1], 0), axis=0, keepdims=True)
    fill_hi = jnp.sum(jnp.where(diag, pad_end[:, 0:1], 0), axis=0, keepdims=True)
    meta_ref[...] = jnp.concatenate([block_e, n_used, fill_lo, fill_hi, jnp.zeros((SUBLANES - 4, nbl), I32)],
                                    axis=0)


def _plan(cnt, tbef, n_blocks):
    nbl = -(-n_blocks // LANES) * LANES
    table = pl.BlockSpec((N_EXPERTS, LANES), lambda i: (0, 0))
    return pl.pallas_call(
        functools.partial(_plan_kernel, nbl=nbl),
        grid=(1,),
        in_specs=[table, table],
        out_specs=[pl.BlockSpec((SUBLANES, nbl), lambda i: (0, 0)), table],
        out_shape=[jax.ShapeDtypeStruct((SUBLANES, nbl), I32),
                   jax.ShapeDtypeStruct((N_EXPERTS, LANES), I32)],
        compiler_params=_cparams("arbitrary"),
        name="moe_plan",
    )(cnt, tbef)


def _rows(ref, row, nrows):
    return ref.at[pl.ds(pl.multiple_of(row * ROW_CHUNKS, ROW_CHUNKS), nrows * ROW_CHUNKS), :]


def _rows_wait(src_hbm, buf, sem):
    pltpu.make_async_copy(src_hbm.at[pl.ds(0, buf.shape[0]), :], buf, sem).wait()


def _copy_pieces(src, src_row, dst, dst_row, count, max_rows, sem, wait=False):
    bit = max_rows.bit_length() - 1
    while bit >= 0:
        size = 1 << bit
        done = (count >> (bit + 1)) << (bit + 1)

        @pl.when(((count >> bit) & 1) == 1)
        def _():
            cp = pltpu.make_async_copy(_rows(src, src_row + done, size), _rows(dst, dst_row + done, size), sem)
            cp.start()
            if wait:
                cp.wait()

        bit -= 1


def _tile_runs(tcnt_ref, toff_ref, rstart_ref, tile, buf, hbm, sem, *, to_hbm, tm):
    def per_expert(e, carry):
        k = tile * N_EXPERTS + e
        if to_hbm:
            _copy_pieces(buf, toff_ref[k], hbm, rstart_ref[k], tcnt_ref[k], tm, sem)
        else:
            _copy_pieces(hbm, rstart_ref[k], buf, toff_ref[k], tcnt_ref[k], tm, sem)
        return carry

    lax.fori_loop(0, N_EXPERTS, per_expert, 0)


def _dispatch_kernel(lpos_ref, tcnt_ref, toff_ref, rstart_ref, flo_ref, fhi_ref, nu_ref, x_ref, xs_hbm,
                     s0, s1, zbuf, sem, zsem, *, n, tm, n_blocks):
    i = pl.program_id(0)
    nt = pl.num_programs(0)
    bufs = (s0, s1)
    unroll = PLACE_UNROLL

    for slot in range(2):
        @pl.when(i % 2 == slot)
        def _():
            buf = bufs[slot]

            @pl.when(i >= 2)
            def _():
                _rows_wait(xs_hbm, buf, sem.at[slot])

            def place(c, carry):
                tok = i * tm + c * unroll
                src = pl.multiple_of(c * (unroll * ROW_CHUNKS), unroll * ROW_CHUNKS)
                for u in range(unroll):
                    v = x_ref[pl.ds(src + u * ROW_CHUNKS, ROW_CHUNKS), :]
                    for k in range(2):
                        p = lpos_ref[k * n + tok + u]
                        buf[pl.ds(pl.multiple_of(p, ROW_CHUNKS), ROW_CHUNKS), :] = v
                return carry

            lax.fori_loop(0, tm // unroll, place, 0)
            _tile_runs(tcnt_ref, toff_ref, rstart_ref, i, buf, xs_hbm, sem.at[slot], to_hbm=True, tm=tm)

    @pl.when(i == nt - 1)
    def _():
        for slot in range(2):
            @pl.when(nt > slot)
            def _():
                _rows_wait(xs_hbm, bufs[slot], sem.at[slot])

        zbuf[...] = jnp.zeros_like(zbuf)
        for e in range(N_EXPERTS):
            _copy_pieces(zbuf, 0, xs_hbm, flo_ref[e], fhi_ref[e] - flo_ref[e], MOE_BM // 2, zsem, wait=True)

        def zero_block(j, carry):
            cp = pltpu.make_async_copy(zbuf, _rows(xs_hbm, j * MOE_BM, MOE_BM), zsem)
            cp.start()
            cp.wait()
            return carry

        lax.fori_loop(nu_ref[0], n_blocks, zero_block, 0)


def _dispatch(lpos_flat, tcnt, toff, rstart, fill_lo, fill_hi, n_used, xr, n_blocks, tm):
    n = xr.shape[0] // ROW_CHUNKS
    return pl.pallas_call(
        functools.partial(_dispatch_kernel, n=n, tm=tm, n_blocks=n_blocks),
        grid_spec=pltpu.PrefetchScalarGridSpec(
            num_scalar_prefetch=7,
            grid=(n // tm,),
            in_specs=[pl.BlockSpec((tm * ROW_CHUNKS, LANES), lambda i, *_: (i, 0))],
            out_specs=pl.BlockSpec(memory_space=pl.ANY),
            scratch_shapes=[pltpu.VMEM((2 * tm * ROW_CHUNKS, LANES), F32),
                            pltpu.VMEM((2 * tm * ROW_CHUNKS, LANES), F32),
                            pltpu.VMEM((MOE_BM * ROW_CHUNKS, LANES), F32),
                            pltpu.SemaphoreType.DMA((2,)),
                            pltpu.SemaphoreType.DMA(())]),
        out_shape=jax.ShapeDtypeStruct((n_blocks * MOE_BM * ROW_CHUNKS, LANES), F32),
        compiler_params=_cparams("arbitrary"),
        name="moe_dispatch",
    )(lpos_flat, tcnt, toff, rstart, fill_lo, fill_hi, n_used, xr)


def _block_copies(hbm, blk, buf, sem, to_hbm):
    copies = []
    for c in range(ROW_CHUNKS):
        h = hbm.at[pl.ds(blk * MOE_BM, MOE_BM), c, :]
        v = buf.at[:, pl.ds(c * LANES, LANES)]
        copies.append(pltpu.make_async_copy(v, h, sem) if to_hbm else pltpu.make_async_copy(h, v, sem))
    return copies


def _experts_kernel(be_ref, nu_ref, xs_hbm, wg_ref, wu_ref, wd_ref, y_hbm, wgb, wub, wdb, x0, x1, y0, y1,
                    sem_in, sem_out):
    j = pl.program_id(0)
    nb = pl.num_programs(0)
    n_used = nu_ref[0]
    used = j < n_used
    xb, yb = (x0, x1), (y0, y1)

    @pl.when(j == 0)
    def _():
        for cp in _block_copies(xs_hbm, 0, x0, sem_in.at[0], False):
            cp.start()

    @pl.when(jnp.logical_and(used, jnp.logical_or(j == 0, be_ref[j] != be_ref[jnp.maximum(j - 1, 0)])))
    def _():
        wgb[...] = wg_ref[0, 0].astype(BF16)
        wub[...] = wu_ref[0, 0].astype(BF16)
        wdb[...] = wd_ref[0, 0].astype(BF16)

    for slot in range(2):
        @pl.when(j % 2 == slot)
        def _():
            @pl.when(j >= 2)
            def _():
                for cp in _block_copies(y_hbm, j - 2, yb[slot], sem_out.at[slot], True):
                    cp.wait()

            @pl.when(used)
            def _():
                @pl.when(j + 1 < n_used)
                def _():
                    for cp in _block_copies(xs_hbm, j + 1, xb[1 - slot], sem_in.at[1 - slot], False):
                        cp.start()

                for cp in _block_copies(xs_hbm, j, xb[slot], sem_in.at[slot], False):
                    cp.wait()
                x = xb[slot][...].astype(BF16)
                h = _silu(_dot(x, wgb[...])) * _dot(x, wub[...])
                yb[slot][...] = _dot(h.astype(BF16), wdb[...])

            @pl.when(jnp.logical_not(used))
            def _():
                yb[slot][...] = jnp.zeros_like(yb[slot])

            for cp in _block_copies(y_hbm, j, yb[slot], sem_out.at[slot], True):
                cp.start()

    @pl.when(j == nb - 1)
    def _():
        for back in range(2):
            @pl.when(nb > back)
            def _():
                for cp in _block_copies(y_hbm, j - back, y0, sem_out.at[(j - back) % 2], True):
                    cp.wait()


def _experts(block_e, n_used, xs, wg, wu, wd, layer):
    n_blocks = block_e.shape[0]

    def last_used(j, nu):
        return jnp.minimum(j, nu[0] - 1)

    def wblk(j, be, nu):
        return (layer, be[last_used(j, nu)], 0, 0)

    return pl.pallas_call(
        _experts_kernel,
        grid_spec=pltpu.PrefetchScalarGridSpec(
            num_scalar_prefetch=2,
            grid=(n_blocks,),
            in_specs=[pl.BlockSpec(memory_space=pl.ANY),
                      pl.BlockSpec((1, 1, D_MODEL, D_EXPERT), wblk),
                      pl.BlockSpec((1, 1, D_MODEL, D_EXPERT), wblk),
                      pl.BlockSpec((1, 1, D_EXPERT, D_MODEL), wblk)],
            out_specs=pl.BlockSpec(memory_space=pl.ANY),
            scratch_shapes=[pltpu.VMEM((D_MODEL, D_EXPERT), BF16), pltpu.VMEM((D_MODEL, D_EXPERT), BF16),
                            pltpu.VMEM((D_EXPERT, D_MODEL), BF16)]
                           + [pltpu.VMEM((MOE_BM, D_MODEL), F32)] * 4
                           + [pltpu.SemaphoreType.DMA((2,)), pltpu.SemaphoreType.DMA((2,))]),
        out_shape=jax.ShapeDtypeStruct((n_blocks * MOE_BM, ROW_CHUNKS, LANES), F32),
        compiler_params=_cparams("arbitrary"),
        name="moe_experts",
    )(block_e, n_used, xs.reshape(n_blocks * MOE_BM, ROW_CHUNKS, LANES), wg, wu, wd).reshape(xs.shape)


def _combine_ln_kernel(lpos_ref, tcnt_ref, toff_ref, rstart_ref, y_hbm, x_ref, w1_ref, w2_ref, g_ref, b_ref, o_ref,
                       r0, r1, u1, u2, sem, *, n, tm):
    i = pl.program_id(0)
    nt = pl.num_programs(0)
    bufs = (r0, r1)
    unroll = PLACE_UNROLL

    def fetch(tile, slot):
        _tile_runs(tcnt_ref, toff_ref, rstart_ref, tile, bufs[slot], y_hbm, sem.at[slot], to_hbm=False, tm=tm)

    @pl.when(i == 0)
    def _():
        fetch(0, 0)

    for slot in range(2):
        @pl.when(i % 2 == slot)
        def _():
            @pl.when(i + 1 < nt)
            def _():
                fetch(i + 1, 1 - slot)

            buf = bufs[slot]
            _rows_wait(y_hbm, buf, sem.at[slot])

            def place(c, carry):
                tok = i * tm + c * unroll
                dst0 = pl.multiple_of(c * (unroll * ROW_CHUNKS), unroll * ROW_CHUNKS)
                for u in range(unroll):
                    dst = pl.ds(dst0 + u * ROW_CHUNKS, ROW_CHUNKS)
                    for k, out in enumerate((u1, u2)):
                        p = lpos_ref[k * n + tok + u]
                        out[dst, :] = buf[pl.ds(pl.multiple_of(p, ROW_CHUNKS), ROW_CHUNKS), :]
                return carry

            lax.fori_loop(0, tm // unroll, place, 0)
            moe = w1_ref[...] * _read_rows(u1, tm) + w2_ref[...] * _read_rows(u2, tm)
            o_ref[...] = _ln(ALPHA * _read_rows(x_ref, tm) + moe, g_ref[...], b_ref[...])


def _combine_ln(lpos_flat, tcnt, toff, rstart, y, xr, w1, w2, g, b, tm):
    n = xr.shape[0] // ROW_CHUNKS
    return pl.pallas_call(
        functools.partial(_combine_ln_kernel, n=n, tm=tm),
        grid_spec=pltpu.PrefetchScalarGridSpec(
            num_scalar_prefetch=4,
            grid=(n // tm,),
            in_specs=[pl.BlockSpec(memory_space=pl.ANY),
                      pl.BlockSpec((tm * ROW_CHUNKS, LANES), lambda i, *_: (i, 0)),
                      pl.BlockSpec((tm, 1), lambda i, *_: (i, 0)),
                      pl.BlockSpec((tm, 1), lambda i, *_: (i, 0)),
                      pl.BlockSpec((1, D_MODEL), lambda i, *_: (0, 0)),
                      pl.BlockSpec((1, D_MODEL), lambda i, *_: (0, 0))],
            out_specs=pl.BlockSpec((tm, D_MODEL), lambda i, *_: (i, 0)),
            scratch_shapes=[pltpu.VMEM((2 * tm * ROW_CHUNKS, LANES), F32)] * 2
                           + [pltpu.VMEM((tm * ROW_CHUNKS, LANES), F32)] * 2
                           + [pltpu.SemaphoreType.DMA((2,))]),
        out_shape=jax.ShapeDtypeStruct((n, D_MODEL), F32),
        compiler_params=_cparams("arbitrary"),
        name="moe_combine_ln",
    )(lpos_flat, tcnt, toff, rstart, y, xr, w1, w2, g, b)


def _moe_ln(xr, rw, rb, wg, wu, wd, layer, g, b):
    n = xr.shape[0] // ROW_CHUNKS
    n_blocks = (2 * n) // MOE_BM + N_EXPERTS
    tm = MOE_TILE
    nt = n // tm
    lpos, w, cnt, tcnt, toff, tbef = _router(xr, rw, rb, tm)
    meta, rstart = _plan(cnt, tbef, n_blocks)
    block_e = meta[0, :n_blocks]
    n_used = meta[1, :1]

    def per_tile(table):
        return table[:, :nt].T.reshape(nt * N_EXPERTS).astype(I32)

    lpos_flat = lpos.reshape(2 * n)
    tcnt, toff, rstart = per_tile(tcnt), per_tile(toff), per_tile(rstart)
    xs = _dispatch(lpos_flat, tcnt, toff, rstart, meta[2, :N_EXPERTS], meta[3, :N_EXPERTS], n_used, xr,
                   n_blocks, tm)
    y = _experts(block_e, n_used, xs, wg, wu, wd, layer)
    return _combine_ln(lpos_flat, tcnt, toff, rstart, y, xr, w[0].reshape(n, 1), w[1].reshape(n, 1), g, b, tm)


def _conv_qkv_kernel(xm_ref, cw_ref, cb_ref, wq_ref, wk_ref, wv_ref, q_ref, k_ref, v_ref, xc_ref, *, s):
    xm_b = xm_ref[0]
    xm = xm_b.astype(F32)
    cw = cw_ref[...]
    row = lax.broadcasted_iota(I32, (s, 1), 0)
    half = CONV_K // 2
    acc = cb_ref[...] + xm * cw[half:half + 1, :]
    for sh in range(1, half + 1):
        past = jnp.where(row >= sh, pltpu.roll(xm, sh, axis=0), 0.0)
        acc = acc + past * cw[half - sh:half - sh + 1, :]
        nxt = jnp.where(row < s - sh, pltpu.roll(xm, s - sh, axis=0), 0.0)
        acc = acc + nxt * cw[half + sh:half + sh + 1, :]
    xc = _silu(acc).astype(BF16)
    xc_ref[0] = xc
    q_ref[0] = _dot(xc, wq_ref[0]).astype(BF16)
    k_ref[0] = (_dot_nt(wk_ref[0], xc) * (ML_DH ** -0.5)).astype(BF16)
    v = _dot(xm_b, wv_ref[0])
    ones_lane = lax.broadcasted_iota(I32, (1, ML_DHP), 1) == ML_DH
    v_ref[0] = jnp.where(ones_lane, 1.0, v).astype(BF16)


def _conv_qkv(main3, cw, cb, wq, wk_t, wv):
    b, s, _ = main3.shape
    tok = pl.BlockSpec((1, s, ML_DHP), lambda i, h: (i, 0, h))
    wspec = pl.BlockSpec((1, ML_DHP, ML_DHP), lambda i, h: (h, 0, 0))
    tok_shape = jax.ShapeDtypeStruct((b, s, ML_WP), BF16)
    return pl.pallas_call(
        functools.partial(_conv_qkv_kernel, s=s),
        grid=(b, ML_HEADS),
        in_specs=[tok,
                  pl.BlockSpec((CONV_K, ML_DHP), lambda i, h: (0, h)),
                  pl.BlockSpec((1, ML_DHP), lambda i, h: (0, h)),
                  wspec, wspec, wspec],
        out_specs=[tok, pl.BlockSpec((1, ML_DHP, s), lambda i, h: (i, h, 0)), tok, tok],
        out_shape=[tok_shape, jax.ShapeDtypeStruct((b, ML_WP, s), BF16), tok_shape, tok_shape],
        compiler_params=_cparams("parallel", "parallel"),
        name="conv_qkv",
    )(main3, cw, cb, wq, wk_t, wv)


def _mlstm_kernel(q_ref, kt_ref, v_ref, gc_ref, gr_ref, z_ref, xc_ref, ng_ref, sk_ref,
                  y_ref, hf_ref, hb_ref, cf_ref, cb_ref, m_ref, *, s):
    head0 = pl.program_id(1) * ML_HPS
    nc = s // CHUNK
    sub = lax.broadcasted_iota(I32, (LANES, 1), 0)
    gate = lax.broadcasted_iota(I32, (LANES, LANES), 0)
    ti = lax.broadcasted_iota(I32, (CHUNK, CHUNK), 0)
    tj = lax.broadcasted_iota(I32, (CHUNK, CHUNK), 1)

    for ref in (cf_ref, cb_ref, m_ref):
        ref[...] = jnp.zeros_like(ref)

    def intra(c, j, rev):
        t0 = pl.multiple_of(c * CHUNK, CHUNK)
        hl = slice(j * ML_DHP, (j + 1) * ML_DHP)
        qb = q_ref[0, pl.ds(t0, CHUNK), hl]
        kt = kt_ref[0, hl, pl.ds(t0, CHUNK)]
        vb = v_ref[0, pl.ds(t0, CHUNK), hl]
        gc = gc_ref[0, pl.ds(t0, CHUNK), :]
        gr = gr_ref[:, pl.ds(t0, CHUNK)]
        i_idx = head0 + j + (2 * ML_HEADS if rev else 0)
        f_idx = i_idx + ML_HEADS
        allowed = (tj >= ti) if rev else (tj <= ti)
        sel = (gate == f_idx).astype(BF16)
        b_rep = sum(_dot(part, sel) for part in _split3(gc))
        b_row = jnp.sum(jnp.where(sub == f_idx, gr, 0.0), axis=0, keepdims=True)
        i_row = jnp.sum(jnp.where(sub == i_idx, gr, 0.0), axis=0, keepdims=True)
        b_last = (b_rep[0:1, :] if rev else b_rep[CHUNK - 1:CHUNK, :])[:, 0:1]

        b_wide = jnp.concatenate([b_rep] * (CHUNK // LANES), axis=1)
        d = jnp.where(allowed, b_wide - b_row + i_row, NEG)
        m_in = jnp.max(d, axis=1, keepdims=True)
        sc = _dot(qb, kt) * jnp.exp(d - m_in)
        nd_in = _dot(sc.astype(BF16), vb)
        w_row = b_last - b_row + i_row
        return t0, qb, kt, vb, b_rep, b_last, m_in, nd_in, w_row

    def twice(a):
        return jnp.concatenate([a, a], axis=1)

    def update(parts, j, rev):
        t0, qb, kt, vb, b_rep, b_last, m_in, nd_in, w_row = parts
        h_ref, c_ref = (hb_ref, cb_ref) if rev else (hf_ref, cf_ref)
        hl = slice(j * ML_DHP, (j + 1) * ML_DHP)
        mrow = 2 * j + int(rev)
        m = m_ref[mrow:mrow + 1, 0:1]
        cmat = c_ref[j]
        inter = b_rep + m
        m_t = jnp.maximum(m_in, inter)
        a_in = jnp.exp(m_in - m_t)
        iexp = jnp.exp(inter - m_t)
        nd = twice(a_in) * nd_in + twice(iexp) * _dot(qb, cmat.astype(BF16))
        den = nd[:, ML_DH:ML_DH + 1]
        h_ref[pl.ds(t0, CHUNK), hl] = nd * (1.0 / jnp.maximum(jnp.abs(den), jnp.exp(-m_t[:, 0:1])))

        m_new = jnp.maximum(b_last + m, jnp.max(w_row, axis=1, keepdims=True))
        wexp = jnp.exp(w_row - m_new)
        cexp = jnp.exp(b_last + m - m_new)
        kw = (kt.astype(F32) * wexp).astype(BF16)
        c_ref[j] = cexp * cmat + _dot(kw, vb)
        m_ref[mrow:mrow + 1, :] = jnp.broadcast_to(m_new, (1, LANES))

    def step(i, carry):
        for j in range(ML_HPS):
            parts = [intra(nc - 1 - i if rev else i, j, rev) for rev in (False, True)]
            for p, rev in zip(parts, (False, True)):
                update(p, j, rev)
        return carry

    lax.fori_loop(0, nc, step, 0)

    real = lax.broadcasted_iota(I32, (1, ML_DHP), 1) < ML_DH
    tb = CHUNK

    def fin(c, carry):
        t0 = pl.multiple_of(c * tb, tb)
        for j in range(ML_HPS):
            hl = slice(j * ML_DHP, (j + 1) * ML_DHP)
            hs = jnp.where(real, hf_ref[pl.ds(t0, tb), hl] + hb_ref[pl.ds(t0, tb), hl], 0.0)
            mu = jnp.sum(hs, axis=1, keepdims=True) * (1.0 / ML_DH)
            dev = jnp.where(real, hs - mu, 0.0)
            var = jnp.sum(dev * dev, axis=1, keepdims=True) * (1.0 / ML_DH)
            hn = dev * lax.rsqrt(var + LN_EPS) * ng_ref[:, hl]
            xc = xc_ref[0, pl.ds(t0, tb), hl].astype(F32)
            z = z_ref[0, pl.ds(t0, tb), hl].astype(F32)
            y_ref[0, pl.ds(t0, tb), hl] = ((hn + sk_ref[:, hl] * xc) * _silu(z)).astype(BF16)
        return carry

    lax.fori_loop(0, s // tb, fin, 0)


def _mlstm(q, kt, v, gcol3, grow, main3, xc, ng, sk):
    b, s, _ = q.shape
    width = ML_HPS * ML_DHP
    steps = ML_HEADS // ML_HPS
    tok = pl.BlockSpec((1, s, width), lambda i, h: (i, 0, h))
    vec = pl.BlockSpec((1, width), lambda i, h: (0, h))
    return pl.pallas_call(
        functools.partial(_mlstm_kernel, s=s),
        grid=(b, steps),
        in_specs=[tok, pl.BlockSpec((1, width, s), lambda i, h: (i, h, 0)), tok,
                  pl.BlockSpec((1, s, LANES), lambda i, h: (i, 0, 0)),
                  pl.BlockSpec((LANES, s), lambda i, h: (0, i)),
                  pl.BlockSpec((1, s, width), lambda i, h: (i, 0, steps + h)),
                  tok, vec, vec],
        out_specs=tok,
        out_shape=jax.ShapeDtypeStruct((b, s, ML_WP), BF16),
        scratch_shapes=[pltpu.VMEM((s, width), F32), pltpu.VMEM((s, width), F32),
                        pltpu.VMEM((ML_HPS, ML_DHP, ML_DHP), F32), pltpu.VMEM((ML_HPS, ML_DHP, ML_DHP), F32),
                        pltpu.VMEM((SUBLANES, LANES), F32)],
        compiler_params=_cparams("parallel", "parallel"),
        name="mlstm",
    )(q, kt, v, gcol3, grow, main3, xc, ng, sk)


def _pad_heads(a, axis):
    a = jnp.moveaxis(a, axis, -1)
    lead = a.shape[:-1]
    a = a.reshape(lead + (ML_HEADS, ML_DH))
    a = jnp.pad(a, [(0, 0)] * len(lead) + [(0, 0), (0, ML_DHP - ML_DH)])
    return jnp.moveaxis(a.reshape(lead + (ML_WP,)), -1, axis)


def kernel(x, mem, mem_ln_g, mem_ln_b, w_mem_kv, router_w, router_b, na_w_in, na_rpb, ml_w_in, ml_conv_w,
           ml_conv_b, ml_w_qkv, ml_gate_b, ml_norm_g, ml_skip, w_out, ln_g, ln_b, exp_w_gate, exp_w_up,
           exp_w_down):
    b, s, d = x.shape
    n = b * s
    nm = mem.shape[1]
    row = lambda a: a.reshape(1, -1)

    mem_k, mem_v = _memkv(mem.reshape(b * nm, d), row(mem_ln_g), row(mem_ln_b), w_mem_kv.astype(BF16))
    mem_k3 = mem_k.reshape(b, nm, MEM_W)
    mem_v3 = mem_v.reshape(b, nm, MEM_W)
    rw_pad = jnp.pad(router_w, ((0, 0), (0, LANES - N_EXPERTS)))
    rw_hi = rw_pad.astype(BF16)
    rw = (rw_hi, (rw_pad - rw_hi.astype(F32)).astype(BF16))
    rb = router_b.reshape(N_EXPERTS, 1)

    x2 = x.reshape(n, d)

    h0 = _proj(x2, na_w_in[0].astype(BF16)).reshape(b, s, 3 * NA_W + MEM_W)
    y_na = _na_attention(h0, _na_bias_table(na_rpb[0]))
    wo = w_out[0].astype(BF16)
    xr = _outproj_ln(y_na.reshape(n, NA_W), h0.reshape(n, 3 * NA_W + MEM_W), 3 * NA_W // MEM_W, mem_k3, mem_v3,
                     wo[:NA_W], wo[NA_W:], x2, row(ln_g[0, 0]), row(ln_b[0, 0]))
    x2 = _moe_ln(xr, rw, rb, exp_w_gate, exp_w_up, exp_w_down, 0, row(ln_g[0, 1]), row(ln_b[0, 1]))

    w1 = ml_w_in[0]
    w_main = jnp.concatenate([_pad_heads(w1[:, :ML_W], 1), _pad_heads(w1[:, ML_W:2 * ML_W], 1),
                              w1[:, 2 * ML_W + 4 * ML_HEADS:]], axis=1).astype(BF16)
    w_g = jnp.pad(w1[:, 2 * ML_W:2 * ML_W + 4 * ML_HEADS], ((0, 0), (0, LANES - 4 * ML_HEADS))).astype(BF16)
    gb = jnp.pad(ml_gate_b[0].reshape(4 * ML_HEADS), (0, LANES - 4 * ML_HEADS))
    main, acol, arow = _proj_gates(x2, w_main, w_g, w_g.T, gb.reshape(1, LANES), gb.reshape(LANES, 1))
    main3 = main.reshape(b, s, 2 * ML_WP + MEM_W)
    wqkv = jnp.pad(ml_w_qkv[0], ((0, 0), (0, 0), (0, ML_DHP - ML_DH), (0, ML_DHP - ML_DH))).astype(BF16)
    q, k, v, xc = _conv_qkv(main3, _pad_heads(ml_conv_w[0], 1), _pad_heads(row(ml_conv_b[0]), 1),
                            wqkv[0], jnp.swapaxes(wqkv[1], 1, 2), wqkv[2])
    y_ml = _mlstm(q, k, v, acol.reshape(b, s, LANES), arow, main3, xc,
                  _pad_heads(row(ml_norm_g[0]), 1), _pad_heads(row(ml_skip[0]), 1))
    wo = w_out[1]
    xr = _outproj_ln(y_ml.reshape(n, ML_WP), main, 2 * ML_WP // MEM_W, mem_k3, mem_v3,
                     _pad_heads(wo[:ML_W], 0).astype(BF16), wo[ML_W:].astype(BF16), x2,
                     row(ln_g[1, 0]), row(ln_b[1, 0]))
    x2 = _moe_ln(xr, rw, rb, exp_w_gate, exp_w_up, exp_w_down, 1, row(ln_g[1, 1]), row(ln_b[1, 1]))
    return x2.reshape(b, s, d)
```

```python
import functools

import numpy as np
import jax
import jax.numpy as jnp
from jax import lax
from jax.experimental import pallas as pl
from jax.experimental.pallas import tpu as pltpu

F32 = jnp.float32
BF16 = jnp.bfloat16
I32 = jnp.int32

D_MODEL = 1024
DEPTH = 2
GRID_W = 64
MEM_HEADS = 4
MEM_DH = 64
MEM_W = MEM_HEADS * MEM_DH
NA_HEADS = 12
NA_DH = 64
NA_W = NA_HEADS * NA_DH
WIN_H = 8
WIN_W = 16
ML_HEADS = 4
ML_DH = 192
ML_DHP = 256
ML_W = ML_HEADS * ML_DH
ML_WP = ML_HEADS * ML_DHP
CONV_K = 5
CHUNK = 256
N_EXPERTS = 16
N_GROUPS = 4
EXPERTS_PER_GROUP = N_EXPERTS // N_GROUPS
D_EXPERT = 512
ALPHA = (2 * DEPTH) ** 0.25
LN_EPS = 1e-5
NEG = -1e30

LANES = 128
SUBLANES = 8
ROW_CHUNKS = D_MODEL // LANES
MOE_BM = 512
MOE_TILE = 512
ML_HPS = 2
NA_ROWS_PER_STEP = 16
ROW_TILE = 1024
PLACE_UNROLL = 8
VMEM_LIMIT = 48 * 1024 * 1024


def _cparams(*sem):
    return pltpu.CompilerParams(dimension_semantics=sem, vmem_limit_bytes=VMEM_LIMIT)


def _dot(a, b):
    return jnp.dot(a, b, preferred_element_type=F32)


def _dot_nt(a, b, precision=None):
    return lax.dot_general(a, b, (((1,), (1,)), ((), ())), precision=precision,
                           preferred_element_type=F32)


def _ln(z, g, b):
    mu = jnp.mean(z, axis=-1, keepdims=True)
    zc = z - mu
    var = jnp.mean(zc * zc, axis=-1, keepdims=True)
    return zc * lax.rsqrt(var + LN_EPS) * g + b


def _silu(x):
    return x * jax.nn.sigmoid(x)


def _read_rows(ref, n):
    return jnp.concatenate([ref[pl.ds(j, n, stride=ROW_CHUNKS), :] for j in range(ROW_CHUNKS)], axis=1)


def _write_rows(ref, val, n):
    for j in range(ROW_CHUNKS):
        ref[pl.ds(j, n, stride=ROW_CHUNKS), :] = val[:, j * LANES:(j + 1) * LANES]


def _memkv_kernel(m_ref, g_ref, b_ref, w_ref, k_ref, v_ref):
    z = _ln(m_ref[...], g_ref[...], b_ref[...])
    kv = _dot(z.astype(BF16), w_ref[...])
    k_ref[...] = kv[:, :MEM_W].astype(BF16)
    v_ref[...] = kv[:, MEM_W:].astype(BF16)


def _memkv(mem2, g, b, w):
    n = mem2.shape[0]
    tm = min(ROW_TILE, n)
    return pl.pallas_call(
        _memkv_kernel,
        grid=(n // tm,),
        in_specs=[pl.BlockSpec((tm, D_MODEL), lambda i: (i, 0)),
                  pl.BlockSpec((1, D_MODEL), lambda i: (0, 0)),
                  pl.BlockSpec((1, D_MODEL), lambda i: (0, 0)),
                  pl.BlockSpec((D_MODEL, 2 * MEM_W), lambda i: (0, 0))],
        out_specs=[pl.BlockSpec((tm, MEM_W), lambda i: (i, 0)),
                   pl.BlockSpec((tm, MEM_W), lambda i: (i, 0))],
        out_shape=[jax.ShapeDtypeStruct((n, MEM_W), BF16)] * 2,
        compiler_params=_cparams("parallel"),
        name="memkv",
    )(mem2, g, b, w)


def _proj_kernel(x_ref, w_ref, o_ref):
    o_ref[...] = _dot(x_ref[...].astype(BF16), w_ref[...]).astype(o_ref.dtype)


def _proj(x2, w, tm=ROW_TILE):
    n, k = x2.shape
    nout = w.shape[1]
    return pl.pallas_call(
        _proj_kernel,
        grid=(n // tm,),
        in_specs=[pl.BlockSpec((tm, k), lambda i: (i, 0)),
                  pl.BlockSpec((k, nout), lambda i: (0, 0))],
        out_specs=pl.BlockSpec((tm, nout), lambda i: (i, 0)),
        out_shape=jax.ShapeDtypeStruct((n, nout), BF16),
        compiler_params=_cparams("parallel"),
        name="in_proj",
    )(x2, w)


def _split3(x):
    hi = x.astype(BF16)
    r1 = x - hi.astype(F32)
    mid = r1.astype(BF16)
    lo = (r1 - mid.astype(F32)).astype(BF16)
    return hi, mid, lo


def _proj_gates_kernel(x_ref, w_ref, wg_ref, wgt_ref, gbc_ref, gbr_ref, o_ref, g_ref, gt_ref, *, tm):
    xb = x_ref[...].astype(BF16)
    o_ref[...] = _dot(xb, w_ref[...]).astype(BF16)
    gcol = _dot(xb, wg_ref[...]) + gbc_ref[...]
    grow = _dot_nt(wgt_ref[...], xb) + gbr_ref[...]
    lane = lax.broadcasted_iota(I32, (1, LANES), 1)
    sub = lax.broadcasted_iota(I32, (LANES, 1), 0)
    ti = lax.broadcasted_iota(I32, (CHUNK, CHUNK), 0)
    tj = lax.broadcasted_iota(I32, (CHUNK, CHUNK), 1)
    lower = (tj <= ti).astype(BF16)
    upper = (ti <= tj).astype(BF16)

    def pick(idx, pre, suf, raw):
        fwd = jnp.logical_and(idx >= ML_HEADS, idx < 2 * ML_HEADS)
        bwd = jnp.logical_and(idx >= 3 * ML_HEADS, idx < 4 * ML_HEADS)
        return jnp.where(fwd, pre, jnp.where(bwd, suf, raw))

    for c in range(tm // CHUNK):
        tc = slice(c * CHUNK, (c + 1) * CHUNK)
        g = gcol[tc, :]
        ls = jax.nn.log_sigmoid(g)
        pre = sum(_dot(lower, part) for part in _split3(ls))
        suf = jnp.sum(ls, axis=0, keepdims=True) - pre + ls
        g_ref[tc, :] = pick(lane, pre, suf, g)
        g = grow[:, tc]
        ls = jax.nn.log_sigmoid(g)
        pre = sum(_dot(part, upper) for part in _split3(ls))
        suf = jnp.sum(ls, axis=1, keepdims=True) - pre + ls
        gt_ref[:, tc] = pick(sub, pre, suf, g)


def _proj_gates(x2, w, wg, wgt, gbc, gbr, tm=ROW_TILE):
    n, k = x2.shape
    nout = w.shape[1]
    return pl.pallas_call(
        functools.partial(_proj_gates_kernel, tm=tm),
        grid=(n // tm,),
        in_specs=[pl.BlockSpec((tm, k), lambda i: (i, 0)),
                  pl.BlockSpec((k, nout), lambda i: (0, 0)),
                  pl.BlockSpec((k, LANES), lambda i: (0, 0)),
                  pl.BlockSpec((LANES, k), lambda i: (0, 0)),
                  pl.BlockSpec((1, LANES), lambda i: (0, 0)),
                  pl.BlockSpec((LANES, 1), lambda i: (0, 0))],
        out_specs=[pl.BlockSpec((tm, nout), lambda i: (i, 0)),
                   pl.BlockSpec((tm, LANES), lambda i: (i, 0)),
                   pl.BlockSpec((LANES, tm), lambda i: (0, i))],
        out_shape=[jax.ShapeDtypeStruct((n, nout), BF16),
                   jax.ShapeDtypeStruct((n, LANES), F32),
                   jax.ShapeDtypeStruct((LANES, n), F32)],
        compiler_params=_cparams("parallel"),
        name="in_proj_gates",
    )(x2, w, wg, wgt, gbc, gbr)


def _na_kernel(q_ref, k_ref, v_ref, tbl_ref, o_ref, *, rows):
    lane = lax.broadcasted_iota(I32, (1, LANES), 1)
    first = lane < NA_DH
    nkeys = WIN_H * GRID_W

    def rows_step(i, carry):
        rr = [i * NA_ROWS_PER_STEP + u for u in range(NA_ROWS_PER_STEP)]
        rss = [jnp.clip(r - WIN_H // 2, 0, rows - WIN_H) for r in rr]
        scores = []
        for r, rs in zip(rr, rss):
            q = q_ref[0, pl.ds(pl.multiple_of(r * GRID_W, GRID_W), GRID_W), :]
            q = q * jnp.asarray(NA_DH ** -0.5, BF16)
            q2 = jnp.concatenate([jnp.where(first, q, jnp.zeros_like(q)),
                                  jnp.where(first, jnp.zeros_like(q), q)], axis=0)
            k = k_ref[0, pl.ds(pl.multiple_of(rs * GRID_W, GRID_W), nkeys), :]
            dr0 = rs - r + WIN_H - 1
            bias = jnp.concatenate(
                [jnp.concatenate([tbl_ref[0, half, dr0 + 2 * m] for m in range(WIN_H // 2)], axis=1)
                 for half in range(2)], axis=0)
            scores.append(_dot_nt(q2, k) + bias)
        probs = []
        for s in scores:
            p = jnp.exp(s - jnp.max(s, axis=-1, keepdims=True))
            probs.append((p.astype(BF16), jnp.sum(p, axis=-1, keepdims=True)))
        for r, rs, (p, l) in zip(rr, rss, probs):
            v = v_ref[0, pl.ds(pl.multiple_of(rs * GRID_W, GRID_W), nkeys), :]
            o = _dot(p, v) / l
            o = jnp.where(first, o[:GRID_W], o[GRID_W:])
            o_ref[0, pl.ds(pl.multiple_of(r * GRID_W, GRID_W), GRID_W), :] = o.astype(o_ref.dtype)
        return carry

    lax.fori_loop(0, rows // NA_ROWS_PER_STEP, rows_step, 0)


def _na_bias_table(rpb):
    qc = np.arange(GRID_W)[:, None]
    kc = np.arange(GRID_W)[None, :]
    cs = np.clip(qc - WIN_W // 2, 0, GRID_W - WIN_W)
    col_in = (kc >= cs) & (kc < cs + WIN_W)
    side = GRID_W - WIN_W
    wide = jnp.pad(rpb, ((0, 0), (0, 0), (side, side)))
    t = jnp.stack([wide[:, :, GRID_W - 1 - q:2 * GRID_W - 1 - q] for q in range(GRID_W)], axis=2)
    t = jnp.where(col_in, t, NEG).astype(F32)
    t2 = jnp.concatenate([t[:, :-1], t[:, 1:]], axis=-1)
    return t2.reshape(NA_HEADS // 2, 2, 2 * WIN_H - 2, GRID_W, 2 * GRID_W)


def _na_attention(h3, tbl):
    b, s, _ = h3.shape
    rows = s // GRID_W
    npair = NA_HEADS // 2
    return pl.pallas_call(
        functools.partial(_na_kernel, rows=rows),
        grid=(b, npair),
        in_specs=[pl.BlockSpec((1, s, LANES), lambda i, p: (i, 0, p)),
                  pl.BlockSpec((1, s, LANES), lambda i, p: (i, 0, npair + p)),
                  pl.BlockSpec((1, s, LANES), lambda i, p: (i, 0, 2 * npair + p)),
                  pl.BlockSpec((1, 2, 2 * WIN_H - 2, GRID_W, 2 * GRID_W), lambda i, p: (p, 0, 0, 0, 0))],
        out_specs=pl.BlockSpec((1, s, LANES), lambda i, p: (i, 0, p)),
        out_shape=jax.ShapeDtypeStruct((b, s, NA_W), BF16),
        compiler_params=_cparams("parallel", "parallel"),
        name="na_attention",
    )(h3, h3, h3, tbl)


def _outproj_ln_kernel(ya_ref, qm_ref, mk_ref, mv_ref, wa_ref, wm_ref, x_ref, g_ref, b_ref, or_ref, *, tm):
    lane = lax.broadcasted_iota(I32, (1, LANES), 1)
    first = lane < MEM_DH
    q = qm_ref[...] * jnp.asarray(MEM_DH ** -0.5, BF16)
    cols = [slice(p * LANES, (p + 1) * LANES) for p in range(MEM_HEADS // 2)]
    scores = []
    for c in cols:
        qp = q[:, c]
        q2 = jnp.concatenate([jnp.where(first, qp, jnp.zeros_like(qp)),
                              jnp.where(first, jnp.zeros_like(qp), qp)], axis=0)
        scores.append(_dot_nt(q2, mk_ref[0, :, c]))
    probs = []
    for s in scores:
        p = jnp.exp(s - jnp.max(s, axis=-1, keepdims=True))
        probs.append((p.astype(BF16), jnp.sum(p, axis=-1, keepdims=True)))
    outs = []
    for c, (p, l) in zip(cols, probs):
        o = _dot(p, mv_ref[0, :, c]) / l
        outs.append(jnp.where(first, o[:tm], o[tm:]))
    ym = jnp.concatenate(outs, axis=1).astype(BF16)
    acc = _dot(ya_ref[...], wa_ref[...]) + _dot(ym, wm_ref[...])
    _write_rows(or_ref, _ln(ALPHA * x_ref[...] + acc, g_ref[...], b_ref[...]), tm)


def _outproj_ln(ya, h2, qm_block, mem_k3, mem_v3, wa, wm, x2, g, b, tm=ROW_TILE):
    n = x2.shape[0]
    ka = ya.shape[1]
    nb, nm, _ = mem_k3.shape
    per_batch = n // nb // tm
    full = lambda shape: pl.BlockSpec(shape, lambda i: (0,) * len(shape))
    return pl.pallas_call(
        functools.partial(_outproj_ln_kernel, tm=tm),
        grid=(n // tm,),
        in_specs=[pl.BlockSpec((tm, ka), lambda i: (i, 0)),
                  pl.BlockSpec((tm, MEM_W), lambda i: (i, qm_block)),
                  pl.BlockSpec((1, nm, MEM_W), lambda i: (i // per_batch, 0, 0)),
                  pl.BlockSpec((1, nm, MEM_W), lambda i: (i // per_batch, 0, 0)),
                  full((ka, D_MODEL)), full((MEM_W, D_MODEL)),
                  pl.BlockSpec((tm, D_MODEL), lambda i: (i, 0)),
                  full((1, D_MODEL)), full((1, D_MODEL))],
        out_specs=pl.BlockSpec((tm * ROW_CHUNKS, LANES), lambda i: (i, 0)),
        out_shape=jax.ShapeDtypeStruct((n * ROW_CHUNKS, LANES), F32),
        compiler_params=_cparams("parallel"),
        name="outproj_ln",
    )(ya, h2, mem_k3, mem_v3, wa, wm, x2, g, b)


def _router_kernel(x_ref, rwh_ref, rwl_ref, rb_ref, lpos_ref, w_ref, cnt_ref, tcnt_ref, toff_ref, tbef_ref, *, tm):
    @pl.when(pl.program_id(0) == 0)
    def _():
        cnt_ref[...] = jnp.zeros_like(cnt_ref)

    x = _read_rows(x_ref, tm)
    xh = x.astype(BF16)
    xl = (x - xh.astype(F32)).astype(BF16)
    logits_t = _dot(xh, rwh_ref[...]) + (_dot(xh, rwl_ref[...]) + _dot(xl, rwh_ref[...]))
    logits = logits_t.T[:N_EXPERTS]
    scores = jax.nn.sigmoid(logits)
    biased = scores + rb_ref[...]
    bv = [biased[e:e + 1, :] for e in range(N_EXPERTS)]
    sv = [scores[e:e + 1, :] for e in range(N_EXPERTS)]

    grp = []
    for g in range(N_GROUPS):
        m = bv[g * EXPERTS_PER_GROUP:(g + 1) * EXPERTS_PER_GROUP]
        best = None
        for a in range(EXPERTS_PER_GROUP):
            for c in range(a + 1, EXPERTS_PER_GROUP):
                pair = m[a] + m[c]
                best = pair if best is None else jnp.maximum(best, pair)
        grp.append(best)
    gsel = jnp.zeros((1, tm), I32)
    gbest = grp[0]
    for g in range(1, N_GROUPS):
        better = grp[g] > gbest
        gsel = jnp.where(better, g, gsel)
        gbest = jnp.where(better, grp[g], gbest)

    def pick(vals, j):
        out = vals[j]
        for g in range(1, N_GROUPS):
            out = jnp.where(gsel == g, vals[g * EXPERTS_PER_GROUP + j], out)
        return out

    cb = [pick(bv, j) for j in range(EXPERTS_PER_GROUP)]
    cs = [pick(sv, j) for j in range(EXPERTS_PER_GROUP)]
    i1 = jnp.zeros((1, tm), I32)
    m1 = cb[0]
    s1 = cs[0]
    for j in range(1, EXPERTS_PER_GROUP):
        gt = cb[j] > m1
        i1 = jnp.where(gt, j, i1)
        m1 = jnp.where(gt, cb[j], m1)
        s1 = jnp.where(gt, cs[j], s1)
    i2 = jnp.zeros((1, tm), I32)
    m2 = jnp.full((1, tm), -jnp.inf, F32)
    s2 = jnp.zeros((1, tm), F32)
    for j in range(EXPERTS_PER_GROUP):
        ok = jnp.logical_and(i1 != j, cb[j] > m2)
        i2 = jnp.where(ok, j, i2)
        m2 = jnp.where(ok, cb[j], m2)
        s2 = jnp.where(ok, cs[j], s2)
    e1 = gsel * EXPERTS_PER_GROUP + i1
    e2 = gsel * EXPERTS_PER_GROUP + i2
    tot = s1 + s2
    w_ref[...] = jnp.concatenate([s1 / tot, s2 / tot], axis=0)

    i = pl.program_id(0)
    eio = lax.broadcasted_iota(I32, (N_EXPERTS, tm), 0)
    oh1 = eio == e1
    oh2 = eio == e2
    ohs = jnp.logical_or(oh1, oh2).astype(F32)
    before = (lax.broadcasted_iota(I32, (tm, tm), 0) < lax.broadcasted_iota(I32, (tm, tm), 1))
    pre = _dot(ohs.astype(BF16), before.astype(BF16))
    tile_cnt = jnp.sum(ohs, axis=1, keepdims=True)
    offs = []
    acc = jnp.zeros((1, 1), F32)
    for e in range(N_EXPERTS):
        offs.append(acc)
        acc = acc + tile_cnt[e:e + 1, :]
    tile_off = jnp.concatenate(offs, axis=0)
    pos = tile_off + pre
    p1 = jnp.sum(jnp.where(oh1, pos, 0.0), axis=0, keepdims=True)
    p2 = jnp.sum(jnp.where(oh2, pos, 0.0), axis=0, keepdims=True)
    lpos_ref[...] = jnp.concatenate([p1, p2], axis=0).astype(I32) * ROW_CHUNKS

    @pl.when(i == 0)
    def _():
        for ref in (tcnt_ref, toff_ref, tbef_ref):
            ref[...] = jnp.zeros_like(ref)

    here = lax.broadcasted_iota(I32, (1, LANES), 1) == i
    tcnt_ref[...] = jnp.where(here, tile_cnt, tcnt_ref[...])
    toff_ref[...] = jnp.where(here, tile_off, toff_ref[...])
    tbef_ref[...] = jnp.where(here, cnt_ref[:, 0:1], tbef_ref[...])
    cnt_ref[...] += tile_cnt


def _router(xr, rw, rb, tm):
    n = xr.shape[0] // ROW_CHUNKS
    assert n // tm <= LANES
    table = pl.BlockSpec((N_EXPERTS, LANES), lambda i: (0, 0))
    return pl.pallas_call(
        functools.partial(_router_kernel, tm=tm),
        grid=(n // tm,),
        in_specs=[pl.BlockSpec((tm * ROW_CHUNKS, LANES), lambda i: (i, 0)),
                  pl.BlockSpec((D_MODEL, LANES), lambda i: (0, 0)),
                  pl.BlockSpec((D_MODEL, LANES), lambda i: (0, 0)),
                  pl.BlockSpec((N_EXPERTS, 1), lambda i: (0, 0))],
        out_specs=[pl.BlockSpec((2, tm), lambda i: (0, i)),
                   pl.BlockSpec((2, tm), lambda i: (0, i)),
                   table, table, table, table],
        out_shape=[jax.ShapeDtypeStruct((2, n), I32),
                   jax.ShapeDtypeStruct((2, n), F32)]
                  + [jax.ShapeDtypeStruct((N_EXPERTS, LANES), F32)] * 4,
        compiler_params=_cparams("arbitrary"),
        name="router",
    )(xr, rw[0], rw[1], rb)


def _plan_kernel(cnt_ref, tbef_ref, meta_ref, rstart_ref, *, nbl):
    shift = MOE_BM.bit_length() - 1
    cnt = cnt_ref[...].astype(I32)
    padded = ((cnt + (MOE_BM - 1)) >> shift) << shift
    starts = []
    acc = jnp.zeros((1, LANES), I32)
    for e in range(N_EXPERTS):
        starts.append(acc)
        acc = acc + padded[e:e + 1, :]
    pad_start = jnp.concatenate(starts, axis=0)
    pad_end = pad_start + padded
    rstart_ref[...] = pad_start + tbef_ref[...].astype(I32)
    blk0 = lax.broadcasted_iota(I32, (N_EXPERTS, nbl), 1) * MOE_BM
    block_e = jnp.sum((pad_end[:, 0:1] <= blk0).astype(I32), axis=0, keepdims=True)
    block_e = jnp.minimum(block_e, N_EXPERTS - 1)
    n_used = jnp.broadcast_to(acc[:, 0:1] >> shift, (1, nbl))
    diag = lax.broadcasted_iota(I32, (N_EXPERTS, nbl), 0) == lax.broadcasted_iota(I32, (N_EXPERTS, nbl), 1)
    fill_lo = jnp.sum(jnp.where(diag, (pad_start + cnt)[:, 0:1], 0), axis=0, keepdims=True)
    fill_hi = jnp.sum(jnp.where(diag, pad_end[:, 0:1], 0), axis=0, keepdims=True)
    meta_ref[...] = jnp.concatenate([block_e, n_used, fill_lo, fill_hi, jnp.zeros((SUBLANES - 4, nbl), I32)],
                                    axis=0)


def _plan(cnt, tbef, n_blocks):
    nbl = -(-n_blocks // LANES) * LANES
    table = pl.BlockSpec((N_EXPERTS, LANES), lambda i: (0, 0))
    return pl.pallas_call(
        functools.partial(_plan_kernel, nbl=nbl),
        grid=(1,),
        in_specs=[table, table],
        out_specs=[pl.BlockSpec((SUBLANES, nbl), lambda i: (0, 0)), table],
        out_shape=[jax.ShapeDtypeStruct((SUBLANES, nbl), I32),
                   jax.ShapeDtypeStruct((N_EXPERTS, LANES), I32)],
        compiler_params=_cparams("arbitrary"),
        name="moe_plan",
    )(cnt, tbef)


def _rows(ref, row, nrows):
    return ref.at[pl.ds(pl.multiple_of(row * ROW_CHUNKS, ROW_CHUNKS), nrows * ROW_CHUNKS), :]


def _rows_wait(src_hbm, buf, sem):
    pltpu.make_async_copy(src_hbm.at[pl.ds(0, buf.shape[0]), :], buf, sem).wait()


def _copy_pieces(src, src_row, dst, dst_row, count, max_rows, sem, wait=False):
    bit = max_rows.bit_length() - 1
    while bit >= 0:
        size = 1 << bit
        done = (count >> (bit + 1)) << (bit + 1)

        @pl.when(((count >> bit) & 1) == 1)
        def _():
            cp = pltpu.make_async_copy(_rows(src, src_row + done, size), _rows(dst, dst_row + done, size), sem)
            cp.start()
            if wait:
                cp.wait()

        bit -= 1


def _run_copy(tcnt_ref, toff_ref, rstart_ref, tile, e, buf, hbm, sem, *, to_hbm, tm):
    k = tile * N_EXPERTS + e
    if to_hbm:
        _copy_pieces(buf, toff_ref[k], hbm, rstart_ref[k], tcnt_ref[k], tm, sem)
    else:
        _copy_pieces(hbm, rstart_ref[k], buf, toff_ref[k], tcnt_ref[k], tm, sem)


def _tile_runs(tcnt_ref, toff_ref, rstart_ref, tile, buf, hbm, sem, *, to_hbm, tm):
    def per_expert(e, carry):
        _run_copy(tcnt_ref, toff_ref, rstart_ref, tile, e, buf, hbm, sem, to_hbm=to_hbm, tm=tm)
        return carry

    lax.fori_loop(0, N_EXPERTS, per_expert, 0)


def _dispatch_kernel(lpos_ref, tcnt_ref, toff_ref, rstart_ref, flo_ref, fhi_ref, nu_ref, x_ref, xs_hbm,
                     s0, s1, zbuf, sem, zsem, *, n, tm, n_blocks):
    i = pl.program_id(0)
    nt = pl.num_programs(0)
    bufs = (s0, s1)
    unroll = PLACE_UNROLL
    seg = tm // unroll // N_EXPERTS

    for slot in range(2):
        @pl.when(i % 2 == slot)
        def _():
            buf = bufs[slot]

            @pl.when(i >= 2)
            def _():
                _rows_wait(xs_hbm, buf, sem.at[slot])

            def place(c, carry):
                tok = i * tm + c * unroll
                src = pl.multiple_of(c * (unroll * ROW_CHUNKS), unroll * ROW_CHUNKS)
                for u in range(unroll):
                    v = x_ref[pl.ds(src + u * ROW_CHUNKS, ROW_CHUNKS), :]
                    for k in range(2):
                        p = lpos_ref[k * n + tok + u]
                        buf[pl.ds(pl.multiple_of(p, ROW_CHUNKS), ROW_CHUNKS), :] = v
                return carry

            for e in range(N_EXPERTS):
                lax.fori_loop(e * seg, (e + 1) * seg, place, 0)

                @pl.when(i >= 1)
                def _():
                    _run_copy(tcnt_ref, toff_ref, rstart_ref, i - 1, e, bufs[1 - slot], xs_hbm, sem.at[1 - slot],
                              to_hbm=True, tm=tm)

            @pl.when(i == nt - 1)
            def _():
                _tile_runs(tcnt_ref, toff_ref, rstart_ref, i, buf, xs_hbm, sem.at[slot], to_hbm=True, tm=tm)

    @pl.when(i == nt - 1)
    def _():
        for slot in range(2):
            @pl.when(nt > slot)
            def _():
                _rows_wait(xs_hbm, bufs[slot], sem.at[slot])

        zbuf[...] = jnp.zeros_like(zbuf)
        for e in range(N_EXPERTS):
            _copy_pieces(zbuf, 0, xs_hbm, flo_ref[e], fhi_ref[e] - flo_ref[e], MOE_BM // 2, zsem, wait=True)

        def zero_block(j, carry):
            cp = pltpu.make_async_copy(zbuf, _rows(xs_hbm, j * MOE_BM, MOE_BM), zsem)
            cp.start()
            cp.wait()
            return carry

        lax.fori_loop(nu_ref[0], n_blocks, zero_block, 0)


def _dispatch(lpos_flat, tcnt, toff, rstart, fill_lo, fill_hi, n_used, xr, n_blocks, tm):
    n = xr.shape[0] // ROW_CHUNKS
    return pl.pallas_call(
        functools.partial(_dispatch_kernel, n=n, tm=tm, n_blocks=n_blocks),
        grid_spec=pltpu.PrefetchScalarGridSpec(
            num_scalar_prefetch=7,
            grid=(n // tm,),
            in_specs=[pl.BlockSpec((tm * ROW_CHUNKS, LANES), lambda i, *_: (i, 0))],
            out_specs=pl.BlockSpec(memory_space=pl.ANY),
            scratch_shapes=[pltpu.VMEM((2 * tm * ROW_CHUNKS, LANES), F32),
                            pltpu.VMEM((2 * tm * ROW_CHUNKS, LANES), F32),
                            pltpu.VMEM((MOE_BM * ROW_CHUNKS, LANES), F32),
                            pltpu.SemaphoreType.DMA((2,)),
                            pltpu.SemaphoreType.DMA(())]),
        out_shape=jax.ShapeDtypeStruct((n_blocks * MOE_BM * ROW_CHUNKS, LANES), F32),
        compiler_params=_cparams("arbitrary"),
        name="moe_dispatch",
    )(lpos_flat, tcnt, toff, rstart, fill_lo, fill_hi, n_used, xr)


def _experts_kernel(be_ref, nu_ref, xs_ref, wg_ref, wu_ref, wd_ref, y_ref, wgb, wub, wdb):
    j = pl.program_id(0)
    used = j < nu_ref[0]

    @pl.when(jnp.logical_and(used, jnp.logical_or(j == 0, be_ref[j] != be_ref[jnp.maximum(j - 1, 0)])))
    def _():
        wgb[...] = wg_ref[0, 0].astype(BF16)
        wub[...] = wu_ref[0, 0].astype(BF16)
        wdb[...] = wd_ref[0, 0].astype(BF16)

    @pl.when(used)
    def _():
        x = _read_rows(xs_ref, MOE_BM).astype(BF16)
        h = _silu(_dot(x, wgb[...])) * _dot(x, wub[...])
        _write_rows(y_ref, _dot(h.astype(BF16), wdb[...]), MOE_BM)

    @pl.when(jnp.logical_not(used))
    def _():
        y_ref[...] = jnp.zeros_like(y_ref)


def _experts(block_e, n_used, xs, wg, wu, wd, layer):
    n_blocks = block_e.shape[0]

    def last_used(j, nu):
        return jnp.minimum(j, nu[0] - 1)

    def wblk(j, be, nu):
        return (layer, be[last_used(j, nu)], 0, 0)

    return pl.pallas_call(
        _experts_kernel,
        grid_spec=pltpu.PrefetchScalarGridSpec(
            num_scalar_prefetch=2,
            grid=(n_blocks,),
            in_specs=[pl.BlockSpec((MOE_BM * ROW_CHUNKS, LANES), lambda j, be, nu: (last_used(j, nu), 0)),
                      pl.BlockSpec((1, 1, D_MODEL, D_EXPERT), wblk),
                      pl.BlockSpec((1, 1, D_MODEL, D_EXPERT), wblk),
                      pl.BlockSpec((1, 1, D_EXPERT, D_MODEL), wblk)],
            out_specs=pl.BlockSpec((MOE_BM * ROW_CHUNKS, LANES), lambda j, be, nu: (j, 0)),
            scratch_shapes=[pltpu.VMEM((D_MODEL, D_EXPERT), BF16), pltpu.VMEM((D_MODEL, D_EXPERT), BF16),
                            pltpu.VMEM((D_EXPERT, D_MODEL), BF16)]),
        out_shape=jax.ShapeDtypeStruct(xs.shape, F32),
        compiler_params=_cparams("arbitrary"),
        name="moe_experts",
    )(block_e, n_used, xs, wg, wu, wd)


def _combine_ln_kernel(lpos_ref, tcnt_ref, toff_ref, rstart_ref, y_hbm, x_ref, w1_ref, w2_ref, g_ref, b_ref, o_ref,
                       r0, r1, u1, u2, sem, *, n, tm):
    i = pl.program_id(0)
    nt = pl.num_programs(0)
    bufs = (r0, r1)
    unroll = PLACE_UNROLL
    seg = tm // unroll // N_EXPERTS

    @pl.when(i == 0)
    def _():
        _tile_runs(tcnt_ref, toff_ref, rstart_ref, 0, r0, y_hbm, sem.at[0], to_hbm=False, tm=tm)

    for slot in range(2):
        @pl.when(i % 2 == slot)
        def _():
            buf = bufs[slot]
            _rows_wait(y_hbm, buf, sem.at[slot])

            def place(c, carry):
                tok = i * tm + c * unroll
                dst0 = pl.multiple_of(c * (unroll * ROW_CHUNKS), unroll * ROW_CHUNKS)
                for u in range(unroll):
                    dst = pl.ds(dst0 + u * ROW_CHUNKS, ROW_CHUNKS)
                    for k, out in enumerate((u1, u2)):
                        p = lpos_ref[k * n + tok + u]
                        out[dst, :] = buf[pl.ds(pl.multiple_of(p, ROW_CHUNKS), ROW_CHUNKS), :]
                return carry

            for e in range(N_EXPERTS):
                lax.fori_loop(e * seg, (e + 1) * seg, place, 0)

                @pl.when(i + 1 < nt)
                def _():
                    _run_copy(tcnt_ref, toff_ref, rstart_ref, i + 1, e, bufs[1 - slot], y_hbm, sem.at[1 - slot],
                              to_hbm=False, tm=tm)

            moe = w1_ref[...] * _read_rows(u1, tm) + w2_ref[...] * _read_rows(u2, tm)
            o_ref[...] = _ln(ALPHA * _read_rows(x_ref, tm) + moe, g_ref[...], b_ref[...])


def _combine_ln(lpos_flat, tcnt, toff, rstart, y, xr, w1, w2, g, b, tm):
    n = xr.shape[0] // ROW_CHUNKS
    return pl.pallas_call(
        functools.partial(_combine_ln_kernel, n=n, tm=tm),
        grid_spec=pltpu.PrefetchScalarGridSpec(
            num_scalar_prefetch=4,
            grid=(n // tm,),
            in_specs=[pl.BlockSpec(memory_space=pl.ANY),
                      pl.BlockSpec((tm * ROW_CHUNKS, LANES), lambda i, *_: (i, 0)),
                      pl.BlockSpec((tm, 1), lambda i, *_: (i, 0)),
                      pl.BlockSpec((tm, 1), lambda i, *_: (i, 0)),
                      pl.BlockSpec((1, D_MODEL), lambda i, *_: (0, 0)),
                      pl.BlockSpec((1, D_MODEL), lambda i, *_: (0, 0))],
            out_specs=pl.BlockSpec((tm, D_MODEL), lambda i, *_: (i, 0)),
            scratch_shapes=[pltpu.VMEM((2 * tm * ROW_CHUNKS, LANES), F32)] * 2
                           + [pltpu.VMEM((tm * ROW_CHUNKS, LANES), F32)] * 2
                           + [pltpu.SemaphoreType.DMA((2,))]),
        out_shape=jax.ShapeDtypeStruct((n, D_MODEL), F32),
        compiler_params=_cparams("arbitrary"),
        name="moe_combine_ln",
    )(lpos_flat, tcnt, toff, rstart, y, xr, w1, w2, g, b)


def _moe_ln(xr, rw, rb, wg, wu, wd, layer, g, b):
    n = xr.shape[0] // ROW_CHUNKS
    n_blocks = (2 * n) // MOE_BM + N_EXPERTS
    tm = MOE_TILE
    nt = n // tm
    lpos, w, cnt, tcnt, toff, tbef = _router(xr, rw, rb, tm)
    meta, rstart = _plan(cnt, tbef, n_blocks)
    block_e = meta[0, :n_blocks]
    n_used = meta[1, :1]

    def per_tile(table):
        return table[:, :nt].T.reshape(nt * N_EXPERTS).astype(I32)

    lpos_flat = lpos.reshape(2 * n)
    tcnt, toff, rstart = per_tile(tcnt), per_tile(toff), per_tile(rstart)
    xs = _dispatch(lpos_flat, tcnt, toff, rstart, meta[2, :N_EXPERTS], meta[3, :N_EXPERTS], n_used, xr,
                   n_blocks, tm)
    y = _experts(block_e, n_used, xs, wg, wu, wd, layer)
    return _combine_ln(lpos_flat, tcnt, toff, rstart, y, xr, w[0].reshape(n, 1), w[1].reshape(n, 1), g, b, tm)


def _conv_qkv_kernel(xm_ref, cw_ref, cb_ref, wq_ref, wk_ref, wv_ref, q_ref, k_ref, v_ref, xc_ref, *, s):
    xm_b = xm_ref[0]
    xm = xm_b.astype(F32)
    cw = cw_ref[...]
    row = lax.broadcasted_iota(I32, (s, 1), 0)
    half = CONV_K // 2
    acc = cb_ref[...] + xm * cw[half:half + 1, :]
    for sh in range(1, half + 1):
        past = jnp.where(row >= sh, pltpu.roll(xm, sh, axis=0), 0.0)
        acc = acc + past * cw[half - sh:half - sh + 1, :]
        nxt = jnp.where(row < s - sh, pltpu.roll(xm, s - sh, axis=0), 0.0)
        acc = acc + nxt * cw[half + sh:half + sh + 1, :]
    xc = _silu(acc).astype(BF16)
    xc_ref[0] = xc
    q_ref[0] = _dot(xc, wq_ref[0]).astype(BF16)
    k_ref[0] = (_dot_nt(wk_ref[0], xc) * (ML_DH ** -0.5)).astype(BF16)
    v = _dot(xm_b, wv_ref[0])
    ones_lane = lax.broadcasted_iota(I32, (1, ML_DHP), 1) == ML_DH
    v_ref[0] = jnp.where(ones_lane, 1.0, v).astype(BF16)


def _conv_qkv(main3, cw, cb, wq, wk_t, wv):
    b, s, _ = main3.shape
    tok = pl.BlockSpec((1, s, ML_DHP), lambda i, h: (i, 0, h))
    wspec = pl.BlockSpec((1, ML_DHP, ML_DHP), lambda i, h: (h, 0, 0))
    tok_shape = jax.ShapeDtypeStruct((b, s, ML_WP), BF16)
    return pl.pallas_call(
        functools.partial(_conv_qkv_kernel, s=s),
        grid=(b, ML_HEADS),
        in_specs=[tok,
                  pl.BlockSpec((CONV_K, ML_DHP), lambda i, h: (0, h)),
                  pl.BlockSpec((1, ML_DHP), lambda i, h: (0, h)),
                  wspec, wspec, wspec],
        out_specs=[tok, pl.BlockSpec((1, ML_DHP, s), lambda i, h: (i, h, 0)), tok, tok],
        out_shape=[tok_shape, jax.ShapeDtypeStruct((b, ML_WP, s), BF16), tok_shape, tok_shape],
        compiler_params=_cparams("parallel", "parallel"),
        name="conv_qkv",
    )(main3, cw, cb, wq, wk_t, wv)


def _mlstm_kernel(q_ref, kt_ref, v_ref, gc_ref, gr_ref, z_ref, xc_ref, ng_ref, sk_ref,
                  y_ref, hf_ref, hb_ref, cf_ref, cb_ref, m_ref, *, s):
    head0 = pl.program_id(1) * ML_HPS
    nc = s // CHUNK
    sub = lax.broadcasted_iota(I32, (LANES, 1), 0)
    gate = lax.broadcasted_iota(I32, (LANES, LANES), 0)
    ti = lax.broadcasted_iota(I32, (CHUNK, CHUNK), 0)
    tj = lax.broadcasted_iota(I32, (CHUNK, CHUNK), 1)

    for ref in (cf_ref, cb_ref, m_ref):
        ref[...] = jnp.zeros_like(ref)

    def intra(c, j, rev):
        t0 = pl.multiple_of(c * CHUNK, CHUNK)
        hl = slice(j * ML_DHP, (j + 1) * ML_DHP)
        qb = q_ref[0, pl.ds(t0, CHUNK), hl]
        kt = kt_ref[0, hl, pl.ds(t0, CHUNK)]
        vb = v_ref[0, pl.ds(t0, CHUNK), hl]
        gc = gc_ref[0, pl.ds(t0, CHUNK), :]
        gr = gr_ref[:, pl.ds(t0, CHUNK)]
        i_idx = head0 + j + (2 * ML_HEADS if rev else 0)
        f_idx = i_idx + ML_HEADS
        allowed = (tj >= ti) if rev else (tj <= ti)
        sel = (gate == f_idx).astype(BF16)
        b_rep = sum(_dot(part, sel) for part in _split3(gc))
        b_row = jnp.sum(jnp.where(sub == f_idx, gr, 0.0), axis=0, keepdims=True)
        i_row = jnp.sum(jnp.where(sub == i_idx, gr, 0.0), axis=0, keepdims=True)
        b_last = (b_rep[0:1, :] if rev else b_rep[CHUNK - 1:CHUNK, :])[:, 0:1]

        b_wide = jnp.concatenate([b_rep] * (CHUNK // LANES), axis=1)
        d = jnp.where(allowed, b_wide - b_row + i_row, NEG)
        m_in = jnp.max(d, axis=1, keepdims=True)
        sc = _dot(qb, kt) * jnp.exp(d - m_in)
        nd_in = _dot(sc.astype(BF16), vb)
        w_row = b_last - b_row + i_row
        return t0, qb, kt, vb, b_rep, b_last, m_in, nd_in, w_row

    def twice(a):
        return jnp.concatenate([a, a], axis=1)

    def update(parts, j, rev):
        t0, qb, kt, vb, b_rep, b_last, m_in, nd_in, w_row = parts
        h_ref, c_ref = (hb_ref, cb_ref) if rev else (hf_ref, cf_ref)
        hl = slice(j * ML_DHP, (j + 1) * ML_DHP)
        mrow = 2 * j + int(rev)
        m = m_ref[mrow:mrow + 1, 0:1]
        cmat = c_ref[j]
        inter = b_rep + m
        m_t = jnp.maximum(m_in, inter)
        a_in = jnp.exp(m_in - m_t)
        iexp = jnp.exp(inter - m_t)
        nd = twice(a_in) * nd_in + twice(iexp) * _dot(qb, cmat.astype(BF16))
        den = nd[:, ML_DH:ML_DH + 1]
        h_ref[pl.ds(t0, CHUNK), hl] = nd * (1.0 / jnp.maximum(jnp.abs(den), jnp.exp(-m_t[:, 0:1])))

        m_new = jnp.maximum(b_last + m, jnp.max(w_row, axis=1, keepdims=True))
        wexp = jnp.exp(w_row - m_new)
        cexp = jnp.exp(b_last + m - m_new)
        kw = (kt.astype(F32) * wexp).astype(BF16)
        c_ref[j] = cexp * cmat + _dot(kw, vb)
        m_ref[mrow:mrow + 1, :] = jnp.broadcast_to(m_new, (1, LANES))

    def step(i, carry):
        for j in range(ML_HPS):
            parts = [intra(nc - 1 - i if rev else i, j, rev) for rev in (False, True)]
            for p, rev in zip(parts, (False, True)):
                update(p, j, rev)
        return carry

    lax.fori_loop(0, nc, step, 0)

    real = lax.broadcasted_iota(I32, (1, ML_DHP), 1) < ML_DH
    tb = CHUNK

    def fin(c, carry):
        t0 = pl.multiple_of(c * tb, tb)
        for j in range(ML_HPS):
            hl = slice(j * ML_DHP, (j + 1) * ML_DHP)
            hs = jnp.where(real, hf_ref[pl.ds(t0, tb), hl] + hb_ref[pl.ds(t0, tb), hl], 0.0)
            mu = jnp.sum(hs, axis=1, keepdims=True) * (1.0 / ML_DH)
            dev = jnp.where(real, hs - mu, 0.0)
            var = jnp.sum(dev * dev, axis=1, keepdims=True) * (1.0 / ML_DH)
            hn = dev * lax.rsqrt(var + LN_EPS) * ng_ref[:, hl]
            xc = xc_ref[0, pl.ds(t0, tb), hl].astype(F32)
            z = z_ref[0, pl.ds(t0, tb), hl].astype(F32)
            y_ref[0, pl.ds(t0, tb), hl] = ((hn + sk_ref[:, hl] * xc) * _silu(z)).astype(BF16)
        return carry

    lax.fori_loop(0, s // tb, fin, 0)


def _mlstm(q, kt, v, gcol3, grow, main3, xc, ng, sk):
    b, s, _ = q.shape
    width = ML_HPS * ML_DHP
    steps = ML_HEADS // ML_HPS
    tok = pl.BlockSpec((1, s, width), lambda i, h: (i, 0, h))
    vec = pl.BlockSpec((1, width), lambda i, h: (0, h))
    return pl.pallas_call(
        functools.partial(_mlstm_kernel, s=s),
        grid=(b, steps),
        in_specs=[tok, pl.BlockSpec((1, width, s), lambda i, h: (i, h, 0)), tok,
                  pl.BlockSpec((1, s, LANES), lambda i, h: (i, 0, 0)),
                  pl.BlockSpec((LANES, s), lambda i, h: (0, i)),
                  pl.BlockSpec((1, s, width), lambda i, h: (i, 0, steps + h)),
                  tok, vec, vec],
        out_specs=tok,
        out_shape=jax.ShapeDtypeStruct((b, s, ML_WP), BF16),
        scratch_shapes=[pltpu.VMEM((s, width), F32), pltpu.VMEM((s, width), F32),
                        pltpu.VMEM((ML_HPS, ML_DHP, ML_DHP), F32), pltpu.VMEM((ML_HPS, ML_DHP, ML_DHP), F32),
                        pltpu.VMEM((SUBLANES, LANES), F32)],
        compiler_params=_cparams("parallel", "parallel"),
        name="mlstm",
    )(q, kt, v, gcol3, grow, main3, xc, ng, sk)


def _pad_heads(a, axis):
    a = jnp.moveaxis(a, axis, -1)
    lead = a.shape[:-1]
    a = a.reshape(lead + (ML_HEADS, ML_DH))
    a = jnp.pad(a, [(0, 0)] * len(lead) + [(0, 0), (0, ML_DHP - ML_DH)])
    return jnp.moveaxis(a.reshape(lead + (ML_WP,)), -1, axis)


def kernel(x, mem, mem_ln_g, mem_ln_b, w_mem_kv, router_w, router_b, na_w_in, na_rpb, ml_w_in, ml_conv_w,
           ml_conv_b, ml_w_qkv, ml_gate_b, ml_norm_g, ml_skip, w_out, ln_g, ln_b, exp_w_gate, exp_w_up,
           exp_w_down):
    b, s, d = x.shape
    n = b * s
    nm = mem.shape[1]
    row = lambda a: a.reshape(1, -1)

    mem_k, mem_v = _memkv(mem.reshape(b * nm, d), row(mem_ln_g), row(mem_ln_b), w_mem_kv.astype(BF16))
    mem_k3 = mem_k.reshape(b, nm, MEM_W)
    mem_v3 = mem_v.reshape(b, nm, MEM_W)
    rw_pad = jnp.pad(router_w, ((0, 0), (0, LANES - N_EXPERTS)))
    rw_hi = rw_pad.astype(BF16)
    rw = (rw_hi, (rw_pad - rw_hi.astype(F32)).astype(BF16))
    rb = router_b.reshape(N_EXPERTS, 1)

    x2 = x.reshape(n, d)

    h0 = _proj(x2, na_w_in[0].astype(BF16)).reshape(b, s, 3 * NA_W + MEM_W)
    y_na = _na_attention(h0, _na_bias_table(na_rpb[0]))
    wo = w_out[0].astype(BF16)
    xr = _outproj_ln(y_na.reshape(n, NA_W), h0.reshape(n, 3 * NA_W + MEM_W), 3 * NA_W // MEM_W, mem_k3, mem_v3,
                     wo[:NA_W], wo[NA_W:], x2, row(ln_g[0, 0]), row(ln_b[0, 0]))
    x2 = _moe_ln(xr, rw, rb, exp_w_gate, exp_w_up, exp_w_down, 0, row(ln_g[0, 1]), row(ln_b[0, 1]))

    w1 = ml_w_in[0]
    w_main = jnp.concatenate([_pad_heads(w1[:, :ML_W], 1), _pad_heads(w1[:, ML_W:2 * ML_W], 1),
                              w1[:, 2 * ML_W + 4 * ML_HEADS:]], axis=1).astype(BF16)
    w_g = jnp.pad(w1[:, 2 * ML_W:2 * ML_W + 4 * ML_HEADS], ((0, 0), (0, LANES - 4 * ML_HEADS))).astype(BF16)
    gb = jnp.pad(ml_gate_b[0].reshape(4 * ML_HEADS), (0, LANES - 4 * ML_HEADS))
    main, acol, arow = _proj_gates(x2, w_main, w_g, w_g.T, gb.reshape(1, LANES), gb.reshape(LANES, 1))
    main3 = main.reshape(b, s, 2 * ML_WP + MEM_W)
    wqkv = jnp.pad(ml_w_qkv[0], ((0, 0), (0, 0), (0, ML_DHP - ML_DH), (0, ML_DHP - ML_DH))).astype(BF16)
    q, k, v, xc = _conv_qkv(main3, _pad_heads(ml_conv_w[0], 1), _pad_heads(row(ml_conv_b[0]), 1),
                            wqkv[0], jnp.swapaxes(wqkv[1], 1, 2), wqkv[2])
    y_ml = _mlstm(q, k, v, acol.reshape(b, s, LANES), arow, main3, xc,
                  _pad_heads(row(ml_norm_g[0]), 1), _pad_heads(row(ml_skip[0]), 1))
    wo = w_out[1]
    xr = _outproj_ln(y_ml.reshape(n, ML_WP), main, 2 * ML_WP // MEM_W, mem_k3, mem_v3,
                     _pad_heads(wo[:ML_W], 0).astype(BF16), wo[ML_W:].astype(BF16), x2,
                     row(ln_g[1, 0]), row(ln_b[1, 0]))
    x2 = _moe_ln(xr, rw, rb, exp_w_gate, exp_w_up, exp_w_down, 1, row(ln_g[1, 1]), row(ln_b[1, 1]))
    return x2.reshape(b, s, d)
```

```python
import functools

import numpy as np
import jax
import jax.numpy as jnp
from jax import lax
from jax.experimental import pallas as pl
from jax.experimental.pallas import tpu as pltpu

F32 = jnp.float32
BF16 = jnp.bfloat16
I32 = jnp.int32

D_MODEL = 1024
DEPTH = 2
GRID_W = 64
MEM_HEADS = 4
MEM_DH = 64
MEM_W = MEM_HEADS * MEM_DH
NA_HEADS = 12
NA_DH = 64
NA_W = NA_HEADS * NA_DH
WIN_H = 8
WIN_W = 16
ML_HEADS = 4
ML_DH = 192
ML_DHP = 256
ML_W = ML_HEADS * ML_DH
ML_WP = ML_HEADS * ML_DHP
CONV_K = 5
CHUNK = 256
N_EXPERTS = 16
N_GROUPS = 4
EXPERTS_PER_GROUP = N_EXPERTS // N_GROUPS
D_EXPERT = 512
ALPHA = (2 * DEPTH) ** 0.25
LN_EPS = 1e-5
NEG = -1e30

LANES = 128
SUBLANES = 8
ROW_CHUNKS = D_MODEL // LANES
MOE_BM = 512
MOE_TILE = 512
ML_HPS = 2
NA_ROWS_PER_STEP = 16
ROW_TILE = 1024
PLACE_UNROLL = 8
VMEM_LIMIT = 48 * 1024 * 1024


def _cparams(*sem):
    return pltpu.CompilerParams(dimension_semantics=sem, vmem_limit_bytes=VMEM_LIMIT)


def _dot(a, b):
    return jnp.dot(a, b, preferred_element_type=F32)


def _dot_nt(a, b, precision=None):
    return lax.dot_general(a, b, (((1,), (1,)), ((), ())), precision=precision,
                           preferred_element_type=F32)


def _ln(z, g, b):
    mu = jnp.mean(z, axis=-1, keepdims=True)
    zc = z - mu
    var = jnp.mean(zc * zc, axis=-1, keepdims=True)
    return zc * lax.rsqrt(var + LN_EPS) * g + b


def _silu(x):
    return x * jax.nn.sigmoid(x)


def _read_rows(ref, n):
    return jnp.concatenate([ref[pl.ds(j, n, stride=ROW_CHUNKS), :] for j in range(ROW_CHUNKS)], axis=1)


def _write_rows(ref, val, n):
    for j in range(ROW_CHUNKS):
        ref[pl.ds(j, n, stride=ROW_CHUNKS), :] = val[:, j * LANES:(j + 1) * LANES]


def _memkv_kernel(m_ref, g_ref, b_ref, w_ref, k_ref, v_ref):
    z = _ln(m_ref[...], g_ref[...], b_ref[...])
    kv = _dot(z.astype(BF16), w_ref[...])
    k_ref[...] = kv[:, :MEM_W].astype(BF16)
    v_ref[...] = kv[:, MEM_W:].astype(BF16)


def _memkv(mem2, g, b, w):
    n = mem2.shape[0]
    tm = min(ROW_TILE, n)
    return pl.pallas_call(
        _memkv_kernel,
        grid=(n // tm,),
        in_specs=[pl.BlockSpec((tm, D_MODEL), lambda i: (i, 0)),
                  pl.BlockSpec((1, D_MODEL), lambda i: (0, 0)),
                  pl.BlockSpec((1, D_MODEL), lambda i: (0, 0)),
                  pl.BlockSpec((D_MODEL, 2 * MEM_W), lambda i: (0, 0))],
        out_specs=[pl.BlockSpec((tm, MEM_W), lambda i: (i, 0)),
                   pl.BlockSpec((tm, MEM_W), lambda i: (i, 0))],
        out_shape=[jax.ShapeDtypeStruct((n, MEM_W), BF16)] * 2,
        compiler_params=_cparams("parallel"),
        name="memkv",
    )(mem2, g, b, w)


def _proj_kernel(x_ref, w_ref, o_ref):
    o_ref[...] = _dot(x_ref[...].astype(BF16), w_ref[...]).astype(o_ref.dtype)


def _proj(x2, w, tm=ROW_TILE):
    n, k = x2.shape
    nout = w.shape[1]
    return pl.pallas_call(
        _proj_kernel,
        grid=(n // tm,),
        in_specs=[pl.BlockSpec((tm, k), lambda i: (i, 0)),
                  pl.BlockSpec((k, nout), lambda i: (0, 0))],
        out_specs=pl.BlockSpec((tm, nout), lambda i: (i, 0)),
        out_shape=jax.ShapeDtypeStruct((n, nout), BF16),
        compiler_params=_cparams("parallel"),
        name="in_proj",
    )(x2, w)


def _split3(x):
    hi = x.astype(BF16)
    r1 = x - hi.astype(F32)
    mid = r1.astype(BF16)
    lo = (r1 - mid.astype(F32)).astype(BF16)
    return hi, mid, lo


def _proj_gates_kernel(x_ref, w_ref, wg_ref, wgt_ref, gbc_ref, gbr_ref, o_ref, g_ref, gt_ref, *, tm):
    xb = x_ref[...].astype(BF16)
    o_ref[...] = _dot(xb, w_ref[...]).astype(BF16)
    gcol = _dot(xb, wg_ref[...]) + gbc_ref[...]
    grow = _dot_nt(wgt_ref[...], xb) + gbr_ref[...]
    lane = lax.broadcasted_iota(I32, (1, LANES), 1)
    sub = lax.broadcasted_iota(I32, (LANES, 1), 0)
    ti = lax.broadcasted_iota(I32, (CHUNK, CHUNK), 0)
    tj = lax.broadcasted_iota(I32, (CHUNK, CHUNK), 1)
    lower = (tj <= ti).astype(BF16)
    upper = (ti <= tj).astype(BF16)

    def pick(idx, pre, suf, raw):
        fwd = jnp.logical_and(idx >= ML_HEADS, idx < 2 * ML_HEADS)
        bwd = jnp.logical_and(idx >= 3 * ML_HEADS, idx < 4 * ML_HEADS)
        return jnp.where(fwd, pre, jnp.where(bwd, suf, raw))

    for c in range(tm // CHUNK):
        tc = slice(c * CHUNK, (c + 1) * CHUNK)
        g = gcol[tc, :]
        ls = jax.nn.log_sigmoid(g)
        pre = sum(_dot(lower, part) for part in _split3(ls))
        suf = jnp.sum(ls, axis=0, keepdims=True) - pre + ls
        g_ref[tc, :] = pick(lane, pre, suf, g)
        g = grow[:, tc]
        ls = jax.nn.log_sigmoid(g)
        pre = sum(_dot(part, upper) for part in _split3(ls))
        suf = jnp.sum(ls, axis=1, keepdims=True) - pre + ls
        gt_ref[:, tc] = pick(sub, pre, suf, g)


def _proj_gates(x2, w, wg, wgt, gbc, gbr, tm=ROW_TILE):
    n, k = x2.shape
    nout = w.shape[1]
    return pl.pallas_call(
        functools.partial(_proj_gates_kernel, tm=tm),
        grid=(n // tm,),
        in_specs=[pl.BlockSpec((tm, k), lambda i: (i, 0)),
                  pl.BlockSpec((k, nout), lambda i: (0, 0)),
                  pl.BlockSpec((k, LANES), lambda i: (0, 0)),
                  pl.BlockSpec((LANES, k), lambda i: (0, 0)),
                  pl.BlockSpec((1, LANES), lambda i: (0, 0)),
                  pl.BlockSpec((LANES, 1), lambda i: (0, 0))],
        out_specs=[pl.BlockSpec((tm, nout), lambda i: (i, 0)),
                   pl.BlockSpec((tm, LANES), lambda i: (i, 0)),
                   pl.BlockSpec((LANES, tm), lambda i: (0, i))],
        out_shape=[jax.ShapeDtypeStruct((n, nout), BF16),
                   jax.ShapeDtypeStruct((n, LANES), F32),
                   jax.ShapeDtypeStruct((LANES, n), F32)],
        compiler_params=_cparams("parallel"),
        name="in_proj_gates",
    )(x2, w, wg, wgt, gbc, gbr)


def _na_kernel(q_ref, k_ref, v_ref, tbl_ref, o_ref, *, rows):
    lane = lax.broadcasted_iota(I32, (1, LANES), 1)
    first = lane < NA_DH
    nkeys = WIN_H * GRID_W

    def rows_step(i, carry):
        rr = [i * NA_ROWS_PER_STEP + u for u in range(NA_ROWS_PER_STEP)]
        rss = [jnp.clip(r - WIN_H // 2, 0, rows - WIN_H) for r in rr]
        scores = []
        for r, rs in zip(rr, rss):
            q = q_ref[0, pl.ds(pl.multiple_of(r * GRID_W, GRID_W), GRID_W), :]
            q = q * jnp.asarray(NA_DH ** -0.5, BF16)
            q2 = jnp.concatenate([jnp.where(first, q, jnp.zeros_like(q)),
                                  jnp.where(first, jnp.zeros_like(q), q)], axis=0)
            k = k_ref[0, pl.ds(pl.multiple_of(rs * GRID_W, GRID_W), nkeys), :]
            dr0 = rs - r + WIN_H - 1
            bias = jnp.concatenate(
                [jnp.concatenate([tbl_ref[0, half, dr0 + 2 * m] for m in range(WIN_H // 2)], axis=1)
                 for half in range(2)], axis=0)
            scores.append(_dot_nt(q2, k) + bias)
        probs = []
        for s in scores:
            p = jnp.exp(s - jnp.max(s, axis=-1, keepdims=True))
            probs.append((p.astype(BF16), jnp.sum(p, axis=-1, keepdims=True)))
        for r, rs, (p, l) in zip(rr, rss, probs):
            v = v_ref[0, pl.ds(pl.multiple_of(rs * GRID_W, GRID_W), nkeys), :]
            o = _dot(p, v) / l
            o = jnp.where(first, o[:GRID_W], o[GRID_W:])
            o_ref[0, pl.ds(pl.multiple_of(r * GRID_W, GRID_W), GRID_W), :] = o.astype(o_ref.dtype)
        return carry

    lax.fori_loop(0, rows // NA_ROWS_PER_STEP, rows_step, 0)


def _na_bias_table(rpb):
    qc = np.arange(GRID_W)[:, None]
    kc = np.arange(GRID_W)[None, :]
    cs = np.clip(qc - WIN_W // 2, 0, GRID_W - WIN_W)
    col_in = (kc >= cs) & (kc < cs + WIN_W)
    side = GRID_W - WIN_W
    wide = jnp.pad(rpb, ((0, 0), (0, 0), (side, side)))
    t = jnp.stack([wide[:, :, GRID_W - 1 - q:2 * GRID_W - 1 - q] for q in range(GRID_W)], axis=2)
    t = jnp.where(col_in, t, NEG).astype(F32)
    t2 = jnp.concatenate([t[:, :-1], t[:, 1:]], axis=-1)
    return t2.reshape(NA_HEADS // 2, 2, 2 * WIN_H - 2, GRID_W, 2 * GRID_W)


def _na_attention(h3, tbl):
    b, s, _ = h3.shape
    rows = s // GRID_W
    npair = NA_HEADS // 2
    return pl.pallas_call(
        functools.partial(_na_kernel, rows=rows),
        grid=(b, npair),
        in_specs=[pl.BlockSpec((1, s, LANES), lambda i, p: (i, 0, p)),
                  pl.BlockSpec((1, s, LANES), lambda i, p: (i, 0, npair + p)),
                  pl.BlockSpec((1, s, LANES), lambda i, p: (i, 0, 2 * npair + p)),
                  pl.BlockSpec((1, 2, 2 * WIN_H - 2, GRID_W, 2 * GRID_W), lambda i, p: (p, 0, 0, 0, 0))],
        out_specs=pl.BlockSpec((1, s, LANES), lambda i, p: (i, 0, p)),
        out_shape=jax.ShapeDtypeStruct((b, s, NA_W), BF16),
        compiler_params=_cparams("parallel", "parallel"),
        name="na_attention",
    )(h3, h3, h3, tbl)


def _outproj_ln_kernel(ya_ref, qm_ref, mk_ref, mv_ref, wa_ref, wm_ref, x_ref, g_ref, b_ref, or_ref, *, tm):
    lane = lax.broadcasted_iota(I32, (1, LANES), 1)
    first = lane < MEM_DH
    q = qm_ref[...] * jnp.asarray(MEM_DH ** -0.5, BF16)
    cols = [slice(p * LANES, (p + 1) * LANES) for p in range(MEM_HEADS // 2)]
    scores = []
    for c in cols:
        qp = q[:, c]
        q2 = jnp.concatenate([jnp.where(first, qp, jnp.zeros_like(qp)),
                              jnp.where(first, jnp.zeros_like(qp), qp)], axis=0)
        scores.append(_dot_nt(q2, mk_ref[0, :, c]))
    probs = []
    for s in scores:
        p = jnp.exp(s - jnp.max(s, axis=-1, keepdims=True))
        probs.append((p.astype(BF16), jnp.sum(p, axis=-1, keepdims=True)))
    outs = []
    for c, (p, l) in zip(cols, probs):
        o = _dot(p, mv_ref[0, :, c]) / l
        outs.append(jnp.where(first, o[:tm], o[tm:]))
    ym = jnp.concatenate(outs, axis=1).astype(BF16)
    acc = _dot(ya_ref[...], wa_ref[...]) + _dot(ym, wm_ref[...])
    _write_rows(or_ref, _ln(ALPHA * x_ref[...] + acc, g_ref[...], b_ref[...]), tm)


def _outproj_ln(ya, h2, qm_block, mem_k3, mem_v3, wa, wm, x2, g, b, tm=ROW_TILE):
    n = x2.shape[0]
    ka = ya.shape[1]
    nb, nm, _ = mem_k3.shape
    per_batch = n // nb // tm
    full = lambda shape: pl.BlockSpec(shape, lambda i: (0,) * len(shape))
    return pl.pallas_call(
        functools.partial(_outproj_ln_kernel, tm=tm),
        grid=(n // tm,),
        in_specs=[pl.BlockSpec((tm, ka), lambda i: (i, 0)),
                  pl.BlockSpec((tm, MEM_W), lambda i: (i, qm_block)),
                  pl.BlockSpec((1, nm, MEM_W), lambda i: (i // per_batch, 0, 0)),
                  pl.BlockSpec((1, nm, MEM_W), lambda i: (i // per_batch, 0, 0)),
                  full((ka, D_MODEL)), full((MEM_W, D_MODEL)),
                  pl.BlockSpec((tm, D_MODEL), lambda i: (i, 0)),
                  full((1, D_MODEL)), full((1, D_MODEL))],
        out_specs=pl.BlockSpec((tm * ROW_CHUNKS, LANES), lambda i: (i, 0)),
        out_shape=jax.ShapeDtypeStruct((n * ROW_CHUNKS, LANES), F32),
        compiler_params=_cparams("parallel"),
        name="outproj_ln",
    )(ya, h2, mem_k3, mem_v3, wa, wm, x2, g, b)


def _router_kernel(x_ref, rwh_ref, rwl_ref, rb_ref, lpos_ref, w_ref, cnt_ref, tcnt_ref, toff_ref, tbef_ref, *, tm):
    @pl.when(pl.program_id(0) == 0)
    def _():
        cnt_ref[...] = jnp.zeros_like(cnt_ref)

    x = _read_rows(x_ref, tm)
    xh = x.astype(BF16)
    xl = (x - xh.astype(F32)).astype(BF16)
    logits_t = _dot(xh, rwh_ref[...]) + (_dot(xh, rwl_ref[...]) + _dot(xl, rwh_ref[...]))
    logits = logits_t.T[:N_EXPERTS]
    scores = jax.nn.sigmoid(logits)
    biased = scores + rb_ref[...]
    bv = [biased[e:e + 1, :] for e in range(N_EXPERTS)]
    sv = [scores[e:e + 1, :] for e in range(N_EXPERTS)]

    grp = []
    for g in range(N_GROUPS):
        m = bv[g * EXPERTS_PER_GROUP:(g + 1) * EXPERTS_PER_GROUP]
        best = None
        for a in range(EXPERTS_PER_GROUP):
            for c in range(a + 1, EXPERTS_PER_GROUP):
                pair = m[a] + m[c]
                best = pair if best is None else jnp.maximum(best, pair)
        grp.append(best)
    gsel = jnp.zeros((1, tm), I32)
    gbest = grp[0]
    for g in range(1, N_GROUPS):
        better = grp[g] > gbest
        gsel = jnp.where(better, g, gsel)
        gbest = jnp.where(better, grp[g], gbest)

    def pick(vals, j):
        out = vals[j]
        for g in range(1, N_GROUPS):
            out = jnp.where(gsel == g, vals[g * EXPERTS_PER_GROUP + j], out)
        return out

    cb = [pick(bv, j) for j in range(EXPERTS_PER_GROUP)]
    cs = [pick(sv, j) for j in range(EXPERTS_PER_GROUP)]
    i1 = jnp.zeros((1, tm), I32)
    m1 = cb[0]
    s1 = cs[0]
    for j in range(1, EXPERTS_PER_GROUP):
        gt = cb[j] > m1
        i1 = jnp.where(gt, j, i1)
        m1 = jnp.where(gt, cb[j], m1)
        s1 = jnp.where(gt, cs[j], s1)
    i2 = jnp.zeros((1, tm), I32)
    m2 = jnp.full((1, tm), -jnp.inf, F32)
    s2 = jnp.zeros((1, tm), F32)
    for j in range(EXPERTS_PER_GROUP):
        ok = jnp.logical_and(i1 != j, cb[j] > m2)
        i2 = jnp.where(ok, j, i2)
        m2 = jnp.where(ok, cb[j], m2)
        s2 = jnp.where(ok, cs[j], s2)
    e1 = gsel * EXPERTS_PER_GROUP + i1
    e2 = gsel * EXPERTS_PER_GROUP + i2
    tot = s1 + s2
    w_ref[...] = jnp.concatenate([s1 / tot, s2 / tot], axis=0)

    i = pl.program_id(0)
    eio = lax.broadcasted_iota(I32, (N_EXPERTS, tm), 0)
    oh1 = eio == e1
    oh2 = eio == e2
    ohs = jnp.logical_or(oh1, oh2).astype(F32)
    before = (lax.broadcasted_iota(I32, (tm, tm), 0) < lax.broadcasted_iota(I32, (tm, tm), 1))
    pre = _dot(ohs.astype(BF16), before.astype(BF16))
    tile_cnt = jnp.sum(ohs, axis=1, keepdims=True)
    offs = []
    acc = jnp.zeros((1, 1), F32)
    for e in range(N_EXPERTS):
        offs.append(acc)
        acc = acc + tile_cnt[e:e + 1, :]
    tile_off = jnp.concatenate(offs, axis=0)
    pos = tile_off + pre
    p1 = jnp.sum(jnp.where(oh1, pos, 0.0), axis=0, keepdims=True)
    p2 = jnp.sum(jnp.where(oh2, pos, 0.0), axis=0, keepdims=True)
    lpos_ref[...] = jnp.concatenate([p1, p2], axis=0).astype(I32) * ROW_CHUNKS

    @pl.when(i == 0)
    def _():
        for ref in (tcnt_ref, toff_ref, tbef_ref):
            ref[...] = jnp.zeros_like(ref)

    here = lax.broadcasted_iota(I32, (1, LANES), 1) == i
    tcnt_ref[...] = jnp.where(here, tile_cnt, tcnt_ref[...])
    toff_ref[...] = jnp.where(here, tile_off, toff_ref[...])
    tbef_ref[...] = jnp.where(here, cnt_ref[:, 0:1], tbef_ref[...])
    cnt_ref[...] += tile_cnt


def _router(xr, rw, rb, tm):
    n = xr.shape[0] // ROW_CHUNKS
    assert n // tm <= LANES
    table = pl.BlockSpec((N_EXPERTS, LANES), lambda i: (0, 0))
    return pl.pallas_call(
        functools.partial(_router_kernel, tm=tm),
        grid=(n // tm,),
        in_specs=[pl.BlockSpec((tm * ROW_CHUNKS, LANES), lambda i: (i, 0)),
                  pl.BlockSpec((D_MODEL, LANES), lambda i: (0, 0)),
                  pl.BlockSpec((D_MODEL, LANES), lambda i: (0, 0)),
                  pl.BlockSpec((N_EXPERTS, 1), lambda i: (0, 0))],
        out_specs=[pl.BlockSpec((2, tm), lambda i: (0, i)),
                   pl.BlockSpec((2, tm), lambda i: (0, i)),
                   table, table, table, table],
        out_shape=[jax.ShapeDtypeStruct((2, n), I32),
                   jax.ShapeDtypeStruct((2, n), F32)]
                  + [jax.ShapeDtypeStruct((N_EXPERTS, LANES), F32)] * 4,
        compiler_params=_cparams("arbitrary"),
        name="router",
    )(xr, rw[0], rw[1], rb)


def _plan_kernel(cnt_ref, tbef_ref, meta_ref, rstart_ref, *, nbl):
    shift = MOE_BM.bit_length() - 1
    cnt = cnt_ref[...].astype(I32)
    padded = ((cnt + (MOE_BM - 1)) >> shift) << shift
    starts = []
    acc = jnp.zeros((1, LANES), I32)
    for e in range(N_EXPERTS):
        starts.append(acc)
        acc = acc + padded[e:e + 1, :]
    pad_start = jnp.concatenate(starts, axis=0)
    pad_end = pad_start + padded
    rstart_ref[...] = pad_start + tbef_ref[...].astype(I32)
    blk0 = lax.broadcasted_iota(I32, (N_EXPERTS, nbl), 1) * MOE_BM
    block_e = jnp.sum((pad_end[:, 0:1] <= blk0).astype(I32), axis=0, keepdims=True)
    block_e = jnp.minimum(block_e, N_EXPERTS - 1)
    n_used = jnp.broadcast_to(acc[:, 0:1] >> shift, (1, nbl))
    diag = lax.broadcasted_iota(I32, (N_EXPERTS, nbl), 0) == lax.broadcasted_iota(I32, (N_EXPERTS, nbl), 1)
    fill_lo = jnp.sum(jnp.where(diag, (pad_start + cnt)[:, 0:1], 0), axis=0, keepdims=True)
    fill_hi = jnp.sum(jnp.where(diag, pad_end[:, 0:1], 0), axis=0, keepdims=True)
    own = lax.broadcasted_iota(I32, (N_EXPERTS, nbl), 0) == block_e
    first_pad = jnp.sum(jnp.where(own, (pad_start + cnt)[:, 0:1], 0), axis=0, keepdims=True)
    block_rows = jnp.clip(first_pad - blk0[0:1, :], 0, MOE_BM)
    meta_ref[...] = jnp.concatenate([block_e, n_used, fill_lo, fill_hi, block_rows,
                                     jnp.zeros((SUBLANES - 5, nbl), I32)], axis=0)


def _plan(cnt, tbef, n_blocks):
    nbl = -(-n_blocks // LANES) * LANES
    table = pl.BlockSpec((N_EXPERTS, LANES), lambda i: (0, 0))
    return pl.pallas_call(
        functools.partial(_plan_kernel, nbl=nbl),
        grid=(1,),
        in_specs=[table, table],
        out_specs=[pl.BlockSpec((SUBLANES, nbl), lambda i: (0, 0)), table],
        out_shape=[jax.ShapeDtypeStruct((SUBLANES, nbl), I32),
                   jax.ShapeDtypeStruct((N_EXPERTS, LANES), I32)],
        compiler_params=_cparams("arbitrary"),
        name="moe_plan",
    )(cnt, tbef)


def _rows(ref, row, nrows):
    return ref.at[pl.ds(pl.multiple_of(row * ROW_CHUNKS, ROW_CHUNKS), nrows * ROW_CHUNKS), :]


def _rows_wait(src_hbm, buf, sem):
    pltpu.make_async_copy(src_hbm.at[pl.ds(0, buf.shape[0]), :], buf, sem).wait()


def _copy_pieces(src, src_row, dst, dst_row, count, max_rows, sem, wait=False):
    bit = max_rows.bit_length() - 1
    while bit >= 0:
        size = 1 << bit
        done = (count >> (bit + 1)) << (bit + 1)

        @pl.when(((count >> bit) & 1) == 1)
        def _():
            cp = pltpu.make_async_copy(_rows(src, src_row + done, size), _rows(dst, dst_row + done, size), sem)
            cp.start()
            if wait:
                cp.wait()

        bit -= 1


def _tile_runs(tcnt_ref, toff_ref, rstart_ref, tile, buf, hbm, sem, *, to_hbm, tm):
    def per_expert(e, carry):
        k = tile * N_EXPERTS + e
        if to_hbm:
            _copy_pieces(buf, toff_ref[k], hbm, rstart_ref[k], tcnt_ref[k], tm, sem)
        else:
            _copy_pieces(hbm, rstart_ref[k], buf, toff_ref[k], tcnt_ref[k], tm, sem)
        return carry

    lax.fori_loop(0, N_EXPERTS, per_expert, 0)


def _dispatch_kernel(lpos_ref, tcnt_ref, toff_ref, rstart_ref, flo_ref, fhi_ref, nu_ref, x_ref, xs_hbm,
                     s0, s1, zbuf, sem, zsem, *, n, tm, n_blocks):
    i = pl.program_id(0)
    nt = pl.num_programs(0)
    bufs = (s0, s1)
    unroll = PLACE_UNROLL

    for slot in range(2):
        @pl.when(i % 2 == slot)
        def _():
            buf = bufs[slot]

            @pl.when(i >= 2)
            def _():
                _rows_wait(xs_hbm, buf, sem.at[slot])

            def place(c, carry):
                tok = i * tm + c * unroll
                src = pl.multiple_of(c * (unroll * ROW_CHUNKS), unroll * ROW_CHUNKS)
                for u in range(unroll):
                    v = x_ref[pl.ds(src + u * ROW_CHUNKS, ROW_CHUNKS), :]
                    for k in range(2):
                        p = lpos_ref[k * n + tok + u]
                        buf[pl.ds(pl.multiple_of(p, ROW_CHUNKS), ROW_CHUNKS), :] = v
                return carry

            lax.fori_loop(0, tm // unroll, place, 0)
            _tile_runs(tcnt_ref, toff_ref, rstart_ref, i, buf, xs_hbm, sem.at[slot], to_hbm=True, tm=tm)

    @pl.when(i == nt - 1)
    def _():
        for slot in range(2):
            @pl.when(nt > slot)
            def _():
                _rows_wait(xs_hbm, bufs[slot], sem.at[slot])

        zbuf[...] = jnp.zeros_like(zbuf)
        for e in range(N_EXPERTS):
            _copy_pieces(zbuf, 0, xs_hbm, flo_ref[e], fhi_ref[e] - flo_ref[e], MOE_BM // 2, zsem, wait=True)

        def zero_block(j, carry):
            cp = pltpu.make_async_copy(zbuf, _rows(xs_hbm, j * MOE_BM, MOE_BM), zsem)
            cp.start()
            cp.wait()
            return carry

        lax.fori_loop(nu_ref[0], n_blocks, zero_block, 0)


def _dispatch(lpos_flat, tcnt, toff, rstart, fill_lo, fill_hi, n_used, xr, n_blocks, tm):
    n = xr.shape[0] // ROW_CHUNKS
    return pl.pallas_call(
        functools.partial(_dispatch_kernel, n=n, tm=tm, n_blocks=n_blocks),
        grid_spec=pltpu.PrefetchScalarGridSpec(
            num_scalar_prefetch=7,
            grid=(n // tm,),
            in_specs=[pl.BlockSpec((tm * ROW_CHUNKS, LANES), lambda i, *_: (i, 0))],
            out_specs=pl.BlockSpec(memory_space=pl.ANY),
            scratch_shapes=[pltpu.VMEM((2 * tm * ROW_CHUNKS, LANES), F32),
                            pltpu.VMEM((2 * tm * ROW_CHUNKS, LANES), F32),
                            pltpu.VMEM((MOE_BM * ROW_CHUNKS, LANES), F32),
                            pltpu.SemaphoreType.DMA((2,)),
                            pltpu.SemaphoreType.DMA(())]),
        out_shape=jax.ShapeDtypeStruct((n_blocks * MOE_BM * ROW_CHUNKS, LANES), F32),
        compiler_params=_cparams("arbitrary"),
        name="moe_dispatch",
    )(lpos_flat, tcnt, toff, rstart, fill_lo, fill_hi, n_used, xr)


def _experts_kernel(be_ref, nu_ref, br_ref, xs_ref, wg_ref, wu_ref, wd_ref, y_ref, wgb, wub, wdb):
    j = pl.program_id(0)
    used = j < nu_ref[0]
    half = MOE_BM // 2
    half_rows = half * ROW_CHUNKS

    @pl.when(jnp.logical_and(used, jnp.logical_or(j == 0, be_ref[j] != be_ref[jnp.maximum(j - 1, 0)])))
    def _():
        wgb[...] = wg_ref[0, 0].astype(BF16)
        wub[...] = wu_ref[0, 0].astype(BF16)
        wdb[...] = wd_ref[0, 0].astype(BF16)

    def ffn(x_ref, o_ref, rows):
        x = _read_rows(x_ref, rows).astype(BF16)
        h = _silu(_dot(x, wgb[...])) * _dot(x, wub[...])
        _write_rows(o_ref, _dot(h.astype(BF16), wdb[...]), rows)

    @pl.when(jnp.logical_and(used, br_ref[j] > half))
    def _():
        ffn(xs_ref, y_ref, MOE_BM)

    @pl.when(jnp.logical_and(used, br_ref[j] <= half))
    def _():
        ffn(xs_ref.at[pl.ds(0, half_rows), :], y_ref.at[pl.ds(0, half_rows), :], half)
        y_ref[pl.ds(half_rows, half_rows), :] = jnp.zeros((half_rows, LANES), F32)

    @pl.when(jnp.logical_not(used))
    def _():
        y_ref[...] = jnp.zeros_like(y_ref)


def _experts(block_e, n_used, block_rows, xs, wg, wu, wd, layer):
    n_blocks = block_e.shape[0]

    def last_used(j, nu):
        return jnp.minimum(j, nu[0] - 1)

    def wblk(j, be, nu, br):
        return (layer, be[last_used(j, nu)], 0, 0)

    return pl.pallas_call(
        _experts_kernel,
        grid_spec=pltpu.PrefetchScalarGridSpec(
            num_scalar_prefetch=3,
            grid=(n_blocks,),
            in_specs=[pl.BlockSpec((MOE_BM * ROW_CHUNKS, LANES), lambda j, be, nu, br: (last_used(j, nu), 0)),
                      pl.BlockSpec((1, 1, D_MODEL, D_EXPERT), wblk),
                      pl.BlockSpec((1, 1, D_MODEL, D_EXPERT), wblk),
                      pl.BlockSpec((1, 1, D_EXPERT, D_MODEL), wblk)],
            out_specs=pl.BlockSpec((MOE_BM * ROW_CHUNKS, LANES), lambda j, be, nu, br: (j, 0)),
            scratch_shapes=[pltpu.VMEM((D_MODEL, D_EXPERT), BF16), pltpu.VMEM((D_MODEL, D_EXPERT), BF16),
                            pltpu.VMEM((D_EXPERT, D_MODEL), BF16)]),
        out_shape=jax.ShapeDtypeStruct(xs.shape, F32),
        compiler_params=_cparams("arbitrary"),
        name="moe_experts",
    )(block_e, n_used, block_rows, xs, wg, wu, wd)


def _combine_ln_kernel(lpos_ref, tcnt_ref, toff_ref, rstart_ref, y_hbm, x_ref, w1_ref, w2_ref, g_ref, b_ref, o_ref,
                       r0, r1, u1, u2, sem, *, n, tm):
    i = pl.program_id(0)
    nt = pl.num_programs(0)
    bufs = (r0, r1)
    unroll = PLACE_UNROLL

    def fetch(tile, slot):
        _tile_runs(tcnt_ref, toff_ref, rstart_ref, tile, bufs[slot], y_hbm, sem.at[slot], to_hbm=False, tm=tm)

    @pl.when(i == 0)
    def _():
        fetch(0, 0)

    for slot in range(2):
        @pl.when(i % 2 == slot)
        def _():
            @pl.when(i + 1 < nt)
            def _():
                fetch(i + 1, 1 - slot)

            buf = bufs[slot]
            _rows_wait(y_hbm, buf, sem.at[slot])

            def place(c, carry):
                tok = i * tm + c * unroll
                dst0 = pl.multiple_of(c * (unroll * ROW_CHUNKS), unroll * ROW_CHUNKS)
                for u in range(unroll):
                    dst = pl.ds(dst0 + u * ROW_CHUNKS, ROW_CHUNKS)
                    for k, out in enumerate((u1, u2)):
                        p = lpos_ref[k * n + tok + u]
                        out[dst, :] = buf[pl.ds(pl.multiple_of(p, ROW_CHUNKS), ROW_CHUNKS), :]
                return carry

            lax.fori_loop(0, tm // unroll, place, 0)
            moe = w1_ref[...] * _read_rows(u1, tm) + w2_ref[...] * _read_rows(u2, tm)
            o_ref[...] = _ln(ALPHA * _read_rows(x_ref, tm) + moe, g_ref[...], b_ref[...])


def _combine_ln(lpos_flat, tcnt, toff, rstart, y, xr, w1, w2, g, b, tm):
    n = xr.shape[0] // ROW_CHUNKS
    return pl.pallas_call(
        functools.partial(_combine_ln_kernel, n=n, tm=tm),
        grid_spec=pltpu.PrefetchScalarGridSpec(
            num_scalar_prefetch=4,
            grid=(n // tm,),
            in_specs=[pl.BlockSpec(memory_space=pl.ANY),
                      pl.BlockSpec((tm * ROW_CHUNKS, LANES), lambda i, *_: (i, 0)),
                      pl.BlockSpec((tm, 1), lambda i, *_: (i, 0)),
                      pl.BlockSpec((tm, 1), lambda i, *_: (i, 0)),
                      pl.BlockSpec((1, D_MODEL), lambda i, *_: (0, 0)),
                      pl.BlockSpec((1, D_MODEL), lambda i, *_: (0, 0))],
            out_specs=pl.BlockSpec((tm, D_MODEL), lambda i, *_: (i, 0)),
            scratch_shapes=[pltpu.VMEM((2 * tm * ROW_CHUNKS, LANES), F32)] * 2
                           + [pltpu.VMEM((tm * ROW_CHUNKS, LANES), F32)] * 2
                           + [pltpu.SemaphoreType.DMA((2,))]),
        out_shape=jax.ShapeDtypeStruct((n, D_MODEL), F32),
        compiler_params=_cparams("arbitrary"),
        name="moe_combine_ln",
    )(lpos_flat, tcnt, toff, rstart, y, xr, w1, w2, g, b)


def _moe_ln(xr, rw, rb, wg, wu, wd, layer, g, b):
    n = xr.shape[0] // ROW_CHUNKS
    n_blocks = (2 * n) // MOE_BM + N_EXPERTS
    tm = MOE_TILE
    nt = n // tm
    lpos, w, cnt, tcnt, toff, tbef = _router(xr, rw, rb, tm)
    meta, rstart = _plan(cnt, tbef, n_blocks)
    block_e = meta[0, :n_blocks]
    n_used = meta[1, :1]

    def per_tile(table):
        return table[:, :nt].T.reshape(nt * N_EXPERTS).astype(I32)

    lpos_flat = lpos.reshape(2 * n)
    tcnt, toff, rstart = per_tile(tcnt), per_tile(toff), per_tile(rstart)
    xs = _dispatch(lpos_flat, tcnt, toff, rstart, meta[2, :N_EXPERTS], meta[3, :N_EXPERTS], n_used, xr,
                   n_blocks, tm)
    y = _experts(block_e, n_used, meta[4, :n_blocks], xs, wg, wu, wd, layer)
    return _combine_ln(lpos_flat, tcnt, toff, rstart, y, xr, w[0].reshape(n, 1), w[1].reshape(n, 1), g, b, tm)


def _conv_qkv_kernel(xm_ref, cw_ref, cb_ref, wq_ref, wk_ref, wv_ref, q_ref, k_ref, v_ref, xc_ref, *, s):
    xm_b = xm_ref[0]
    xm = xm_b.astype(F32)
    cw = cw_ref[...]
    row = lax.broadcasted_iota(I32, (s, 1), 0)
    half = CONV_K // 2
    acc = cb_ref[...] + xm * cw[half:half + 1, :]
    for sh in range(1, half + 1):
        past = jnp.where(row >= sh, pltpu.roll(xm, sh, axis=0), 0.0)
        acc = acc + past * cw[half - sh:half - sh + 1, :]
        nxt = jnp.where(row < s - sh, pltpu.roll(xm, s - sh, axis=0), 0.0)
        acc = acc + nxt * cw[half + sh:half + sh + 1, :]
    xc = _silu(acc).astype(BF16)
    xc_ref[0] = xc
    q_ref[0] = _dot(xc, wq_ref[0]).astype(BF16)
    k_ref[0] = (_dot_nt(wk_ref[0], xc) * (ML_DH ** -0.5)).astype(BF16)
    v = _dot(xm_b, wv_ref[0])
    ones_lane = lax.broadcasted_iota(I32, (1, ML_DHP), 1) == ML_DH
    v_ref[0] = jnp.where(ones_lane, 1.0, v).astype(BF16)


def _conv_qkv(main3, cw, cb, wq, wk_t, wv):
    b, s, _ = main3.shape
    tok = pl.BlockSpec((1, s, ML_DHP), lambda i, h: (i, 0, h))
    wspec = pl.BlockSpec((1, ML_DHP, ML_DHP), lambda i, h: (h, 0, 0))
    tok_shape = jax.ShapeDtypeStruct((b, s, ML_WP), BF16)
    return pl.pallas_call(
        functools.partial(_conv_qkv_kernel, s=s),
        grid=(b, ML_HEADS),
        in_specs=[tok,
                  pl.BlockSpec((CONV_K, ML_DHP), lambda i, h: (0, h)),
                  pl.BlockSpec((1, ML_DHP), lambda i, h: (0, h)),
                  wspec, wspec, wspec],
        out_specs=[tok, pl.BlockSpec((1, ML_DHP, s), lambda i, h: (i, h, 0)), tok, tok],
        out_shape=[tok_shape, jax.ShapeDtypeStruct((b, ML_WP, s), BF16), tok_shape, tok_shape],
        compiler_params=_cparams("parallel", "parallel"),
        name="conv_qkv",
    )(main3, cw, cb, wq, wk_t, wv)


def _mlstm_kernel(q_ref, kt_ref, v_ref, gc_ref, gr_ref, z_ref, xc_ref, ng_ref, sk_ref,
                  y_ref, hf_ref, hb_ref, cf_ref, cb_ref, m_ref, *, s):
    head0 = pl.program_id(1) * ML_HPS
    nc = s // CHUNK
    sub = lax.broadcasted_iota(I32, (LANES, 1), 0)
    gate = lax.broadcasted_iota(I32, (LANES, LANES), 0)
    ti = lax.broadcasted_iota(I32, (CHUNK, CHUNK), 0)
    tj = lax.broadcasted_iota(I32, (CHUNK, CHUNK), 1)

    for ref in (cf_ref, cb_ref, m_ref):
        ref[...] = jnp.zeros_like(ref)

    def intra(c, j, rev):
        t0 = pl.multiple_of(c * CHUNK, CHUNK)
        hl = slice(j * ML_DHP, (j + 1) * ML_DHP)
        qb = q_ref[0, pl.ds(t0, CHUNK), hl]
        kt = kt_ref[0, hl, pl.ds(t0, CHUNK)]
        vb = v_ref[0, pl.ds(t0, CHUNK), hl]
        gc = gc_ref[0, pl.ds(t0, CHUNK), :]
        gr = gr_ref[:, pl.ds(t0, CHUNK)]
        i_idx = head0 + j + (2 * ML_HEADS if rev else 0)
        f_idx = i_idx + ML_HEADS
        allowed = (tj >= ti) if rev else (tj <= ti)
        sel = (gate == f_idx).astype(BF16)
        b_rep = sum(_dot(part, sel) for part in _split3(gc))
        b_row = jnp.sum(jnp.where(sub == f_idx, gr, 0.0), axis=0, keepdims=True)
        i_row = jnp.sum(jnp.where(sub == i_idx, gr, 0.0), axis=0, keepdims=True)
        b_last = (b_rep[0:1, :] if rev else b_rep[CHUNK - 1:CHUNK, :])[:, 0:1]

        b_wide = jnp.concatenate([b_rep] * (CHUNK // LANES), axis=1)
        d = jnp.where(allowed, b_wide - b_row + i_row, NEG)
        m_in = jnp.max(d, axis=1, keepdims=True)
        sc = _dot(qb, kt) * jnp.exp(d - m_in)
        nd_in = _dot(sc.astype(BF16), vb)
        w_row = b_last - b_row + i_row
        return t0, qb, kt, vb, b_rep, b_last, m_in, nd_in, w_row

    def twice(a):
        return jnp.concatenate([a, a], axis=1)

    def update(parts, j, rev):
        t0, qb, kt, vb, b_rep, b_last, m_in, nd_in, w_row = parts
        h_ref, c_ref = (hb_ref, cb_ref) if rev else (hf_ref, cf_ref)
        hl = slice(j * ML_DHP, (j + 1) * ML_DHP)
        mrow = 2 * j + int(rev)
        m = m_ref[mrow:mrow + 1, 0:1]
        cmat = c_ref[j]
        inter = b_rep + m
        m_t = jnp.maximum(m_in, inter)
        a_in = jnp.exp(m_in - m_t)
        iexp = jnp.exp(inter - m_t)
        nd = twice(a_in) * nd_in + twice(iexp) * _dot(qb, cmat.astype(BF16))
        den = nd[:, ML_DH:ML_DH + 1]
        h_ref[pl.ds(t0, CHUNK), hl] = nd * (1.0 / jnp.maximum(jnp.abs(den), jnp.exp(-m_t[:, 0:1])))

        m_new = jnp.maximum(b_last + m, jnp.max(w_row, axis=1, keepdims=True))
        wexp = jnp.exp(w_row - m_new)
        cexp = jnp.exp(b_last + m - m_new)
        kw = (kt.astype(F32) * wexp).astype(BF16)
        c_ref[j] = cexp * cmat + _dot(kw, vb)
        m_ref[mrow:mrow + 1, :] = jnp.broadcast_to(m_new, (1, LANES))

    def step(i, carry):
        for j in range(ML_HPS):
            parts = [intra(nc - 1 - i if rev else i, j, rev) for rev in (False, True)]
            for p, rev in zip(parts, (False, True)):
                update(p, j, rev)
        return carry

    lax.fori_loop(0, nc, step, 0)

    real = lax.broadcasted_iota(I32, (1, ML_DHP), 1) < ML_DH
    tb = CHUNK

    def fin(c, carry):
        t0 = pl.multiple_of(c * tb, tb)
        for j in range(ML_HPS):
            hl = slice(j * ML_DHP, (j + 1) * ML_DHP)
            hs = jnp.where(real, hf_ref[pl.ds(t0, tb), hl] + hb_ref[pl.ds(t0, tb), hl], 0.0)
            mu = jnp.sum(hs, axis=1, keepdims=True) * (1.0 / ML_DH)
            dev = jnp.where(real, hs - mu, 0.0)
            var = jnp.sum(dev * dev, axis=1, keepdims=True) * (1.0 / ML_DH)
            hn = dev * lax.rsqrt(var + LN_EPS) * ng_ref[:, hl]
            xc = xc_ref[0, pl.ds(t0, tb), hl].astype(F32)
            z = z_ref[0, pl.ds(t0, tb), hl].astype(F32)
            y_ref[0, pl.ds(t0, tb), hl] = ((hn + sk_ref[:, hl] * xc) * _silu(z)).astype(BF16)
        return carry

    lax.fori_loop(0, s // tb, fin, 0)


def _mlstm(q, kt, v, gcol3, grow, main3, xc, ng, sk):
    b, s, _ = q.shape
    width = ML_HPS * ML_DHP
    steps = ML_HEADS // ML_HPS
    tok = pl.BlockSpec((1, s, width), lambda i, h: (i, 0, h))
    vec = pl.BlockSpec((1, width), lambda i, h: (0, h))
    return pl.pallas_call(
        functools.partial(_mlstm_kernel, s=s),
        grid=(b, steps),
        in_specs=[tok, pl.BlockSpec((1, width, s), lambda i, h: (i, h, 0)), tok,
                  pl.BlockSpec((1, s, LANES), lambda i, h: (i, 0, 0)),
                  pl.BlockSpec((LANES, s), lambda i, h: (0, i)),
                  pl.BlockSpec((1, s, width), lambda i, h: (i, 0, steps + h)),
                  tok, vec, vec],
        out_specs=tok,
        out_shape=jax.ShapeDtypeStruct((b, s, ML_WP), BF16),
        scratch_shapes=[pltpu.VMEM((s, width), F32), pltpu.VMEM((s, width), F32),
                        pltpu.VMEM((ML_HPS, ML_DHP, ML_DHP), F32), pltpu.VMEM((ML_HPS, ML_DHP, ML_DHP), F32),
                        pltpu.VMEM((SUBLANES, LANES), F32)],
        compiler_params=_cparams("parallel", "parallel"),
        name="mlstm",
    )(q, kt, v, gcol3, grow, main3, xc, ng, sk)


def _pad_heads(a, axis):
    a = jnp.moveaxis(a, axis, -1)
    lead = a.shape[:-1]
    a = a.reshape(lead + (ML_HEADS, ML_DH))
    a = jnp.pad(a, [(0, 0)] * len(lead) + [(0, 0), (0, ML_DHP - ML_DH)])
    return jnp.moveaxis(a.reshape(lead + (ML_WP,)), -1, axis)


def kernel(x, mem, mem_ln_g, mem_ln_b, w_mem_kv, router_w, router_b, na_w_in, na_rpb, ml_w_in, ml_conv_w,
           ml_conv_b, ml_w_qkv, ml_gate_b, ml_norm_g, ml_skip, w_out, ln_g, ln_b, exp_w_gate, exp_w_up,
           exp_w_down):
    b, s, d = x.shape
    n = b * s
    nm = mem.shape[1]
    row = lambda a: a.reshape(1, -1)

    mem_k, mem_v = _memkv(mem.reshape(b * nm, d), row(mem_ln_g), row(mem_ln_b), w_mem_kv.astype(BF16))
    mem_k3 = mem_k.reshape(b, nm, MEM_W)
    mem_v3 = mem_v.reshape(b, nm, MEM_W)
    rw_pad = jnp.pad(router_w, ((0, 0), (0, LANES - N_EXPERTS)))
    rw_hi = rw_pad.astype(BF16)
    rw = (rw_hi, (rw_pad - rw_hi.astype(F32)).astype(BF16))
    rb = router_b.reshape(N_EXPERTS, 1)

    x2 = x.reshape(n, d)

    h0 = _proj(x2, na_w_in[0].astype(BF16)).reshape(b, s, 3 * NA_W + MEM_W)
    y_na = _na_attention(h0, _na_bias_table(na_rpb[0]))
    wo = w_out[0].astype(BF16)
    xr = _outproj_ln(y_na.reshape(n, NA_W), h0.reshape(n, 3 * NA_W + MEM_W), 3 * NA_W // MEM_W, mem_k3, mem_v3,
                     wo[:NA_W], wo[NA_W:], x2, row(ln_g[0, 0]), row(ln_b[0, 0]))
    x2 = _moe_ln(xr, rw, rb, exp_w_gate, exp_w_up, exp_w_down, 0, row(ln_g[0, 1]), row(ln_b[0, 1]))

    w1 = ml_w_in[0]
    w_main = jnp.concatenate([_pad_heads(w1[:, :ML_W], 1), _pad_heads(w1[:, ML_W:2 * ML_W], 1),
                              w1[:, 2 * ML_W + 4 * ML_HEADS:]], axis=1).astype(BF16)
    w_g = jnp.pad(w1[:, 2 * ML_W:2 * ML_W + 4 * ML_HEADS], ((0, 0), (0, LANES - 4 * ML_HEADS))).astype(BF16)
    gb = jnp.pad(ml_gate_b[0].reshape(4 * ML_HEADS), (0, LANES - 4 * ML_HEADS))
    main, acol, arow = _proj_gates(x2, w_main, w_g, w_g.T, gb.reshape(1, LANES), gb.reshape(LANES, 1))
    main3 = main.reshape(b, s, 2 * ML_WP + MEM_W)
    wqkv = jnp.pad(ml_w_qkv[0], ((0, 0), (0, 0), (0, ML_DHP - ML_DH), (0, ML_DHP - ML_DH))).astype(BF16)
    q, k, v, xc = _conv_qkv(main3, _pad_heads(ml_conv_w[0], 1), _pad_heads(row(ml_conv_b[0]), 1),
                            wqkv[0], jnp.swapaxes(wqkv[1], 1, 2), wqkv[2])
    y_ml = _mlstm(q, k, v, acol.reshape(b, s, LANES), arow, main3, xc,
                  _pad_heads(row(ml_norm_g[0]), 1), _pad_heads(row(ml_skip[0]), 1))
    wo = w_out[1]
    xr = _outproj_ln(y_ml.reshape(n, ML_WP), main, 2 * ML_WP // MEM_W, mem_k3, mem_v3,
                     _pad_heads(wo[:ML_W], 0).astype(BF16), wo[ML_W:].astype(BF16), x2,
                     row(ln_g[1, 0]), row(ln_b[1, 0]))
    x2 = _moe_ln(xr, rw, rb, exp_w_gate, exp_w_up, exp_w_down, 1, row(ln_g[1, 1]), row(ln_b[1, 1]))
    return x2.reshape(b, s, d)
```

```python
import functools

import numpy as np
import jax
import jax.numpy as jnp
from jax import lax
from jax.experimental import pallas as pl
from jax.experimental.pallas import tpu as pltpu

F32 = jnp.float32
BF16 = jnp.bfloat16
I32 = jnp.int32

D_MODEL = 1024
DEPTH = 2
GRID_W = 64
MEM_HEADS = 4
MEM_DH = 64
MEM_W = MEM_HEADS * MEM_DH
NA_HEADS = 12
NA_DH = 64
NA_W = NA_HEADS * NA_DH
WIN_H = 8
WIN_W = 16
ML_HEADS = 4
ML_DH = 192
ML_DHP = 256
ML_W = ML_HEADS * ML_DH
ML_WP = ML_HEADS * ML_DHP
CONV_K = 5
CHUNK = 256
N_EXPERTS = 16
N_GROUPS = 4
EXPERTS_PER_GROUP = N_EXPERTS // N_GROUPS
D_EXPERT = 512
ALPHA = (2 * DEPTH) ** 0.25
LN_EPS = 1e-5
NEG = -1e30

LANES = 128
SUBLANES = 8
ROW_CHUNKS = D_MODEL // LANES
MOE_BM = 512
MOE_TILE = 512
ML_HPS = 2
NA_ROWS_PER_STEP = 16
ROW_TILE = 1024
PLACE_UNROLL = 8
VMEM_LIMIT = 48 * 1024 * 1024


def _cparams(*sem):
    return pltpu.CompilerParams(dimension_semantics=sem, vmem_limit_bytes=VMEM_LIMIT)


def _dot(a, b):
    return jnp.dot(a, b, preferred_element_type=F32)


def _dot_nt(a, b, precision=None):
    return lax.dot_general(a, b, (((1,), (1,)), ((), ())), precision=precision,
                           preferred_element_type=F32)


def _ln(z, g, b):
    mu = jnp.mean(z, axis=-1, keepdims=True)
    zc = z - mu
    var = jnp.mean(zc * zc, axis=-1, keepdims=True)
    return zc * lax.rsqrt(var + LN_EPS) * g + b


def _silu(x):
    return x * jax.nn.sigmoid(x)


def _read_rows(ref, n):
    return jnp.concatenate([ref[pl.ds(j, n, stride=ROW_CHUNKS), :] for j in range(ROW_CHUNKS)], axis=1)


def _write_rows(ref, val, n):
    for j in range(ROW_CHUNKS):
        ref[pl.ds(j, n, stride=ROW_CHUNKS), :] = val[:, j * LANES:(j + 1) * LANES]


def _memkv_kernel(m_ref, g_ref, b_ref, w_ref, k_ref, v_ref):
    z = _ln(m_ref[...], g_ref[...], b_ref[...])
    kv = _dot(z.astype(BF16), w_ref[...])
    k_ref[...] = kv[:, :MEM_W].astype(BF16)
    v_ref[...] = kv[:, MEM_W:].astype(BF16)


def _memkv(mem2, g, b, w):
    n = mem2.shape[0]
    tm = min(ROW_TILE, n)
    return pl.pallas_call(
        _memkv_kernel,
        grid=(n // tm,),
        in_specs=[pl.BlockSpec((tm, D_MODEL), lambda i: (i, 0)),
                  pl.BlockSpec((1, D_MODEL), lambda i: (0, 0)),
                  pl.BlockSpec((1, D_MODEL), lambda i: (0, 0)),
                  pl.BlockSpec((D_MODEL, 2 * MEM_W), lambda i: (0, 0))],
        out_specs=[pl.BlockSpec((tm, MEM_W), lambda i: (i, 0)),
                   pl.BlockSpec((tm, MEM_W), lambda i: (i, 0))],
        out_shape=[jax.ShapeDtypeStruct((n, MEM_W), BF16)] * 2,
        compiler_params=_cparams("parallel"),
        name="memkv",
    )(mem2, g, b, w)


def _proj_kernel(x_ref, w_ref, o_ref):
    o_ref[...] = _dot(x_ref[...].astype(BF16), w_ref[...]).astype(o_ref.dtype)


def _proj(x2, w, tm=ROW_TILE):
    n, k = x2.shape
    nout = w.shape[1]
    return pl.pallas_call(
        _proj_kernel,
        grid=(n // tm,),
        in_specs=[pl.BlockSpec((tm, k), lambda i: (i, 0)),
                  pl.BlockSpec((k, nout), lambda i: (0, 0))],
        out_specs=pl.BlockSpec((tm, nout), lambda i: (i, 0)),
        out_shape=jax.ShapeDtypeStruct((n, nout), BF16),
        compiler_params=_cparams("parallel"),
        name="in_proj",
    )(x2, w)


def _split3(x):
    hi = x.astype(BF16)
    r1 = x - hi.astype(F32)
    mid = r1.astype(BF16)
    lo = (r1 - mid.astype(F32)).astype(BF16)
    return hi, mid, lo


def _proj_gates_kernel(x_ref, w_ref, wg_ref, wgt_ref, gbc_ref, gbr_ref, o_ref, g_ref, gt_ref, *, tm):
    xb = x_ref[...].astype(BF16)
    o_ref[...] = _dot(xb, w_ref[...]).astype(BF16)
    gcol = _dot(xb, wg_ref[...]) + gbc_ref[...]
    grow = _dot_nt(wgt_ref[...], xb) + gbr_ref[...]
    lane = lax.broadcasted_iota(I32, (1, LANES), 1)
    sub = lax.broadcasted_iota(I32, (LANES, 1), 0)
    ti = lax.broadcasted_iota(I32, (CHUNK, CHUNK), 0)
    tj = lax.broadcasted_iota(I32, (CHUNK, CHUNK), 1)
    lower = (tj <= ti).astype(BF16)
    upper = (ti <= tj).astype(BF16)

    def pick(idx, pre, suf, raw):
        fwd = jnp.logical_and(idx >= ML_HEADS, idx < 2 * ML_HEADS)
        bwd = jnp.logical_and(idx >= 3 * ML_HEADS, idx < 4 * ML_HEADS)
        return jnp.where(fwd, pre, jnp.where(bwd, suf, raw))

    for c in range(tm // CHUNK):
        tc = slice(c * CHUNK, (c + 1) * CHUNK)
        g = gcol[tc, :]
        ls = jax.nn.log_sigmoid(g)
        pre = sum(_dot(lower, part) for part in _split3(ls))
        suf = jnp.sum(ls, axis=0, keepdims=True) - pre + ls
        g_ref[tc, :] = pick(lane, pre, suf, g)
        g = grow[:, tc]
        ls = jax.nn.log_sigmoid(g)
        pre = sum(_dot(part, upper) for part in _split3(ls))
        suf = jnp.sum(ls, axis=1, keepdims=True) - pre + ls
        gt_ref[:, tc] = pick(sub, pre, suf, g)


def _proj_gates(x2, w, wg, wgt, gbc, gbr, tm=ROW_TILE):
    n, k = x2.shape
    nout = w.shape[1]
    return pl.pallas_call(
        functools.partial(_proj_gates_kernel, tm=tm),
        grid=(n // tm,),
        in_specs=[pl.BlockSpec((tm, k), lambda i: (i, 0)),
                  pl.BlockSpec((k, nout), lambda i: (0, 0)),
                  pl.BlockSpec((k, LANES), lambda i: (0, 0)),
                  pl.BlockSpec((LANES, k), lambda i: (0, 0)),
                  pl.BlockSpec((1, LANES), lambda i: (0, 0)),
                  pl.BlockSpec((LANES, 1), lambda i: (0, 0))],
        out_specs=[pl.BlockSpec((tm, nout), lambda i: (i, 0)),
                   pl.BlockSpec((tm, LANES), lambda i: (i, 0)),
                   pl.BlockSpec((LANES, tm), lambda i: (0, i))],
        out_shape=[jax.ShapeDtypeStruct((n, nout), BF16),
                   jax.ShapeDtypeStruct((n, LANES), F32),
                   jax.ShapeDtypeStruct((LANES, n), F32)],
        compiler_params=_cparams("parallel"),
        name="in_proj_gates",
    )(x2, w, wg, wgt, gbc, gbr)


def _na_kernel(q_ref, k_ref, v_ref, tbl_ref, o_ref, *, rows):
    lane = lax.broadcasted_iota(I32, (1, LANES), 1)
    first = lane < NA_DH
    nkeys = WIN_H * GRID_W

    def rows_step(i, carry):
        rr = [i * NA_ROWS_PER_STEP + u for u in range(NA_ROWS_PER_STEP)]
        rss = [jnp.clip(r - WIN_H // 2, 0, rows - WIN_H) for r in rr]
        scores = []
        for r, rs in zip(rr, rss):
            q = q_ref[0, pl.ds(pl.multiple_of(r * GRID_W, GRID_W), GRID_W), :]
            q = q * jnp.asarray(NA_DH ** -0.5, BF16)
            q2 = jnp.concatenate([jnp.where(first, q, jnp.zeros_like(q)),
                                  jnp.where(first, jnp.zeros_like(q), q)], axis=0)
            k = k_ref[0, pl.ds(pl.multiple_of(rs * GRID_W, GRID_W), nkeys), :]
            dr0 = rs - r + WIN_H - 1
            bias = jnp.concatenate(
                [jnp.concatenate([tbl_ref[0, half, dr0 + 2 * m] for m in range(WIN_H // 2)], axis=1)
                 for half in range(2)], axis=0)
            scores.append(_dot_nt(q2, k) + bias)
        probs = []
        for s in scores:
            p = jnp.exp(s - jnp.max(s, axis=-1, keepdims=True))
            probs.append((p.astype(BF16), jnp.sum(p, axis=-1, keepdims=True)))
        for r, rs, (p, l) in zip(rr, rss, probs):
            v = v_ref[0, pl.ds(pl.multiple_of(rs * GRID_W, GRID_W), nkeys), :]
            o = _dot(p, v) / l
            o = jnp.where(first, o[:GRID_W], o[GRID_W:])
            o_ref[0, pl.ds(pl.multiple_of(r * GRID_W, GRID_W), GRID_W), :] = o.astype(o_ref.dtype)
        return carry

    lax.fori_loop(0, rows // NA_ROWS_PER_STEP, rows_step, 0)


def _na_bias_table(rpb):
    qc = np.arange(GRID_W)[:, None]
    kc = np.arange(GRID_W)[None, :]
    cs = np.clip(qc - WIN_W // 2, 0, GRID_W - WIN_W)
    col_in = (kc >= cs) & (kc < cs + WIN_W)
    side = GRID_W - WIN_W
    wide = jnp.pad(rpb, ((0, 0), (0, 0), (side, side)))
    t = jnp.stack([wide[:, :, GRID_W - 1 - q:2 * GRID_W - 1 - q] for q in range(GRID_W)], axis=2)
    t = jnp.where(col_in, t, NEG).astype(F32)
    t2 = jnp.concatenate([t[:, :-1], t[:, 1:]], axis=-1)
    return t2.reshape(NA_HEADS // 2, 2, 2 * WIN_H - 2, GRID_W, 2 * GRID_W)


def _na_attention(h3, tbl):
    b, s, _ = h3.shape
    rows = s // GRID_W
    npair = NA_HEADS // 2
    return pl.pallas_call(
        functools.partial(_na_kernel, rows=rows),
        grid=(b, npair),
        in_specs=[pl.BlockSpec((1, s, LANES), lambda i, p: (i, 0, p)),
                  pl.BlockSpec((1, s, LANES), lambda i, p: (i, 0, npair + p)),
                  pl.BlockSpec((1, s, LANES), lambda i, p: (i, 0, 2 * npair + p)),
                  pl.BlockSpec((1, 2, 2 * WIN_H - 2, GRID_W, 2 * GRID_W), lambda i, p: (p, 0, 0, 0, 0))],
        out_specs=pl.BlockSpec((1, s, LANES), lambda i, p: (i, 0, p)),
        out_shape=jax.ShapeDtypeStruct((b, s, NA_W), BF16),
        compiler_params=_cparams("parallel", "parallel"),
        name="na_attention",
    )(h3, h3, h3, tbl)


def _outproj_ln_kernel(ya_ref, qm_ref, mk_ref, mv_ref, wa_ref, wm_ref, x_ref, g_ref, b_ref, or_ref, *, tm):
    lane = lax.broadcasted_iota(I32, (1, LANES), 1)
    first = lane < MEM_DH
    q = qm_ref[...] * jnp.asarray(MEM_DH ** -0.5, BF16)
    cols = [slice(p * LANES, (p + 1) * LANES) for p in range(MEM_HEADS // 2)]
    scores = []
    for c in cols:
        qp = q[:, c]
        q2 = jnp.concatenate([jnp.where(first, qp, jnp.zeros_like(qp)),
                              jnp.where(first, jnp.zeros_like(qp), qp)], axis=0)
        scores.append(_dot_nt(q2, mk_ref[0, :, c]))
    probs = []
    for s in scores:
        p = jnp.exp(s - jnp.max(s, axis=-1, keepdims=True))
        probs.append((p.astype(BF16), jnp.sum(p, axis=-1, keepdims=True)))
    outs = []
    for c, (p, l) in zip(cols, probs):
        o = _dot(p, mv_ref[0, :, c]) / l
        outs.append(jnp.where(first, o[:tm], o[tm:]))
    ym = jnp.concatenate(outs, axis=1).astype(BF16)
    acc = _dot(ya_ref[...], wa_ref[...]) + _dot(ym, wm_ref[...])
    _write_rows(or_ref, _ln(ALPHA * x_ref[...] + acc, g_ref[...], b_ref[...]), tm)


def _outproj_ln(ya, h2, qm_block, mem_k3, mem_v3, wa, wm, x2, g, b, tm=ROW_TILE):
    n = x2.shape[0]
    ka = ya.shape[1]
    nb, nm, _ = mem_k3.shape
    per_batch = n // nb // tm
    full = lambda shape: pl.BlockSpec(shape, lambda i: (0,) * len(shape))
    return pl.pallas_call(
        functools.partial(_outproj_ln_kernel, tm=tm),
        grid=(n // tm,),
        in_specs=[pl.BlockSpec((tm, ka), lambda i: (i, 0)),
                  pl.BlockSpec((tm, MEM_W), lambda i: (i, qm_block)),
                  pl.BlockSpec((1, nm, MEM_W), lambda i: (i // per_batch, 0, 0)),
                  pl.BlockSpec((1, nm, MEM_W), lambda i: (i // per_batch, 0, 0)),
                  full((ka, D_MODEL)), full((MEM_W, D_MODEL)),
                  pl.BlockSpec((tm, D_MODEL), lambda i: (i, 0)),
                  full((1, D_MODEL)), full((1, D_MODEL))],
        out_specs=pl.BlockSpec((tm * ROW_CHUNKS, LANES), lambda i: (i, 0)),
        out_shape=jax.ShapeDtypeStruct((n * ROW_CHUNKS, LANES), F32),
        compiler_params=_cparams("parallel"),
        name="outproj_ln",
    )(ya, h2, mem_k3, mem_v3, wa, wm, x2, g, b)


def _router_kernel(x_ref, rwh_ref, rwl_ref, rb_ref, lpos_ref, w_ref, cnt_ref, tcnt_ref, toff_ref, tbef_ref, *, tm):
    @pl.when(pl.program_id(0) == 0)
    def _():
        cnt_ref[...] = jnp.zeros_like(cnt_ref)

    x = _read_rows(x_ref, tm)
    xh = x.astype(BF16)
    xl = (x - xh.astype(F32)).astype(BF16)
    logits_t = _dot(xh, rwh_ref[...]) + (_dot(xh, rwl_ref[...]) + _dot(xl, rwh_ref[...]))
    logits = logits_t.T[:N_EXPERTS]
    scores = jax.nn.sigmoid(logits)
    biased = scores + rb_ref[...]
    bv = [biased[e:e + 1, :] for e in range(N_EXPERTS)]
    sv = [scores[e:e + 1, :] for e in range(N_EXPERTS)]

    grp = []
    for g in range(N_GROUPS):
        m = bv[g * EXPERTS_PER_GROUP:(g + 1) * EXPERTS_PER_GROUP]
        best = None
        for a in range(EXPERTS_PER_GROUP):
            for c in range(a + 1, EXPERTS_PER_GROUP):
                pair = m[a] + m[c]
                best = pair if best is None else jnp.maximum(best, pair)
        grp.append(best)
    gsel = jnp.zeros((1, tm), I32)
    gbest = grp[0]
    for g in range(1, N_GROUPS):
        better = grp[g] > gbest
        gsel = jnp.where(better, g, gsel)
        gbest = jnp.where(better, grp[g], gbest)

    def pick(vals, j):
        out = vals[j]
        for g in range(1, N_GROUPS):
            out = jnp.where(gsel == g, vals[g * EXPERTS_PER_GROUP + j], out)
        return out

    cb = [pick(bv, j) for j in range(EXPERTS_PER_GROUP)]
    cs = [pick(sv, j) for j in range(EXPERTS_PER_GROUP)]
    i1 = jnp.zeros((1, tm), I32)
    m1 = cb[0]
    s1 = cs[0]
    for j in range(1, EXPERTS_PER_GROUP):
        gt = cb[j] > m1
        i1 = jnp.where(gt, j, i1)
        m1 = jnp.where(gt, cb[j], m1)
        s1 = jnp.where(gt, cs[j], s1)
    i2 = jnp.zeros((1, tm), I32)
    m2 = jnp.full((1, tm), -jnp.inf, F32)
    s2 = jnp.zeros((1, tm), F32)
    for j in range(EXPERTS_PER_GROUP):
        ok = jnp.logical_and(i1 != j, cb[j] > m2)
        i2 = jnp.where(ok, j, i2)
        m2 = jnp.where(ok, cb[j], m2)
        s2 = jnp.where(ok, cs[j], s2)
    e1 = gsel * EXPERTS_PER_GROUP + i1
    e2 = gsel * EXPERTS_PER_GROUP + i2
    tot = s1 + s2
    w_ref[...] = jnp.concatenate([s1 / tot, s2 / tot], axis=0)

    i = pl.program_id(0)
    eio = lax.broadcasted_iota(I32, (N_EXPERTS, tm), 0)
    oh1 = eio == e1
    oh2 = eio == e2
    ohs = jnp.logical_or(oh1, oh2).astype(F32)
    before = (lax.broadcasted_iota(I32, (tm, tm), 0) < lax.broadcasted_iota(I32, (tm, tm), 1))
    pre = _dot(ohs.astype(BF16), before.astype(BF16))
    tile_cnt = jnp.sum(ohs, axis=1, keepdims=True)
    offs = []
    acc = jnp.zeros((1, 1), F32)
    for e in range(N_EXPERTS):
        offs.append(acc)
        acc = acc + tile_cnt[e:e + 1, :]
    tile_off = jnp.concatenate(offs, axis=0)
    pos = tile_off + pre
    p1 = jnp.sum(jnp.where(oh1, pos, 0.0), axis=0, keepdims=True)
    p2 = jnp.sum(jnp.where(oh2, pos, 0.0), axis=0, keepdims=True)
    lpos_ref[...] = jnp.concatenate([p1, p2], axis=0).astype(I32) * ROW_CHUNKS

    @pl.when(i == 0)
    def _():
        for ref in (tcnt_ref, toff_ref, tbef_ref):
            ref[...] = jnp.zeros_like(ref)

    here = lax.broadcasted_iota(I32, (1, LANES), 1) == i
    tcnt_ref[...] = jnp.where(here, tile_cnt, tcnt_ref[...])
    toff_ref[...] = jnp.where(here, tile_off, toff_ref[...])
    tbef_ref[...] = jnp.where(here, cnt_ref[:, 0:1], tbef_ref[...])
    cnt_ref[...] += tile_cnt


def _router(xr, rw, rb, tm):
    n = xr.shape[0] // ROW_CHUNKS
    assert n // tm <= LANES
    table = pl.BlockSpec((N_EXPERTS, LANES), lambda i: (0, 0))
    return pl.pallas_call(
        functools.partial(_router_kernel, tm=tm),
        grid=(n // tm,),
        in_specs=[pl.BlockSpec((tm * ROW_CHUNKS, LANES), lambda i: (i, 0)),
                  pl.BlockSpec((D_MODEL, LANES), lambda i: (0, 0)),
                  pl.BlockSpec((D_MODEL, LANES), lambda i: (0, 0)),
                  pl.BlockSpec((N_EXPERTS, 1), lambda i: (0, 0))],
        out_specs=[pl.BlockSpec((2, tm), lambda i: (0, i)),
                   pl.BlockSpec((2, tm), lambda i: (0, i)),
                   table, table, table, table],
        out_shape=[jax.ShapeDtypeStruct((2, n), I32),
                   jax.ShapeDtypeStruct((2, n), F32)]
                  + [jax.ShapeDtypeStruct((N_EXPERTS, LANES), F32)] * 4,
        compiler_params=_cparams("arbitrary"),
        name="router",
    )(xr, rw[0], rw[1], rb)


def _plan_kernel(cnt_ref, tbef_ref, meta_ref, rstart_ref, *, nbl):
    shift = MOE_BM.bit_length() - 1
    cnt = cnt_ref[...].astype(I32)
    padded = ((cnt + (MOE_BM - 1)) >> shift) << shift
    starts = []
    acc = jnp.zeros((1, LANES), I32)
    for e in range(N_EXPERTS):
        starts.append(acc)
        acc = acc + padded[e:e + 1, :]
    pad_start = jnp.concatenate(starts, axis=0)
    pad_end = pad_start + padded
    rstart_ref[...] = pad_start + tbef_ref[...].astype(I32)
    blk0 = lax.broadcasted_iota(I32, (N_EXPERTS, nbl), 1) * MOE_BM
    block_e = jnp.sum((pad_end[:, 0:1] <= blk0).astype(I32), axis=0, keepdims=True)
    block_e = jnp.minimum(block_e, N_EXPERTS - 1)
    n_used = jnp.broadcast_to(acc[:, 0:1] >> shift, (1, nbl))
    diag = lax.broadcasted_iota(I32, (N_EXPERTS, nbl), 0) == lax.broadcasted_iota(I32, (N_EXPERTS, nbl), 1)
    fill_lo = jnp.sum(jnp.where(diag, (pad_start + cnt)[:, 0:1], 0), axis=0, keepdims=True)
    fill_hi = jnp.sum(jnp.where(diag, pad_end[:, 0:1], 0), axis=0, keepdims=True)
    meta_ref[...] = jnp.concatenate([block_e, n_used, fill_lo, fill_hi, jnp.zeros((SUBLANES - 4, nbl), I32)],
                                    axis=0)


def _plan(cnt, tbef, n_blocks):
    nbl = -(-n_blocks // LANES) * LANES
    table = pl.BlockSpec((N_EXPERTS, LANES), lambda i: (0, 0))
    return pl.pallas_call(
        functools.partial(_plan_kernel, nbl=nbl),
        grid=(1,),
        in_specs=[table, table],
        out_specs=[pl.BlockSpec((SUBLANES, nbl), lambda i: (0, 0)), table],
        out_shape=[jax.ShapeDtypeStruct((SUBLANES, nbl), I32),
                   jax.ShapeDtypeStruct((N_EXPERTS, LANES), I32)],
        compiler_params=_cparams("arbitrary"),
        name="moe_plan",
    )(cnt, tbef)


def _rows(ref, row, nrows):
    return ref.at[pl.ds(pl.multiple_of(row * ROW_CHUNKS, ROW_CHUNKS), nrows * ROW_CHUNKS), :]


def _rows_wait(src_hbm, buf, sem):
    pltpu.make_async_copy(src_hbm.at[pl.ds(0, buf.shape[0]), :], buf, sem).wait()


def _copy_pieces(src, src_row, dst, dst_row, count, max_rows, sem, wait=False):
    bit = max_rows.bit_length() - 1
    while bit >= 0:
        size = 1 << bit
        done = (count >> (bit + 1)) << (bit + 1)

        @pl.when(((count >> bit) & 1) == 1)
        def _():
            cp = pltpu.make_async_copy(_rows(src, src_row + done, size), _rows(dst, dst_row + done, size), sem)
            cp.start()
            if wait:
                cp.wait()

        bit -= 1


def _tile_runs(tcnt_ref, toff_ref, rstart_ref, tile, buf, hbm, sem, *, to_hbm, tm):
    def per_expert(e, carry):
        k = tile * N_EXPERTS + e
        if to_hbm:
            _copy_pieces(buf, toff_ref[k], hbm, rstart_ref[k], tcnt_ref[k], tm, sem)
        else:
            _copy_pieces(hbm, rstart_ref[k], buf, toff_ref[k], tcnt_ref[k], tm, sem)
        return carry

    lax.fori_loop(0, N_EXPERTS, per_expert, 0)


def _dispatch_kernel(lpos_ref, tcnt_ref, toff_ref, rstart_ref, flo_ref, fhi_ref, nu_ref, x_ref, xs_hbm,
                     s0, s1, zbuf, sem, zsem, *, n, tm, n_blocks):
    i = pl.program_id(0)
    nt = pl.num_programs(0)
    bufs = (s0, s1)
    unroll = PLACE_UNROLL

    for slot in range(2):
        @pl.when(i % 2 == slot)
        def _():
            buf = bufs[slot]

            @pl.when(i >= 2)
            def _():
                _rows_wait(xs_hbm, buf, sem.at[slot])

            def place(c, carry):
                tok = i * tm + c * unroll
                src = pl.multiple_of(c * (unroll * ROW_CHUNKS), unroll * ROW_CHUNKS)
                for u in range(unroll):
                    v = x_ref[pl.ds(src + u * ROW_CHUNKS, ROW_CHUNKS), :]
                    for k in range(2):
                        p = lpos_ref[k * n + tok + u]
                        buf[pl.ds(pl.multiple_of(p, ROW_CHUNKS), ROW_CHUNKS), :] = v
                return carry

            lax.fori_loop(0, tm // unroll, place, 0)
            _tile_runs(tcnt_ref, toff_ref, rstart_ref, i, buf, xs_hbm, sem.at[slot], to_hbm=True, tm=tm)

    @pl.when(i == nt - 1)
    def _():
        for slot in range(2):
            @pl.when(nt > slot)
            def _():
                _rows_wait(xs_hbm, bufs[slot], sem.at[slot])

        zbuf[...] = jnp.zeros_like(zbuf)
        for e in range(N_EXPERTS):
            _copy_pieces(zbuf, 0, xs_hbm, flo_ref[e], fhi_ref[e] - flo_ref[e], MOE_BM // 2, zsem, wait=True)

        def zero_block(j, carry):
            cp = pltpu.make_async_copy(zbuf, _rows(xs_hbm, j * MOE_BM, MOE_BM), zsem)
            cp.start()
            cp.wait()
            return carry

        lax.fori_loop(nu_ref[0], n_blocks, zero_block, 0)


def _dispatch(lpos_flat, tcnt, toff, rstart, fill_lo, fill_hi, n_used, xr, n_blocks, tm):
    n = xr.shape[0] // ROW_CHUNKS
    return pl.pallas_call(
        functools.partial(_dispatch_kernel, n=n, tm=tm, n_blocks=n_blocks),
        grid_spec=pltpu.PrefetchScalarGridSpec(
            num_scalar_prefetch=7,
            grid=(n // tm,),
            in_specs=[pl.BlockSpec((tm * ROW_CHUNKS, LANES), lambda i, *_: (i, 0))],
            out_specs=pl.BlockSpec(memory_space=pl.ANY),
            scratch_shapes=[pltpu.VMEM((2 * tm * ROW_CHUNKS, LANES), F32),
                            pltpu.VMEM((2 * tm * ROW_CHUNKS, LANES), F32),
                            pltpu.VMEM((MOE_BM * ROW_CHUNKS, LANES), F32),
                            pltpu.SemaphoreType.DMA((2,)),
                            pltpu.SemaphoreType.DMA(())]),
        out_shape=jax.ShapeDtypeStruct((n_blocks * MOE_BM * ROW_CHUNKS, LANES), F32),
        compiler_params=_cparams("arbitrary"),
        name="moe_dispatch",
    )(lpos_flat, tcnt, toff, rstart, fill_lo, fill_hi, n_used, xr)


def _experts_kernel(be_ref, nu_ref, xs_ref, wg_ref, wu_ref, wd_ref, y_ref, wgb, wub, wdb):
    j = pl.program_id(0)
    used = j < nu_ref[0]

    @pl.when(jnp.logical_and(used, jnp.logical_or(j == 0, be_ref[j] != be_ref[jnp.maximum(j - 1, 0)])))
    def _():
        wgb[...] = wg_ref[0, 0].astype(BF16)
        wub[...] = wu_ref[0, 0].astype(BF16)
        wdb[...] = wd_ref[0, 0].astype(BF16)

    @pl.when(used)
    def _():
        x = _read_rows(xs_ref, MOE_BM).astype(BF16)
        h = _silu(_dot(x, wgb[...])) * _dot(x, wub[...])
        _write_rows(y_ref, _dot(h.astype(BF16), wdb[...]), MOE_BM)


def _experts(block_e, n_used, xs, wg, wu, wd, layer):
    n_blocks = block_e.shape[0]

    def last_used(j, nu):
        return jnp.minimum(j, nu[0] - 1)

    def wblk(j, be, nu):
        return (layer, be[last_used(j, nu)], 0, 0)

    return pl.pallas_call(
        _experts_kernel,
        grid_spec=pltpu.PrefetchScalarGridSpec(
            num_scalar_prefetch=2,
            grid=(n_blocks,),
            in_specs=[pl.BlockSpec((MOE_BM * ROW_CHUNKS, LANES), lambda j, be, nu: (last_used(j, nu), 0)),
                      pl.BlockSpec((1, 1, D_MODEL, D_EXPERT), wblk),
                      pl.BlockSpec((1, 1, D_MODEL, D_EXPERT), wblk),
                      pl.BlockSpec((1, 1, D_EXPERT, D_MODEL), wblk)],
            out_specs=pl.BlockSpec((MOE_BM * ROW_CHUNKS, LANES), lambda j, be, nu: (last_used(j, nu), 0)),
            scratch_shapes=[pltpu.VMEM((D_MODEL, D_EXPERT), BF16), pltpu.VMEM((D_MODEL, D_EXPERT), BF16),
                            pltpu.VMEM((D_EXPERT, D_MODEL), BF16)]),
        out_shape=jax.ShapeDtypeStruct(xs.shape, F32),
        input_output_aliases={2: 0},
        compiler_params=_cparams("arbitrary"),
        name="moe_experts",
    )(block_e, n_used, xs, wg, wu, wd)


def _combine_ln_kernel(lpos_ref, tcnt_ref, toff_ref, rstart_ref, y_hbm, x_ref, w1_ref, w2_ref, g_ref, b_ref, o_ref,
                       r0, r1, u1, u2, sem, *, n, tm):
    i = pl.program_id(0)
    nt = pl.num_programs(0)
    bufs = (r0, r1)
    unroll = PLACE_UNROLL

    def fetch(tile, slot):
        _tile_runs(tcnt_ref, toff_ref, rstart_ref, tile, bufs[slot], y_hbm, sem.at[slot], to_hbm=False, tm=tm)

    @pl.when(i == 0)
    def _():
        fetch(0, 0)

    for slot in range(2):
        @pl.when(i % 2 == slot)
        def _():
            @pl.when(i + 1 < nt)
            def _():
                fetch(i + 1, 1 - slot)

            buf = bufs[slot]
            _rows_wait(y_hbm, buf, sem.at[slot])

            def place(c, carry):
                tok = i * tm + c * unroll
                dst0 = pl.multiple_of(c * (unroll * ROW_CHUNKS), unroll * ROW_CHUNKS)
                for u in range(unroll):
                    dst = pl.ds(dst0 + u * ROW_CHUNKS, ROW_CHUNKS)
                    for k, out in enumerate((u1, u2)):
                        p = lpos_ref[k * n + tok + u]
                        out[dst, :] = buf[pl.ds(pl.multiple_of(p, ROW_CHUNKS), ROW_CHUNKS), :]
                return carry

            lax.fori_loop(0, tm // unroll, place, 0)
            moe = w1_ref[...] * _read_rows(u1, tm) + w2_ref[...] * _read_rows(u2, tm)
            o_ref[...] = _ln(ALPHA * _read_rows(x_ref, tm) + moe, g_ref[...], b_ref[...])


def _combine_ln(lpos_flat, tcnt, toff, rstart, y, xr, w1, w2, g, b, tm):
    n = xr.shape[0] // ROW_CHUNKS
    return pl.pallas_call(
        functools.partial(_combine_ln_kernel, n=n, tm=tm),
        grid_spec=pltpu.PrefetchScalarGridSpec(
            num_scalar_prefetch=4,
            grid=(n // tm,),
            in_specs=[pl.BlockSpec(memory_space=pl.ANY),
                      pl.BlockSpec((tm * ROW_CHUNKS, LANES), lambda i, *_: (i, 0)),
                      pl.BlockSpec((tm, 1), lambda i, *_: (i, 0)),
                      pl.BlockSpec((tm, 1), lambda i, *_: (i, 0)),
                      pl.BlockSpec((1, D_MODEL), lambda i, *_: (0, 0)),
                      pl.BlockSpec((1, D_MODEL), lambda i, *_: (0, 0))],
            out_specs=pl.BlockSpec((tm, D_MODEL), lambda i, *_: (i, 0)),
            scratch_shapes=[pltpu.VMEM((2 * tm * ROW_CHUNKS, LANES), F32)] * 2
                           + [pltpu.VMEM((tm * ROW_CHUNKS, LANES), F32)] * 2
                           + [pltpu.SemaphoreType.DMA((2,))]),
        out_shape=jax.ShapeDtypeStruct((n, D_MODEL), F32),
        compiler_params=_cparams("arbitrary"),
        name="moe_combine_ln",
    )(lpos_flat, tcnt, toff, rstart, y, xr, w1, w2, g, b)


def _moe_ln(xr, rw, rb, wg, wu, wd, layer, g, b):
    n = xr.shape[0] // ROW_CHUNKS
    n_blocks = (2 * n) // MOE_BM + N_EXPERTS
    tm = MOE_TILE
    nt = n // tm
    lpos, w, cnt, tcnt, toff, tbef = _router(xr, rw, rb, tm)
    meta, rstart = _plan(cnt, tbef, n_blocks)
    block_e = meta[0, :n_blocks]
    n_used = meta[1, :1]

    def per_tile(table):
        return table[:, :nt].T.reshape(nt * N_EXPERTS).astype(I32)

    lpos_flat = lpos.reshape(2 * n)
    tcnt, toff, rstart = per_tile(tcnt), per_tile(toff), per_tile(rstart)
    xs = _dispatch(lpos_flat, tcnt, toff, rstart, meta[2, :N_EXPERTS], meta[3, :N_EXPERTS], n_used, xr,
                   n_blocks, tm)
    y = _experts(block_e, n_used, xs, wg, wu, wd, layer)
    return _combine_ln(lpos_flat, tcnt, toff, rstart, y, xr, w[0].reshape(n, 1), w[1].reshape(n, 1), g, b, tm)


def _conv_qkv_kernel(xm_ref, cw_ref, cb_ref, wq_ref, wk_ref, wv_ref, q_ref, k_ref, v_ref, xc_ref, *, s):
    xm_b = xm_ref[0]
    xm = xm_b.astype(F32)
    cw = cw_ref[...]
    row = lax.broadcasted_iota(I32, (s, 1), 0)
    half = CONV_K // 2
    acc = cb_ref[...] + xm * cw[half:half + 1, :]
    for sh in range(1, half + 1):
        past = jnp.where(row >= sh, pltpu.roll(xm, sh, axis=0), 0.0)
        acc = acc + past * cw[half - sh:half - sh + 1, :]
        nxt = jnp.where(row < s - sh, pltpu.roll(xm, s - sh, axis=0), 0.0)
        acc = acc + nxt * cw[half + sh:half + sh + 1, :]
    xc = _silu(acc).astype(BF16)
    xc_ref[0] = xc
    q_ref[0] = _dot(xc, wq_ref[0]).astype(BF16)
    k_ref[0] = (_dot_nt(wk_ref[0], xc) * (ML_DH ** -0.5)).astype(BF16)
    v = _dot(xm_b, wv_ref[0])
    ones_lane = lax.broadcasted_iota(I32, (1, ML_DHP), 1) == ML_DH
    v_ref[0] = jnp.where(ones_lane, 1.0, v).astype(BF16)


def _conv_qkv(main3, cw, cb, wq, wk_t, wv):
    b, s, _ = main3.shape
    tok = pl.BlockSpec((1, s, ML_DHP), lambda i, h: (i, 0, h))
    wspec = pl.BlockSpec((1, ML_DHP, ML_DHP), lambda i, h: (h, 0, 0))
    tok_shape = jax.ShapeDtypeStruct((b, s, ML_WP), BF16)
    return pl.pallas_call(
        functools.partial(_conv_qkv_kernel, s=s),
        grid=(b, ML_HEADS),
        in_specs=[tok,
                  pl.BlockSpec((CONV_K, ML_DHP), lambda i, h: (0, h)),
                  pl.BlockSpec((1, ML_DHP), lambda i, h: (0, h)),
                  wspec, wspec, wspec],
        out_specs=[tok, pl.BlockSpec((1, ML_DHP, s), lambda i, h: (i, h, 0)), tok, tok],
        out_shape=[tok_shape, jax.ShapeDtypeStruct((b, ML_WP, s), BF16), tok_shape, tok_shape],
        compiler_params=_cparams("parallel", "parallel"),
        name="conv_qkv",
    )(main3, cw, cb, wq, wk_t, wv)


def _mlstm_kernel(q_ref, kt_ref, v_ref, gc_ref, gr_ref, z_ref, xc_ref, ng_ref, sk_ref,
                  y_ref, hf_ref, hb_ref, cf_ref, cb_ref, m_ref, *, s):
    head0 = pl.program_id(1) * ML_HPS
    nc = s // CHUNK
    sub = lax.broadcasted_iota(I32, (LANES, 1), 0)
    gate = lax.broadcasted_iota(I32, (LANES, LANES), 0)
    ti = lax.broadcasted_iota(I32, (CHUNK, CHUNK), 0)
    tj = lax.broadcasted_iota(I32, (CHUNK, CHUNK), 1)

    for ref in (cf_ref, cb_ref, m_ref):
        ref[...] = jnp.zeros_like(ref)

    def intra(c, j, rev):
        t0 = pl.multiple_of(c * CHUNK, CHUNK)
        hl = slice(j * ML_DHP, (j + 1) * ML_DHP)
        qb = q_ref[0, pl.ds(t0, CHUNK), hl]
        kt = kt_ref[0, hl, pl.ds(t0, CHUNK)]
        vb = v_ref[0, pl.ds(t0, CHUNK), hl]
        gc = gc_ref[0, pl.ds(t0, CHUNK), :]
        gr = gr_ref[:, pl.ds(t0, CHUNK)]
        i_idx = head0 + j + (2 * ML_HEADS if rev else 0)
        f_idx = i_idx + ML_HEADS
        allowed = (tj >= ti) if rev else (tj <= ti)
        sel = (gate == f_idx).astype(BF16)
        b_rep = sum(_dot(part, sel) for part in _split3(gc))
        b_row = jnp.sum(jnp.where(sub == f_idx, gr, 0.0), axis=0, keepdims=True)
        i_row = jnp.sum(jnp.where(sub == i_idx, gr, 0.0), axis=0, keepdims=True)
        b_last = (b_rep[0:1, :] if rev else b_rep[CHUNK - 1:CHUNK, :])[:, 0:1]

        b_wide = jnp.concatenate([b_rep] * (CHUNK // LANES), axis=1)
        d = jnp.where(allowed, b_wide - b_row + i_row, NEG)
        m_in = jnp.max(d, axis=1, keepdims=True)
        sc = _dot(qb, kt) * jnp.exp(d - m_in)
        nd_in = _dot(sc.astype(BF16), vb)
        w_row = b_last - b_row + i_row
        return t0, qb, kt, vb, b_rep, b_last, m_in, nd_in, w_row

    def twice(a):
        return jnp.concatenate([a, a], axis=1)

    def update(parts, j, rev):
        t0, qb, kt, vb, b_rep, b_last, m_in, nd_in, w_row = parts
        h_ref, c_ref = (hb_ref, cb_ref) if rev else (hf_ref, cf_ref)
        hl = slice(j * ML_DHP, (j + 1) * ML_DHP)
        mrow = 2 * j + int(rev)
        m = m_ref[mrow:mrow + 1, 0:1]
        cmat = c_ref[j]
        inter = b_rep + m
        m_t = jnp.maximum(m_in, inter)
        a_in = jnp.exp(m_in - m_t)
        iexp = jnp.exp(inter - m_t)
        nd = twice(a_in) * nd_in + twice(iexp) * _dot(qb, cmat.astype(BF16))
        den = nd[:, ML_DH:ML_DH + 1]
        h_ref[pl.ds(t0, CHUNK), hl] = nd * (1.0 / jnp.maximum(jnp.abs(den), jnp.exp(-m_t[:, 0:1])))

        m_new = jnp.maximum(b_last + m, jnp.max(w_row, axis=1, keepdims=True))
        wexp = jnp.exp(w_row - m_new)
        cexp = jnp.exp(b_last + m - m_new)
        kw = (kt.astype(F32) * wexp).astype(BF16)
        c_ref[j] = cexp * cmat + _dot(kw, vb)
        m_ref[mrow:mrow + 1, :] = jnp.broadcast_to(m_new, (1, LANES))

    def step(i, carry):
        for j in range(ML_HPS):
            parts = [intra(nc - 1 - i if rev else i, j, rev) for rev in (False, True)]
            for p, rev in zip(parts, (False, True)):
                update(p, j, rev)
        return carry

    lax.fori_loop(0, nc, step, 0)

    real = lax.broadcasted_iota(I32, (1, ML_DHP), 1) < ML_DH
    tb = CHUNK

    def fin(c, carry):
        t0 = pl.multiple_of(c * tb, tb)
        for j in range(ML_HPS):
            hl = slice(j * ML_DHP, (j + 1) * ML_DHP)
            hs = jnp.where(real, hf_ref[pl.ds(t0, tb), hl] + hb_ref[pl.ds(t0, tb), hl], 0.0)
            mu = jnp.sum(hs, axis=1, keepdims=True) * (1.0 / ML_DH)
            dev = jnp.where(real, hs - mu, 0.0)
            var = jnp.sum(dev * dev, axis=1, keepdims=True) * (1.0 / ML_DH)
            hn = dev * lax.rsqrt(var + LN_EPS) * ng_ref[:, hl]
            xc = xc_ref[0, pl.ds(t0, tb), hl].astype(F32)
            z = z_ref[0, pl.ds(t0, tb), hl].astype(F32)
            y_ref[0, pl.ds(t0, tb), hl] = ((hn + sk_ref[:, hl] * xc) * _silu(z)).astype(BF16)
        return carry

    lax.fori_loop(0, s // tb, fin, 0)


def _mlstm(q, kt, v, gcol3, grow, main3, xc, ng, sk):
    b, s, _ = q.shape
    width = ML_HPS * ML_DHP
    steps = ML_HEADS // ML_HPS
    tok = pl.BlockSpec((1, s, width), lambda i, h: (i, 0, h))
    vec = pl.BlockSpec((1, width), lambda i, h: (0, h))
    return pl.pallas_call(
        functools.partial(_mlstm_kernel, s=s),
        grid=(b, steps),
        in_specs=[tok, pl.BlockSpec((1, width, s), lambda i, h: (i, h, 0)), tok,
                  pl.BlockSpec((1, s, LANES), lambda i, h: (i, 0, 0)),
                  pl.BlockSpec((LANES, s), lambda i, h: (0, i)),
                  pl.BlockSpec((1, s, width), lambda i, h: (i, 0, steps + h)),
                  tok, vec, vec],
        out_specs=tok,
        out_shape=jax.ShapeDtypeStruct((b, s, ML_WP), BF16),
        scratch_shapes=[pltpu.VMEM((s, width), F32), pltpu.VMEM((s, width), F32),
                        pltpu.VMEM((ML_HPS, ML_DHP, ML_DHP), F32), pltpu.VMEM((ML_HPS, ML_DHP, ML_DHP), F32),
                        pltpu.VMEM((SUBLANES, LANES), F32)],
        compiler_params=_cparams("parallel", "parallel"),
        name="mlstm",
    )(q, kt, v, gcol3, grow, main3, xc, ng, sk)


def _pad_heads(a, axis):
    a = jnp.moveaxis(a, axis, -1)
    lead = a.shape[:-1]
    a = a.reshape(lead + (ML_HEADS, ML_DH))
    a = jnp.pad(a, [(0, 0)] * len(lead) + [(0, 0), (0, ML_DHP - ML_DH)])
    return jnp.moveaxis(a.reshape(lead + (ML_WP,)), -1, axis)


def kernel(x, mem, mem_ln_g, mem_ln_b, w_mem_kv, router_w, router_b, na_w_in, na_rpb, ml_w_in, ml_conv_w,
           ml_conv_b, ml_w_qkv, ml_gate_b, ml_norm_g, ml_skip, w_out, ln_g, ln_b, exp_w_gate, exp_w_up,
           exp_w_down):
    b, s, d = x.shape
    n = b * s
    nm = mem.shape[1]
    row = lambda a: a.reshape(1, -1)

    mem_k, mem_v = _memkv(mem.reshape(b * nm, d), row(mem_ln_g), row(mem_ln_b), w_mem_kv.astype(BF16))
    mem_k3 = mem_k.reshape(b, nm, MEM_W)
    mem_v3 = mem_v.reshape(b, nm, MEM_W)
    rw_pad = jnp.pad(router_w, ((0, 0), (0, LANES - N_EXPERTS)))
    rw_hi = rw_pad.astype(BF16)
    rw = (rw_hi, (rw_pad - rw_hi.astype(F32)).astype(BF16))
    rb = router_b.reshape(N_EXPERTS, 1)

    x2 = x.reshape(n, d)

    h0 = _proj(x2, na_w_in[0].astype(BF16)).reshape(b, s, 3 * NA_W + MEM_W)
    y_na = _na_attention(h0, _na_bias_table(na_rpb[0]))
    wo = w_out[0].astype(BF16)
    xr = _outproj_ln(y_na.reshape(n, NA_W), h0.reshape(n, 3 * NA_W + MEM_W), 3 * NA_W // MEM_W, mem_k3, mem_v3,
                     wo[:NA_W], wo[NA_W:], x2, row(ln_g[0, 0]), row(ln_b[0, 0]))
    x2 = _moe_ln(xr, rw, rb, exp_w_gate, exp_w_up, exp_w_down, 0, row(ln_g[0, 1]), row(ln_b[0, 1]))

    w1 = ml_w_in[0]
    w_main = jnp.concatenate([_pad_heads(w1[:, :ML_W], 1), _pad_heads(w1[:, ML_W:2 * ML_W], 1),
                              w1[:, 2 * ML_W + 4 * ML_HEADS:]], axis=1).astype(BF16)
    w_g = jnp.pad(w1[:, 2 * ML_W:2 * ML_W + 4 * ML_HEADS], ((0, 0), (0, LANES - 4 * ML_HEADS))).astype(BF16)
    gb = jnp.pad(ml_gate_b[0].reshape(4 * ML_HEADS), (0, LANES - 4 * ML_HEADS))
    main, acol, arow = _proj_gates(x2, w_main, w_g, w_g.T, gb.reshape(1, LANES), gb.reshape(LANES, 1))
    main3 = main.reshape(b, s, 2 * ML_WP + MEM_W)
    wqkv = jnp.pad(ml_w_qkv[0], ((0, 0), (0, 0), (0, ML_DHP - ML_DH), (0, ML_DHP - ML_DH))).astype(BF16)
    q, k, v, xc = _conv_qkv(main3, _pad_heads(ml_conv_w[0], 1), _pad_heads(row(ml_conv_b[0]), 1),
                            wqkv[0], jnp.swapaxes(wqkv[1], 1, 2), wqkv[2])
    y_ml = _mlstm(q, k, v, acol.reshape(b, s, LANES), arow, main3, xc,
                  _pad_heads(row(ml_norm_g[0]), 1), _pad_heads(row(ml_skip[0]), 1))
    wo = w_out[1]
    xr = _outproj_ln(y_ml.reshape(n, ML_WP), main, 2 * ML_WP // MEM_W, mem_k3, mem_v3,
                     _pad_heads(wo[:ML_W], 0).astype(BF16), wo[ML_W:].astype(BF16), x2,
                     row(ln_g[1, 0]), row(ln_b[1, 0]))
    x2 = _moe_ln(xr, rw, rb, exp_w_gate, exp_w_up, exp_w_down, 1, row(ln_g[1, 1]), row(ln_b[1, 1]))
    return x2.reshape(b, s, d)
```

```python
import functools

import numpy as np
import jax
import jax.numpy as jnp
from jax import lax
from jax.experimental import pallas as pl
from jax.experimental.pallas import tpu as pltpu

F32 = jnp.float32
BF16 = jnp.bfloat16
I32 = jnp.int32

D_MODEL = 1024
DEPTH = 2
GRID_W = 64
MEM_HEADS = 4
MEM_DH = 64
MEM_W = MEM_HEADS * MEM_DH
NA_HEADS = 12
NA_DH = 64
NA_W = NA_HEADS * NA_DH
WIN_H = 8
WIN_W = 16
ML_HEADS = 4
ML_DH = 192
ML_DHP = 256
ML_W = ML_HEADS * ML_DH
ML_WP = ML_HEADS * ML_DHP
CONV_K = 5
CHUNK = 256
N_EXPERTS = 16
N_GROUPS = 4
EXPERTS_PER_GROUP = N_EXPERTS // N_GROUPS
D_EXPERT = 512
ALPHA = (2 * DEPTH) ** 0.25
LN_EPS = 1e-5
NEG = -1e30

LANES = 128
SUBLANES = 8
ROW_CHUNKS = D_MODEL // LANES
MOE_BM = 512
MOE_TILE = 512
ML_HPS = 2
NA_ROWS_PER_STEP = 16
ROW_TILE = 1024
PLACE_UNROLL = 8
VMEM_LIMIT = 48 * 1024 * 1024


def _cparams(*sem):
    return pltpu.CompilerParams(dimension_semantics=sem, vmem_limit_bytes=VMEM_LIMIT)


def _dot(a, b):
    return jnp.dot(a, b, preferred_element_type=F32)


def _dot_nt(a, b, precision=None):
    return lax.dot_general(a, b, (((1,), (1,)), ((), ())), precision=precision,
                           preferred_element_type=F32)


def _ln(z, g, b):
    mu = jnp.mean(z, axis=-1, keepdims=True)
    zc = z - mu
    var = jnp.mean(zc * zc, axis=-1, keepdims=True)
    return zc * lax.rsqrt(var + LN_EPS) * g + b


def _silu(x):
    return x * jax.nn.sigmoid(x)


def _read_rows(ref, n):
    return jnp.concatenate([ref[pl.ds(j, n, stride=ROW_CHUNKS), :] for j in range(ROW_CHUNKS)], axis=1)


def _write_rows(ref, val, n):
    for j in range(ROW_CHUNKS):
        ref[pl.ds(j, n, stride=ROW_CHUNKS), :] = val[:, j * LANES:(j + 1) * LANES]


def _memkv_kernel(m_ref, g_ref, b_ref, w_ref, k_ref, v_ref):
    z = _ln(m_ref[...], g_ref[...], b_ref[...])
    kv = _dot(z.astype(BF16), w_ref[...])
    k_ref[...] = kv[:, :MEM_W].astype(BF16)
    v_ref[...] = kv[:, MEM_W:].astype(BF16)


def _memkv(mem2, g, b, w):
    n = mem2.shape[0]
    tm = min(ROW_TILE, n)
    return pl.pallas_call(
        _memkv_kernel,
        grid=(n // tm,),
        in_specs=[pl.BlockSpec((tm, D_MODEL), lambda i: (i, 0)),
                  pl.BlockSpec((1, D_MODEL), lambda i: (0, 0)),
                  pl.BlockSpec((1, D_MODEL), lambda i: (0, 0)),
                  pl.BlockSpec((D_MODEL, 2 * MEM_W), lambda i: (0, 0))],
        out_specs=[pl.BlockSpec((tm, MEM_W), lambda i: (i, 0)),
                   pl.BlockSpec((tm, MEM_W), lambda i: (i, 0))],
        out_shape=[jax.ShapeDtypeStruct((n, MEM_W), BF16)] * 2,
        compiler_params=_cparams("parallel"),
        name="memkv",
    )(mem2, g, b, w)


def _proj_kernel(x_ref, w_ref, o_ref):
    o_ref[...] = _dot(x_ref[...].astype(BF16), w_ref[...]).astype(o_ref.dtype)


def _proj(x2, w, tm=ROW_TILE):
    n, k = x2.shape
    nout = w.shape[1]
    return pl.pallas_call(
        _proj_kernel,
        grid=(n // tm,),
        in_specs=[pl.BlockSpec((tm, k), lambda i: (i, 0)),
                  pl.BlockSpec((k, nout), lambda i: (0, 0))],
        out_specs=pl.BlockSpec((tm, nout), lambda i: (i, 0)),
        out_shape=jax.ShapeDtypeStruct((n, nout), BF16),
        compiler_params=_cparams("parallel"),
        name="in_proj",
    )(x2, w)


def _split3(x):
    hi = x.astype(BF16)
    r1 = x - hi.astype(F32)
    mid = r1.astype(BF16)
    lo = (r1 - mid.astype(F32)).astype(BF16)
    return hi, mid, lo


def _proj_gates_kernel(x_ref, w_ref, wg_ref, wgt_ref, gbc_ref, gbr_ref, o_ref, g_ref, gt_ref, *, tm):
    xb = x_ref[...].astype(BF16)
    o_ref[...] = _dot(xb, w_ref[...]).astype(BF16)
    gcol = _dot(xb, wg_ref[...]) + gbc_ref[...]
    grow = _dot_nt(wgt_ref[...], xb) + gbr_ref[...]
    lane = lax.broadcasted_iota(I32, (1, LANES), 1)
    sub = lax.broadcasted_iota(I32, (LANES, 1), 0)
    ti = lax.broadcasted_iota(I32, (CHUNK, CHUNK), 0)
    tj = lax.broadcasted_iota(I32, (CHUNK, CHUNK), 1)
    lower = (tj <= ti).astype(BF16)
    upper = (ti <= tj).astype(BF16)

    def pick(idx, pre, suf, raw):
        fwd = jnp.logical_and(idx >= ML_HEADS, idx < 2 * ML_HEADS)
        bwd = jnp.logical_and(idx >= 3 * ML_HEADS, idx < 4 * ML_HEADS)
        return jnp.where(fwd, pre, jnp.where(bwd, suf, raw))

    for c in range(tm // CHUNK):
        tc = slice(c * CHUNK, (c + 1) * CHUNK)
        g = gcol[tc, :]
        ls = jax.nn.log_sigmoid(g)
        pre = sum(_dot(lower, part) for part in _split3(ls))
        suf = jnp.sum(ls, axis=0, keepdims=True) - pre + ls
        g_ref[tc, :] = pick(lane, pre, suf, g)
        g = grow[:, tc]
        ls = jax.nn.log_sigmoid(g)
        pre = sum(_dot(part, upper) for part in _split3(ls))
        suf = jnp.sum(ls, axis=1, keepdims=True) - pre + ls
        gt_ref[:, tc] = pick(sub, pre, suf, g)


def _proj_gates(x2, w, wg, wgt, gbc, gbr, tm=ROW_TILE):
    n, k = x2.shape
    nout = w.shape[1]
    return pl.pallas_call(
        functools.partial(_proj_gates_kernel, tm=tm),
        grid=(n // tm,),
        in_specs=[pl.BlockSpec((tm, k), lambda i: (i, 0)),
                  pl.BlockSpec((k, nout), lambda i: (0, 0)),
                  pl.BlockSpec((k, LANES), lambda i: (0, 0)),
                  pl.BlockSpec((LANES, k), lambda i: (0, 0)),
                  pl.BlockSpec((1, LANES), lambda i: (0, 0)),
                  pl.BlockSpec((LANES, 1), lambda i: (0, 0))],
        out_specs=[pl.BlockSpec((tm, nout), lambda i: (i, 0)),
                   pl.BlockSpec((tm, LANES), lambda i: (i, 0)),
                   pl.BlockSpec((LANES, tm), lambda i: (0, i))],
        out_shape=[jax.ShapeDtypeStruct((n, nout), BF16),
                   jax.ShapeDtypeStruct((n, LANES), F32),
                   jax.ShapeDtypeStruct((LANES, n), F32)],
        compiler_params=_cparams("parallel"),
        name="in_proj_gates",
    )(x2, w, wg, wgt, gbc, gbr)


def _na_kernel(q_ref, k_ref, v_ref, tbl_ref, o_ref, *, rows):
    lane = lax.broadcasted_iota(I32, (1, LANES), 1)
    first = lane < NA_DH
    nkeys = WIN_H * GRID_W

    def rows_step(i, carry):
        rr = [i * NA_ROWS_PER_STEP + u for u in range(NA_ROWS_PER_STEP)]
        rss = [jnp.clip(r - WIN_H // 2, 0, rows - WIN_H) for r in rr]
        scores = []
        for r, rs in zip(rr, rss):
            q = q_ref[0, pl.ds(pl.multiple_of(r * GRID_W, GRID_W), GRID_W), :]
            q = q * jnp.asarray(NA_DH ** -0.5, BF16)
            q2 = jnp.concatenate([jnp.where(first, q, jnp.zeros_like(q)),
                                  jnp.where(first, jnp.zeros_like(q), q)], axis=0)
            k = k_ref[0, pl.ds(pl.multiple_of(rs * GRID_W, GRID_W), nkeys), :]
            dr0 = rs - r + WIN_H - 1
            bias = jnp.concatenate(
                [jnp.concatenate([tbl_ref[0, half, dr0 + 2 * m] for m in range(WIN_H // 2)], axis=1)
                 for half in range(2)], axis=0)
            scores.append(_dot_nt(q2, k) + bias)
        probs = []
        for s in scores:
            p = jnp.exp(s - jnp.max(s, axis=-1, keepdims=True))
            probs.append((p.astype(BF16), jnp.sum(p, axis=-1, keepdims=True)))
        for r, rs, (p, l) in zip(rr, rss, probs):
            v = v_ref[0, pl.ds(pl.multiple_of(rs * GRID_W, GRID_W), nkeys), :]
            o = _dot(p, v) / l
            o = jnp.where(first, o[:GRID_W], o[GRID_W:])
            o_ref[0, pl.ds(pl.multiple_of(r * GRID_W, GRID_W), GRID_W), :] = o.astype(o_ref.dtype)
        return carry

    lax.fori_loop(0, rows // NA_ROWS_PER_STEP, rows_step, 0)


def _na_bias_table(rpb):
    qc = np.arange(GRID_W)[:, None]
    kc = np.arange(GRID_W)[None, :]
    cs = np.clip(qc - WIN_W // 2, 0, GRID_W - WIN_W)
    col_in = (kc >= cs) & (kc < cs + WIN_W)
    side = GRID_W - WIN_W
    wide = jnp.pad(rpb, ((0, 0), (0, 0), (side, side)))
    t = jnp.stack([wide[:, :, GRID_W - 1 - q:2 * GRID_W - 1 - q] for q in range(GRID_W)], axis=2)
    t = jnp.where(col_in, t, NEG).astype(F32)
    t2 = jnp.concatenate([t[:, :-1], t[:, 1:]], axis=-1)
    return t2.reshape(NA_HEADS // 2, 2, 2 * WIN_H - 2, GRID_W, 2 * GRID_W)


def _na_attention(h3, tbl):
    b, s, _ = h3.shape
    rows = s // GRID_W
    npair = NA_HEADS // 2
    return pl.pallas_call(
        functools.partial(_na_kernel, rows=rows),
        grid=(b, npair),
        in_specs=[pl.BlockSpec((1, s, LANES), lambda i, p: (i, 0, p)),
                  pl.BlockSpec((1, s, LANES), lambda i, p: (i, 0, npair + p)),
                  pl.BlockSpec((1, s, LANES), lambda i, p: (i, 0, 2 * npair + p)),
                  pl.BlockSpec((1, 2, 2 * WIN_H - 2, GRID_W, 2 * GRID_W), lambda i, p: (p, 0, 0, 0, 0))],
        out_specs=pl.BlockSpec((1, s, LANES), lambda i, p: (i, 0, p)),
        out_shape=jax.ShapeDtypeStruct((b, s, NA_W), BF16),
        compiler_params=_cparams("parallel", "parallel"),
        name="na_attention",
    )(h3, h3, h3, tbl)


def _outproj_ln_kernel(ya_ref, qm_ref, mk_ref, mv_ref, wa_ref, wm_ref, x_ref, g_ref, b_ref, or_ref, *, tm):
    lane = lax.broadcasted_iota(I32, (1, LANES), 1)
    first = lane < MEM_DH
    q = qm_ref[...] * jnp.asarray(MEM_DH ** -0.5, BF16)
    cols = [slice(p * LANES, (p + 1) * LANES) for p in range(MEM_HEADS // 2)]
    scores = []
    for c in cols:
        qp = q[:, c]
        q2 = jnp.concatenate([jnp.where(first, qp, jnp.zeros_like(qp)),
                              jnp.where(first, jnp.zeros_like(qp), qp)], axis=0)
        scores.append(_dot_nt(q2, mk_ref[0, :, c]))
    probs = []
    for s in scores:
        p = jnp.exp(s - jnp.max(s, axis=-1, keepdims=True))
        probs.append((p.astype(BF16), jnp.sum(p, axis=-1, keepdims=True)))
    outs = []
    for c, (p, l) in zip(cols, probs):
        o = _dot(p, mv_ref[0, :, c]) / l
        outs.append(jnp.where(first, o[:tm], o[tm:]))
    ym = jnp.concatenate(outs, axis=1).astype(BF16)
    acc = _dot(ya_ref[...], wa_ref[...]) + _dot(ym, wm_ref[...])
    _write_rows(or_ref, _ln(ALPHA * x_ref[...] + acc, g_ref[...], b_ref[...]), tm)


def _outproj_ln(ya, h2, qm_block, mem_k3, mem_v3, wa, wm, x2, g, b, tm=ROW_TILE):
    n = x2.shape[0]
    ka = ya.shape[1]
    nb, nm, _ = mem_k3.shape
    per_batch = n // nb // tm
    full = lambda shape: pl.BlockSpec(shape, lambda i: (0,) * len(shape))
    return pl.pallas_call(
        functools.partial(_outproj_ln_kernel, tm=tm),
        grid=(n // tm,),
        in_specs=[pl.BlockSpec((tm, ka), lambda i: (i, 0)),
                  pl.BlockSpec((tm, MEM_W), lambda i: (i, qm_block)),
                  pl.BlockSpec((1, nm, MEM_W), lambda i: (i // per_batch, 0, 0)),
                  pl.BlockSpec((1, nm, MEM_W), lambda i: (i // per_batch, 0, 0)),
                  full((ka, D_MODEL)), full((MEM_W, D_MODEL)),
                  pl.BlockSpec((tm, D_MODEL), lambda i: (i, 0)),
                  full((1, D_MODEL)), full((1, D_MODEL))],
        out_specs=pl.BlockSpec((tm * ROW_CHUNKS, LANES), lambda i: (i, 0)),
        out_shape=jax.ShapeDtypeStruct((n * ROW_CHUNKS, LANES), F32),
        compiler_params=_cparams("parallel"),
        name="outproj_ln",
    )(ya, h2, mem_k3, mem_v3, wa, wm, x2, g, b)


def _router_kernel(x_ref, rwh_ref, rwl_ref, rb_ref, lpos_ref, w_ref, cnt_ref, tcnt_ref, toff_ref, tbef_ref, *, tm):
    @pl.when(pl.program_id(0) == 0)
    def _():
        cnt_ref[...] = jnp.zeros_like(cnt_ref)

    x = _read_rows(x_ref, tm)
    xh = x.astype(BF16)
    xl = (x - xh.astype(F32)).astype(BF16)
    logits_t = _dot(xh, rwh_ref[...]) + (_dot(xh, rwl_ref[...]) + _dot(xl, rwh_ref[...]))
    logits = logits_t.T[:N_EXPERTS]
    scores = jax.nn.sigmoid(logits)
    biased = scores + rb_ref[...]
    bv = [biased[e:e + 1, :] for e in range(N_EXPERTS)]
    sv = [scores[e:e + 1, :] for e in range(N_EXPERTS)]

    grp = []
    for g in range(N_GROUPS):
        m = bv[g * EXPERTS_PER_GROUP:(g + 1) * EXPERTS_PER_GROUP]
        best = None
        for a in range(EXPERTS_PER_GROUP):
            for c in range(a + 1, EXPERTS_PER_GROUP):
                pair = m[a] + m[c]
                best = pair if best is None else jnp.maximum(best, pair)
        grp.append(best)
    gsel = jnp.zeros((1, tm), I32)
    gbest = grp[0]
    for g in range(1, N_GROUPS):
        better = grp[g] > gbest
        gsel = jnp.where(better, g, gsel)
        gbest = jnp.where(better, grp[g], gbest)

    def pick(vals, j):
        out = vals[j]
        for g in range(1, N_GROUPS):
            out = jnp.where(gsel == g, vals[g * EXPERTS_PER_GROUP + j], out)
        return out

    cb = [pick(bv, j) for j in range(EXPERTS_PER_GROUP)]
    cs = [pick(sv, j) for j in range(EXPERTS_PER_GROUP)]
    i1 = jnp.zeros((1, tm), I32)
    m1 = cb[0]
    s1 = cs[0]
    for j in range(1, EXPERTS_PER_GROUP):
        gt = cb[j] > m1
        i1 = jnp.where(gt, j, i1)
        m1 = jnp.where(gt, cb[j], m1)
        s1 = jnp.where(gt, cs[j], s1)
    i2 = jnp.zeros((1, tm), I32)
    m2 = jnp.full((1, tm), -jnp.inf, F32)
    s2 = jnp.zeros((1, tm), F32)
    for j in range(EXPERTS_PER_GROUP):
        ok = jnp.logical_and(i1 != j, cb[j] > m2)
        i2 = jnp.where(ok, j, i2)
        m2 = jnp.where(ok, cb[j], m2)
        s2 = jnp.where(ok, cs[j], s2)
    e1 = gsel * EXPERTS_PER_GROUP + i1
    e2 = gsel * EXPERTS_PER_GROUP + i2
    tot = s1 + s2
    w_ref[...] = jnp.concatenate([s1 / tot, s2 / tot], axis=0)

    i = pl.program_id(0)
    eio = lax.broadcasted_iota(I32, (N_EXPERTS, tm), 0)
    oh1 = eio == e1
    oh2 = eio == e2
    ohs = jnp.logical_or(oh1, oh2).astype(F32)
    before = (lax.broadcasted_iota(I32, (tm, tm), 0) < lax.broadcasted_iota(I32, (tm, tm), 1))
    pre = _dot(ohs.astype(BF16), before.astype(BF16))
    tile_cnt = jnp.sum(ohs, axis=1, keepdims=True)
    offs = []
    acc = jnp.zeros((1, 1), F32)
    for e in range(N_EXPERTS):
        offs.append(acc)
        acc = acc + tile_cnt[e:e + 1, :]
    tile_off = jnp.concatenate(offs, axis=0)
    pos = tile_off + pre
    p1 = jnp.sum(jnp.where(oh1, pos, 0.0), axis=0, keepdims=True)
    p2 = jnp.sum(jnp.where(oh2, pos, 0.0), axis=0, keepdims=True)
    lpos_ref[...] = jnp.concatenate([p1, p2], axis=0).astype(I32) * ROW_CHUNKS

    @pl.when(i == 0)
    def _():
        for ref in (tcnt_ref, toff_ref, tbef_ref):
            ref[...] = jnp.zeros_like(ref)

    here = lax.broadcasted_iota(I32, (1, LANES), 1) == i
    tcnt_ref[...] = jnp.where(here, tile_cnt, tcnt_ref[...])
    toff_ref[...] = jnp.where(here, tile_off, toff_ref[...])
    tbef_ref[...] = jnp.where(here, cnt_ref[:, 0:1], tbef_ref[...])
    cnt_ref[...] += tile_cnt


def _router(xr, rw, rb, tm):
    n = xr.shape[0] // ROW_CHUNKS
    assert n // tm <= LANES
    table = pl.BlockSpec((N_EXPERTS, LANES), lambda i: (0, 0))
    return pl.pallas_call(
        functools.partial(_router_kernel, tm=tm),
        grid=(n // tm,),
        in_specs=[pl.BlockSpec((tm * ROW_CHUNKS, LANES), lambda i: (i, 0)),
                  pl.BlockSpec((D_MODEL, LANES), lambda i: (0, 0)),
                  pl.BlockSpec((D_MODEL, LANES), lambda i: (0, 0)),
                  pl.BlockSpec((N_EXPERTS, 1), lambda i: (0, 0))],
        out_specs=[pl.BlockSpec((2, tm), lambda i: (0, i)),
                   pl.BlockSpec((2, tm), lambda i: (0, i)),
                   table, table, table, table],
        out_shape=[jax.ShapeDtypeStruct((2, n), I32),
                   jax.ShapeDtypeStruct((2, n), F32)]
                  + [jax.ShapeDtypeStruct((N_EXPERTS, LANES), F32)] * 4,
        compiler_params=_cparams("arbitrary"),
        name="router",
    )(xr, rw[0], rw[1], rb)


def _plan_kernel(cnt_ref, tbef_ref, meta_ref, rstart_ref, *, nbl):
    shift = MOE_BM.bit_length() - 1
    cnt = cnt_ref[...].astype(I32)
    padded = ((cnt + (MOE_BM - 1)) >> shift) << shift
    starts = []
    acc = jnp.zeros((1, LANES), I32)
    for e in range(N_EXPERTS):
        starts.append(acc)
        acc = acc + padded[e:e + 1, :]
    pad_start = jnp.concatenate(starts, axis=0)
    pad_end = pad_start + padded
    rstart_ref[...] = pad_start + tbef_ref[...].astype(I32)
    blk0 = lax.broadcasted_iota(I32, (N_EXPERTS, nbl), 1) * MOE_BM
    block_e = jnp.sum((pad_end[:, 0:1] <= blk0).astype(I32), axis=0, keepdims=True)
    block_e = jnp.minimum(block_e, N_EXPERTS - 1)
    n_used = jnp.broadcast_to(acc[:, 0:1] >> shift, (1, nbl))
    diag = lax.broadcasted_iota(I32, (N_EXPERTS, nbl), 0) == lax.broadcasted_iota(I32, (N_EXPERTS, nbl), 1)
    fill_lo = jnp.sum(jnp.where(diag, (pad_start + cnt)[:, 0:1], 0), axis=0, keepdims=True)
    fill_hi = jnp.sum(jnp.where(diag, pad_end[:, 0:1], 0), axis=0, keepdims=True)
    meta_ref[...] = jnp.concatenate([block_e, n_used, fill_lo, fill_hi, jnp.zeros((SUBLANES - 4, nbl), I32)],
                                    axis=0)


def _plan(cnt, tbef, n_blocks):
    nbl = -(-n_blocks // LANES) * LANES
    table = pl.BlockSpec((N_EXPERTS, LANES), lambda i: (0, 0))
    return pl.pallas_call(
        functools.partial(_plan_kernel, nbl=nbl),
        grid=(1,),
        in_specs=[table, table],
        out_specs=[pl.BlockSpec((SUBLANES, nbl), lambda i: (0, 0)), table],
        out_shape=[jax.ShapeDtypeStruct((SUBLANES, nbl), I32),
                   jax.ShapeDtypeStruct((N_EXPERTS, LANES), I32)],
        compiler_params=_cparams("arbitrary"),
        name="moe_plan",
    )(cnt, tbef)


def _rows(ref, row, nrows):
    return ref.at[pl.ds(pl.multiple_of(row * ROW_CHUNKS, ROW_CHUNKS), nrows * ROW_CHUNKS), :]


def _rows_wait(src_hbm, buf, sem):
    pltpu.make_async_copy(src_hbm.at[pl.ds(0, buf.shape[0]), :], buf, sem).wait()


def _copy_pieces(src, src_row, dst, dst_row, count, max_rows, sem, wait=False):
    bit = max_rows.bit_length() - 1
    while bit >= 0:
        size = 1 << bit
        done = (count >> (bit + 1)) << (bit + 1)

        @pl.when(((count >> bit) & 1) == 1)
        def _():
            cp = pltpu.make_async_copy(_rows(src, src_row + done, size), _rows(dst, dst_row + done, size), sem)
            cp.start()
            if wait:
                cp.wait()

        bit -= 1


def _tile_runs(tcnt_ref, toff_ref, rstart_ref, tile, buf, hbm, sem, *, to_hbm, tm):
    def per_expert(e, carry):
        k = tile * N_EXPERTS + e
        if to_hbm:
            _copy_pieces(buf, toff_ref[k], hbm, rstart_ref[k], tcnt_ref[k], tm, sem)
        else:
            _copy_pieces(hbm, rstart_ref[k], buf, toff_ref[k], tcnt_ref[k], tm, sem)
        return carry

    lax.fori_loop(0, N_EXPERTS, per_expert, 0)


def _dispatch_kernel(lpos_ref, tcnt_ref, toff_ref, rstart_ref, flo_ref, fhi_ref, nu_ref, x_ref, old_hbm, xs_hbm,
                     s0, s1, zbuf, sem, zsem, *, n, tm, n_blocks, reuse):
    del old_hbm
    i = pl.program_id(0)
    nt = pl.num_programs(0)
    bufs = (s0, s1)
    unroll = PLACE_UNROLL

    for slot in range(2):
        @pl.when(i % 2 == slot)
        def _():
            buf = bufs[slot]

            @pl.when(i >= 2)
            def _():
                _rows_wait(xs_hbm, buf, sem.at[slot])

            def place(c, carry):
                tok = i * tm + c * unroll
                src = pl.multiple_of(c * (unroll * ROW_CHUNKS), unroll * ROW_CHUNKS)
                for u in range(unroll):
                    v = x_ref[pl.ds(src + u * ROW_CHUNKS, ROW_CHUNKS), :]
                    for k in range(2):
                        p = lpos_ref[k * n + tok + u]
                        buf[pl.ds(pl.multiple_of(p, ROW_CHUNKS), ROW_CHUNKS), :] = v
                return carry

            lax.fori_loop(0, tm // unroll, place, 0)
            _tile_runs(tcnt_ref, toff_ref, rstart_ref, i, buf, xs_hbm, sem.at[slot], to_hbm=True, tm=tm)

    @pl.when(i == nt - 1)
    def _():
        for slot in range(2):
            @pl.when(nt > slot)
            def _():
                _rows_wait(xs_hbm, bufs[slot], sem.at[slot])

        if not reuse:
            zbuf[...] = jnp.zeros_like(zbuf)
            for e in range(N_EXPERTS):
                _copy_pieces(zbuf, 0, xs_hbm, flo_ref[e], fhi_ref[e] - flo_ref[e], MOE_BM // 2, zsem, wait=True)

            def zero_block(j, carry):
                cp = pltpu.make_async_copy(zbuf, _rows(xs_hbm, j * MOE_BM, MOE_BM), zsem)
                cp.start()
                cp.wait()
                return carry

            lax.fori_loop(nu_ref[0], n_blocks, zero_block, 0)


def _dispatch(lpos_flat, tcnt, toff, rstart, fill_lo, fill_hi, n_used, xr, n_blocks, tm, old=None):
    n = xr.shape[0] // ROW_CHUNKS
    reuse = old is not None
    if not reuse:
        old = jnp.zeros((SUBLANES, LANES), F32)
    return pl.pallas_call(
        functools.partial(_dispatch_kernel, n=n, tm=tm, n_blocks=n_blocks, reuse=reuse),
        grid_spec=pltpu.PrefetchScalarGridSpec(
            num_scalar_prefetch=7,
            grid=(n // tm,),
            in_specs=[pl.BlockSpec((tm * ROW_CHUNKS, LANES), lambda i, *_: (i, 0)),
                      pl.BlockSpec(memory_space=pl.ANY)],
            out_specs=pl.BlockSpec(memory_space=pl.ANY),
            scratch_shapes=[pltpu.VMEM((2 * tm * ROW_CHUNKS, LANES), F32),
                            pltpu.VMEM((2 * tm * ROW_CHUNKS, LANES), F32),
                            pltpu.VMEM((MOE_BM * ROW_CHUNKS, LANES), F32),
                            pltpu.SemaphoreType.DMA((2,)),
                            pltpu.SemaphoreType.DMA(())]),
        out_shape=jax.ShapeDtypeStruct((n_blocks * MOE_BM * ROW_CHUNKS, LANES), F32),
        input_output_aliases={8: 0} if reuse else {},
        compiler_params=_cparams("arbitrary"),
        name="moe_dispatch",
    )(lpos_flat, tcnt, toff, rstart, fill_lo, fill_hi, n_used, xr, old)


def _experts_kernel(be_ref, nu_ref, xs_ref, wg_ref, wu_ref, wd_ref, y_ref, wgb, wub, wdb):
    j = pl.program_id(0)
    used = j < nu_ref[0]

    @pl.when(jnp.logical_and(used, jnp.logical_or(j == 0, be_ref[j] != be_ref[jnp.maximum(j - 1, 0)])))
    def _():
        wgb[...] = wg_ref[0, 0].astype(BF16)
        wub[...] = wu_ref[0, 0].astype(BF16)
        wdb[...] = wd_ref[0, 0].astype(BF16)

    @pl.when(used)
    def _():
        x = _read_rows(xs_ref, MOE_BM).astype(BF16)
        h = _silu(_dot(x, wgb[...])) * _dot(x, wub[...])
        _write_rows(y_ref, _dot(h.astype(BF16), wdb[...]), MOE_BM)


def _experts(block_e, n_used, xs, wg, wu, wd, layer):
    n_blocks = block_e.shape[0]

    def last_used(j, nu):
        return jnp.minimum(j, nu[0] - 1)

    def wblk(j, be, nu):
        return (layer, be[last_used(j, nu)], 0, 0)

    return pl.pallas_call(
        _experts_kernel,
        grid_spec=pltpu.PrefetchScalarGridSpec(
            num_scalar_prefetch=2,
            grid=(n_blocks,),
            in_specs=[pl.BlockSpec((MOE_BM * ROW_CHUNKS, LANES), lambda j, be, nu: (last_used(j, nu), 0)),
                      pl.BlockSpec((1, 1, D_MODEL, D_EXPERT), wblk),
                      pl.BlockSpec((1, 1, D_MODEL, D_EXPERT), wblk),
                      pl.BlockSpec((1, 1, D_EXPERT, D_MODEL), wblk)],
            out_specs=pl.BlockSpec((MOE_BM * ROW_CHUNKS, LANES), lambda j, be, nu: (last_used(j, nu), 0)),
            scratch_shapes=[pltpu.VMEM((D_MODEL, D_EXPERT), BF16), pltpu.VMEM((D_MODEL, D_EXPERT), BF16),
                            pltpu.VMEM((D_EXPERT, D_MODEL), BF16)]),
        out_shape=jax.ShapeDtypeStruct(xs.shape, F32),
        input_output_aliases={2: 0},
        compiler_params=_cparams("arbitrary"),
        name="moe_experts",
    )(block_e, n_used, xs, wg, wu, wd)


def _combine_ln_kernel(lpos_ref, tcnt_ref, toff_ref, rstart_ref, y_hbm, x_ref, w1_ref, w2_ref, g_ref, b_ref, o_ref,
                       r0, r1, u1, u2, sem, *, n, tm):
    i = pl.program_id(0)
    nt = pl.num_programs(0)
    bufs = (r0, r1)
    unroll = PLACE_UNROLL

    def fetch(tile, slot):
        _tile_runs(tcnt_ref, toff_ref, rstart_ref, tile, bufs[slot], y_hbm, sem.at[slot], to_hbm=False, tm=tm)

    @pl.when(i == 0)
    def _():
        fetch(0, 0)

    for slot in range(2):
        @pl.when(i % 2 == slot)
        def _():
            @pl.when(i + 1 < nt)
            def _():
                fetch(i + 1, 1 - slot)

            buf = bufs[slot]
            _rows_wait(y_hbm, buf, sem.at[slot])

            def place(c, carry):
                tok = i * tm + c * unroll
                dst0 = pl.multiple_of(c * (unroll * ROW_CHUNKS), unroll * ROW_CHUNKS)
                for u in range(unroll):
                    dst = pl.ds(dst0 + u * ROW_CHUNKS, ROW_CHUNKS)
                    for k, out in enumerate((u1, u2)):
                        p = lpos_ref[k * n + tok + u]
                        out[dst, :] = buf[pl.ds(pl.multiple_of(p, ROW_CHUNKS), ROW_CHUNKS), :]
                return carry

            lax.fori_loop(0, tm // unroll, place, 0)
            moe = w1_ref[...] * _read_rows(u1, tm) + w2_ref[...] * _read_rows(u2, tm)
            o_ref[...] = _ln(ALPHA * _read_rows(x_ref, tm) + moe, g_ref[...], b_ref[...])


def _combine_ln(lpos_flat, tcnt, toff, rstart, y, xr, w1, w2, g, b, tm):
    n = xr.shape[0] // ROW_CHUNKS
    return pl.pallas_call(
        functools.partial(_combine_ln_kernel, n=n, tm=tm),
        grid_spec=pltpu.PrefetchScalarGridSpec(
            num_scalar_prefetch=4,
            grid=(n // tm,),
            in_specs=[pl.BlockSpec(memory_space=pl.ANY),
                      pl.BlockSpec((tm * ROW_CHUNKS, LANES), lambda i, *_: (i, 0)),
                      pl.BlockSpec((tm, 1), lambda i, *_: (i, 0)),
                      pl.BlockSpec((tm, 1), lambda i, *_: (i, 0)),
                      pl.BlockSpec((1, D_MODEL), lambda i, *_: (0, 0)),
                      pl.BlockSpec((1, D_MODEL), lambda i, *_: (0, 0))],
            out_specs=pl.BlockSpec((tm, D_MODEL), lambda i, *_: (i, 0)),
            scratch_shapes=[pltpu.VMEM((2 * tm * ROW_CHUNKS, LANES), F32)] * 2
                           + [pltpu.VMEM((tm * ROW_CHUNKS, LANES), F32)] * 2
                           + [pltpu.SemaphoreType.DMA((2,))]),
        out_shape=jax.ShapeDtypeStruct((n, D_MODEL), F32),
        compiler_params=_cparams("arbitrary"),
        name="moe_combine_ln",
    )(lpos_flat, tcnt, toff, rstart, y, xr, w1, w2, g, b)


def _moe_ln(xr, rw, rb, wg, wu, wd, layer, g, b, old=None):
    n = xr.shape[0] // ROW_CHUNKS
    n_blocks = (2 * n) // MOE_BM + N_EXPERTS
    tm = MOE_TILE
    nt = n // tm
    lpos, w, cnt, tcnt, toff, tbef = _router(xr, rw, rb, tm)
    meta, rstart = _plan(cnt, tbef, n_blocks)
    block_e = meta[0, :n_blocks]
    n_used = meta[1, :1]

    def per_tile(table):
        return table[:, :nt].T.reshape(nt * N_EXPERTS).astype(I32)

    lpos_flat = lpos.reshape(2 * n)
    tcnt, toff, rstart = per_tile(tcnt), per_tile(toff), per_tile(rstart)
    xs = _dispatch(lpos_flat, tcnt, toff, rstart, meta[2, :N_EXPERTS], meta[3, :N_EXPERTS], n_used, xr,
                   n_blocks, tm, old)
    y = _experts(block_e, n_used, xs, wg, wu, wd, layer)
    out = _combine_ln(lpos_flat, tcnt, toff, rstart, y, xr, w[0].reshape(n, 1), w[1].reshape(n, 1), g, b, tm)
    return out, y


def _conv_qkv_kernel(xm_ref, cw_ref, cb_ref, wq_ref, wk_ref, wv_ref, q_ref, k_ref, v_ref, xc_ref, *, s):
    xm_b = xm_ref[0]
    xm = xm_b.astype(F32)
    cw = cw_ref[...]
    row = lax.broadcasted_iota(I32, (s, 1), 0)
    half = CONV_K // 2
    acc = cb_ref[...] + xm * cw[half:half + 1, :]
    for sh in range(1, half + 1):
        past = jnp.where(row >= sh, pltpu.roll(xm, sh, axis=0), 0.0)
        acc = acc + past * cw[half - sh:half - sh + 1, :]
        nxt = jnp.where(row < s - sh, pltpu.roll(xm, s - sh, axis=0), 0.0)
        acc = acc + nxt * cw[half + sh:half + sh + 1, :]
    xc = _silu(acc).astype(BF16)
    xc_ref[0] = xc
    q_ref[0] = _dot(xc, wq_ref[0]).astype(BF16)
    k_ref[0] = (_dot_nt(wk_ref[0], xc) * (ML_DH ** -0.5)).astype(BF16)
    v = _dot(xm_b, wv_ref[0])
    ones_lane = lax.broadcasted_iota(I32, (1, ML_DHP), 1) == ML_DH
    v_ref[0] = jnp.where(ones_lane, 1.0, v).astype(BF16)


def _conv_qkv(main3, cw, cb, wq, wk_t, wv):
    b, s, _ = main3.shape
    tok = pl.BlockSpec((1, s, ML_DHP), lambda i, h: (i, 0, h))
    wspec = pl.BlockSpec((1, ML_DHP, ML_DHP), lambda i, h: (h, 0, 0))
    tok_shape = jax.ShapeDtypeStruct((b, s, ML_WP), BF16)
    return pl.pallas_call(
        functools.partial(_conv_qkv_kernel, s=s),
        grid=(b, ML_HEADS),
        in_specs=[tok,
                  pl.BlockSpec((CONV_K, ML_DHP), lambda i, h: (0, h)),
                  pl.BlockSpec((1, ML_DHP), lambda i, h: (0, h)),
                  wspec, wspec, wspec],
        out_specs=[tok, pl.BlockSpec((1, ML_DHP, s), lambda i, h: (i, h, 0)), tok, tok],
        out_shape=[tok_shape, jax.ShapeDtypeStruct((b, ML_WP, s), BF16), tok_shape, tok_shape],
        compiler_params=_cparams("parallel", "parallel"),
        name="conv_qkv",
    )(main3, cw, cb, wq, wk_t, wv)


def _mlstm_kernel(q_ref, kt_ref, v_ref, gc_ref, gr_ref, z_ref, xc_ref, ng_ref, sk_ref,
                  y_ref, hf_ref, hb_ref, cf_ref, cb_ref, m_ref, *, s):
    head0 = pl.program_id(1) * ML_HPS
    nc = s // CHUNK
    sub = lax.broadcasted_iota(I32, (LANES, 1), 0)
    gate = lax.broadcasted_iota(I32, (LANES, LANES), 0)
    ti = lax.broadcasted_iota(I32, (CHUNK, CHUNK), 0)
    tj = lax.broadcasted_iota(I32, (CHUNK, CHUNK), 1)

    for ref in (cf_ref, cb_ref, m_ref):
        ref[...] = jnp.zeros_like(ref)

    def intra(c, j, rev):
        t0 = pl.multiple_of(c * CHUNK, CHUNK)
        hl = slice(j * ML_DHP, (j + 1) * ML_DHP)
        qb = q_ref[0, pl.ds(t0, CHUNK), hl]
        kt = kt_ref[0, hl, pl.ds(t0, CHUNK)]
        vb = v_ref[0, pl.ds(t0, CHUNK), hl]
        gc = gc_ref[0, pl.ds(t0, CHUNK), :]
        gr = gr_ref[:, pl.ds(t0, CHUNK)]
        i_idx = head0 + j + (2 * ML_HEADS if rev else 0)
        f_idx = i_idx + ML_HEADS
        allowed = (tj >= ti) if rev else (tj <= ti)
        sel = (gate == f_idx).astype(BF16)
        b_rep = sum(_dot(part, sel) for part in _split3(gc))
        b_row = jnp.sum(jnp.where(sub == f_idx, gr, 0.0), axis=0, keepdims=True)
        i_row = jnp.sum(jnp.where(sub == i_idx, gr, 0.0), axis=0, keepdims=True)
        b_last = (b_rep[0:1, :] if rev else b_rep[CHUNK - 1:CHUNK, :])[:, 0:1]

        b_wide = jnp.concatenate([b_rep] * (CHUNK // LANES), axis=1)
        d = jnp.where(allowed, b_wide - b_row + i_row, NEG)
        m_in = jnp.max(d, axis=1, keepdims=True)
        sc = _dot(qb, kt) * jnp.exp(d - m_in)
        nd_in = _dot(sc.astype(BF16), vb)
        w_row = b_last - b_row + i_row
        return t0, qb, kt, vb, b_rep, b_last, m_in, nd_in, w_row

    def twice(a):
        return jnp.concatenate([a, a], axis=1)

    def update(parts, j, rev):
        t0, qb, kt, vb, b_rep, b_last, m_in, nd_in, w_row = parts
        h_ref, c_ref = (hb_ref, cb_ref) if rev else (hf_ref, cf_ref)
        hl = slice(j * ML_DHP, (j + 1) * ML_DHP)
        mrow = 2 * j + int(rev)
        m = m_ref[mrow:mrow + 1, 0:1]
        cmat = c_ref[j]
        inter = b_rep + m
        m_t = jnp.maximum(m_in, inter)
        a_in = jnp.exp(m_in - m_t)
        iexp = jnp.exp(inter - m_t)
        nd = twice(a_in) * nd_in + twice(iexp) * _dot(qb, cmat.astype(BF16))
        den = nd[:, ML_DH:ML_DH + 1]
        h_ref[pl.ds(t0, CHUNK), hl] = nd * (1.0 / jnp.maximum(jnp.abs(den), jnp.exp(-m_t[:, 0:1])))

        m_new = jnp.maximum(b_last + m, jnp.max(w_row, axis=1, keepdims=True))
        wexp = jnp.exp(w_row - m_new)
        cexp = jnp.exp(b_last + m - m_new)
        kw = (kt.astype(F32) * wexp).astype(BF16)
        c_ref[j] = cexp * cmat + _dot(kw, vb)
        m_ref[mrow:mrow + 1, :] = jnp.broadcast_to(m_new, (1, LANES))

    def step(i, carry):
        for j in range(ML_HPS):
            parts = [intra(nc - 1 - i if rev else i, j, rev) for rev in (False, True)]
            for p, rev in zip(parts, (False, True)):
                update(p, j, rev)
        return carry

    lax.fori_loop(0, nc, step, 0)

    real = lax.broadcasted_iota(I32, (1, ML_DHP), 1) < ML_DH
    tb = CHUNK

    def fin(c, carry):
        t0 = pl.multiple_of(c * tb, tb)
        for j in range(ML_HPS):
            hl = slice(j * ML_DHP, (j + 1) * ML_DHP)
            hs = jnp.where(real, hf_ref[pl.ds(t0, tb), hl] + hb_ref[pl.ds(t0, tb), hl], 0.0)
            mu = jnp.sum(hs, axis=1, keepdims=True) * (1.0 / ML_DH)
            dev = jnp.where(real, hs - mu, 0.0)
            var = jnp.sum(dev * dev, axis=1, keepdims=True) * (1.0 / ML_DH)
            hn = dev * lax.rsqrt(var + LN_EPS) * ng_ref[:, hl]
            xc = xc_ref[0, pl.ds(t0, tb), hl].astype(F32)
            z = z_ref[0, pl.ds(t0, tb), hl].astype(F32)
            y_ref[0, pl.ds(t0, tb), hl] = ((hn + sk_ref[:, hl] * xc) * _silu(z)).astype(BF16)
        return carry

    lax.fori_loop(0, s // tb, fin, 0)


def _mlstm(q, kt, v, gcol3, grow, main3, xc, ng, sk):
    b, s, _ = q.shape
    width = ML_HPS * ML_DHP
    steps = ML_HEADS // ML_HPS
    tok = pl.BlockSpec((1, s, width), lambda i, h: (i, 0, h))
    vec = pl.BlockSpec((1, width), lambda i, h: (0, h))
    return pl.pallas_call(
        functools.partial(_mlstm_kernel, s=s),
        grid=(b, steps),
        in_specs=[tok, pl.BlockSpec((1, width, s), lambda i, h: (i, h, 0)), tok,
                  pl.BlockSpec((1, s, LANES), lambda i, h: (i, 0, 0)),
                  pl.BlockSpec((LANES, s), lambda i, h: (0, i)),
                  pl.BlockSpec((1, s, width), lambda i, h: (i, 0, steps + h)),
                  tok, vec, vec],
        out_specs=tok,
        out_shape=jax.ShapeDtypeStruct((b, s, ML_WP), BF16),
        scratch_shapes=[pltpu.VMEM((s, width), F32), pltpu.VMEM((s, width), F32),
                        pltpu.VMEM((ML_HPS, ML_DHP, ML_DHP), F32), pltpu.VMEM((ML_HPS, ML_DHP, ML_DHP), F32),
                        pltpu.VMEM((SUBLANES, LANES), F32)],
        compiler_params=_cparams("parallel", "parallel"),
        name="mlstm",
    )(q, kt, v, gcol3, grow, main3, xc, ng, sk)


def _pad_heads(a, axis):
    a = jnp.moveaxis(a, axis, -1)
    lead = a.shape[:-1]
    a = a.reshape(lead + (ML_HEADS, ML_DH))
    a = jnp.pad(a, [(0, 0)] * len(lead) + [(0, 0), (0, ML_DHP - ML_DH)])
    return jnp.moveaxis(a.reshape(lead + (ML_WP,)), -1, axis)


def kernel(x, mem, mem_ln_g, mem_ln_b, w_mem_kv, router_w, router_b, na_w_in, na_rpb, ml_w_in, ml_conv_w,
           ml_conv_b, ml_w_qkv, ml_gate_b, ml_norm_g, ml_skip, w_out, ln_g, ln_b, exp_w_gate, exp_w_up,
           exp_w_down):
    b, s, d = x.shape
    n = b * s
    nm = mem.shape[1]
    row = lambda a: a.reshape(1, -1)

    mem_k, mem_v = _memkv(mem.reshape(b * nm, d), row(mem_ln_g), row(mem_ln_b), w_mem_kv.astype(BF16))
    mem_k3 = mem_k.reshape(b, nm, MEM_W)
    mem_v3 = mem_v.reshape(b, nm, MEM_W)
    rw_pad = jnp.pad(router_w, ((0, 0), (0, LANES - N_EXPERTS)))
    rw_hi = rw_pad.astype(BF16)
    rw = (rw_hi, (rw_pad - rw_hi.astype(F32)).astype(BF16))
    rb = router_b.reshape(N_EXPERTS, 1)

    x2 = x.reshape(n, d)

    h0 = _proj(x2, na_w_in[0].astype(BF16)).reshape(b, s, 3 * NA_W + MEM_W)
    y_na = _na_attention(h0, _na_bias_table(na_rpb[0]))
    wo = w_out[0].astype(BF16)
    xr = _outproj_ln(y_na.reshape(n, NA_W), h0.reshape(n, 3 * NA_W + MEM_W), 3 * NA_W // MEM_W, mem_k3, mem_v3,
                     wo[:NA_W], wo[NA_W:], x2, row(ln_g[0, 0]), row(ln_b[0, 0]))
    x2, moe_buf = _moe_ln(xr, rw, rb, exp_w_gate, exp_w_up, exp_w_down, 0, row(ln_g[0, 1]), row(ln_b[0, 1]))

    w1 = ml_w_in[0]
    w_main = jnp.concatenate([_pad_heads(w1[:, :ML_W], 1), _pad_heads(w1[:, ML_W:2 * ML_W], 1),
                              w1[:, 2 * ML_W + 4 * ML_HEADS:]], axis=1).astype(BF16)
    w_g = jnp.pad(w1[:, 2 * ML_W:2 * ML_W + 4 * ML_HEADS], ((0, 0), (0, LANES - 4 * ML_HEADS))).astype(BF16)
    gb = jnp.pad(ml_gate_b[0].reshape(4 * ML_HEADS), (0, LANES - 4 * ML_HEADS))
    main, acol, arow = _proj_gates(x2, w_main, w_g, w_g.T, gb.reshape(1, LANES), gb.reshape(LANES, 1))
    main3 = main.reshape(b, s, 2 * ML_WP + MEM_W)
    wqkv = jnp.pad(ml_w_qkv[0], ((0, 0), (0, 0), (0, ML_DHP - ML_DH), (0, ML_DHP - ML_DH))).astype(BF16)
    q, k, v, xc = _conv_qkv(main3, _pad_heads(ml_conv_w[0], 1), _pad_heads(row(ml_conv_b[0]), 1),
                            wqkv[0], jnp.swapaxes(wqkv[1], 1, 2), wqkv[2])
    y_ml = _mlstm(q, k, v, acol.reshape(b, s, LANES), arow, main3, xc,
                  _pad_heads(row(ml_norm_g[0]), 1), _pad_heads(row(ml_skip[0]), 1))
    wo = w_out[1]
    xr = _outproj_ln(y_ml.reshape(n, ML_WP), main, 2 * ML_WP // MEM_W, mem_k3, mem_v3,
                     _pad_heads(wo[:ML_W], 0).astype(BF16), wo[ML_W:].astype(BF16), x2,
                     row(ln_g[1, 0]), row(ln_b[1, 0]))
    x2, _ = _moe_ln(xr, rw, rb, exp_w_gate, exp_w_up, exp_w_down, 1, row(ln_g[1, 1]), row(ln_b[1, 1]), moe_buf)
    return x2.reshape(b, s, d)
```

```python
import functools

import numpy as np
import jax
import jax.numpy as jnp
from jax import lax
from jax.experimental import pallas as pl
from jax.experimental.pallas import tpu as pltpu

F32 = jnp.float32
BF16 = jnp.bfloat16
I32 = jnp.int32

D_MODEL = 1024
DEPTH = 2
GRID_W = 64
MEM_HEADS = 4
MEM_DH = 64
MEM_W = MEM_HEADS * MEM_DH
NA_HEADS = 12
NA_DH = 64
NA_W = NA_HEADS * NA_DH
WIN_H = 8
WIN_W = 16
ML_HEADS = 4
ML_DH = 192
ML_DHP = 256
ML_W = ML_HEADS * ML_DH
ML_WP = ML_HEADS * ML_DHP
CONV_K = 5
CHUNK = 256
N_EXPERTS = 16
N_GROUPS = 4
EXPERTS_PER_GROUP = N_EXPERTS // N_GROUPS
D_EXPERT = 512
ALPHA = (2 * DEPTH) ** 0.25
LN_EPS = 1e-5
NEG = -1e30

LANES = 128
SUBLANES = 8
ROW_CHUNKS = D_MODEL // LANES
MOE_BM = 512
MOE_TILE = 512
ML_HPS = 2
NA_ROWS_PER_STEP = 16
ROW_TILE = 1024
PLACE_UNROLL = 8
VMEM_LIMIT = 48 * 1024 * 1024


def _cparams(*sem):
    return pltpu.CompilerParams(dimension_semantics=sem, vmem_limit_bytes=VMEM_LIMIT)


def _dot(a, b):
    return jnp.dot(a, b, preferred_element_type=F32)


def _dot_nt(a, b, precision=None):
    return lax.dot_general(a, b, (((1,), (1,)), ((), ())), precision=precision,
                           preferred_element_type=F32)


def _ln(z, g, b):
    mu = jnp.mean(z, axis=-1, keepdims=True)
    zc = z - mu
    var = jnp.mean(zc * zc, axis=-1, keepdims=True)
    return zc * lax.rsqrt(var + LN_EPS) * g + b


def _silu(x):
    return x * jax.nn.sigmoid(x)


def _read_rows(ref, n):
    return jnp.concatenate([ref[pl.ds(j, n, stride=ROW_CHUNKS), :] for j in range(ROW_CHUNKS)], axis=1)


def _write_rows(ref, val, n):
    for j in range(ROW_CHUNKS):
        ref[pl.ds(j, n, stride=ROW_CHUNKS), :] = val[:, j * LANES:(j + 1) * LANES]


def _memkv_kernel(m_ref, g_ref, b_ref, w_ref, k_ref, v_ref):
    z = _ln(m_ref[...], g_ref[...], b_ref[...])
    kv = _dot(z.astype(BF16), w_ref[...])
    k_ref[...] = kv[:, :MEM_W].astype(BF16)
    v_ref[...] = kv[:, MEM_W:].astype(BF16)


def _memkv(mem2, g, b, w):
    n = mem2.shape[0]
    tm = min(ROW_TILE, n)
    return pl.pallas_call(
        _memkv_kernel,
        grid=(n // tm,),
        in_specs=[pl.BlockSpec((tm, D_MODEL), lambda i: (i, 0)),
                  pl.BlockSpec((1, D_MODEL), lambda i: (0, 0)),
                  pl.BlockSpec((1, D_MODEL), lambda i: (0, 0)),
                  pl.BlockSpec((D_MODEL, 2 * MEM_W), lambda i: (0, 0))],
        out_specs=[pl.BlockSpec((tm, MEM_W), lambda i: (i, 0)),
                   pl.BlockSpec((tm, MEM_W), lambda i: (i, 0))],
        out_shape=[jax.ShapeDtypeStruct((n, MEM_W), BF16)] * 2,
        compiler_params=_cparams("parallel"),
        name="memkv",
    )(mem2, g, b, w)


def _proj_kernel(x_ref, w_ref, o_ref):
    o_ref[...] = _dot(x_ref[...].astype(BF16), w_ref[...]).astype(o_ref.dtype)


def _proj(x2, w, tm=ROW_TILE):
    n, k = x2.shape
    nout = w.shape[1]
    return pl.pallas_call(
        _proj_kernel,
        grid=(n // tm,),
        in_specs=[pl.BlockSpec((tm, k), lambda i: (i, 0)),
                  pl.BlockSpec((k, nout), lambda i: (0, 0))],
        out_specs=pl.BlockSpec((tm, nout), lambda i: (i, 0)),
        out_shape=jax.ShapeDtypeStruct((n, nout), BF16),
        compiler_params=_cparams("parallel"),
        name="in_proj",
    )(x2, w)


def _split3(x):
    hi = x.astype(BF16)
    r1 = x - hi.astype(F32)
    mid = r1.astype(BF16)
    lo = (r1 - mid.astype(F32)).astype(BF16)
    return hi, mid, lo


def _proj_gates_kernel(x_ref, w_ref, wg_ref, wgt_ref, gbc_ref, gbr_ref, o_ref, g_ref, gt_ref, *, tm):
    xb = x_ref[...].astype(BF16)
    o_ref[...] = _dot(xb, w_ref[...]).astype(BF16)
    gcol = _dot(xb, wg_ref[...]) + gbc_ref[...]
    grow = _dot_nt(wgt_ref[...], xb) + gbr_ref[...]
    lane = lax.broadcasted_iota(I32, (1, LANES), 1)
    sub = lax.broadcasted_iota(I32, (LANES, 1), 0)
    ti = lax.broadcasted_iota(I32, (CHUNK, CHUNK), 0)
    tj = lax.broadcasted_iota(I32, (CHUNK, CHUNK), 1)
    lower = (tj <= ti).astype(BF16)
    upper = (ti <= tj).astype(BF16)

    def pick(idx, pre, suf, raw):
        fwd = jnp.logical_and(idx >= ML_HEADS, idx < 2 * ML_HEADS)
        bwd = jnp.logical_and(idx >= 3 * ML_HEADS, idx < 4 * ML_HEADS)
        return jnp.where(fwd, pre, jnp.where(bwd, suf, raw))

    for c in range(tm // CHUNK):
        tc = slice(c * CHUNK, (c + 1) * CHUNK)
        g = gcol[tc, :]
        ls = jax.nn.log_sigmoid(g)
        pre = sum(_dot(lower, part) for part in _split3(ls))
        suf = jnp.sum(ls, axis=0, keepdims=True) - pre + ls
        g_ref[tc, :] = pick(lane, pre, suf, g)
        g = grow[:, tc]
        ls = jax.nn.log_sigmoid(g)
        pre = sum(_dot(part, upper) for part in _split3(ls))
        suf = jnp.sum(ls, axis=1, keepdims=True) - pre + ls
        gt_ref[:, tc] = pick(sub, pre, suf, g)


def _proj_gates(x2, w, wg, wgt, gbc, gbr, tm=ROW_TILE):
    n, k = x2.shape
    nout = w.shape[1]
    return pl.pallas_call(
        functools.partial(_proj_gates_kernel, tm=tm),
        grid=(n // tm,),
        in_specs=[pl.BlockSpec((tm, k), lambda i: (i, 0)),
                  pl.BlockSpec((k, nout), lambda i: (0, 0)),
                  pl.BlockSpec((k, LANES), lambda i: (0, 0)),
                  pl.BlockSpec((LANES, k), lambda i: (0, 0)),
                  pl.BlockSpec((1, LANES), lambda i: (0, 0)),
                  pl.BlockSpec((LANES, 1), lambda i: (0, 0))],
        out_specs=[pl.BlockSpec((tm, nout), lambda i: (i, 0)),
                   pl.BlockSpec((tm, LANES), lambda i: (i, 0)),
                   pl.BlockSpec((LANES, tm), lambda i: (0, i))],
        out_shape=[jax.ShapeDtypeStruct((n, nout), BF16),
                   jax.ShapeDtypeStruct((n, LANES), F32),
                   jax.ShapeDtypeStruct((LANES, n), F32)],
        compiler_params=_cparams("parallel"),
        name="in_proj_gates",
    )(x2, w, wg, wgt, gbc, gbr)


def _na_kernel(q_ref, k_ref, v_ref, tbl_ref, o_ref, *, rows):
    lane = lax.broadcasted_iota(I32, (1, LANES), 1)
    first = lane < NA_DH
    nkeys = WIN_H * GRID_W

    def rows_step(i, carry):
        rr = [i * NA_ROWS_PER_STEP + u for u in range(NA_ROWS_PER_STEP)]
        rss = [jnp.clip(r - WIN_H // 2, 0, rows - WIN_H) for r in rr]
        scores = []
        for r, rs in zip(rr, rss):
            q = q_ref[0, pl.ds(pl.multiple_of(r * GRID_W, GRID_W), GRID_W), :]
            q = q * jnp.asarray(NA_DH ** -0.5, BF16)
            q2 = jnp.concatenate([jnp.where(first, q, jnp.zeros_like(q)),
                                  jnp.where(first, jnp.zeros_like(q), q)], axis=0)
            k = k_ref[0, pl.ds(pl.multiple_of(rs * GRID_W, GRID_W), nkeys), :]
            dr0 = rs - r + WIN_H - 1
            bias = jnp.concatenate(
                [jnp.concatenate([tbl_ref[0, half, dr0 + 2 * m] for m in range(WIN_H // 2)], axis=1)
                 for half in range(2)], axis=0)
            scores.append(_dot_nt(q2, k) + bias)
        probs = []
        for s in scores:
            p = jnp.exp(s - jnp.max(s, axis=-1, keepdims=True))
            probs.append((p.astype(BF16), jnp.sum(p, axis=-1, keepdims=True)))
        for r, rs, (p, l) in zip(rr, rss, probs):
            v = v_ref[0, pl.ds(pl.multiple_of(rs * GRID_W, GRID_W), nkeys), :]
            o = _dot(p, v) / l
            o = jnp.where(first, o[:GRID_W], o[GRID_W:])
            o_ref[0, pl.ds(pl.multiple_of(r * GRID_W, GRID_W), GRID_W), :] = o.astype(o_ref.dtype)
        return carry

    lax.fori_loop(0, rows // NA_ROWS_PER_STEP, rows_step, 0)


def _na_bias_table(rpb):
    qc = np.arange(GRID_W)[:, None]
    kc = np.arange(GRID_W)[None, :]
    cs = np.clip(qc - WIN_W // 2, 0, GRID_W - WIN_W)
    col_in = (kc >= cs) & (kc < cs + WIN_W)
    side = GRID_W - WIN_W
    wide = jnp.pad(rpb, ((0, 0), (0, 0), (side, side)))
    t = jnp.stack([wide[:, :, GRID_W - 1 - q:2 * GRID_W - 1 - q] for q in range(GRID_W)], axis=2)
    t = jnp.where(col_in, t, NEG).astype(F32)
    t2 = jnp.concatenate([t[:, :-1], t[:, 1:]], axis=-1)
    return t2.reshape(NA_HEADS // 2, 2, 2 * WIN_H - 2, GRID_W, 2 * GRID_W)


def _na_attention(h3, tbl):
    b, s, _ = h3.shape
    rows = s // GRID_W
    npair = NA_HEADS // 2
    return pl.pallas_call(
        functools.partial(_na_kernel, rows=rows),
        grid=(b, npair),
        in_specs=[pl.BlockSpec((1, s, LANES), lambda i, p: (i, 0, p)),
                  pl.BlockSpec((1, s, LANES), lambda i, p: (i, 0, npair + p)),
                  pl.BlockSpec((1, s, LANES), lambda i, p: (i, 0, 2 * npair + p)),
                  pl.BlockSpec((1, 2, 2 * WIN_H - 2, GRID_W, 2 * GRID_W), lambda i, p: (p, 0, 0, 0, 0))],
        out_specs=pl.BlockSpec((1, s, LANES), lambda i, p: (i, 0, p)),
        out_shape=jax.ShapeDtypeStruct((b, s, NA_W), BF16),
        compiler_params=_cparams("parallel", "parallel"),
        name="na_attention",
    )(h3, h3, h3, tbl)


def _outproj_ln_kernel(ya_ref, qm_ref, mk_ref, mv_ref, wa_ref, wm_ref, x_ref, g_ref, b_ref, or_ref, *, tm):
    lane = lax.broadcasted_iota(I32, (1, LANES), 1)
    first = lane < MEM_DH
    q = qm_ref[...] * jnp.asarray(MEM_DH ** -0.5, BF16)
    cols = [slice(p * LANES, (p + 1) * LANES) for p in range(MEM_HEADS // 2)]
    scores = []
    for c in cols:
        qp = q[:, c]
        q2 = jnp.concatenate([jnp.where(first, qp, jnp.zeros_like(qp)),
                              jnp.where(first, jnp.zeros_like(qp), qp)], axis=0)
        scores.append(_dot_nt(q2, mk_ref[0, :, c]))
    probs = []
    for s in scores:
        p = jnp.exp(s - jnp.max(s, axis=-1, keepdims=True))
        probs.append((p.astype(BF16), jnp.sum(p, axis=-1, keepdims=True)))
    outs = []
    for c, (p, l) in zip(cols, probs):
        o = _dot(p, mv_ref[0, :, c]) / l
        outs.append(jnp.where(first, o[:tm], o[tm:]))
    ym = jnp.concatenate(outs, axis=1).astype(BF16)
    acc = _dot(ya_ref[...], wa_ref[...]) + _dot(ym, wm_ref[...])
    _write_rows(or_ref, _ln(ALPHA * x_ref[...] + acc, g_ref[...], b_ref[...]), tm)


def _outproj_ln(ya, h2, qm_block, mem_k3, mem_v3, wa, wm, x2, g, b, tm=ROW_TILE):
    n = x2.shape[0]
    ka = ya.shape[1]
    nb, nm, _ = mem_k3.shape
    per_batch = n // nb // tm
    full = lambda shape: pl.BlockSpec(shape, lambda i: (0,) * len(shape))
    return pl.pallas_call(
        functools.partial(_outproj_ln_kernel, tm=tm),
        grid=(n // tm,),
        in_specs=[pl.BlockSpec((tm, ka), lambda i: (i, 0)),
                  pl.BlockSpec((tm, MEM_W), lambda i: (i, qm_block)),
                  pl.BlockSpec((1, nm, MEM_W), lambda i: (i // per_batch, 0, 0)),
                  pl.BlockSpec((1, nm, MEM_W), lambda i: (i // per_batch, 0, 0)),
                  full((ka, D_MODEL)), full((MEM_W, D_MODEL)),
                  pl.BlockSpec((tm, D_MODEL), lambda i: (i, 0)),
                  full((1, D_MODEL)), full((1, D_MODEL))],
        out_specs=pl.BlockSpec((tm * ROW_CHUNKS, LANES), lambda i: (i, 0)),
        out_shape=jax.ShapeDtypeStruct((n * ROW_CHUNKS, LANES), F32),
        compiler_params=_cparams("parallel"),
        name="outproj_ln",
    )(ya, h2, mem_k3, mem_v3, wa, wm, x2, g, b)


def _router_kernel(x_ref, rwh_ref, rwl_ref, rb_ref, lpos_ref, w_ref, cnt_ref, tcnt_ref, toff_ref, tbef_ref, *, tm):
    @pl.when(pl.program_id(0) == 0)
    def _():
        cnt_ref[...] = jnp.zeros_like(cnt_ref)

    x = _read_rows(x_ref, tm)
    xh = x.astype(BF16)
    xl = (x - xh.astype(F32)).astype(BF16)
    logits_t = _dot(xh, rwh_ref[...]) + (_dot(xh, rwl_ref[...]) + _dot(xl, rwh_ref[...]))
    logits = logits_t.T[:N_EXPERTS]
    scores = jax.nn.sigmoid(logits)
    biased = scores + rb_ref[...]
    bv = [biased[e:e + 1, :] for e in range(N_EXPERTS)]
    sv = [scores[e:e + 1, :] for e in range(N_EXPERTS)]

    grp = []
    for g in range(N_GROUPS):
        m = bv[g * EXPERTS_PER_GROUP:(g + 1) * EXPERTS_PER_GROUP]
        best = None
        for a in range(EXPERTS_PER_GROUP):
            for c in range(a + 1, EXPERTS_PER_GROUP):
                pair = m[a] + m[c]
                best = pair if best is None else jnp.maximum(best, pair)
        grp.append(best)
    gsel = jnp.zeros((1, tm), I32)
    gbest = grp[0]
    for g in range(1, N_GROUPS):
        better = grp[g] > gbest
        gsel = jnp.where(better, g, gsel)
        gbest = jnp.where(better, grp[g], gbest)

    def pick(vals, j):
        out = vals[j]
        for g in range(1, N_GROUPS):
            out = jnp.where(gsel == g, vals[g * EXPERTS_PER_GROUP + j], out)
        return out

    cb = [pick(bv, j) for j in range(EXPERTS_PER_GROUP)]
    cs = [pick(sv, j) for j in range(EXPERTS_PER_GROUP)]
    i1 = jnp.zeros((1, tm), I32)
    m1 = cb[0]
    s1 = cs[0]
    for j in range(1, EXPERTS_PER_GROUP):
        gt = cb[j] > m1
        i1 = jnp.where(gt, j, i1)
        m1 = jnp.where(gt, cb[j], m1)
        s1 = jnp.where(gt, cs[j], s1)
    i2 = jnp.zeros((1, tm), I32)
    m2 = jnp.full((1, tm), -jnp.inf, F32)
    s2 = jnp.zeros((1, tm), F32)
    for j in range(EXPERTS_PER_GROUP):
        ok = jnp.logical_and(i1 != j, cb[j] > m2)
        i2 = jnp.where(ok, j, i2)
        m2 = jnp.where(ok, cb[j], m2)
        s2 = jnp.where(ok, cs[j], s2)
    e1 = gsel * EXPERTS_PER_GROUP + i1
    e2 = gsel * EXPERTS_PER_GROUP + i2
    tot = s1 + s2
    w_ref[...] = jnp.concatenate([s1 / tot, s2 / tot], axis=0)

    i = pl.program_id(0)
    eio = lax.broadcasted_iota(I32, (N_EXPERTS, tm), 0)
    oh1 = eio == e1
    oh2 = eio == e2
    ohs = jnp.logical_or(oh1, oh2).astype(F32)
    before = (lax.broadcasted_iota(I32, (tm, tm), 0) < lax.broadcasted_iota(I32, (tm, tm), 1))
    pre = _dot(ohs.astype(BF16), before.astype(BF16))
    tile_cnt = jnp.sum(ohs, axis=1, keepdims=True)
    offs = []
    acc = jnp.zeros((1, 1), F32)
    for e in range(N_EXPERTS):
        offs.append(acc)
        acc = acc + tile_cnt[e:e + 1, :]
    tile_off = jnp.concatenate(offs, axis=0)
    pos = tile_off + pre
    p1 = jnp.sum(jnp.where(oh1, pos, 0.0), axis=0, keepdims=True)
    p2 = jnp.sum(jnp.where(oh2, pos, 0.0), axis=0, keepdims=True)
    lpos_ref[...] = jnp.concatenate([p1, p2], axis=0).astype(I32) * ROW_CHUNKS

    @pl.when(i == 0)
    def _():
        for ref in (tcnt_ref, toff_ref, tbef_ref):
            ref[...] = jnp.zeros_like(ref)

    here = lax.broadcasted_iota(I32, (1, LANES), 1) == i
    tcnt_ref[...] = jnp.where(here, tile_cnt, tcnt_ref[...])
    toff_ref[...] = jnp.where(here, tile_off, toff_ref[...])
    tbef_ref[...] = jnp.where(here, cnt_ref[:, 0:1], tbef_ref[...])
    cnt_ref[...] += tile_cnt


def _router(xr, rw, rb, tm):
    n = xr.shape[0] // ROW_CHUNKS
    assert n // tm <= LANES
    table = pl.BlockSpec((N_EXPERTS, LANES), lambda i: (0, 0))
    return pl.pallas_call(
        functools.partial(_router_kernel, tm=tm),
        grid=(n // tm,),
        in_specs=[pl.BlockSpec((tm * ROW_CHUNKS, LANES), lambda i: (i, 0)),
                  pl.BlockSpec((D_MODEL, LANES), lambda i: (0, 0)),
                  pl.BlockSpec((D_MODEL, LANES), lambda i: (0, 0)),
                  pl.BlockSpec((N_EXPERTS, 1), lambda i: (0, 0))],
        out_specs=[pl.BlockSpec((2, tm), lambda i: (0, i)),
                   pl.BlockSpec((2, tm), lambda i: (0, i)),
                   table, table, table, table],
        out_shape=[jax.ShapeDtypeStruct((2, n), I32),
                   jax.ShapeDtypeStruct((2, n), F32)]
                  + [jax.ShapeDtypeStruct((N_EXPERTS, LANES), F32)] * 4,
        compiler_params=_cparams("arbitrary"),
        name="router",
    )(xr, rw[0], rw[1], rb)


def _plan_kernel(cnt_ref, tbef_ref, meta_ref, rstart_ref, *, nbl):
    shift = MOE_BM.bit_length() - 1
    cnt = cnt_ref[...].astype(I32)
    padded = ((cnt + (MOE_BM - 1)) >> shift) << shift
    starts = []
    acc = jnp.zeros((1, LANES), I32)
    for e in range(N_EXPERTS):
        starts.append(acc)
        acc = acc + padded[e:e + 1, :]
    pad_start = jnp.concatenate(starts, axis=0)
    pad_end = pad_start + padded
    rstart_ref[...] = pad_start + tbef_ref[...].astype(I32)
    blk0 = lax.broadcasted_iota(I32, (N_EXPERTS, nbl), 1) * MOE_BM
    block_e = jnp.sum((pad_end[:, 0:1] <= blk0).astype(I32), axis=0, keepdims=True)
    block_e = jnp.minimum(block_e, N_EXPERTS - 1)
    n_used = jnp.broadcast_to(acc[:, 0:1] >> shift, (1, nbl))
    diag = lax.broadcasted_iota(I32, (N_EXPERTS, nbl), 0) == lax.broadcasted_iota(I32, (N_EXPERTS, nbl), 1)
    fill_lo = jnp.sum(jnp.where(diag, (pad_start + cnt)[:, 0:1], 0), axis=0, keepdims=True)
    fill_hi = jnp.sum(jnp.where(diag, pad_end[:, 0:1], 0), axis=0, keepdims=True)
    meta_ref[...] = jnp.concatenate([block_e, n_used, fill_lo, fill_hi, jnp.zeros((SUBLANES - 4, nbl), I32)],
                                    axis=0)


def _plan(cnt, tbef, n_blocks):
    nbl = -(-n_blocks // LANES) * LANES
    table = pl.BlockSpec((N_EXPERTS, LANES), lambda i: (0, 0))
    return pl.pallas_call(
        functools.partial(_plan_kernel, nbl=nbl),
        grid=(1,),
        in_specs=[table, table],
        out_specs=[pl.BlockSpec((SUBLANES, nbl), lambda i: (0, 0)), table],
        out_shape=[jax.ShapeDtypeStruct((SUBLANES, nbl), I32),
                   jax.ShapeDtypeStruct((N_EXPERTS, LANES), I32)],
        compiler_params=_cparams("arbitrary"),
        name="moe_plan",
    )(cnt, tbef)


def _rows(ref, row, nrows):
    return ref.at[pl.ds(pl.multiple_of(row * ROW_CHUNKS, ROW_CHUNKS), nrows * ROW_CHUNKS), :]


def _rows_wait(src_hbm, buf, sem):
    pltpu.make_async_copy(src_hbm.at[pl.ds(0, buf.shape[0]), :], buf, sem).wait()


def _copy_pieces(src, src_row, dst, dst_row, count, max_rows, sem, wait=False):
    bit = max_rows.bit_length() - 1
    while bit >= 0:
        size = 1 << bit
        done = (count >> (bit + 1)) << (bit + 1)

        @pl.when(((count >> bit) & 1) == 1)
        def _():
            cp = pltpu.make_async_copy(_rows(src, src_row + done, size), _rows(dst, dst_row + done, size), sem)
            cp.start()
            if wait:
                cp.wait()

        bit -= 1


def _tile_runs(tcnt_ref, toff_ref, rstart_ref, tile, buf, hbm, sem, *, to_hbm, tm):
    def per_expert(e, carry):
        k = tile * N_EXPERTS + e
        if to_hbm:
            _copy_pieces(buf, toff_ref[k], hbm, rstart_ref[k], tcnt_ref[k], tm, sem)
        else:
            _copy_pieces(hbm, rstart_ref[k], buf, toff_ref[k], tcnt_ref[k], tm, sem)
        return carry

    lax.fori_loop(0, N_EXPERTS, per_expert, 0)


def _dispatch_kernel(lpos_ref, tcnt_ref, toff_ref, rstart_ref, flo_ref, fhi_ref, nu_ref, x_ref, old_hbm, xs_hbm,
                     s0, s1, zbuf, sem, zsem, *, n, tm, n_blocks, reuse):
    del old_hbm
    i = pl.program_id(0)
    nt = pl.num_programs(0)
    bufs = (s0, s1)
    unroll = PLACE_UNROLL

    for slot in range(2):
        @pl.when(i % 2 == slot)
        def _():
            buf = bufs[slot]

            @pl.when(i >= 2)
            def _():
                _rows_wait(xs_hbm, buf, sem.at[slot])

            def place(c, carry):
                tok = i * tm + c * unroll
                src = pl.multiple_of(c * (unroll * ROW_CHUNKS), unroll * ROW_CHUNKS)
                for u in range(unroll):
                    v = x_ref[pl.ds(src + u * ROW_CHUNKS, ROW_CHUNKS), :]
                    for k in range(2):
                        p = lpos_ref[k * n + tok + u]
                        buf[pl.ds(pl.multiple_of(p, ROW_CHUNKS), ROW_CHUNKS), :] = v
                return carry

            lax.fori_loop(0, tm // unroll, place, 0)
            _tile_runs(tcnt_ref, toff_ref, rstart_ref, i, buf, xs_hbm, sem.at[slot], to_hbm=True, tm=tm)

    zero_rows = n_blocks * MOE_BM - 2 * n
    if not reuse:
        @pl.when(i == 0)
        def _():
            zbuf[...] = jnp.zeros_like(zbuf)
            for e in range(N_EXPERTS):
                _copy_pieces(zbuf, 0, xs_hbm, flo_ref[e], fhi_ref[e] - flo_ref[e], MOE_BM // 2, zsem)

            def zero_block(j, carry):
                pltpu.make_async_copy(zbuf, _rows(xs_hbm, j * MOE_BM, MOE_BM), zsem).start()
                return carry

            lax.fori_loop(nu_ref[0], n_blocks, zero_block, 0)

    @pl.when(i == nt - 1)
    def _():
        for slot in range(2):
            @pl.when(nt > slot)
            def _():
                _rows_wait(xs_hbm, bufs[slot], sem.at[slot])

        if not reuse:
            pltpu.make_async_copy(_rows(xs_hbm, 0, zero_rows), _rows(xs_hbm, 0, zero_rows), zsem).wait()


def _dispatch(lpos_flat, tcnt, toff, rstart, fill_lo, fill_hi, n_used, xr, n_blocks, tm, old=None):
    n = xr.shape[0] // ROW_CHUNKS
    reuse = old is not None
    if not reuse:
        old = jnp.zeros((SUBLANES, LANES), F32)
    return pl.pallas_call(
        functools.partial(_dispatch_kernel, n=n, tm=tm, n_blocks=n_blocks, reuse=reuse),
        grid_spec=pltpu.PrefetchScalarGridSpec(
            num_scalar_prefetch=7,
            grid=(n // tm,),
            in_specs=[pl.BlockSpec((tm * ROW_CHUNKS, LANES), lambda i, *_: (i, 0)),
                      pl.BlockSpec(memory_space=pl.ANY)],
            out_specs=pl.BlockSpec(memory_space=pl.ANY),
            scratch_shapes=[pltpu.VMEM((2 * tm * ROW_CHUNKS, LANES), F32),
                            pltpu.VMEM((2 * tm * ROW_CHUNKS, LANES), F32),
                            pltpu.VMEM((MOE_BM * ROW_CHUNKS, LANES), F32),
                            pltpu.SemaphoreType.DMA((2,)),
                            pltpu.SemaphoreType.DMA(())]),
        out_shape=jax.ShapeDtypeStruct((n_blocks * MOE_BM * ROW_CHUNKS, LANES), F32),
        input_output_aliases={8: 0} if reuse else {},
        compiler_params=_cparams("arbitrary"),
        name="moe_dispatch",
    )(lpos_flat, tcnt, toff, rstart, fill_lo, fill_hi, n_used, xr, old)


def _experts_kernel(be_ref, nu_ref, xs_ref, wg_ref, wu_ref, wd_ref, y_ref, wgb, wub, wdb):
    j = pl.program_id(0)
    used = j < nu_ref[0]

    @pl.when(jnp.logical_and(used, jnp.logical_or(j == 0, be_ref[j] != be_ref[jnp.maximum(j - 1, 0)])))
    def _():
        wgb[...] = wg_ref[0, 0].astype(BF16)
        wub[...] = wu_ref[0, 0].astype(BF16)
        wdb[...] = wd_ref[0, 0].astype(BF16)

    @pl.when(used)
    def _():
        x = _read_rows(xs_ref, MOE_BM).astype(BF16)
        h = _silu(_dot(x, wgb[...])) * _dot(x, wub[...])
        _write_rows(y_ref, _dot(h.astype(BF16), wdb[...]), MOE_BM)


def _experts(block_e, n_used, xs, wg, wu, wd, layer):
    n_blocks = block_e.shape[0]

    def last_used(j, nu):
        return jnp.minimum(j, nu[0] - 1)

    def wblk(j, be, nu):
        return (layer, be[last_used(j, nu)], 0, 0)

    return pl.pallas_call(
        _experts_kernel,
        grid_spec=pltpu.PrefetchScalarGridSpec(
            num_scalar_prefetch=2,
            grid=(n_blocks,),
            in_specs=[pl.BlockSpec((MOE_BM * ROW_CHUNKS, LANES), lambda j, be, nu: (last_used(j, nu), 0)),
                      pl.BlockSpec((1, 1, D_MODEL, D_EXPERT), wblk),
                      pl.BlockSpec((1, 1, D_MODEL, D_EXPERT), wblk),
                      pl.BlockSpec((1, 1, D_EXPERT, D_MODEL), wblk)],
            out_specs=pl.BlockSpec((MOE_BM * ROW_CHUNKS, LANES), lambda j, be, nu: (last_used(j, nu), 0)),
            scratch_shapes=[pltpu.VMEM((D_MODEL, D_EXPERT), BF16), pltpu.VMEM((D_MODEL, D_EXPERT), BF16),
                            pltpu.VMEM((D_EXPERT, D_MODEL), BF16)]),
        out_shape=jax.ShapeDtypeStruct(xs.shape, F32),
        input_output_aliases={2: 0},
        compiler_params=_cparams("arbitrary"),
        name="moe_experts",
    )(block_e, n_used, xs, wg, wu, wd)


def _combine_ln_kernel(lpos_ref, tcnt_ref, toff_ref, rstart_ref, y_hbm, x_ref, w1_ref, w2_ref, g_ref, b_ref, o_ref,
                       r0, r1, u1, u2, sem, *, n, tm):
    i = pl.program_id(0)
    nt = pl.num_programs(0)
    bufs = (r0, r1)
    unroll = PLACE_UNROLL

    def fetch(tile, slot):
        _tile_runs(tcnt_ref, toff_ref, rstart_ref, tile, bufs[slot], y_hbm, sem.at[slot], to_hbm=False, tm=tm)

    @pl.when(i == 0)
    def _():
        fetch(0, 0)

    for slot in range(2):
        @pl.when(i % 2 == slot)
        def _():
            @pl.when(i + 1 < nt)
            def _():
                fetch(i + 1, 1 - slot)

            buf = bufs[slot]
            _rows_wait(y_hbm, buf, sem.at[slot])

            def place(c, carry):
                tok = i * tm + c * unroll
                dst0 = pl.multiple_of(c * (unroll * ROW_CHUNKS), unroll * ROW_CHUNKS)
                for u in range(unroll):
                    dst = pl.ds(dst0 + u * ROW_CHUNKS, ROW_CHUNKS)
                    for k, out in enumerate((u1, u2)):
                        p = lpos_ref[k * n + tok + u]
                        out[dst, :] = buf[pl.ds(pl.multiple_of(p, ROW_CHUNKS), ROW_CHUNKS), :]
                return carry

            lax.fori_loop(0, tm // unroll, place, 0)
            moe = w1_ref[...] * _read_rows(u1, tm) + w2_ref[...] * _read_rows(u2, tm)
            o_ref[...] = _ln(ALPHA * _read_rows(x_ref, tm) + moe, g_ref[...], b_ref[...])


def _combine_ln(lpos_flat, tcnt, toff, rstart, y, xr, w1, w2, g, b, tm):
    n = xr.shape[0] // ROW_CHUNKS
    return pl.pallas_call(
        functools.partial(_combine_ln_kernel, n=n, tm=tm),
        grid_spec=pltpu.PrefetchScalarGridSpec(
            num_scalar_prefetch=4,
            grid=(n // tm,),
            in_specs=[pl.BlockSpec(memory_space=pl.ANY),
                      pl.BlockSpec((tm * ROW_CHUNKS, LANES), lambda i, *_: (i, 0)),
                      pl.BlockSpec((tm, 1), lambda i, *_: (i, 0)),
                      pl.BlockSpec((tm, 1), lambda i, *_: (i, 0)),
                      pl.BlockSpec((1, D_MODEL), lambda i, *_: (0, 0)),
                      pl.BlockSpec((1, D_MODEL), lambda i, *_: (0, 0))],
            out_specs=pl.BlockSpec((tm, D_MODEL), lambda i, *_: (i, 0)),
            scratch_shapes=[pltpu.VMEM((2 * tm * ROW_CHUNKS, LANES), F32)] * 2
                           + [pltpu.VMEM((tm * ROW_CHUNKS, LANES), F32)] * 2
                           + [pltpu.SemaphoreType.DMA((2,))]),
        out_shape=jax.ShapeDtypeStruct((n, D_MODEL), F32),
        compiler_params=_cparams("arbitrary"),
        name="moe_combine_ln",
    )(lpos_flat, tcnt, toff, rstart, y, xr, w1, w2, g, b)


def _moe_ln(xr, rw, rb, wg, wu, wd, layer, g, b, old=None):
    n = xr.shape[0] // ROW_CHUNKS
    n_blocks = (2 * n) // MOE_BM + N_EXPERTS
    tm = MOE_TILE
    nt = n // tm
    lpos, w, cnt, tcnt, toff, tbef = _router(xr, rw, rb, tm)
    meta, rstart = _plan(cnt, tbef, n_blocks)
    block_e = meta[0, :n_blocks]
    n_used = meta[1, :1]

    def per_tile(table):
        return table[:, :nt].T.reshape(nt * N_EXPERTS).astype(I32)

    lpos_flat = lpos.reshape(2 * n)
    tcnt, toff, rstart = per_tile(tcnt), per_tile(toff), per_tile(rstart)
    xs = _dispatch(lpos_flat, tcnt, toff, rstart, meta[2, :N_EXPERTS], meta[3, :N_EXPERTS], n_used, xr,
                   n_blocks, tm, old)
    y = _experts(block_e, n_used, xs, wg, wu, wd, layer)
    out = _combine_ln(lpos_flat, tcnt, toff, rstart, y, xr, w[0].reshape(n, 1), w[1].reshape(n, 1), g, b, tm)
    return out, y


def _conv_qkv_kernel(xm_ref, cw_ref, cb_ref, wq_ref, wk_ref, wv_ref, q_ref, k_ref, v_ref, xc_ref, *, s):
    xm_b = xm_ref[0]
    xm = xm_b.astype(F32)
    cw = cw_ref[...]
    row = lax.broadcasted_iota(I32, (s, 1), 0)
    half = CONV_K // 2
    acc = cb_ref[...] + xm * cw[half:half + 1, :]
    for sh in range(1, half + 1):
        past = jnp.where(row >= sh, pltpu.roll(xm, sh, axis=0), 0.0)
        acc = acc + past * cw[half - sh:half - sh + 1, :]
        nxt = jnp.where(row < s - sh, pltpu.roll(xm, s - sh, axis=0), 0.0)
        acc = acc + nxt * cw[half + sh:half + sh + 1, :]
    xc = _silu(acc).astype(BF16)
    xc_ref[0] = xc
    q_ref[0] = _dot(xc, wq_ref[0]).astype(BF16)
    k_ref[0] = (_dot_nt(wk_ref[0], xc) * (ML_DH ** -0.5)).astype(BF16)
    v = _dot(xm_b, wv_ref[0])
    ones_lane = lax.broadcasted_iota(I32, (1, ML_DHP), 1) == ML_DH
    v_ref[0] = jnp.where(ones_lane, 1.0, v).astype(BF16)


def _conv_qkv(main3, cw, cb, wq, wk_t, wv):
    b, s, _ = main3.shape
    tok = pl.BlockSpec((1, s, ML_DHP), lambda i, h: (i, 0, h))
    wspec = pl.BlockSpec((1, ML_DHP, ML_DHP), lambda i, h: (h, 0, 0))
    tok_shape = jax.ShapeDtypeStruct((b, s, ML_WP), BF16)
    return pl.pallas_call(
        functools.partial(_conv_qkv_kernel, s=s),
        grid=(b, ML_HEADS),
        in_specs=[tok,
                  pl.BlockSpec((CONV_K, ML_DHP), lambda i, h: (0, h)),
                  pl.BlockSpec((1, ML_DHP), lambda i, h: (0, h)),
                  wspec, wspec, wspec],
        out_specs=[tok, pl.BlockSpec((1, ML_DHP, s), lambda i, h: (i, h, 0)), tok, tok],
        out_shape=[tok_shape, jax.ShapeDtypeStruct((b, ML_WP, s), BF16), tok_shape, tok_shape],
        compiler_params=_cparams("parallel", "parallel"),
        name="conv_qkv",
    )(main3, cw, cb, wq, wk_t, wv)


def _mlstm_kernel(q_ref, kt_ref, v_ref, gc_ref, gr_ref, z_ref, xc_ref, ng_ref, sk_ref,
                  y_ref, hf_ref, hb_ref, cf_ref, cb_ref, m_ref, *, s):
    head0 = pl.program_id(1) * ML_HPS
    nc = s // CHUNK
    sub = lax.broadcasted_iota(I32, (LANES, 1), 0)
    gate = lax.broadcasted_iota(I32, (LANES, LANES), 0)
    ti = lax.broadcasted_iota(I32, (CHUNK, CHUNK), 0)
    tj = lax.broadcasted_iota(I32, (CHUNK, CHUNK), 1)

    for ref in (cf_ref, cb_ref, m_ref):
        ref[...] = jnp.zeros_like(ref)

    def intra(c, j, rev):
        t0 = pl.multiple_of(c * CHUNK, CHUNK)
        hl = slice(j * ML_DHP, (j + 1) * ML_DHP)
        qb = q_ref[0, pl.ds(t0, CHUNK), hl]
        kt = kt_ref[0, hl, pl.ds(t0, CHUNK)]
        vb = v_ref[0, pl.ds(t0, CHUNK), hl]
        gc = gc_ref[0, pl.ds(t0, CHUNK), :]
        gr = gr_ref[:, pl.ds(t0, CHUNK)]
        i_idx = head0 + j + (2 * ML_HEADS if rev else 0)
        f_idx = i_idx + ML_HEADS
        allowed = (tj >= ti) if rev else (tj <= ti)
        sel = (gate == f_idx).astype(BF16)
        b_rep = sum(_dot(part, sel) for part in _split3(gc))
        b_row = jnp.sum(jnp.where(sub == f_idx, gr, 0.0), axis=0, keepdims=True)
        i_row = jnp.sum(jnp.where(sub == i_idx, gr, 0.0), axis=0, keepdims=True)
        b_last = (b_rep[0:1, :] if rev else b_rep[CHUNK - 1:CHUNK, :])[:, 0:1]

        b_wide = jnp.concatenate([b_rep] * (CHUNK // LANES), axis=1)
        d = jnp.where(allowed, b_wide - b_row + i_row, NEG)
        m_in = jnp.max(d, axis=1, keepdims=True)
        sc = _dot(qb, kt) * jnp.exp(d - m_in)
        nd_in = _dot(sc.astype(BF16), vb)
        w_row = b_last - b_row + i_row
        return t0, qb, kt, vb, b_rep, b_last, m_in, nd_in, w_row

    def twice(a):
        return jnp.concatenate([a, a], axis=1)

    def update(parts, j, rev):
        t0, qb, kt, vb, b_rep, b_last, m_in, nd_in, w_row = parts
        h_ref, c_ref = (hb_ref, cb_ref) if rev else (hf_ref, cf_ref)
        hl = slice(j * ML_DHP, (j + 1) * ML_DHP)
        mrow = 2 * j + int(rev)
        m = m_ref[mrow:mrow + 1, 0:1]
        cmat = c_ref[j]
        inter = b_rep + m
        m_t = jnp.maximum(m_in, inter)
        a_in = jnp.exp(m_in - m_t)
        iexp = jnp.exp(inter - m_t)
        nd = twice(a_in) * nd_in + twice(iexp) * _dot(qb, cmat.astype(BF16))
        den = nd[:, ML_DH:ML_DH + 1]
        h_ref[pl.ds(t0, CHUNK), hl] = nd * (1.0 / jnp.maximum(jnp.abs(den), jnp.exp(-m_t[:, 0:1])))

        m_new = jnp.maximum(b_last + m, jnp.max(w_row, axis=1, keepdims=True))
        wexp = jnp.exp(w_row - m_new)
        cexp = jnp.exp(b_last + m - m_new)
        kw = (kt.astype(F32) * wexp).astype(BF16)
        c_ref[j] = cexp * cmat + _dot(kw, vb)
        m_ref[mrow:mrow + 1, :] = jnp.broadcast_to(m_new, (1, LANES))

    def step(i, carry):
        for j in range(ML_HPS):
            parts = [intra(nc - 1 - i if rev else i, j, rev) for rev in (False, True)]
            for p, rev in zip(parts, (False, True)):
                update(p, j, rev)
        return carry

    lax.fori_loop(0, nc, step, 0)

    real = lax.broadcasted_iota(I32, (1, ML_DHP), 1) < ML_DH
    tb = CHUNK

    def fin(c, carry):
        t0 = pl.multiple_of(c * tb, tb)
        for j in range(ML_HPS):
            hl = slice(j * ML_DHP, (j + 1) * ML_DHP)
            hs = jnp.where(real, hf_ref[pl.ds(t0, tb), hl] + hb_ref[pl.ds(t0, tb), hl], 0.0)
            mu = jnp.sum(hs, axis=1, keepdims=True) * (1.0 / ML_DH)
            dev = jnp.where(real, hs - mu, 0.0)
            var = jnp.sum(dev * dev, axis=1, keepdims=True) * (1.0 / ML_DH)
            hn = dev * lax.rsqrt(var + LN_EPS) * ng_ref[:, hl]
            xc = xc_ref[0, pl.ds(t0, tb), hl].astype(F32)
            z = z_ref[0, pl.ds(t0, tb), hl].astype(F32)
            y_ref[0, pl.ds(t0, tb), hl] = ((hn + sk_ref[:, hl] * xc) * _silu(z)).astype(BF16)
        return carry

    lax.fori_loop(0, s // tb, fin, 0)


def _mlstm(q, kt, v, gcol3, grow, main3, xc, ng, sk):
    b, s, _ = q.shape
    width = ML_HPS * ML_DHP
    steps = ML_HEADS // ML_HPS
    tok = pl.BlockSpec((1, s, width), lambda i, h: (i, 0, h))
    vec = pl.BlockSpec((1, width), lambda i, h: (0, h))
    return pl.pallas_call(
        functools.partial(_mlstm_kernel, s=s),
        grid=(b, steps),
        in_specs=[tok, pl.BlockSpec((1, width, s), lambda i, h: (i, h, 0)), tok,
                  pl.BlockSpec((1, s, LANES), lambda i, h: (i, 0, 0)),
                  pl.BlockSpec((LANES, s), lambda i, h: (0, i)),
                  pl.BlockSpec((1, s, width), lambda i, h: (i, 0, steps + h)),
                  tok, vec, vec],
        out_specs=tok,
        out_shape=jax.ShapeDtypeStruct((b, s, ML_WP), BF16),
        scratch_shapes=[pltpu.VMEM((s, width), F32), pltpu.VMEM((s, width), F32),
                        pltpu.VMEM((ML_HPS, ML_DHP, ML_DHP), F32), pltpu.VMEM((ML_HPS, ML_DHP, ML_DHP), F32),
                        pltpu.VMEM((SUBLANES, LANES), F32)],
        compiler_params=_cparams("parallel", "parallel"),
        name="mlstm",
    )(q, kt, v, gcol3, grow, main3, xc, ng, sk)


def _pad_heads(a, axis):
    a = jnp.moveaxis(a, axis, -1)
    lead = a.shape[:-1]
    a = a.reshape(lead + (ML_HEADS, ML_DH))
    a = jnp.pad(a, [(0, 0)] * len(lead) + [(0, 0), (0, ML_DHP - ML_DH)])
    return jnp.moveaxis(a.reshape(lead + (ML_WP,)), -1, axis)


def kernel(x, mem, mem_ln_g, mem_ln_b, w_mem_kv, router_w, router_b, na_w_in, na_rpb, ml_w_in, ml_conv_w,
           ml_conv_b, ml_w_qkv, ml_gate_b, ml_norm_g, ml_skip, w_out, ln_g, ln_b, exp_w_gate, exp_w_up,
           exp_w_down):
    b, s, d = x.shape
    n = b * s
    nm = mem.shape[1]
    row = lambda a: a.reshape(1, -1)

    mem_k, mem_v = _memkv(mem.reshape(b * nm, d), row(mem_ln_g), row(mem_ln_b), w_mem_kv.astype(BF16))
    mem_k3 = mem_k.reshape(b, nm, MEM_W)
    mem_v3 = mem_v.reshape(b, nm, MEM_W)
    rw_pad = jnp.pad(router_w, ((0, 0), (0, LANES - N_EXPERTS)))
    rw_hi = rw_pad.astype(BF16)
    rw = (rw_hi, (rw_pad - rw_hi.astype(F32)).astype(BF16))
    rb = router_b.reshape(N_EXPERTS, 1)

    x2 = x.reshape(n, d)

    h0 = _proj(x2, na_w_in[0].astype(BF16)).reshape(b, s, 3 * NA_W + MEM_W)
    y_na = _na_attention(h0, _na_bias_table(na_rpb[0]))
    wo = w_out[0].astype(BF16)
    xr = _outproj_ln(y_na.reshape(n, NA_W), h0.reshape(n, 3 * NA_W + MEM_W), 3 * NA_W // MEM_W, mem_k3, mem_v3,
                     wo[:NA_W], wo[NA_W:], x2, row(ln_g[0, 0]), row(ln_b[0, 0]))
    x2, moe_buf = _moe_ln(xr, rw, rb, exp_w_gate, exp_w_up, exp_w_down, 0, row(ln_g[0, 1]), row(ln_b[0, 1]))

    w1 = ml_w_in[0]
    w_main = jnp.concatenate([_pad_heads(w1[:, :ML_W], 1), _pad_heads(w1[:, ML_W:2 * ML_W], 1),
                              w1[:, 2 * ML_W + 4 * ML_HEADS:]], axis=1).astype(BF16)
    w_g = jnp.pad(w1[:, 2 * ML_W:2 * ML_W + 4 * ML_HEADS], ((0, 0), (0, LANES - 4 * ML_HEADS))).astype(BF16)
    gb = jnp.pad(ml_gate_b[0].reshape(4 * ML_HEADS), (0, LANES - 4 * ML_HEADS))
    main, acol, arow = _proj_gates(x2, w_main, w_g, w_g.T, gb.reshape(1, LANES), gb.reshape(LANES, 1))
    main3 = main.reshape(b, s, 2 * ML_WP + MEM_W)
    wqkv = jnp.pad(ml_w_qkv[0], ((0, 0), (0, 0), (0, ML_DHP - ML_DH), (0, ML_DHP - ML_DH))).astype(BF16)
    q, k, v, xc = _conv_qkv(main3, _pad_heads(ml_conv_w[0], 1), _pad_heads(row(ml_conv_b[0]), 1),
                            wqkv[0], jnp.swapaxes(wqkv[1], 1, 2), wqkv[2])
    y_ml = _mlstm(q, k, v, acol.reshape(b, s, LANES), arow, main3, xc,
                  _pad_heads(row(ml_norm_g[0]), 1), _pad_heads(row(ml_skip[0]), 1))
    wo = w_out[1]
    xr = _outproj_ln(y_ml.reshape(n, ML_WP), main, 2 * ML_WP // MEM_W, mem_k3, mem_v3,
                     _pad_heads(wo[:ML_W], 0).astype(BF16), wo[ML_W:].astype(BF16), x2,
                     row(ln_g[1, 0]), row(ln_b[1, 0]))
    x2, _ = _moe_ln(xr, rw, rb, exp_w_gate, exp_w_up, exp_w_down, 1, row(ln_g[1, 1]), row(ln_b[1, 1]), moe_buf)
    return x2.reshape(b, s, d)
```

```python
import functools

import numpy as np
import jax
import jax.numpy as jnp
from jax import lax
from jax.experimental import pallas as pl
from jax.experimental.pallas import tpu as pltpu

F32 = jnp.float32
BF16 = jnp.bfloat16
I32 = jnp.int32

D_MODEL = 1024
DEPTH = 2
GRID_W = 64
MEM_HEADS = 4
MEM_DH = 64
MEM_W = MEM_HEADS * MEM_DH
NA_HEADS = 12
NA_DH = 64
NA_W = NA_HEADS * NA_DH
WIN_H = 8
WIN_W = 16
ML_HEADS = 4
ML_DH = 192
ML_DHP = 256
ML_W = ML_HEADS * ML_DH
ML_WP = ML_HEADS * ML_DHP
CONV_K = 5
CHUNK = 256
N_EXPERTS = 16
N_GROUPS = 4
EXPERTS_PER_GROUP = N_EXPERTS // N_GROUPS
D_EXPERT = 512
ALPHA = (2 * DEPTH) ** 0.25
LN_EPS = 1e-5
NEG = -1e30

LANES = 128
SUBLANES = 8
ROW_CHUNKS = D_MODEL // LANES
MOE_BM = 512
MOE_TILE = 512
ML_HPS = 2
NA_ROWS_PER_STEP = 16
ROW_TILE = 1024
PLACE_UNROLL = 8
VMEM_LIMIT = 48 * 1024 * 1024


def _cparams(*sem):
    return pltpu.CompilerParams(dimension_semantics=sem, vmem_limit_bytes=VMEM_LIMIT)


def _dot(a, b):
    return jnp.dot(a, b, preferred_element_type=F32)


def _dot_nt(a, b, precision=None):
    return lax.dot_general(a, b, (((1,), (1,)), ((), ())), precision=precision,
                           preferred_element_type=F32)


def _ln(z, g, b):
    mu = jnp.mean(z, axis=-1, keepdims=True)
    zc = z - mu
    var = jnp.mean(zc * zc, axis=-1, keepdims=True)
    return zc * lax.rsqrt(var + LN_EPS) * g + b


def _silu(x):
    return x * jax.nn.sigmoid(x)


def _read_rows(ref, n):
    return jnp.concatenate([ref[pl.ds(j, n, stride=ROW_CHUNKS), :] for j in range(ROW_CHUNKS)], axis=1)


def _write_rows(ref, val, n):
    for j in range(ROW_CHUNKS):
        ref[pl.ds(j, n, stride=ROW_CHUNKS), :] = val[:, j * LANES:(j + 1) * LANES]


def _memkv_kernel(m_ref, g_ref, b_ref, w_ref, k_ref, v_ref):
    z = _ln(m_ref[...], g_ref[...], b_ref[...])
    kv = _dot(z.astype(BF16), w_ref[...])
    k_ref[...] = kv[:, :MEM_W].astype(BF16)
    v_ref[...] = kv[:, MEM_W:].astype(BF16)


def _memkv(mem2, g, b, w):
    n = mem2.shape[0]
    tm = min(ROW_TILE, n)
    return pl.pallas_call(
        _memkv_kernel,
        grid=(n // tm,),
        in_specs=[pl.BlockSpec((tm, D_MODEL), lambda i: (i, 0)),
                  pl.BlockSpec((1, D_MODEL), lambda i: (0, 0)),
                  pl.BlockSpec((1, D_MODEL), lambda i: (0, 0)),
                  pl.BlockSpec((D_MODEL, 2 * MEM_W), lambda i: (0, 0))],
        out_specs=[pl.BlockSpec((tm, MEM_W), lambda i: (i, 0)),
                   pl.BlockSpec((tm, MEM_W), lambda i: (i, 0))],
        out_shape=[jax.ShapeDtypeStruct((n, MEM_W), BF16)] * 2,
        compiler_params=_cparams("parallel"),
        name="memkv",
    )(mem2, g, b, w)


def _proj_kernel(x_ref, w_ref, o_ref):
    o_ref[...] = _dot(x_ref[...].astype(BF16), w_ref[...]).astype(o_ref.dtype)


def _proj(x2, w, tm=ROW_TILE):
    n, k = x2.shape
    nout = w.shape[1]
    return pl.pallas_call(
        _proj_kernel,
        grid=(n // tm,),
        in_specs=[pl.BlockSpec((tm, k), lambda i: (i, 0)),
                  pl.BlockSpec((k, nout), lambda i: (0, 0))],
        out_specs=pl.BlockSpec((tm, nout), lambda i: (i, 0)),
        out_shape=jax.ShapeDtypeStruct((n, nout), BF16),
        compiler_params=_cparams("parallel"),
        name="in_proj",
    )(x2, w)


def _split3(x):
    hi = x.astype(BF16)
    r1 = x - hi.astype(F32)
    mid = r1.astype(BF16)
    lo = (r1 - mid.astype(F32)).astype(BF16)
    return hi, mid, lo


def _proj_gates_kernel(x_ref, w_ref, wg_ref, wgt_ref, gbc_ref, gbr_ref, o_ref, g_ref, gt_ref, *, tm):
    xb = x_ref[...].astype(BF16)
    o_ref[...] = _dot(xb, w_ref[...]).astype(BF16)
    gcol = _dot(xb, wg_ref[...]) + gbc_ref[...]
    grow = _dot_nt(wgt_ref[...], xb) + gbr_ref[...]
    lane = lax.broadcasted_iota(I32, (1, LANES), 1)
    sub = lax.broadcasted_iota(I32, (LANES, 1), 0)
    ti = lax.broadcasted_iota(I32, (CHUNK, CHUNK), 0)
    tj = lax.broadcasted_iota(I32, (CHUNK, CHUNK), 1)
    lower = (tj <= ti).astype(BF16)
    upper = (ti <= tj).astype(BF16)

    def pick(idx, pre, suf, raw):
        fwd = jnp.logical_and(idx >= ML_HEADS, idx < 2 * ML_HEADS)
        bwd = jnp.logical_and(idx >= 3 * ML_HEADS, idx < 4 * ML_HEADS)
        return jnp.where(fwd, pre, jnp.where(bwd, suf, raw))

    for c in range(tm // CHUNK):
        tc = slice(c * CHUNK, (c + 1) * CHUNK)
        g = gcol[tc, :]
        ls = jax.nn.log_sigmoid(g)
        pre = sum(_dot(lower, part) for part in _split3(ls))
        suf = jnp.sum(ls, axis=0, keepdims=True) - pre + ls
        g_ref[tc, :] = pick(lane, pre, suf, g)
        g = grow[:, tc]
        ls = jax.nn.log_sigmoid(g)
        pre = sum(_dot(part, upper) for part in _split3(ls))
        suf = jnp.sum(ls, axis=1, keepdims=True) - pre + ls
        gt_ref[:, tc] = pick(sub, pre, suf, g)


def _proj_gates(x2, w, wg, wgt, gbc, gbr, tm=ROW_TILE):
    n, k = x2.shape
    nout = w.shape[1]
    return pl.pallas_call(
        functools.partial(_proj_gates_kernel, tm=tm),
        grid=(n // tm,),
        in_specs=[pl.BlockSpec((tm, k), lambda i: (i, 0)),
                  pl.BlockSpec((k, nout), lambda i: (0, 0)),
                  pl.BlockSpec((k, LANES), lambda i: (0, 0)),
                  pl.BlockSpec((LANES, k), lambda i: (0, 0)),
                  pl.BlockSpec((1, LANES), lambda i: (0, 0)),
                  pl.BlockSpec((LANES, 1), lambda i: (0, 0))],
        out_specs=[pl.BlockSpec((tm, nout), lambda i: (i, 0)),
                   pl.BlockSpec((tm, LANES), lambda i: (i, 0)),
                   pl.BlockSpec((LANES, tm), lambda i: (0, i))],
        out_shape=[jax.ShapeDtypeStruct((n, nout), BF16),
                   jax.ShapeDtypeStruct((n, LANES), F32),
                   jax.ShapeDtypeStruct((LANES, n), F32)],
        compiler_params=_cparams("parallel"),
        name="in_proj_gates",
    )(x2, w, wg, wgt, gbc, gbr)


def _na_kernel(q_ref, k_ref, v_ref, tbl_ref, o_ref, *, rows):
    lane = lax.broadcasted_iota(I32, (1, LANES), 1)
    first = lane < NA_DH
    nkeys = WIN_H * GRID_W

    def rows_step(i, carry):
        rr = [i * NA_ROWS_PER_STEP + u for u in range(NA_ROWS_PER_STEP)]
        rss = [jnp.clip(r - WIN_H // 2, 0, rows - WIN_H) for r in rr]
        scores = []
        for r, rs in zip(rr, rss):
            q = q_ref[0, pl.ds(pl.multiple_of(r * GRID_W, GRID_W), GRID_W), :]
            q = q * jnp.asarray(NA_DH ** -0.5, BF16)
            q2 = jnp.concatenate([jnp.where(first, q, jnp.zeros_like(q)),
                                  jnp.where(first, jnp.zeros_like(q), q)], axis=0)
            k = k_ref[0, pl.ds(pl.multiple_of(rs * GRID_W, GRID_W), nkeys), :]
            dr0 = rs - r + WIN_H - 1
            bias = jnp.concatenate(
                [jnp.concatenate([tbl_ref[0, half, dr0 + 2 * m] for m in range(WIN_H // 2)], axis=1)
                 for half in range(2)], axis=0)
            scores.append(_dot_nt(q2, k) + bias)
        probs = []
        for s in scores:
            p = jnp.exp(s - jnp.max(s, axis=-1, keepdims=True))
            probs.append((p.astype(BF16), jnp.sum(p, axis=-1, keepdims=True)))
        for r, rs, (p, l) in zip(rr, rss, probs):
            v = v_ref[0, pl.ds(pl.multiple_of(rs * GRID_W, GRID_W), nkeys), :]
            o = _dot(p, v) / l
            o = jnp.where(first, o[:GRID_W], o[GRID_W:])
            o_ref[0, pl.ds(pl.multiple_of(r * GRID_W, GRID_W), GRID_W), :] = o.astype(o_ref.dtype)
        return carry

    lax.fori_loop(0, rows // NA_ROWS_PER_STEP, rows_step, 0)


def _na_bias_table(rpb):
    qc = np.arange(GRID_W)[:, None]
    kc = np.arange(GRID_W)[None, :]
    cs = np.clip(qc - WIN_W // 2, 0, GRID_W - WIN_W)
    col_in = (kc >= cs) & (kc < cs + WIN_W)
    side = GRID_W - WIN_W
    wide = jnp.pad(rpb, ((0, 0), (0, 0), (side, side)))
    t = jnp.stack([wide[:, :, GRID_W - 1 - q:2 * GRID_W - 1 - q] for q in range(GRID_W)], axis=2)
    t = jnp.where(col_in, t, NEG).astype(F32)
    t2 = jnp.concatenate([t[:, :-1], t[:, 1:]], axis=-1)
    return t2.reshape(NA_HEADS // 2, 2, 2 * WIN_H - 2, GRID_W, 2 * GRID_W)


def _na_attention(h3, tbl):
    b, s, _ = h3.shape
    rows = s // GRID_W
    npair = NA_HEADS // 2
    return pl.pallas_call(
        functools.partial(_na_kernel, rows=rows),
        grid=(b, npair),
        in_specs=[pl.BlockSpec((1, s, LANES), lambda i, p: (i, 0, p)),
                  pl.BlockSpec((1, s, LANES), lambda i, p: (i, 0, npair + p)),
                  pl.BlockSpec((1, s, LANES), lambda i, p: (i, 0, 2 * npair + p)),
                  pl.BlockSpec((1, 2, 2 * WIN_H - 2, GRID_W, 2 * GRID_W), lambda i, p: (p, 0, 0, 0, 0))],
        out_specs=pl.BlockSpec((1, s, LANES), lambda i, p: (i, 0, p)),
        out_shape=jax.ShapeDtypeStruct((b, s, NA_W), BF16),
        compiler_params=_cparams("parallel", "parallel"),
        name="na_attention",
    )(h3, h3, h3, tbl)


def _outproj_ln_kernel(ya_ref, qm_ref, mk_ref, mv_ref, wa_ref, wm_ref, x_ref, g_ref, b_ref, or_ref, *, tm):
    lane = lax.broadcasted_iota(I32, (1, LANES), 1)
    first = lane < MEM_DH
    q = qm_ref[...] * jnp.asarray(MEM_DH ** -0.5, BF16)
    cols = [slice(p * LANES, (p + 1) * LANES) for p in range(MEM_HEADS // 2)]
    scores = []
    for c in cols:
        qp = q[:, c]
        q2 = jnp.concatenate([jnp.where(first, qp, jnp.zeros_like(qp)),
                              jnp.where(first, jnp.zeros_like(qp), qp)], axis=0)
        scores.append(_dot_nt(q2, mk_ref[0, :, c]))
    probs = []
    for s in scores:
        p = jnp.exp(s - jnp.max(s, axis=-1, keepdims=True))
        probs.append((p.astype(BF16), jnp.sum(p, axis=-1, keepdims=True)))
    outs = []
    for c, (p, l) in zip(cols, probs):
        o = _dot(p, mv_ref[0, :, c]) / l
        outs.append(jnp.where(first, o[:tm], o[tm:]))
    ym = jnp.concatenate(outs, axis=1).astype(BF16)
    acc = _dot(ya_ref[...], wa_ref[...]) + _dot(ym, wm_ref[...])
    _write_rows(or_ref, _ln(ALPHA * x_ref[...] + acc, g_ref[...], b_ref[...]), tm)


def _outproj_ln(ya, h2, qm_block, mem_k3, mem_v3, wa, wm, x2, g, b, tm=ROW_TILE):
    n = x2.shape[0]
    ka = ya.shape[1]
    nb, nm, _ = mem_k3.shape
    per_batch = n // nb // tm
    full = lambda shape: pl.BlockSpec(shape, lambda i: (0,) * len(shape))
    return pl.pallas_call(
        functools.partial(_outproj_ln_kernel, tm=tm),
        grid=(n // tm,),
        in_specs=[pl.BlockSpec((tm, ka), lambda i: (i, 0)),
                  pl.BlockSpec((tm, MEM_W), lambda i: (i, qm_block)),
                  pl.BlockSpec((1, nm, MEM_W), lambda i: (i // per_batch, 0, 0)),
                  pl.BlockSpec((1, nm, MEM_W), lambda i: (i // per_batch, 0, 0)),
                  full((ka, D_MODEL)), full((MEM_W, D_MODEL)),
                  pl.BlockSpec((tm, D_MODEL), lambda i: (i, 0)),
                  full((1, D_MODEL)), full((1, D_MODEL))],
        out_specs=pl.BlockSpec((tm * ROW_CHUNKS, LANES), lambda i: (i, 0)),
        out_shape=jax.ShapeDtypeStruct((n * ROW_CHUNKS, LANES), F32),
        compiler_params=_cparams("parallel"),
        name="outproj_ln",
    )(ya, h2, mem_k3, mem_v3, wa, wm, x2, g, b)


def _router_kernel(x_ref, rwh_ref, rwl_ref, rb_ref, lpos_ref, w_ref, cnt_ref, tcnt_ref, toff_ref, tbef_ref, *, tm):
    @pl.when(pl.program_id(0) == 0)
    def _():
        cnt_ref[...] = jnp.zeros_like(cnt_ref)

    x = _read_rows(x_ref, tm)
    xh = x.astype(BF16)
    xl = (x - xh.astype(F32)).astype(BF16)
    logits_t = _dot(xh, rwh_ref[...]) + (_dot(xh, rwl_ref[...]) + _dot(xl, rwh_ref[...]))
    logits = logits_t.T[:N_EXPERTS]
    scores = jax.nn.sigmoid(logits)
    biased = scores + rb_ref[...]
    bv = [biased[e:e + 1, :] for e in range(N_EXPERTS)]
    sv = [scores[e:e + 1, :] for e in range(N_EXPERTS)]

    grp = []
    for g in range(N_GROUPS):
        m = bv[g * EXPERTS_PER_GROUP:(g + 1) * EXPERTS_PER_GROUP]
        best = None
        for a in range(EXPERTS_PER_GROUP):
            for c in range(a + 1, EXPERTS_PER_GROUP):
                pair = m[a] + m[c]
                best = pair if best is None else jnp.maximum(best, pair)
        grp.append(best)
    gsel = jnp.zeros((1, tm), I32)
    gbest = grp[0]
    for g in range(1, N_GROUPS):
        better = grp[g] > gbest
        gsel = jnp.where(better, g, gsel)
        gbest = jnp.where(better, grp[g], gbest)

    def pick(vals, j):
        out = vals[j]
        for g in range(1, N_GROUPS):
            out = jnp.where(gsel == g, vals[g * EXPERTS_PER_GROUP + j], out)
        return out

    cb = [pick(bv, j) for j in range(EXPERTS_PER_GROUP)]
    cs = [pick(sv, j) for j in range(EXPERTS_PER_GROUP)]
    i1 = jnp.zeros((1, tm), I32)
    m1 = cb[0]
    s1 = cs[0]
    for j in range(1, EXPERTS_PER_GROUP):
        gt = cb[j] > m1
        i1 = jnp.where(gt, j, i1)
        m1 = jnp.where(gt, cb[j], m1)
        s1 = jnp.where(gt, cs[j], s1)
    i2 = jnp.zeros((1, tm), I32)
    m2 = jnp.full((1, tm), -jnp.inf, F32)
    s2 = jnp.zeros((1, tm), F32)
    for j in range(EXPERTS_PER_GROUP):
        ok = jnp.logical_and(i1 != j, cb[j] > m2)
        i2 = jnp.where(ok, j, i2)
        m2 = jnp.where(ok, cb[j], m2)
        s2 = jnp.where(ok, cs[j], s2)
    e1 = gsel * EXPERTS_PER_GROUP + i1
    e2 = gsel * EXPERTS_PER_GROUP + i2
    tot = s1 + s2
    w_ref[...] = jnp.concatenate([s1 / tot, s2 / tot], axis=0)

    i = pl.program_id(0)
    eio = lax.broadcasted_iota(I32, (N_EXPERTS, tm), 0)
    oh1 = eio == e1
    oh2 = eio == e2
    ohs = jnp.logical_or(oh1, oh2).astype(F32)
    before = (lax.broadcasted_iota(I32, (tm, tm), 0) < lax.broadcasted_iota(I32, (tm, tm), 1))
    pre = _dot(ohs.astype(BF16), before.astype(BF16))
    tile_cnt = jnp.sum(ohs, axis=1, keepdims=True)
    offs = []
    acc = jnp.zeros((1, 1), F32)
    for e in range(N_EXPERTS):
        offs.append(acc)
        acc = acc + tile_cnt[e:e + 1, :]
    tile_off = jnp.concatenate(offs, axis=0)
    pos = tile_off + pre
    p1 = jnp.sum(jnp.where(oh1, pos, 0.0), axis=0, keepdims=True)
    p2 = jnp.sum(jnp.where(oh2, pos, 0.0), axis=0, keepdims=True)
    lpos_ref[...] = jnp.concatenate([p1, p2], axis=0).astype(I32) * ROW_CHUNKS

    @pl.when(i == 0)
    def _():
        for ref in (tcnt_ref, toff_ref, tbef_ref):
            ref[...] = jnp.zeros_like(ref)

    here = lax.broadcasted_iota(I32, (1, LANES), 1) == i
    tcnt_ref[...] = jnp.where(here, tile_cnt, tcnt_ref[...])
    toff_ref[...] = jnp.where(here, tile_off, toff_ref[...])
    tbef_ref[...] = jnp.where(here, cnt_ref[:, 0:1], tbef_ref[...])
    cnt_ref[...] += tile_cnt


def _router(xr, rw, rb, tm):
    n = xr.shape[0] // ROW_CHUNKS
    assert n // tm <= LANES
    table = pl.BlockSpec((N_EXPERTS, LANES), lambda i: (0, 0))
    return pl.pallas_call(
        functools.partial(_router_kernel, tm=tm),
        grid=(n // tm,),
        in_specs=[pl.BlockSpec((tm * ROW_CHUNKS, LANES), lambda i: (i, 0)),
                  pl.BlockSpec((D_MODEL, LANES), lambda i: (0, 0)),
                  pl.BlockSpec((D_MODEL, LANES), lambda i: (0, 0)),
                  pl.BlockSpec((N_EXPERTS, 1), lambda i: (0, 0))],
        out_specs=[pl.BlockSpec((2, tm), lambda i: (0, i)),
                   pl.BlockSpec((2, tm), lambda i: (0, i)),
                   table, table, table, table],
        out_shape=[jax.ShapeDtypeStruct((2, n), I32),
                   jax.ShapeDtypeStruct((2, n), F32)]
                  + [jax.ShapeDtypeStruct((N_EXPERTS, LANES), F32)] * 4,
        compiler_params=_cparams("arbitrary"),
        name="router",
    )(xr, rw[0], rw[1], rb)


def _plan_kernel(cnt_ref, tbef_ref, meta_ref, rstart_ref, *, nbl):
    shift = MOE_BM.bit_length() - 1
    cnt = cnt_ref[...].astype(I32)
    padded = ((cnt + (MOE_BM - 1)) >> shift) << shift
    starts = []
    acc = jnp.zeros((1, LANES), I32)
    for e in range(N_EXPERTS):
        starts.append(acc)
        acc = acc + padded[e:e + 1, :]
    pad_start = jnp.concatenate(starts, axis=0)
    pad_end = pad_start + padded
    rstart_ref[...] = pad_start + tbef_ref[...].astype(I32)
    blk0 = lax.broadcasted_iota(I32, (N_EXPERTS, nbl), 1) * MOE_BM
    block_e = jnp.sum((pad_end[:, 0:1] <= blk0).astype(I32), axis=0, keepdims=True)
    block_e = jnp.minimum(block_e, N_EXPERTS - 1)
    n_used = jnp.broadcast_to(acc[:, 0:1] >> shift, (1, nbl))
    diag = lax.broadcasted_iota(I32, (N_EXPERTS, nbl), 0) == lax.broadcasted_iota(I32, (N_EXPERTS, nbl), 1)
    fill_lo = jnp.sum(jnp.where(diag, (pad_start + cnt)[:, 0:1], 0), axis=0, keepdims=True)
    fill_hi = jnp.sum(jnp.where(diag, pad_end[:, 0:1], 0), axis=0, keepdims=True)
    meta_ref[...] = jnp.concatenate([block_e, n_used, fill_lo, fill_hi, jnp.zeros((SUBLANES - 4, nbl), I32)],
                                    axis=0)


def _plan(cnt, tbef, n_blocks):
    nbl = -(-n_blocks // LANES) * LANES
    table = pl.BlockSpec((N_EXPERTS, LANES), lambda i: (0, 0))
    return pl.pallas_call(
        functools.partial(_plan_kernel, nbl=nbl),
        grid=(1,),
        in_specs=[table, table],
        out_specs=[pl.BlockSpec((SUBLANES, nbl), lambda i: (0, 0)), table],
        out_shape=[jax.ShapeDtypeStruct((SUBLANES, nbl), I32),
                   jax.ShapeDtypeStruct((N_EXPERTS, LANES), I32)],
        compiler_params=_cparams("arbitrary"),
        name="moe_plan",
    )(cnt, tbef)


def _rows(ref, row, nrows):
    return ref.at[pl.ds(pl.multiple_of(row * ROW_CHUNKS, ROW_CHUNKS), nrows * ROW_CHUNKS), :]


def _rows_wait(src_hbm, buf, sem):
    pltpu.make_async_copy(src_hbm.at[pl.ds(0, buf.shape[0]), :], buf, sem).wait()


def _copy_pieces(src, src_row, dst, dst_row, count, max_rows, sem, wait=False):
    bit = max_rows.bit_length() - 1
    while bit >= 0:
        size = 1 << bit
        done = (count >> (bit + 1)) << (bit + 1)

        @pl.when(((count >> bit) & 1) == 1)
        def _():
            cp = pltpu.make_async_copy(_rows(src, src_row + done, size), _rows(dst, dst_row + done, size), sem)
            cp.start()
            if wait:
                cp.wait()

        bit -= 1


def _tile_runs(tcnt_ref, toff_ref, rstart_ref, tile, buf, hbm, sem, *, to_hbm, tm):
    def per_expert(e, carry):
        k = tile * N_EXPERTS + e
        if to_hbm:
            _copy_pieces(buf, toff_ref[k], hbm, rstart_ref[k], tcnt_ref[k], tm, sem)
        else:
            _copy_pieces(hbm, rstart_ref[k], buf, toff_ref[k], tcnt_ref[k], tm, sem)
        return carry

    lax.fori_loop(0, N_EXPERTS, per_expert, 0)


def _dispatch_kernel(lpos_ref, tcnt_ref, toff_ref, rstart_ref, flo_ref, fhi_ref, nu_ref, x_ref, old_hbm, xs_hbm,
                     s0, s1, zbuf, sem, zsem, *, n, tm, n_blocks, reuse):
    del old_hbm
    i = pl.program_id(0)
    nt = pl.num_programs(0)
    bufs = (s0, s1)
    unroll = PLACE_UNROLL

    for slot in range(2):
        @pl.when(i % 2 == slot)
        def _():
            buf = bufs[slot]

            @pl.when(i >= 2)
            def _():
                _rows_wait(xs_hbm, buf, sem.at[slot])

            def place(c, carry):
                tok = i * tm + c * unroll
                src = pl.multiple_of(c * (unroll * ROW_CHUNKS), unroll * ROW_CHUNKS)
                for u in range(unroll):
                    v = x_ref[pl.ds(src + u * ROW_CHUNKS, ROW_CHUNKS), :]
                    for k in range(2):
                        p = lpos_ref[k * n + tok + u]
                        buf[pl.ds(pl.multiple_of(p, ROW_CHUNKS), ROW_CHUNKS), :] = v
                return carry

            lax.fori_loop(0, tm // unroll, place, 0)
            _tile_runs(tcnt_ref, toff_ref, rstart_ref, i, buf, xs_hbm, sem.at[slot], to_hbm=True, tm=tm)

    zero_rows = n_blocks * MOE_BM - 2 * n
    if not reuse:
        @pl.when(i == 0)
        def _():
            zbuf[...] = jnp.zeros_like(zbuf)
            for e in range(N_EXPERTS):
                _copy_pieces(zbuf, 0, xs_hbm, flo_ref[e], fhi_ref[e] - flo_ref[e], MOE_BM // 2, zsem)

            def zero_block(j, carry):
                pltpu.make_async_copy(zbuf, _rows(xs_hbm, j * MOE_BM, MOE_BM), zsem).start()
                return carry

            lax.fori_loop(nu_ref[0], n_blocks, zero_block, 0)

    @pl.when(i == nt - 1)
    def _():
        for slot in range(2):
            @pl.when(nt > slot)
            def _():
                _rows_wait(xs_hbm, bufs[slot], sem.at[slot])

        if not reuse:
            pltpu.make_async_copy(_rows(xs_hbm, 0, zero_rows), _rows(xs_hbm, 0, zero_rows), zsem).wait()


def _dispatch(lpos_flat, tcnt, toff, rstart, fill_lo, fill_hi, n_used, xr, n_blocks, tm, old=None):
    n = xr.shape[0] // ROW_CHUNKS
    reuse = old is not None
    if not reuse:
        old = jnp.zeros((SUBLANES, LANES), F32)
    return pl.pallas_call(
        functools.partial(_dispatch_kernel, n=n, tm=tm, n_blocks=n_blocks, reuse=reuse),
        grid_spec=pltpu.PrefetchScalarGridSpec(
            num_scalar_prefetch=7,
            grid=(n // tm,),
            in_specs=[pl.BlockSpec((tm * ROW_CHUNKS, LANES), lambda i, *_: (i, 0)),
                      pl.BlockSpec(memory_space=pl.ANY)],
            out_specs=pl.BlockSpec(memory_space=pl.ANY),
            scratch_shapes=[pltpu.VMEM((2 * tm * ROW_CHUNKS, LANES), F32),
                            pltpu.VMEM((2 * tm * ROW_CHUNKS, LANES), F32),
                            pltpu.VMEM((MOE_BM * ROW_CHUNKS, LANES), F32),
                            pltpu.SemaphoreType.DMA((2,)),
                            pltpu.SemaphoreType.DMA(())]),
        out_shape=jax.ShapeDtypeStruct((n_blocks * MOE_BM * ROW_CHUNKS, LANES), F32),
        input_output_aliases={8: 0} if reuse else {},
        compiler_params=_cparams("arbitrary"),
        name="moe_dispatch",
    )(lpos_flat, tcnt, toff, rstart, fill_lo, fill_hi, n_used, xr, old)


EXPERT_BUFS = 3


def _experts_kernel(be_ref, nu_ref, xs_hbm, wg_ref, wu_ref, wd_ref, y_hbm, wgb, wub, wdb, xin, yout,
                    sem_in, sem_out):
    j = pl.program_id(0)
    nb = pl.num_programs(0)
    n_used = nu_ref[0]
    used = j < n_used
    ahead = EXPERT_BUFS - 1

    def fetch(blk, slot):
        return pltpu.make_async_copy(_rows(xs_hbm, blk * MOE_BM, MOE_BM), xin.at[slot], sem_in.at[slot])

    def write_back(blk, slot):
        return pltpu.make_async_copy(yout.at[slot], _rows(y_hbm, blk * MOE_BM, MOE_BM), sem_out.at[slot])

    @pl.when(j == 0)
    def _():
        for d in range(ahead):
            @pl.when(d < n_used)
            def _():
                fetch(d, d).start()

    @pl.when(jnp.logical_and(used, jnp.logical_or(j == 0, be_ref[j] != be_ref[jnp.maximum(j - 1, 0)])))
    def _():
        wgb[...] = wg_ref[0, 0].astype(BF16)
        wub[...] = wu_ref[0, 0].astype(BF16)
        wdb[...] = wd_ref[0, 0].astype(BF16)

    for slot in range(EXPERT_BUFS):
        @pl.when(jnp.logical_and(used, j % EXPERT_BUFS == slot))
        def _():
            @pl.when(j >= EXPERT_BUFS)
            def _():
                write_back(j - EXPERT_BUFS, slot).wait()

            @pl.when(j + ahead < n_used)
            def _():
                fetch(j + ahead, (slot + ahead) % EXPERT_BUFS).start()

            fetch(j, slot).wait()
            x = _read_rows(xin.at[slot], MOE_BM).astype(BF16)
            h = _silu(_dot(x, wgb[...])) * _dot(x, wub[...])
            _write_rows(yout.at[slot], _dot(h.astype(BF16), wdb[...]), MOE_BM)
            write_back(j, slot).start()

    @pl.when(j == nb - 1)
    def _():
        for back in range(1, EXPERT_BUFS + 1):
            @pl.when(n_used - back >= 0)
            def _():
                blk = n_used - back
                pltpu.make_async_copy(yout.at[0], _rows(y_hbm, blk * MOE_BM, MOE_BM),
                                      sem_out.at[blk % EXPERT_BUFS]).wait()


def _experts(block_e, n_used, xs, wg, wu, wd, layer):
    n_blocks = block_e.shape[0]

    def last_used(j, nu):
        return jnp.minimum(j, nu[0] - 1)

    def wblk(j, be, nu):
        return (layer, be[last_used(j, nu)], 0, 0)

    return pl.pallas_call(
        _experts_kernel,
        grid_spec=pltpu.PrefetchScalarGridSpec(
            num_scalar_prefetch=2,
            grid=(n_blocks,),
            in_specs=[pl.BlockSpec(memory_space=pl.ANY),
                      pl.BlockSpec((1, 1, D_MODEL, D_EXPERT), wblk),
                      pl.BlockSpec((1, 1, D_MODEL, D_EXPERT), wblk),
                      pl.BlockSpec((1, 1, D_EXPERT, D_MODEL), wblk)],
            out_specs=pl.BlockSpec(memory_space=pl.ANY),
            scratch_shapes=[pltpu.VMEM((D_MODEL, D_EXPERT), BF16), pltpu.VMEM((D_MODEL, D_EXPERT), BF16),
                            pltpu.VMEM((D_EXPERT, D_MODEL), BF16),
                            pltpu.VMEM((EXPERT_BUFS, MOE_BM * ROW_CHUNKS, LANES), F32),
                            pltpu.VMEM((EXPERT_BUFS, MOE_BM * ROW_CHUNKS, LANES), F32),
                            pltpu.SemaphoreType.DMA((EXPERT_BUFS,)), pltpu.SemaphoreType.DMA((EXPERT_BUFS,))]),
        out_shape=jax.ShapeDtypeStruct(xs.shape, F32),
        input_output_aliases={2: 0},
        compiler_params=_cparams("arbitrary"),
        name="moe_experts",
    )(block_e, n_used, xs, wg, wu, wd)


def _combine_ln_kernel(lpos_ref, tcnt_ref, toff_ref, rstart_ref, y_hbm, x_ref, w1_ref, w2_ref, g_ref, b_ref, o_ref,
                       r0, r1, u1, u2, sem, *, n, tm):
    i = pl.program_id(0)
    nt = pl.num_programs(0)
    bufs = (r0, r1)
    unroll = PLACE_UNROLL

    def fetch(tile, slot):
        _tile_runs(tcnt_ref, toff_ref, rstart_ref, tile, bufs[slot], y_hbm, sem.at[slot], to_hbm=False, tm=tm)

    @pl.when(i == 0)
    def _():
        fetch(0, 0)

    for slot in range(2):
        @pl.when(i % 2 == slot)
        def _():
            @pl.when(i + 1 < nt)
            def _():
                fetch(i + 1, 1 - slot)

            buf = bufs[slot]
            _rows_wait(y_hbm, buf, sem.at[slot])

            def place(c, carry):
                tok = i * tm + c * unroll
                dst0 = pl.multiple_of(c * (unroll * ROW_CHUNKS), unroll * ROW_CHUNKS)
                for u in range(unroll):
                    dst = pl.ds(dst0 + u * ROW_CHUNKS, ROW_CHUNKS)
                    for k, out in enumerate((u1, u2)):
                        p = lpos_ref[k * n + tok + u]
                        out[dst, :] = buf[pl.ds(pl.multiple_of(p, ROW_CHUNKS), ROW_CHUNKS), :]
                return carry

            lax.fori_loop(0, tm // unroll, place, 0)
            moe = w1_ref[...] * _read_rows(u1, tm) + w2_ref[...] * _read_rows(u2, tm)
            o_ref[...] = _ln(ALPHA * _read_rows(x_ref, tm) + moe, g_ref[...], b_ref[...])


def _combine_ln(lpos_flat, tcnt, toff, rstart, y, xr, w1, w2, g, b, tm):
    n = xr.shape[0] // ROW_CHUNKS
    return pl.pallas_call(
        functools.partial(_combine_ln_kernel, n=n, tm=tm),
        grid_spec=pltpu.PrefetchScalarGridSpec(
            num_scalar_prefetch=4,
            grid=(n // tm,),
            in_specs=[pl.BlockSpec(memory_space=pl.ANY),
                      pl.BlockSpec((tm * ROW_CHUNKS, LANES), lambda i, *_: (i, 0)),
                      pl.BlockSpec((tm, 1), lambda i, *_: (i, 0)),
                      pl.BlockSpec((tm, 1), lambda i, *_: (i, 0)),
                      pl.BlockSpec((1, D_MODEL), lambda i, *_: (0, 0)),
                      pl.BlockSpec((1, D_MODEL), lambda i, *_: (0, 0))],
            out_specs=pl.BlockSpec((tm, D_MODEL), lambda i, *_: (i, 0)),
            scratch_shapes=[pltpu.VMEM((2 * tm * ROW_CHUNKS, LANES), F32)] * 2
                           + [pltpu.VMEM((tm * ROW_CHUNKS, LANES), F32)] * 2
                           + [pltpu.SemaphoreType.DMA((2,))]),
        out_shape=jax.ShapeDtypeStruct((n, D_MODEL), F32),
        compiler_params=_cparams("arbitrary"),
        name="moe_combine_ln",
    )(lpos_flat, tcnt, toff, rstart, y, xr, w1, w2, g, b)


def _moe_ln(xr, rw, rb, wg, wu, wd, layer, g, b, old=None):
    n = xr.shape[0] // ROW_CHUNKS
    n_blocks = (2 * n) // MOE_BM + N_EXPERTS
    tm = MOE_TILE
    nt = n // tm
    lpos, w, cnt, tcnt, toff, tbef = _router(xr, rw, rb, tm)
    meta, rstart = _plan(cnt, tbef, n_blocks)
    block_e = meta[0, :n_blocks]
    n_used = meta[1, :1]

    def per_tile(table):
        return table[:, :nt].T.reshape(nt * N_EXPERTS).astype(I32)

    lpos_flat = lpos.reshape(2 * n)
    tcnt, toff, rstart = per_tile(tcnt), per_tile(toff), per_tile(rstart)
    xs = _dispatch(lpos_flat, tcnt, toff, rstart, meta[2, :N_EXPERTS], meta[3, :N_EXPERTS], n_used, xr,
                   n_blocks, tm, old)
    y = _experts(block_e, n_used, xs, wg, wu, wd, layer)
    out = _combine_ln(lpos_flat, tcnt, toff, rstart, y, xr, w[0].reshape(n, 1), w[1].reshape(n, 1), g, b, tm)
    return out, y


def _conv_qkv_kernel(xm_ref, cw_ref, cb_ref, wq_ref, wk_ref, wv_ref, q_ref, k_ref, v_ref, xc_ref, *, s):
    xm_b = xm_ref[0]
    xm = xm_b.astype(F32)
    cw = cw_ref[...]
    row = lax.broadcasted_iota(I32, (s, 1), 0)
    half = CONV_K // 2
    acc = cb_ref[...] + xm * cw[half:half + 1, :]
    for sh in range(1, half + 1):
        past = jnp.where(row >= sh, pltpu.roll(xm, sh, axis=0), 0.0)
        acc = acc + past * cw[half - sh:half - sh + 1, :]
        nxt = jnp.where(row < s - sh, pltpu.roll(xm, s - sh, axis=0), 0.0)
        acc = acc + nxt * cw[half + sh:half + sh + 1, :]
    xc = _silu(acc).astype(BF16)
    xc_ref[0] = xc
    q_ref[0] = _dot(xc, wq_ref[0]).astype(BF16)
    k_ref[0] = (_dot_nt(wk_ref[0], xc) * (ML_DH ** -0.5)).astype(BF16)
    v = _dot(xm_b, wv_ref[0])
    ones_lane = lax.broadcasted_iota(I32, (1, ML_DHP), 1) == ML_DH
    v_ref[0] = jnp.where(ones_lane, 1.0, v).astype(BF16)


def _conv_qkv(main3, cw, cb, wq, wk_t, wv):
    b, s, _ = main3.shape
    tok = pl.BlockSpec((1, s, ML_DHP), lambda i, h: (i, 0, h))
    wspec = pl.BlockSpec((1, ML_DHP, ML_DHP), lambda i, h: (h, 0, 0))
    tok_shape = jax.ShapeDtypeStruct((b, s, ML_WP), BF16)
    return pl.pallas_call(
        functools.partial(_conv_qkv_kernel, s=s),
        grid=(b, ML_HEADS),
        in_specs=[tok,
                  pl.BlockSpec((CONV_K, ML_DHP), lambda i, h: (0, h)),
                  pl.BlockSpec((1, ML_DHP), lambda i, h: (0, h)),
                  wspec, wspec, wspec],
        out_specs=[tok, pl.BlockSpec((1, ML_DHP, s), lambda i, h: (i, h, 0)), tok, tok],
        out_shape=[tok_shape, jax.ShapeDtypeStruct((b, ML_WP, s), BF16), tok_shape, tok_shape],
        compiler_params=_cparams("parallel", "parallel"),
        name="conv_qkv",
    )(main3, cw, cb, wq, wk_t, wv)


def _mlstm_kernel(q_ref, kt_ref, v_ref, gc_ref, gr_ref, z_ref, xc_ref, ng_ref, sk_ref,
                  y_ref, hf_ref, hb_ref, cf_ref, cb_ref, m_ref, *, s):
    head0 = pl.program_id(1) * ML_HPS
    nc = s // CHUNK
    sub = lax.broadcasted_iota(I32, (LANES, 1), 0)
    gate = lax.broadcasted_iota(I32, (LANES, LANES), 0)
    ti = lax.broadcasted_iota(I32, (CHUNK, CHUNK), 0)
    tj = lax.broadcasted_iota(I32, (CHUNK, CHUNK), 1)

    for ref in (cf_ref, cb_ref, m_ref):
        ref[...] = jnp.zeros_like(ref)

    def intra(c, j, rev):
        t0 = pl.multiple_of(c * CHUNK, CHUNK)
        hl = slice(j * ML_DHP, (j + 1) * ML_DHP)
        qb = q_ref[0, pl.ds(t0, CHUNK), hl]
        kt = kt_ref[0, hl, pl.ds(t0, CHUNK)]
        vb = v_ref[0, pl.ds(t0, CHUNK), hl]
        gc = gc_ref[0, pl.ds(t0, CHUNK), :]
        gr = gr_ref[:, pl.ds(t0, CHUNK)]
        i_idx = head0 + j + (2 * ML_HEADS if rev else 0)
        f_idx = i_idx + ML_HEADS
        allowed = (tj >= ti) if rev else (tj <= ti)
        sel = (gate == f_idx).astype(BF16)
        b_rep = sum(_dot(part, sel) for part in _split3(gc))
        b_row = jnp.sum(jnp.where(sub == f_idx, gr, 0.0), axis=0, keepdims=True)
        i_row = jnp.sum(jnp.where(sub == i_idx, gr, 0.0), axis=0, keepdims=True)
        b_last = (b_rep[0:1, :] if rev else b_rep[CHUNK - 1:CHUNK, :])[:, 0:1]

        b_wide = jnp.concatenate([b_rep] * (CHUNK // LANES), axis=1)
        d = jnp.where(allowed, b_wide - b_row + i_row, NEG)
        m_in = jnp.max(d, axis=1, keepdims=True)
        sc = _dot(qb, kt) * jnp.exp(d - m_in)
        nd_in = _dot(sc.astype(BF16), vb)
        w_row = b_last - b_row + i_row
        return t0, qb, kt, vb, b_rep, b_last, m_in, nd_in, w_row

    def twice(a):
        return jnp.concatenate([a, a], axis=1)

    def update(parts, j, rev):
        t0, qb, kt, vb, b_rep, b_last, m_in, nd_in, w_row = parts
        h_ref, c_ref = (hb_ref, cb_ref) if rev else (hf_ref, cf_ref)
        hl = slice(j * ML_DHP, (j + 1) * ML_DHP)
        mrow = 2 * j + int(rev)
        m = m_ref[mrow:mrow + 1, 0:1]
        cmat = c_ref[j]
        inter = b_rep + m
        m_t = jnp.maximum(m_in, inter)
        a_in = jnp.exp(m_in - m_t)
        iexp = jnp.exp(inter - m_t)
        nd = twice(a_in) * nd_in + twice(iexp) * _dot(qb, cmat.astype(BF16))
        den = nd[:, ML_DH:ML_DH + 1]
        h_ref[pl.ds(t0, CHUNK), hl] = nd * (1.0 / jnp.maximum(jnp.abs(den), jnp.exp(-m_t[:, 0:1])))

        m_new = jnp.maximum(b_last + m, jnp.max(w_row, axis=1, keepdims=True))
        wexp = jnp.exp(w_row - m_new)
        cexp = jnp.exp(b_last + m - m_new)
        kw = (kt.astype(F32) * wexp).astype(BF16)
        c_ref[j] = cexp * cmat + _dot(kw, vb)
        m_ref[mrow:mrow + 1, :] = jnp.broadcast_to(m_new, (1, LANES))

    def step(i, carry):
        for j in range(ML_HPS):
            parts = [intra(nc - 1 - i if rev else i, j, rev) for rev in (False, True)]
            for p, rev in zip(parts, (False, True)):
                update(p, j, rev)
        return carry

    lax.fori_loop(0, nc, step, 0)

    real = lax.broadcasted_iota(I32, (1, ML_DHP), 1) < ML_DH
    tb = CHUNK

    def fin(c, carry):
        t0 = pl.multiple_of(c * tb, tb)
        for j in range(ML_HPS):
            hl = slice(j * ML_DHP, (j + 1) * ML_DHP)
            hs = jnp.where(real, hf_ref[pl.ds(t0, tb), hl] + hb_ref[pl.ds(t0, tb), hl], 0.0)
            mu = jnp.sum(hs, axis=1, keepdims=True) * (1.0 / ML_DH)
            dev = jnp.where(real, hs - mu, 0.0)
            var = jnp.sum(dev * dev, axis=1, keepdims=True) * (1.0 / ML_DH)
            hn = dev * lax.rsqrt(var + LN_EPS) * ng_ref[:, hl]
            xc = xc_ref[0, pl.ds(t0, tb), hl].astype(F32)
            z = z_ref[0, pl.ds(t0, tb), hl].astype(F32)
            y_ref[0, pl.ds(t0, tb), hl] = ((hn + sk_ref[:, hl] * xc) * _silu(z)).astype(BF16)
        return carry

    lax.fori_loop(0, s // tb, fin, 0)


def _mlstm(q, kt, v, gcol3, grow, main3, xc, ng, sk):
    b, s, _ = q.shape
    width = ML_HPS * ML_DHP
    steps = ML_HEADS // ML_HPS
    tok = pl.BlockSpec((1, s, width), lambda i, h: (i, 0, h))
    vec = pl.BlockSpec((1, width), lambda i, h: (0, h))
    return pl.pallas_call(
        functools.partial(_mlstm_kernel, s=s),
        grid=(b, steps),
        in_specs=[tok, pl.BlockSpec((1, width, s), lambda i, h: (i, h, 0)), tok,
                  pl.BlockSpec((1, s, LANES), lambda i, h: (i, 0, 0)),
                  pl.BlockSpec((LANES, s), lambda i, h: (0, i)),
                  pl.BlockSpec((1, s, width), lambda i, h: (i, 0, steps + h)),
                  tok, vec, vec],
        out_specs=tok,
        out_shape=jax.ShapeDtypeStruct((b, s, ML_WP), BF16),
        scratch_shapes=[pltpu.VMEM((s, width), F32), pltpu.VMEM((s, width), F32),
                        pltpu.VMEM((ML_HPS, ML_DHP, ML_DHP), F32), pltpu.VMEM((ML_HPS, ML_DHP, ML_DHP), F32),
                        pltpu.VMEM((SUBLANES, LANES), F32)],
        compiler_params=_cparams("parallel", "parallel"),
        name="mlstm",
    )(q, kt, v, gcol3, grow, main3, xc, ng, sk)


def _pad_heads(a, axis):
    a = jnp.moveaxis(a, axis, -1)
    lead = a.shape[:-1]
    a = a.reshape(lead + (ML_HEADS, ML_DH))
    a = jnp.pad(a, [(0, 0)] * len(lead) + [(0, 0), (0, ML_DHP - ML_DH)])
    return jnp.moveaxis(a.reshape(lead + (ML_WP,)), -1, axis)


def kernel(x, mem, mem_ln_g, mem_ln_b, w_mem_kv, router_w, router_b, na_w_in, na_rpb, ml_w_in, ml_conv_w,
           ml_conv_b, ml_w_qkv, ml_gate_b, ml_norm_g, ml_skip, w_out, ln_g, ln_b, exp_w_gate, exp_w_up,
           exp_w_down):
    b, s, d = x.shape
    n = b * s
    nm = mem.shape[1]
    row = lambda a: a.reshape(1, -1)

    mem_k, mem_v = _memkv(mem.reshape(b * nm, d), row(mem_ln_g), row(mem_ln_b), w_mem_kv.astype(BF16))
    mem_k3 = mem_k.reshape(b, nm, MEM_W)
    mem_v3 = mem_v.reshape(b, nm, MEM_W)
    rw_pad = jnp.pad(router_w, ((0, 0), (0, LANES - N_EXPERTS)))
    rw_hi = rw_pad.astype(BF16)
    rw = (rw_hi, (rw_pad - rw_hi.astype(F32)).astype(BF16))
    rb = router_b.reshape(N_EXPERTS, 1)

    x2 = x.reshape(n, d)

    h0 = _proj(x2, na_w_in[0].astype(BF16)).reshape(b, s, 3 * NA_W + MEM_W)
    y_na = _na_attention(h0, _na_bias_table(na_rpb[0]))
    wo = w_out[0].astype(BF16)
    xr = _outproj_ln(y_na.reshape(n, NA_W), h0.reshape(n, 3 * NA_W + MEM_W), 3 * NA_W // MEM_W, mem_k3, mem_v3,
                     wo[:NA_W], wo[NA_W:], x2, row(ln_g[0, 0]), row(ln_b[0, 0]))
    x2, moe_buf = _moe_ln(xr, rw, rb, exp_w_gate, exp_w_up, exp_w_down, 0, row(ln_g[0, 1]), row(ln_b[0, 1]))

    w1 = ml_w_in[0]
    w_main = jnp.concatenate([_pad_heads(w1[:, :ML_W], 1), _pad_heads(w1[:, ML_W:2 * ML_W], 1),
                              w1[:, 2 * ML_W + 4 * ML_HEADS:]], axis=1).astype(BF16)
    w_g = jnp.pad(w1[:, 2 * ML_W:2 * ML_W + 4 * ML_HEADS], ((0, 0), (0, LANES - 4 * ML_HEADS))).astype(BF16)
    gb = jnp.pad(ml_gate_b[0].reshape(4 * ML_HEADS), (0, LANES - 4 * ML_HEADS))
    main, acol, arow = _proj_gates(x2, w_main, w_g, w_g.T, gb.reshape(1, LANES), gb.reshape(LANES, 1))
    main3 = main.reshape(b, s, 2 * ML_WP + MEM_W)
    wqkv = jnp.pad(ml_w_qkv[0], ((0, 0), (0, 0), (0, ML_DHP - ML_DH), (0, ML_DHP - ML_DH))).astype(BF16)
    q, k, v, xc = _conv_qkv(main3, _pad_heads(ml_conv_w[0], 1), _pad_heads(row(ml_conv_b[0]), 1),
                            wqkv[0], jnp.swapaxes(wqkv[1], 1, 2), wqkv[2])
    y_ml = _mlstm(q, k, v, acol.reshape(b, s, LANES), arow, main3, xc,
                  _pad_heads(row(ml_norm_g[0]), 1), _pad_heads(row(ml_skip[0]), 1))
    wo = w_out[1]
    xr = _outproj_ln(y_ml.reshape(n, ML_WP), main, 2 * ML_WP // MEM_W, mem_k3, mem_v3,
                     _pad_heads(wo[:ML_W], 0).astype(BF16), wo[ML_W:].astype(BF16), x2,
                     row(ln_g[1, 0]), row(ln_b[1, 0]))
    x2, _ = _moe_ln(xr, rw, rb, exp_w_gate, exp_w_up, exp_w_down, 1, row(ln_g[1, 1]), row(ln_b[1, 1]), moe_buf)
    return x2.reshape(b, s, d)
```

```python
import functools

import numpy as np
import jax
import jax.numpy as jnp
from jax import lax
from jax.experimental import pallas as pl
from jax.experimental.pallas import tpu as pltpu

F32 = jnp.float32
BF16 = jnp.bfloat16
I32 = jnp.int32

D_MODEL = 1024
DEPTH = 2
GRID_W = 64
MEM_HEADS = 4
MEM_DH = 64
MEM_W = MEM_HEADS * MEM_DH
NA_HEADS = 12
NA_DH = 64
NA_W = NA_HEADS * NA_DH
WIN_H = 8
WIN_W = 16
ML_HEADS = 4
ML_DH = 192
ML_DHP = 256
ML_W = ML_HEADS * ML_DH
ML_WP = ML_HEADS * ML_DHP
CONV_K = 5
CHUNK = 256
N_EXPERTS = 16
N_GROUPS = 4
EXPERTS_PER_GROUP = N_EXPERTS // N_GROUPS
D_EXPERT = 512
ALPHA = (2 * DEPTH) ** 0.25
LN_EPS = 1e-5
NEG = -1e30

LANES = 128
SUBLANES = 8
ROW_CHUNKS = D_MODEL // LANES
MOE_BM = 512
MOE_TILE = 512
ML_HPS = 2
NA_ROWS_PER_STEP = 16
ROW_TILE = 1024
PLACE_UNROLL = 8
VMEM_LIMIT = 48 * 1024 * 1024


def _cparams(*sem):
    return pltpu.CompilerParams(dimension_semantics=sem, vmem_limit_bytes=VMEM_LIMIT)


def _dot(a, b):
    return jnp.dot(a, b, preferred_element_type=F32)


def _dot_nt(a, b, precision=None):
    return lax.dot_general(a, b, (((1,), (1,)), ((), ())), precision=precision,
                           preferred_element_type=F32)


def _ln(z, g, b):
    mu = jnp.mean(z, axis=-1, keepdims=True)
    zc = z - mu
    var = jnp.mean(zc * zc, axis=-1, keepdims=True)
    return zc * lax.rsqrt(var + LN_EPS) * g + b


def _silu(x):
    return x * jax.nn.sigmoid(x)


def _read_rows(ref, n):
    return jnp.concatenate([ref[pl.ds(j, n, stride=ROW_CHUNKS), :] for j in range(ROW_CHUNKS)], axis=1)


def _write_rows(ref, val, n):
    for j in range(ROW_CHUNKS):
        ref[pl.ds(j, n, stride=ROW_CHUNKS), :] = val[:, j * LANES:(j + 1) * LANES]


def _memkv_kernel(m_ref, g_ref, b_ref, w_ref, k_ref, v_ref):
    z = _ln(m_ref[...], g_ref[...], b_ref[...])
    kv = _dot(z.astype(BF16), w_ref[...])
    k_ref[...] = kv[:, :MEM_W].astype(BF16)
    v_ref[...] = kv[:, MEM_W:].astype(BF16)


def _memkv(mem2, g, b, w):
    n = mem2.shape[0]
    tm = min(ROW_TILE, n)
    return pl.pallas_call(
        _memkv_kernel,
        grid=(n // tm,),
        in_specs=[pl.BlockSpec((tm, D_MODEL), lambda i: (i, 0)),
                  pl.BlockSpec((1, D_MODEL), lambda i: (0, 0)),
                  pl.BlockSpec((1, D_MODEL), lambda i: (0, 0)),
                  pl.BlockSpec((D_MODEL, 2 * MEM_W), lambda i: (0, 0))],
        out_specs=[pl.BlockSpec((tm, MEM_W), lambda i: (i, 0)),
                   pl.BlockSpec((tm, MEM_W), lambda i: (i, 0))],
        out_shape=[jax.ShapeDtypeStruct((n, MEM_W), BF16)] * 2,
        compiler_params=_cparams("parallel"),
        name="memkv",
    )(mem2, g, b, w)


def _proj_kernel(x_ref, w_ref, o_ref):
    o_ref[...] = _dot(x_ref[...].astype(BF16), w_ref[...]).astype(o_ref.dtype)


def _proj(x2, w, tm=ROW_TILE):
    n, k = x2.shape
    nout = w.shape[1]
    return pl.pallas_call(
        _proj_kernel,
        grid=(n // tm,),
        in_specs=[pl.BlockSpec((tm, k), lambda i: (i, 0)),
                  pl.BlockSpec((k, nout), lambda i: (0, 0))],
        out_specs=pl.BlockSpec((tm, nout), lambda i: (i, 0)),
        out_shape=jax.ShapeDtypeStruct((n, nout), BF16),
        compiler_params=_cparams("parallel"),
        name="in_proj",
    )(x2, w)


def _split3(x):
    hi = x.astype(BF16)
    r1 = x - hi.astype(F32)
    mid = r1.astype(BF16)
    lo = (r1 - mid.astype(F32)).astype(BF16)
    return hi, mid, lo


def _proj_gates_kernel(x_ref, w_ref, wg_ref, wgt_ref, gbc_ref, gbr_ref, o_ref, g_ref, gt_ref, *, tm):
    xb = x_ref[...].astype(BF16)
    o_ref[...] = _dot(xb, w_ref[...]).astype(BF16)
    gcol = _dot(xb, wg_ref[...]) + gbc_ref[...]
    grow = _dot_nt(wgt_ref[...], xb) + gbr_ref[...]
    lane = lax.broadcasted_iota(I32, (1, LANES), 1)
    sub = lax.broadcasted_iota(I32, (LANES, 1), 0)
    ti = lax.broadcasted_iota(I32, (CHUNK, CHUNK), 0)
    tj = lax.broadcasted_iota(I32, (CHUNK, CHUNK), 1)
    lower = (tj <= ti).astype(BF16)
    upper = (ti <= tj).astype(BF16)

    def pick(idx, pre, suf, raw):
        fwd = jnp.logical_and(idx >= ML_HEADS, idx < 2 * ML_HEADS)
        bwd = jnp.logical_and(idx >= 3 * ML_HEADS, idx < 4 * ML_HEADS)
        return jnp.where(fwd, pre, jnp.where(bwd, suf, raw))

    for c in range(tm // CHUNK):
        tc = slice(c * CHUNK, (c + 1) * CHUNK)
        g = gcol[tc, :]
        ls = jax.nn.log_sigmoid(g)
        pre = sum(_dot(lower, part) for part in _split3(ls))
        suf = jnp.sum(ls, axis=0, keepdims=True) - pre + ls
        g_ref[tc, :] = pick(lane, pre, suf, g)
        g = grow[:, tc]
        ls = jax.nn.log_sigmoid(g)
        pre = sum(_dot(part, upper) for part in _split3(ls))
        suf = jnp.sum(ls, axis=1, keepdims=True) - pre + ls
        gt_ref[:, tc] = pick(sub, pre, suf, g)


def _proj_gates(x2, w, wg, wgt, gbc, gbr, tm=ROW_TILE):
    n, k = x2.shape
    nout = w.shape[1]
    return pl.pallas_call(
        functools.partial(_proj_gates_kernel, tm=tm),
        grid=(n // tm,),
        in_specs=[pl.BlockSpec((tm, k), lambda i: (i, 0)),
                  pl.BlockSpec((k, nout), lambda i: (0, 0)),
                  pl.BlockSpec((k, LANES), lambda i: (0, 0)),
                  pl.BlockSpec((LANES, k), lambda i: (0, 0)),
                  pl.BlockSpec((1, LANES), lambda i: (0, 0)),
                  pl.BlockSpec((LANES, 1), lambda i: (0, 0))],
        out_specs=[pl.BlockSpec((tm, nout), lambda i: (i, 0)),
                   pl.BlockSpec((tm, LANES), lambda i: (i, 0)),
                   pl.BlockSpec((LANES, tm), lambda i: (0, i))],
        out_shape=[jax.ShapeDtypeStruct((n, nout), BF16),
                   jax.ShapeDtypeStruct((n, LANES), F32),
                   jax.ShapeDtypeStruct((LANES, n), F32)],
        compiler_params=_cparams("parallel"),
        name="in_proj_gates",
    )(x2, w, wg, wgt, gbc, gbr)


def _na_kernel(q_ref, k_ref, v_ref, tbl_ref, o_ref, *, rows):
    lane = lax.broadcasted_iota(I32, (1, LANES), 1)
    first = lane < NA_DH
    nkeys = WIN_H * GRID_W

    def rows_step(i, carry):
        rr = [i * NA_ROWS_PER_STEP + u for u in range(NA_ROWS_PER_STEP)]
        rss = [jnp.clip(r - WIN_H // 2, 0, rows - WIN_H) for r in rr]
        scores = []
        for r, rs in zip(rr, rss):
            q = q_ref[0, pl.ds(pl.multiple_of(r * GRID_W, GRID_W), GRID_W), :]
            q = q * jnp.asarray(NA_DH ** -0.5, BF16)
            q2 = jnp.concatenate([jnp.where(first, q, jnp.zeros_like(q)),
                                  jnp.where(first, jnp.zeros_like(q), q)], axis=0)
            k = k_ref[0, pl.ds(pl.multiple_of(rs * GRID_W, GRID_W), nkeys), :]
            dr0 = rs - r + WIN_H - 1
            bias = jnp.concatenate(
                [jnp.concatenate([tbl_ref[0, half, dr0 + 2 * m] for m in range(WIN_H // 2)], axis=1)
                 for half in range(2)], axis=0)
            scores.append(_dot_nt(q2, k) + bias)
        probs = []
        for s in scores:
            p = jnp.exp(s - jnp.max(s, axis=-1, keepdims=True))
            probs.append((p.astype(BF16), jnp.sum(p, axis=-1, keepdims=True)))
        for r, rs, (p, l) in zip(rr, rss, probs):
            v = v_ref[0, pl.ds(pl.multiple_of(rs * GRID_W, GRID_W), nkeys), :]
            o = _dot(p, v) / l
            o = jnp.where(first, o[:GRID_W], o[GRID_W:])
            o_ref[0, pl.ds(pl.multiple_of(r * GRID_W, GRID_W), GRID_W), :] = o.astype(o_ref.dtype)
        return carry

    lax.fori_loop(0, rows // NA_ROWS_PER_STEP, rows_step, 0)


def _na_bias_table(rpb):
    qc = np.arange(GRID_W)[:, None]
    kc = np.arange(GRID_W)[None, :]
    cs = np.clip(qc - WIN_W // 2, 0, GRID_W - WIN_W)
    col_in = (kc >= cs) & (kc < cs + WIN_W)
    side = GRID_W - WIN_W
    wide = jnp.pad(rpb, ((0, 0), (0, 0), (side, side)))
    t = jnp.stack([wide[:, :, GRID_W - 1 - q:2 * GRID_W - 1 - q] for q in range(GRID_W)], axis=2)
    t = jnp.where(col_in, t, NEG).astype(F32)
    t2 = jnp.concatenate([t[:, :-1], t[:, 1:]], axis=-1)
    return t2.reshape(NA_HEADS // 2, 2, 2 * WIN_H - 2, GRID_W, 2 * GRID_W)


def _na_attention(h3, tbl):
    b, s, _ = h3.shape
    rows = s // GRID_W
    npair = NA_HEADS // 2
    return pl.pallas_call(
        functools.partial(_na_kernel, rows=rows),
        grid=(b, npair),
        in_specs=[pl.BlockSpec((1, s, LANES), lambda i, p: (i, 0, p)),
                  pl.BlockSpec((1, s, LANES), lambda i, p: (i, 0, npair + p)),
                  pl.BlockSpec((1, s, LANES), lambda i, p: (i, 0, 2 * npair + p)),
                  pl.BlockSpec((1, 2, 2 * WIN_H - 2, GRID_W, 2 * GRID_W), lambda i, p: (p, 0, 0, 0, 0))],
        out_specs=pl.BlockSpec((1, s, LANES), lambda i, p: (i, 0, p)),
        out_shape=jax.ShapeDtypeStruct((b, s, NA_W), BF16),
        compiler_params=_cparams("parallel", "parallel"),
        name="na_attention",
    )(h3, h3, h3, tbl)


def _outproj_ln_kernel(ya_ref, qm_ref, mk_ref, mv_ref, wa_ref, wm_ref, x_ref, g_ref, b_ref, or_ref, *, tm):
    lane = lax.broadcasted_iota(I32, (1, LANES), 1)
    first = lane < MEM_DH
    q = qm_ref[...] * jnp.asarray(MEM_DH ** -0.5, BF16)
    cols = [slice(p * LANES, (p + 1) * LANES) for p in range(MEM_HEADS // 2)]
    scores = []
    for c in cols:
        qp = q[:, c]
        q2 = jnp.concatenate([jnp.where(first, qp, jnp.zeros_like(qp)),
                              jnp.where(first, jnp.zeros_like(qp), qp)], axis=0)
        scores.append(_dot_nt(q2, mk_ref[0, :, c]))
    probs = []
    for s in scores:
        p = jnp.exp(s - jnp.max(s, axis=-1, keepdims=True))
        probs.append((p.astype(BF16), jnp.sum(p, axis=-1, keepdims=True)))
    outs = []
    for c, (p, l) in zip(cols, probs):
        o = _dot(p, mv_ref[0, :, c]) / l
        outs.append(jnp.where(first, o[:tm], o[tm:]))
    ym = jnp.concatenate(outs, axis=1).astype(BF16)
    acc = _dot(ya_ref[...], wa_ref[...]) + _dot(ym, wm_ref[...])
    _write_rows(or_ref, _ln(ALPHA * x_ref[...] + acc, g_ref[...], b_ref[...]), tm)


def _outproj_ln(ya, h2, qm_block, mem_k3, mem_v3, wa, wm, x2, g, b, tm=ROW_TILE):
    n = x2.shape[0]
    ka = ya.shape[1]
    nb, nm, _ = mem_k3.shape
    per_batch = n // nb // tm
    full = lambda shape: pl.BlockSpec(shape, lambda i: (0,) * len(shape))
    return pl.pallas_call(
        functools.partial(_outproj_ln_kernel, tm=tm),
        grid=(n // tm,),
        in_specs=[pl.BlockSpec((tm, ka), lambda i: (i, 0)),
                  pl.BlockSpec((tm, MEM_W), lambda i: (i, qm_block)),
                  pl.BlockSpec((1, nm, MEM_W), lambda i: (i // per_batch, 0, 0)),
                  pl.BlockSpec((1, nm, MEM_W), lambda i: (i // per_batch, 0, 0)),
                  full((ka, D_MODEL)), full((MEM_W, D_MODEL)),
                  pl.BlockSpec((tm, D_MODEL), lambda i: (i, 0)),
                  full((1, D_MODEL)), full((1, D_MODEL))],
        out_specs=pl.BlockSpec((tm * ROW_CHUNKS, LANES), lambda i: (i, 0)),
        out_shape=jax.ShapeDtypeStruct((n * ROW_CHUNKS, LANES), F32),
        compiler_params=_cparams("parallel"),
        name="outproj_ln",
    )(ya, h2, mem_k3, mem_v3, wa, wm, x2, g, b)


def _router_kernel(x_ref, rwh_ref, rwl_ref, rb_ref, lpos_ref, w_ref, cnt_ref, tcnt_ref, toff_ref, tbef_ref, *, tm):
    @pl.when(pl.program_id(0) == 0)
    def _():
        cnt_ref[...] = jnp.zeros_like(cnt_ref)

    x = _read_rows(x_ref, tm)
    xh = x.astype(BF16)
    xl = (x - xh.astype(F32)).astype(BF16)
    logits_t = _dot(xh, rwh_ref[...]) + (_dot(xh, rwl_ref[...]) + _dot(xl, rwh_ref[...]))
    logits = logits_t.T[:N_EXPERTS]
    scores = jax.nn.sigmoid(logits)
    biased = scores + rb_ref[...]
    bv = [biased[e:e + 1, :] for e in range(N_EXPERTS)]
    sv = [scores[e:e + 1, :] for e in range(N_EXPERTS)]

    grp = []
    for g in range(N_GROUPS):
        m = bv[g * EXPERTS_PER_GROUP:(g + 1) * EXPERTS_PER_GROUP]
        best = None
        for a in range(EXPERTS_PER_GROUP):
            for c in range(a + 1, EXPERTS_PER_GROUP):
                pair = m[a] + m[c]
                best = pair if best is None else jnp.maximum(best, pair)
        grp.append(best)
    gsel = jnp.zeros((1, tm), I32)
    gbest = grp[0]
    for g in range(1, N_GROUPS):
        better = grp[g] > gbest
        gsel = jnp.where(better, g, gsel)
        gbest = jnp.where(better, grp[g], gbest)

    def pick(vals, j):
        out = vals[j]
        for g in range(1, N_GROUPS):
            out = jnp.where(gsel == g, vals[g * EXPERTS_PER_GROUP + j], out)
        return out

    cb = [pick(bv, j) for j in range(EXPERTS_PER_GROUP)]
    cs = [pick(sv, j) for j in range(EXPERTS_PER_GROUP)]
    i1 = jnp.zeros((1, tm), I32)
    m1 = cb[0]
    s1 = cs[0]
    for j in range(1, EXPERTS_PER_GROUP):
        gt = cb[j] > m1
        i1 = jnp.where(gt, j, i1)
        m1 = jnp.where(gt, cb[j], m1)
        s1 = jnp.where(gt, cs[j], s1)
    i2 = jnp.zeros((1, tm), I32)
    m2 = jnp.full((1, tm), -jnp.inf, F32)
    s2 = jnp.zeros((1, tm), F32)
    for j in range(EXPERTS_PER_GROUP):
        ok = jnp.logical_and(i1 != j, cb[j] > m2)
        i2 = jnp.where(ok, j, i2)
        m2 = jnp.where(ok, cb[j], m2)
        s2 = jnp.where(ok, cs[j], s2)
    e1 = gsel * EXPERTS_PER_GROUP + i1
    e2 = gsel * EXPERTS_PER_GROUP + i2
    tot = s1 + s2
    w_ref[...] = jnp.concatenate([s1 / tot, s2 / tot], axis=0)

    i = pl.program_id(0)
    eio = lax.broadcasted_iota(I32, (N_EXPERTS, tm), 0)
    oh1 = eio == e1
    oh2 = eio == e2
    ohs = jnp.logical_or(oh1, oh2).astype(F32)
    before = (lax.broadcasted_iota(I32, (tm, tm), 0) < lax.broadcasted_iota(I32, (tm, tm), 1))
    pre = _dot(ohs.astype(BF16), before.astype(BF16))
    tile_cnt = jnp.sum(ohs, axis=1, keepdims=True)
    offs = []
    acc = jnp.zeros((1, 1), F32)
    for e in range(N_EXPERTS):
        offs.append(acc)
        acc = acc + tile_cnt[e:e + 1, :]
    tile_off = jnp.concatenate(offs, axis=0)
    pos = tile_off + pre
    p1 = jnp.sum(jnp.where(oh1, pos, 0.0), axis=0, keepdims=True)
    p2 = jnp.sum(jnp.where(oh2, pos, 0.0), axis=0, keepdims=True)
    lpos_ref[...] = jnp.concatenate([p1, p2], axis=0).astype(I32) * ROW_CHUNKS

    @pl.when(i == 0)
    def _():
        for ref in (tcnt_ref, toff_ref, tbef_ref):
            ref[...] = jnp.zeros_like(ref)

    here = lax.broadcasted_iota(I32, (1, LANES), 1) == i
    tcnt_ref[...] = jnp.where(here, tile_cnt, tcnt_ref[...])
    toff_ref[...] = jnp.where(here, tile_off, toff_ref[...])
    tbef_ref[...] = jnp.where(here, cnt_ref[:, 0:1], tbef_ref[...])
    cnt_ref[...] += tile_cnt


def _router(xr, rw, rb, tm):
    n = xr.shape[0] // ROW_CHUNKS
    assert n // tm <= LANES
    table = pl.BlockSpec((N_EXPERTS, LANES), lambda i: (0, 0))
    return pl.pallas_call(
        functools.partial(_router_kernel, tm=tm),
        grid=(n // tm,),
        in_specs=[pl.BlockSpec((tm * ROW_CHUNKS, LANES), lambda i: (i, 0)),
                  pl.BlockSpec((D_MODEL, LANES), lambda i: (0, 0)),
                  pl.BlockSpec((D_MODEL, LANES), lambda i: (0, 0)),
                  pl.BlockSpec((N_EXPERTS, 1), lambda i: (0, 0))],
        out_specs=[pl.BlockSpec((2, tm), lambda i: (0, i)),
                   pl.BlockSpec((2, tm), lambda i: (0, i)),
                   table, table, table, table],
        out_shape=[jax.ShapeDtypeStruct((2, n), I32),
                   jax.ShapeDtypeStruct((2, n), F32)]
                  + [jax.ShapeDtypeStruct((N_EXPERTS, LANES), F32)] * 4,
        compiler_params=_cparams("arbitrary"),
        name="router",
    )(xr, rw[0], rw[1], rb)


def _plan_kernel(cnt_ref, tbef_ref, meta_ref, rstart_ref, *, nbl):
    shift = MOE_BM.bit_length() - 1
    cnt = cnt_ref[...].astype(I32)
    padded = ((cnt + (MOE_BM - 1)) >> shift) << shift
    starts = []
    acc = jnp.zeros((1, LANES), I32)
    for e in range(N_EXPERTS):
        starts.append(acc)
        acc = acc + padded[e:e + 1, :]
    pad_start = jnp.concatenate(starts, axis=0)
    pad_end = pad_start + padded
    rstart_ref[...] = pad_start + tbef_ref[...].astype(I32)
    blk0 = lax.broadcasted_iota(I32, (N_EXPERTS, nbl), 1) * MOE_BM
    block_e = jnp.sum((pad_end[:, 0:1] <= blk0).astype(I32), axis=0, keepdims=True)
    block_e = jnp.minimum(block_e, N_EXPERTS - 1)
    n_used = jnp.broadcast_to(acc[:, 0:1] >> shift, (1, nbl))
    diag = lax.broadcasted_iota(I32, (N_EXPERTS, nbl), 0) == lax.broadcasted_iota(I32, (N_EXPERTS, nbl), 1)
    fill_lo = jnp.sum(jnp.where(diag, (pad_start + cnt)[:, 0:1], 0), axis=0, keepdims=True)
    fill_hi = jnp.sum(jnp.where(diag, pad_end[:, 0:1], 0), axis=0, keepdims=True)
    meta_ref[...] = jnp.concatenate([block_e, n_used, fill_lo, fill_hi, jnp.zeros((SUBLANES - 4, nbl), I32)],
                                    axis=0)


def _plan(cnt, tbef, n_blocks):
    nbl = -(-n_blocks // LANES) * LANES
    table = pl.BlockSpec((N_EXPERTS, LANES), lambda i: (0, 0))
    return pl.pallas_call(
        functools.partial(_plan_kernel, nbl=nbl),
        grid=(1,),
        in_specs=[table, table],
        out_specs=[pl.BlockSpec((SUBLANES, nbl), lambda i: (0, 0)), table],
        out_shape=[jax.ShapeDtypeStruct((SUBLANES, nbl), I32),
                   jax.ShapeDtypeStruct((N_EXPERTS, LANES), I32)],
        compiler_params=_cparams("arbitrary"),
        name="moe_plan",
    )(cnt, tbef)


def _rows(ref, row, nrows):
    return ref.at[pl.ds(pl.multiple_of(row * ROW_CHUNKS, ROW_CHUNKS), nrows * ROW_CHUNKS), :]


def _rows_wait(src_hbm, buf, sem):
    pltpu.make_async_copy(src_hbm.at[pl.ds(0, buf.shape[0]), :], buf, sem).wait()


def _copy_pieces(src, src_row, dst, dst_row, count, max_rows, sem, wait=False):
    bit = max_rows.bit_length() - 1
    while bit >= 0:
        size = 1 << bit
        done = (count >> (bit + 1)) << (bit + 1)

        @pl.when(((count >> bit) & 1) == 1)
        def _():
            cp = pltpu.make_async_copy(_rows(src, src_row + done, size), _rows(dst, dst_row + done, size), sem)
            cp.start()
            if wait:
                cp.wait()

        bit -= 1


def _tile_runs(tcnt_ref, toff_ref, rstart_ref, tile, buf, hbm, sem, *, to_hbm, tm):
    def per_expert(e, carry):
        k = tile * N_EXPERTS + e
        if to_hbm:
            _copy_pieces(buf, toff_ref[k], hbm, rstart_ref[k], tcnt_ref[k], tm, sem)
        else:
            _copy_pieces(hbm, rstart_ref[k], buf, toff_ref[k], tcnt_ref[k], tm, sem)
        return carry

    lax.fori_loop(0, N_EXPERTS, per_expert, 0)


def _dispatch_kernel(lpos_ref, tcnt_ref, toff_ref, rstart_ref, flo_ref, fhi_ref, nu_ref, x_ref, old_hbm, xs_hbm,
                     s0, s1, zbuf, sem, zsem, *, n, tm, n_blocks, reuse):
    del old_hbm
    i = pl.program_id(0)
    nt = pl.num_programs(0)
    bufs = (s0, s1)
    unroll = PLACE_UNROLL

    for slot in range(2):
        @pl.when(i % 2 == slot)
        def _():
            buf = bufs[slot]

            @pl.when(i >= 2)
            def _():
                _rows_wait(xs_hbm, buf, sem.at[slot])

            def place(c, carry):
                tok = i * tm + c * unroll
                src = pl.multiple_of(c * (unroll * ROW_CHUNKS), unroll * ROW_CHUNKS)
                for u in range(unroll):
                    v = x_ref[pl.ds(src + u * ROW_CHUNKS, ROW_CHUNKS), :]
                    for k in range(2):
                        p = lpos_ref[k * n + tok + u]
                        buf[pl.ds(pl.multiple_of(p, ROW_CHUNKS), ROW_CHUNKS), :] = v
                return carry

            lax.fori_loop(0, tm // unroll, place, 0)
            _tile_runs(tcnt_ref, toff_ref, rstart_ref, i, buf, xs_hbm, sem.at[slot], to_hbm=True, tm=tm)

    zero_rows = n_blocks * MOE_BM - 2 * n
    if not reuse:
        @pl.when(i == 0)
        def _():
            zbuf[...] = jnp.zeros_like(zbuf)
            for e in range(N_EXPERTS):
                _copy_pieces(zbuf, 0, xs_hbm, flo_ref[e], fhi_ref[e] - flo_ref[e], MOE_BM // 2, zsem)

            def zero_block(j, carry):
                pltpu.make_async_copy(zbuf, _rows(xs_hbm, j * MOE_BM, MOE_BM), zsem).start()
                return carry

            lax.fori_loop(nu_ref[0], n_blocks, zero_block, 0)

    @pl.when(i == nt - 1)
    def _():
        for slot in range(2):
            @pl.when(nt > slot)
            def _():
                _rows_wait(xs_hbm, bufs[slot], sem.at[slot])

        if not reuse:
            pltpu.make_async_copy(_rows(xs_hbm, 0, zero_rows), _rows(xs_hbm, 0, zero_rows), zsem).wait()


def _dispatch(lpos_flat, tcnt, toff, rstart, fill_lo, fill_hi, n_used, xr, n_blocks, tm, old=None):
    n = xr.shape[0] // ROW_CHUNKS
    reuse = old is not None
    if not reuse:
        old = jnp.zeros((SUBLANES, LANES), F32)
    return pl.pallas_call(
        functools.partial(_dispatch_kernel, n=n, tm=tm, n_blocks=n_blocks, reuse=reuse),
        grid_spec=pltpu.PrefetchScalarGridSpec(
            num_scalar_prefetch=7,
            grid=(n // tm,),
            in_specs=[pl.BlockSpec((tm * ROW_CHUNKS, LANES), lambda i, *_: (i, 0)),
                      pl.BlockSpec(memory_space=pl.ANY)],
            out_specs=pl.BlockSpec(memory_space=pl.ANY),
            scratch_shapes=[pltpu.VMEM((2 * tm * ROW_CHUNKS, LANES), F32),
                            pltpu.VMEM((2 * tm * ROW_CHUNKS, LANES), F32),
                            pltpu.VMEM((MOE_BM * ROW_CHUNKS, LANES), F32),
                            pltpu.SemaphoreType.DMA((2,)),
                            pltpu.SemaphoreType.DMA(())]),
        out_shape=jax.ShapeDtypeStruct((n_blocks * MOE_BM * ROW_CHUNKS, LANES), F32),
        input_output_aliases={8: 0} if reuse else {},
        compiler_params=_cparams("arbitrary"),
        name="moe_dispatch",
    )(lpos_flat, tcnt, toff, rstart, fill_lo, fill_hi, n_used, xr, old)


EXPERT_BUFS = 4


def _experts_kernel(be_ref, nu_ref, xs_hbm, wg_ref, wu_ref, wd_ref, y_hbm, wgb, wub, wdb, xin, yout,
                    sem_in, sem_out):
    j = pl.program_id(0)
    nb = pl.num_programs(0)
    n_used = nu_ref[0]
    used = j < n_used
    ahead = EXPERT_BUFS - 1

    def fetch(blk, slot):
        return pltpu.make_async_copy(_rows(xs_hbm, blk * MOE_BM, MOE_BM), xin.at[slot], sem_in.at[slot])

    def write_back(blk, slot):
        return pltpu.make_async_copy(yout.at[slot], _rows(y_hbm, blk * MOE_BM, MOE_BM), sem_out.at[slot])

    @pl.when(j == 0)
    def _():
        for d in range(ahead):
            @pl.when(d < n_used)
            def _():
                fetch(d, d).start()

    @pl.when(jnp.logical_and(used, jnp.logical_or(j == 0, be_ref[j] != be_ref[jnp.maximum(j - 1, 0)])))
    def _():
        wgb[...] = wg_ref[0, 0].astype(BF16)
        wub[...] = wu_ref[0, 0].astype(BF16)
        wdb[...] = wd_ref[0, 0].astype(BF16)

    for slot in range(EXPERT_BUFS):
        @pl.when(jnp.logical_and(used, j % EXPERT_BUFS == slot))
        def _():
            @pl.when(j >= EXPERT_BUFS)
            def _():
                write_back(j - EXPERT_BUFS, slot).wait()

            @pl.when(j + ahead < n_used)
            def _():
                fetch(j + ahead, (slot + ahead) % EXPERT_BUFS).start()

            fetch(j, slot).wait()
            x = _read_rows(xin.at[slot], MOE_BM).astype(BF16)
            h = _silu(_dot(x, wgb[...])) * _dot(x, wub[...])
            _write_rows(yout.at[slot], _dot(h.astype(BF16), wdb[...]), MOE_BM)
            write_back(j, slot).start()

    @pl.when(j == nb - 1)
    def _():
        for back in range(1, EXPERT_BUFS + 1):
            @pl.when(n_used - back >= 0)
            def _():
                blk = n_used - back
                pltpu.make_async_copy(yout.at[0], _rows(y_hbm, blk * MOE_BM, MOE_BM),
                                      sem_out.at[blk % EXPERT_BUFS]).wait()


def _experts(block_e, n_used, xs, wg, wu, wd, layer):
    n_blocks = block_e.shape[0]

    def last_used(j, nu):
        return jnp.minimum(j, nu[0] - 1)

    def wblk(j, be, nu):
        return (layer, be[last_used(j, nu)], 0, 0)

    return pl.pallas_call(
        _experts_kernel,
        grid_spec=pltpu.PrefetchScalarGridSpec(
            num_scalar_prefetch=2,
            grid=(n_blocks,),
            in_specs=[pl.BlockSpec(memory_space=pl.ANY),
                      pl.BlockSpec((1, 1, D_MODEL, D_EXPERT), wblk),
                      pl.BlockSpec((1, 1, D_MODEL, D_EXPERT), wblk),
                      pl.BlockSpec((1, 1, D_EXPERT, D_MODEL), wblk)],
            out_specs=pl.BlockSpec(memory_space=pl.ANY),
            scratch_shapes=[pltpu.VMEM((D_MODEL, D_EXPERT), BF16), pltpu.VMEM((D_MODEL, D_EXPERT), BF16),
                            pltpu.VMEM((D_EXPERT, D_MODEL), BF16),
                            pltpu.VMEM((EXPERT_BUFS, MOE_BM * ROW_CHUNKS, LANES), F32),
                            pltpu.VMEM((EXPERT_BUFS, MOE_BM * ROW_CHUNKS, LANES), F32),
                            pltpu.SemaphoreType.DMA((EXPERT_BUFS,)), pltpu.SemaphoreType.DMA((EXPERT_BUFS,))]),
        out_shape=jax.ShapeDtypeStruct(xs.shape, F32),
        input_output_aliases={2: 0},
        compiler_params=_cparams("arbitrary"),
        name="moe_experts",
    )(block_e, n_used, xs, wg, wu, wd)


def _combine_ln_kernel(lpos_ref, tcnt_ref, toff_ref, rstart_ref, y_hbm, x_ref, w1_ref, w2_ref, g_ref, b_ref, o_ref,
                       r0, r1, u1, u2, sem, *, n, tm):
    i = pl.program_id(0)
    nt = pl.num_programs(0)
    bufs = (r0, r1)
    unroll = PLACE_UNROLL

    def fetch(tile, slot):
        _tile_runs(tcnt_ref, toff_ref, rstart_ref, tile, bufs[slot], y_hbm, sem.at[slot], to_hbm=False, tm=tm)

    @pl.when(i == 0)
    def _():
        fetch(0, 0)

    for slot in range(2):
        @pl.when(i % 2 == slot)
        def _():
            @pl.when(i + 1 < nt)
            def _():
                fetch(i + 1, 1 - slot)

            buf = bufs[slot]
            _rows_wait(y_hbm, buf, sem.at[slot])

            def place(c, carry):
                tok = i * tm + c * unroll
                dst0 = pl.multiple_of(c * (unroll * ROW_CHUNKS), unroll * ROW_CHUNKS)
                for u in range(unroll):
                    dst = pl.ds(dst0 + u * ROW_CHUNKS, ROW_CHUNKS)
                    for k, out in enumerate((u1, u2)):
                        p = lpos_ref[k * n + tok + u]
                        out[dst, :] = buf[pl.ds(pl.multiple_of(p, ROW_CHUNKS), ROW_CHUNKS), :]
                return carry

            lax.fori_loop(0, tm // unroll, place, 0)
            moe = w1_ref[...] * _read_rows(u1, tm) + w2_ref[...] * _read_rows(u2, tm)
            o_ref[...] = _ln(ALPHA * _read_rows(x_ref, tm) + moe, g_ref[...], b_ref[...])


def _combine_ln(lpos_flat, tcnt, toff, rstart, y, xr, w1, w2, g, b, tm):
    n = xr.shape[0] // ROW_CHUNKS
    return pl.pallas_call(
        functools.partial(_combine_ln_kernel, n=n, tm=tm),
        grid_spec=pltpu.PrefetchScalarGridSpec(
            num_scalar_prefetch=4,
            grid=(n // tm,),
            in_specs=[pl.BlockSpec(memory_space=pl.ANY),
                      pl.BlockSpec((tm * ROW_CHUNKS, LANES), lambda i, *_: (i, 0)),
                      pl.BlockSpec((tm, 1), lambda i, *_: (i, 0)),
                      pl.BlockSpec((tm, 1), lambda i, *_: (i, 0)),
                      pl.BlockSpec((1, D_MODEL), lambda i, *_: (0, 0)),
                      pl.BlockSpec((1, D_MODEL), lambda i, *_: (0, 0))],
            out_specs=pl.BlockSpec((tm, D_MODEL), lambda i, *_: (i, 0)),
            scratch_shapes=[pltpu.VMEM((2 * tm * ROW_CHUNKS, LANES), F32)] * 2
                           + [pltpu.VMEM((tm * ROW_CHUNKS, LANES), F32)] * 2
                           + [pltpu.SemaphoreType.DMA((2,))]),
        out_shape=jax.ShapeDtypeStruct((n, D_MODEL), F32),
        compiler_params=_cparams("arbitrary"),
        name="moe_combine_ln",
    )(lpos_flat, tcnt, toff, rstart, y, xr, w1, w2, g, b)


def _moe_ln(xr, rw, rb, wg, wu, wd, layer, g, b, old=None):
    n = xr.shape[0] // ROW_CHUNKS
    n_blocks = (2 * n) // MOE_BM + N_EXPERTS
    tm = MOE_TILE
    nt = n // tm
    lpos, w, cnt, tcnt, toff, tbef = _router(xr, rw, rb, tm)
    meta, rstart = _plan(cnt, tbef, n_blocks)
    block_e = meta[0, :n_blocks]
    n_used = meta[1, :1]

    def per_tile(table):
        return table[:, :nt].T.reshape(nt * N_EXPERTS).astype(I32)

    lpos_flat = lpos.reshape(2 * n)
    tcnt, toff, rstart = per_tile(tcnt), per_tile(toff), per_tile(rstart)
    xs = _dispatch(lpos_flat, tcnt, toff, rstart, meta[2, :N_EXPERTS], meta[3, :N_EXPERTS], n_used, xr,
                   n_blocks, tm, old)
    y = _experts(block_e, n_used, xs, wg, wu, wd, layer)
    out = _combine_ln(lpos_flat, tcnt, toff, rstart, y, xr, w[0].reshape(n, 1), w[1].reshape(n, 1), g, b, tm)
    return out, y


def _conv_qkv_kernel(xm_ref, cw_ref, cb_ref, wq_ref, wk_ref, wv_ref, q_ref, k_ref, v_ref, xc_ref, *, s):
    xm_b = xm_ref[0]
    xm = xm_b.astype(F32)
    cw = cw_ref[...]
    row = lax.broadcasted_iota(I32, (s, 1), 0)
    half = CONV_K // 2
    acc = cb_ref[...] + xm * cw[half:half + 1, :]
    for sh in range(1, half + 1):
        past = jnp.where(row >= sh, pltpu.roll(xm, sh, axis=0), 0.0)
        acc = acc + past * cw[half - sh:half - sh + 1, :]
        nxt = jnp.where(row < s - sh, pltpu.roll(xm, s - sh, axis=0), 0.0)
        acc = acc + nxt * cw[half + sh:half + sh + 1, :]
    xc = _silu(acc).astype(BF16)
    xc_ref[0] = xc
    q_ref[0] = _dot(xc, wq_ref[0]).astype(BF16)
    k_ref[0] = (_dot_nt(wk_ref[0], xc) * (ML_DH ** -0.5)).astype(BF16)
    v = _dot(xm_b, wv_ref[0])
    ones_lane = lax.broadcasted_iota(I32, (1, ML_DHP), 1) == ML_DH
    v_ref[0] = jnp.where(ones_lane, 1.0, v).astype(BF16)


def _conv_qkv(main3, cw, cb, wq, wk_t, wv):
    b, s, _ = main3.shape
    tok = pl.BlockSpec((1, s, ML_DHP), lambda i, h: (i, 0, h))
    wspec = pl.BlockSpec((1, ML_DHP, ML_DHP), lambda i, h: (h, 0, 0))
    tok_shape = jax.ShapeDtypeStruct((b, s, ML_WP), BF16)
    return pl.pallas_call(
        functools.partial(_conv_qkv_kernel, s=s),
        grid=(b, ML_HEADS),
        in_specs=[tok,
                  pl.BlockSpec((CONV_K, ML_DHP), lambda i, h: (0, h)),
                  pl.BlockSpec((1, ML_DHP), lambda i, h: (0, h)),
                  wspec, wspec, wspec],
        out_specs=[tok, pl.BlockSpec((1, ML_DHP, s), lambda i, h: (i, h, 0)), tok, tok],
        out_shape=[tok_shape, jax.ShapeDtypeStruct((b, ML_WP, s), BF16), tok_shape, tok_shape],
        compiler_params=_cparams("parallel", "parallel"),
        name="conv_qkv",
    )(main3, cw, cb, wq, wk_t, wv)


def _mlstm_kernel(q_ref, kt_ref, v_ref, gc_ref, gr_ref, z_ref, xc_ref, ng_ref, sk_ref,
                  y_ref, hf_ref, hb_ref, cf_ref, cb_ref, m_ref, *, s):
    head0 = pl.program_id(1) * ML_HPS
    nc = s // CHUNK
    sub = lax.broadcasted_iota(I32, (LANES, 1), 0)
    gate = lax.broadcasted_iota(I32, (LANES, LANES), 0)
    ti = lax.broadcasted_iota(I32, (CHUNK, CHUNK), 0)
    tj = lax.broadcasted_iota(I32, (CHUNK, CHUNK), 1)

    for ref in (cf_ref, cb_ref, m_ref):
        ref[...] = jnp.zeros_like(ref)

    def intra(c, j, rev):
        t0 = pl.multiple_of(c * CHUNK, CHUNK)
        hl = slice(j * ML_DHP, (j + 1) * ML_DHP)
        qb = q_ref[0, pl.ds(t0, CHUNK), hl]
        kt = kt_ref[0, hl, pl.ds(t0, CHUNK)]
        vb = v_ref[0, pl.ds(t0, CHUNK), hl]
        gc = gc_ref[0, pl.ds(t0, CHUNK), :]
        gr = gr_ref[:, pl.ds(t0, CHUNK)]
        i_idx = head0 + j + (2 * ML_HEADS if rev else 0)
        f_idx = i_idx + ML_HEADS
        allowed = (tj >= ti) if rev else (tj <= ti)
        sel = (gate == f_idx).astype(BF16)
        b_rep = sum(_dot(part, sel) for part in _split3(gc))
        b_row = jnp.sum(jnp.where(sub == f_idx, gr, 0.0), axis=0, keepdims=True)
        i_row = jnp.sum(jnp.where(sub == i_idx, gr, 0.0), axis=0, keepdims=True)
        b_last = (b_rep[0:1, :] if rev else b_rep[CHUNK - 1:CHUNK, :])[:, 0:1]

        b_wide = jnp.concatenate([b_rep] * (CHUNK // LANES), axis=1)
        d = jnp.where(allowed, b_wide - b_row + i_row, NEG)
        m_in = jnp.max(d, axis=1, keepdims=True)
        sc = _dot(qb, kt) * jnp.exp(d - m_in)
        nd_in = _dot(sc.astype(BF16), vb)
        w_row = b_last - b_row + i_row
        return t0, qb, kt, vb, b_rep, b_last, m_in, nd_in, w_row

    def twice(a):
        return jnp.concatenate([a, a], axis=1)

    def update(parts, j, rev):
        t0, qb, kt, vb, b_rep, b_last, m_in, nd_in, w_row = parts
        h_ref, c_ref = (hb_ref, cb_ref) if rev else (hf_ref, cf_ref)
        hl = slice(j * ML_DHP, (j + 1) * ML_DHP)
        mrow = 2 * j + int(rev)
        m = m_ref[mrow:mrow + 1, 0:1]
        cmat = c_ref[j]
        inter = b_rep + m
        m_t = jnp.maximum(m_in, inter)
        a_in = jnp.exp(m_in - m_t)
        iexp = jnp.exp(inter - m_t)
        nd = twice(a_in) * nd_in + twice(iexp) * _dot(qb, cmat.astype(BF16))
        den = nd[:, ML_DH:ML_DH + 1]
        h_ref[pl.ds(t0, CHUNK), hl] = nd * (1.0 / jnp.maximum(jnp.abs(den), jnp.exp(-m_t[:, 0:1])))

        m_new = jnp.maximum(b_last + m, jnp.max(w_row, axis=1, keepdims=True))
        wexp = jnp.exp(w_row - m_new)
        cexp = jnp.exp(b_last + m - m_new)
        kw = (kt.astype(F32) * wexp).astype(BF16)
        c_ref[j] = cexp * cmat + _dot(kw, vb)
        m_ref[mrow:mrow + 1, :] = jnp.broadcast_to(m_new, (1, LANES))

    def step(i, carry):
        for j in range(ML_HPS):
            parts = [intra(nc - 1 - i if rev else i, j, rev) for rev in (False, True)]
            for p, rev in zip(parts, (False, True)):
                update(p, j, rev)
        return carry

    lax.fori_loop(0, nc, step, 0)

    real = lax.broadcasted_iota(I32, (1, ML_DHP), 1) < ML_DH
    tb = CHUNK

    def fin(c, carry):
        t0 = pl.multiple_of(c * tb, tb)
        for j in range(ML_HPS):
            hl = slice(j * ML_DHP, (j + 1) * ML_DHP)
            hs = jnp.where(real, hf_ref[pl.ds(t0, tb), hl] + hb_ref[pl.ds(t0, tb), hl], 0.0)
            mu = jnp.sum(hs, axis=1, keepdims=True) * (1.0 / ML_DH)
            dev = jnp.where(real, hs - mu, 0.0)
            var = jnp.sum(dev * dev, axis=1, keepdims=True) * (1.0 / ML_DH)
            hn = dev * lax.rsqrt(var + LN_EPS) * ng_ref[:, hl]
            xc = xc_ref[0, pl.ds(t0, tb), hl].astype(F32)
            z = z_ref[0, pl.ds(t0, tb), hl].astype(F32)
            y_ref[0, pl.ds(t0, tb), hl] = ((hn + sk_ref[:, hl] * xc) * _silu(z)).astype(BF16)
        return carry

    lax.fori_loop(0, s // tb, fin, 0)


def _mlstm(q, kt, v, gcol3, grow, main3, xc, ng, sk):
    b, s, _ = q.shape
    width = ML_HPS * ML_DHP
    steps = ML_HEADS // ML_HPS
    tok = pl.BlockSpec((1, s, width), lambda i, h: (i, 0, h))
    vec = pl.BlockSpec((1, width), lambda i, h: (0, h))
    return pl.pallas_call(
        functools.partial(_mlstm_kernel, s=s),
        grid=(b, steps),
        in_specs=[tok, pl.BlockSpec((1, width, s), lambda i, h: (i, h, 0)), tok,
                  pl.BlockSpec((1, s, LANES), lambda i, h: (i, 0, 0)),
                  pl.BlockSpec((LANES, s), lambda i, h: (0, i)),
                  pl.BlockSpec((1, s, width), lambda i, h: (i, 0, steps + h)),
                  tok, vec, vec],
        out_specs=tok,
        out_shape=jax.ShapeDtypeStruct((b, s, ML_WP), BF16),
        scratch_shapes=[pltpu.VMEM((s, width), F32), pltpu.VMEM((s, width), F32),
                        pltpu.VMEM((ML_HPS, ML_DHP, ML_DHP), F32), pltpu.VMEM((ML_HPS, ML_DHP, ML_DHP), F32),
                        pltpu.VMEM((SUBLANES, LANES), F32)],
        compiler_params=_cparams("parallel", "parallel"),
        name="mlstm",
    )(q, kt, v, gcol3, grow, main3, xc, ng, sk)


def _pad_heads(a, axis):
    a = jnp.moveaxis(a, axis, -1)
    lead = a.shape[:-1]
    a = a.reshape(lead + (ML_HEADS, ML_DH))
    a = jnp.pad(a, [(0, 0)] * len(lead) + [(0, 0), (0, ML_DHP - ML_DH)])
    return jnp.moveaxis(a.reshape(lead + (ML_WP,)), -1, axis)


def kernel(x, mem, mem_ln_g, mem_ln_b, w_mem_kv, router_w, router_b, na_w_in, na_rpb, ml_w_in, ml_conv_w,
           ml_conv_b, ml_w_qkv, ml_gate_b, ml_norm_g, ml_skip, w_out, ln_g, ln_b, exp_w_gate, exp_w_up,
           exp_w_down):
    b, s, d = x.shape
    n = b * s
    nm = mem.shape[1]
    row = lambda a: a.reshape(1, -1)

    mem_k, mem_v = _memkv(mem.reshape(b * nm, d), row(mem_ln_g), row(mem_ln_b), w_mem_kv.astype(BF16))
    mem_k3 = mem_k.reshape(b, nm, MEM_W)
    mem_v3 = mem_v.reshape(b, nm, MEM_W)
    rw_pad = jnp.pad(router_w, ((0, 0), (0, LANES - N_EXPERTS)))
    rw_hi = rw_pad.astype(BF16)
    rw = (rw_hi, (rw_pad - rw_hi.astype(F32)).astype(BF16))
    rb = router_b.reshape(N_EXPERTS, 1)

    x2 = x.reshape(n, d)

    h0 = _proj(x2, na_w_in[0].astype(BF16)).reshape(b, s, 3 * NA_W + MEM_W)
    y_na = _na_attention(h0, _na_bias_table(na_rpb[0]))
    wo = w_out[0].astype(BF16)
    xr = _outproj_ln(y_na.reshape(n, NA_W), h0.reshape(n, 3 * NA_W + MEM_W), 3 * NA_W // MEM_W, mem_k3, mem_v3,
                     wo[:NA_W], wo[NA_W:], x2, row(ln_g[0, 0]), row(ln_b[0, 0]))
    x2, moe_buf = _moe_ln(xr, rw, rb, exp_w_gate, exp_w_up, exp_w_down, 0, row(ln_g[0, 1]), row(ln_b[0, 1]))

    w1 = ml_w_in[0]
    w_main = jnp.concatenate([_pad_heads(w1[:, :ML_W], 1), _pad_heads(w1[:, ML_W:2 * ML_W], 1),
                              w1[:, 2 * ML_W + 4 * ML_HEADS:]], axis=1).astype(BF16)
    w_g = jnp.pad(w1[:, 2 * ML_W:2 * ML_W + 4 * ML_HEADS], ((0, 0), (0, LANES - 4 * ML_HEADS))).astype(BF16)
    gb = jnp.pad(ml_gate_b[0].reshape(4 * ML_HEADS), (0, LANES - 4 * ML_HEADS))
    main, acol, arow = _proj_gates(x2, w_main, w_g, w_g.T, gb.reshape(1, LANES), gb.reshape(LANES, 1))
    main3 = main.reshape(b, s, 2 * ML_WP + MEM_W)
    wqkv = jnp.pad(ml_w_qkv[0], ((0, 0), (0, 0), (0, ML_DHP - ML_DH), (0, ML_DHP - ML_DH))).astype(BF16)
    q, k, v, xc = _conv_qkv(main3, _pad_heads(ml_conv_w[0], 1), _pad_heads(row(ml_conv_b[0]), 1),
                            wqkv[0], jnp.swapaxes(wqkv[1], 1, 2), wqkv[2])
    y_ml = _mlstm(q, k, v, acol.reshape(b, s, LANES), arow, main3, xc,
                  _pad_heads(row(ml_norm_g[0]), 1), _pad_heads(row(ml_skip[0]), 1))
    wo = w_out[1]
    xr = _outproj_ln(y_ml.reshape(n, ML_WP), main, 2 * ML_WP // MEM_W, mem_k3, mem_v3,
                     _pad_heads(wo[:ML_W], 0).astype(BF16), wo[ML_W:].astype(BF16), x2,
                     row(ln_g[1, 0]), row(ln_b[1, 0]))
    x2, _ = _moe_ln(xr, rw, rb, exp_w_gate, exp_w_up, exp_w_down, 1, row(ln_g[1, 1]), row(ln_b[1, 1]), moe_buf)
    return x2.reshape(b, s, d)
```

```python
import functools

import numpy as np
import jax
import jax.numpy as jnp
from jax import lax
from jax.experimental import pallas as pl
from jax.experimental.pallas import tpu as pltpu

F32 = jnp.float32
BF16 = jnp.bfloat16
I32 = jnp.int32

D_MODEL = 1024
DEPTH = 2
GRID_W = 64
MEM_HEADS = 4
MEM_DH = 64
MEM_W = MEM_HEADS * MEM_DH
NA_HEADS = 12
NA_DH = 64
NA_W = NA_HEADS * NA_DH
WIN_H = 8
WIN_W = 16
ML_HEADS = 4
ML_DH = 192
ML_DHP = 256
ML_W = ML_HEADS * ML_DH
ML_WP = ML_HEADS * ML_DHP
CONV_K = 5
CHUNK = 256
N_EXPERTS = 16
N_GROUPS = 4
EXPERTS_PER_GROUP = N_EXPERTS // N_GROUPS
D_EXPERT = 512
ALPHA = (2 * DEPTH) ** 0.25
LN_EPS = 1e-5
NEG = -1e30

LANES = 128
SUBLANES = 8
ROW_CHUNKS = D_MODEL // LANES
MOE_BM = 512
MOE_TILE = 512
ML_HPS = 2
NA_ROWS_PER_STEP = 16
ROW_TILE = 1024
PLACE_UNROLL = 8
VMEM_LIMIT = 48 * 1024 * 1024


def _cparams(*sem):
    return pltpu.CompilerParams(dimension_semantics=sem, vmem_limit_bytes=VMEM_LIMIT)


def _dot(a, b):
    return jnp.dot(a, b, preferred_element_type=F32)


def _dot_nt(a, b, precision=None):
    return lax.dot_general(a, b, (((1,), (1,)), ((), ())), precision=precision,
                           preferred_element_type=F32)


def _ln(z, g, b):
    mu = jnp.mean(z, axis=-1, keepdims=True)
    zc = z - mu
    var = jnp.mean(zc * zc, axis=-1, keepdims=True)
    return zc * lax.rsqrt(var + LN_EPS) * g + b


def _silu(x):
    return x * jax.nn.sigmoid(x)


def _read_rows(ref, n):
    return jnp.concatenate([ref[pl.ds(j, n, stride=ROW_CHUNKS), :] for j in range(ROW_CHUNKS)], axis=1)


def _write_rows(ref, val, n):
    for j in range(ROW_CHUNKS):
        ref[pl.ds(j, n, stride=ROW_CHUNKS), :] = val[:, j * LANES:(j + 1) * LANES]


def _memkv_kernel(m_ref, g_ref, b_ref, w_ref, k_ref, v_ref):
    z = _ln(m_ref[...], g_ref[...], b_ref[...])
    kv = _dot(z.astype(BF16), w_ref[...])
    k_ref[...] = kv[:, :MEM_W].astype(BF16)
    v_ref[...] = kv[:, MEM_W:].astype(BF16)


def _memkv(mem2, g, b, w):
    n = mem2.shape[0]
    tm = min(ROW_TILE, n)
    return pl.pallas_call(
        _memkv_kernel,
        grid=(n // tm,),
        in_specs=[pl.BlockSpec((tm, D_MODEL), lambda i: (i, 0)),
                  pl.BlockSpec((1, D_MODEL), lambda i: (0, 0)),
                  pl.BlockSpec((1, D_MODEL), lambda i: (0, 0)),
                  pl.BlockSpec((D_MODEL, 2 * MEM_W), lambda i: (0, 0))],
        out_specs=[pl.BlockSpec((tm, MEM_W), lambda i: (i, 0)),
                   pl.BlockSpec((tm, MEM_W), lambda i: (i, 0))],
        out_shape=[jax.ShapeDtypeStruct((n, MEM_W), BF16)] * 2,
        compiler_params=_cparams("parallel"),
        name="memkv",
    )(mem2, g, b, w)


def _proj_kernel(x_ref, w_ref, o_ref):
    o_ref[...] = _dot(x_ref[...].astype(BF16), w_ref[...]).astype(o_ref.dtype)


def _proj(x2, w, tm=ROW_TILE):
    n, k = x2.shape
    nout = w.shape[1]
    return pl.pallas_call(
        _proj_kernel,
        grid=(n // tm,),
        in_specs=[pl.BlockSpec((tm, k), lambda i: (i, 0)),
                  pl.BlockSpec((k, nout), lambda i: (0, 0))],
        out_specs=pl.BlockSpec((tm, nout), lambda i: (i, 0)),
        out_shape=jax.ShapeDtypeStruct((n, nout), BF16),
        compiler_params=_cparams("parallel"),
        name="in_proj",
    )(x2, w)


def _split3(x):
    hi = x.astype(BF16)
    r1 = x - hi.astype(F32)
    mid = r1.astype(BF16)
    lo = (r1 - mid.astype(F32)).astype(BF16)
    return hi, mid, lo


def _proj_gates_kernel(x_ref, w_ref, wg_ref, wgt_ref, gbc_ref, gbr_ref, o_ref, g_ref, gt_ref, *, tm):
    xb = x_ref[...].astype(BF16)
    o_ref[...] = _dot(xb, w_ref[...]).astype(BF16)
    gcol = _dot(xb, wg_ref[...]) + gbc_ref[...]
    grow = _dot_nt(wgt_ref[...], xb) + gbr_ref[...]
    lane = lax.broadcasted_iota(I32, (1, LANES), 1)
    sub = lax.broadcasted_iota(I32, (LANES, 1), 0)
    ti = lax.broadcasted_iota(I32, (CHUNK, CHUNK), 0)
    tj = lax.broadcasted_iota(I32, (CHUNK, CHUNK), 1)
    lower = (tj <= ti).astype(BF16)
    upper = (ti <= tj).astype(BF16)

    def pick(idx, pre, suf, raw):
        fwd = jnp.logical_and(idx >= ML_HEADS, idx < 2 * ML_HEADS)
        bwd = jnp.logical_and(idx >= 3 * ML_HEADS, idx < 4 * ML_HEADS)
        return jnp.where(fwd, pre, jnp.where(bwd, suf, raw))

    for c in range(tm // CHUNK):
        tc = slice(c * CHUNK, (c + 1) * CHUNK)
        g = gcol[tc, :]
        ls = jax.nn.log_sigmoid(g)
        pre = sum(_dot(lower, part) for part in _split3(ls))
        suf = jnp.sum(ls, axis=0, keepdims=True) - pre + ls
        g_ref[tc, :] = pick(lane, pre, suf, g)
        g = grow[:, tc]
        ls = jax.nn.log_sigmoid(g)
        pre = sum(_dot(part, upper) for part in _split3(ls))
        suf = jnp.sum(ls, axis=1, keepdims=True) - pre + ls
        gt_ref[:, tc] = pick(sub, pre, suf, g)


def _proj_gates(x2, w, wg, wgt, gbc, gbr, tm=ROW_TILE):
    n, k = x2.shape
    nout = w.shape[1]
    return pl.pallas_call(
        functools.partial(_proj_gates_kernel, tm=tm),
        grid=(n // tm,),
        in_specs=[pl.BlockSpec((tm, k), lambda i: (i, 0)),
                  pl.BlockSpec((k, nout), lambda i: (0, 0)),
                  pl.BlockSpec((k, LANES), lambda i: (0, 0)),
                  pl.BlockSpec((LANES, k), lambda i: (0, 0)),
                  pl.BlockSpec((1, LANES), lambda i: (0, 0)),
                  pl.BlockSpec((LANES, 1), lambda i: (0, 0))],
        out_specs=[pl.BlockSpec((tm, nout), lambda i: (i, 0)),
                   pl.BlockSpec((tm, LANES), lambda i: (i, 0)),
                   pl.BlockSpec((LANES, tm), lambda i: (0, i))],
        out_shape=[jax.ShapeDtypeStruct((n, nout), BF16),
                   jax.ShapeDtypeStruct((n, LANES), F32),
                   jax.ShapeDtypeStruct((LANES, n), F32)],
        compiler_params=_cparams("parallel"),
        name="in_proj_gates",
    )(x2, w, wg, wgt, gbc, gbr)


def _na_kernel(q_ref, k_ref, v_ref, tbl_ref, o_ref, *, rows):
    lane = lax.broadcasted_iota(I32, (1, LANES), 1)
    first = lane < NA_DH
    nkeys = WIN_H * GRID_W

    def rows_step(i, carry):
        rr = [i * NA_ROWS_PER_STEP + u for u in range(NA_ROWS_PER_STEP)]
        rss = [jnp.clip(r - WIN_H // 2, 0, rows - WIN_H) for r in rr]
        scores = []
        for r, rs in zip(rr, rss):
            q = q_ref[0, pl.ds(pl.multiple_of(r * GRID_W, GRID_W), GRID_W), :]
            q = q * jnp.asarray(NA_DH ** -0.5, BF16)
            q2 = jnp.concatenate([jnp.where(first, q, jnp.zeros_like(q)),
                                  jnp.where(first, jnp.zeros_like(q), q)], axis=0)
            k = k_ref[0, pl.ds(pl.multiple_of(rs * GRID_W, GRID_W), nkeys), :]
            dr0 = rs - r + WIN_H - 1
            bias = jnp.concatenate(
                [jnp.concatenate([tbl_ref[0, half, dr0 + 2 * m] for m in range(WIN_H // 2)], axis=1)
                 for half in range(2)], axis=0)
            scores.append(_dot_nt(q2, k) + bias)
        probs = []
        for s in scores:
            p = jnp.exp(s - jnp.max(s, axis=-1, keepdims=True))
            probs.append((p.astype(BF16), jnp.sum(p, axis=-1, keepdims=True)))
        for r, rs, (p, l) in zip(rr, rss, probs):
            v = v_ref[0, pl.ds(pl.multiple_of(rs * GRID_W, GRID_W), nkeys), :]
            o = _dot(p, v) / l
            o = jnp.where(first, o[:GRID_W], o[GRID_W:])
            o_ref[0, pl.ds(pl.multiple_of(r * GRID_W, GRID_W), GRID_W), :] = o.astype(o_ref.dtype)
        return carry

    lax.fori_loop(0, rows // NA_ROWS_PER_STEP, rows_step, 0)


def _na_bias_table(rpb):
    qc = np.arange(GRID_W)[:, None]
    kc = np.arange(GRID_W)[None, :]
    cs = np.clip(qc - WIN_W // 2, 0, GRID_W - WIN_W)
    col_in = (kc >= cs) & (kc < cs + WIN_W)
    side = GRID_W - WIN_W
    wide = jnp.pad(rpb, ((0, 0), (0, 0), (side, side)))
    t = jnp.stack([wide[:, :, GRID_W - 1 - q:2 * GRID_W - 1 - q] for q in range(GRID_W)], axis=2)
    t = jnp.where(col_in, t, NEG).astype(F32)
    t2 = jnp.concatenate([t[:, :-1], t[:, 1:]], axis=-1)
    return t2.reshape(NA_HEADS // 2, 2, 2 * WIN_H - 2, GRID_W, 2 * GRID_W)


def _na_attention(h3, tbl):
    b, s, _ = h3.shape
    rows = s // GRID_W
    npair = NA_HEADS // 2
    return pl.pallas_call(
        functools.partial(_na_kernel, rows=rows),
        grid=(b, npair),
        in_specs=[pl.BlockSpec((1, s, LANES), lambda i, p: (i, 0, p)),
                  pl.BlockSpec((1, s, LANES), lambda i, p: (i, 0, npair + p)),
                  pl.BlockSpec((1, s, LANES), lambda i, p: (i, 0, 2 * npair + p)),
                  pl.BlockSpec((1, 2, 2 * WIN_H - 2, GRID_W, 2 * GRID_W), lambda i, p: (p, 0, 0, 0, 0))],
        out_specs=pl.BlockSpec((1, s, LANES), lambda i, p: (i, 0, p)),
        out_shape=jax.ShapeDtypeStruct((b, s, NA_W), BF16),
        compiler_params=_cparams("parallel", "parallel"),
        name="na_attention",
    )(h3, h3, h3, tbl)


def _outproj_ln_kernel(ya_ref, qm_ref, mk_ref, mv_ref, wa_ref, wm_ref, x_ref, g_ref, b_ref, or_ref, *, tm):
    lane = lax.broadcasted_iota(I32, (1, LANES), 1)
    first = lane < MEM_DH
    q = qm_ref[...] * jnp.asarray(MEM_DH ** -0.5, BF16)
    cols = [slice(p * LANES, (p + 1) * LANES) for p in range(MEM_HEADS // 2)]
    scores = []
    for c in cols:
        qp = q[:, c]
        q2 = jnp.concatenate([jnp.where(first, qp, jnp.zeros_like(qp)),
                              jnp.where(first, jnp.zeros_like(qp), qp)], axis=0)
        scores.append(_dot_nt(q2, mk_ref[0, :, c]))
    probs = []
    for s in scores:
        p = jnp.exp(s - jnp.max(s, axis=-1, keepdims=True))
        probs.append((p.astype(BF16), jnp.sum(p, axis=-1, keepdims=True)))
    outs = []
    for c, (p, l) in zip(cols, probs):
        o = _dot(p, mv_ref[0, :, c]) / l
        outs.append(jnp.where(first, o[:tm], o[tm:]))
    ym = jnp.concatenate(outs, axis=1).astype(BF16)
    acc = _dot(ya_ref[...], wa_ref[...]) + _dot(ym, wm_ref[...])
    _write_rows(or_ref, _ln(ALPHA * x_ref[...] + acc, g_ref[...], b_ref[...]), tm)


def _outproj_ln(ya, h2, qm_block, mem_k3, mem_v3, wa, wm, x2, g, b, tm=ROW_TILE):
    n = x2.shape[0]
    ka = ya.shape[1]
    nb, nm, _ = mem_k3.shape
    per_batch = n // nb // tm
    full = lambda shape: pl.BlockSpec(shape, lambda i: (0,) * len(shape))
    return pl.pallas_call(
        functools.partial(_outproj_ln_kernel, tm=tm),
        grid=(n // tm,),
        in_specs=[pl.BlockSpec((tm, ka), lambda i: (i, 0)),
                  pl.BlockSpec((tm, MEM_W), lambda i: (i, qm_block)),
                  pl.BlockSpec((1, nm, MEM_W), lambda i: (i // per_batch, 0, 0)),
                  pl.BlockSpec((1, nm, MEM_W), lambda i: (i // per_batch, 0, 0)),
                  full((ka, D_MODEL)), full((MEM_W, D_MODEL)),
                  pl.BlockSpec((tm, D_MODEL), lambda i: (i, 0)),
                  full((1, D_MODEL)), full((1, D_MODEL))],
        out_specs=pl.BlockSpec((tm * ROW_CHUNKS, LANES), lambda i: (i, 0)),
        out_shape=jax.ShapeDtypeStruct((n * ROW_CHUNKS, LANES), F32),
        compiler_params=_cparams("parallel"),
        name="outproj_ln",
    )(ya, h2, mem_k3, mem_v3, wa, wm, x2, g, b)


def _router_kernel(x_ref, rwh_ref, rwl_ref, rb_ref, lpos_ref, w_ref, cnt_ref, tcnt_ref, toff_ref, tbef_ref, *, tm):
    @pl.when(pl.program_id(0) == 0)
    def _():
        cnt_ref[...] = jnp.zeros_like(cnt_ref)

    x = _read_rows(x_ref, tm)
    xh = x.astype(BF16)
    xl = (x - xh.astype(F32)).astype(BF16)
    logits_t = _dot(xh, rwh_ref[...]) + (_dot(xh, rwl_ref[...]) + _dot(xl, rwh_ref[...]))
    logits = logits_t.T[:N_EXPERTS]
    scores = jax.nn.sigmoid(logits)
    biased = scores + rb_ref[...]
    bv = [biased[e:e + 1, :] for e in range(N_EXPERTS)]
    sv = [scores[e:e + 1, :] for e in range(N_EXPERTS)]

    grp = []
    for g in range(N_GROUPS):
        m = bv[g * EXPERTS_PER_GROUP:(g + 1) * EXPERTS_PER_GROUP]
        best = None
        for a in range(EXPERTS_PER_GROUP):
            for c in range(a + 1, EXPERTS_PER_GROUP):
                pair = m[a] + m[c]
                best = pair if best is None else jnp.maximum(best, pair)
        grp.append(best)
    gsel = jnp.zeros((1, tm), I32)
    gbest = grp[0]
    for g in range(1, N_GROUPS):
        better = grp[g] > gbest
        gsel = jnp.where(better, g, gsel)
        gbest = jnp.where(better, grp[g], gbest)

    def pick(vals, j):
        out = vals[j]
        for g in range(1, N_GROUPS):
            out = jnp.where(gsel == g, vals[g * EXPERTS_PER_GROUP + j], out)
        return out

    cb = [pick(bv, j) for j in range(EXPERTS_PER_GROUP)]
    cs = [pick(sv, j) for j in range(EXPERTS_PER_GROUP)]
    i1 = jnp.zeros((1, tm), I32)
    m1 = cb[0]
    s1 = cs[0]
    for j in range(1, EXPERTS_PER_GROUP):
        gt = cb[j] > m1
        i1 = jnp.where(gt, j, i1)
        m1 = jnp.where(gt, cb[j], m1)
        s1 = jnp.where(gt, cs[j], s1)
    i2 = jnp.zeros((1, tm), I32)
    m2 = jnp.full((1, tm), -jnp.inf, F32)
    s2 = jnp.zeros((1, tm), F32)
    for j in range(EXPERTS_PER_GROUP):
        ok = jnp.logical_and(i1 != j, cb[j] > m2)
        i2 = jnp.where(ok, j, i2)
        m2 = jnp.where(ok, cb[j], m2)
        s2 = jnp.where(ok, cs[j], s2)
    e1 = gsel * EXPERTS_PER_GROUP + i1
    e2 = gsel * EXPERTS_PER_GROUP + i2
    tot = s1 + s2
    w_ref[...] = jnp.concatenate([s1 / tot, s2 / tot], axis=0)

    i = pl.program_id(0)
    eio = lax.broadcasted_iota(I32, (N_EXPERTS, tm), 0)
    oh1 = eio == e1
    oh2 = eio == e2
    ohs = jnp.logical_or(oh1, oh2).astype(F32)
    before = (lax.broadcasted_iota(I32, (tm, tm), 0) < lax.broadcasted_iota(I32, (tm, tm), 1))
    pre = _dot(ohs.astype(BF16), before.astype(BF16))
    tile_cnt = jnp.sum(ohs, axis=1, keepdims=True)
    offs = []
    acc = jnp.zeros((1, 1), F32)
    for e in range(N_EXPERTS):
        offs.append(acc)
        acc = acc + tile_cnt[e:e + 1, :]
    tile_off = jnp.concatenate(offs, axis=0)
    pos = tile_off + pre
    p1 = jnp.sum(jnp.where(oh1, pos, 0.0), axis=0, keepdims=True)
    p2 = jnp.sum(jnp.where(oh2, pos, 0.0), axis=0, keepdims=True)
    lpos_ref[...] = jnp.concatenate([p1, p2], axis=0).astype(I32) * ROW_CHUNKS

    @pl.when(i == 0)
    def _():
        for ref in (tcnt_ref, toff_ref, tbef_ref):
            ref[...] = jnp.zeros_like(ref)

    here = lax.broadcasted_iota(I32, (1, LANES), 1) == i
    tcnt_ref[...] = jnp.where(here, tile_cnt, tcnt_ref[...])
    toff_ref[...] = jnp.where(here, tile_off, toff_ref[...])
    tbef_ref[...] = jnp.where(here, cnt_ref[:, 0:1], tbef_ref[...])
    cnt_ref[...] += tile_cnt


def _router(xr, rw, rb, tm):
    n = xr.shape[0] // ROW_CHUNKS
    assert n // tm <= LANES
    table = pl.BlockSpec((N_EXPERTS, LANES), lambda i: (0, 0))
    return pl.pallas_call(
        functools.partial(_router_kernel, tm=tm),
        grid=(n // tm,),
        in_specs=[pl.BlockSpec((tm * ROW_CHUNKS, LANES), lambda i: (i, 0)),
                  pl.BlockSpec((D_MODEL, LANES), lambda i: (0, 0)),
                  pl.BlockSpec((D_MODEL, LANES), lambda i: (0, 0)),
                  pl.BlockSpec((N_EXPERTS, 1), lambda i: (0, 0))],
        out_specs=[pl.BlockSpec((2, tm), lambda i: (0, i)),
                   pl.BlockSpec((2, tm), lambda i: (0, i)),
                   table, table, table, table],
        out_shape=[jax.ShapeDtypeStruct((2, n), I32),
                   jax.ShapeDtypeStruct((2, n), F32)]
                  + [jax.ShapeDtypeStruct((N_EXPERTS, LANES), F32)] * 4,
        compiler_params=_cparams("arbitrary"),
        name="router",
    )(xr, rw[0], rw[1], rb)


def _plan_kernel(cnt_ref, tbef_ref, meta_ref, rstart_ref, *, nbl):
    shift = MOE_BM.bit_length() - 1
    cnt = cnt_ref[...].astype(I32)
    padded = ((cnt + (MOE_BM - 1)) >> shift) << shift
    starts = []
    acc = jnp.zeros((1, LANES), I32)
    for e in range(N_EXPERTS):
        starts.append(acc)
        acc = acc + padded[e:e + 1, :]
    pad_start = jnp.concatenate(starts, axis=0)
    pad_end = pad_start + padded
    rstart_ref[...] = pad_start + tbef_ref[...].astype(I32)
    blk0 = lax.broadcasted_iota(I32, (N_EXPERTS, nbl), 1) * MOE_BM
    block_e = jnp.sum((pad_end[:, 0:1] <= blk0).astype(I32), axis=0, keepdims=True)
    block_e = jnp.minimum(block_e, N_EXPERTS - 1)
    n_used = jnp.broadcast_to(acc[:, 0:1] >> shift, (1, nbl))
    diag = lax.broadcasted_iota(I32, (N_EXPERTS, nbl), 0) == lax.broadcasted_iota(I32, (N_EXPERTS, nbl), 1)
    fill_lo = jnp.sum(jnp.where(diag, (pad_start + cnt)[:, 0:1], 0), axis=0, keepdims=True)
    fill_hi = jnp.sum(jnp.where(diag, pad_end[:, 0:1], 0), axis=0, keepdims=True)
    meta_ref[...] = jnp.concatenate([block_e, n_used, fill_lo, fill_hi, jnp.zeros((SUBLANES - 4, nbl), I32)],
                                    axis=0)


def _plan(cnt, tbef, n_blocks):
    nbl = -(-n_blocks // LANES) * LANES
    table = pl.BlockSpec((N_EXPERTS, LANES), lambda i: (0, 0))
    return pl.pallas_call(
        functools.partial(_plan_kernel, nbl=nbl),
        grid=(1,),
        in_specs=[table, table],
        out_specs=[pl.BlockSpec((SUBLANES, nbl), lambda i: (0, 0)), table],
        out_shape=[jax.ShapeDtypeStruct((SUBLANES, nbl), I32),
                   jax.ShapeDtypeStruct((N_EXPERTS, LANES), I32)],
        compiler_params=_cparams("arbitrary"),
        name="moe_plan",
    )(cnt, tbef)


def _rows(ref, row, nrows):
    return ref.at[pl.ds(pl.multiple_of(row * ROW_CHUNKS, ROW_CHUNKS), nrows * ROW_CHUNKS), :]


def _rows_wait(src_hbm, buf, sem):
    pltpu.make_async_copy(src_hbm.at[pl.ds(0, buf.shape[0]), :], buf, sem).wait()


def _copy_pieces(src, src_row, dst, dst_row, count, max_rows, sem, wait=False):
    bit = max_rows.bit_length() - 1
    while bit >= 0:
        size = 1 << bit
        done = (count >> (bit + 1)) << (bit + 1)

        @pl.when(((count >> bit) & 1) == 1)
        def _():
            cp = pltpu.make_async_copy(_rows(src, src_row + done, size), _rows(dst, dst_row + done, size), sem)
            cp.start()
            if wait:
                cp.wait()

        bit -= 1


def _tile_runs(tcnt_ref, toff_ref, rstart_ref, tile, buf, hbm, sem, *, to_hbm, tm):
    def per_expert(e, carry):
        k = tile * N_EXPERTS + e
        if to_hbm:
            _copy_pieces(buf, toff_ref[k], hbm, rstart_ref[k], tcnt_ref[k], tm, sem)
        else:
            _copy_pieces(hbm, rstart_ref[k], buf, toff_ref[k], tcnt_ref[k], tm, sem)
        return carry

    lax.fori_loop(0, N_EXPERTS, per_expert, 0)


def _dispatch_kernel(lpos_ref, tcnt_ref, toff_ref, rstart_ref, flo_ref, fhi_ref, nu_ref, x_ref, old_hbm, xs_hbm,
                     s0, s1, zbuf, sem, zsem, *, n, tm, n_blocks, reuse):
    del old_hbm
    i = pl.program_id(0)
    nt = pl.num_programs(0)
    bufs = (s0, s1)
    unroll = PLACE_UNROLL

    for slot in range(2):
        @pl.when(i % 2 == slot)
        def _():
            buf = bufs[slot]

            @pl.when(i >= 2)
            def _():
                _rows_wait(xs_hbm, buf, sem.at[slot])

            def place(c, carry):
                tok = i * tm + c * unroll
                src = pl.multiple_of(c * (unroll * ROW_CHUNKS), unroll * ROW_CHUNKS)
                for u in range(unroll):
                    v = x_ref[pl.ds(src + u * ROW_CHUNKS, ROW_CHUNKS), :]
                    for k in range(2):
                        p = lpos_ref[k * n + tok + u]
                        buf[pl.ds(pl.multiple_of(p, ROW_CHUNKS), ROW_CHUNKS), :] = v
                return carry

            lax.fori_loop(0, tm // unroll, place, 0)
            _tile_runs(tcnt_ref, toff_ref, rstart_ref, i, buf, xs_hbm, sem.at[slot], to_hbm=True, tm=tm)

    zero_rows = n_blocks * MOE_BM - 2 * n
    if not reuse:
        @pl.when(i == 0)
        def _():
            zbuf[...] = jnp.zeros_like(zbuf)
            for e in range(N_EXPERTS):
                _copy_pieces(zbuf, 0, xs_hbm, flo_ref[e], fhi_ref[e] - flo_ref[e], MOE_BM // 2, zsem)

            def zero_block(j, carry):
                pltpu.make_async_copy(zbuf, _rows(xs_hbm, j * MOE_BM, MOE_BM), zsem).start()
                return carry

            lax.fori_loop(nu_ref[0], n_blocks, zero_block, 0)

    @pl.when(i == nt - 1)
    def _():
        for slot in range(2):
            @pl.when(nt > slot)
            def _():
                _rows_wait(xs_hbm, bufs[slot], sem.at[slot])

        if not reuse:
            pltpu.make_async_copy(_rows(xs_hbm, 0, zero_rows), _rows(xs_hbm, 0, zero_rows), zsem).wait()


def _dispatch(lpos_flat, tcnt, toff, rstart, fill_lo, fill_hi, n_used, xr, n_blocks, tm, old=None):
    n = xr.shape[0] // ROW_CHUNKS
    reuse = old is not None
    if not reuse:
        old = jnp.zeros((SUBLANES, LANES), F32)
    return pl.pallas_call(
        functools.partial(_dispatch_kernel, n=n, tm=tm, n_blocks=n_blocks, reuse=reuse),
        grid_spec=pltpu.PrefetchScalarGridSpec(
            num_scalar_prefetch=7,
            grid=(n // tm,),
            in_specs=[pl.BlockSpec((tm * ROW_CHUNKS, LANES), lambda i, *_: (i, 0)),
                      pl.BlockSpec(memory_space=pl.ANY)],
            out_specs=pl.BlockSpec(memory_space=pl.ANY),
            scratch_shapes=[pltpu.VMEM((2 * tm * ROW_CHUNKS, LANES), F32),
                            pltpu.VMEM((2 * tm * ROW_CHUNKS, LANES), F32),
                            pltpu.VMEM((MOE_BM * ROW_CHUNKS, LANES), F32),
                            pltpu.SemaphoreType.DMA((2,)),
                            pltpu.SemaphoreType.DMA(())]),
        out_shape=jax.ShapeDtypeStruct((n_blocks * MOE_BM * ROW_CHUNKS, LANES), F32),
        input_output_aliases={8: 0} if reuse else {},
        compiler_params=_cparams("arbitrary"),
        name="moe_dispatch",
    )(lpos_flat, tcnt, toff, rstart, fill_lo, fill_hi, n_used, xr, old)


EXPERT_BUFS = 3


def _experts_kernel(be_ref, nu_ref, xs_hbm, wg_ref, wu_ref, wd_ref, y_hbm, wgb, wub, wdb, xin, yout,
                    sem_in, sem_out):
    j = pl.program_id(0)
    nb = pl.num_programs(0)
    n_used = nu_ref[0]
    used = j < n_used
    ahead = EXPERT_BUFS - 1

    def fetch(blk, slot):
        return pltpu.make_async_copy(_rows(xs_hbm, blk * MOE_BM, MOE_BM), xin.at[slot], sem_in.at[slot])

    def write_back(blk, slot):
        return pltpu.make_async_copy(yout.at[slot], _rows(y_hbm, blk * MOE_BM, MOE_BM), sem_out.at[slot])

    @pl.when(j == 0)
    def _():
        for d in range(ahead):
            @pl.when(d < n_used)
            def _():
                fetch(d, d).start()

    @pl.when(jnp.logical_and(used, jnp.logical_or(j == 0, be_ref[j] != be_ref[jnp.maximum(j - 1, 0)])))
    def _():
        wgb[...] = wg_ref[0, 0].astype(BF16)
        wub[...] = wu_ref[0, 0].astype(BF16)
        wdb[...] = wd_ref[0, 0].astype(BF16)

    for slot in range(EXPERT_BUFS):
        @pl.when(jnp.logical_and(used, j % EXPERT_BUFS == slot))
        def _():
            @pl.when(j >= EXPERT_BUFS)
            def _():
                write_back(j - EXPERT_BUFS, slot).wait()

            @pl.when(j + ahead < n_used)
            def _():
                fetch(j + ahead, (slot + ahead) % EXPERT_BUFS).start()

            fetch(j, slot).wait()
            x = _read_rows(xin.at[slot], MOE_BM).astype(BF16)
            h = _silu(_dot(x, wgb[...])) * _dot(x, wub[...])
            _write_rows(yout.at[slot], _dot(h.astype(BF16), wdb[...]), MOE_BM)
            write_back(j, slot).start()

    @pl.when(j == nb - 1)
    def _():
        for back in range(1, EXPERT_BUFS + 1):
            @pl.when(n_used - back >= 0)
            def _():
                blk = n_used - back
                pltpu.make_async_copy(yout.at[0], _rows(y_hbm, blk * MOE_BM, MOE_BM),
                                      sem_out.at[blk % EXPERT_BUFS]).wait()


def _experts(block_e, n_used, xs, wg, wu, wd, layer):
    n_blocks = block_e.shape[0]

    def last_used(j, nu):
        return jnp.minimum(j, nu[0] - 1)

    def wblk(j, be, nu):
        return (layer, be[last_used(j, nu)], 0, 0)

    return pl.pallas_call(
        _experts_kernel,
        grid_spec=pltpu.PrefetchScalarGridSpec(
            num_scalar_prefetch=2,
            grid=(n_blocks,),
            in_specs=[pl.BlockSpec(memory_space=pl.ANY),
                      pl.BlockSpec((1, 1, D_MODEL, D_EXPERT), wblk),
                      pl.BlockSpec((1, 1, D_MODEL, D_EXPERT), wblk),
                      pl.BlockSpec((1, 1, D_EXPERT, D_MODEL), wblk)],
            out_specs=pl.BlockSpec(memory_space=pl.ANY),
            scratch_shapes=[pltpu.VMEM((D_MODEL, D_EXPERT), BF16), pltpu.VMEM((D_MODEL, D_EXPERT), BF16),
                            pltpu.VMEM((D_EXPERT, D_MODEL), BF16),
                            pltpu.VMEM((EXPERT_BUFS, MOE_BM * ROW_CHUNKS, LANES), F32),
                            pltpu.VMEM((EXPERT_BUFS, MOE_BM * ROW_CHUNKS, LANES), F32),
                            pltpu.SemaphoreType.DMA((EXPERT_BUFS,)), pltpu.SemaphoreType.DMA((EXPERT_BUFS,))]),
        out_shape=jax.ShapeDtypeStruct(xs.shape, F32),
        input_output_aliases={2: 0},
        compiler_params=_cparams("arbitrary"),
        name="moe_experts",
    )(block_e, n_used, xs, wg, wu, wd)


def _combine_ln_kernel(lpos_ref, tcnt_ref, toff_ref, rstart_ref, y_hbm, x_ref, w1_ref, w2_ref, g_ref, b_ref, o_ref,
                       r0, r1, r2, u1, u2, sem, *, n, tm):
    i = pl.program_id(0)
    nt = pl.num_programs(0)
    bufs = (r0, r1, r2)
    nbuf = len(bufs)
    unroll = PLACE_UNROLL

    def fetch(tile, slot):
        _tile_runs(tcnt_ref, toff_ref, rstart_ref, tile, bufs[slot], y_hbm, sem.at[slot], to_hbm=False, tm=tm)

    @pl.when(i == 0)
    def _():
        for d in range(nbuf - 1):
            @pl.when(d < nt)
            def _():
                fetch(d, d)

    for slot in range(nbuf):
        @pl.when(i % nbuf == slot)
        def _():
            @pl.when(i + nbuf - 1 < nt)
            def _():
                fetch(i + nbuf - 1, (slot + nbuf - 1) % nbuf)

            buf = bufs[slot]
            _rows_wait(y_hbm, buf, sem.at[slot])

            def place(c, carry):
                tok = i * tm + c * unroll
                dst0 = pl.multiple_of(c * (unroll * ROW_CHUNKS), unroll * ROW_CHUNKS)
                for u in range(unroll):
                    dst = pl.ds(dst0 + u * ROW_CHUNKS, ROW_CHUNKS)
                    for k, out in enumerate((u1, u2)):
                        p = lpos_ref[k * n + tok + u]
                        out[dst, :] = buf[pl.ds(pl.multiple_of(p, ROW_CHUNKS), ROW_CHUNKS), :]
                return carry

            lax.fori_loop(0, tm // unroll, place, 0)
            moe = w1_ref[...] * _read_rows(u1, tm) + w2_ref[...] * _read_rows(u2, tm)
            o_ref[...] = _ln(ALPHA * _read_rows(x_ref, tm) + moe, g_ref[...], b_ref[...])


def _combine_ln(lpos_flat, tcnt, toff, rstart, y, xr, w1, w2, g, b, tm):
    n = xr.shape[0] // ROW_CHUNKS
    return pl.pallas_call(
        functools.partial(_combine_ln_kernel, n=n, tm=tm),
        grid_spec=pltpu.PrefetchScalarGridSpec(
            num_scalar_prefetch=4,
            grid=(n // tm,),
            in_specs=[pl.BlockSpec(memory_space=pl.ANY),
                      pl.BlockSpec((tm * ROW_CHUNKS, LANES), lambda i, *_: (i, 0)),
                      pl.BlockSpec((tm, 1), lambda i, *_: (i, 0)),
                      pl.BlockSpec((tm, 1), lambda i, *_: (i, 0)),
                      pl.BlockSpec((1, D_MODEL), lambda i, *_: (0, 0)),
                      pl.BlockSpec((1, D_MODEL), lambda i, *_: (0, 0))],
            out_specs=pl.BlockSpec((tm, D_MODEL), lambda i, *_: (i, 0)),
            scratch_shapes=[pltpu.VMEM((2 * tm * ROW_CHUNKS, LANES), F32)] * 3
                           + [pltpu.VMEM((tm * ROW_CHUNKS, LANES), F32)] * 2
                           + [pltpu.SemaphoreType.DMA((3,))]),
        out_shape=jax.ShapeDtypeStruct((n, D_MODEL), F32),
        compiler_params=_cparams("arbitrary"),
        name="moe_combine_ln",
    )(lpos_flat, tcnt, toff, rstart, y, xr, w1, w2, g, b)


def _moe_ln(xr, rw, rb, wg, wu, wd, layer, g, b, old=None):
    n = xr.shape[0] // ROW_CHUNKS
    n_blocks = (2 * n) // MOE_BM + N_EXPERTS
    tm = MOE_TILE
    nt = n // tm
    lpos, w, cnt, tcnt, toff, tbef = _router(xr, rw, rb, tm)
    meta, rstart = _plan(cnt, tbef, n_blocks)
    block_e = meta[0, :n_blocks]
    n_used = meta[1, :1]

    def per_tile(table):
        return table[:, :nt].T.reshape(nt * N_EXPERTS).astype(I32)

    lpos_flat = lpos.reshape(2 * n)
    tcnt, toff, rstart = per_tile(tcnt), per_tile(toff), per_tile(rstart)
    xs = _dispatch(lpos_flat, tcnt, toff, rstart, meta[2, :N_EXPERTS], meta[3, :N_EXPERTS], n_used, xr,
                   n_blocks, tm, old)
    y = _experts(block_e, n_used, xs, wg, wu, wd, layer)
    out = _combine_ln(lpos_flat, tcnt, toff, rstart, y, xr, w[0].reshape(n, 1), w[1].reshape(n, 1), g, b, tm)
    return out, y


def _conv_qkv_kernel(xm_ref, cw_ref, cb_ref, wq_ref, wk_ref, wv_ref, q_ref, k_ref, v_ref, xc_ref, *, s):
    xm_b = xm_ref[0]
    xm = xm_b.astype(F32)
    cw = cw_ref[...]
    row = lax.broadcasted_iota(I32, (s, 1), 0)
    half = CONV_K // 2
    acc = cb_ref[...] + xm * cw[half:half + 1, :]
    for sh in range(1, half + 1):
        past = jnp.where(row >= sh, pltpu.roll(xm, sh, axis=0), 0.0)
        acc = acc + past * cw[half - sh:half - sh + 1, :]
        nxt = jnp.where(row < s - sh, pltpu.roll(xm, s - sh, axis=0), 0.0)
        acc = acc + nxt * cw[half + sh:half + sh + 1, :]
    xc = _silu(acc).astype(BF16)
    xc_ref[0] = xc
    q_ref[0] = _dot(xc, wq_ref[0]).astype(BF16)
    k_ref[0] = (_dot_nt(wk_ref[0], xc) * (ML_DH ** -0.5)).astype(BF16)
    v = _dot(xm_b, wv_ref[0])
    ones_lane = lax.broadcasted_iota(I32, (1, ML_DHP), 1) == ML_DH
    v_ref[0] = jnp.where(ones_lane, 1.0, v).astype(BF16)


def _conv_qkv(main3, cw, cb, wq, wk_t, wv):
    b, s, _ = main3.shape
    tok = pl.BlockSpec((1, s, ML_DHP), lambda i, h: (i, 0, h))
    wspec = pl.BlockSpec((1, ML_DHP, ML_DHP), lambda i, h: (h, 0, 0))
    tok_shape = jax.ShapeDtypeStruct((b, s, ML_WP), BF16)
    return pl.pallas_call(
        functools.partial(_conv_qkv_kernel, s=s),
        grid=(b, ML_HEADS),
        in_specs=[tok,
                  pl.BlockSpec((CONV_K, ML_DHP), lambda i, h: (0, h)),
                  pl.BlockSpec((1, ML_DHP), lambda i, h: (0, h)),
                  wspec, wspec, wspec],
        out_specs=[tok, pl.BlockSpec((1, ML_DHP, s), lambda i, h: (i, h, 0)), tok, tok],
        out_shape=[tok_shape, jax.ShapeDtypeStruct((b, ML_WP, s), BF16), tok_shape, tok_shape],
        compiler_params=_cparams("parallel", "parallel"),
        name="conv_qkv",
    )(main3, cw, cb, wq, wk_t, wv)


def _mlstm_kernel(q_ref, kt_ref, v_ref, gc_ref, gr_ref, z_ref, xc_ref, ng_ref, sk_ref,
                  y_ref, hf_ref, hb_ref, cf_ref, cb_ref, m_ref, *, s):
    head0 = pl.program_id(1) * ML_HPS
    nc = s // CHUNK
    sub = lax.broadcasted_iota(I32, (LANES, 1), 0)
    gate = lax.broadcasted_iota(I32, (LANES, LANES), 0)
    ti = lax.broadcasted_iota(I32, (CHUNK, CHUNK), 0)
    tj = lax.broadcasted_iota(I32, (CHUNK, CHUNK), 1)

    for ref in (cf_ref, cb_ref, m_ref):
        ref[...] = jnp.zeros_like(ref)

    def intra(c, j, rev):
        t0 = pl.multiple_of(c * CHUNK, CHUNK)
        hl = slice(j * ML_DHP, (j + 1) * ML_DHP)
        qb = q_ref[0, pl.ds(t0, CHUNK), hl]
        kt = kt_ref[0, hl, pl.ds(t0, CHUNK)]
        vb = v_ref[0, pl.ds(t0, CHUNK), hl]
        gc = gc_ref[0, pl.ds(t0, CHUNK), :]
        gr = gr_ref[:, pl.ds(t0, CHUNK)]
        i_idx = head0 + j + (2 * ML_HEADS if rev else 0)
        f_idx = i_idx + ML_HEADS
        allowed = (tj >= ti) if rev else (tj <= ti)
        sel = (gate == f_idx).astype(BF16)
        b_rep = sum(_dot(part, sel) for part in _split3(gc))
        b_row = jnp.sum(jnp.where(sub == f_idx, gr, 0.0), axis=0, keepdims=True)
        i_row = jnp.sum(jnp.where(sub == i_idx, gr, 0.0), axis=0, keepdims=True)
        b_last = (b_rep[0:1, :] if rev else b_rep[CHUNK - 1:CHUNK, :])[:, 0:1]

        b_wide = jnp.concatenate([b_rep] * (CHUNK // LANES), axis=1)
        d = jnp.where(allowed, b_wide - b_row + i_row, NEG)
        m_in = jnp.max(d, axis=1, keepdims=True)
        sc = _dot(qb, kt) * jnp.exp(d - m_in)
        nd_in = _dot(sc.astype(BF16), vb)
        w_row = b_last - b_row + i_row
        return t0, qb, kt, vb, b_rep, b_last, m_in, nd_in, w_row

    def twice(a):
        return jnp.concatenate([a, a], axis=1)

    def update(parts, j, rev):
        t0, qb, kt, vb, b_rep, b_last, m_in, nd_in, w_row = parts
        h_ref, c_ref = (hb_ref, cb_ref) if rev else (hf_ref, cf_ref)
        hl = slice(j * ML_DHP, (j + 1) * ML_DHP)
        mrow = 2 * j + int(rev)
        m = m_ref[mrow:mrow + 1, 0:1]
        cmat = c_ref[j]
        inter = b_rep + m
        m_t = jnp.maximum(m_in, inter)
        a_in = jnp.exp(m_in - m_t)
        iexp = jnp.exp(inter - m_t)
        nd = twice(a_in) * nd_in + twice(iexp) * _dot(qb, cmat.astype(BF16))
        den = nd[:, ML_DH:ML_DH + 1]
        h_ref[pl.ds(t0, CHUNK), hl] = nd * (1.0 / jnp.maximum(jnp.abs(den), jnp.exp(-m_t[:, 0:1])))

        m_new = jnp.maximum(b_last + m, jnp.max(w_row, axis=1, keepdims=True))
        wexp = jnp.exp(w_row - m_new)
        cexp = jnp.exp(b_last + m - m_new)
        kw = (kt.astype(F32) * wexp).astype(BF16)
        c_ref[j] = cexp * cmat + _dot(kw, vb)
        m_ref[mrow:mrow + 1, :] = jnp.broadcast_to(m_new, (1, LANES))

    def step(i, carry):
        for j in range(ML_HPS):
            parts = [intra(nc - 1 - i if rev else i, j, rev) for rev in (False, True)]
            for p, rev in zip(parts, (False, True)):
                update(p, j, rev)
        return carry

    lax.fori_loop(0, nc, step, 0)

    real = lax.broadcasted_iota(I32, (1, ML_DHP), 1) < ML_DH
    tb = CHUNK

    def fin(c, carry):
        t0 = pl.multiple_of(c * tb, tb)
        for j in range(ML_HPS):
            hl = slice(j * ML_DHP, (j + 1) * ML_DHP)
            hs = jnp.where(real, hf_ref[pl.ds(t0, tb), hl] + hb_ref[pl.ds(t0, tb), hl], 0.0)
            mu = jnp.sum(hs, axis=1, keepdims=True) * (1.0 / ML_DH)
            dev = jnp.where(real, hs - mu, 0.0)
            var = jnp.sum(dev * dev, axis=1, keepdims=True) * (1.0 / ML_DH)
            hn = dev * lax.rsqrt(var + LN_EPS) * ng_ref[:, hl]
            xc = xc_ref[0, pl.ds(t0, tb), hl].astype(F32)
            z = z_ref[0, pl.ds(t0, tb), hl].astype(F32)
            y_ref[0, pl.ds(t0, tb), hl] = ((hn + sk_ref[:, hl] * xc) * _silu(z)).astype(BF16)
        return carry

    lax.fori_loop(0, s // tb, fin, 0)


def _mlstm(q, kt, v, gcol3, grow, main3, xc, ng, sk):
    b, s, _ = q.shape
    width = ML_HPS * ML_DHP
    steps = ML_HEADS // ML_HPS
    tok = pl.BlockSpec((1, s, width), lambda i, h: (i, 0, h))
    vec = pl.BlockSpec((1, width), lambda i, h: (0, h))
    return pl.pallas_call(
        functools.partial(_mlstm_kernel, s=s),
        grid=(b, steps),
        in_specs=[tok, pl.BlockSpec((1, width, s), lambda i, h: (i, h, 0)), tok,
                  pl.BlockSpec((1, s, LANES), lambda i, h: (i, 0, 0)),
                  pl.BlockSpec((LANES, s), lambda i, h: (0, i)),
                  pl.BlockSpec((1, s, width), lambda i, h: (i, 0, steps + h)),
                  tok, vec, vec],
        out_specs=tok,
        out_shape=jax.ShapeDtypeStruct((b, s, ML_WP), BF16),
        scratch_shapes=[pltpu.VMEM((s, width), F32), pltpu.VMEM((s, width), F32),
                        pltpu.VMEM((ML_HPS, ML_DHP, ML_DHP), F32), pltpu.VMEM((ML_HPS, ML_DHP, ML_DHP), F32),
                        pltpu.VMEM((SUBLANES, LANES), F32)],
        compiler_params=_cparams("parallel", "parallel"),
        name="mlstm",
    )(q, kt, v, gcol3, grow, main3, xc, ng, sk)


def _pad_heads(a, axis):
    a = jnp.moveaxis(a, axis, -1)
    lead = a.shape[:-1]
    a = a.reshape(lead + (ML_HEADS, ML_DH))
    a = jnp.pad(a, [(0, 0)] * len(lead) + [(0, 0), (0, ML_DHP - ML_DH)])
    return jnp.moveaxis(a.reshape(lead + (ML_WP,)), -1, axis)


def kernel(x, mem, mem_ln_g, mem_ln_b, w_mem_kv, router_w, router_b, na_w_in, na_rpb, ml_w_in, ml_conv_w,
           ml_conv_b, ml_w_qkv, ml_gate_b, ml_norm_g, ml_skip, w_out, ln_g, ln_b, exp_w_gate, exp_w_up,
           exp_w_down):
    b, s, d = x.shape
    n = b * s
    nm = mem.shape[1]
    row = lambda a: a.reshape(1, -1)

    mem_k, mem_v = _memkv(mem.reshape(b * nm, d), row(mem_ln_g), row(mem_ln_b), w_mem_kv.astype(BF16))
    mem_k3 = mem_k.reshape(b, nm, MEM_W)
    mem_v3 = mem_v.reshape(b, nm, MEM_W)
    rw_pad = jnp.pad(router_w, ((0, 0), (0, LANES - N_EXPERTS)))
    rw_hi = rw_pad.astype(BF16)
    rw = (rw_hi, (rw_pad - rw_hi.astype(F32)).astype(BF16))
    rb = router_b.reshape(N_EXPERTS, 1)

    x2 = x.reshape(n, d)

    h0 = _proj(x2, na_w_in[0].astype(BF16)).reshape(b, s, 3 * NA_W + MEM_W)
    y_na = _na_attention(h0, _na_bias_table(na_rpb[0]))
    wo = w_out[0].astype(BF16)
    xr = _outproj_ln(y_na.reshape(n, NA_W), h0.reshape(n, 3 * NA_W + MEM_W), 3 * NA_W // MEM_W, mem_k3, mem_v3,
                     wo[:NA_W], wo[NA_W:], x2, row(ln_g[0, 0]), row(ln_b[0, 0]))
    x2, moe_buf = _moe_ln(xr, rw, rb, exp_w_gate, exp_w_up, exp_w_down, 0, row(ln_g[0, 1]), row(ln_b[0, 1]))

    w1 = ml_w_in[0]
    w_main = jnp.concatenate([_pad_heads(w1[:, :ML_W], 1), _pad_heads(w1[:, ML_W:2 * ML_W], 1),
                              w1[:, 2 * ML_W + 4 * ML_HEADS:]], axis=1).astype(BF16)
    w_g = jnp.pad(w1[:, 2 * ML_W:2 * ML_W + 4 * ML_HEADS], ((0, 0), (0, LANES - 4 * ML_HEADS))).astype(BF16)
    gb = jnp.pad(ml_gate_b[0].reshape(4 * ML_HEADS), (0, LANES - 4 * ML_HEADS))
    main, acol, arow = _proj_gates(x2, w_main, w_g, w_g.T, gb.reshape(1, LANES), gb.reshape(LANES, 1))
    main3 = main.reshape(b, s, 2 * ML_WP + MEM_W)
    wqkv = jnp.pad(ml_w_qkv[0], ((0, 0), (0, 0), (0, ML_DHP - ML_DH), (0, ML_DHP - ML_DH))).astype(BF16)
    q, k, v, xc = _conv_qkv(main3, _pad_heads(ml_conv_w[0], 1), _pad_heads(row(ml_conv_b[0]), 1),
                            wqkv[0], jnp.swapaxes(wqkv[1], 1, 2), wqkv[2])
    y_ml = _mlstm(q, k, v, acol.reshape(b, s, LANES), arow, main3, xc,
                  _pad_heads(row(ml_norm_g[0]), 1), _pad_heads(row(ml_skip[0]), 1))
    wo = w_out[1]
    xr = _outproj_ln(y_ml.reshape(n, ML_WP), main, 2 * ML_WP // MEM_W, mem_k3, mem_v3,
                     _pad_heads(wo[:ML_W], 0).astype(BF16), wo[ML_W:].astype(BF16), x2,
                     row(ln_g[1, 0]), row(ln_b[1, 0]))
    x2, _ = _moe_ln(xr, rw, rb, exp_w_gate, exp_w_up, exp_w_down, 1, row(ln_g[1, 1]), row(ln_b[1, 1]), moe_buf)
    return x2.reshape(b, s, d)
```

```python
import functools

import numpy as np
import jax
import jax.numpy as jnp
from jax import lax
from jax.experimental import pallas as pl
from jax.experimental.pallas import tpu as pltpu

F32 = jnp.float32
BF16 = jnp.bfloat16
I32 = jnp.int32

D_MODEL = 1024
DEPTH = 2
GRID_W = 64
MEM_HEADS = 4
MEM_DH = 64
MEM_W = MEM_HEADS * MEM_DH
NA_HEADS = 12
NA_DH = 64
NA_W = NA_HEADS * NA_DH
WIN_H = 8
WIN_W = 16
ML_HEADS = 4
ML_DH = 192
ML_DHP = 256
ML_W = ML_HEADS * ML_DH
ML_WP = ML_HEADS * ML_DHP
CONV_K = 5
CHUNK = 256
N_EXPERTS = 16
N_GROUPS = 4
EXPERTS_PER_GROUP = N_EXPERTS // N_GROUPS
D_EXPERT = 512
ALPHA = (2 * DEPTH) ** 0.25
LN_EPS = 1e-5
NEG = -1e30

LANES = 128
SUBLANES = 8
ROW_CHUNKS = D_MODEL // LANES
MOE_BM = 512
MOE_TILE = 512
ML_HPS = 2
NA_ROWS_PER_STEP = 16
ROW_TILE = 1024
PLACE_UNROLL = 16
VMEM_LIMIT = 48 * 1024 * 1024


def _cparams(*sem):
    return pltpu.CompilerParams(dimension_semantics=sem, vmem_limit_bytes=VMEM_LIMIT)


def _dot(a, b):
    return jnp.dot(a, b, preferred_element_type=F32)


def _dot_nt(a, b, precision=None):
    return lax.dot_general(a, b, (((1,), (1,)), ((), ())), precision=precision,
                           preferred_element_type=F32)


def _ln(z, g, b):
    mu = jnp.mean(z, axis=-1, keepdims=True)
    zc = z - mu
    var = jnp.mean(zc * zc, axis=-1, keepdims=True)
    return zc * lax.rsqrt(var + LN_EPS) * g + b


def _silu(x):
    return x * jax.nn.sigmoid(x)


def _read_rows(ref, n):
    return jnp.concatenate([ref[pl.ds(j, n, stride=ROW_CHUNKS), :] for j in range(ROW_CHUNKS)], axis=1)


def _write_rows(ref, val, n):
    for j in range(ROW_CHUNKS):
        ref[pl.ds(j, n, stride=ROW_CHUNKS), :] = val[:, j * LANES:(j + 1) * LANES]


def _memkv_kernel(m_ref, g_ref, b_ref, w_ref, k_ref, v_ref):
    z = _ln(m_ref[...], g_ref[...], b_ref[...])
    kv = _dot(z.astype(BF16), w_ref[...])
    k_ref[...] = kv[:, :MEM_W].astype(BF16)
    v_ref[...] = kv[:, MEM_W:].astype(BF16)


def _memkv(mem2, g, b, w):
    n = mem2.shape[0]
    tm = min(ROW_TILE, n)
    return pl.pallas_call(
        _memkv_kernel,
        grid=(n // tm,),
        in_specs=[pl.BlockSpec((tm, D_MODEL), lambda i: (i, 0)),
                  pl.BlockSpec((1, D_MODEL), lambda i: (0, 0)),
                  pl.BlockSpec((1, D_MODEL), lambda i: (0, 0)),
                  pl.BlockSpec((D_MODEL, 2 * MEM_W), lambda i: (0, 0))],
        out_specs=[pl.BlockSpec((tm, MEM_W), lambda i: (i, 0)),
                   pl.BlockSpec((tm, MEM_W), lambda i: (i, 0))],
        out_shape=[jax.ShapeDtypeStruct((n, MEM_W), BF16)] * 2,
        compiler_params=_cparams("parallel"),
        name="memkv",
    )(mem2, g, b, w)


def _proj_kernel(x_ref, w_ref, o_ref):
    o_ref[...] = _dot(x_ref[...].astype(BF16), w_ref[...]).astype(o_ref.dtype)


def _proj(x2, w, tm=ROW_TILE):
    n, k = x2.shape
    nout = w.shape[1]
    return pl.pallas_call(
        _proj_kernel,
        grid=(n // tm,),
        in_specs=[pl.BlockSpec((tm, k), lambda i: (i, 0)),
                  pl.BlockSpec((k, nout), lambda i: (0, 0))],
        out_specs=pl.BlockSpec((tm, nout), lambda i: (i, 0)),
        out_shape=jax.ShapeDtypeStruct((n, nout), BF16),
        compiler_params=_cparams("parallel"),
        name="in_proj",
    )(x2, w)


def _split3(x):
    hi = x.astype(BF16)
    r1 = x - hi.astype(F32)
    mid = r1.astype(BF16)
    lo = (r1 - mid.astype(F32)).astype(BF16)
    return hi, mid, lo


def _proj_gates_kernel(x_ref, w_ref, wg_ref, wgt_ref, gbc_ref, gbr_ref, o_ref, g_ref, gt_ref, *, tm):
    xb = x_ref[...].astype(BF16)
    o_ref[...] = _dot(xb, w_ref[...]).astype(BF16)
    gcol = _dot(xb, wg_ref[...]) + gbc_ref[...]
    grow = _dot_nt(wgt_ref[...], xb) + gbr_ref[...]
    lane = lax.broadcasted_iota(I32, (1, LANES), 1)
    sub = lax.broadcasted_iota(I32, (LANES, 1), 0)
    ti = lax.broadcasted_iota(I32, (CHUNK, CHUNK), 0)
    tj = lax.broadcasted_iota(I32, (CHUNK, CHUNK), 1)
    lower = (tj <= ti).astype(BF16)
    upper = (ti <= tj).astype(BF16)

    def pick(idx, pre, suf, raw):
        fwd = jnp.logical_and(idx >= ML_HEADS, idx < 2 * ML_HEADS)
        bwd = jnp.logical_and(idx >= 3 * ML_HEADS, idx < 4 * ML_HEADS)
        return jnp.where(fwd, pre, jnp.where(bwd, suf, raw))

    for c in range(tm // CHUNK):
        tc = slice(c * CHUNK, (c + 1) * CHUNK)
        g = gcol[tc, :]
        ls = jax.nn.log_sigmoid(g)
        pre = sum(_dot(lower, part) for part in _split3(ls))
        suf = jnp.sum(ls, axis=0, keepdims=True) - pre + ls
        g_ref[tc, :] = pick(lane, pre, suf, g)
        g = grow[:, tc]
        ls = jax.nn.log_sigmoid(g)
        pre = sum(_dot(part, upper) for part in _split3(ls))
        suf = jnp.sum(ls, axis=1, keepdims=True) - pre + ls
        gt_ref[:, tc] = pick(sub, pre, suf, g)


def _proj_gates(x2, w, wg, wgt, gbc, gbr, tm=ROW_TILE):
    n, k = x2.shape
    nout = w.shape[1]
    return pl.pallas_call(
        functools.partial(_proj_gates_kernel, tm=tm),
        grid=(n // tm,),
        in_specs=[pl.BlockSpec((tm, k), lambda i: (i, 0)),
                  pl.BlockSpec((k, nout), lambda i: (0, 0)),
                  pl.BlockSpec((k, LANES), lambda i: (0, 0)),
                  pl.BlockSpec((LANES, k), lambda i: (0, 0)),
                  pl.BlockSpec((1, LANES), lambda i: (0, 0)),
                  pl.BlockSpec((LANES, 1), lambda i: (0, 0))],
        out_specs=[pl.BlockSpec((tm, nout), lambda i: (i, 0)),
                   pl.BlockSpec((tm, LANES), lambda i: (i, 0)),
                   pl.BlockSpec((LANES, tm), lambda i: (0, i))],
        out_shape=[jax.ShapeDtypeStruct((n, nout), BF16),
                   jax.ShapeDtypeStruct((n, LANES), F32),
                   jax.ShapeDtypeStruct((LANES, n), F32)],
        compiler_params=_cparams("parallel"),
        name="in_proj_gates",
    )(x2, w, wg, wgt, gbc, gbr)


def _na_kernel(q_ref, k_ref, v_ref, tbl_ref, o_ref, *, rows):
    lane = lax.broadcasted_iota(I32, (1, LANES), 1)
    first = lane < NA_DH
    nkeys = WIN_H * GRID_W

    def rows_step(i, carry):
        rr = [i * NA_ROWS_PER_STEP + u for u in range(NA_ROWS_PER_STEP)]
        rss = [jnp.clip(r - WIN_H // 2, 0, rows - WIN_H) for r in rr]
        scores = []
        for r, rs in zip(rr, rss):
            q = q_ref[0, pl.ds(pl.multiple_of(r * GRID_W, GRID_W), GRID_W), :]
            q = q * jnp.asarray(NA_DH ** -0.5, BF16)
            q2 = jnp.concatenate([jnp.where(first, q, jnp.zeros_like(q)),
                                  jnp.where(first, jnp.zeros_like(q), q)], axis=0)
            k = k_ref[0, pl.ds(pl.multiple_of(rs * GRID_W, GRID_W), nkeys), :]
            dr0 = rs - r + WIN_H - 1
            bias = jnp.concatenate(
                [jnp.concatenate([tbl_ref[0, half, dr0 + 2 * m] for m in range(WIN_H // 2)], axis=1)
                 for half in range(2)], axis=0)
            scores.append(_dot_nt(q2, k) + bias)
        probs = []
        for s in scores:
            p = jnp.exp(s - jnp.max(s, axis=-1, keepdims=True))
            probs.append((p.astype(BF16), jnp.sum(p, axis=-1, keepdims=True)))
        for r, rs, (p, l) in zip(rr, rss, probs):
            v = v_ref[0, pl.ds(pl.multiple_of(rs * GRID_W, GRID_W), nkeys), :]
            o = _dot(p, v) / l
            o = jnp.where(first, o[:GRID_W], o[GRID_W:])
            o_ref[0, pl.ds(pl.multiple_of(r * GRID_W, GRID_W), GRID_W), :] = o.astype(o_ref.dtype)
        return carry

    lax.fori_loop(0, rows // NA_ROWS_PER_STEP, rows_step, 0)


def _na_bias_table(rpb):
    qc = np.arange(GRID_W)[:, None]
    kc = np.arange(GRID_W)[None, :]
    cs = np.clip(qc - WIN_W // 2, 0, GRID_W - WIN_W)
    col_in = (kc >= cs) & (kc < cs + WIN_W)
    side = GRID_W - WIN_W
    wide = jnp.pad(rpb, ((0, 0), (0, 0), (side, side)))
    t = jnp.stack([wide[:, :, GRID_W - 1 - q:2 * GRID_W - 1 - q] for q in range(GRID_W)], axis=2)
    t = jnp.where(col_in, t, NEG).astype(F32)
    t2 = jnp.concatenate([t[:, :-1], t[:, 1:]], axis=-1)
    return t2.reshape(NA_HEADS // 2, 2, 2 * WIN_H - 2, GRID_W, 2 * GRID_W)


def _na_attention(h3, tbl):
    b, s, _ = h3.shape
    rows = s // GRID_W
    npair = NA_HEADS // 2
    return pl.pallas_call(
        functools.partial(_na_kernel, rows=rows),
        grid=(b, npair),
        in_specs=[pl.BlockSpec((1, s, LANES), lambda i, p: (i, 0, p)),
                  pl.BlockSpec((1, s, LANES), lambda i, p: (i, 0, npair + p)),
                  pl.BlockSpec((1, s, LANES), lambda i, p: (i, 0, 2 * npair + p)),
                  pl.BlockSpec((1, 2, 2 * WIN_H - 2, GRID_W, 2 * GRID_W), lambda i, p: (p, 0, 0, 0, 0))],
        out_specs=pl.BlockSpec((1, s, LANES), lambda i, p: (i, 0, p)),
        out_shape=jax.ShapeDtypeStruct((b, s, NA_W), BF16),
        compiler_params=_cparams("parallel", "parallel"),
        name="na_attention",
    )(h3, h3, h3, tbl)


def _outproj_ln_kernel(ya_ref, qm_ref, mk_ref, mv_ref, wa_ref, wm_ref, x_ref, g_ref, b_ref, or_ref, *, tm):
    lane = lax.broadcasted_iota(I32, (1, LANES), 1)
    first = lane < MEM_DH
    q = qm_ref[...] * jnp.asarray(MEM_DH ** -0.5, BF16)
    cols = [slice(p * LANES, (p + 1) * LANES) for p in range(MEM_HEADS // 2)]
    scores = []
    for c in cols:
        qp = q[:, c]
        q2 = jnp.concatenate([jnp.where(first, qp, jnp.zeros_like(qp)),
                              jnp.where(first, jnp.zeros_like(qp), qp)], axis=0)
        scores.append(_dot_nt(q2, mk_ref[0, :, c]))
    probs = []
    for s in scores:
        p = jnp.exp(s - jnp.max(s, axis=-1, keepdims=True))
        probs.append((p.astype(BF16), jnp.sum(p, axis=-1, keepdims=True)))
    outs = []
    for c, (p, l) in zip(cols, probs):
        o = _dot(p, mv_ref[0, :, c]) / l
        outs.append(jnp.where(first, o[:tm], o[tm:]))
    ym = jnp.concatenate(outs, axis=1).astype(BF16)
    acc = _dot(ya_ref[...], wa_ref[...]) + _dot(ym, wm_ref[...])
    _write_rows(or_ref, _ln(ALPHA * x_ref[...] + acc, g_ref[...], b_ref[...]), tm)


def _outproj_ln(ya, h2, qm_block, mem_k3, mem_v3, wa, wm, x2, g, b, tm=ROW_TILE):
    n = x2.shape[0]
    ka = ya.shape[1]
    nb, nm, _ = mem_k3.shape
    per_batch = n // nb // tm
    full = lambda shape: pl.BlockSpec(shape, lambda i: (0,) * len(shape))
    return pl.pallas_call(
        functools.partial(_outproj_ln_kernel, tm=tm),
        grid=(n // tm,),
        in_specs=[pl.BlockSpec((tm, ka), lambda i: (i, 0)),
                  pl.BlockSpec((tm, MEM_W), lambda i: (i, qm_block)),
                  pl.BlockSpec((1, nm, MEM_W), lambda i: (i // per_batch, 0, 0)),
                  pl.BlockSpec((1, nm, MEM_W), lambda i: (i // per_batch, 0, 0)),
                  full((ka, D_MODEL)), full((MEM_W, D_MODEL)),
                  pl.BlockSpec((tm, D_MODEL), lambda i: (i, 0)),
                  full((1, D_MODEL)), full((1, D_MODEL))],
        out_specs=pl.BlockSpec((tm * ROW_CHUNKS, LANES), lambda i: (i, 0)),
        out_shape=jax.ShapeDtypeStruct((n * ROW_CHUNKS, LANES), F32),
        compiler_params=_cparams("parallel"),
        name="outproj_ln",
    )(ya, h2, mem_k3, mem_v3, wa, wm, x2, g, b)


def _router_kernel(x_ref, rwh_ref, rwl_ref, rb_ref, lpos_ref, w_ref, cnt_ref, tcnt_ref, toff_ref, tbef_ref, *, tm):
    @pl.when(pl.program_id(0) == 0)
    def _():
        cnt_ref[...] = jnp.zeros_like(cnt_ref)

    x = _read_rows(x_ref, tm)
    xh = x.astype(BF16)
    xl = (x - xh.astype(F32)).astype(BF16)
    logits_t = _dot(xh, rwh_ref[...]) + (_dot(xh, rwl_ref[...]) + _dot(xl, rwh_ref[...]))
    logits = logits_t.T[:N_EXPERTS]
    scores = jax.nn.sigmoid(logits)
    biased = scores + rb_ref[...]
    bv = [biased[e:e + 1, :] for e in range(N_EXPERTS)]
    sv = [scores[e:e + 1, :] for e in range(N_EXPERTS)]

    grp = []
    for g in range(N_GROUPS):
        m = bv[g * EXPERTS_PER_GROUP:(g + 1) * EXPERTS_PER_GROUP]
        best = None
        for a in range(EXPERTS_PER_GROUP):
            for c in range(a + 1, EXPERTS_PER_GROUP):
                pair = m[a] + m[c]
                best = pair if best is None else jnp.maximum(best, pair)
        grp.append(best)
    gsel = jnp.zeros((1, tm), I32)
    gbest = grp[0]
    for g in range(1, N_GROUPS):
        better = grp[g] > gbest
        gsel = jnp.where(better, g, gsel)
        gbest = jnp.where(better, grp[g], gbest)

    def pick(vals, j):
        out = vals[j]
        for g in range(1, N_GROUPS):
            out = jnp.where(gsel == g, vals[g * EXPERTS_PER_GROUP + j], out)
        return out

    cb = [pick(bv, j) for j in range(EXPERTS_PER_GROUP)]
    cs = [pick(sv, j) for j in range(EXPERTS_PER_GROUP)]
    i1 = jnp.zeros((1, tm), I32)
    m1 = cb[0]
    s1 = cs[0]
    for j in range(1, EXPERTS_PER_GROUP):
        gt = cb[j] > m1
        i1 = jnp.where(gt, j, i1)
        m1 = jnp.where(gt, cb[j], m1)
        s1 = jnp.where(gt, cs[j], s1)
    i2 = jnp.zeros((1, tm), I32)
    m2 = jnp.full((1, tm), -jnp.inf, F32)
    s2 = jnp.zeros((1, tm), F32)
    for j in range(EXPERTS_PER_GROUP):
        ok = jnp.logical_and(i1 != j, cb[j] > m2)
        i2 = jnp.where(ok, j, i2)
        m2 = jnp.where(ok, cb[j], m2)
        s2 = jnp.where(ok, cs[j], s2)
    e1 = gsel * EXPERTS_PER_GROUP + i1
    e2 = gsel * EXPERTS_PER_GROUP + i2
    tot = s1 + s2
    w_ref[...] = jnp.concatenate([s1 / tot, s2 / tot], axis=0)

    i = pl.program_id(0)
    eio = lax.broadcasted_iota(I32, (N_EXPERTS, tm), 0)
    oh1 = eio == e1
    oh2 = eio == e2
    ohs = jnp.logical_or(oh1, oh2).astype(F32)
    before = (lax.broadcasted_iota(I32, (tm, tm), 0) < lax.broadcasted_iota(I32, (tm, tm), 1))
    pre = _dot(ohs.astype(BF16), before.astype(BF16))
    tile_cnt = jnp.sum(ohs, axis=1, keepdims=True)
    offs = []
    acc = jnp.zeros((1, 1), F32)
    for e in range(N_EXPERTS):
        offs.append(acc)
        acc = acc + tile_cnt[e:e + 1, :]
    tile_off = jnp.concatenate(offs, axis=0)
    pos = tile_off + pre
    p1 = jnp.sum(jnp.where(oh1, pos, 0.0), axis=0, keepdims=True)
    p2 = jnp.sum(jnp.where(oh2, pos, 0.0), axis=0, keepdims=True)
    lpos_ref[...] = jnp.concatenate([p1, p2], axis=0).astype(I32) * ROW_CHUNKS

    @pl.when(i == 0)
    def _():
        for ref in (tcnt_ref, toff_ref, tbef_ref):
            ref[...] = jnp.zeros_like(ref)

    here = lax.broadcasted_iota(I32, (1, LANES), 1) == i
    tcnt_ref[...] = jnp.where(here, tile_cnt, tcnt_ref[...])
    toff_ref[...] = jnp.where(here, tile_off, toff_ref[...])
    tbef_ref[...] = jnp.where(here, cnt_ref[:, 0:1], tbef_ref[...])
    cnt_ref[...] += tile_cnt


def _router(xr, rw, rb, tm):
    n = xr.shape[0] // ROW_CHUNKS
    assert n // tm <= LANES
    table = pl.BlockSpec((N_EXPERTS, LANES), lambda i: (0, 0))
    return pl.pallas_call(
        functools.partial(_router_kernel, tm=tm),
        grid=(n // tm,),
        in_specs=[pl.BlockSpec((tm * ROW_CHUNKS, LANES), lambda i: (i, 0)),
                  pl.BlockSpec((D_MODEL, LANES), lambda i: (0, 0)),
                  pl.BlockSpec((D_MODEL, LANES), lambda i: (0, 0)),
                  pl.BlockSpec((N_EXPERTS, 1), lambda i: (0, 0))],
        out_specs=[pl.BlockSpec((2, tm), lambda i: (0, i)),
                   pl.BlockSpec((2, tm), lambda i: (0, i)),
                   table, table, table, table],
        out_shape=[jax.ShapeDtypeStruct((2, n), I32),
                   jax.ShapeDtypeStruct((2, n), F32)]
                  + [jax.ShapeDtypeStruct((N_EXPERTS, LANES), F32)] * 4,
        compiler_params=_cparams("arbitrary"),
        name="router",
    )(xr, rw[0], rw[1], rb)


def _plan_kernel(cnt_ref, tbef_ref, meta_ref, rstart_ref, *, nbl):
    shift = MOE_BM.bit_length() - 1
    cnt = cnt_ref[...].astype(I32)
    padded = ((cnt + (MOE_BM - 1)) >> shift) << shift
    starts = []
    acc = jnp.zeros((1, LANES), I32)
    for e in range(N_EXPERTS):
        starts.append(acc)
        acc = acc + padded[e:e + 1, :]
    pad_start = jnp.concatenate(starts, axis=0)
    pad_end = pad_start + padded
    rstart_ref[...] = pad_start + tbef_ref[...].astype(I32)
    blk0 = lax.broadcasted_iota(I32, (N_EXPERTS, nbl), 1) * MOE_BM
    block_e = jnp.sum((pad_end[:, 0:1] <= blk0).astype(I32), axis=0, keepdims=True)
    block_e = jnp.minimum(block_e, N_EXPERTS - 1)
    n_used = jnp.broadcast_to(acc[:, 0:1] >> shift, (1, nbl))
    diag = lax.broadcasted_iota(I32, (N_EXPERTS, nbl), 0) == lax.broadcasted_iota(I32, (N_EXPERTS, nbl), 1)
    fill_lo = jnp.sum(jnp.where(diag, (pad_start + cnt)[:, 0:1], 0), axis=0, keepdims=True)
    fill_hi = jnp.sum(jnp.where(diag, pad_end[:, 0:1], 0), axis=0, keepdims=True)
    meta_ref[...] = jnp.concatenate([block_e, n_used, fill_lo, fill_hi, jnp.zeros((SUBLANES - 4, nbl), I32)],
                                    axis=0)


def _plan(cnt, tbef, n_blocks):
    nbl = -(-n_blocks // LANES) * LANES
    table = pl.BlockSpec((N_EXPERTS, LANES), lambda i: (0, 0))
    return pl.pallas_call(
        functools.partial(_plan_kernel, nbl=nbl),
        grid=(1,),
        in_specs=[table, table],
        out_specs=[pl.BlockSpec((SUBLANES, nbl), lambda i: (0, 0)), table],
        out_shape=[jax.ShapeDtypeStruct((SUBLANES, nbl), I32),
                   jax.ShapeDtypeStruct((N_EXPERTS, LANES), I32)],
        compiler_params=_cparams("arbitrary"),
        name="moe_plan",
    )(cnt, tbef)


def _rows(ref, row, nrows):
    return ref.at[pl.ds(pl.multiple_of(row * ROW_CHUNKS, ROW_CHUNKS), nrows * ROW_CHUNKS), :]


def _rows_wait(src_hbm, buf, sem):
    pltpu.make_async_copy(src_hbm.at[pl.ds(0, buf.shape[0]), :], buf, sem).wait()


def _copy_pieces(src, src_row, dst, dst_row, count, max_rows, sem, wait=False):
    bit = max_rows.bit_length() - 1
    while bit >= 0:
        size = 1 << bit
        done = (count >> (bit + 1)) << (bit + 1)

        @pl.when(((count >> bit) & 1) == 1)
        def _():
            cp = pltpu.make_async_copy(_rows(src, src_row + done, size), _rows(dst, dst_row + done, size), sem)
            cp.start()
            if wait:
                cp.wait()

        bit -= 1


def _tile_runs(tcnt_ref, toff_ref, rstart_ref, tile, buf, hbm, sem, *, to_hbm, tm):
    def per_expert(e, carry):
        k = tile * N_EXPERTS + e
        if to_hbm:
            _copy_pieces(buf, toff_ref[k], hbm, rstart_ref[k], tcnt_ref[k], tm, sem)
        else:
            _copy_pieces(hbm, rstart_ref[k], buf, toff_ref[k], tcnt_ref[k], tm, sem)
        return carry

    lax.fori_loop(0, N_EXPERTS, per_expert, 0)


def _dispatch_kernel(lpos_ref, tcnt_ref, toff_ref, rstart_ref, flo_ref, fhi_ref, nu_ref, x_ref, old_hbm, xs_hbm,
                     s0, s1, zbuf, sem, zsem, *, n, tm, n_blocks, reuse):
    del old_hbm
    i = pl.program_id(0)
    nt = pl.num_programs(0)
    bufs = (s0, s1)
    unroll = PLACE_UNROLL

    for slot in range(2):
        @pl.when(i % 2 == slot)
        def _():
            buf = bufs[slot]

            @pl.when(i >= 2)
            def _():
                _rows_wait(xs_hbm, buf, sem.at[slot])

            def place(c, carry):
                tok = i * tm + c * unroll
                src = pl.multiple_of(c * (unroll * ROW_CHUNKS), unroll * ROW_CHUNKS)
                for u in range(unroll):
                    v = x_ref[pl.ds(src + u * ROW_CHUNKS, ROW_CHUNKS), :]
                    for k in range(2):
                        p = lpos_ref[k * n + tok + u]
                        buf[pl.ds(pl.multiple_of(p, ROW_CHUNKS), ROW_CHUNKS), :] = v
                return carry

            lax.fori_loop(0, tm // unroll, place, 0)
            _tile_runs(tcnt_ref, toff_ref, rstart_ref, i, buf, xs_hbm, sem.at[slot], to_hbm=True, tm=tm)

    zero_rows = n_blocks * MOE_BM - 2 * n
    if not reuse:
        @pl.when(i == 0)
        def _():
            zbuf[...] = jnp.zeros_like(zbuf)
            for e in range(N_EXPERTS):
                _copy_pieces(zbuf, 0, xs_hbm, flo_ref[e], fhi_ref[e] - flo_ref[e], MOE_BM // 2, zsem)

            def zero_block(j, carry):
                pltpu.make_async_copy(zbuf, _rows(xs_hbm, j * MOE_BM, MOE_BM), zsem).start()
                return carry

            lax.fori_loop(nu_ref[0], n_blocks, zero_block, 0)

    @pl.when(i == nt - 1)
    def _():
        for slot in range(2):
            @pl.when(nt > slot)
            def _():
                _rows_wait(xs_hbm, bufs[slot], sem.at[slot])

        if not reuse:
            pltpu.make_async_copy(_rows(xs_hbm, 0, zero_rows), _rows(xs_hbm, 0, zero_rows), zsem).wait()


def _dispatch(lpos_flat, tcnt, toff, rstart, fill_lo, fill_hi, n_used, xr, n_blocks, tm, old=None):
    n = xr.shape[0] // ROW_CHUNKS
    reuse = old is not None
    if not reuse:
        old = jnp.zeros((SUBLANES, LANES), F32)
    return pl.pallas_call(
        functools.partial(_dispatch_kernel, n=n, tm=tm, n_blocks=n_blocks, reuse=reuse),
        grid_spec=pltpu.PrefetchScalarGridSpec(
            num_scalar_prefetch=7,
            grid=(n // tm,),
            in_specs=[pl.BlockSpec((tm * ROW_CHUNKS, LANES), lambda i, *_: (i, 0)),
                      pl.BlockSpec(memory_space=pl.ANY)],
            out_specs=pl.BlockSpec(memory_space=pl.ANY),
            scratch_shapes=[pltpu.VMEM((2 * tm * ROW_CHUNKS, LANES), F32),
                            pltpu.VMEM((2 * tm * ROW_CHUNKS, LANES), F32),
                            pltpu.VMEM((MOE_BM * ROW_CHUNKS, LANES), F32),
                            pltpu.SemaphoreType.DMA((2,)),
                            pltpu.SemaphoreType.DMA(())]),
        out_shape=jax.ShapeDtypeStruct((n_blocks * MOE_BM * ROW_CHUNKS, LANES), F32),
        input_output_aliases={8: 0} if reuse else {},
        compiler_params=_cparams("arbitrary"),
        name="moe_dispatch",
    )(lpos_flat, tcnt, toff, rstart, fill_lo, fill_hi, n_used, xr, old)


EXPERT_BUFS = 3


def _experts_kernel(be_ref, nu_ref, xs_hbm, wg_ref, wu_ref, wd_ref, y_hbm, wgb, wub, wdb, xin, yout,
                    sem_in, sem_out):
    j = pl.program_id(0)
    nb = pl.num_programs(0)
    n_used = nu_ref[0]
    used = j < n_used
    ahead = EXPERT_BUFS - 1

    def fetch(blk, slot):
        return pltpu.make_async_copy(_rows(xs_hbm, blk * MOE_BM, MOE_BM), xin.at[slot], sem_in.at[slot])

    def write_back(blk, slot):
        return pltpu.make_async_copy(yout.at[slot], _rows(y_hbm, blk * MOE_BM, MOE_BM), sem_out.at[slot])

    @pl.when(j == 0)
    def _():
        for d in range(ahead):
            @pl.when(d < n_used)
            def _():
                fetch(d, d).start()

    @pl.when(jnp.logical_and(used, jnp.logical_or(j == 0, be_ref[j] != be_ref[jnp.maximum(j - 1, 0)])))
    def _():
        wgb[...] = wg_ref[0, 0].astype(BF16)
        wub[...] = wu_ref[0, 0].astype(BF16)
        wdb[...] = wd_ref[0, 0].astype(BF16)

    for slot in range(EXPERT_BUFS):
        @pl.when(jnp.logical_and(used, j % EXPERT_BUFS == slot))
        def _():
            @pl.when(j >= EXPERT_BUFS)
            def _():
                write_back(j - EXPERT_BUFS, slot).wait()

            @pl.when(j + ahead < n_used)
            def _():
                fetch(j + ahead, (slot + ahead) % EXPERT_BUFS).start()

            fetch(j, slot).wait()
            x = _read_rows(xin.at[slot], MOE_BM).astype(BF16)
            h = _silu(_dot(x, wgb[...])) * _dot(x, wub[...])
            _write_rows(yout.at[slot], _dot(h.astype(BF16), wdb[...]), MOE_BM)
            write_back(j, slot).start()

    @pl.when(j == nb - 1)
    def _():
        for back in range(1, EXPERT_BUFS + 1):
            @pl.when(n_used - back >= 0)
            def _():
                blk = n_used - back
                pltpu.make_async_copy(yout.at[0], _rows(y_hbm, blk * MOE_BM, MOE_BM),
                                      sem_out.at[blk % EXPERT_BUFS]).wait()


def _experts(block_e, n_used, xs, wg, wu, wd, layer):
    n_blocks = block_e.shape[0]

    def last_used(j, nu):
        return jnp.minimum(j, nu[0] - 1)

    def wblk(j, be, nu):
        return (layer, be[last_used(j, nu)], 0, 0)

    return pl.pallas_call(
        _experts_kernel,
        grid_spec=pltpu.PrefetchScalarGridSpec(
            num_scalar_prefetch=2,
            grid=(n_blocks,),
            in_specs=[pl.BlockSpec(memory_space=pl.ANY),
                      pl.BlockSpec((1, 1, D_MODEL, D_EXPERT), wblk),
                      pl.BlockSpec((1, 1, D_MODEL, D_EXPERT), wblk),
                      pl.BlockSpec((1, 1, D_EXPERT, D_MODEL), wblk)],
            out_specs=pl.BlockSpec(memory_space=pl.ANY),
            scratch_shapes=[pltpu.VMEM((D_MODEL, D_EXPERT), BF16), pltpu.VMEM((D_MODEL, D_EXPERT), BF16),
                            pltpu.VMEM((D_EXPERT, D_MODEL), BF16),
                            pltpu.VMEM((EXPERT_BUFS, MOE_BM * ROW_CHUNKS, LANES), F32),
                            pltpu.VMEM((EXPERT_BUFS, MOE_BM * ROW_CHUNKS, LANES), F32),
                            pltpu.SemaphoreType.DMA((EXPERT_BUFS,)), pltpu.SemaphoreType.DMA((EXPERT_BUFS,))]),
        out_shape=jax.ShapeDtypeStruct(xs.shape, F32),
        input_output_aliases={2: 0},
        compiler_params=_cparams("arbitrary"),
        name="moe_experts",
    )(block_e, n_used, xs, wg, wu, wd)


def _combine_ln_kernel(lpos_ref, tcnt_ref, toff_ref, rstart_ref, y_hbm, x_ref, w1_ref, w2_ref, g_ref, b_ref, o_ref,
                       r0, r1, u1, u2, sem, *, n, tm):
    i = pl.program_id(0)
    nt = pl.num_programs(0)
    bufs = (r0, r1)
    unroll = PLACE_UNROLL

    def fetch(tile, slot):
        _tile_runs(tcnt_ref, toff_ref, rstart_ref, tile, bufs[slot], y_hbm, sem.at[slot], to_hbm=False, tm=tm)

    @pl.when(i == 0)
    def _():
        fetch(0, 0)

    for slot in range(2):
        @pl.when(i % 2 == slot)
        def _():
            @pl.when(i + 1 < nt)
            def _():
                fetch(i + 1, 1 - slot)

            buf = bufs[slot]
            _rows_wait(y_hbm, buf, sem.at[slot])

            def place(c, carry):
                tok = i * tm + c * unroll
                dst0 = pl.multiple_of(c * (unroll * ROW_CHUNKS), unroll * ROW_CHUNKS)
                for u in range(unroll):
                    dst = pl.ds(dst0 + u * ROW_CHUNKS, ROW_CHUNKS)
                    for k, out in enumerate((u1, u2)):
                        p = lpos_ref[k * n + tok + u]
                        out[dst, :] = buf[pl.ds(pl.multiple_of(p, ROW_CHUNKS), ROW_CHUNKS), :]
                return carry

            lax.fori_loop(0, tm // unroll, place, 0)
            moe = w1_ref[...] * _read_rows(u1, tm) + w2_ref[...] * _read_rows(u2, tm)
            o_ref[...] = _ln(ALPHA * _read_rows(x_ref, tm) + moe, g_ref[...], b_ref[...])


def _combine_ln(lpos_flat, tcnt, toff, rstart, y, xr, w1, w2, g, b, tm):
    n = xr.shape[0] // ROW_CHUNKS
    return pl.pallas_call(
        functools.partial(_combine_ln_kernel, n=n, tm=tm),
        grid_spec=pltpu.PrefetchScalarGridSpec(
            num_scalar_prefetch=4,
            grid=(n // tm,),
            in_specs=[pl.BlockSpec(memory_space=pl.ANY),
                      pl.BlockSpec((tm * ROW_CHUNKS, LANES), lambda i, *_: (i, 0)),
                      pl.BlockSpec((tm, 1), lambda i, *_: (i, 0)),
                      pl.BlockSpec((tm, 1), lambda i, *_: (i, 0)),
                      pl.BlockSpec((1, D_MODEL), lambda i, *_: (0, 0)),
                      pl.BlockSpec((1, D_MODEL), lambda i, *_: (0, 0))],
            out_specs=pl.BlockSpec((tm, D_MODEL), lambda i, *_: (i, 0)),
            scratch_shapes=[pltpu.VMEM((2 * tm * ROW_CHUNKS, LANES), F32)] * 2
                           + [pltpu.VMEM((tm * ROW_CHUNKS, LANES), F32)] * 2
                           + [pltpu.SemaphoreType.DMA((2,))]),
        out_shape=jax.ShapeDtypeStruct((n, D_MODEL), F32),
        compiler_params=_cparams("arbitrary"),
        name="moe_combine_ln",
    )(lpos_flat, tcnt, toff, rstart, y, xr, w1, w2, g, b)


def _moe_ln(xr, rw, rb, wg, wu, wd, layer, g, b, old=None):
    n = xr.shape[0] // ROW_CHUNKS
    n_blocks = (2 * n) // MOE_BM + N_EXPERTS
    tm = MOE_TILE
    nt = n // tm
    lpos, w, cnt, tcnt, toff, tbef = _router(xr, rw, rb, tm)
    meta, rstart = _plan(cnt, tbef, n_blocks)
    block_e = meta[0, :n_blocks]
    n_used = meta[1, :1]

    def per_tile(table):
        return table[:, :nt].T.reshape(nt * N_EXPERTS).astype(I32)

    lpos_flat = lpos.reshape(2 * n)
    tcnt, toff, rstart = per_tile(tcnt), per_tile(toff), per_tile(rstart)
    xs = _dispatch(lpos_flat, tcnt, toff, rstart, meta[2, :N_EXPERTS], meta[3, :N_EXPERTS], n_used, xr,
                   n_blocks, tm, old)
    y = _experts(block_e, n_used, xs, wg, wu, wd, layer)
    out = _combine_ln(lpos_flat, tcnt, toff, rstart, y, xr, w[0].reshape(n, 1), w[1].reshape(n, 1), g, b, tm)
    return out, y


def _conv_qkv_kernel(xm_ref, cw_ref, cb_ref, wq_ref, wk_ref, wv_ref, q_ref, k_ref, v_ref, xc_ref, *, s):
    xm_b = xm_ref[0]
    xm = xm_b.astype(F32)
    cw = cw_ref[...]
    row = lax.broadcasted_iota(I32, (s, 1), 0)
    half = CONV_K // 2
    acc = cb_ref[...] + xm * cw[half:half + 1, :]
    for sh in range(1, half + 1):
        past = jnp.where(row >= sh, pltpu.roll(xm, sh, axis=0), 0.0)
        acc = acc + past * cw[half - sh:half - sh + 1, :]
        nxt = jnp.where(row < s - sh, pltpu.roll(xm, s - sh, axis=0), 0.0)
        acc = acc + nxt * cw[half + sh:half + sh + 1, :]
    xc = _silu(acc).astype(BF16)
    xc_ref[0] = xc
    q_ref[0] = _dot(xc, wq_ref[0]).astype(BF16)
    k_ref[0] = (_dot_nt(wk_ref[0], xc) * (ML_DH ** -0.5)).astype(BF16)
    v = _dot(xm_b, wv_ref[0])
    ones_lane = lax.broadcasted_iota(I32, (1, ML_DHP), 1) == ML_DH
    v_ref[0] = jnp.where(ones_lane, 1.0, v).astype(BF16)


def _conv_qkv(main3, cw, cb, wq, wk_t, wv):
    b, s, _ = main3.shape
    tok = pl.BlockSpec((1, s, ML_DHP), lambda i, h: (i, 0, h))
    wspec = pl.BlockSpec((1, ML_DHP, ML_DHP), lambda i, h: (h, 0, 0))
    tok_shape = jax.ShapeDtypeStruct((b, s, ML_WP), BF16)
    return pl.pallas_call(
        functools.partial(_conv_qkv_kernel, s=s),
        grid=(b, ML_HEADS),
        in_specs=[tok,
                  pl.BlockSpec((CONV_K, ML_DHP), lambda i, h: (0, h)),
                  pl.BlockSpec((1, ML_DHP), lambda i, h: (0, h)),
                  wspec, wspec, wspec],
        out_specs=[tok, pl.BlockSpec((1, ML_DHP, s), lambda i, h: (i, h, 0)), tok, tok],
        out_shape=[tok_shape, jax.ShapeDtypeStruct((b, ML_WP, s), BF16), tok_shape, tok_shape],
        compiler_params=_cparams("parallel", "parallel"),
        name="conv_qkv",
    )(main3, cw, cb, wq, wk_t, wv)


def _mlstm_kernel(q_ref, kt_ref, v_ref, gc_ref, gr_ref, z_ref, xc_ref, ng_ref, sk_ref,
                  y_ref, hf_ref, hb_ref, cf_ref, cb_ref, m_ref, *, s):
    head0 = pl.program_id(1) * ML_HPS
    nc = s // CHUNK
    sub = lax.broadcasted_iota(I32, (LANES, 1), 0)
    gate = lax.broadcasted_iota(I32, (LANES, LANES), 0)
    ti = lax.broadcasted_iota(I32, (CHUNK, CHUNK), 0)
    tj = lax.broadcasted_iota(I32, (CHUNK, CHUNK), 1)

    for ref in (cf_ref, cb_ref, m_ref):
        ref[...] = jnp.zeros_like(ref)

    def intra(c, j, rev):
        t0 = pl.multiple_of(c * CHUNK, CHUNK)
        hl = slice(j * ML_DHP, (j + 1) * ML_DHP)
        qb = q_ref[0, pl.ds(t0, CHUNK), hl]
        kt = kt_ref[0, hl, pl.ds(t0, CHUNK)]
        vb = v_ref[0, pl.ds(t0, CHUNK), hl]
        gc = gc_ref[0, pl.ds(t0, CHUNK), :]
        gr = gr_ref[:, pl.ds(t0, CHUNK)]
        i_idx = head0 + j + (2 * ML_HEADS if rev else 0)
        f_idx = i_idx + ML_HEADS
        allowed = (tj >= ti) if rev else (tj <= ti)
        sel = (gate == f_idx).astype(BF16)
        b_rep = sum(_dot(part, sel) for part in _split3(gc))
        b_row = jnp.sum(jnp.where(sub == f_idx, gr, 0.0), axis=0, keepdims=True)
        i_row = jnp.sum(jnp.where(sub == i_idx, gr, 0.0), axis=0, keepdims=True)
        b_last = (b_rep[0:1, :] if rev else b_rep[CHUNK - 1:CHUNK, :])[:, 0:1]

        b_wide = jnp.concatenate([b_rep] * (CHUNK // LANES), axis=1)
        d = jnp.where(allowed, b_wide - b_row + i_row, NEG)
        m_in = jnp.max(d, axis=1, keepdims=True)
        sc = _dot(qb, kt) * jnp.exp(d - m_in)
        nd_in = _dot(sc.astype(BF16), vb)
        w_row = b_last - b_row + i_row
        return t0, qb, kt, vb, b_rep, b_last, m_in, nd_in, w_row

    def twice(a):
        return jnp.concatenate([a, a], axis=1)

    def update(parts, j, rev):
        t0, qb, kt, vb, b_rep, b_last, m_in, nd_in, w_row = parts
        h_ref, c_ref = (hb_ref, cb_ref) if rev else (hf_ref, cf_ref)
        hl = slice(j * ML_DHP, (j + 1) * ML_DHP)
        mrow = 2 * j + int(rev)
        m = m_ref[mrow:mrow + 1, 0:1]
        cmat = c_ref[j]
        inter = b_rep + m
        m_t = jnp.maximum(m_in, inter)
        a_in = jnp.exp(m_in - m_t)
        iexp = jnp.exp(inter - m_t)
        nd = twice(a_in) * nd_in + twice(iexp) * _dot(qb, cmat.astype(BF16))
        den = nd[:, ML_DH:ML_DH + 1]
        h_ref[pl.ds(t0, CHUNK), hl] = nd * (1.0 / jnp.maximum(jnp.abs(den), jnp.exp(-m_t[:, 0:1])))

        m_new = jnp.maximum(b_last + m, jnp.max(w_row, axis=1, keepdims=True))
        wexp = jnp.exp(w_row - m_new)
        cexp = jnp.exp(b_last + m - m_new)
        kw = (kt.astype(F32) * wexp).astype(BF16)
        c_ref[j] = cexp * cmat + _dot(kw, vb)
        m_ref[mrow:mrow + 1, :] = jnp.broadcast_to(m_new, (1, LANES))

    def step(i, carry):
        for j in range(ML_HPS):
            parts = [intra(nc - 1 - i if rev else i, j, rev) for rev in (False, True)]
            for p, rev in zip(parts, (False, True)):
                update(p, j, rev)
        return carry

    lax.fori_loop(0, nc, step, 0)

    real = lax.broadcasted_iota(I32, (1, ML_DHP), 1) < ML_DH
    tb = CHUNK

    def fin(c, carry):
        t0 = pl.multiple_of(c * tb, tb)
        for j in range(ML_HPS):
            hl = slice(j * ML_DHP, (j + 1) * ML_DHP)
            hs = jnp.where(real, hf_ref[pl.ds(t0, tb), hl] + hb_ref[pl.ds(t0, tb), hl], 0.0)
            mu = jnp.sum(hs, axis=1, keepdims=True) * (1.0 / ML_DH)
            dev = jnp.where(real, hs - mu, 0.0)
            var = jnp.sum(dev * dev, axis=1, keepdims=True) * (1.0 / ML_DH)
            hn = dev * lax.rsqrt(var + LN_EPS) * ng_ref[:, hl]
            xc = xc_ref[0, pl.ds(t0, tb), hl].astype(F32)
            z = z_ref[0, pl.ds(t0, tb), hl].astype(F32)
            y_ref[0, pl.ds(t0, tb), hl] = ((hn + sk_ref[:, hl] * xc) * _silu(z)).astype(BF16)
        return carry

    lax.fori_loop(0, s // tb, fin, 0)


def _mlstm(q, kt, v, gcol3, grow, main3, xc, ng, sk):
    b, s, _ = q.shape
    width = ML_HPS * ML_DHP
    steps = ML_HEADS // ML_HPS
    tok = pl.BlockSpec((1, s, width), lambda i, h: (i, 0, h))
    vec = pl.BlockSpec((1, width), lambda i, h: (0, h))
    return pl.pallas_call(
        functools.partial(_mlstm_kernel, s=s),
        grid=(b, steps),
        in_specs=[tok, pl.BlockSpec((1, width, s), lambda i, h: (i, h, 0)), tok,
                  pl.BlockSpec((1, s, LANES), lambda i, h: (i, 0, 0)),
                  pl.BlockSpec((LANES, s), lambda i, h: (0, i)),
                  pl.BlockSpec((1, s, width), lambda i, h: (i, 0, steps + h)),
                  tok, vec, vec],
        out_specs=tok,
        out_shape=jax.ShapeDtypeStruct((b, s, ML_WP), BF16),
        scratch_shapes=[pltpu.VMEM((s, width), F32), pltpu.VMEM((s, width), F32),
                        pltpu.VMEM((ML_HPS, ML_DHP, ML_DHP), F32), pltpu.VMEM((ML_HPS, ML_DHP, ML_DHP), F32),
                        pltpu.VMEM((SUBLANES, LANES), F32)],
        compiler_params=_cparams("parallel", "parallel"),
        name="mlstm",
    )(q, kt, v, gcol3, grow, main3, xc, ng, sk)


def _pad_heads(a, axis):
    a = jnp.moveaxis(a, axis, -1)
    lead = a.shape[:-1]
    a = a.reshape(lead + (ML_HEADS, ML_DH))
    a = jnp.pad(a, [(0, 0)] * len(lead) + [(0, 0), (0, ML_DHP - ML_DH)])
    return jnp.moveaxis(a.reshape(lead + (ML_WP,)), -1, axis)


def kernel(x, mem, mem_ln_g, mem_ln_b, w_mem_kv, router_w, router_b, na_w_in, na_rpb, ml_w_in, ml_conv_w,
           ml_conv_b, ml_w_qkv, ml_gate_b, ml_norm_g, ml_skip, w_out, ln_g, ln_b, exp_w_gate, exp_w_up,
           exp_w_down):
    b, s, d = x.shape
    n = b * s
    nm = mem.shape[1]
    row = lambda a: a.reshape(1, -1)

    mem_k, mem_v = _memkv(mem.reshape(b * nm, d), row(mem_ln_g), row(mem_ln_b), w_mem_kv.astype(BF16))
    mem_k3 = mem_k.reshape(b, nm, MEM_W)
    mem_v3 = mem_v.reshape(b, nm, MEM_W)
    rw_pad = jnp.pad(router_w, ((0, 0), (0, LANES - N_EXPERTS)))
    rw_hi = rw_pad.astype(BF16)
    rw = (rw_hi, (rw_pad - rw_hi.astype(F32)).astype(BF16))
    rb = router_b.reshape(N_EXPERTS, 1)

    x2 = x.reshape(n, d)

    h0 = _proj(x2, na_w_in[0].astype(BF16)).reshape(b, s, 3 * NA_W + MEM_W)
    y_na = _na_attention(h0, _na_bias_table(na_rpb[0]))
    wo = w_out[0].astype(BF16)
    xr = _outproj_ln(y_na.reshape(n, NA_W), h0.reshape(n, 3 * NA_W + MEM_W), 3 * NA_W // MEM_W, mem_k3, mem_v3,
                     wo[:NA_W], wo[NA_W:], x2, row(ln_g[0, 0]), row(ln_b[0, 0]))
    x2, moe_buf = _moe_ln(xr, rw, rb, exp_w_gate, exp_w_up, exp_w_down, 0, row(ln_g[0, 1]), row(ln_b[0, 1]))

    w1 = ml_w_in[0]
    w_main = jnp.concatenate([_pad_heads(w1[:, :ML_W], 1), _pad_heads(w1[:, ML_W:2 * ML_W], 1),
                              w1[:, 2 * ML_W + 4 * ML_HEADS:]], axis=1).astype(BF16)
    w_g = jnp.pad(w1[:, 2 * ML_W:2 * ML_W + 4 * ML_HEADS], ((0, 0), (0, LANES - 4 * ML_HEADS))).astype(BF16)
    gb = jnp.pad(ml_gate_b[0].reshape(4 * ML_HEADS), (0, LANES - 4 * ML_HEADS))
    main, acol, arow = _proj_gates(x2, w_main, w_g, w_g.T, gb.reshape(1, LANES), gb.reshape(LANES, 1))
    main3 = main.reshape(b, s, 2 * ML_WP + MEM_W)
    wqkv = jnp.pad(ml_w_qkv[0], ((0, 0), (0, 0), (0, ML_DHP - ML_DH), (0, ML_DHP - ML_DH))).astype(BF16)
    q, k, v, xc = _conv_qkv(main3, _pad_heads(ml_conv_w[0], 1), _pad_heads(row(ml_conv_b[0]), 1),
                            wqkv[0], jnp.swapaxes(wqkv[1], 1, 2), wqkv[2])
    y_ml = _mlstm(q, k, v, acol.reshape(b, s, LANES), arow, main3, xc,
                  _pad_heads(row(ml_norm_g[0]), 1), _pad_heads(row(ml_skip[0]), 1))
    wo = w_out[1]
    xr = _outproj_ln(y_ml.reshape(n, ML_WP), main, 2 * ML_WP // MEM_W, mem_k3, mem_v3,
                     _pad_heads(wo[:ML_W], 0).astype(BF16), wo[ML_W:].astype(BF16), x2,
                     row(ln_g[1, 0]), row(ln_b[1, 0]))
    x2, _ = _moe_ln(xr, rw, rb, exp_w_gate, exp_w_up, exp_w_down, 1, row(ln_g[1, 1]), row(ln_b[1, 1]), moe_buf)
    return x2.reshape(b, s, d)
```

```python
import functools

import numpy as np
import jax
import jax.numpy as jnp
from jax import lax
from jax.experimental import pallas as pl
from jax.experimental.pallas import tpu as pltpu

F32 = jnp.float32
BF16 = jnp.bfloat16
I32 = jnp.int32

D_MODEL = 1024
DEPTH = 2
GRID_W = 64
MEM_HEADS = 4
MEM_DH = 64
MEM_W = MEM_HEADS * MEM_DH
NA_HEADS = 12
NA_DH = 64
NA_W = NA_HEADS * NA_DH
WIN_H = 8
WIN_W = 16
ML_HEADS = 4
ML_DH = 192
ML_DHP = 256
ML_W = ML_HEADS * ML_DH
ML_WP = ML_HEADS * ML_DHP
CONV_K = 5
CHUNK = 256
N_EXPERTS = 16
N_GROUPS = 4
EXPERTS_PER_GROUP = N_EXPERTS // N_GROUPS
D_EXPERT = 512
ALPHA = (2 * DEPTH) ** 0.25
LN_EPS = 1e-5
NEG = -1e30

LANES = 128
SUBLANES = 8
ROW_CHUNKS = D_MODEL // LANES
MOE_BM = 512
MOE_TILE = 512
ML_HPS = 2
NA_ROWS_PER_STEP = 16
ROW_TILE = 1024
PLACE_UNROLL = 16
VMEM_LIMIT = 48 * 1024 * 1024


def _cparams(*sem):
    return pltpu.CompilerParams(dimension_semantics=sem, vmem_limit_bytes=VMEM_LIMIT)


def _dot(a, b):
    return jnp.dot(a, b, preferred_element_type=F32)


def _dot_nt(a, b, precision=None):
    return lax.dot_general(a, b, (((1,), (1,)), ((), ())), precision=precision,
                           preferred_element_type=F32)


def _ln(z, g, b):
    mu = jnp.mean(z, axis=-1, keepdims=True)
    zc = z - mu
    var = jnp.mean(zc * zc, axis=-1, keepdims=True)
    return zc * lax.rsqrt(var + LN_EPS) * g + b


def _silu(x):
    return x * jax.nn.sigmoid(x)


def _read_rows(ref, n):
    return jnp.concatenate([ref[pl.ds(j, n, stride=ROW_CHUNKS), :] for j in range(ROW_CHUNKS)], axis=1)


def _write_rows(ref, val, n):
    for j in range(ROW_CHUNKS):
        ref[pl.ds(j, n, stride=ROW_CHUNKS), :] = val[:, j * LANES:(j + 1) * LANES]


def _memkv_kernel(m_ref, g_ref, b_ref, w_ref, k_ref, v_ref):
    z = _ln(m_ref[...], g_ref[...], b_ref[...])
    kv = _dot(z.astype(BF16), w_ref[...])
    k_ref[...] = kv[:, :MEM_W].astype(BF16)
    v_ref[...] = kv[:, MEM_W:].astype(BF16)


def _memkv(mem2, g, b, w):
    n = mem2.shape[0]
    tm = min(ROW_TILE, n)
    return pl.pallas_call(
        _memkv_kernel,
        grid=(n // tm,),
        in_specs=[pl.BlockSpec((tm, D_MODEL), lambda i: (i, 0)),
                  pl.BlockSpec((1, D_MODEL), lambda i: (0, 0)),
                  pl.BlockSpec((1, D_MODEL), lambda i: (0, 0)),
                  pl.BlockSpec((D_MODEL, 2 * MEM_W), lambda i: (0, 0))],
        out_specs=[pl.BlockSpec((tm, MEM_W), lambda i: (i, 0)),
                   pl.BlockSpec((tm, MEM_W), lambda i: (i, 0))],
        out_shape=[jax.ShapeDtypeStruct((n, MEM_W), BF16)] * 2,
        compiler_params=_cparams("parallel"),
        name="memkv",
    )(mem2, g, b, w)


def _proj_kernel(x_ref, w_ref, o_ref):
    o_ref[...] = _dot(x_ref[...].astype(BF16), w_ref[...]).astype(o_ref.dtype)


def _proj(x2, w, tm=ROW_TILE):
    n, k = x2.shape
    nout = w.shape[1]
    return pl.pallas_call(
        _proj_kernel,
        grid=(n // tm,),
        in_specs=[pl.BlockSpec((tm, k), lambda i: (i, 0)),
                  pl.BlockSpec((k, nout), lambda i: (0, 0))],
        out_specs=pl.BlockSpec((tm, nout), lambda i: (i, 0)),
        out_shape=jax.ShapeDtypeStruct((n, nout), BF16),
        compiler_params=_cparams("parallel"),
        name="in_proj",
    )(x2, w)


def _split3(x):
    hi = x.astype(BF16)
    r1 = x - hi.astype(F32)
    mid = r1.astype(BF16)
    lo = (r1 - mid.astype(F32)).astype(BF16)
    return hi, mid, lo


def _proj_gates_kernel(x_ref, w_ref, wg_ref, wgt_ref, gbc_ref, gbr_ref, o_ref, g_ref, gt_ref, *, tm):
    xb = x_ref[...].astype(BF16)
    o_ref[...] = _dot(xb, w_ref[...]).astype(BF16)
    gcol = _dot(xb, wg_ref[...]) + gbc_ref[...]
    grow = _dot_nt(wgt_ref[...], xb) + gbr_ref[...]
    lane = lax.broadcasted_iota(I32, (1, LANES), 1)
    sub = lax.broadcasted_iota(I32, (LANES, 1), 0)
    ti = lax.broadcasted_iota(I32, (CHUNK, CHUNK), 0)
    tj = lax.broadcasted_iota(I32, (CHUNK, CHUNK), 1)
    lower = (tj <= ti).astype(BF16)
    upper = (ti <= tj).astype(BF16)

    def pick(idx, pre, suf, raw):
        fwd = jnp.logical_and(idx >= ML_HEADS, idx < 2 * ML_HEADS)
        bwd = jnp.logical_and(idx >= 3 * ML_HEADS, idx < 4 * ML_HEADS)
        return jnp.where(fwd, pre, jnp.where(bwd, suf, raw))

    for c in range(tm // CHUNK):
        tc = slice(c * CHUNK, (c + 1) * CHUNK)
        g = gcol[tc, :]
        ls = jax.nn.log_sigmoid(g)
        pre = sum(_dot(lower, part) for part in _split3(ls))
        suf = jnp.sum(ls, axis=0, keepdims=True) - pre + ls
        g_ref[tc, :] = pick(lane, pre, suf, g)
        g = grow[:, tc]
        ls = jax.nn.log_sigmoid(g)
        pre = sum(_dot(part, upper) for part in _split3(ls))
        suf = jnp.sum(ls, axis=1, keepdims=True) - pre + ls
        gt_ref[:, tc] = pick(sub, pre, suf, g)


def _proj_gates(x2, w, wg, wgt, gbc, gbr, tm=ROW_TILE):
    n, k = x2.shape
    nout = w.shape[1]
    return pl.pallas_call(
        functools.partial(_proj_gates_kernel, tm=tm),
        grid=(n // tm,),
        in_specs=[pl.BlockSpec((tm, k), lambda i: (i, 0)),
                  pl.BlockSpec((k, nout), lambda i: (0, 0)),
                  pl.BlockSpec((k, LANES), lambda i: (0, 0)),
                  pl.BlockSpec((LANES, k), lambda i: (0, 0)),
                  pl.BlockSpec((1, LANES), lambda i: (0, 0)),
                  pl.BlockSpec((LANES, 1), lambda i: (0, 0))],
        out_specs=[pl.BlockSpec((tm, nout), lambda i: (i, 0)),
                   pl.BlockSpec((tm, LANES), lambda i: (i, 0)),
                   pl.BlockSpec((LANES, tm), lambda i: (0, i))],
        out_shape=[jax.ShapeDtypeStruct((n, nout), BF16),
                   jax.ShapeDtypeStruct((n, LANES), F32),
                   jax.ShapeDtypeStruct((LANES, n), F32)],
        compiler_params=_cparams("parallel"),
        name="in_proj_gates",
    )(x2, w, wg, wgt, gbc, gbr)


def _na_kernel(q_ref, k_ref, v_ref, tbl_ref, o_ref, *, rows):
    lane = lax.broadcasted_iota(I32, (1, LANES), 1)
    first = lane < NA_DH
    nkeys = WIN_H * GRID_W

    def rows_step(i, carry):
        rr = [i * NA_ROWS_PER_STEP + u for u in range(NA_ROWS_PER_STEP)]
        rss = [jnp.clip(r - WIN_H // 2, 0, rows - WIN_H) for r in rr]
        hk = nkeys // 2
        scores = []
        for r, rs in zip(rr, rss):
            q = q_ref[0, pl.ds(pl.multiple_of(r * GRID_W, GRID_W), GRID_W), :]
            q = q * jnp.asarray(NA_DH ** -0.5, BF16)
            q2 = jnp.concatenate([jnp.where(first, q, jnp.zeros_like(q)),
                                  jnp.where(first, jnp.zeros_like(q), q)], axis=0)
            k = k_ref[0, pl.ds(pl.multiple_of(rs * GRID_W, GRID_W), nkeys), :]
            dr0 = rs - r + WIN_H - 1
            bias = jnp.concatenate(
                [jnp.concatenate([tbl_ref[0, half, dr0 + 2 * m] for m in range(WIN_H // 2)], axis=1)
                 for half in range(2)], axis=0)
            scores.append([_dot_nt(q2, k[h * hk:(h + 1) * hk]) + bias[:, h * hk:(h + 1) * hk] for h in range(2)])
        probs = []
        for s0, s1 in scores:
            m = jnp.maximum(jnp.max(s0, axis=-1, keepdims=True), jnp.max(s1, axis=-1, keepdims=True))
            p0, p1 = jnp.exp(s0 - m), jnp.exp(s1 - m)
            l = jnp.sum(p0, axis=-1, keepdims=True) + jnp.sum(p1, axis=-1, keepdims=True)
            probs.append(((p0.astype(BF16), p1.astype(BF16)), l))
        for r, rs, ((p0, p1), l) in zip(rr, rss, probs):
            v = v_ref[0, pl.ds(pl.multiple_of(rs * GRID_W, GRID_W), nkeys), :]
            o = (_dot(p0, v[:hk]) + _dot(p1, v[hk:])) / l
            o = jnp.where(first, o[:GRID_W], o[GRID_W:])
            o_ref[0, pl.ds(pl.multiple_of(r * GRID_W, GRID_W), GRID_W), :] = o.astype(o_ref.dtype)
        return carry

    lax.fori_loop(0, rows // NA_ROWS_PER_STEP, rows_step, 0)


def _na_bias_table(rpb):
    qc = np.arange(GRID_W)[:, None]
    kc = np.arange(GRID_W)[None, :]
    cs = np.clip(qc - WIN_W // 2, 0, GRID_W - WIN_W)
    col_in = (kc >= cs) & (kc < cs + WIN_W)
    side = GRID_W - WIN_W
    wide = jnp.pad(rpb, ((0, 0), (0, 0), (side, side)))
    t = jnp.stack([wide[:, :, GRID_W - 1 - q:2 * GRID_W - 1 - q] for q in range(GRID_W)], axis=2)
    t = jnp.where(col_in, t, NEG).astype(F32)
    t2 = jnp.concatenate([t[:, :-1], t[:, 1:]], axis=-1)
    return t2.reshape(NA_HEADS // 2, 2, 2 * WIN_H - 2, GRID_W, 2 * GRID_W)


def _na_attention(h3, tbl):
    b, s, _ = h3.shape
    rows = s // GRID_W
    npair = NA_HEADS // 2
    return pl.pallas_call(
        functools.partial(_na_kernel, rows=rows),
        grid=(b, npair),
        in_specs=[pl.BlockSpec((1, s, LANES), lambda i, p: (i, 0, p)),
                  pl.BlockSpec((1, s, LANES), lambda i, p: (i, 0, npair + p)),
                  pl.BlockSpec((1, s, LANES), lambda i, p: (i, 0, 2 * npair + p)),
                  pl.BlockSpec((1, 2, 2 * WIN_H - 2, GRID_W, 2 * GRID_W), lambda i, p: (p, 0, 0, 0, 0))],
        out_specs=pl.BlockSpec((1, s, LANES), lambda i, p: (i, 0, p)),
        out_shape=jax.ShapeDtypeStruct((b, s, NA_W), BF16),
        compiler_params=_cparams("parallel", "parallel"),
        name="na_attention",
    )(h3, h3, h3, tbl)


def _outproj_ln_kernel(ya_ref, qm_ref, mk_ref, mv_ref, wa_ref, wm_ref, x_ref, g_ref, b_ref, or_ref, *, tm):
    lane = lax.broadcasted_iota(I32, (1, LANES), 1)
    first = lane < MEM_DH
    q = qm_ref[...] * jnp.asarray(MEM_DH ** -0.5, BF16)
    cols = [slice(p * LANES, (p + 1) * LANES) for p in range(MEM_HEADS // 2)]
    scores = []
    for c in cols:
        qp = q[:, c]
        q2 = jnp.concatenate([jnp.where(first, qp, jnp.zeros_like(qp)),
                              jnp.where(first, jnp.zeros_like(qp), qp)], axis=0)
        scores.append(_dot_nt(q2, mk_ref[0, :, c]))
    probs = []
    for s in scores:
        p = jnp.exp(s - jnp.max(s, axis=-1, keepdims=True))
        probs.append((p.astype(BF16), jnp.sum(p, axis=-1, keepdims=True)))
    outs = []
    for c, (p, l) in zip(cols, probs):
        o = _dot(p, mv_ref[0, :, c]) / l
        outs.append(jnp.where(first, o[:tm], o[tm:]))
    ym = jnp.concatenate(outs, axis=1).astype(BF16)
    acc = _dot(ya_ref[...], wa_ref[...]) + _dot(ym, wm_ref[...])
    _write_rows(or_ref, _ln(ALPHA * x_ref[...] + acc, g_ref[...], b_ref[...]), tm)


def _outproj_ln(ya, h2, qm_block, mem_k3, mem_v3, wa, wm, x2, g, b, tm=ROW_TILE):
    n = x2.shape[0]
    ka = ya.shape[1]
    nb, nm, _ = mem_k3.shape
    per_batch = n // nb // tm
    full = lambda shape: pl.BlockSpec(shape, lambda i: (0,) * len(shape))
    return pl.pallas_call(
        functools.partial(_outproj_ln_kernel, tm=tm),
        grid=(n // tm,),
        in_specs=[pl.BlockSpec((tm, ka), lambda i: (i, 0)),
                  pl.BlockSpec((tm, MEM_W), lambda i: (i, qm_block)),
                  pl.BlockSpec((1, nm, MEM_W), lambda i: (i // per_batch, 0, 0)),
                  pl.BlockSpec((1, nm, MEM_W), lambda i: (i // per_batch, 0, 0)),
                  full((ka, D_MODEL)), full((MEM_W, D_MODEL)),
                  pl.BlockSpec((tm, D_MODEL), lambda i: (i, 0)),
                  full((1, D_MODEL)), full((1, D_MODEL))],
        out_specs=pl.BlockSpec((tm * ROW_CHUNKS, LANES), lambda i: (i, 0)),
        out_shape=jax.ShapeDtypeStruct((n * ROW_CHUNKS, LANES), F32),
        compiler_params=_cparams("parallel"),
        name="outproj_ln",
    )(ya, h2, mem_k3, mem_v3, wa, wm, x2, g, b)


def _router_kernel(x_ref, rwh_ref, rwl_ref, rb_ref, lpos_ref, w_ref, cnt_ref, tcnt_ref, toff_ref, tbef_ref, *, tm):
    @pl.when(pl.program_id(0) == 0)
    def _():
        cnt_ref[...] = jnp.zeros_like(cnt_ref)

    x = _read_rows(x_ref, tm)
    xh = x.astype(BF16)
    xl = (x - xh.astype(F32)).astype(BF16)
    logits_t = _dot(xh, rwh_ref[...]) + (_dot(xh, rwl_ref[...]) + _dot(xl, rwh_ref[...]))
    logits = logits_t.T[:N_EXPERTS]
    scores = jax.nn.sigmoid(logits)
    biased = scores + rb_ref[...]
    bv = [biased[e:e + 1, :] for e in range(N_EXPERTS)]
    sv = [scores[e:e + 1, :] for e in range(N_EXPERTS)]

    grp = []
    for g in range(N_GROUPS):
        m = bv[g * EXPERTS_PER_GROUP:(g + 1) * EXPERTS_PER_GROUP]
        best = None
        for a in range(EXPERTS_PER_GROUP):
            for c in range(a + 1, EXPERTS_PER_GROUP):
                pair = m[a] + m[c]
                best = pair if best is None else jnp.maximum(best, pair)
        grp.append(best)
    gsel = jnp.zeros((1, tm), I32)
    gbest = grp[0]
    for g in range(1, N_GROUPS):
        better = grp[g] > gbest
        gsel = jnp.where(better, g, gsel)
        gbest = jnp.where(better, grp[g], gbest)

    def pick(vals, j):
        out = vals[j]
        for g in range(1, N_GROUPS):
            out = jnp.where(gsel == g, vals[g * EXPERTS_PER_GROUP + j], out)
        return out

    cb = [pick(bv, j) for j in range(EXPERTS_PER_GROUP)]
    cs = [pick(sv, j) for j in range(EXPERTS_PER_GROUP)]
    i1 = jnp.zeros((1, tm), I32)
    m1 = cb[0]
    s1 = cs[0]
    for j in range(1, EXPERTS_PER_GROUP):
        gt = cb[j] > m1
        i1 = jnp.where(gt, j, i1)
        m1 = jnp.where(gt, cb[j], m1)
        s1 = jnp.where(gt, cs[j], s1)
    i2 = jnp.zeros((1, tm), I32)
    m2 = jnp.full((1, tm), -jnp.inf, F32)
    s2 = jnp.zeros((1, tm), F32)
    for j in range(EXPERTS_PER_GROUP):
        ok = jnp.logical_and(i1 != j, cb[j] > m2)
        i2 = jnp.where(ok, j, i2)
        m2 = jnp.where(ok, cb[j], m2)
        s2 = jnp.where(ok, cs[j], s2)
    e1 = gsel * EXPERTS_PER_GROUP + i1
    e2 = gsel * EXPERTS_PER_GROUP + i2
    tot = s1 + s2
    w_ref[...] = jnp.concatenate([s1 / tot, s2 / tot], axis=0)

    i = pl.program_id(0)
    eio = lax.broadcasted_iota(I32, (N_EXPERTS, tm), 0)
    oh1 = eio == e1
    oh2 = eio == e2
    ohs = jnp.logical_or(oh1, oh2).astype(F32)
    before = (lax.broadcasted_iota(I32, (tm, tm), 0) < lax.broadcasted_iota(I32, (tm, tm), 1))
    pre = _dot(ohs.astype(BF16), before.astype(BF16))
    tile_cnt = jnp.sum(ohs, axis=1, keepdims=True)
    offs = []
    acc = jnp.zeros((1, 1), F32)
    for e in range(N_EXPERTS):
        offs.append(acc)
        acc = acc + tile_cnt[e:e + 1, :]
    tile_off = jnp.concatenate(offs, axis=0)
    pos = tile_off + pre
    p1 = jnp.sum(jnp.where(oh1, pos, 0.0), axis=0, keepdims=True)
    p2 = jnp.sum(jnp.where(oh2, pos, 0.0), axis=0, keepdims=True)
    lpos_ref[...] = jnp.concatenate([p1, p2], axis=0).astype(I32) * ROW_CHUNKS

    @pl.when(i == 0)
    def _():
        for ref in (tcnt_ref, toff_ref, tbef_ref):
            ref[...] = jnp.zeros_like(ref)

    here = lax.broadcasted_iota(I32, (1, LANES), 1) == i
    tcnt_ref[...] = jnp.where(here, tile_cnt, tcnt_ref[...])
    toff_ref[...] = jnp.where(here, tile_off, toff_ref[...])
    tbef_ref[...] = jnp.where(here, cnt_ref[:, 0:1], tbef_ref[...])
    cnt_ref[...] += tile_cnt


def _router(xr, rw, rb, tm):
    n = xr.shape[0] // ROW_CHUNKS
    assert n // tm <= LANES
    table = pl.BlockSpec((N_EXPERTS, LANES), lambda i: (0, 0))
    return pl.pallas_call(
        functools.partial(_router_kernel, tm=tm),
        grid=(n // tm,),
        in_specs=[pl.BlockSpec((tm * ROW_CHUNKS, LANES), lambda i: (i, 0)),
                  pl.BlockSpec((D_MODEL, LANES), lambda i: (0, 0)),
                  pl.BlockSpec((D_MODEL, LANES), lambda i: (0, 0)),
                  pl.BlockSpec((N_EXPERTS, 1), lambda i: (0, 0))],
        out_specs=[pl.BlockSpec((2, tm), lambda i: (0, i)),
                   pl.BlockSpec((2, tm), lambda i: (0, i)),
                   table, table, table, table],
        out_shape=[jax.ShapeDtypeStruct((2, n), I32),
                   jax.ShapeDtypeStruct((2, n), F32)]
                  + [jax.ShapeDtypeStruct((N_EXPERTS, LANES), F32)] * 4,
        compiler_params=_cparams("arbitrary"),
        name="router",
    )(xr, rw[0], rw[1], rb)


def _plan_kernel(cnt_ref, tbef_ref, meta_ref, rstart_ref, *, nbl):
    shift = MOE_BM.bit_length() - 1
    cnt = cnt_ref[...].astype(I32)
    padded = ((cnt + (MOE_BM - 1)) >> shift) << shift
    starts = []
    acc = jnp.zeros((1, LANES), I32)
    for e in range(N_EXPERTS):
        starts.append(acc)
        acc = acc + padded[e:e + 1, :]
    pad_start = jnp.concatenate(starts, axis=0)
    pad_end = pad_start + padded
    rstart_ref[...] = pad_start + tbef_ref[...].astype(I32)
    blk0 = lax.broadcasted_iota(I32, (N_EXPERTS, nbl), 1) * MOE_BM
    block_e = jnp.sum((pad_end[:, 0:1] <= blk0).astype(I32), axis=0, keepdims=True)
    block_e = jnp.minimum(block_e, N_EXPERTS - 1)
    n_used = jnp.broadcast_to(acc[:, 0:1] >> shift, (1, nbl))
    diag = lax.broadcasted_iota(I32, (N_EXPERTS, nbl), 0) == lax.broadcasted_iota(I32, (N_EXPERTS, nbl), 1)
    fill_lo = jnp.sum(jnp.where(diag, (pad_start + cnt)[:, 0:1], 0), axis=0, keepdims=True)
    fill_hi = jnp.sum(jnp.where(diag, pad_end[:, 0:1], 0), axis=0, keepdims=True)
    meta_ref[...] = jnp.concatenate([block_e, n_used, fill_lo, fill_hi, jnp.zeros((SUBLANES - 4, nbl), I32)],
                                    axis=0)


def _plan(cnt, tbef, n_blocks):
    nbl = -(-n_blocks // LANES) * LANES
    table = pl.BlockSpec((N_EXPERTS, LANES), lambda i: (0, 0))
    return pl.pallas_call(
        functools.partial(_plan_kernel, nbl=nbl),
        grid=(1,),
        in_specs=[table, table],
        out_specs=[pl.BlockSpec((SUBLANES, nbl), lambda i: (0, 0)), table],
        out_shape=[jax.ShapeDtypeStruct((SUBLANES, nbl), I32),
                   jax.ShapeDtypeStruct((N_EXPERTS, LANES), I32)],
        compiler_params=_cparams("arbitrary"),
        name="moe_plan",
    )(cnt, tbef)


def _rows(ref, row, nrows):
    return ref.at[pl.ds(pl.multiple_of(row * ROW_CHUNKS, ROW_CHUNKS), nrows * ROW_CHUNKS), :]


def _rows_wait(src_hbm, buf, sem):
    pltpu.make_async_copy(src_hbm.at[pl.ds(0, buf.shape[0]), :], buf, sem).wait()


def _copy_pieces(src, src_row, dst, dst_row, count, max_rows, sem, wait=False):
    bit = max_rows.bit_length() - 1
    while bit >= 0:
        size = 1 << bit
        done = (count >> (bit + 1)) << (bit + 1)

        @pl.when(((count >> bit) & 1) == 1)
        def _():
            cp = pltpu.make_async_copy(_rows(src, src_row + done, size), _rows(dst, dst_row + done, size), sem)
            cp.start()
            if wait:
                cp.wait()

        bit -= 1


def _tile_runs(tcnt_ref, toff_ref, rstart_ref, tile, buf, hbm, sem, *, to_hbm, tm):
    def per_expert(e, carry):
        k = tile * N_EXPERTS + e
        if to_hbm:
            _copy_pieces(buf, toff_ref[k], hbm, rstart_ref[k], tcnt_ref[k], tm, sem)
        else:
            _copy_pieces(hbm, rstart_ref[k], buf, toff_ref[k], tcnt_ref[k], tm, sem)
        return carry

    lax.fori_loop(0, N_EXPERTS, per_expert, 0)


def _dispatch_kernel(lpos_ref, tcnt_ref, toff_ref, rstart_ref, flo_ref, fhi_ref, nu_ref, x_ref, old_hbm, xs_hbm,
                     s0, s1, zbuf, sem, zsem, *, n, tm, n_blocks, reuse):
    del old_hbm
    i = pl.program_id(0)
    nt = pl.num_programs(0)
    bufs = (s0, s1)
    unroll = PLACE_UNROLL

    for slot in range(2):
        @pl.when(i % 2 == slot)
        def _():
            buf = bufs[slot]

            @pl.when(i >= 2)
            def _():
                _rows_wait(xs_hbm, buf, sem.at[slot])

            def place(c, carry):
                tok = i * tm + c * unroll
                src = pl.multiple_of(c * (unroll * ROW_CHUNKS), unroll * ROW_CHUNKS)
                for u in range(unroll):
                    v = x_ref[pl.ds(src + u * ROW_CHUNKS, ROW_CHUNKS), :]
                    for k in range(2):
                        p = lpos_ref[k * n + tok + u]
                        buf[pl.ds(pl.multiple_of(p, ROW_CHUNKS), ROW_CHUNKS), :] = v
                return carry

            lax.fori_loop(0, tm // unroll, place, 0)
            _tile_runs(tcnt_ref, toff_ref, rstart_ref, i, buf, xs_hbm, sem.at[slot], to_hbm=True, tm=tm)

    zero_rows = n_blocks * MOE_BM - 2 * n
    if not reuse:
        @pl.when(i == 0)
        def _():
            zbuf[...] = jnp.zeros_like(zbuf)
            for e in range(N_EXPERTS):
                _copy_pieces(zbuf, 0, xs_hbm, flo_ref[e], fhi_ref[e] - flo_ref[e], MOE_BM // 2, zsem)

            def zero_block(j, carry):
                pltpu.make_async_copy(zbuf, _rows(xs_hbm, j * MOE_BM, MOE_BM), zsem).start()
                return carry

            lax.fori_loop(nu_ref[0], n_blocks, zero_block, 0)

    @pl.when(i == nt - 1)
    def _():
        for slot in range(2):
            @pl.when(nt > slot)
            def _():
                _rows_wait(xs_hbm, bufs[slot], sem.at[slot])

        if not reuse:
            pltpu.make_async_copy(_rows(xs_hbm, 0, zero_rows), _rows(xs_hbm, 0, zero_rows), zsem).wait()


def _dispatch(lpos_flat, tcnt, toff, rstart, fill_lo, fill_hi, n_used, xr, n_blocks, tm, old=None):
    n = xr.shape[0] // ROW_CHUNKS
    reuse = old is not None
    if not reuse:
        old = jnp.zeros((SUBLANES, LANES), F32)
    return pl.pallas_call(
        functools.partial(_dispatch_kernel, n=n, tm=tm, n_blocks=n_blocks, reuse=reuse),
        grid_spec=pltpu.PrefetchScalarGridSpec(
            num_scalar_prefetch=7,
            grid=(n // tm,),
            in_specs=[pl.BlockSpec((tm * ROW_CHUNKS, LANES), lambda i, *_: (i, 0)),
                      pl.BlockSpec(memory_space=pl.ANY)],
            out_specs=pl.BlockSpec(memory_space=pl.ANY),
            scratch_shapes=[pltpu.VMEM((2 * tm * ROW_CHUNKS, LANES), F32),
                            pltpu.VMEM((2 * tm * ROW_CHUNKS, LANES), F32),
                            pltpu.VMEM((MOE_BM * ROW_CHUNKS, LANES), F32),
                            pltpu.SemaphoreType.DMA((2,)),
                            pltpu.SemaphoreType.DMA(())]),
        out_shape=jax.ShapeDtypeStruct((n_blocks * MOE_BM * ROW_CHUNKS, LANES), F32),
        input_output_aliases={8: 0} if reuse else {},
        compiler_params=_cparams("arbitrary"),
        name="moe_dispatch",
    )(lpos_flat, tcnt, toff, rstart, fill_lo, fill_hi, n_used, xr, old)


EXPERT_BUFS = 3


def _experts_kernel(be_ref, nu_ref, xs_hbm, wg_ref, wu_ref, wd_ref, y_hbm, wgb, wub, wdb, xin, yout,
                    sem_in, sem_out):
    j = pl.program_id(0)
    nb = pl.num_programs(0)
    n_used = nu_ref[0]
    used = j < n_used
    ahead = EXPERT_BUFS - 1

    def fetch(blk, slot):
        return pltpu.make_async_copy(_rows(xs_hbm, blk * MOE_BM, MOE_BM), xin.at[slot], sem_in.at[slot])

    def write_back(blk, slot):
        return pltpu.make_async_copy(yout.at[slot], _rows(y_hbm, blk * MOE_BM, MOE_BM), sem_out.at[slot])

    @pl.when(j == 0)
    def _():
        for d in range(ahead):
            @pl.when(d < n_used)
            def _():
                fetch(d, d).start()

    @pl.when(jnp.logical_and(used, jnp.logical_or(j == 0, be_ref[j] != be_ref[jnp.maximum(j - 1, 0)])))
    def _():
        wgb[...] = wg_ref[0, 0].astype(BF16)
        wub[...] = wu_ref[0, 0].astype(BF16)
        wdb[...] = wd_ref[0, 0].astype(BF16)

    for slot in range(EXPERT_BUFS):
        @pl.when(jnp.logical_and(used, j % EXPERT_BUFS == slot))
        def _():
            @pl.when(j >= EXPERT_BUFS)
            def _():
                write_back(j - EXPERT_BUFS, slot).wait()

            @pl.when(j + ahead < n_used)
            def _():
                fetch(j + ahead, (slot + ahead) % EXPERT_BUFS).start()

            fetch(j, slot).wait()
            x = _read_rows(xin.at[slot], MOE_BM).astype(BF16)
            h = _silu(_dot(x, wgb[...])) * _dot(x, wub[...])
            _write_rows(yout.at[slot], _dot(h.astype(BF16), wdb[...]), MOE_BM)
            write_back(j, slot).start()

    @pl.when(j == nb - 1)
    def _():
        for back in range(1, EXPERT_BUFS + 1):
            @pl.when(n_used - back >= 0)
            def _():
                blk = n_used - back
                pltpu.make_async_copy(yout.at[0], _rows(y_hbm, blk * MOE_BM, MOE_BM),
                                      sem_out.at[blk % EXPERT_BUFS]).wait()


def _experts(block_e, n_used, xs, wg, wu, wd, layer):
    n_blocks = block_e.shape[0]

    def last_used(j, nu):
        return jnp.minimum(j, nu[0] - 1)

    def wblk(j, be, nu):
        return (layer, be[last_used(j, nu)], 0, 0)

    return pl.pallas_call(
        _experts_kernel,
        grid_spec=pltpu.PrefetchScalarGridSpec(
            num_scalar_prefetch=2,
            grid=(n_blocks,),
            in_specs=[pl.BlockSpec(memory_space=pl.ANY),
                      pl.BlockSpec((1, 1, D_MODEL, D_EXPERT), wblk),
                      pl.BlockSpec((1, 1, D_MODEL, D_EXPERT), wblk),
                      pl.BlockSpec((1, 1, D_EXPERT, D_MODEL), wblk)],
            out_specs=pl.BlockSpec(memory_space=pl.ANY),
            scratch_shapes=[pltpu.VMEM((D_MODEL, D_EXPERT), BF16), pltpu.VMEM((D_MODEL, D_EXPERT), BF16),
                            pltpu.VMEM((D_EXPERT, D_MODEL), BF16),
                            pltpu.VMEM((EXPERT_BUFS, MOE_BM * ROW_CHUNKS, LANES), F32),
                            pltpu.VMEM((EXPERT_BUFS, MOE_BM * ROW_CHUNKS, LANES), F32),
                            pltpu.SemaphoreType.DMA((EXPERT_BUFS,)), pltpu.SemaphoreType.DMA((EXPERT_BUFS,))]),
        out_shape=jax.ShapeDtypeStruct(xs.shape, F32),
        input_output_aliases={2: 0},
        compiler_params=_cparams("arbitrary"),
        name="moe_experts",
    )(block_e, n_used, xs, wg, wu, wd)


def _combine_ln_kernel(lpos_ref, tcnt_ref, toff_ref, rstart_ref, y_hbm, x_ref, w1_ref, w2_ref, g_ref, b_ref, o_ref,
                       r0, r1, u1, u2, sem, *, n, tm):
    i = pl.program_id(0)
    nt = pl.num_programs(0)
    bufs = (r0, r1)
    unroll = PLACE_UNROLL

    def fetch(tile, slot):
        _tile_runs(tcnt_ref, toff_ref, rstart_ref, tile, bufs[slot], y_hbm, sem.at[slot], to_hbm=False, tm=tm)

    @pl.when(i == 0)
    def _():
        fetch(0, 0)

    for slot in range(2):
        @pl.when(i % 2 == slot)
        def _():
            @pl.when(i + 1 < nt)
            def _():
                fetch(i + 1, 1 - slot)

            buf = bufs[slot]
            _rows_wait(y_hbm, buf, sem.at[slot])

            def place(c, carry):
                tok = i * tm + c * unroll
                dst0 = pl.multiple_of(c * (unroll * ROW_CHUNKS), unroll * ROW_CHUNKS)
                for u in range(unroll):
                    dst = pl.ds(dst0 + u * ROW_CHUNKS, ROW_CHUNKS)
                    for k, out in enumerate((u1, u2)):
                        p = lpos_ref[k * n + tok + u]
                        out[dst, :] = buf[pl.ds(pl.multiple_of(p, ROW_CHUNKS), ROW_CHUNKS), :]
                return carry

            lax.fori_loop(0, tm // unroll, place, 0)
            moe = w1_ref[...] * _read_rows(u1, tm) + w2_ref[...] * _read_rows(u2, tm)
            o_ref[...] = _ln(ALPHA * _read_rows(x_ref, tm) + moe, g_ref[...], b_ref[...])


def _combine_ln(lpos_flat, tcnt, toff, rstart, y, xr, w1, w2, g, b, tm):
    n = xr.shape[0] // ROW_CHUNKS
    return pl.pallas_call(
        functools.partial(_combine_ln_kernel, n=n, tm=tm),
        grid_spec=pltpu.PrefetchScalarGridSpec(
            num_scalar_prefetch=4,
            grid=(n // tm,),
            in_specs=[pl.BlockSpec(memory_space=pl.ANY),
                      pl.BlockSpec((tm * ROW_CHUNKS, LANES), lambda i, *_: (i, 0)),
                      pl.BlockSpec((tm, 1), lambda i, *_: (i, 0)),
                      pl.BlockSpec((tm, 1), lambda i, *_: (i, 0)),
                      pl.BlockSpec((1, D_MODEL), lambda i, *_: (0, 0)),
                      pl.BlockSpec((1, D_MODEL), lambda i, *_: (0, 0))],
            out_specs=pl.BlockSpec((tm, D_MODEL), lambda i, *_: (i, 0)),
            scratch_shapes=[pltpu.VMEM((2 * tm * ROW_CHUNKS, LANES), F32)] * 2
                           + [pltpu.VMEM((tm * ROW_CHUNKS, LANES), F32)] * 2
                           + [pltpu.SemaphoreType.DMA((2,))]),
        out_shape=jax.ShapeDtypeStruct((n, D_MODEL), F32),
        compiler_params=_cparams("arbitrary"),
        name="moe_combine_ln",
    )(lpos_flat, tcnt, toff, rstart, y, xr, w1, w2, g, b)


def _moe_ln(xr, rw, rb, wg, wu, wd, layer, g, b, old=None):
    n = xr.shape[0] // ROW_CHUNKS
    n_blocks = (2 * n) // MOE_BM + N_EXPERTS
    tm = MOE_TILE
    nt = n // tm
    lpos, w, cnt, tcnt, toff, tbef = _router(xr, rw, rb, tm)
    meta, rstart = _plan(cnt, tbef, n_blocks)
    block_e = meta[0, :n_blocks]
    n_used = meta[1, :1]

    def per_tile(table):
        return table[:, :nt].T.reshape(nt * N_EXPERTS).astype(I32)

    lpos_flat = lpos.reshape(2 * n)
    tcnt, toff, rstart = per_tile(tcnt), per_tile(toff), per_tile(rstart)
    xs = _dispatch(lpos_flat, tcnt, toff, rstart, meta[2, :N_EXPERTS], meta[3, :N_EXPERTS], n_used, xr,
                   n_blocks, tm, old)
    y = _experts(block_e, n_used, xs, wg, wu, wd, layer)
    out = _combine_ln(lpos_flat, tcnt, toff, rstart, y, xr, w[0].reshape(n, 1), w[1].reshape(n, 1), g, b, tm)
    return out, y


def _conv_qkv_kernel(xm_ref, cw_ref, cb_ref, wq_ref, wk_ref, wv_ref, q_ref, k_ref, v_ref, xc_ref, *, s):
    xm_b = xm_ref[0]
    xm = xm_b.astype(F32)
    cw = cw_ref[...]
    row = lax.broadcasted_iota(I32, (s, 1), 0)
    half = CONV_K // 2
    acc = cb_ref[...] + xm * cw[half:half + 1, :]
    for sh in range(1, half + 1):
        past = jnp.where(row >= sh, pltpu.roll(xm, sh, axis=0), 0.0)
        acc = acc + past * cw[half - sh:half - sh + 1, :]
        nxt = jnp.where(row < s - sh, pltpu.roll(xm, s - sh, axis=0), 0.0)
        acc = acc + nxt * cw[half + sh:half + sh + 1, :]
    xc = _silu(acc).astype(BF16)
    xc_ref[0] = xc
    q_ref[0] = _dot(xc, wq_ref[0]).astype(BF16)
    k_ref[0] = (_dot_nt(wk_ref[0], xc) * (ML_DH ** -0.5)).astype(BF16)
    v = _dot(xm_b, wv_ref[0])
    ones_lane = lax.broadcasted_iota(I32, (1, ML_DHP), 1) == ML_DH
    v_ref[0] = jnp.where(ones_lane, 1.0, v).astype(BF16)


def _conv_qkv(main3, cw, cb, wq, wk_t, wv):
    b, s, _ = main3.shape
    tok = pl.BlockSpec((1, s, ML_DHP), lambda i, h: (i, 0, h))
    wspec = pl.BlockSpec((1, ML_DHP, ML_DHP), lambda i, h: (h, 0, 0))
    tok_shape = jax.ShapeDtypeStruct((b, s, ML_WP), BF16)
    return pl.pallas_call(
        functools.partial(_conv_qkv_kernel, s=s),
        grid=(b, ML_HEADS),
        in_specs=[tok,
                  pl.BlockSpec((CONV_K, ML_DHP), lambda i, h: (0, h)),
                  pl.BlockSpec((1, ML_DHP), lambda i, h: (0, h)),
                  wspec, wspec, wspec],
        out_specs=[tok, pl.BlockSpec((1, ML_DHP, s), lambda i, h: (i, h, 0)), tok, tok],
        out_shape=[tok_shape, jax.ShapeDtypeStruct((b, ML_WP, s), BF16), tok_shape, tok_shape],
        compiler_params=_cparams("parallel", "parallel"),
        name="conv_qkv",
    )(main3, cw, cb, wq, wk_t, wv)


def _mlstm_kernel(q_ref, kt_ref, v_ref, gc_ref, gr_ref, z_ref, xc_ref, ng_ref, sk_ref,
                  y_ref, hf_ref, hb_ref, cf_ref, cb_ref, m_ref, *, s):
    head0 = pl.program_id(1) * ML_HPS
    nc = s // CHUNK
    sub = lax.broadcasted_iota(I32, (LANES, 1), 0)
    gate = lax.broadcasted_iota(I32, (LANES, LANES), 0)
    ti = lax.broadcasted_iota(I32, (CHUNK, CHUNK), 0)
    tj = lax.broadcasted_iota(I32, (CHUNK, CHUNK), 1)

    for ref in (cf_ref, cb_ref, m_ref):
        ref[...] = jnp.zeros_like(ref)

    def intra(c, j, rev):
        t0 = pl.multiple_of(c * CHUNK, CHUNK)
        hl = slice(j * ML_DHP, (j + 1) * ML_DHP)
        qb = q_ref[0, pl.ds(t0, CHUNK), hl]
        kt = kt_ref[0, hl, pl.ds(t0, CHUNK)]
        vb = v_ref[0, pl.ds(t0, CHUNK), hl]
        gc = gc_ref[0, pl.ds(t0, CHUNK), :]
        gr = gr_ref[:, pl.ds(t0, CHUNK)]
        i_idx = head0 + j + (2 * ML_HEADS if rev else 0)
        f_idx = i_idx + ML_HEADS
        allowed = (tj >= ti) if rev else (tj <= ti)
        sel = (gate == f_idx).astype(BF16)
        b_rep = sum(_dot(part, sel) for part in _split3(gc))
        b_row = jnp.sum(jnp.where(sub == f_idx, gr, 0.0), axis=0, keepdims=True)
        i_row = jnp.sum(jnp.where(sub == i_idx, gr, 0.0), axis=0, keepdims=True)
        b_last = (b_rep[0:1, :] if rev else b_rep[CHUNK - 1:CHUNK, :])[:, 0:1]

        b_wide = jnp.concatenate([b_rep] * (CHUNK // LANES), axis=1)
        d = jnp.where(allowed, b_wide - b_row + i_row, NEG)
        m_in = jnp.max(d, axis=1, keepdims=True)
        sc = _dot(qb, kt) * jnp.exp(d - m_in)
        nd_in = _dot(sc.astype(BF16), vb)
        w_row = b_last - b_row + i_row
        return t0, qb, kt, vb, b_rep, b_last, m_in, nd_in, w_row

    def twice(a):
        return jnp.concatenate([a, a], axis=1)

    def update(parts, j, rev):
        t0, qb, kt, vb, b_rep, b_last, m_in, nd_in, w_row = parts
        h_ref, c_ref = (hb_ref, cb_ref) if rev else (hf_ref, cf_ref)
        hl = slice(j * ML_DHP, (j + 1) * ML_DHP)
        mrow = 2 * j + int(rev)
        m = m_ref[mrow:mrow + 1, 0:1]
        cmat = c_ref[j]
        inter = b_rep + m
        m_t = jnp.maximum(m_in, inter)
        a_in = jnp.exp(m_in - m_t)
        iexp = jnp.exp(inter - m_t)
        nd = twice(a_in) * nd_in + twice(iexp) * _dot(qb, cmat.astype(BF16))
        den = nd[:, ML_DH:ML_DH + 1]
        h_ref[pl.ds(t0, CHUNK), hl] = nd * (1.0 / jnp.maximum(jnp.abs(den), jnp.exp(-m_t[:, 0:1])))

        m_new = jnp.maximum(b_last + m, jnp.max(w_row, axis=1, keepdims=True))
        wexp = jnp.exp(w_row - m_new)
        cexp = jnp.exp(b_last + m - m_new)
        kw = (kt.astype(F32) * wexp).astype(BF16)
        c_ref[j] = cexp * cmat + _dot(kw, vb)
        m_ref[mrow:mrow + 1, :] = jnp.broadcast_to(m_new, (1, LANES))

    def step(i, carry):
        for j in range(ML_HPS):
            parts = [intra(nc - 1 - i if rev else i, j, rev) for rev in (False, True)]
            for p, rev in zip(parts, (False, True)):
                update(p, j, rev)
        return carry

    lax.fori_loop(0, nc, step, 0)

    real = lax.broadcasted_iota(I32, (1, ML_DHP), 1) < ML_DH
    tb = CHUNK

    def fin(c, carry):
        t0 = pl.multiple_of(c * tb, tb)
        for j in range(ML_HPS):
            hl = slice(j * ML_DHP, (j + 1) * ML_DHP)
            hs = jnp.where(real, hf_ref[pl.ds(t0, tb), hl] + hb_ref[pl.ds(t0, tb), hl], 0.0)
            mu = jnp.sum(hs, axis=1, keepdims=True) * (1.0 / ML_DH)
            dev = jnp.where(real, hs - mu, 0.0)
            var = jnp.sum(dev * dev, axis=1, keepdims=True) * (1.0 / ML_DH)
            hn = dev * lax.rsqrt(var + LN_EPS) * ng_ref[:, hl]
            xc = xc_ref[0, pl.ds(t0, tb), hl].astype(F32)
            z = z_ref[0, pl.ds(t0, tb), hl].astype(F32)
            y_ref[0, pl.ds(t0, tb), hl] = ((hn + sk_ref[:, hl] * xc) * _silu(z)).astype(BF16)
        return carry

    lax.fori_loop(0, s // tb, fin, 0)


def _mlstm(q, kt, v, gcol3, grow, main3, xc, ng, sk):
    b, s, _ = q.shape
    width = ML_HPS * ML_DHP
    steps = ML_HEADS // ML_HPS
    tok = pl.BlockSpec((1, s, width), lambda i, h: (i, 0, h))
    vec = pl.BlockSpec((1, width), lambda i, h: (0, h))
    return pl.pallas_call(
        functools.partial(_mlstm_kernel, s=s),
        grid=(b, steps),
        in_specs=[tok, pl.BlockSpec((1, width, s), lambda i, h: (i, h, 0)), tok,
                  pl.BlockSpec((1, s, LANES), lambda i, h: (i, 0, 0)),
                  pl.BlockSpec((LANES, s), lambda i, h: (0, i)),
                  pl.BlockSpec((1, s, width), lambda i, h: (i, 0, steps + h)),
                  tok, vec, vec],
        out_specs=tok,
        out_shape=jax.ShapeDtypeStruct((b, s, ML_WP), BF16),
        scratch_shapes=[pltpu.VMEM((s, width), F32), pltpu.VMEM((s, width), F32),
                        pltpu.VMEM((ML_HPS, ML_DHP, ML_DHP), F32), pltpu.VMEM((ML_HPS, ML_DHP, ML_DHP), F32),
                        pltpu.VMEM((SUBLANES, LANES), F32)],
        compiler_params=_cparams("parallel", "parallel"),
        name="mlstm",
    )(q, kt, v, gcol3, grow, main3, xc, ng, sk)


def _pad_heads(a, axis):
    a = jnp.moveaxis(a, axis, -1)
    lead = a.shape[:-1]
    a = a.reshape(lead + (ML_HEADS, ML_DH))
    a = jnp.pad(a, [(0, 0)] * len(lead) + [(0, 0), (0, ML_DHP - ML_DH)])
    return jnp.moveaxis(a.reshape(lead + (ML_WP,)), -1, axis)


def kernel(x, mem, mem_ln_g, mem_ln_b, w_mem_kv, router_w, router_b, na_w_in, na_rpb, ml_w_in, ml_conv_w,
           ml_conv_b, ml_w_qkv, ml_gate_b, ml_norm_g, ml_skip, w_out, ln_g, ln_b, exp_w_gate, exp_w_up,
           exp_w_down):
    b, s, d = x.shape
    n = b * s
    nm = mem.shape[1]
    row = lambda a: a.reshape(1, -1)

    mem_k, mem_v = _memkv(mem.reshape(b * nm, d), row(mem_ln_g), row(mem_ln_b), w_mem_kv.astype(BF16))
    mem_k3 = mem_k.reshape(b, nm, MEM_W)
    mem_v3 = mem_v.reshape(b, nm, MEM_W)
    rw_pad = jnp.pad(router_w, ((0, 0), (0, LANES - N_EXPERTS)))
    rw_hi = rw_pad.astype(BF16)
    rw = (rw_hi, (rw_pad - rw_hi.astype(F32)).astype(BF16))
    rb = router_b.reshape(N_EXPERTS, 1)

    x2 = x.reshape(n, d)

    h0 = _proj(x2, na_w_in[0].astype(BF16)).reshape(b, s, 3 * NA_W + MEM_W)
    y_na = _na_attention(h0, _na_bias_table(na_rpb[0]))
    wo = w_out[0].astype(BF16)
    xr = _outproj_ln(y_na.reshape(n, NA_W), h0.reshape(n, 3 * NA_W + MEM_W), 3 * NA_W // MEM_W, mem_k3, mem_v3,
                     wo[:NA_W], wo[NA_W:], x2, row(ln_g[0, 0]), row(ln_b[0, 0]))
    x2, moe_buf = _moe_ln(xr, rw, rb, exp_w_gate, exp_w_up, exp_w_down, 0, row(ln_g[0, 1]), row(ln_b[0, 1]))

    w1 = ml_w_in[0]
    w_main = jnp.concatenate([_pad_heads(w1[:, :ML_W], 1), _pad_heads(w1[:, ML_W:2 * ML_W], 1),
                              w1[:, 2 * ML_W + 4 * ML_HEADS:]], axis=1).astype(BF16)
    w_g = jnp.pad(w1[:, 2 * ML_W:2 * ML_W + 4 * ML_HEADS], ((0, 0), (0, LANES - 4 * ML_HEADS))).astype(BF16)
    gb = jnp.pad(ml_gate_b[0].reshape(4 * ML_HEADS), (0, LANES - 4 * ML_HEADS))
    main, acol, arow = _proj_gates(x2, w_main, w_g, w_g.T, gb.reshape(1, LANES), gb.reshape(LANES, 1))
    main3 = main.reshape(b, s, 2 * ML_WP + MEM_W)
    wqkv = jnp.pad(ml_w_qkv[0], ((0, 0), (0, 0), (0, ML_DHP - ML_DH), (0, ML_DHP - ML_DH))).astype(BF16)
    q, k, v, xc = _conv_qkv(main3, _pad_heads(ml_conv_w[0], 1), _pad_heads(row(ml_conv_b[0]), 1),
                            wqkv[0], jnp.swapaxes(wqkv[1], 1, 2), wqkv[2])
    y_ml = _mlstm(q, k, v, acol.reshape(b, s, LANES), arow, main3, xc,
                  _pad_heads(row(ml_norm_g[0]), 1), _pad_heads(row(ml_skip[0]), 1))
    wo = w_out[1]
    xr = _outproj_ln(y_ml.reshape(n, ML_WP), main, 2 * ML_WP // MEM_W, mem_k3, mem_v3,
                     _pad_heads(wo[:ML_W], 0).astype(BF16), wo[ML_W:].astype(BF16), x2,
                     row(ln_g[1, 0]), row(ln_b[1, 0]))
    x2, _ = _moe_ln(xr, rw, rb, exp_w_gate, exp_w_up, exp_w_down, 1, row(ln_g[1, 1]), row(ln_b[1, 1]), moe_buf)
    return x2.reshape(b, s, d)
```
